```python
import math
import jax
import jax.numpy as jnp
from jax import lax
import numpy as np

D_MODEL = 1024
BATCH = 4
SEQ = 4096
DEPTH = 2

RMS_EPS = 1e-6
L2_EPS = 1e-6
CHUNK = 64
CONV_K = 4

GDN_HEADS = 4
GDN_DK = 128
GDN_DV = 128
HGRN_HEADS = 4
HGRN_DK = 128
HGRN_DV = 128
RET_HEADS = 4
RET_DK = D_MODEL // RET_HEADS
RET_DV = 2 * RET_DK
ROPE_BASE = 10000.0
N_GROUPS = 4
EXPERTS_PER_GROUP = 4
N_EXPERTS = N_GROUPS * EXPERTS_PER_GROUP
D_EXPERT = 256
TOP_K_INNER = 2

N_EVEN = (DEPTH + 1) // 2
N_ODD = DEPTH // 2

GDN_QK_W = GDN_HEADS * GDN_DK
GDN_V_W = GDN_HEADS * GDN_DV
GDN_CONV_W = 2 * GDN_QK_W + GDN_V_W
HGRN_K_W = HGRN_HEADS * HGRN_DK
HGRN_V_W = HGRN_HEADS * HGRN_DV
EVEN_SPLITS = (GDN_CONV_W, GDN_V_W, GDN_HEADS, GDN_HEADS, HGRN_K_W, HGRN_K_W, HGRN_V_W, HGRN_V_W)
EVEN_IN = sum(EVEN_SPLITS)
EVEN_MIX_W = GDN_V_W + HGRN_V_W
RET_QK_W = RET_HEADS * RET_DK
RET_V_W = RET_HEADS * RET_DV
ODD_SPLITS = (RET_QK_W, RET_QK_W, RET_V_W, RET_V_W)
ODD_IN = sum(ODD_SPLITS)
ODD_MIX_W = RET_V_W

kernel_name = 'hybrid_gdn_hgrn2_retnet_hmoe'


def rms_norm(x, w):
    xf = x.astype(jnp.float32)
    xf = xf * lax.rsqrt(jnp.mean(xf * xf, axis=-1, keepdims=True) + RMS_EPS)
    return (xf * w.astype(jnp.float32)).astype(x.dtype)


def l2_normalize(x):
    xf = x.astype(jnp.float32)
    return (xf * lax.rsqrt(jnp.sum(xf * xf, axis=-1, keepdims=True) + L2_EPS)).astype(x.dtype)


def split_heads(x, n_heads):
    b, t, c = x.shape
    return x.reshape(b, t, n_heads, c // n_heads).transpose(0, 2, 1, 3)


def merge_heads(x):
    b, h, t, d = x.shape
    return x.transpose(0, 2, 1, 3).reshape(b, t, h * d)


def split_cols(p, sizes):
    idx = [int(i) for i in np.cumsum(sizes)[:-1]]
    return jnp.split(p, idx, axis=-1)


def to_chunks(x):
    b, h, t = x.shape[:3]
    x = x.reshape((b, h, t // CHUNK, CHUNK) + x.shape[3:])
    return jnp.moveaxis(x, 2, 0)


def from_chunks(y):
    n, b, h, c, d = y.shape
    return jnp.moveaxis(y, 0, 2).reshape(b, h, n * c, d)


def causal_short_conv(x, w):
    t = x.shape[1]
    xp = jnp.pad(x, ((0, 0), (CONV_K - 1, 0), (0, 0)))
    return sum(xp[:, j:j + t] * w[j] for j in range(CONV_K))


def rotary(x):
    t, d = x.shape[2], x.shape[3]
    inv = 1.0 / (ROPE_BASE ** jnp.linspace(0.0, 1.0, d // 2, dtype=jnp.float32))
    ang = jnp.arange(t, dtype=jnp.float32)[:, None] * inv[None, :]
    cos = jnp.cos(ang).astype(x.dtype)
    sin = jnp.sin(ang).astype(x.dtype)
    x1, x2 = x[..., 0::2], x[..., 1::2]
    return jnp.stack([x1 * cos - x2 * sin, x1 * sin + x2 * cos], axis=-1).reshape(x.shape)


def gated_delta_rule(q, k, v, beta, g):
    f32 = jnp.float32
    dk, dv = q.shape[-1], v.shape[-1]
    qc = to_chunks(q.astype(f32) * dk ** -0.5)
    kc = to_chunks(k.astype(f32))
    vc = to_chunks(v.astype(f32))
    bc = to_chunks(beta.astype(f32))[..., None]
    gc = jnp.cumsum(to_chunks(g.astype(f32)), axis=-1)
    causal = jnp.tril(jnp.ones((CHUNK, CHUNK), bool))
    decay = jnp.exp(jnp.where(causal, gc[..., :, None] - gc[..., None, :], -jnp.inf))
    a = jnp.tril(jnp.einsum('nbhid,nbhjd->nbhij', kc * bc, kc) * decay, -1)
    rhs = jnp.concatenate([vc * bc, kc * bc * jnp.exp(gc)[..., None]], axis=-1)
    sol = lax.linalg.triangular_solve(a, rhs, left_side=True, lower=True, unit_diagonal=True)
    u, w = sol[..., :dv], sol[..., dv:]

    def step(state, inp):
        q_i, k_i, u_i, w_i, g_i, d_i = inp
        v_new = u_i - jnp.einsum('bhck,bhkv->bhcv', w_i, state)
        attn = jnp.einsum('bhik,bhjk->bhij', q_i, k_i) * d_i
        o = (jnp.einsum('bhck,bhkv->bhcv', q_i * jnp.exp(g_i)[..., None], state)
             + jnp.einsum('bhij,bhjv->bhiv', attn, v_new))
        g_last = g_i[..., -1:]
        k_dec = k_i * jnp.exp(g_last - g_i)[..., None]
        state = state * jnp.exp(g_last)[..., None] + jnp.einsum('bhck,bhcv->bhkv', k_dec, v_new)
        return state, o

    s0 = jnp.zeros(qc.shape[1:3] + (dk, dv), f32)
    _, o = lax.scan(step, s0, (qc, kc, u, w, gc, decay))
    return from_chunks(o).astype(v.dtype)


def hgrn2_recurrence(q, k, i, log_f):
    f32 = jnp.float32
    dk, dv = q.shape[-1], i.shape[-1]
    qc = to_chunks(q.astype(f32) * dk ** -0.5)
    kc = to_chunks(k.astype(f32))
    ic = to_chunks(i.astype(f32))
    bc = jnp.cumsum(to_chunks(log_f.astype(f32)), axis=-2)
    causal = jnp.tril(jnp.ones((CHUNK, CHUNK), bool))[:, :, None]

    def step(state, inp):
        q_i, k_i, i_i, b_i = inp
        dmat = jnp.exp(jnp.where(causal, b_i[:, :, :, None, :] - b_i[:, :, None, :, :], -jnp.inf))
        attn = jnp.einsum('bhtk,bhsk,bhtsk->bhts', q_i, k_i, dmat)
        o = (jnp.einsum('bhtk,bhkv->bhtv', q_i * jnp.exp(b_i), state)
             + jnp.einsum('bhts,bhsv->bhtv', attn, i_i))
        b_last = b_i[:, :, -1:, :]
        state = (state * jnp.exp(b_last[:, :, 0, :])[..., None]
                 + jnp.einsum('bhsk,bhsv->bhkv', k_i * jnp.exp(b_last - b_i), i_i))
        return state, o

    s0 = jnp.zeros(qc.shape[1:3] + (dk, dv), f32)
    _, o = lax.scan(step, s0, (qc, kc, ic, bc))
    return from_chunks(o).astype(i.dtype)


def multiscale_retention(q, k, v):
    f32 = jnp.float32
    n_heads, dk, dv = q.shape[1], q.shape[-1], v.shape[-1]
    log_gamma = jnp.log(1.0 - 2.0 ** (-5.0 - jnp.arange(n_heads, dtype=f32)))
    pos = jnp.arange(CHUNK, dtype=f32)
    rel = pos[:, None] - pos[None, :]
    intra = jnp.where(rel >= 0, jnp.exp(jnp.maximum(rel, 0.0) * log_gamma[:, None, None]), 0.0)
    q_decay = jnp.exp((pos + 1.0) * log_gamma[:, None])[:, :, None]
    k_decay = jnp.exp((CHUNK - 1.0 - pos) * log_gamma[:, None])[:, :, None]
    chunk_decay = jnp.exp(CHUNK * log_gamma)[:, None, None]
    qc = to_chunks(q.astype(f32))
    kc = to_chunks(k.astype(f32) * dk ** -0.5)
    vc = to_chunks(v.astype(f32))

    def step(state, inp):
        q_i, k_i, v_i = inp
        attn = jnp.einsum('bhik,bhjk->bhij', q_i, k_i) * intra
        o = (jnp.einsum('bhck,bhkv->bhcv', q_i, state) * q_decay
             + jnp.einsum('bhij,bhjv->bhiv', attn, v_i))
        state = state * chunk_decay + jnp.einsum('bhck,bhcv->bhkv', k_i * k_decay, v_i)
        return state, o

    s0 = jnp.zeros(qc.shape[1:3] + (dk, dv), f32)
    _, o = lax.scan(step, s0, (qc, kc, vc))
    return from_chunks(o).astype(v.dtype)


def even_mixer(h, w_in, conv_w, a_log, dt_bias, gdn_norm_w, lower_bound, hgrn_norm_w, w_out):
    f32 = jnp.float32
    g_qkv, g_z, g_b, g_a, h_q, h_f, h_i, h_g = split_cols(h @ w_in, EVEN_SPLITS)
    qkv = jax.nn.silu(causal_short_conv(g_qkv, conv_w))
    q, k, v = split_cols(qkv, (GDN_QK_W, GDN_QK_W, GDN_V_W))
    q = l2_normalize(split_heads(q, GDN_HEADS))
    k = l2_normalize(split_heads(k, GDN_HEADS))
    v = split_heads(v, GDN_HEADS)
    beta = jax.nn.sigmoid(g_b.astype(f32)).transpose(0, 2, 1)
    g = (-jnp.exp(a_log.astype(f32)) * jax.nn.softplus(g_a.astype(f32) + dt_bias.astype(f32))).transpose(0, 2, 1)
    o_a = gated_delta_rule(q, k, v, beta, g)
    o_a = rms_norm(o_a, gdn_norm_w) * jax.nn.silu(split_heads(g_z, GDN_HEADS))
    f_gate = lower_bound + (1.0 - lower_bound) * jax.nn.sigmoid(h_f.astype(f32))
    o_b = hgrn2_recurrence(split_heads(jax.nn.silu(h_q), HGRN_HEADS),
                           split_heads((1.0 - f_gate).astype(h.dtype), HGRN_HEADS),
                           split_heads(h_i, HGRN_HEADS),
                           split_heads(jnp.log(f_gate), HGRN_HEADS))
    o_b = rms_norm(o_b, hgrn_norm_w) * jax.nn.silu(split_heads(h_g, HGRN_HEADS))
    return jnp.concatenate([merge_heads(o_a), merge_heads(o_b)], axis=-1) @ w_out


def retention_mixer(h, w_in, norm_w, w_out):
    q, k, v, g = split_cols(h @ w_in, ODD_SPLITS)
    q = rotary(split_heads(q, RET_HEADS))
    k = rotary(split_heads(k, RET_HEADS))
    o = multiscale_retention(q, k, split_heads(v, RET_HEADS))
    o = rms_norm(o, norm_w) * jax.nn.silu(split_heads(g, RET_HEADS))
    return merge_heads(o) @ w_out


def hierarchical_moe(h, wc, bc, wf, bf, w_gate_up, w_down):
    b, t, d = h.shape
    xt = h.reshape(b * t, d)
    n = xt.shape[0]
    coarse_prob = jax.nn.softmax((xt @ wc + bc).astype(jnp.float32), axis=-1)
    g_prob, g_idx = lax.top_k(coarse_prob, 1)
    fine_logits = (xt @ wf + bf).astype(jnp.float32).reshape(n, N_GROUPS, EXPERTS_PER_GROUP)
    fine_sel = jnp.take_along_axis(fine_logits, g_idx[:, :, None], axis=1)[:, 0]
    e_logit, e_idx = lax.top_k(fine_sel, TOP_K_INNER)
    e_w = jax.nn.softmax(e_logit, axis=-1) * g_prob
    expert_id = g_idx * EXPERTS_PER_GROUP + e_idx
    gates = jnp.sum(jax.nn.one_hot(expert_id, N_EXPERTS, dtype=jnp.float32) * e_w[..., None], axis=1)
    gu = jnp.einsum('nd,edf->nef', xt, w_gate_up)
    gate, up = jnp.split(gu, 2, axis=-1)
    act = jax.nn.silu(gate) * up * gates.astype(h.dtype)[..., None]
    return jnp.einsum('nef,efd->nd', act, w_down).reshape(b, t, d)


def setup_inputs(seed: int = 0) -> dict:
    key = jax.random.key(seed)
    ks = jax.random.split(key, 24)
    f32 = jnp.float32

    def normal(k, shape, scale):
        return scale * jax.random.normal(k, shape, f32)

    def gain(k, shape):
        return 1.0 + 0.02 * jax.random.normal(k, shape, f32)

    dt = jnp.exp(jax.random.uniform(ks[6], (N_EVEN, GDN_HEADS), f32, math.log(1e-3), math.log(1e-1)))
    return {
        'x': normal(ks[0], (BATCH, SEQ, D_MODEL), 1.0),
        'norm_mix_w': gain(ks[1], (DEPTH, D_MODEL)),
        'norm_ffn_w': gain(ks[2], (DEPTH, D_MODEL)),
        'even_w_in': normal(ks[3], (N_EVEN, D_MODEL, EVEN_IN), D_MODEL ** -0.5),
        'gdn_conv_w': normal(ks[4], (N_EVEN, CONV_K, GDN_CONV_W), CONV_K ** -0.5),
        'gdn_a_log': jnp.log(jax.random.uniform(ks[5], (N_EVEN, GDN_HEADS), f32, 1.0, 16.0)),
        'gdn_dt_bias': dt + jnp.log(-jnp.expm1(-dt)),
        'gdn_norm_w': gain(ks[7], (N_EVEN, GDN_DV)),
        'hgrn_lb_logits': normal(ks[8], (DEPTH + 1, HGRN_K_W), 0.1),
        'hgrn_norm_w': gain(ks[9], (N_EVEN, HGRN_DV)),
        'even_w_out': normal(ks[10], (N_EVEN, EVEN_MIX_W, D_MODEL), EVEN_MIX_W ** -0.5),
        'odd_w_in': normal(ks[11], (N_ODD, D_MODEL, ODD_IN), D_MODEL ** -0.5),
        'ret_norm_w': gain(ks[12], (N_ODD, RET_DV)),
        'odd_w_out': normal(ks[13], (N_ODD, ODD_MIX_W, D_MODEL), ODD_MIX_W ** -0.5),
        'router_c_w': normal(ks[14], (DEPTH, D_MODEL, N_GROUPS), D_MODEL ** -0.5),
        'router_c_b': normal(ks[15], (DEPTH, N_GROUPS), 0.01),
        'router_f_w': normal(ks[16], (DEPTH, D_MODEL, N_EXPERTS), D_MODEL ** -0.5),
        'router_f_b': normal(ks[17], (DEPTH, N_EXPERTS), 0.01),
        'moe_w_gate_up': normal(ks[18], (DEPTH, N_EXPERTS, D_MODEL, 2 * D_EXPERT), D_MODEL ** -0.5),
        'moe_w_down': normal(ks[19], (DEPTH, N_EXPERTS, D_EXPERT, D_MODEL), D_EXPERT ** -0.5),
        'final_norm_w': gain(ks[20], (D_MODEL,)),
    }


def reference(x, norm_mix_w, norm_ffn_w, even_w_in, gdn_conv_w, gdn_a_log, gdn_dt_bias, gdn_norm_w,
              hgrn_lb_logits, hgrn_norm_w, even_w_out, odd_w_in, ret_norm_w, odd_w_out,
              router_c_w, router_c_b, router_f_w, router_f_b, moe_w_gate_up, moe_w_down, final_norm_w):
    lower_bounds = jnp.cumsum(jax.nn.softmax(hgrn_lb_logits.astype(jnp.float32), axis=0), axis=0)
    for layer in range(DEPTH):
        j = layer // 2
        h = rms_norm(x, norm_mix_w[layer])
        if layer % 2 == 0:
            mix = even_mixer(h, even_w_in[j], gdn_conv_w[j], gdn_a_log[j], gdn_dt_bias[j], gdn_norm_w[j],
                             lower_bounds[layer], hgrn_norm_w[j], even_w_out[j])
        else:
            mix = retention_mixer(h, odd_w_in[j], ret_norm_w[j], odd_w_out[j])
        x = x + mix
        h = rms_norm(x, norm_ffn_w[layer])
        x = x + hierarchical_moe(h, router_c_w[layer], router_c_b[layer], router_f_w[layer], router_f_b[layer],
                                 moe_w_gate_up[layer], moe_w_down[layer])
    return rms_norm(x, final_norm_w)
```

```python
import functools
import math

import numpy as np
import jax
import jax.numpy as jnp
from jax import lax
from jax.experimental import pallas as pl
from jax.experimental.pallas import tpu as pltpu

F32 = jnp.float32
BF16 = jnp.bfloat16

D_MODEL = 1024
RMS_EPS = 1e-6
L2_EPS = 1e-6
CHUNK = 64
CONV_K = 4
N_HEADS = 4
HEAD_D = 128
RET_DK = 256
RET_DV = 512
RET_CHUNK = 256
ROPE_BASE = 10000.0
N_GROUPS = 4
EXPERTS_PER_GROUP = 4
N_EXPERTS = 16
D_EXPERT = 256
LANES = 128
VMEM_LIMIT = 56 * 1024 * 1024


def _mm(a, b):
    return jnp.dot(a.astype(BF16), b.astype(BF16), preferred_element_type=F32)


def _mm_nt(a, b):
    return lax.dot_general(a.astype(BF16), b.astype(BF16), (((1,), (1,)), ((), ())),
                           preferred_element_type=F32)


def _mm_tn(a, b):
    return _mm(a.T, b)


def _mm_01(m01, x):
    hi = x.astype(BF16)
    lo = (x - hi.astype(F32)).astype(BF16)
    return (jnp.dot(m01, hi, preferred_element_type=F32)
            + jnp.dot(m01, lo, preferred_element_type=F32))


def _sigmoid(x):
    return 1.0 / (1.0 + jnp.exp(-x))


def _silu(x):
    return x * _sigmoid(x)


def _softplus(x):
    return jnp.maximum(x, 0.0) + jnp.log(1.0 + jnp.exp(-jnp.abs(x)))


def _iota2(shape, dim):
    return lax.broadcasted_iota(jnp.int32, shape, dim)


def _head_rms_gate(o, gate, nw, width):
    outs = []
    for h in range(o.shape[1] // width):
        oh = o[:, h * width:(h + 1) * width]
        ms = jnp.mean(oh * oh, axis=-1, keepdims=True)
        outs.append(oh * lax.rsqrt(ms + RMS_EPS) * nw)
    return jnp.concatenate(outs, axis=1) * _silu(gate)


def _norm_mm_kernel(x_ref, nw_ref, w_ref, ws_ref, o_ref, os_ref, h_ref):
    @pl.when(pl.program_id(1) == 0)
    def _():
        x = x_ref[...]
        ms = jnp.mean(x * x, axis=-1, keepdims=True)
        hb = (x * lax.rsqrt(ms + RMS_EPS) * nw_ref[...]).astype(BF16)
        h_ref[...] = hb
        os_ref[...] = jnp.dot(hb, ws_ref[...], preferred_element_type=F32)

    o_ref[...] = jnp.dot(h_ref[...], w_ref[...], preferred_element_type=F32).astype(o_ref.dtype)


def _norm_matmul(x, nw, w, ws, *, tm, tn, name):
    n, d = x.shape
    nout = w.shape[1]
    nsmall = ws.shape[1]
    return pl.pallas_call(
        _norm_mm_kernel,
        grid=(n // tm, nout // tn),
        in_specs=[
            pl.BlockSpec((tm, d), lambda i, j: (i, 0)),
            pl.BlockSpec((1, d), lambda i, j: (0, 0)),
            pl.BlockSpec((d, tn), lambda i, j: (0, j)),
            pl.BlockSpec((d, nsmall), lambda i, j: (0, 0)),
        ],
        out_specs=[
            pl.BlockSpec((tm, tn), lambda i, j: (i, j)),
            pl.BlockSpec((tm, nsmall), lambda i, j: (i, 0)),
        ],
        out_shape=[jax.ShapeDtypeStruct((n, nout), BF16),
                   jax.ShapeDtypeStruct((n, nsmall), F32)],
        scratch_shapes=[pltpu.VMEM((tm, d), BF16)],
        compiler_params=pltpu.CompilerParams(
            dimension_semantics=("parallel", "arbitrary"), vmem_limit_bytes=VMEM_LIMIT),
        name=name,
    )(x, nw, w, ws)


def _gdn_kernel(qkv_ref, z_ref, sm_ref, cw_ref, alog_ref, dtb_ref, nw_ref, o_ref,
                xs_ref, qkv_s, bg_s, o_s, st_ref, *, tb):
    @pl.when(pl.program_id(1) == 0)
    def _():
        xs_ref[0:8, :] = jnp.zeros((8, xs_ref.shape[1]), F32)
        st_ref[...] = jnp.zeros(st_ref.shape, F32)

    x = qkv_ref[0].astype(F32)
    xs_ref[8:8 + tb, :] = x
    cw = cw_ref[...]
    y = x * cw[CONV_K - 1:CONV_K, :]
    for j in range(1, CONV_K):
        y = y + xs_ref[pl.ds(8 - j, tb), :] * cw[CONV_K - 1 - j:CONV_K - j, :]
    xs_ref[0:8, :] = xs_ref[tb:tb + 8, :]
    y = _silu(y)

    qk_w = N_HEADS * HEAD_D
    for h in range(N_HEADS):
        for base, scale in ((0, HEAD_D ** -0.5), (qk_w, 1.0)):
            c0 = base + h * HEAD_D
            v = y[:, c0:c0 + HEAD_D]
            inv = lax.rsqrt(jnp.sum(v * v, axis=-1, keepdims=True) + L2_EPS)
            qkv_s[:, c0:c0 + HEAD_D] = v * (inv * scale)
    qkv_s[:, 2 * qk_w:] = y[:, 2 * qk_w:]

    sm = sm_ref[0]
    bg_s[0] = _sigmoid(sm[:, :LANES])
    bg_s[1] = -jnp.exp(alog_ref[...]) * _softplus(sm[:, LANES:] + dtb_ref[...])

    c = CHUNK
    row = _iota2((c, c), 0)
    col = _iota2((c, c), 1)
    causal = row >= col
    strict = row > col
    same16 = (row >> 4) == (col >> 4)
    same32 = (row >> 5) == (col >> 5)
    eye = (row == col).astype(F32)
    ltri = causal.astype(BF16)

    def chunk_body(ci, carry):
        r0 = pl.multiple_of(ci * c, c)
        gc = _mm_01(ltri, bg_s[1, pl.ds(r0, c), :])
        gct = gc.T
        beta_all = bg_s[0, pl.ds(r0, c), :]
        for h in range(N_HEADS):
            q = qkv_s[pl.ds(r0, c), h * HEAD_D:(h + 1) * HEAD_D]
            k = qkv_s[pl.ds(r0, c), qk_w + h * HEAD_D:qk_w + (h + 1) * HEAD_D]
            v = qkv_s[pl.ds(r0, c), 2 * qk_w + h * HEAD_D:2 * qk_w + (h + 1) * HEAD_D]
            beta = beta_all[:, h:h + 1]
            gcol = gc[:, h:h + 1]
            grow = gct[h:h + 1, :]
            decay = jnp.exp(jnp.where(causal, gcol - grow, -jnp.inf))
            kb = k * beta
            a = jnp.where(strict, _mm_nt(kb, k) * decay, 0.0)
            d = jnp.where(same16, a, 0.0)
            x_inv = eye - d
            dp = d
            for _ in range(3):
                dp = _mm(dp, dp)
                x_inv = x_inv + _mm(x_inv, dp)
            e = jnp.where(jnp.logical_and(same32, jnp.logical_not(same16)), a, 0.0)
            x_inv = x_inv - _mm(x_inv, _mm(e, x_inv))
            f = jnp.where(same32, 0.0, a)
            x_inv = x_inv - _mm(x_inv, _mm(f, x_inv))
            rhs = jnp.concatenate([v * beta, kb * jnp.exp(gcol)], axis=1)
            sol = _mm(x_inv, rhs)
            u = sol[:, :HEAD_D]
            w = sol[:, HEAD_D:]
            s_h = st_ref[h]
            v_new = u - _mm(w, s_h)
            attn = _mm_nt(q, k) * decay
            o = _mm(q * jnp.exp(gcol), s_h) + _mm(attn, v_new)
            g_last = gcol[c - 1:c, :]
            st_ref[h] = s_h * jnp.exp(g_last) + _mm_tn(k * jnp.exp(g_last - gcol), v_new)
            o_s[pl.ds(r0, c), h * HEAD_D:(h + 1) * HEAD_D] = o
        return carry

    lax.fori_loop(0, tb // c, chunk_body, 0)
    o_ref[0] = _head_rms_gate(o_s[...], z_ref[0].astype(F32), nw_ref[...], HEAD_D).astype(o_ref.dtype)


def _gdn(proj, small, conv_w, a_log, dt_bias, norm_w, *, tb):
    b, t, _ = proj.shape
    conv_cols = 3 * N_HEADS * HEAD_D
    mix_w = N_HEADS * HEAD_D
    return pl.pallas_call(
        functools.partial(_gdn_kernel, tb=tb),
        grid=(b, t // tb),
        in_specs=[
            pl.BlockSpec((1, tb, conv_cols), lambda i, j: (i, j, 0)),
            pl.BlockSpec((1, tb, mix_w), lambda i, j: (i, j, conv_cols // mix_w)),
            pl.BlockSpec((1, tb, 2 * LANES), lambda i, j: (i, j, 0)),
            pl.BlockSpec((CONV_K, conv_cols), lambda i, j: (0, 0)),
            pl.BlockSpec((1, LANES), lambda i, j: (0, 0)),
            pl.BlockSpec((1, LANES), lambda i, j: (0, 0)),
            pl.BlockSpec((1, HEAD_D), lambda i, j: (0, 0)),
        ],
        out_specs=pl.BlockSpec((1, tb, mix_w), lambda i, j: (i, j, 0)),
        out_shape=jax.ShapeDtypeStruct((b, t, mix_w), BF16),
        scratch_shapes=[
            pltpu.VMEM((tb + 8, conv_cols), F32),
            pltpu.VMEM((tb, conv_cols), F32),
            pltpu.VMEM((2, tb, LANES), F32),
            pltpu.VMEM((tb, mix_w), F32),
            pltpu.VMEM((N_HEADS, HEAD_D, HEAD_D), F32),
        ],
        compiler_params=pltpu.CompilerParams(
            dimension_semantics=("parallel", "arbitrary"), vmem_limit_bytes=VMEM_LIMIT),
        name="gdn",
    )(proj, proj, small, conv_w, a_log, dt_bias, norm_w)


def _hgrn_kernel(q_ref, f_ref, i_ref, g_ref, lb_ref, nw_ref, o_ref,
                 q_s, k_s, lf_s, i_s, o_s, st_ref, *, tb, layer):
    @pl.when(pl.program_id(1) == 0)
    def _():
        st_ref[...] = jnp.zeros(st_ref.shape, F32)

    lbl = lb_ref[...]
    e_lb = jnp.exp(lbl - jnp.max(lbl, axis=0, keepdims=True))
    lb = jnp.sum(e_lb[:layer + 1], axis=0, keepdims=True) / jnp.sum(e_lb, axis=0, keepdims=True)

    f = lb + (1.0 - lb) * _sigmoid(f_ref[0].astype(F32))
    k_s[...] = 1.0 - f
    lf_s[...] = jnp.log(f)
    q_s[...] = _silu(q_ref[0].astype(F32)) * (HEAD_D ** -0.5)
    i_s[...] = i_ref[0].astype(F32)

    c = CHUNK
    row = _iota2((c, c), 0)
    col = _iota2((c, c), 1)
    ltri = (row >= col).astype(BF16)
    level_masks = {}
    for m in (32, 16, 8):
        sh = int(math.log2(2 * m))
        level_masks[m] = jnp.logical_and(
            (row >> sh) == (col >> sh),
            jnp.logical_and((row & (2 * m - 1)) >= m, (col & (2 * m - 1)) < m))
    sub = _iota2((8, HEAD_D), 0)

    def chunk_body(ci, carry):
        r0 = pl.multiple_of(ci * c, c)
        b_all = _mm_01(ltri, lf_s[pl.ds(r0, c), :])
        for h in range(N_HEADS):
            sl = slice(h * HEAD_D, (h + 1) * HEAD_D)
            b = b_all[:, sl]
            q = q_s[pl.ds(r0, c), sl]
            k = k_s[pl.ds(r0, c), sl]
            iv = i_s[pl.ds(r0, c), sl]
            attn = jnp.zeros((c, c), F32)
            for m in (32, 16, 8):
                ref = jnp.concatenate(
                    [jnp.broadcast_to(b[r:r + 1, :], (2 * m, HEAD_D)) for r in range(m, c, 2 * m)], axis=0)
                e = jnp.exp(-jnp.abs(b - ref))
                attn = attn + jnp.where(level_masks[m], _mm_nt(q * e, k * e), 0.0)
            s_t = st_ref[h]
            o = _mm_nt(q * jnp.exp(b), s_t) + _mm(attn, iv)
            diag = []
            for r in range(0, c, 8):
                qb, kb, bb, ib = q[r:r + 8], k[r:r + 8], b[r:r + 8], iv[r:r + 8]
                ob = jnp.zeros((8, HEAD_D), F32)
                for s in range(8):
                    dec = jnp.exp(jnp.where(sub >= s, bb - bb[s:s + 1, :], -jnp.inf))
                    a_col = jnp.sum(dec * qb * kb[s:s + 1, :], axis=-1, keepdims=True)
                    ob = ob + a_col * ib[s:s + 1, :]
                diag.append(ob)
            o = o + jnp.concatenate(diag, axis=0)
            b_last = b[c - 1:c, :]
            st_ref[h] = s_t * jnp.exp(b_last) + _mm_tn(iv, k * jnp.exp(b_last - b))
            o_s[pl.ds(r0, c), sl] = o
        return carry

    lax.fori_loop(0, tb // c, chunk_body, 0)
    o_ref[0] = _head_rms_gate(o_s[...], g_ref[0].astype(F32), nw_ref[...], HEAD_D).astype(o_ref.dtype)


def _hgrn(proj, lb_logits, norm_w, *, tb, layer, col0):
    b, t, _ = proj.shape
    w = N_HEADS * HEAD_D
    specs = [pl.BlockSpec((1, tb, w), functools.partial(lambda i, j, off: (i, j, off), off=col0 + n))
             for n in range(4)]
    return pl.pallas_call(
        functools.partial(_hgrn_kernel, tb=tb, layer=layer),
        grid=(b, t // tb),
        in_specs=specs + [
            pl.BlockSpec(lb_logits.shape, lambda i, j: (0, 0)),
            pl.BlockSpec((1, HEAD_D), lambda i, j: (0, 0)),
        ],
        out_specs=pl.BlockSpec((1, tb, w), lambda i, j: (i, j, 0)),
        out_shape=jax.ShapeDtypeStruct((b, t, w), BF16),
        scratch_shapes=[pltpu.VMEM((tb, w), F32)] * 5 + [pltpu.VMEM((N_HEADS, HEAD_D, HEAD_D), F32)],
        compiler_params=pltpu.CompilerParams(
            dimension_semantics=("parallel", "arbitrary"), vmem_limit_bytes=VMEM_LIMIT),
        name="hgrn2",
    )(proj, proj, proj, proj, lb_logits, norm_w)


def _ret_kernel(q_ref, k_ref, v_ref, g_ref, cos_ref, sin_ref, nw_ref, o_ref, dmat_s, st_ref, *, tc):
    hh = pl.program_id(1).astype(F32)
    log_gamma = jnp.log(1.0 - jnp.exp((jnp.full((1, 1), -5.0, F32) - hh) * math.log(2.0)))

    @pl.when(pl.program_id(2) == 0)
    def _():
        st_ref[...] = jnp.zeros(st_ref.shape, F32)
        rel = (_iota2((tc, tc), 0) - _iota2((tc, tc), 1)).astype(F32)
        dmat_s[...] = jnp.where(rel >= 0, jnp.exp(jnp.maximum(rel, 0.0) * log_gamma), 0.0)

    pos = _iota2((tc, 1), 0).astype(F32)
    q_decay = jnp.exp((pos + 1.0) * log_gamma)
    k_decay = jnp.exp((tc - 1.0 - pos) * log_gamma)
    chunk_decay = jnp.exp(tc * log_gamma)

    cos = cos_ref[...]
    sin = sin_ref[...]
    half = RET_DK // 2

    def rot(x):
        x1, x2 = x[:, :half], x[:, half:]
        return jnp.concatenate([x1 * cos - x2 * sin, x1 * sin + x2 * cos], axis=1)

    q = rot(q_ref[0].astype(F32))
    k = rot(k_ref[0].astype(F32)) * (RET_DK ** -0.5)
    v = v_ref[0]
    s = st_ref[...]
    attn = _mm_nt(q, k) * dmat_s[...]
    o = _mm(q, s) * q_decay + _mm(attn, v)
    st_ref[...] = s * chunk_decay + _mm_tn(k * k_decay, v)
    ms = jnp.mean(o * o, axis=-1, keepdims=True)
    o_ref[0] = (o * lax.rsqrt(ms + RMS_EPS) * nw_ref[...] * _silu(g_ref[0].astype(F32))).astype(o_ref.dtype)


def _retention(proj, cos, sin, norm_w, *, tc):
    b, t, _ = proj.shape
    qb = N_HEADS
    vb = 2 * N_HEADS * RET_DK // RET_DV
    gb = vb + N_HEADS
    return pl.pallas_call(
        functools.partial(_ret_kernel, tc=tc),
        grid=(b, N_HEADS, t // tc),
        in_specs=[
            pl.BlockSpec((1, tc, RET_DK), lambda i, h, j: (i, j, h)),
            pl.BlockSpec((1, tc, RET_DK), lambda i, h, j: (i, j, qb + h)),
            pl.BlockSpec((1, tc, RET_DV), lambda i, h, j: (i, j, vb + h)),
            pl.BlockSpec((1, tc, RET_DV), lambda i, h, j: (i, j, gb + h)),
            pl.BlockSpec((tc, RET_DK // 2), lambda i, h, j: (j, 0)),
            pl.BlockSpec((tc, RET_DK // 2), lambda i, h, j: (j, 0)),
            pl.BlockSpec((1, RET_DV), lambda i, h, j: (0, 0)),
        ],
        out_specs=pl.BlockSpec((1, tc, RET_DV), lambda i, h, j: (i, j, h)),
        out_shape=jax.ShapeDtypeStruct((b, t, N_HEADS * RET_DV), BF16),
        scratch_shapes=[pltpu.VMEM((tc, tc), F32), pltpu.VMEM((RET_DK, RET_DV), F32)],
        compiler_params=pltpu.CompilerParams(
            dimension_semantics=("parallel", "parallel", "arbitrary"), vmem_limit_bytes=VMEM_LIMIT),
        name="retention",
    )(proj, proj, proj, proj, cos, sin, norm_w)


def _route_gates(logits_t):
    cl = [logits_t[g:g + 1, :] for g in range(N_GROUPS)]
    cmax = functools.reduce(jnp.maximum, cl)
    denom = sum(jnp.exp(x - cmax) for x in cl)
    g_prob = 1.0 / denom
    g_idx = jnp.full(cmax.shape, N_GROUPS - 1, jnp.int32)
    for g in range(N_GROUPS - 2, -1, -1):
        g_idx = jnp.where(cl[g] == cmax, g, g_idx)
    def fine_row(g, j):
        r = N_GROUPS + g * EXPERTS_PER_GROUP + j
        return logits_t[r:r + 1, :]

    fl = []
    for j in range(EXPERTS_PER_GROUP):
        x = fine_row(N_GROUPS - 1, j)
        for g in range(N_GROUPS - 2, -1, -1):
            x = jnp.where(g_idx == g, fine_row(g, j), x)
        fl.append(x)
    m1 = functools.reduce(jnp.maximum, fl)
    i1 = jnp.full(m1.shape, EXPERTS_PER_GROUP - 1, jnp.int32)
    for j in range(EXPERTS_PER_GROUP - 2, -1, -1):
        i1 = jnp.where(fl[j] == m1, j, i1)
    rest = [jnp.where(i1 == j, -jnp.inf, fl[j]) for j in range(EXPERTS_PER_GROUP)]
    m2 = functools.reduce(jnp.maximum, rest)
    i2 = jnp.full(m2.shape, EXPERTS_PER_GROUP - 1, jnp.int32)
    for j in range(EXPERTS_PER_GROUP - 2, -1, -1):
        i2 = jnp.where(jnp.logical_and(rest[j] == m2, i1 != j), j, i2)
    e2 = jnp.exp(m2 - m1)
    w1 = g_prob / (1.0 + e2)
    w2 = g_prob * e2 / (1.0 + e2)
    id1 = g_idx * EXPERTS_PER_GROUP + i1
    id2 = g_idx * EXPERTS_PER_GROUP + i2
    eidx = _iota2((N_EXPERTS, cmax.shape[1]), 0)
    return jnp.where(eidx == id1, w1, 0.0) + jnp.where(eidx == id2, w2, 0.0)


def _outproj_kernel(a_ref, b_ref, wa_ref, wb_ref, x_ref, nw_ref, rw_ref, rb_ref,
                    x1_ref, h_ref, gates_ref):
    x1 = (x_ref[...]
          + jnp.dot(a_ref[...], wa_ref[...], preferred_element_type=F32)
          + jnp.dot(b_ref[...], wb_ref[...], preferred_element_type=F32))
    x1_ref[...] = x1
    ms = jnp.mean(x1 * x1, axis=-1, keepdims=True)
    h = x1 * lax.rsqrt(ms + RMS_EPS) * nw_ref[...]
    h_ref[...] = h.astype(BF16)
    rw = rw_ref[...]
    hh = h.astype(BF16)
    hl = (h - hh.astype(F32)).astype(BF16)
    wh = rw.astype(BF16)
    wl = (rw - wh.astype(F32)).astype(BF16)
    logits = (jnp.dot(hh, wh, preferred_element_type=F32)
              + jnp.dot(hl, wh, preferred_element_type=F32)
              + jnp.dot(hh, wl, preferred_element_type=F32)) + rb_ref[...]
    gates_t = _route_gates(logits.T)
    gates_pad = jnp.concatenate(
        [gates_t, jnp.zeros((LANES - N_EXPERTS, gates_t.shape[1]), F32)], axis=0)
    gates_ref[...] = gates_pad.T


def _outproj_route(a, b, a_blk, b_blk, wa, wb, x, nw, rw, rb, *, tm, name):
    n, d = x.shape
    ka = wa.shape[0]
    kb = wb.shape[0]
    return pl.pallas_call(
        _outproj_kernel,
        grid=(n // tm,),
        in_specs=[
            pl.BlockSpec((tm, ka), lambda i: (i, a_blk)),
            pl.BlockSpec((tm, kb), lambda i: (i, b_blk)),
            pl.BlockSpec((ka, d), lambda i: (0, 0)),
            pl.BlockSpec((kb, d), lambda i: (0, 0)),
            pl.BlockSpec((tm, d), lambda i: (i, 0)),
            pl.BlockSpec((1, d), lambda i: (0, 0)),
            pl.BlockSpec((d, LANES), lambda i: (0, 0)),
            pl.BlockSpec((1, LANES), lambda i: (0, 0)),
        ],
        out_specs=[
            pl.BlockSpec((tm, d), lambda i: (i, 0)),
            pl.BlockSpec((tm, d), lambda i: (i, 0)),
            pl.BlockSpec((tm, LANES), lambda i: (i, 0)),
        ],
        out_shape=[jax.ShapeDtypeStruct((n, d), F32),
                   jax.ShapeDtypeStruct((n, d), BF16),
                   jax.ShapeDtypeStruct((n, LANES), F32)],
        compiler_params=pltpu.CompilerParams(
            dimension_semantics=("parallel",), vmem_limit_bytes=VMEM_LIMIT),
        name=name,
    )(a, b, wa, wb, x, nw, rw, rb)


def _moe_kernel(h_ref, gates_ref, wgu_ref, wd_ref, x_ref, fnw_ref, o_ref, acc_ref, *, final_norm):
    e = pl.program_id(1)

    @pl.when(e == 0)
    def _():
        acc_ref[...] = x_ref[...]

    gu = jnp.dot(h_ref[...], wgu_ref[0], preferred_element_type=F32)
    gates = gates_ref[...]
    gate_col = jnp.sum(jnp.where(_iota2(gates.shape, 1) == e, gates, 0.0), axis=-1, keepdims=True)
    act = _silu(gu[:, :D_EXPERT]) * gu[:, D_EXPERT:] * gate_col
    acc_ref[...] += jnp.dot(act.astype(BF16), wd_ref[0], preferred_element_type=F32)

    @pl.when(e == pl.num_programs(1) - 1)
    def _():
        y = acc_ref[...]
        if final_norm:
            ms = jnp.mean(y * y, axis=-1, keepdims=True)
            y = y * lax.rsqrt(ms + RMS_EPS) * fnw_ref[...]
        o_ref[...] = y


def _moe(h, gates, wgu, wd, x, fnw, *, tm, final_norm, name):
    n, d = x.shape
    return pl.pallas_call(
        functools.partial(_moe_kernel, final_norm=final_norm),
        grid=(n // tm, N_EXPERTS),
        in_specs=[
            pl.BlockSpec((tm, d), lambda i, e: (i, 0)),
            pl.BlockSpec((tm, LANES), lambda i, e: (i, 0)),
            pl.BlockSpec((1, d, 2 * D_EXPERT), lambda i, e: (e, 0, 0)),
            pl.BlockSpec((1, D_EXPERT, d), lambda i, e: (e, 0, 0)),
            pl.BlockSpec((tm, d), lambda i, e: (i, 0)),
            pl.BlockSpec((1, d), lambda i, e: (0, 0)),
        ],
        out_specs=pl.BlockSpec((tm, d), lambda i, e: (i, 0)),
        out_shape=jax.ShapeDtypeStruct((n, d), F32),
        scratch_shapes=[pltpu.VMEM((tm, d), F32)],
        compiler_params=pltpu.CompilerParams(
            dimension_semantics=("parallel", "arbitrary"), vmem_limit_bytes=VMEM_LIMIT),
        name=name,
    )(h, gates, wgu, wd, x, fnw)


def _pad_cols(a, width):
    return jnp.pad(a, ((0, 0), (0, width - a.shape[1])))


def _router_params(wc, bc, wf, bf):
    rw = _pad_cols(jnp.concatenate([wc, wf], axis=1), LANES)
    rb = _pad_cols(jnp.concatenate([bc, bf])[None, :], LANES)
    return rw, rb


def kernel(x, norm_mix_w, norm_ffn_w, even_w_in, gdn_conv_w, gdn_a_log, gdn_dt_bias, gdn_norm_w, hgrn_lb_logits, hgrn_norm_w, even_w_out, odd_w_in, ret_norm_w, odd_w_out, router_c_w, router_c_b, router_f_w, router_f_b, moe_w_gate_up, moe_w_down, final_norm_w):
    bsz, seq, d = x.shape
    n = bsz * seq
    xt = x.reshape(n, d)
    mix_w = N_HEADS * HEAD_D
    conv_cols = 3 * mix_w
    gdn_main = conv_cols + mix_w

    w_in = even_w_in[0]
    small0 = gdn_main
    w_main = jnp.concatenate([w_in[:, :small0], w_in[:, small0 + 2 * N_HEADS:]], axis=1).astype(BF16)
    w_small = jnp.concatenate(
        [_pad_cols(w_in[:, small0:small0 + N_HEADS], LANES),
         _pad_cols(w_in[:, small0 + N_HEADS:small0 + 2 * N_HEADS], LANES)], axis=1).astype(BF16)
    proj, small = _norm_matmul(xt, norm_mix_w[0][None, :], w_main, w_small, tm=512, tn=1024, name="in_proj_even")
    proj = proj.reshape(bsz, seq, -1)
    small = small.reshape(bsz, seq, -1)
    o_a = _gdn(proj, small, gdn_conv_w[0], _pad_cols(gdn_a_log[0][None, :], LANES),
               _pad_cols(gdn_dt_bias[0][None, :], LANES), gdn_norm_w[0][None, :], tb=512)
    o_b = _hgrn(proj, hgrn_lb_logits, hgrn_norm_w[0][None, :], tb=512, layer=0, col0=gdn_main // mix_w)
    w_out = even_w_out[0].astype(BF16)
    rw, rb = _router_params(router_c_w[0], router_c_b[0], router_f_w[0], router_f_b[0])
    x1, h, gates = _outproj_route(o_a.reshape(n, mix_w), o_b.reshape(n, mix_w), 0, 0,
                                  w_out[:mix_w], w_out[mix_w:], xt, norm_ffn_w[0][None, :], rw, rb,
                                  tm=512, name="out_proj_even")
    x2 = _moe(h, gates, moe_w_gate_up[0].astype(BF16), moe_w_down[0].astype(BF16), x1,
              final_norm_w[None, :], tm=1024, final_norm=False, name="moe0")

    perm = np.concatenate([np.arange(0, RET_DK, 2), np.arange(1, RET_DK, 2)])
    qk_perm = np.concatenate([hd * RET_DK + perm for hd in range(2 * N_HEADS)])
    w_odd = odd_w_in[0]
    w_odd = jnp.concatenate([w_odd[:, qk_perm], w_odd[:, 2 * N_HEADS * RET_DK:]], axis=1).astype(BF16)
    proj1, _ = _norm_matmul(x2, norm_mix_w[1][None, :], w_odd, jnp.zeros((d, LANES), BF16),
                            tm=512, tn=1024, name="in_proj_odd")
    inv = 1.0 / (ROPE_BASE ** jnp.linspace(0.0, 1.0, RET_DK // 2, dtype=F32))
    ang = jnp.arange(seq, dtype=F32)[:, None] * inv[None, :]
    o_c = _retention(proj1.reshape(bsz, seq, -1), jnp.cos(ang), jnp.sin(ang), ret_norm_w[0][None, :],
                     tc=RET_CHUNK)
    o_c = o_c.reshape(n, -1)
    w_out1 = odd_w_out[0].astype(BF16)
    half = w_out1.shape[0] // 2
    rw, rb = _router_params(router_c_w[1], router_c_b[1], router_f_w[1], router_f_b[1])
    x3, h, gates = _outproj_route(o_c, o_c, 0, 1, w_out1[:half], w_out1[half:], x2,
                                  norm_ffn_w[1][None, :], rw, rb, tm=512, name="out_proj_odd")
    out = _moe(h, gates, moe_w_gate_up[1].astype(BF16), moe_w_down[1].astype(BF16), x3,
               final_norm_w[None, :], tm=1024, final_norm=True, name="moe1")
    return out.reshape(bsz, seq, d)
```

```python
import functools
import math

import numpy as np
import jax
import jax.numpy as jnp
from jax import lax
from jax.experimental import pallas as pl
from jax.experimental.pallas import tpu as pltpu

F32 = jnp.float32
BF16 = jnp.bfloat16

D_MODEL = 1024
RMS_EPS = 1e-6
L2_EPS = 1e-6
CHUNK = 64
CONV_K = 4
N_HEADS = 4
HEAD_D = 128
RET_DK = 256
RET_DV = 512
RET_CHUNK = 256
ROPE_BASE = 10000.0
N_GROUPS = 4
EXPERTS_PER_GROUP = 4
N_EXPERTS = 16
D_EXPERT = 256
LANES = 128
VMEM_LIMIT = 56 * 1024 * 1024


def _mm(a, b):
    return jnp.dot(a.astype(BF16), b.astype(BF16), preferred_element_type=F32)


def _mm_nt(a, b):
    return lax.dot_general(a.astype(BF16), b.astype(BF16), (((1,), (1,)), ((), ())),
                           preferred_element_type=F32)


def _mm_tn(a, b):
    return _mm(a.T, b)


def _mm_01(m01, x):
    hi = x.astype(BF16)
    lo = (x - hi.astype(F32)).astype(BF16)
    return (jnp.dot(m01, hi, preferred_element_type=F32)
            + jnp.dot(m01, lo, preferred_element_type=F32))


def _bmm(a, b):
    return jnp.einsum('cik,ckj->cij', a.astype(BF16), b.astype(BF16), preferred_element_type=F32)


def _bmm_nt(a, b):
    return jnp.einsum('cik,cjk->cij', a.astype(BF16), b.astype(BF16), preferred_element_type=F32)


def _bmm_tn(a, b):
    return _bmm(jnp.swapaxes(a, 1, 2), b)


def _bmm_01(m01, x):
    hi = x.astype(BF16)
    lo = (x - hi.astype(F32)).astype(BF16)
    return (jnp.einsum('cik,ckj->cij', m01, hi, preferred_element_type=F32)
            + jnp.einsum('cik,ckj->cij', m01, lo, preferred_element_type=F32))


def _sigmoid(x):
    return 1.0 / (1.0 + jnp.exp(-x))


def _silu(x):
    return x * _sigmoid(x)


def _softplus(x):
    return jnp.maximum(x, 0.0) + jnp.log(1.0 + jnp.exp(-jnp.abs(x)))


def _iota2(shape, dim):
    return lax.broadcasted_iota(jnp.int32, shape, dim)


def _head_rms_gate(o, gate, nw, width):
    outs = []
    for h in range(o.shape[1] // width):
        oh = o[:, h * width:(h + 1) * width]
        ms = jnp.mean(oh * oh, axis=-1, keepdims=True)
        outs.append(oh * lax.rsqrt(ms + RMS_EPS) * nw)
    return jnp.concatenate(outs, axis=1) * _silu(gate)


def _norm_mm_kernel(x_ref, nw_ref, w_ref, ws_ref, o_ref, os_ref, h_ref):
    @pl.when(pl.program_id(1) == 0)
    def _():
        x = x_ref[...]
        ms = jnp.mean(x * x, axis=-1, keepdims=True)
        hb = (x * lax.rsqrt(ms + RMS_EPS) * nw_ref[...]).astype(BF16)
        h_ref[...] = hb
        os_ref[...] = jnp.dot(hb, ws_ref[...], preferred_element_type=F32)

    o_ref[...] = jnp.dot(h_ref[...], w_ref[...], preferred_element_type=F32).astype(o_ref.dtype)


def _norm_matmul(x, nw, w, ws, *, tm, tn, name):
    n, d = x.shape
    nout = w.shape[1]
    nsmall = ws.shape[1]
    return pl.pallas_call(
        _norm_mm_kernel,
        grid=(n // tm, nout // tn),
        in_specs=[
            pl.BlockSpec((tm, d), lambda i, j: (i, 0)),
            pl.BlockSpec((1, d), lambda i, j: (0, 0)),
            pl.BlockSpec((d, tn), lambda i, j: (0, j)),
            pl.BlockSpec((d, nsmall), lambda i, j: (0, 0)),
        ],
        out_specs=[
            pl.BlockSpec((tm, tn), lambda i, j: (i, j)),
            pl.BlockSpec((tm, nsmall), lambda i, j: (i, 0)),
        ],
        out_shape=[jax.ShapeDtypeStruct((n, nout), BF16),
                   jax.ShapeDtypeStruct((n, nsmall), F32)],
        scratch_shapes=[pltpu.VMEM((tm, d), BF16)],
        compiler_params=pltpu.CompilerParams(
            dimension_semantics=("parallel", "arbitrary"), vmem_limit_bytes=VMEM_LIMIT),
        name=name,
    )(x, nw, w, ws)


def _gdn_kernel(qkv_ref, z_ref, sm_ref, cw_ref, alog_ref, dtb_ref, nw_ref, o_ref,
                xs_ref, qkv_s, bg_s, o_s, m_s, sq_s, qe_s, dec_s, st_ref, *, tb):
    @pl.when(pl.program_id(1) == 0)
    def _():
        xs_ref[0:8, :] = jnp.zeros((8, xs_ref.shape[1]), F32)
        st_ref[...] = jnp.zeros(st_ref.shape, F32)

    x = qkv_ref[0].astype(F32)
    xs_ref[8:8 + tb, :] = x
    cw = cw_ref[...]
    y = x * cw[CONV_K - 1:CONV_K, :]
    for j in range(1, CONV_K):
        y = y + xs_ref[pl.ds(8 - j, tb), :] * cw[CONV_K - 1 - j:CONV_K - j, :]
    xs_ref[0:8, :] = xs_ref[tb:tb + 8, :]
    y = _silu(y)

    qk_w = N_HEADS * HEAD_D
    for h in range(N_HEADS):
        for base, scale in ((0, HEAD_D ** -0.5), (qk_w, 1.0)):
            c0 = base + h * HEAD_D
            v = y[:, c0:c0 + HEAD_D]
            inv = lax.rsqrt(jnp.sum(v * v, axis=-1, keepdims=True) + L2_EPS)
            qkv_s[:, c0:c0 + HEAD_D] = v * (inv * scale)
    qkv_s[:, 2 * qk_w:] = y[:, 2 * qk_w:]

    sm = sm_ref[0]
    bg_s[0] = _sigmoid(sm[:, :LANES])
    bg_s[1] = -jnp.exp(alog_ref[...]) * _softplus(sm[:, LANES:] + dtb_ref[...])

    c = CHUNK
    nc = tb // c
    row = _iota2((c, c), 0)
    col = _iota2((c, c), 1)
    causal = (row >= col)[None]
    strict = (row > col)[None]
    same16 = ((row >> 4) == (col >> 4))[None]
    same32 = ((row >> 5) == (col >> 5))[None]
    eye = (row == col).astype(F32)[None]
    ltri = jnp.broadcast_to((row >= col).astype(BF16)[None], (nc, c, c))
    lane = _iota2((nc, c, LANES), 2)

    gc_all = _bmm_01(ltri, bg_s[1].reshape(nc, c, LANES))
    beta_all = bg_s[0].reshape(nc, c, LANES)
    for h in range(N_HEADS):
        hs = slice(h * HEAD_D, (h + 1) * HEAD_D)
        q = qkv_s[:, h * HEAD_D:(h + 1) * HEAD_D].reshape(nc, c, HEAD_D)
        k = qkv_s[:, qk_w + h * HEAD_D:qk_w + (h + 1) * HEAD_D].reshape(nc, c, HEAD_D)
        v = qkv_s[:, 2 * qk_w + h * HEAD_D:2 * qk_w + (h + 1) * HEAD_D].reshape(nc, c, HEAD_D)
        beta = beta_all[:, :, h:h + 1]
        gcol = gc_all[:, :, h:h + 1]
        g_hi = gcol.astype(BF16).astype(F32)
        g_mid = (gcol - g_hi).astype(BF16).astype(F32)
        g_lo = gcol - g_hi - g_mid
        ones_hi = jnp.where(lane < 6, 1.0, 0.0)
        lhs = jnp.where(lane == 0, g_hi, jnp.where(lane == 1, g_mid, jnp.where(lane == 2, g_lo, ones_hi)))
        rhs_g = jnp.where(lane == 3, -g_hi, jnp.where(lane == 4, -g_mid, jnp.where(lane == 5, -g_lo, ones_hi)))
        decay = jnp.exp(jnp.where(causal, _bmm_nt(lhs, rhs_g), -jnp.inf))
        kb = k * beta
        a = jnp.where(strict, _bmm_nt(kb, k) * decay, 0.0)
        d = jnp.where(same16, a, 0.0)
        x_inv = eye - d
        dp = d
        for _ in range(3):
            dp = _bmm(dp, dp)
            x_inv = x_inv + _bmm(x_inv, dp)
        e = jnp.where(jnp.logical_and(same32, jnp.logical_not(same16)), a, 0.0)
        x_inv = x_inv - _bmm(x_inv, _bmm(e, x_inv))
        f = jnp.where(same32, 0.0, a)
        x_inv = x_inv - _bmm(x_inv, _bmm(f, x_inv))
        wu = _bmm(x_inv, jnp.concatenate([kb * jnp.exp(gcol), v * beta], axis=2))
        attn = _bmm_nt(q, k) * decay
        g_last = gcol[:, c - 1:c, :]
        kd = k * jnp.exp(g_last - gcol)
        mq = _bmm_tn(kd, wu)
        aw = _bmm(attn, wu)
        m_s[h] = mq[:, :, :HEAD_D].astype(BF16)
        sq_s[h] = mq[:, :, HEAD_D:]
        qe_s[h] = (q * jnp.exp(gcol) - aw[:, :, :HEAD_D]).astype(BF16)
        dec_s[h] = jnp.broadcast_to(jnp.exp(g_last), (nc, 1, HEAD_D))
        o_s[:, hs] = aw[:, :, HEAD_D:].reshape(tb, HEAD_D)

    for ci in range(nc):
        for h in range(N_HEADS):
            hs = slice(h * HEAD_D, (h + 1) * HEAD_D)
            s_h = st_ref[h]
            s_b = s_h.astype(BF16)
            o_s[ci * c:(ci + 1) * c, hs] += jnp.dot(qe_s[h, ci], s_b, preferred_element_type=F32)
            st_ref[h] = (s_h * dec_s[h, ci] - jnp.dot(m_s[h, ci], s_b, preferred_element_type=F32)
                         + sq_s[h, ci])

    o_ref[0] = _head_rms_gate(o_s[...], z_ref[0].astype(F32), nw_ref[...], HEAD_D).astype(o_ref.dtype)


def _gdn(proj, small, conv_w, a_log, dt_bias, norm_w, *, tb):
    b, t, _ = proj.shape
    conv_cols = 3 * N_HEADS * HEAD_D
    mix_w = N_HEADS * HEAD_D
    return pl.pallas_call(
        functools.partial(_gdn_kernel, tb=tb),
        grid=(b, t // tb),
        in_specs=[
            pl.BlockSpec((1, tb, conv_cols), lambda i, j: (i, j, 0)),
            pl.BlockSpec((1, tb, mix_w), lambda i, j: (i, j, conv_cols // mix_w)),
            pl.BlockSpec((1, tb, 2 * LANES), lambda i, j: (i, j, 0)),
            pl.BlockSpec((CONV_K, conv_cols), lambda i, j: (0, 0)),
            pl.BlockSpec((1, LANES), lambda i, j: (0, 0)),
            pl.BlockSpec((1, LANES), lambda i, j: (0, 0)),
            pl.BlockSpec((1, HEAD_D), lambda i, j: (0, 0)),
        ],
        out_specs=pl.BlockSpec((1, tb, mix_w), lambda i, j: (i, j, 0)),
        out_shape=jax.ShapeDtypeStruct((b, t, mix_w), BF16),
        scratch_shapes=[
            pltpu.VMEM((tb + 8, conv_cols), F32),
            pltpu.VMEM((tb, conv_cols), F32),
            pltpu.VMEM((2, tb, LANES), F32),
            pltpu.VMEM((tb, mix_w), F32),
            pltpu.VMEM((N_HEADS, tb // CHUNK, HEAD_D, HEAD_D), BF16),
            pltpu.VMEM((N_HEADS, tb // CHUNK, HEAD_D, HEAD_D), F32),
            pltpu.VMEM((N_HEADS, tb // CHUNK, CHUNK, HEAD_D), BF16),
            pltpu.VMEM((N_HEADS, tb // CHUNK, 1, HEAD_D), F32),
            pltpu.VMEM((N_HEADS, HEAD_D, HEAD_D), F32),
        ],
        compiler_params=pltpu.CompilerParams(
            dimension_semantics=("parallel", "arbitrary"), vmem_limit_bytes=VMEM_LIMIT),
        name="gdn",
    )(proj, proj, small, conv_w, a_log, dt_bias, norm_w)


def _hgrn_kernel(q_ref, f_ref, i_ref, g_ref, lb_ref, nw_ref, o_ref,
                 q_s, k_s, lf_s, i_s, o_s, st_ref, *, tb, layer):
    @pl.when(pl.program_id(1) == 0)
    def _():
        st_ref[...] = jnp.zeros(st_ref.shape, F32)

    lbl = lb_ref[...]
    e_lb = jnp.exp(lbl - jnp.max(lbl, axis=0, keepdims=True))
    lb = jnp.sum(e_lb[:layer + 1], axis=0, keepdims=True) / jnp.sum(e_lb, axis=0, keepdims=True)

    f = lb + (1.0 - lb) * _sigmoid(f_ref[0].astype(F32))
    k_s[...] = 1.0 - f
    lf_s[...] = jnp.log(f)
    q_s[...] = _silu(q_ref[0].astype(F32)) * (HEAD_D ** -0.5)
    i_s[...] = i_ref[0].astype(F32)

    c = CHUNK
    row = _iota2((c, c), 0)
    col = _iota2((c, c), 1)
    ltri = (row >= col).astype(BF16)
    level_masks = {}
    for m in (32, 16, 8):
        sh = int(math.log2(2 * m))
        level_masks[m] = jnp.logical_and(
            (row >> sh) == (col >> sh),
            jnp.logical_and((row & (2 * m - 1)) >= m, (col & (2 * m - 1)) < m))
    sub = _iota2((8, HEAD_D), 0)

    def chunk_body(ci, carry):
        r0 = pl.multiple_of(ci * c, c)
        b_all = _mm_01(ltri, lf_s[pl.ds(r0, c), :])
        for h in range(N_HEADS):
            sl = slice(h * HEAD_D, (h + 1) * HEAD_D)
            b = b_all[:, sl]
            q = q_s[pl.ds(r0, c), sl]
            k = k_s[pl.ds(r0, c), sl]
            iv = i_s[pl.ds(r0, c), sl]
            attn = jnp.zeros((c, c), F32)
            for m in (32, 16, 8):
                ref = jnp.concatenate(
                    [jnp.broadcast_to(b[r:r + 1, :], (2 * m, HEAD_D)) for r in range(m, c, 2 * m)], axis=0)
                e = jnp.exp(-jnp.abs(b - ref))
                attn = attn + jnp.where(level_masks[m], _mm_nt(q * e, k * e), 0.0)
            s_t = st_ref[h]
            o = _mm_nt(q * jnp.exp(b), s_t) + _mm(attn, iv)
            diag = []
            for r in range(0, c, 8):
                qb, kb, bb, ib = q[r:r + 8], k[r:r + 8], b[r:r + 8], iv[r:r + 8]
                ob = jnp.zeros((8, HEAD_D), F32)
                for s in range(8):
                    dec = jnp.exp(jnp.where(sub >= s, bb - bb[s:s + 1, :], -jnp.inf))
                    a_col = jnp.sum(dec * qb * kb[s:s + 1, :], axis=-1, keepdims=True)
                    ob = ob + a_col * ib[s:s + 1, :]
                diag.append(ob)
            o = o + jnp.concatenate(diag, axis=0)
            b_last = b[c - 1:c, :]
            st_ref[h] = s_t * jnp.exp(b_last) + _mm_tn(iv, k * jnp.exp(b_last - b))
            o_s[pl.ds(r0, c), sl] = o
        return carry

    lax.fori_loop(0, tb // c, chunk_body, 0)
    o_ref[0] = _head_rms_gate(o_s[...], g_ref[0].astype(F32), nw_ref[...], HEAD_D).astype(o_ref.dtype)


def _hgrn(proj, lb_logits, norm_w, *, tb, layer, col0):
    b, t, _ = proj.shape
    w = N_HEADS * HEAD_D
    specs = [pl.BlockSpec((1, tb, w), functools.partial(lambda i, j, off: (i, j, off), off=col0 + n))
             for n in range(4)]
    return pl.pallas_call(
        functools.partial(_hgrn_kernel, tb=tb, layer=layer),
        grid=(b, t // tb),
        in_specs=specs + [
            pl.BlockSpec(lb_logits.shape, lambda i, j: (0, 0)),
            pl.BlockSpec((1, HEAD_D), lambda i, j: (0, 0)),
        ],
        out_specs=pl.BlockSpec((1, tb, w), lambda i, j: (i, j, 0)),
        out_shape=jax.ShapeDtypeStruct((b, t, w), BF16),
        scratch_shapes=[pltpu.VMEM((tb, w), F32)] * 5 + [pltpu.VMEM((N_HEADS, HEAD_D, HEAD_D), F32)],
        compiler_params=pltpu.CompilerParams(
            dimension_semantics=("parallel", "arbitrary"), vmem_limit_bytes=VMEM_LIMIT),
        name="hgrn2",
    )(proj, proj, proj, proj, lb_logits, norm_w)


def _ret_kernel(q_ref, k_ref, v_ref, g_ref, cos_ref, sin_ref, nw_ref, o_ref, dmat_s, st_ref, *, tc):
    hh = pl.program_id(1).astype(F32)
    log_gamma = jnp.log(1.0 - jnp.exp((jnp.full((1, 1), -5.0, F32) - hh) * math.log(2.0)))

    @pl.when(pl.program_id(2) == 0)
    def _():
        st_ref[...] = jnp.zeros(st_ref.shape, F32)
        rel = (_iota2((tc, tc), 0) - _iota2((tc, tc), 1)).astype(F32)
        dmat_s[...] = jnp.where(rel >= 0, jnp.exp(jnp.maximum(rel, 0.0) * log_gamma), 0.0)

    pos = _iota2((tc, 1), 0).astype(F32)
    q_decay = jnp.exp((pos + 1.0) * log_gamma)
    k_decay = jnp.exp((tc - 1.0 - pos) * log_gamma)
    chunk_decay = jnp.exp(tc * log_gamma)

    cos = cos_ref[...]
    sin = sin_ref[...]
    half = RET_DK // 2

    def rot(x):
        x1, x2 = x[:, :half], x[:, half:]
        return jnp.concatenate([x1 * cos - x2 * sin, x1 * sin + x2 * cos], axis=1)

    q = rot(q_ref[0].astype(F32))
    k = rot(k_ref[0].astype(F32)) * (RET_DK ** -0.5)
    v = v_ref[0]
    s = st_ref[...]
    attn = _mm_nt(q, k) * dmat_s[...]
    o = _mm(q, s) * q_decay + _mm(attn, v)
    st_ref[...] = s * chunk_decay + _mm_tn(k * k_decay, v)
    ms = jnp.mean(o * o, axis=-1, keepdims=True)
    o_ref[0] = (o * lax.rsqrt(ms + RMS_EPS) * nw_ref[...] * _silu(g_ref[0].astype(F32))).astype(o_ref.dtype)


def _retention(proj, cos, sin, norm_w, *, tc):
    b, t, _ = proj.shape
    qb = N_HEADS
    vb = 2 * N_HEADS * RET_DK // RET_DV
    gb = vb + N_HEADS
    return pl.pallas_call(
        functools.partial(_ret_kernel, tc=tc),
        grid=(b, N_HEADS, t // tc),
        in_specs=[
            pl.BlockSpec((1, tc, RET_DK), lambda i, h, j: (i, j, h)),
            pl.BlockSpec((1, tc, RET_DK), lambda i, h, j: (i, j, qb + h)),
            pl.BlockSpec((1, tc, RET_DV), lambda i, h, j: (i, j, vb + h)),
            pl.BlockSpec((1, tc, RET_DV), lambda i, h, j: (i, j, gb + h)),
            pl.BlockSpec((tc, RET_DK // 2), lambda i, h, j: (j, 0)),
            pl.BlockSpec((tc, RET_DK // 2), lambda i, h, j: (j, 0)),
            pl.BlockSpec((1, RET_DV), lambda i, h, j: (0, 0)),
        ],
        out_specs=pl.BlockSpec((1, tc, RET_DV), lambda i, h, j: (i, j, h)),
        out_shape=jax.ShapeDtypeStruct((b, t, N_HEADS * RET_DV), BF16),
        scratch_shapes=[pltpu.VMEM((tc, tc), F32), pltpu.VMEM((RET_DK, RET_DV), F32)],
        compiler_params=pltpu.CompilerParams(
            dimension_semantics=("parallel", "parallel", "arbitrary"), vmem_limit_bytes=VMEM_LIMIT),
        name="retention",
    )(proj, proj, proj, proj, cos, sin, norm_w)


def _route_gates(logits_t):
    cl = [logits_t[g:g + 1, :] for g in range(N_GROUPS)]
    cmax = functools.reduce(jnp.maximum, cl)
    denom = sum(jnp.exp(x - cmax) for x in cl)
    g_prob = 1.0 / denom
    g_idx = jnp.full(cmax.shape, N_GROUPS - 1, jnp.int32)
    for g in range(N_GROUPS - 2, -1, -1):
        g_idx = jnp.where(cl[g] == cmax, g, g_idx)
    def fine_row(g, j):
        r = N_GROUPS + g * EXPERTS_PER_GROUP + j
        return logits_t[r:r + 1, :]

    fl = []
    for j in range(EXPERTS_PER_GROUP):
        x = fine_row(N_GROUPS - 1, j)
        for g in range(N_GROUPS - 2, -1, -1):
            x = jnp.where(g_idx == g, fine_row(g, j), x)
        fl.append(x)
    m1 = functools.reduce(jnp.maximum, fl)
    i1 = jnp.full(m1.shape, EXPERTS_PER_GROUP - 1, jnp.int32)
    for j in range(EXPERTS_PER_GROUP - 2, -1, -1):
        i1 = jnp.where(fl[j] == m1, j, i1)
    rest = [jnp.where(i1 == j, -jnp.inf, fl[j]) for j in range(EXPERTS_PER_GROUP)]
    m2 = functools.reduce(jnp.maximum, rest)
    i2 = jnp.full(m2.shape, EXPERTS_PER_GROUP - 1, jnp.int32)
    for j in range(EXPERTS_PER_GROUP - 2, -1, -1):
        i2 = jnp.where(jnp.logical_and(rest[j] == m2, i1 != j), j, i2)
    e2 = jnp.exp(m2 - m1)
    w1 = g_prob / (1.0 + e2)
    w2 = g_prob * e2 / (1.0 + e2)
    id1 = g_idx * EXPERTS_PER_GROUP + i1
    id2 = g_idx * EXPERTS_PER_GROUP + i2
    eidx = _iota2((N_EXPERTS, cmax.shape[1]), 0)
    return jnp.where(eidx == id1, w1, 0.0) + jnp.where(eidx == id2, w2, 0.0)


def _outproj_kernel(a_ref, b_ref, wa_ref, wb_ref, x_ref, nw_ref, rw_ref, rb_ref,
                    x1_ref, h_ref, gates_ref):
    x1 = (x_ref[...]
          + jnp.dot(a_ref[...], wa_ref[...], preferred_element_type=F32)
          + jnp.dot(b_ref[...], wb_ref[...], preferred_element_type=F32))
    x1_ref[...] = x1
    ms = jnp.mean(x1 * x1, axis=-1, keepdims=True)
    h = x1 * lax.rsqrt(ms + RMS_EPS) * nw_ref[...]
    h_ref[...] = h.astype(BF16)
    rw = rw_ref[...]
    hh = h.astype(BF16)
    hl = (h - hh.astype(F32)).astype(BF16)
    wh = rw.astype(BF16)
    wl = (rw - wh.astype(F32)).astype(BF16)
    logits = (jnp.dot(hh, wh, preferred_element_type=F32)
              + jnp.dot(hl, wh, preferred_element_type=F32)
              + jnp.dot(hh, wl, preferred_element_type=F32)) + rb_ref[...]
    gates_t = _route_gates(logits.T)
    gates_pad = jnp.concatenate(
        [gates_t, jnp.zeros((LANES - N_EXPERTS, gates_t.shape[1]), F32)], axis=0)
    gates_ref[...] = gates_pad.T


def _outproj_route(a, b, a_blk, b_blk, wa, wb, x, nw, rw, rb, *, tm, name):
    n, d = x.shape
    ka = wa.shape[0]
    kb = wb.shape[0]
    return pl.pallas_call(
        _outproj_kernel,
        grid=(n // tm,),
        in_specs=[
            pl.BlockSpec((tm, ka), lambda i: (i, a_blk)),
            pl.BlockSpec((tm, kb), lambda i: (i, b_blk)),
            pl.BlockSpec((ka, d), lambda i: (0, 0)),
            pl.BlockSpec((kb, d), lambda i: (0, 0)),
            pl.BlockSpec((tm, d), lambda i: (i, 0)),
            pl.BlockSpec((1, d), lambda i: (0, 0)),
            pl.BlockSpec((d, LANES), lambda i: (0, 0)),
            pl.BlockSpec((1, LANES), lambda i: (0, 0)),
        ],
        out_specs=[
            pl.BlockSpec((tm, d), lambda i: (i, 0)),
            pl.BlockSpec((tm, d), lambda i: (i, 0)),
            pl.BlockSpec((tm, LANES), lambda i: (i, 0)),
        ],
        out_shape=[jax.ShapeDtypeStruct((n, d), F32),
                   jax.ShapeDtypeStruct((n, d), BF16),
                   jax.ShapeDtypeStruct((n, LANES), F32)],
        compiler_params=pltpu.CompilerParams(
            dimension_semantics=("parallel",), vmem_limit_bytes=VMEM_LIMIT),
        name=name,
    )(a, b, wa, wb, x, nw, rw, rb)


def _moe_kernel(h_ref, gates_ref, wgu_ref, wd_ref, x_ref, fnw_ref, o_ref, acc_ref, *, final_norm):
    e = pl.program_id(1)

    @pl.when(e == 0)
    def _():
        acc_ref[...] = x_ref[...]

    gu = jnp.dot(h_ref[...], wgu_ref[0], preferred_element_type=F32)
    gates = gates_ref[...]
    gate_col = jnp.sum(jnp.where(_iota2(gates.shape, 1) == e, gates, 0.0), axis=-1, keepdims=True)
    act = _silu(gu[:, :D_EXPERT]) * gu[:, D_EXPERT:] * gate_col
    acc_ref[...] += jnp.dot(act.astype(BF16), wd_ref[0], preferred_element_type=F32)

    @pl.when(e == pl.num_programs(1) - 1)
    def _():
        y = acc_ref[...]
        if final_norm:
            ms = jnp.mean(y * y, axis=-1, keepdims=True)
            y = y * lax.rsqrt(ms + RMS_EPS) * fnw_ref[...]
        o_ref[...] = y


def _moe(h, gates, wgu, wd, x, fnw, *, tm, final_norm, name):
    n, d = x.shape
    return pl.pallas_call(
        functools.partial(_moe_kernel, final_norm=final_norm),
        grid=(n // tm, N_EXPERTS),
        in_specs=[
            pl.BlockSpec((tm, d), lambda i, e: (i, 0)),
            pl.BlockSpec((tm, LANES), lambda i, e: (i, 0)),
            pl.BlockSpec((1, d, 2 * D_EXPERT), lambda i, e: (e, 0, 0)),
            pl.BlockSpec((1, D_EXPERT, d), lambda i, e: (e, 0, 0)),
            pl.BlockSpec((tm, d), lambda i, e: (i, 0)),
            pl.BlockSpec((1, d), lambda i, e: (0, 0)),
        ],
        out_specs=pl.BlockSpec((tm, d), lambda i, e: (i, 0)),
        out_shape=jax.ShapeDtypeStruct((n, d), F32),
        scratch_shapes=[pltpu.VMEM((tm, d), F32)],
        compiler_params=pltpu.CompilerParams(
            dimension_semantics=("parallel", "arbitrary"), vmem_limit_bytes=VMEM_LIMIT),
        name=name,
    )(h, gates, wgu, wd, x, fnw)


def _pad_cols(a, width):
    return jnp.pad(a, ((0, 0), (0, width - a.shape[1])))


def _router_params(wc, bc, wf, bf):
    rw = _pad_cols(jnp.concatenate([wc, wf], axis=1), LANES)
    rb = _pad_cols(jnp.concatenate([bc, bf])[None, :], LANES)
    return rw, rb


def kernel(x, norm_mix_w, norm_ffn_w, even_w_in, gdn_conv_w, gdn_a_log, gdn_dt_bias, gdn_norm_w, hgrn_lb_logits, hgrn_norm_w, even_w_out, odd_w_in, ret_norm_w, odd_w_out, router_c_w, router_c_b, router_f_w, router_f_b, moe_w_gate_up, moe_w_down, final_norm_w):
    bsz, seq, d = x.shape
    n = bsz * seq
    xt = x.reshape(n, d)
    mix_w = N_HEADS * HEAD_D
    conv_cols = 3 * mix_w
    gdn_main = conv_cols + mix_w

    w_in = even_w_in[0]
    small0 = gdn_main
    w_main = jnp.concatenate([w_in[:, :small0], w_in[:, small0 + 2 * N_HEADS:]], axis=1).astype(BF16)
    w_small = jnp.concatenate(
        [_pad_cols(w_in[:, small0:small0 + N_HEADS], LANES),
         _pad_cols(w_in[:, small0 + N_HEADS:small0 + 2 * N_HEADS], LANES)], axis=1).astype(BF16)
    proj, small = _norm_matmul(xt, norm_mix_w[0][None, :], w_main, w_small, tm=512, tn=1024, name="in_proj_even")
    proj = proj.reshape(bsz, seq, -1)
    small = small.reshape(bsz, seq, -1)
    o_a = _gdn(proj, small, gdn_conv_w[0], _pad_cols(gdn_a_log[0][None, :], LANES),
               _pad_cols(gdn_dt_bias[0][None, :], LANES), gdn_norm_w[0][None, :], tb=512)
    o_b = _hgrn(proj, hgrn_lb_logits, hgrn_norm_w[0][None, :], tb=512, layer=0, col0=gdn_main // mix_w)
    w_out = even_w_out[0].astype(BF16)
    rw, rb = _router_params(router_c_w[0], router_c_b[0], router_f_w[0], router_f_b[0])
    x1, h, gates = _outproj_route(o_a.reshape(n, mix_w), o_b.reshape(n, mix_w), 0, 0,
                                  w_out[:mix_w], w_out[mix_w:], xt, norm_ffn_w[0][None, :], rw, rb,
                                  tm=512, name="out_proj_even")
    x2 = _moe(h, gates, moe_w_gate_up[0].astype(BF16), moe_w_down[0].astype(BF16), x1,
              final_norm_w[None, :], tm=1024, final_norm=False, name="moe0")

    perm = np.concatenate([np.arange(0, RET_DK, 2), np.arange(1, RET_DK, 2)])
    qk_perm = np.concatenate([hd * RET_DK + perm for hd in range(2 * N_HEADS)])
    w_odd = odd_w_in[0]
    w_odd = jnp.concatenate([w_odd[:, qk_perm], w_odd[:, 2 * N_HEADS * RET_DK:]], axis=1).astype(BF16)
    proj1, _ = _norm_matmul(x2, norm_mix_w[1][None, :], w_odd, jnp.zeros((d, LANES), BF16),
                            tm=512, tn=1024, name="in_proj_odd")
    inv = 1.0 / (ROPE_BASE ** jnp.linspace(0.0, 1.0, RET_DK // 2, dtype=F32))
    ang = jnp.arange(seq, dtype=F32)[:, None] * inv[None, :]
    o_c = _retention(proj1.reshape(bsz, seq, -1), jnp.cos(ang), jnp.sin(ang), ret_norm_w[0][None, :],
                     tc=RET_CHUNK)
    o_c = o_c.reshape(n, -1)
    w_out1 = odd_w_out[0].astype(BF16)
    half = w_out1.shape[0] // 2
    rw, rb = _router_params(router_c_w[1], router_c_b[1], router_f_w[1], router_f_b[1])
    x3, h, gates = _outproj_route(o_c, o_c, 0, 1, w_out1[:half], w_out1[half:], x2,
                                  norm_ffn_w[1][None, :], rw, rb, tm=512, name="out_proj_odd")
    out = _moe(h, gates, moe_w_gate_up[1].astype(BF16), moe_w_down[1].astype(BF16), x3,
               final_norm_w[None, :], tm=1024, final_norm=True, name="moe1")
    return out.reshape(bsz, seq, d)
```

```python
import functools
import math

import numpy as np
import jax
import jax.numpy as jnp
from jax import lax
from jax.experimental import pallas as pl
from jax.experimental.pallas import tpu as pltpu

F32 = jnp.float32
BF16 = jnp.bfloat16

D_MODEL = 1024
RMS_EPS = 1e-6
L2_EPS = 1e-6
CHUNK = 64
CONV_K = 4
N_HEADS = 4
HEAD_D = 128
RET_DK = 256
RET_DV = 512
RET_CHUNK = 256
ROPE_BASE = 10000.0
N_GROUPS = 4
EXPERTS_PER_GROUP = 4
N_EXPERTS = 16
D_EXPERT = 256
LANES = 128
VMEM_LIMIT = 56 * 1024 * 1024


def _mm(a, b):
    return jnp.dot(a.astype(BF16), b.astype(BF16), preferred_element_type=F32)


def _mm_nt(a, b):
    return lax.dot_general(a.astype(BF16), b.astype(BF16), (((1,), (1,)), ((), ())),
                           preferred_element_type=F32)


def _mm_tn(a, b):
    return _mm(a.T, b)


def _mm_01(m01, x):
    hi = x.astype(BF16)
    lo = (x - hi.astype(F32)).astype(BF16)
    return (jnp.dot(m01, hi, preferred_element_type=F32)
            + jnp.dot(m01, lo, preferred_element_type=F32))


def _bmm(a, b):
    return jnp.einsum('cik,ckj->cij', a.astype(BF16), b.astype(BF16), preferred_element_type=F32)


def _bmm_nt(a, b):
    return jnp.einsum('cik,cjk->cij', a.astype(BF16), b.astype(BF16), preferred_element_type=F32)


def _bmm_tn(a, b):
    return _bmm(jnp.swapaxes(a, 1, 2), b)


def _bmm_01(m01, x):
    hi = x.astype(BF16)
    lo = (x - hi.astype(F32)).astype(BF16)
    return (jnp.einsum('cik,ckj->cij', m01, hi, preferred_element_type=F32)
            + jnp.einsum('cik,ckj->cij', m01, lo, preferred_element_type=F32))


def _sigmoid(x):
    return 1.0 / (1.0 + jnp.exp(-x))


def _silu(x):
    return x * _sigmoid(x)


def _softplus(x):
    return jnp.maximum(x, 0.0) + jnp.log(1.0 + jnp.exp(-jnp.abs(x)))


def _iota2(shape, dim):
    return lax.broadcasted_iota(jnp.int32, shape, dim)


def _head_rms_gate(o, gate, nw, width):
    outs = []
    for h in range(o.shape[1] // width):
        oh = o[:, h * width:(h + 1) * width]
        ms = jnp.mean(oh * oh, axis=-1, keepdims=True)
        outs.append(oh * lax.rsqrt(ms + RMS_EPS) * nw)
    return jnp.concatenate(outs, axis=1) * _silu(gate)


def _norm_mm_kernel(*refs, tn, has_small):
    if has_small:
        x_ref, nw_ref, w_ref, ws_ref, o_ref, os_ref = refs
    else:
        x_ref, nw_ref, w_ref, o_ref = refs
    x = x_ref[...]
    ms = jnp.mean(x * x, axis=-1, keepdims=True)
    hb = (x * lax.rsqrt(ms + RMS_EPS) * nw_ref[...]).astype(BF16)
    if has_small:
        os_ref[...] = jnp.dot(hb, ws_ref[...], preferred_element_type=F32)
    for j in range(w_ref.shape[1] // tn):
        cols = slice(j * tn, (j + 1) * tn)
        o_ref[:, cols] = jnp.dot(hb, w_ref[:, cols], preferred_element_type=F32).astype(o_ref.dtype)


def _norm_matmul(x, nw, w, ws, *, tm, tn, name):
    n, d = x.shape
    nout = w.shape[1]
    has_small = ws is not None
    in_specs = [
        pl.BlockSpec((tm, d), lambda i: (i, 0)),
        pl.BlockSpec((1, d), lambda i: (0, 0)),
        pl.BlockSpec((d, nout), lambda i: (0, 0)),
    ]
    out_specs = [pl.BlockSpec((tm, nout), lambda i: (i, 0))]
    out_shape = [jax.ShapeDtypeStruct((n, nout), BF16)]
    args = [x, nw, w]
    if has_small:
        nsmall = ws.shape[1]
        in_specs.append(pl.BlockSpec((d, nsmall), lambda i: (0, 0)))
        out_specs.append(pl.BlockSpec((tm, nsmall), lambda i: (i, 0)))
        out_shape.append(jax.ShapeDtypeStruct((n, nsmall), F32))
        args.append(ws)
    return pl.pallas_call(
        functools.partial(_norm_mm_kernel, tn=tn, has_small=has_small),
        grid=(n // tm,),
        in_specs=in_specs,
        out_specs=out_specs,
        out_shape=out_shape,
        compiler_params=pltpu.CompilerParams(
            dimension_semantics=("parallel",), vmem_limit_bytes=VMEM_LIMIT),
        name=name,
    )(*args)


def _gdn_kernel(qkv_ref, z_ref, sm_ref, cw_ref, alog_ref, dtb_ref, nw_ref, o_ref,
                xs_ref, qkv_s, bg_s, o_s, m_s, sq_s, qe_s, dec_s, st_ref, *, tb):
    @pl.when(pl.program_id(1) == 0)
    def _():
        xs_ref[0:8, :] = jnp.zeros((8, xs_ref.shape[1]), F32)
        st_ref[...] = jnp.zeros(st_ref.shape, F32)

    x = qkv_ref[0].astype(F32)
    xs_ref[8:8 + tb, :] = x
    cw = cw_ref[...]
    y = x * cw[CONV_K - 1:CONV_K, :]
    for j in range(1, CONV_K):
        y = y + xs_ref[pl.ds(8 - j, tb), :] * cw[CONV_K - 1 - j:CONV_K - j, :]
    xs_ref[0:8, :] = xs_ref[tb:tb + 8, :]
    y = _silu(y)

    qk_w = N_HEADS * HEAD_D
    for h in range(N_HEADS):
        for base, scale in ((0, HEAD_D ** -0.5), (qk_w, 1.0)):
            c0 = base + h * HEAD_D
            v = y[:, c0:c0 + HEAD_D]
            inv = lax.rsqrt(jnp.sum(v * v, axis=-1, keepdims=True) + L2_EPS)
            qkv_s[:, c0:c0 + HEAD_D] = v * (inv * scale)
    qkv_s[:, 2 * qk_w:] = y[:, 2 * qk_w:]

    sm = sm_ref[0]
    bg_s[0] = _sigmoid(sm[:, :LANES])
    bg_s[1] = -jnp.exp(alog_ref[...]) * _softplus(sm[:, LANES:] + dtb_ref[...])

    c = CHUNK
    nc = tb // c
    row = _iota2((c, c), 0)
    col = _iota2((c, c), 1)
    causal = (row >= col)[None]
    strict = (row > col)[None]
    same16 = ((row >> 4) == (col >> 4))[None]
    same32 = ((row >> 5) == (col >> 5))[None]
    eye = (row == col).astype(F32)[None]
    ltri = jnp.broadcast_to((row >= col).astype(BF16)[None], (nc, c, c))
    lane = _iota2((nc, c, LANES), 2)

    gc_all = _bmm_01(ltri, bg_s[1].reshape(nc, c, LANES))
    beta_all = bg_s[0].reshape(nc, c, LANES)
    for h in range(N_HEADS):
        hs = slice(h * HEAD_D, (h + 1) * HEAD_D)
        q = qkv_s[:, h * HEAD_D:(h + 1) * HEAD_D].reshape(nc, c, HEAD_D)
        k = qkv_s[:, qk_w + h * HEAD_D:qk_w + (h + 1) * HEAD_D].reshape(nc, c, HEAD_D)
        v = qkv_s[:, 2 * qk_w + h * HEAD_D:2 * qk_w + (h + 1) * HEAD_D].reshape(nc, c, HEAD_D)
        beta = beta_all[:, :, h:h + 1]
        gcol = gc_all[:, :, h:h + 1]
        g_hi = gcol.astype(BF16).astype(F32)
        g_mid = (gcol - g_hi).astype(BF16).astype(F32)
        g_lo = gcol - g_hi - g_mid
        ones_hi = jnp.where(lane < 6, 1.0, 0.0)
        lhs = jnp.where(lane == 0, g_hi, jnp.where(lane == 1, g_mid, jnp.where(lane == 2, g_lo, ones_hi)))
        rhs_g = jnp.where(lane == 3, -g_hi, jnp.where(lane == 4, -g_mid, jnp.where(lane == 5, -g_lo, ones_hi)))
        decay = jnp.exp(jnp.where(causal, _bmm_nt(lhs, rhs_g), -jnp.inf))
        kb = k * beta
        a = jnp.where(strict, _bmm_nt(kb, k) * decay, 0.0)
        d = jnp.where(same16, a, 0.0)
        x_inv = eye - d
        dp = d
        for _ in range(3):
            dp = _bmm(dp, dp)
            x_inv = x_inv + _bmm(x_inv, dp)
        e = jnp.where(jnp.logical_and(same32, jnp.logical_not(same16)), a, 0.0)
        x_inv = x_inv - _bmm(x_inv, _bmm(e, x_inv))
        f = jnp.where(same32, 0.0, a)
        x_inv = x_inv - _bmm(x_inv, _bmm(f, x_inv))
        wu = _bmm(x_inv, jnp.concatenate([kb * jnp.exp(gcol), v * beta], axis=2))
        attn = _bmm_nt(q, k) * decay
        g_last = gcol[:, c - 1:c, :]
        kd = k * jnp.exp(g_last - gcol)
        mq = _bmm_tn(kd, wu)
        aw = _bmm(attn, wu)
        m_s[h] = mq[:, :, :HEAD_D].astype(BF16)
        sq_s[h] = mq[:, :, HEAD_D:]
        qe_s[h] = (q * jnp.exp(gcol) - aw[:, :, :HEAD_D]).astype(BF16)
        dec_s[h] = jnp.broadcast_to(jnp.exp(g_last), (nc, 1, HEAD_D))
        o_s[:, hs] = aw[:, :, HEAD_D:].reshape(tb, HEAD_D)

    for ci in range(nc):
        for h in range(N_HEADS):
            hs = slice(h * HEAD_D, (h + 1) * HEAD_D)
            s_h = st_ref[h]
            s_b = s_h.astype(BF16)
            o_s[ci * c:(ci + 1) * c, hs] += jnp.dot(qe_s[h, ci], s_b, preferred_element_type=F32)
            st_ref[h] = (s_h * dec_s[h, ci] - jnp.dot(m_s[h, ci], s_b, preferred_element_type=F32)
                         + sq_s[h, ci])

    o_ref[0] = _head_rms_gate(o_s[...], z_ref[0].astype(F32), nw_ref[...], HEAD_D).astype(o_ref.dtype)


def _gdn(proj, small, conv_w, a_log, dt_bias, norm_w, *, tb):
    b, t, _ = proj.shape
    conv_cols = 3 * N_HEADS * HEAD_D
    mix_w = N_HEADS * HEAD_D
    return pl.pallas_call(
        functools.partial(_gdn_kernel, tb=tb),
        grid=(b, t // tb),
        in_specs=[
            pl.BlockSpec((1, tb, conv_cols), lambda i, j: (i, j, 0)),
            pl.BlockSpec((1, tb, mix_w), lambda i, j: (i, j, conv_cols // mix_w)),
            pl.BlockSpec((1, tb, 2 * LANES), lambda i, j: (i, j, 0)),
            pl.BlockSpec((CONV_K, conv_cols), lambda i, j: (0, 0)),
            pl.BlockSpec((1, LANES), lambda i, j: (0, 0)),
            pl.BlockSpec((1, LANES), lambda i, j: (0, 0)),
            pl.BlockSpec((1, HEAD_D), lambda i, j: (0, 0)),
        ],
        out_specs=pl.BlockSpec((1, tb, mix_w), lambda i, j: (i, j, 0)),
        out_shape=jax.ShapeDtypeStruct((b, t, mix_w), BF16),
        scratch_shapes=[
            pltpu.VMEM((tb + 8, conv_cols), F32),
            pltpu.VMEM((tb, conv_cols), F32),
            pltpu.VMEM((2, tb, LANES), F32),
            pltpu.VMEM((tb, mix_w), F32),
            pltpu.VMEM((N_HEADS, tb // CHUNK, HEAD_D, HEAD_D), BF16),
            pltpu.VMEM((N_HEADS, tb // CHUNK, HEAD_D, HEAD_D), F32),
            pltpu.VMEM((N_HEADS, tb // CHUNK, CHUNK, HEAD_D), BF16),
            pltpu.VMEM((N_HEADS, tb // CHUNK, 1, HEAD_D), F32),
            pltpu.VMEM((N_HEADS, HEAD_D, HEAD_D), F32),
        ],
        compiler_params=pltpu.CompilerParams(
            dimension_semantics=("parallel", "arbitrary"), vmem_limit_bytes=VMEM_LIMIT),
        name="gdn",
    )(proj, proj, small, conv_w, a_log, dt_bias, norm_w)


def _hgrn_kernel(q_ref, f_ref, i_ref, g_ref, lb_ref, nw_ref, o_ref,
                 q_s, k_s, lf_s, i_s, o_s, st_ref, *, tb, layer):
    @pl.when(pl.program_id(1) == 0)
    def _():
        st_ref[...] = jnp.zeros(st_ref.shape, F32)

    lbl = lb_ref[...]
    e_lb = jnp.exp(lbl - jnp.max(lbl, axis=0, keepdims=True))
    lb = jnp.sum(e_lb[:layer + 1], axis=0, keepdims=True) / jnp.sum(e_lb, axis=0, keepdims=True)

    f = lb + (1.0 - lb) * _sigmoid(f_ref[0].astype(F32))
    k_s[...] = 1.0 - f
    lf_s[...] = jnp.log(f)
    q_s[...] = _silu(q_ref[0].astype(F32)) * (HEAD_D ** -0.5)
    i_s[...] = i_ref[0].astype(F32)

    c = CHUNK
    row = _iota2((c, c), 0)
    col = _iota2((c, c), 1)
    ltri = (row >= col).astype(BF16)
    level_masks = {}
    for m in (32, 16, 8):
        sh = int(math.log2(2 * m))
        level_masks[m] = jnp.logical_and(
            (row >> sh) == (col >> sh),
            jnp.logical_and((row & (2 * m - 1)) >= m, (col & (2 * m - 1)) < m))
    sub = _iota2((8, HEAD_D), 0)

    def chunk_body(ci, carry):
        r0 = pl.multiple_of(ci * c, c)
        b_all = _mm_01(ltri, lf_s[pl.ds(r0, c), :])
        for h in range(N_HEADS):
            sl = slice(h * HEAD_D, (h + 1) * HEAD_D)
            b = b_all[:, sl]
            q = q_s[pl.ds(r0, c), sl]
            k = k_s[pl.ds(r0, c), sl]
            iv = i_s[pl.ds(r0, c), sl]
            attn = jnp.zeros((c, c), F32)
            for m in (32, 16, 8):
                ref = jnp.concatenate(
                    [jnp.broadcast_to(b[r:r + 1, :], (2 * m, HEAD_D)) for r in range(m, c, 2 * m)], axis=0)
                e = jnp.exp(-jnp.abs(b - ref))
                attn = attn + jnp.where(level_masks[m], _mm_nt(q * e, k * e), 0.0)
            s_t = st_ref[h]
            o = _mm_nt(q * jnp.exp(b), s_t) + _mm(attn, iv)
            diag = []
            for r in range(0, c, 8):
                qb, kb, bb, ib = q[r:r + 8], k[r:r + 8], b[r:r + 8], iv[r:r + 8]
                ob = jnp.zeros((8, HEAD_D), F32)
                for s in range(8):
                    dec = jnp.exp(jnp.where(sub >= s, bb - bb[s:s + 1, :], -jnp.inf))
                    a_col = jnp.sum(dec * qb * kb[s:s + 1, :], axis=-1, keepdims=True)
                    ob = ob + a_col * ib[s:s + 1, :]
                diag.append(ob)
            o = o + jnp.concatenate(diag, axis=0)
            b_last = b[c - 1:c, :]
            st_ref[h] = s_t * jnp.exp(b_last) + _mm_tn(iv, k * jnp.exp(b_last - b))
            o_s[pl.ds(r0, c), sl] = o
        return carry

    lax.fori_loop(0, tb // c, chunk_body, 0)
    o_ref[0] = _head_rms_gate(o_s[...], g_ref[0].astype(F32), nw_ref[...], HEAD_D).astype(o_ref.dtype)


def _hgrn(proj, lb_logits, norm_w, *, tb, layer, col0):
    b, t, _ = proj.shape
    w = N_HEADS * HEAD_D
    specs = [pl.BlockSpec((1, tb, w), functools.partial(lambda i, j, off: (i, j, off), off=col0 + n))
             for n in range(4)]
    return pl.pallas_call(
        functools.partial(_hgrn_kernel, tb=tb, layer=layer),
        grid=(b, t // tb),
        in_specs=specs + [
            pl.BlockSpec(lb_logits.shape, lambda i, j: (0, 0)),
            pl.BlockSpec((1, HEAD_D), lambda i, j: (0, 0)),
        ],
        out_specs=pl.BlockSpec((1, tb, w), lambda i, j: (i, j, 0)),
        out_shape=jax.ShapeDtypeStruct((b, t, w), BF16),
        scratch_shapes=[pltpu.VMEM((tb, w), F32)] * 5 + [pltpu.VMEM((N_HEADS, HEAD_D, HEAD_D), F32)],
        compiler_params=pltpu.CompilerParams(
            dimension_semantics=("parallel", "arbitrary"), vmem_limit_bytes=VMEM_LIMIT),
        name="hgrn2",
    )(proj, proj, proj, proj, lb_logits, norm_w)


def _ret_kernel(q_ref, k_ref, v_ref, g_ref, cos_ref, sin_ref, nw_ref, o_ref, dmat_s, st_ref, *, tc):
    hh = pl.program_id(1).astype(F32)
    log_gamma = jnp.log(1.0 - jnp.exp((jnp.full((1, 1), -5.0, F32) - hh) * math.log(2.0)))

    @pl.when(pl.program_id(2) == 0)
    def _():
        st_ref[...] = jnp.zeros(st_ref.shape, F32)
        rel = (_iota2((tc, tc), 0) - _iota2((tc, tc), 1)).astype(F32)
        dmat_s[...] = jnp.where(rel >= 0, jnp.exp(jnp.maximum(rel, 0.0) * log_gamma), 0.0)

    pos = _iota2((tc, 1), 0).astype(F32)
    q_decay = jnp.exp((pos + 1.0) * log_gamma)
    k_decay = jnp.exp((tc - 1.0 - pos) * log_gamma)
    chunk_decay = jnp.exp(tc * log_gamma)

    cos = cos_ref[...]
    sin = sin_ref[...]
    half = RET_DK // 2

    def rot(x):
        x1, x2 = x[:, :half], x[:, half:]
        return jnp.concatenate([x1 * cos - x2 * sin, x1 * sin + x2 * cos], axis=1)

    q = rot(q_ref[0].astype(F32))
    k = rot(k_ref[0].astype(F32)) * (RET_DK ** -0.5)
    v = v_ref[0]
    s = st_ref[...]
    attn = _mm_nt(q, k) * dmat_s[...]
    o = _mm(q, s) * q_decay + _mm(attn, v)
    st_ref[...] = s * chunk_decay + _mm_tn(k * k_decay, v)
    ms = jnp.mean(o * o, axis=-1, keepdims=True)
    o_ref[0] = (o * lax.rsqrt(ms + RMS_EPS) * nw_ref[...] * _silu(g_ref[0].astype(F32))).astype(o_ref.dtype)


def _retention(proj, cos, sin, norm_w, *, tc):
    b, t, _ = proj.shape
    qb = N_HEADS
    vb = 2 * N_HEADS * RET_DK // RET_DV
    gb = vb + N_HEADS
    return pl.pallas_call(
        functools.partial(_ret_kernel, tc=tc),
        grid=(b, N_HEADS, t // tc),
        in_specs=[
            pl.BlockSpec((1, tc, RET_DK), lambda i, h, j: (i, j, h)),
            pl.BlockSpec((1, tc, RET_DK), lambda i, h, j: (i, j, qb + h)),
            pl.BlockSpec((1, tc, RET_DV), lambda i, h, j: (i, j, vb + h)),
            pl.BlockSpec((1, tc, RET_DV), lambda i, h, j: (i, j, gb + h)),
            pl.BlockSpec((tc, RET_DK // 2), lambda i, h, j: (j, 0)),
            pl.BlockSpec((tc, RET_DK // 2), lambda i, h, j: (j, 0)),
            pl.BlockSpec((1, RET_DV), lambda i, h, j: (0, 0)),
        ],
        out_specs=pl.BlockSpec((1, tc, RET_DV), lambda i, h, j: (i, j, h)),
        out_shape=jax.ShapeDtypeStruct((b, t, N_HEADS * RET_DV), BF16),
        scratch_shapes=[pltpu.VMEM((tc, tc), F32), pltpu.VMEM((RET_DK, RET_DV), F32)],
        compiler_params=pltpu.CompilerParams(
            dimension_semantics=("parallel", "parallel", "arbitrary"), vmem_limit_bytes=VMEM_LIMIT),
        name="retention",
    )(proj, proj, proj, proj, cos, sin, norm_w)


def _route_gates(logits_t):
    cl = [logits_t[g:g + 1, :] for g in range(N_GROUPS)]
    cmax = functools.reduce(jnp.maximum, cl)
    denom = sum(jnp.exp(x - cmax) for x in cl)
    g_prob = 1.0 / denom
    g_idx = jnp.full(cmax.shape, N_GROUPS - 1, jnp.int32)
    for g in range(N_GROUPS - 2, -1, -1):
        g_idx = jnp.where(cl[g] == cmax, g, g_idx)
    def fine_row(g, j):
        r = N_GROUPS + g * EXPERTS_PER_GROUP + j
        return logits_t[r:r + 1, :]

    fl = []
    for j in range(EXPERTS_PER_GROUP):
        x = fine_row(N_GROUPS - 1, j)
        for g in range(N_GROUPS - 2, -1, -1):
            x = jnp.where(g_idx == g, fine_row(g, j), x)
        fl.append(x)
    m1 = functools.reduce(jnp.maximum, fl)
    i1 = jnp.full(m1.shape, EXPERTS_PER_GROUP - 1, jnp.int32)
    for j in range(EXPERTS_PER_GROUP - 2, -1, -1):
        i1 = jnp.where(fl[j] == m1, j, i1)
    rest = [jnp.where(i1 == j, -jnp.inf, fl[j]) for j in range(EXPERTS_PER_GROUP)]
    m2 = functools.reduce(jnp.maximum, rest)
    i2 = jnp.full(m2.shape, EXPERTS_PER_GROUP - 1, jnp.int32)
    for j in range(EXPERTS_PER_GROUP - 2, -1, -1):
        i2 = jnp.where(jnp.logical_and(rest[j] == m2, i1 != j), j, i2)
    e2 = jnp.exp(m2 - m1)
    w1 = g_prob / (1.0 + e2)
    w2 = g_prob * e2 / (1.0 + e2)
    id1 = g_idx * EXPERTS_PER_GROUP + i1
    id2 = g_idx * EXPERTS_PER_GROUP + i2
    eidx = _iota2((N_EXPERTS, cmax.shape[1]), 0)
    return jnp.where(eidx == id1, w1, 0.0) + jnp.where(eidx == id2, w2, 0.0)


def _outproj_kernel(a_ref, b_ref, wa_ref, wb_ref, x_ref, nw_ref, rw_ref, rb_ref,
                    x1_ref, h_ref, gates_ref):
    x1 = (x_ref[...]
          + jnp.dot(a_ref[...], wa_ref[...], preferred_element_type=F32)
          + jnp.dot(b_ref[...], wb_ref[...], preferred_element_type=F32))
    x1_ref[...] = x1
    ms = jnp.mean(x1 * x1, axis=-1, keepdims=True)
    h = x1 * lax.rsqrt(ms + RMS_EPS) * nw_ref[...]
    h_ref[...] = h.astype(BF16)
    rw = rw_ref[...]
    hh = h.astype(BF16)
    hl = (h - hh.astype(F32)).astype(BF16)
    wh = rw.astype(BF16)
    wl = (rw - wh.astype(F32)).astype(BF16)
    logits = (jnp.dot(hh, wh, preferred_element_type=F32)
              + jnp.dot(hl, wh, preferred_element_type=F32)
              + jnp.dot(hh, wl, preferred_element_type=F32)) + rb_ref[...]
    gates_t = _route_gates(logits.T)
    gates_pad = jnp.concatenate(
        [gates_t, jnp.zeros((LANES - N_EXPERTS, gates_t.shape[1]), F32)], axis=0)
    gates_ref[...] = gates_pad.T


def _outproj_route(a, b, a_blk, b_blk, wa, wb, x, nw, rw, rb, *, tm, name):
    n, d = x.shape
    ka = wa.shape[0]
    kb = wb.shape[0]
    return pl.pallas_call(
        _outproj_kernel,
        grid=(n // tm,),
        in_specs=[
            pl.BlockSpec((tm, ka), lambda i: (i, a_blk)),
            pl.BlockSpec((tm, kb), lambda i: (i, b_blk)),
            pl.BlockSpec((ka, d), lambda i: (0, 0)),
            pl.BlockSpec((kb, d), lambda i: (0, 0)),
            pl.BlockSpec((tm, d), lambda i: (i, 0)),
            pl.BlockSpec((1, d), lambda i: (0, 0)),
            pl.BlockSpec((d, LANES), lambda i: (0, 0)),
            pl.BlockSpec((1, LANES), lambda i: (0, 0)),
        ],
        out_specs=[
            pl.BlockSpec((tm, d), lambda i: (i, 0)),
            pl.BlockSpec((tm, d), lambda i: (i, 0)),
            pl.BlockSpec((tm, LANES), lambda i: (i, 0)),
        ],
        out_shape=[jax.ShapeDtypeStruct((n, d), F32),
                   jax.ShapeDtypeStruct((n, d), BF16),
                   jax.ShapeDtypeStruct((n, LANES), F32)],
        compiler_params=pltpu.CompilerParams(
            dimension_semantics=("parallel",), vmem_limit_bytes=VMEM_LIMIT),
        name=name,
    )(a, b, wa, wb, x, nw, rw, rb)


def _moe_kernel(h_ref, gates_ref, wgu_ref, wd_ref, x_ref, fnw_ref, o_ref, acc_ref, *, final_norm):
    e = pl.program_id(1)

    @pl.when(e == 0)
    def _():
        acc_ref[...] = x_ref[...]

    gu = jnp.dot(h_ref[...], wgu_ref[0, 0].astype(BF16), preferred_element_type=F32)
    gates = gates_ref[...]
    gate_col = jnp.sum(jnp.where(_iota2(gates.shape, 1) == e, gates, 0.0), axis=-1, keepdims=True)
    act = _silu(gu[:, :D_EXPERT]) * gu[:, D_EXPERT:] * gate_col
    acc_ref[...] += jnp.dot(act.astype(BF16), wd_ref[0, 0].astype(BF16), preferred_element_type=F32)

    @pl.when(e == pl.num_programs(1) - 1)
    def _():
        y = acc_ref[...]
        if final_norm:
            ms = jnp.mean(y * y, axis=-1, keepdims=True)
            y = y * lax.rsqrt(ms + RMS_EPS) * fnw_ref[...]
        o_ref[...] = y


def _moe(h, gates, wgu, wd, x, fnw, *, layer, tm, final_norm, name):
    n, d = x.shape
    return pl.pallas_call(
        functools.partial(_moe_kernel, final_norm=final_norm),
        grid=(n // tm, N_EXPERTS),
        in_specs=[
            pl.BlockSpec((tm, d), lambda i, e: (i, 0)),
            pl.BlockSpec((tm, LANES), lambda i, e: (i, 0)),
            pl.BlockSpec((1, 1, d, 2 * D_EXPERT), lambda i, e: (layer, e, 0, 0)),
            pl.BlockSpec((1, 1, D_EXPERT, d), lambda i, e: (layer, e, 0, 0)),
            pl.BlockSpec((tm, d), lambda i, e: (i, 0)),
            pl.BlockSpec((1, d), lambda i, e: (0, 0)),
        ],
        out_specs=pl.BlockSpec((tm, d), lambda i, e: (i, 0)),
        out_shape=jax.ShapeDtypeStruct((n, d), F32),
        scratch_shapes=[pltpu.VMEM((tm, d), F32)],
        compiler_params=pltpu.CompilerParams(
            dimension_semantics=("parallel", "arbitrary"), vmem_limit_bytes=VMEM_LIMIT),
        name=name,
    )(h, gates, wgu, wd, x, fnw)


def _pad_cols(a, width):
    return jnp.pad(a, ((0, 0), (0, width - a.shape[1])))


def _router_params(wc, bc, wf, bf):
    rw = _pad_cols(jnp.concatenate([wc, wf], axis=1), LANES)
    rb = _pad_cols(jnp.concatenate([bc, bf])[None, :], LANES)
    return rw, rb


def kernel(x, norm_mix_w, norm_ffn_w, even_w_in, gdn_conv_w, gdn_a_log, gdn_dt_bias, gdn_norm_w, hgrn_lb_logits, hgrn_norm_w, even_w_out, odd_w_in, ret_norm_w, odd_w_out, router_c_w, router_c_b, router_f_w, router_f_b, moe_w_gate_up, moe_w_down, final_norm_w):
    bsz, seq, d = x.shape
    n = bsz * seq
    xt = x.reshape(n, d)
    mix_w = N_HEADS * HEAD_D
    conv_cols = 3 * mix_w
    gdn_main = conv_cols + mix_w

    w_in = even_w_in[0]
    small0 = gdn_main
    w_main = jnp.concatenate([w_in[:, :small0], w_in[:, small0 + 2 * N_HEADS:]], axis=1).astype(BF16)
    w_small = jnp.concatenate(
        [_pad_cols(w_in[:, small0:small0 + N_HEADS], LANES),
         _pad_cols(w_in[:, small0 + N_HEADS:small0 + 2 * N_HEADS], LANES)], axis=1).astype(BF16)
    proj, small = _norm_matmul(xt, norm_mix_w[0][None, :], w_main, w_small, tm=512, tn=1024, name="in_proj_even")
    proj = proj.reshape(bsz, seq, -1)
    small = small.reshape(bsz, seq, -1)
    o_a = _gdn(proj, small, gdn_conv_w[0], _pad_cols(gdn_a_log[0][None, :], LANES),
               _pad_cols(gdn_dt_bias[0][None, :], LANES), gdn_norm_w[0][None, :], tb=512)
    o_b = _hgrn(proj, hgrn_lb_logits, hgrn_norm_w[0][None, :], tb=512, layer=0, col0=gdn_main // mix_w)
    w_out = even_w_out[0].astype(BF16)
    rw, rb = _router_params(router_c_w[0], router_c_b[0], router_f_w[0], router_f_b[0])
    x1, h, gates = _outproj_route(o_a.reshape(n, mix_w), o_b.reshape(n, mix_w), 0, 0,
                                  w_out[:mix_w], w_out[mix_w:], xt, norm_ffn_w[0][None, :], rw, rb,
                                  tm=512, name="out_proj_even")
    x2 = _moe(h, gates, moe_w_gate_up, moe_w_down, x1,
              final_norm_w[None, :], layer=0, tm=1024, final_norm=False, name="moe0")

    perm = np.concatenate([np.arange(0, RET_DK, 2), np.arange(1, RET_DK, 2)])
    qk_perm = np.concatenate([hd * RET_DK + perm for hd in range(2 * N_HEADS)])
    w_odd = odd_w_in[0]
    w_odd = jnp.concatenate([w_odd[:, qk_perm], w_odd[:, 2 * N_HEADS * RET_DK:]], axis=1).astype(BF16)
    (proj1,) = _norm_matmul(x2, norm_mix_w[1][None, :], w_odd, None, tm=512, tn=1024, name="in_proj_odd")
    inv = 1.0 / (ROPE_BASE ** jnp.linspace(0.0, 1.0, RET_DK // 2, dtype=F32))
    ang = jnp.arange(seq, dtype=F32)[:, None] * inv[None, :]
    o_c = _retention(proj1.reshape(bsz, seq, -1), jnp.cos(ang), jnp.sin(ang), ret_norm_w[0][None, :],
                     tc=RET_CHUNK)
    o_c = o_c.reshape(n, -1)
    w_out1 = odd_w_out[0].astype(BF16)
    half = w_out1.shape[0] // 2
    rw, rb = _router_params(router_c_w[1], router_c_b[1], router_f_w[1], router_f_b[1])
    x3, h, gates = _outproj_route(o_c, o_c, 0, 1, w_out1[:half], w_out1[half:], x2,
                                  norm_ffn_w[1][None, :], rw, rb, tm=512, name="out_proj_odd")
    out = _moe(h, gates, moe_w_gate_up, moe_w_down, x3,
               final_norm_w[None, :], layer=1, tm=1024, final_norm=True, name="moe1")
    return out.reshape(bsz, seq, d)
```

```python
import functools
import math

import numpy as np
import jax
import jax.numpy as jnp
from jax import lax
from jax.experimental import pallas as pl
from jax.experimental.pallas import tpu as pltpu

F32 = jnp.float32
BF16 = jnp.bfloat16

D_MODEL = 1024
RMS_EPS = 1e-6
L2_EPS = 1e-6
CHUNK = 64
CONV_K = 4
N_HEADS = 4
HEAD_D = 128
RET_DK = 256
RET_DV = 512
RET_CHUNK = 256
ROPE_BASE = 10000.0
N_GROUPS = 4
EXPERTS_PER_GROUP = 4
N_EXPERTS = 16
D_EXPERT = 256
LANES = 128
ROUTE_TM = 512
MOE_TM = 1024
MOE_SUB = 256
MOE_BLK = 128
MOE_ALIGN = 16
MOE_KEY_STRIDE = 4096.0
VMEM_LIMIT = 56 * 1024 * 1024


def _mm(a, b):
    return jnp.dot(a.astype(BF16), b.astype(BF16), preferred_element_type=F32)


def _mm_nt(a, b):
    return lax.dot_general(a.astype(BF16), b.astype(BF16), (((1,), (1,)), ((), ())),
                           preferred_element_type=F32)


def _mm_tn(a, b):
    return _mm(a.T, b)


def _mm_01(m01, x):
    hi = x.astype(BF16)
    lo = (x - hi.astype(F32)).astype(BF16)
    return (jnp.dot(m01, hi, preferred_element_type=F32)
            + jnp.dot(m01, lo, preferred_element_type=F32))


def _bmm(a, b):
    return jnp.einsum('cik,ckj->cij', a.astype(BF16), b.astype(BF16), preferred_element_type=F32)


def _bmm_nt(a, b):
    return jnp.einsum('cik,cjk->cij', a.astype(BF16), b.astype(BF16), preferred_element_type=F32)


def _bmm_tn(a, b):
    return _bmm(jnp.swapaxes(a, 1, 2), b)


def _bmm_01(m01, x):
    hi = x.astype(BF16)
    lo = (x - hi.astype(F32)).astype(BF16)
    return (jnp.einsum('cik,ckj->cij', m01, hi, preferred_element_type=F32)
            + jnp.einsum('cik,ckj->cij', m01, lo, preferred_element_type=F32))


def _sigmoid(x):
    return 1.0 / (1.0 + jnp.exp(-x))


def _silu(x):
    return x * _sigmoid(x)


def _softplus(x):
    return jnp.maximum(x, 0.0) + jnp.log(1.0 + jnp.exp(-jnp.abs(x)))


def _iota2(shape, dim):
    return lax.broadcasted_iota(jnp.int32, shape, dim)


def _head_rms_gate(o, gate, nw, width):
    outs = []
    for h in range(o.shape[1] // width):
        oh = o[:, h * width:(h + 1) * width]
        ms = jnp.mean(oh * oh, axis=-1, keepdims=True)
        outs.append(oh * lax.rsqrt(ms + RMS_EPS) * nw)
    return jnp.concatenate(outs, axis=1) * _silu(gate)


def _norm_mm_kernel(*refs, tn, has_small):
    if has_small:
        x_ref, nw_ref, w_ref, ws_ref, o_ref, os_ref = refs
    else:
        x_ref, nw_ref, w_ref, o_ref = refs
    x = x_ref[...]
    ms = jnp.mean(x * x, axis=-1, keepdims=True)
    hb = (x * lax.rsqrt(ms + RMS_EPS) * nw_ref[...]).astype(BF16)
    if has_small:
        os_ref[...] = jnp.dot(hb, ws_ref[...], preferred_element_type=F32)
    for j in range(w_ref.shape[1] // tn):
        cols = slice(j * tn, (j + 1) * tn)
        o_ref[:, cols] = jnp.dot(hb, w_ref[:, cols], preferred_element_type=F32).astype(o_ref.dtype)


def _norm_matmul(x, nw, w, ws, *, tm, tn, name):
    n, d = x.shape
    nout = w.shape[1]
    has_small = ws is not None
    in_specs = [
        pl.BlockSpec((tm, d), lambda i: (i, 0)),
        pl.BlockSpec((1, d), lambda i: (0, 0)),
        pl.BlockSpec((d, nout), lambda i: (0, 0)),
    ]
    out_specs = [pl.BlockSpec((tm, nout), lambda i: (i, 0))]
    out_shape = [jax.ShapeDtypeStruct((n, nout), BF16)]
    args = [x, nw, w]
    if has_small:
        nsmall = ws.shape[1]
        in_specs.append(pl.BlockSpec((d, nsmall), lambda i: (0, 0)))
        out_specs.append(pl.BlockSpec((tm, nsmall), lambda i: (i, 0)))
        out_shape.append(jax.ShapeDtypeStruct((n, nsmall), F32))
        args.append(ws)
    return pl.pallas_call(
        functools.partial(_norm_mm_kernel, tn=tn, has_small=has_small),
        grid=(n // tm,),
        in_specs=in_specs,
        out_specs=out_specs,
        out_shape=out_shape,
        compiler_params=pltpu.CompilerParams(
            dimension_semantics=("parallel",), vmem_limit_bytes=VMEM_LIMIT),
        name=name,
    )(*args)


def _gdn_kernel(qkv_ref, z_ref, sm_ref, cw_ref, alog_ref, dtb_ref, nw_ref, o_ref,
                xs_ref, qkv_s, bg_s, o_s, m_s, sq_s, qe_s, dec_s, st_ref, *, tb):
    @pl.when(pl.program_id(1) == 0)
    def _():
        xs_ref[0:8, :] = jnp.zeros((8, xs_ref.shape[1]), F32)
        st_ref[...] = jnp.zeros(st_ref.shape, F32)

    x = qkv_ref[0].astype(F32)
    xs_ref[8:8 + tb, :] = x
    cw = cw_ref[...]
    y = x * cw[CONV_K - 1:CONV_K, :]
    for j in range(1, CONV_K):
        y = y + xs_ref[pl.ds(8 - j, tb), :] * cw[CONV_K - 1 - j:CONV_K - j, :]
    xs_ref[0:8, :] = xs_ref[tb:tb + 8, :]
    y = _silu(y)

    qk_w = N_HEADS * HEAD_D
    for h in range(N_HEADS):
        for base, scale in ((0, HEAD_D ** -0.5), (qk_w, 1.0)):
            c0 = base + h * HEAD_D
            v = y[:, c0:c0 + HEAD_D]
            inv = lax.rsqrt(jnp.sum(v * v, axis=-1, keepdims=True) + L2_EPS)
            qkv_s[:, c0:c0 + HEAD_D] = v * (inv * scale)
    qkv_s[:, 2 * qk_w:] = y[:, 2 * qk_w:]

    sm = sm_ref[0]
    bg_s[0] = _sigmoid(sm[:, :LANES])
    bg_s[1] = -jnp.exp(alog_ref[...]) * _softplus(sm[:, LANES:] + dtb_ref[...])

    c = CHUNK
    nc = tb // c
    row = _iota2((c, c), 0)
    col = _iota2((c, c), 1)
    causal = (row >= col)[None]
    strict = (row > col)[None]
    same16 = ((row >> 4) == (col >> 4))[None]
    same32 = ((row >> 5) == (col >> 5))[None]
    eye = (row == col).astype(F32)[None]
    ltri = jnp.broadcast_to((row >= col).astype(BF16)[None], (nc, c, c))
    lane = _iota2((nc, c, LANES), 2)

    gc_all = _bmm_01(ltri, bg_s[1].reshape(nc, c, LANES))
    beta_all = bg_s[0].reshape(nc, c, LANES)
    for h in range(N_HEADS):
        hs = slice(h * HEAD_D, (h + 1) * HEAD_D)
        q = qkv_s[:, h * HEAD_D:(h + 1) * HEAD_D].reshape(nc, c, HEAD_D)
        k = qkv_s[:, qk_w + h * HEAD_D:qk_w + (h + 1) * HEAD_D].reshape(nc, c, HEAD_D)
        v = qkv_s[:, 2 * qk_w + h * HEAD_D:2 * qk_w + (h + 1) * HEAD_D].reshape(nc, c, HEAD_D)
        beta = beta_all[:, :, h:h + 1]
        gcol = gc_all[:, :, h:h + 1]
        g_hi = gcol.astype(BF16).astype(F32)
        g_mid = (gcol - g_hi).astype(BF16).astype(F32)
        g_lo = gcol - g_hi - g_mid
        ones_hi = jnp.where(lane < 6, 1.0, 0.0)
        lhs = jnp.where(lane == 0, g_hi, jnp.where(lane == 1, g_mid, jnp.where(lane == 2, g_lo, ones_hi)))
        rhs_g = jnp.where(lane == 3, -g_hi, jnp.where(lane == 4, -g_mid, jnp.where(lane == 5, -g_lo, ones_hi)))
        decay = jnp.exp(jnp.where(causal, _bmm_nt(lhs, rhs_g), -jnp.inf))
        kb = k * beta
        a = jnp.where(strict, _bmm_nt(kb, k) * decay, 0.0)
        d = jnp.where(same16, a, 0.0)
        x_inv = eye - d
        dp = d
        for _ in range(3):
            dp = _bmm(dp, dp)
            x_inv = x_inv + _bmm(x_inv, dp)
        e = jnp.where(jnp.logical_and(same32, jnp.logical_not(same16)), a, 0.0)
        x_inv = x_inv - _bmm(x_inv, _bmm(e, x_inv))
        f = jnp.where(same32, 0.0, a)
        x_inv = x_inv - _bmm(x_inv, _bmm(f, x_inv))
        wu = _bmm(x_inv, jnp.concatenate([kb * jnp.exp(gcol), v * beta], axis=2))
        attn = _bmm_nt(q, k) * decay
        g_last = gcol[:, c - 1:c, :]
        kd = k * jnp.exp(g_last - gcol)
        mq = _bmm_tn(kd, wu)
        aw = _bmm(attn, wu)
        m_s[h] = mq[:, :, :HEAD_D].astype(BF16)
        sq_s[h] = mq[:, :, HEAD_D:]
        qe_s[h] = (q * jnp.exp(gcol) - aw[:, :, :HEAD_D]).astype(BF16)
        dec_s[h] = jnp.broadcast_to(jnp.exp(g_last), (nc, 1, HEAD_D))
        o_s[:, hs] = aw[:, :, HEAD_D:].reshape(tb, HEAD_D)

    for ci in range(nc):
        for h in range(N_HEADS):
            hs = slice(h * HEAD_D, (h + 1) * HEAD_D)
            s_h = st_ref[h]
            s_b = s_h.astype(BF16)
            o_s[ci * c:(ci + 1) * c, hs] += jnp.dot(qe_s[h, ci], s_b, preferred_element_type=F32)
            st_ref[h] = (s_h * dec_s[h, ci] - jnp.dot(m_s[h, ci], s_b, preferred_element_type=F32)
                         + sq_s[h, ci])

    o_ref[0] = _head_rms_gate(o_s[...], z_ref[0].astype(F32), nw_ref[...], HEAD_D).astype(o_ref.dtype)


def _gdn(proj, small, conv_w, a_log, dt_bias, norm_w, *, tb):
    b, t, _ = proj.shape
    conv_cols = 3 * N_HEADS * HEAD_D
    mix_w = N_HEADS * HEAD_D
    return pl.pallas_call(
        functools.partial(_gdn_kernel, tb=tb),
        grid=(b, t // tb),
        in_specs=[
            pl.BlockSpec((1, tb, conv_cols), lambda i, j: (i, j, 0)),
            pl.BlockSpec((1, tb, mix_w), lambda i, j: (i, j, conv_cols // mix_w)),
            pl.BlockSpec((1, tb, 2 * LANES), lambda i, j: (i, j, 0)),
            pl.BlockSpec((CONV_K, conv_cols), lambda i, j: (0, 0)),
            pl.BlockSpec((1, LANES), lambda i, j: (0, 0)),
            pl.BlockSpec((1, LANES), lambda i, j: (0, 0)),
            pl.BlockSpec((1, HEAD_D), lambda i, j: (0, 0)),
        ],
        out_specs=pl.BlockSpec((1, tb, mix_w), lambda i, j: (i, j, 0)),
        out_shape=jax.ShapeDtypeStruct((b, t, mix_w), BF16),
        scratch_shapes=[
            pltpu.VMEM((tb + 8, conv_cols), F32),
            pltpu.VMEM((tb, conv_cols), F32),
            pltpu.VMEM((2, tb, LANES), F32),
            pltpu.VMEM((tb, mix_w), F32),
            pltpu.VMEM((N_HEADS, tb // CHUNK, HEAD_D, HEAD_D), BF16),
            pltpu.VMEM((N_HEADS, tb // CHUNK, HEAD_D, HEAD_D), F32),
            pltpu.VMEM((N_HEADS, tb // CHUNK, CHUNK, HEAD_D), BF16),
            pltpu.VMEM((N_HEADS, tb // CHUNK, 1, HEAD_D), F32),
            pltpu.VMEM((N_HEADS, HEAD_D, HEAD_D), F32),
        ],
        compiler_params=pltpu.CompilerParams(
            dimension_semantics=("parallel", "arbitrary"), vmem_limit_bytes=VMEM_LIMIT),
        name="gdn",
    )(proj, proj, small, conv_w, a_log, dt_bias, norm_w)


def _hgrn_kernel(q_ref, f_ref, i_ref, g_ref, lb_ref, nw_ref, o_ref,
                 q_s, k_s, lf_s, i_s, o_s, st_ref, *, tb, layer):
    @pl.when(pl.program_id(1) == 0)
    def _():
        st_ref[...] = jnp.zeros(st_ref.shape, F32)

    lbl = lb_ref[...]
    e_lb = jnp.exp(lbl - jnp.max(lbl, axis=0, keepdims=True))
    lb = jnp.sum(e_lb[:layer + 1], axis=0, keepdims=True) / jnp.sum(e_lb, axis=0, keepdims=True)

    f = lb + (1.0 - lb) * _sigmoid(f_ref[0].astype(F32))
    k_s[...] = 1.0 - f
    lf_s[...] = jnp.log(f)
    q_s[...] = _silu(q_ref[0].astype(F32)) * (HEAD_D ** -0.5)
    i_s[...] = i_ref[0].astype(F32)

    c = CHUNK
    row = _iota2((c, c), 0)
    col = _iota2((c, c), 1)
    ltri = (row >= col).astype(BF16)
    level_masks = {}
    for m in (32, 16, 8):
        sh = int(math.log2(2 * m))
        level_masks[m] = jnp.logical_and(
            (row >> sh) == (col >> sh),
            jnp.logical_and((row & (2 * m - 1)) >= m, (col & (2 * m - 1)) < m))
    sub = _iota2((8, HEAD_D), 0)

    def chunk_body(ci, carry):
        r0 = pl.multiple_of(ci * c, c)
        b_all = _mm_01(ltri, lf_s[pl.ds(r0, c), :])
        for h in range(N_HEADS):
            sl = slice(h * HEAD_D, (h + 1) * HEAD_D)
            b = b_all[:, sl]
            q = q_s[pl.ds(r0, c), sl]
            k = k_s[pl.ds(r0, c), sl]
            iv = i_s[pl.ds(r0, c), sl]
            attn = jnp.zeros((c, c), F32)
            for m in (32, 16, 8):
                ref = jnp.concatenate(
                    [jnp.broadcast_to(b[r:r + 1, :], (2 * m, HEAD_D)) for r in range(m, c, 2 * m)], axis=0)
                e = jnp.exp(-jnp.abs(b - ref))
                attn = attn + jnp.where(level_masks[m], _mm_nt(q * e, k * e), 0.0)
            s_t = st_ref[h]
            o = _mm_nt(q * jnp.exp(b), s_t) + _mm(attn, iv)
            diag = []
            for r in range(0, c, 8):
                qb, kb, bb, ib = q[r:r + 8], k[r:r + 8], b[r:r + 8], iv[r:r + 8]
                ob = jnp.zeros((8, HEAD_D), F32)
                for s in range(8):
                    dec = jnp.exp(jnp.where(sub >= s, bb - bb[s:s + 1, :], -jnp.inf))
                    a_col = jnp.sum(dec * qb * kb[s:s + 1, :], axis=-1, keepdims=True)
                    ob = ob + a_col * ib[s:s + 1, :]
                diag.append(ob)
            o = o + jnp.concatenate(diag, axis=0)
            b_last = b[c - 1:c, :]
            st_ref[h] = s_t * jnp.exp(b_last) + _mm_tn(iv, k * jnp.exp(b_last - b))
            o_s[pl.ds(r0, c), sl] = o
        return carry

    lax.fori_loop(0, tb // c, chunk_body, 0)
    o_ref[0] = _head_rms_gate(o_s[...], g_ref[0].astype(F32), nw_ref[...], HEAD_D).astype(o_ref.dtype)


def _hgrn(proj, lb_logits, norm_w, *, tb, layer, col0):
    b, t, _ = proj.shape
    w = N_HEADS * HEAD_D
    specs = [pl.BlockSpec((1, tb, w), functools.partial(lambda i, j, off: (i, j, off), off=col0 + n))
             for n in range(4)]
    return pl.pallas_call(
        functools.partial(_hgrn_kernel, tb=tb, layer=layer),
        grid=(b, t // tb),
        in_specs=specs + [
            pl.BlockSpec(lb_logits.shape, lambda i, j: (0, 0)),
            pl.BlockSpec((1, HEAD_D), lambda i, j: (0, 0)),
        ],
        out_specs=pl.BlockSpec((1, tb, w), lambda i, j: (i, j, 0)),
        out_shape=jax.ShapeDtypeStruct((b, t, w), BF16),
        scratch_shapes=[pltpu.VMEM((tb, w), F32)] * 5 + [pltpu.VMEM((N_HEADS, HEAD_D, HEAD_D), F32)],
        compiler_params=pltpu.CompilerParams(
            dimension_semantics=("parallel", "arbitrary"), vmem_limit_bytes=VMEM_LIMIT),
        name="hgrn2",
    )(proj, proj, proj, proj, lb_logits, norm_w)


def _ret_kernel(q_ref, k_ref, v_ref, g_ref, cos_ref, sin_ref, nw_ref, o_ref, dmat_s, st_ref, *, tc):
    hh = pl.program_id(1).astype(F32)
    log_gamma = jnp.log(1.0 - jnp.exp((jnp.full((1, 1), -5.0, F32) - hh) * math.log(2.0)))

    @pl.when(pl.program_id(2) == 0)
    def _():
        st_ref[...] = jnp.zeros(st_ref.shape, F32)
        rel = (_iota2((tc, tc), 0) - _iota2((tc, tc), 1)).astype(F32)
        dmat_s[...] = jnp.where(rel >= 0, jnp.exp(jnp.maximum(rel, 0.0) * log_gamma), 0.0)

    pos = _iota2((tc, 1), 0).astype(F32)
    q_decay = jnp.exp((pos + 1.0) * log_gamma)
    k_decay = jnp.exp((tc - 1.0 - pos) * log_gamma)
    chunk_decay = jnp.exp(tc * log_gamma)

    cos = cos_ref[...]
    sin = sin_ref[...]
    half = RET_DK // 2

    def rot(x):
        x1, x2 = x[:, :half], x[:, half:]
        return jnp.concatenate([x1 * cos - x2 * sin, x1 * sin + x2 * cos], axis=1)

    q = rot(q_ref[0].astype(F32))
    k = rot(k_ref[0].astype(F32)) * (RET_DK ** -0.5)
    v = v_ref[0]
    s = st_ref[...]
    attn = _mm_nt(q, k) * dmat_s[...]
    o = _mm(q, s) * q_decay + _mm(attn, v)
    st_ref[...] = s * chunk_decay + _mm_tn(k * k_decay, v)
    ms = jnp.mean(o * o, axis=-1, keepdims=True)
    o_ref[0] = (o * lax.rsqrt(ms + RMS_EPS) * nw_ref[...] * _silu(g_ref[0].astype(F32))).astype(o_ref.dtype)


def _retention(proj, cos, sin, norm_w, *, tc):
    b, t, _ = proj.shape
    qb = N_HEADS
    vb = 2 * N_HEADS * RET_DK // RET_DV
    gb = vb + N_HEADS
    return pl.pallas_call(
        functools.partial(_ret_kernel, tc=tc),
        grid=(b, N_HEADS, t // tc),
        in_specs=[
            pl.BlockSpec((1, tc, RET_DK), lambda i, h, j: (i, j, h)),
            pl.BlockSpec((1, tc, RET_DK), lambda i, h, j: (i, j, qb + h)),
            pl.BlockSpec((1, tc, RET_DV), lambda i, h, j: (i, j, vb + h)),
            pl.BlockSpec((1, tc, RET_DV), lambda i, h, j: (i, j, gb + h)),
            pl.BlockSpec((tc, RET_DK // 2), lambda i, h, j: (j, 0)),
            pl.BlockSpec((tc, RET_DK // 2), lambda i, h, j: (j, 0)),
            pl.BlockSpec((1, RET_DV), lambda i, h, j: (0, 0)),
        ],
        out_specs=pl.BlockSpec((1, tc, RET_DV), lambda i, h, j: (i, j, h)),
        out_shape=jax.ShapeDtypeStruct((b, t, N_HEADS * RET_DV), BF16),
        scratch_shapes=[pltpu.VMEM((tc, tc), F32), pltpu.VMEM((RET_DK, RET_DV), F32)],
        compiler_params=pltpu.CompilerParams(
            dimension_semantics=("parallel", "parallel", "arbitrary"), vmem_limit_bytes=VMEM_LIMIT),
        name="retention",
    )(proj, proj, proj, proj, cos, sin, norm_w)


def _route_gates(logits_t):
    cl = [logits_t[g:g + 1, :] for g in range(N_GROUPS)]
    cmax = functools.reduce(jnp.maximum, cl)
    denom = sum(jnp.exp(x - cmax) for x in cl)
    g_prob = 1.0 / denom
    g_idx = jnp.full(cmax.shape, N_GROUPS - 1, jnp.int32)
    for g in range(N_GROUPS - 2, -1, -1):
        g_idx = jnp.where(cl[g] == cmax, g, g_idx)
    def fine_row(g, j):
        r = N_GROUPS + g * EXPERTS_PER_GROUP + j
        return logits_t[r:r + 1, :]

    fl = []
    for j in range(EXPERTS_PER_GROUP):
        x = fine_row(N_GROUPS - 1, j)
        for g in range(N_GROUPS - 2, -1, -1):
            x = jnp.where(g_idx == g, fine_row(g, j), x)
        fl.append(x)
    m1 = functools.reduce(jnp.maximum, fl)
    i1 = jnp.full(m1.shape, EXPERTS_PER_GROUP - 1, jnp.int32)
    for j in range(EXPERTS_PER_GROUP - 2, -1, -1):
        i1 = jnp.where(fl[j] == m1, j, i1)
    rest = [jnp.where(i1 == j, -jnp.inf, fl[j]) for j in range(EXPERTS_PER_GROUP)]
    m2 = functools.reduce(jnp.maximum, rest)
    i2 = jnp.full(m2.shape, EXPERTS_PER_GROUP - 1, jnp.int32)
    for j in range(EXPERTS_PER_GROUP - 2, -1, -1):
        i2 = jnp.where(jnp.logical_and(rest[j] == m2, i1 != j), j, i2)
    e2 = jnp.exp(m2 - m1)
    w1 = g_prob / (1.0 + e2)
    w2 = g_prob * e2 / (1.0 + e2)
    local = [jnp.where(i1 == j, w1, 0.0) + jnp.where(i2 == j, w2, 0.0) for j in range(EXPERTS_PER_GROUP)]
    return g_idx, local


def _outproj_kernel(a_ref, b_ref, wa_ref, wb_ref, x_ref, nw_ref, rw_ref, rb_ref,
                    x1_ref, h_ref, gk_ref, krow_ref, cnt_ref):
    x1 = (x_ref[...]
          + jnp.dot(a_ref[...], wa_ref[...], preferred_element_type=F32)
          + jnp.dot(b_ref[...], wb_ref[...], preferred_element_type=F32))
    x1_ref[...] = x1
    ms = jnp.mean(x1 * x1, axis=-1, keepdims=True)
    h = x1 * lax.rsqrt(ms + RMS_EPS) * nw_ref[...]
    h_ref[...] = h.astype(BF16)
    rw = rw_ref[...]
    hh = h.astype(BF16)
    hl = (h - hh.astype(F32)).astype(BF16)
    wh = rw.astype(BF16)
    wl = (rw - wh.astype(F32)).astype(BF16)
    logits = (jnp.dot(hh, wh, preferred_element_type=F32)
              + jnp.dot(hl, wh, preferred_element_type=F32)
              + jnp.dot(hh, wl, preferred_element_type=F32)) + rb_ref[...]
    g_idx, local = _route_gates(logits.T)
    tm = g_idx.shape[1]
    mem = (_iota2((8, tm), 0) == g_idx).astype(BF16)
    s_i = _iota2((tm, tm), 0)
    t_i = _iota2((tm, tm), 1)
    sub_shift = int(math.log2(MOE_SUB))
    before = jnp.logical_and(s_i < t_i, (s_i >> sub_shift) == (t_i >> sub_shift)).astype(BF16)
    rank = jnp.sum(mem.astype(F32) * jnp.dot(mem, before, preferred_element_type=F32), axis=0, keepdims=True)
    key = g_idx.astype(F32) * MOE_KEY_STRIDE + rank
    krow_ref[...] = key
    sub_sel = ((_iota2((tm, LANES), 0) >> sub_shift) == _iota2((tm, LANES), 1)).astype(BF16)
    cnt_ref[...] = jnp.dot(mem, sub_sel, preferred_element_type=F32)
    rows = jnp.concatenate([key] + local + [jnp.zeros((LANES - 1 - EXPERTS_PER_GROUP, tm), F32)], axis=0)
    gk_ref[...] = rows.T


def _outproj_route(a, b, a_blk, b_blk, wa, wb, x, nw, rw, rb, *, tm, name):
    n, d = x.shape
    ka = wa.shape[0]
    kb = wb.shape[0]
    return pl.pallas_call(
        _outproj_kernel,
        grid=(n // tm,),
        in_specs=[
            pl.BlockSpec((tm, ka), lambda i: (i, a_blk)),
            pl.BlockSpec((tm, kb), lambda i: (i, b_blk)),
            pl.BlockSpec((ka, d), lambda i: (0, 0)),
            pl.BlockSpec((kb, d), lambda i: (0, 0)),
            pl.BlockSpec((tm, d), lambda i: (i, 0)),
            pl.BlockSpec((1, d), lambda i: (0, 0)),
            pl.BlockSpec((d, LANES), lambda i: (0, 0)),
            pl.BlockSpec((1, LANES), lambda i: (0, 0)),
        ],
        out_specs=[
            pl.BlockSpec((tm, d), lambda i: (i, 0)),
            pl.BlockSpec((tm, d), lambda i: (i, 0)),
            pl.BlockSpec((tm, LANES), lambda i: (i, 0)),
            pl.BlockSpec((1, tm), lambda i: (0, i)),
            pl.BlockSpec((8, LANES), lambda i: (i, 0)),
        ],
        out_shape=[jax.ShapeDtypeStruct((n, d), F32),
                   jax.ShapeDtypeStruct((n, d), BF16),
                   jax.ShapeDtypeStruct((n, LANES), F32),
                   jax.ShapeDtypeStruct((1, n), F32),
                   jax.ShapeDtypeStruct((8 * (n // tm), LANES), F32)],
        compiler_params=pltpu.CompilerParams(
            dimension_semantics=("parallel",), vmem_limit_bytes=VMEM_LIMIT),
        name=name,
    )(a, b, wa, wb, x, nw, rw, rb)


def _round_up(v, m):
    return ((v + (m - 1)) // m) * m


def _moe_kernel(cnt_ref, h_ref, krow_ref, gk_ref, wgu_ref, wd_ref, x_ref, fnw_ref, o_ref,
                hc_s, gc_s, y_s, *, final_norm):
    i = pl.program_id(0)
    g = pl.program_id(1)
    tm = h_ref.shape[0]
    n_sub = tm // MOE_SUB

    @pl.when(g == 0)
    def _():
        o_ref[...] = x_ref[...]

    counts = [cnt_ref[(i * n_sub + s) * N_GROUPS + g] for s in range(n_sub)]
    offs = [jnp.int32(0)]
    for s in range(n_sub):
        offs.append(offs[-1] + _round_up(counts[s], MOE_ALIGN))
    total = offs[-1]
    key0 = g.astype(F32) * MOE_KEY_STRIDE
    sub_iota = _iota2((MOE_BLK, MOE_SUB), 0).astype(F32)
    lane_iota = _iota2((MOE_SUB, MOE_BLK), 1).astype(F32)

    for s in range(n_sub):
        rows = slice(s * MOE_SUB, (s + 1) * MOE_SUB)

        def pack(w, carry, s=s, rows=rows):
            base = key0 + (w * MOE_BLK).astype(F32)
            sel = (krow_ref[:, rows] == sub_iota + base).astype(BF16)
            dst = pl.ds(pl.multiple_of(offs[s] + w * MOE_BLK, MOE_ALIGN), MOE_BLK)
            hc_s[dst, :] = jnp.dot(sel, h_ref[rows, :], preferred_element_type=F32).astype(BF16)
            gc_s[dst, :] = _mm_01(sel, gk_ref[rows, :])
            return carry

        lax.fori_loop(0, (counts[s] + MOE_BLK - 1) // MOE_BLK, pack, 0)

    tail = pl.ds(pl.multiple_of(total, MOE_ALIGN), MOE_BLK)
    hc_s[tail, :] = jnp.zeros((MOE_BLK, hc_s.shape[1]), BF16)
    gc_s[tail, :] = jnp.zeros((MOE_BLK, LANES), F32)

    n_blk = (total + MOE_BLK - 1) // MOE_BLK

    def experts(bi, carry):
        blk = pl.ds(pl.multiple_of(bi * MOE_BLK, MOE_BLK), MOE_BLK)
        hb = hc_s[blk, :]
        gates = gc_s[blk, :]
        y = jnp.zeros((MOE_BLK, o_ref.shape[1]), F32)
        for e in range(EXPERTS_PER_GROUP):
            gu = jnp.dot(hb, wgu_ref[0, e], preferred_element_type=F32)
            act = _silu(gu[:, :D_EXPERT]) * gu[:, D_EXPERT:] * gates[:, 1 + e:2 + e]
            y = y + jnp.dot(act.astype(BF16), wd_ref[0, e], preferred_element_type=F32)
        y_s[blk, :] = y.astype(BF16)
        return carry

    lax.fori_loop(0, n_blk, experts, 0)
    y_s[pl.ds(pl.multiple_of(n_blk * MOE_BLK, MOE_BLK), MOE_BLK), :] = jnp.zeros((MOE_BLK, y_s.shape[1]), BF16)

    for s in range(n_sub):
        rows = slice(s * MOE_SUB, (s + 1) * MOE_SUB)

        def unpack(w, carry, s=s, rows=rows):
            base = key0 + (w * MOE_BLK).astype(F32)
            sel = (gk_ref[rows, 0:1] == lane_iota + base).astype(BF16)
            src = pl.ds(pl.multiple_of(offs[s] + w * MOE_BLK, MOE_ALIGN), MOE_BLK)
            o_ref[rows, :] += jnp.dot(sel, y_s[src, :], preferred_element_type=F32)
            return carry

        lax.fori_loop(0, (counts[s] + MOE_BLK - 1) // MOE_BLK, unpack, 0)

    if final_norm:
        @pl.when(g == pl.num_programs(1) - 1)
        def _():
            y = o_ref[...]
            ms = jnp.mean(y * y, axis=-1, keepdims=True)
            o_ref[...] = y * lax.rsqrt(ms + RMS_EPS) * fnw_ref[...]


def _moe(h, krow, gk, counts, wgu, wd, x, fnw, *, layer, tm, final_norm, name):
    n, d = x.shape
    buf_rows = _round_up(tm + (tm // MOE_SUB) * MOE_ALIGN, MOE_BLK) + 2 * MOE_BLK
    grid_spec = pltpu.PrefetchScalarGridSpec(
        num_scalar_prefetch=1,
        grid=(n // tm, N_GROUPS),
        in_specs=[
            pl.BlockSpec((tm, d), lambda i, g, c: (i, 0)),
            pl.BlockSpec((1, tm), lambda i, g, c: (0, i)),
            pl.BlockSpec((tm, LANES), lambda i, g, c: (i, 0)),
            pl.BlockSpec((1, EXPERTS_PER_GROUP, d, 2 * D_EXPERT), lambda i, g, c: (layer, g, 0, 0)),
            pl.BlockSpec((1, EXPERTS_PER_GROUP, D_EXPERT, d), lambda i, g, c: (layer, g, 0, 0)),
            pl.BlockSpec((tm, d), lambda i, g, c: (i, 0)),
            pl.BlockSpec((1, d), lambda i, g, c: (0, 0)),
        ],
        out_specs=pl.BlockSpec((tm, d), lambda i, g, c: (i, 0)),
        scratch_shapes=[pltpu.VMEM((buf_rows, d), BF16),
                        pltpu.VMEM((buf_rows, LANES), F32),
                        pltpu.VMEM((buf_rows, d), BF16)],
    )
    return pl.pallas_call(
        functools.partial(_moe_kernel, final_norm=final_norm),
        grid_spec=grid_spec,
        out_shape=jax.ShapeDtypeStruct((n, d), F32),
        compiler_params=pltpu.CompilerParams(
            dimension_semantics=("parallel", "arbitrary"), vmem_limit_bytes=VMEM_LIMIT),
        name=name,
    )(counts, h, krow, gk, wgu, wd, x, fnw)


def _pad_cols(a, width):
    return jnp.pad(a, ((0, 0), (0, width - a.shape[1])))


def _dispatch_counts(cnt, route_tm):
    c = cnt.reshape(-1, 8, LANES)[:, :N_GROUPS, :route_tm // MOE_SUB]
    return jnp.transpose(c, (0, 2, 1)).reshape(-1).astype(jnp.int32)


def _router_params(wc, bc, wf, bf):
    rw = _pad_cols(jnp.concatenate([wc, wf], axis=1), LANES)
    rb = _pad_cols(jnp.concatenate([bc, bf])[None, :], LANES)
    return rw, rb


def kernel(x, norm_mix_w, norm_ffn_w, even_w_in, gdn_conv_w, gdn_a_log, gdn_dt_bias, gdn_norm_w, hgrn_lb_logits, hgrn_norm_w, even_w_out, odd_w_in, ret_norm_w, odd_w_out, router_c_w, router_c_b, router_f_w, router_f_b, moe_w_gate_up, moe_w_down, final_norm_w):
    bsz, seq, d = x.shape
    n = bsz * seq
    xt = x.reshape(n, d)
    mix_w = N_HEADS * HEAD_D
    conv_cols = 3 * mix_w
    gdn_main = conv_cols + mix_w

    w_in = even_w_in[0]
    small0 = gdn_main
    w_main = jnp.concatenate([w_in[:, :small0], w_in[:, small0 + 2 * N_HEADS:]], axis=1).astype(BF16)
    w_small = jnp.concatenate(
        [_pad_cols(w_in[:, small0:small0 + N_HEADS], LANES),
         _pad_cols(w_in[:, small0 + N_HEADS:small0 + 2 * N_HEADS], LANES)], axis=1).astype(BF16)
    proj, small = _norm_matmul(xt, norm_mix_w[0][None, :], w_main, w_small, tm=512, tn=1024, name="in_proj_even")
    proj = proj.reshape(bsz, seq, -1)
    small = small.reshape(bsz, seq, -1)
    o_a = _gdn(proj, small, gdn_conv_w[0], _pad_cols(gdn_a_log[0][None, :], LANES),
               _pad_cols(gdn_dt_bias[0][None, :], LANES), gdn_norm_w[0][None, :], tb=512)
    o_b = _hgrn(proj, hgrn_lb_logits, hgrn_norm_w[0][None, :], tb=512, layer=0, col0=gdn_main // mix_w)
    w_out = even_w_out[0].astype(BF16)
    rw, rb = _router_params(router_c_w[0], router_c_b[0], router_f_w[0], router_f_b[0])
    wgu_b = moe_w_gate_up.astype(BF16)
    wd_b = moe_w_down.astype(BF16)
    x1, h, gk, krow, cnt = _outproj_route(o_a.reshape(n, mix_w), o_b.reshape(n, mix_w), 0, 0,
                                          w_out[:mix_w], w_out[mix_w:], xt, norm_ffn_w[0][None, :], rw, rb,
                                          tm=ROUTE_TM, name="out_proj_even")
    x2 = _moe(h, krow, gk, _dispatch_counts(cnt, ROUTE_TM), wgu_b, wd_b, x1,
              final_norm_w[None, :], layer=0, tm=MOE_TM, final_norm=False, name="moe0")

    perm = np.concatenate([np.arange(0, RET_DK, 2), np.arange(1, RET_DK, 2)])
    qk_perm = np.concatenate([hd * RET_DK + perm for hd in range(2 * N_HEADS)])
    w_odd = odd_w_in[0]
    w_odd = jnp.concatenate([w_odd[:, qk_perm], w_odd[:, 2 * N_HEADS * RET_DK:]], axis=1).astype(BF16)
    (proj1,) = _norm_matmul(x2, norm_mix_w[1][None, :], w_odd, None, tm=512, tn=1024, name="in_proj_odd")
    inv = 1.0 / (ROPE_BASE ** jnp.linspace(0.0, 1.0, RET_DK // 2, dtype=F32))
    ang = jnp.arange(seq, dtype=F32)[:, None] * inv[None, :]
    o_c = _retention(proj1.reshape(bsz, seq, -1), jnp.cos(ang), jnp.sin(ang), ret_norm_w[0][None, :],
                     tc=RET_CHUNK)
    o_c = o_c.reshape(n, -1)
    w_out1 = odd_w_out[0].astype(BF16)
    half = w_out1.shape[0] // 2
    rw, rb = _router_params(router_c_w[1], router_c_b[1], router_f_w[1], router_f_b[1])
    x3, h, gk, krow, cnt = _outproj_route(o_c, o_c, 0, 1, w_out1[:half], w_out1[half:], x2,
                                          norm_ffn_w[1][None, :], rw, rb, tm=ROUTE_TM, name="out_proj_odd")
    out = _moe(h, krow, gk, _dispatch_counts(cnt, ROUTE_TM), wgu_b, wd_b, x3,
               final_norm_w[None, :], layer=1, tm=MOE_TM, final_norm=True, name="moe1")
    return out.reshape(bsz, seq, d)
```

```python
import functools
import math

import numpy as np
import jax
import jax.numpy as jnp
from jax import lax
from jax.experimental import pallas as pl
from jax.experimental.pallas import tpu as pltpu

F32 = jnp.float32
BF16 = jnp.bfloat16

D_MODEL = 1024
RMS_EPS = 1e-6
L2_EPS = 1e-6
CHUNK = 64
CONV_K = 4
N_HEADS = 4
HEAD_D = 128
RET_DK = 256
RET_DV = 512
RET_CHUNK = 256
ROPE_BASE = 10000.0
N_GROUPS = 4
EXPERTS_PER_GROUP = 4
N_EXPERTS = 16
D_EXPERT = 256
LANES = 128
ROUTE_TM = 512
MOE_TM = 1024
MOE_SUB = 256
MOE_BLK = 128
MOE_ALIGN = 16
MOE_KEY_STRIDE = 4096.0
MOE_FIRST_STEP = 32
MOE_FIRST = (256, 288, 320, 352)
VMEM_LIMIT = 56 * 1024 * 1024


def _mm(a, b):
    return jnp.dot(a.astype(BF16), b.astype(BF16), preferred_element_type=F32)


def _mm_nt(a, b):
    return lax.dot_general(a.astype(BF16), b.astype(BF16), (((1,), (1,)), ((), ())),
                           preferred_element_type=F32)


def _mm_tn(a, b):
    return _mm(a.T, b)


def _mm_01(m01, x):
    hi = x.astype(BF16)
    lo = (x - hi.astype(F32)).astype(BF16)
    return (jnp.dot(m01, hi, preferred_element_type=F32)
            + jnp.dot(m01, lo, preferred_element_type=F32))


def _bmm(a, b):
    return jnp.einsum('cik,ckj->cij', a.astype(BF16), b.astype(BF16), preferred_element_type=F32)


def _bmm_nt(a, b):
    return jnp.einsum('cik,cjk->cij', a.astype(BF16), b.astype(BF16), preferred_element_type=F32)


def _bmm_tn(a, b):
    return _bmm(jnp.swapaxes(a, 1, 2), b)


def _bmm_01(m01, x):
    hi = x.astype(BF16)
    lo = (x - hi.astype(F32)).astype(BF16)
    return (jnp.einsum('cik,ckj->cij', m01, hi, preferred_element_type=F32)
            + jnp.einsum('cik,ckj->cij', m01, lo, preferred_element_type=F32))


def _sigmoid(x):
    return 1.0 / (1.0 + jnp.exp(-x))


def _silu(x):
    return x * _sigmoid(x)


def _softplus(x):
    return jnp.maximum(x, 0.0) + jnp.log(1.0 + jnp.exp(-jnp.abs(x)))


def _iota2(shape, dim):
    return lax.broadcasted_iota(jnp.int32, shape, dim)


def _head_rms_gate(o, gate, nw, width):
    outs = []
    for h in range(o.shape[1] // width):
        oh = o[:, h * width:(h + 1) * width]
        ms = jnp.mean(oh * oh, axis=-1, keepdims=True)
        outs.append(oh * lax.rsqrt(ms + RMS_EPS) * nw)
    return jnp.concatenate(outs, axis=1) * _silu(gate)


def _norm_mm_kernel(*refs, tn, has_small):
    if has_small:
        x_ref, nw_ref, w_ref, ws_ref, o_ref, os_ref = refs
    else:
        x_ref, nw_ref, w_ref, o_ref = refs
    x = x_ref[...]
    ms = jnp.mean(x * x, axis=-1, keepdims=True)
    hb = (x * lax.rsqrt(ms + RMS_EPS) * nw_ref[...]).astype(BF16)
    if has_small:
        os_ref[...] = jnp.dot(hb, ws_ref[...], preferred_element_type=F32)
    for j in range(w_ref.shape[1] // tn):
        cols = slice(j * tn, (j + 1) * tn)
        o_ref[:, cols] = jnp.dot(hb, w_ref[:, cols], preferred_element_type=F32).astype(o_ref.dtype)


def _norm_matmul(x, nw, w, ws, *, tm, tn, name):
    n, d = x.shape
    nout = w.shape[1]
    has_small = ws is not None
    in_specs = [
        pl.BlockSpec((tm, d), lambda i: (i, 0)),
        pl.BlockSpec((1, d), lambda i: (0, 0)),
        pl.BlockSpec((d, nout), lambda i: (0, 0)),
    ]
    out_specs = [pl.BlockSpec((tm, nout), lambda i: (i, 0))]
    out_shape = [jax.ShapeDtypeStruct((n, nout), BF16)]
    args = [x, nw, w]
    if has_small:
        nsmall = ws.shape[1]
        in_specs.append(pl.BlockSpec((d, nsmall), lambda i: (0, 0)))
        out_specs.append(pl.BlockSpec((tm, nsmall), lambda i: (i, 0)))
        out_shape.append(jax.ShapeDtypeStruct((n, nsmall), F32))
        args.append(ws)
    return pl.pallas_call(
        functools.partial(_norm_mm_kernel, tn=tn, has_small=has_small),
        grid=(n // tm,),
        in_specs=in_specs,
        out_specs=out_specs,
        out_shape=out_shape,
        compiler_params=pltpu.CompilerParams(
            dimension_semantics=("parallel",), vmem_limit_bytes=VMEM_LIMIT),
        name=name,
    )(*args)


def _gdn_kernel(qkv_ref, z_ref, sm_ref, cw_ref, alog_ref, dtb_ref, nw_ref, o_ref,
                xs_ref, qkv_s, bg_s, o_s, m_s, sq_s, qe_s, dec_s, st_ref, *, tb):
    @pl.when(pl.program_id(1) == 0)
    def _():
        xs_ref[0:8, :] = jnp.zeros((8, xs_ref.shape[1]), F32)
        st_ref[...] = jnp.zeros(st_ref.shape, F32)

    x = qkv_ref[0].astype(F32)
    xs_ref[8:8 + tb, :] = x
    cw = cw_ref[...]
    y = x * cw[CONV_K - 1:CONV_K, :]
    for j in range(1, CONV_K):
        y = y + xs_ref[pl.ds(8 - j, tb), :] * cw[CONV_K - 1 - j:CONV_K - j, :]
    xs_ref[0:8, :] = xs_ref[tb:tb + 8, :]
    y = _silu(y)

    qk_w = N_HEADS * HEAD_D
    for h in range(N_HEADS):
        for base, scale in ((0, HEAD_D ** -0.5), (qk_w, 1.0)):
            c0 = base + h * HEAD_D
            v = y[:, c0:c0 + HEAD_D]
            inv = lax.rsqrt(jnp.sum(v * v, axis=-1, keepdims=True) + L2_EPS)
            qkv_s[:, c0:c0 + HEAD_D] = v * (inv * scale)
    qkv_s[:, 2 * qk_w:] = y[:, 2 * qk_w:]

    sm = sm_ref[0]
    bg_s[0] = _sigmoid(sm[:, :LANES])
    bg_s[1] = -jnp.exp(alog_ref[...]) * _softplus(sm[:, LANES:] + dtb_ref[...])

    c = CHUNK
    nc = tb // c
    row = _iota2((c, c), 0)
    col = _iota2((c, c), 1)
    causal = (row >= col)[None]
    strict = (row > col)[None]
    same16 = ((row >> 4) == (col >> 4))[None]
    same32 = ((row >> 5) == (col >> 5))[None]
    eye = (row == col).astype(F32)[None]
    ltri = jnp.broadcast_to((row >= col).astype(BF16)[None], (nc, c, c))
    lane = _iota2((nc, c, LANES), 2)

    gc_all = _bmm_01(ltri, bg_s[1].reshape(nc, c, LANES))
    beta_all = bg_s[0].reshape(nc, c, LANES)
    for h in range(N_HEADS):
        hs = slice(h * HEAD_D, (h + 1) * HEAD_D)
        q = qkv_s[:, h * HEAD_D:(h + 1) * HEAD_D].reshape(nc, c, HEAD_D)
        k = qkv_s[:, qk_w + h * HEAD_D:qk_w + (h + 1) * HEAD_D].reshape(nc, c, HEAD_D)
        v = qkv_s[:, 2 * qk_w + h * HEAD_D:2 * qk_w + (h + 1) * HEAD_D].reshape(nc, c, HEAD_D)
        beta = beta_all[:, :, h:h + 1]
        gcol = gc_all[:, :, h:h + 1]
        g_hi = gcol.astype(BF16).astype(F32)
        g_mid = (gcol - g_hi).astype(BF16).astype(F32)
        g_lo = gcol - g_hi - g_mid
        ones_hi = jnp.where(lane < 6, 1.0, 0.0)
        lhs = jnp.where(lane == 0, g_hi, jnp.where(lane == 1, g_mid, jnp.where(lane == 2, g_lo, ones_hi)))
        rhs_g = jnp.where(lane == 3, -g_hi, jnp.where(lane == 4, -g_mid, jnp.where(lane == 5, -g_lo, ones_hi)))
        decay = jnp.exp(jnp.where(causal, _bmm_nt(lhs, rhs_g), -jnp.inf))
        kb = k * beta
        a = jnp.where(strict, _bmm_nt(kb, k) * decay, 0.0)
        d = jnp.where(same16, a, 0.0)
        x_inv = eye - d
        dp = d
        for _ in range(3):
            dp = _bmm(dp, dp)
            x_inv = x_inv + _bmm(x_inv, dp)
        e = jnp.where(jnp.logical_and(same32, jnp.logical_not(same16)), a, 0.0)
        x_inv = x_inv - _bmm(x_inv, _bmm(e, x_inv))
        f = jnp.where(same32, 0.0, a)
        x_inv = x_inv - _bmm(x_inv, _bmm(f, x_inv))
        wu = _bmm(x_inv, jnp.concatenate([kb * jnp.exp(gcol), v * beta], axis=2))
        attn = _bmm_nt(q, k) * decay
        g_last = gcol[:, c - 1:c, :]
        kd = k * jnp.exp(g_last - gcol)
        mq = _bmm_tn(kd, wu)
        aw = _bmm(attn, wu)
        m_s[h] = mq[:, :, :HEAD_D].astype(BF16)
        sq_s[h] = mq[:, :, HEAD_D:]
        qe_s[h] = (q * jnp.exp(gcol) - aw[:, :, :HEAD_D]).astype(BF16)
        dec_s[h] = jnp.broadcast_to(jnp.exp(g_last), (nc, 1, HEAD_D))
        o_s[:, hs] = aw[:, :, HEAD_D:].reshape(tb, HEAD_D)

    for ci in range(nc):
        for h in range(N_HEADS):
            hs = slice(h * HEAD_D, (h + 1) * HEAD_D)
            s_h = st_ref[h]
            s_b = s_h.astype(BF16)
            o_s[ci * c:(ci + 1) * c, hs] += jnp.dot(qe_s[h, ci], s_b, preferred_element_type=F32)
            st_ref[h] = (s_h * dec_s[h, ci] - jnp.dot(m_s[h, ci], s_b, preferred_element_type=F32)
                         + sq_s[h, ci])

    o_ref[0] = _head_rms_gate(o_s[...], z_ref[0].astype(F32), nw_ref[...], HEAD_D).astype(o_ref.dtype)


def _gdn(proj, small, conv_w, a_log, dt_bias, norm_w, *, tb):
    b, t, _ = proj.shape
    conv_cols = 3 * N_HEADS * HEAD_D
    mix_w = N_HEADS * HEAD_D
    return pl.pallas_call(
        functools.partial(_gdn_kernel, tb=tb),
        grid=(b, t // tb),
        in_specs=[
            pl.BlockSpec((1, tb, conv_cols), lambda i, j: (i, j, 0)),
            pl.BlockSpec((1, tb, mix_w), lambda i, j: (i, j, conv_cols // mix_w)),
            pl.BlockSpec((1, tb, 2 * LANES), lambda i, j: (i, j, 0)),
            pl.BlockSpec((CONV_K, conv_cols), lambda i, j: (0, 0)),
            pl.BlockSpec((1, LANES), lambda i, j: (0, 0)),
            pl.BlockSpec((1, LANES), lambda i, j: (0, 0)),
            pl.BlockSpec((1, HEAD_D), lambda i, j: (0, 0)),
        ],
        out_specs=pl.BlockSpec((1, tb, mix_w), lambda i, j: (i, j, 0)),
        out_shape=jax.ShapeDtypeStruct((b, t, mix_w), BF16),
        scratch_shapes=[
            pltpu.VMEM((tb + 8, conv_cols), F32),
            pltpu.VMEM((tb, conv_cols), F32),
            pltpu.VMEM((2, tb, LANES), F32),
            pltpu.VMEM((tb, mix_w), F32),
            pltpu.VMEM((N_HEADS, tb // CHUNK, HEAD_D, HEAD_D), BF16),
            pltpu.VMEM((N_HEADS, tb // CHUNK, HEAD_D, HEAD_D), F32),
            pltpu.VMEM((N_HEADS, tb // CHUNK, CHUNK, HEAD_D), BF16),
            pltpu.VMEM((N_HEADS, tb // CHUNK, 1, HEAD_D), F32),
            pltpu.VMEM((N_HEADS, HEAD_D, HEAD_D), F32),
        ],
        compiler_params=pltpu.CompilerParams(
            dimension_semantics=("parallel", "arbitrary"), vmem_limit_bytes=VMEM_LIMIT),
        name="gdn",
    )(proj, proj, small, conv_w, a_log, dt_bias, norm_w)


def _hgrn_kernel(q_ref, f_ref, i_ref, g_ref, lb_ref, nw_ref, o_ref,
                 q_s, k_s, lf_s, i_s, o_s, st_ref, *, tb, layer):
    @pl.when(pl.program_id(1) == 0)
    def _():
        st_ref[...] = jnp.zeros(st_ref.shape, F32)

    lbl = lb_ref[...]
    e_lb = jnp.exp(lbl - jnp.max(lbl, axis=0, keepdims=True))
    lb = jnp.sum(e_lb[:layer + 1], axis=0, keepdims=True) / jnp.sum(e_lb, axis=0, keepdims=True)

    f = lb + (1.0 - lb) * _sigmoid(f_ref[0].astype(F32))
    k_s[...] = 1.0 - f
    lf_s[...] = jnp.log(f)
    q_s[...] = _silu(q_ref[0].astype(F32)) * (HEAD_D ** -0.5)
    i_s[...] = i_ref[0].astype(F32)

    c = CHUNK
    row = _iota2((c, c), 0)
    col = _iota2((c, c), 1)
    ltri = (row >= col).astype(BF16)
    level_masks = {}
    for m in (32, 16, 8):
        sh = int(math.log2(2 * m))
        level_masks[m] = jnp.logical_and(
            (row >> sh) == (col >> sh),
            jnp.logical_and((row & (2 * m - 1)) >= m, (col & (2 * m - 1)) < m))
    sub = _iota2((8, HEAD_D), 0)

    def chunk_body(ci, carry):
        r0 = pl.multiple_of(ci * c, c)
        b_all = _mm_01(ltri, lf_s[pl.ds(r0, c), :])
        for h in range(N_HEADS):
            sl = slice(h * HEAD_D, (h + 1) * HEAD_D)
            b = b_all[:, sl]
            q = q_s[pl.ds(r0, c), sl]
            k = k_s[pl.ds(r0, c), sl]
            iv = i_s[pl.ds(r0, c), sl]
            attn = jnp.zeros((c, c), F32)
            for m in (32, 16, 8):
                ref = jnp.concatenate(
                    [jnp.broadcast_to(b[r:r + 1, :], (2 * m, HEAD_D)) for r in range(m, c, 2 * m)], axis=0)
                e = jnp.exp(-jnp.abs(b - ref))
                attn = attn + jnp.where(level_masks[m], _mm_nt(q * e, k * e), 0.0)
            s_t = st_ref[h]
            o = _mm_nt(q * jnp.exp(b), s_t) + _mm(attn, iv)
            diag = []
            for r in range(0, c, 8):
                qb, kb, bb, ib = q[r:r + 8], k[r:r + 8], b[r:r + 8], iv[r:r + 8]
                ob = jnp.zeros((8, HEAD_D), F32)
                for s in range(8):
                    dec = jnp.exp(jnp.where(sub >= s, bb - bb[s:s + 1, :], -jnp.inf))
                    a_col = jnp.sum(dec * qb * kb[s:s + 1, :], axis=-1, keepdims=True)
                    ob = ob + a_col * ib[s:s + 1, :]
                diag.append(ob)
            o = o + jnp.concatenate(diag, axis=0)
            b_last = b[c - 1:c, :]
            st_ref[h] = s_t * jnp.exp(b_last) + _mm_tn(iv, k * jnp.exp(b_last - b))
            o_s[pl.ds(r0, c), sl] = o
        return carry

    lax.fori_loop(0, tb // c, chunk_body, 0)
    o_ref[0] = _head_rms_gate(o_s[...], g_ref[0].astype(F32), nw_ref[...], HEAD_D).astype(o_ref.dtype)


def _hgrn(proj, lb_logits, norm_w, *, tb, layer, col0):
    b, t, _ = proj.shape
    w = N_HEADS * HEAD_D
    specs = [pl.BlockSpec((1, tb, w), functools.partial(lambda i, j, off: (i, j, off), off=col0 + n))
             for n in range(4)]
    return pl.pallas_call(
        functools.partial(_hgrn_kernel, tb=tb, layer=layer),
        grid=(b, t // tb),
        in_specs=specs + [
            pl.BlockSpec(lb_logits.shape, lambda i, j: (0, 0)),
            pl.BlockSpec((1, HEAD_D), lambda i, j: (0, 0)),
        ],
        out_specs=pl.BlockSpec((1, tb, w), lambda i, j: (i, j, 0)),
        out_shape=jax.ShapeDtypeStruct((b, t, w), BF16),
        scratch_shapes=[pltpu.VMEM((tb, w), F32)] * 5 + [pltpu.VMEM((N_HEADS, HEAD_D, HEAD_D), F32)],
        compiler_params=pltpu.CompilerParams(
            dimension_semantics=("parallel", "arbitrary"), vmem_limit_bytes=VMEM_LIMIT),
        name="hgrn2",
    )(proj, proj, proj, proj, lb_logits, norm_w)


def _ret_kernel(q_ref, k_ref, v_ref, g_ref, cos_ref, sin_ref, nw_ref, o_ref, dmat_s, st_ref, *, tc):
    hh = pl.program_id(1).astype(F32)
    log_gamma = jnp.log(1.0 - jnp.exp((jnp.full((1, 1), -5.0, F32) - hh) * math.log(2.0)))

    @pl.when(pl.program_id(2) == 0)
    def _():
        st_ref[...] = jnp.zeros(st_ref.shape, F32)
        rel = (_iota2((tc, tc), 0) - _iota2((tc, tc), 1)).astype(F32)
        dmat_s[...] = jnp.where(rel >= 0, jnp.exp(jnp.maximum(rel, 0.0) * log_gamma), 0.0)

    pos = _iota2((tc, 1), 0).astype(F32)
    q_decay = jnp.exp((pos + 1.0) * log_gamma)
    k_decay = jnp.exp((tc - 1.0 - pos) * log_gamma)
    chunk_decay = jnp.exp(tc * log_gamma)

    cos = cos_ref[...]
    sin = sin_ref[...]
    half = RET_DK // 2

    def rot(x):
        x1, x2 = x[:, :half], x[:, half:]
        return jnp.concatenate([x1 * cos - x2 * sin, x1 * sin + x2 * cos], axis=1)

    q = rot(q_ref[0].astype(F32))
    k = rot(k_ref[0].astype(F32)) * (RET_DK ** -0.5)
    v = v_ref[0]
    s = st_ref[...]
    attn = _mm_nt(q, k) * dmat_s[...]
    o = _mm(q, s) * q_decay + _mm(attn, v)
    st_ref[...] = s * chunk_decay + _mm_tn(k * k_decay, v)
    ms = jnp.mean(o * o, axis=-1, keepdims=True)
    o_ref[0] = (o * lax.rsqrt(ms + RMS_EPS) * nw_ref[...] * _silu(g_ref[0].astype(F32))).astype(o_ref.dtype)


def _retention(proj, cos, sin, norm_w, *, tc):
    b, t, _ = proj.shape
    qb = N_HEADS
    vb = 2 * N_HEADS * RET_DK // RET_DV
    gb = vb + N_HEADS
    return pl.pallas_call(
        functools.partial(_ret_kernel, tc=tc),
        grid=(b, N_HEADS, t // tc),
        in_specs=[
            pl.BlockSpec((1, tc, RET_DK), lambda i, h, j: (i, j, h)),
            pl.BlockSpec((1, tc, RET_DK), lambda i, h, j: (i, j, qb + h)),
            pl.BlockSpec((1, tc, RET_DV), lambda i, h, j: (i, j, vb + h)),
            pl.BlockSpec((1, tc, RET_DV), lambda i, h, j: (i, j, gb + h)),
            pl.BlockSpec((tc, RET_DK // 2), lambda i, h, j: (j, 0)),
            pl.BlockSpec((tc, RET_DK // 2), lambda i, h, j: (j, 0)),
            pl.BlockSpec((1, RET_DV), lambda i, h, j: (0, 0)),
        ],
        out_specs=pl.BlockSpec((1, tc, RET_DV), lambda i, h, j: (i, j, h)),
        out_shape=jax.ShapeDtypeStruct((b, t, N_HEADS * RET_DV), BF16),
        scratch_shapes=[pltpu.VMEM((tc, tc), F32), pltpu.VMEM((RET_DK, RET_DV), F32)],
        compiler_params=pltpu.CompilerParams(
            dimension_semantics=("parallel", "parallel", "arbitrary"), vmem_limit_bytes=VMEM_LIMIT),
        name="retention",
    )(proj, proj, proj, proj, cos, sin, norm_w)


def _route_gates(logits_t):
    cl = [logits_t[g:g + 1, :] for g in range(N_GROUPS)]
    cmax = functools.reduce(jnp.maximum, cl)
    denom = sum(jnp.exp(x - cmax) for x in cl)
    g_prob = 1.0 / denom
    g_idx = jnp.full(cmax.shape, N_GROUPS - 1, jnp.int32)
    for g in range(N_GROUPS - 2, -1, -1):
        g_idx = jnp.where(cl[g] == cmax, g, g_idx)
    def fine_row(g, j):
        r = N_GROUPS + g * EXPERTS_PER_GROUP + j
        return logits_t[r:r + 1, :]

    fl = []
    for j in range(EXPERTS_PER_GROUP):
        x = fine_row(N_GROUPS - 1, j)
        for g in range(N_GROUPS - 2, -1, -1):
            x = jnp.where(g_idx == g, fine_row(g, j), x)
        fl.append(x)
    m1 = functools.reduce(jnp.maximum, fl)
    i1 = jnp.full(m1.shape, EXPERTS_PER_GROUP - 1, jnp.int32)
    for j in range(EXPERTS_PER_GROUP - 2, -1, -1):
        i1 = jnp.where(fl[j] == m1, j, i1)
    rest = [jnp.where(i1 == j, -jnp.inf, fl[j]) for j in range(EXPERTS_PER_GROUP)]
    m2 = functools.reduce(jnp.maximum, rest)
    i2 = jnp.full(m2.shape, EXPERTS_PER_GROUP - 1, jnp.int32)
    for j in range(EXPERTS_PER_GROUP - 2, -1, -1):
        i2 = jnp.where(jnp.logical_and(rest[j] == m2, i1 != j), j, i2)
    e2 = jnp.exp(m2 - m1)
    w1 = g_prob / (1.0 + e2)
    w2 = g_prob * e2 / (1.0 + e2)
    local = [jnp.where(i1 == j, w1, 0.0) + jnp.where(i2 == j, w2, 0.0) for j in range(EXPERTS_PER_GROUP)]
    return g_idx, local


def _outproj_kernel(a_ref, b_ref, wa_ref, wb_ref, x_ref, nw_ref, rw_ref, rb_ref,
                    x1_ref, h_ref, gk_ref, krow_ref, cnt_ref):
    x1 = (x_ref[...]
          + jnp.dot(a_ref[...], wa_ref[...], preferred_element_type=F32)
          + jnp.dot(b_ref[...], wb_ref[...], preferred_element_type=F32))
    x1_ref[...] = x1
    ms = jnp.mean(x1 * x1, axis=-1, keepdims=True)
    h = x1 * lax.rsqrt(ms + RMS_EPS) * nw_ref[...]
    h_ref[...] = h.astype(BF16)
    rw = rw_ref[...]
    hh = h.astype(BF16)
    hl = (h - hh.astype(F32)).astype(BF16)
    wh = rw.astype(BF16)
    wl = (rw - wh.astype(F32)).astype(BF16)
    logits = (jnp.dot(hh, wh, preferred_element_type=F32)
              + jnp.dot(hl, wh, preferred_element_type=F32)
              + jnp.dot(hh, wl, preferred_element_type=F32)) + rb_ref[...]
    g_idx, local = _route_gates(logits.T)
    tm = g_idx.shape[1]
    mem = (_iota2((8, tm), 0) == g_idx).astype(BF16)
    s_i = _iota2((tm, tm), 0)
    t_i = _iota2((tm, tm), 1)
    sub_shift = int(math.log2(MOE_SUB))
    before = jnp.logical_and(s_i < t_i, (s_i >> sub_shift) == (t_i >> sub_shift)).astype(BF16)
    rank = jnp.sum(mem.astype(F32) * jnp.dot(mem, before, preferred_element_type=F32), axis=0, keepdims=True)
    key = g_idx.astype(F32) * MOE_KEY_STRIDE + rank
    krow_ref[...] = key
    sub_sel = ((_iota2((tm, LANES), 0) >> sub_shift) == _iota2((tm, LANES), 1)).astype(BF16)
    cnt_ref[...] = jnp.dot(mem, sub_sel, preferred_element_type=F32)
    rows = jnp.concatenate([key] + local + [jnp.zeros((LANES - 1 - EXPERTS_PER_GROUP, tm), F32)], axis=0)
    gk_ref[...] = rows.T


def _outproj_route(a, b, a_blk, b_blk, wa, wb, x, nw, rw, rb, *, tm, name):
    n, d = x.shape
    ka = wa.shape[0]
    kb = wb.shape[0]
    return pl.pallas_call(
        _outproj_kernel,
        grid=(n // tm,),
        in_specs=[
            pl.BlockSpec((tm, ka), lambda i: (i, a_blk)),
            pl.BlockSpec((tm, kb), lambda i: (i, b_blk)),
            pl.BlockSpec((ka, d), lambda i: (0, 0)),
            pl.BlockSpec((kb, d), lambda i: (0, 0)),
            pl.BlockSpec((tm, d), lambda i: (i, 0)),
            pl.BlockSpec((1, d), lambda i: (0, 0)),
            pl.BlockSpec((d, LANES), lambda i: (0, 0)),
            pl.BlockSpec((1, LANES), lambda i: (0, 0)),
        ],
        out_specs=[
            pl.BlockSpec((tm, d), lambda i: (i, 0)),
            pl.BlockSpec((tm, d), lambda i: (i, 0)),
            pl.BlockSpec((tm, LANES), lambda i: (i, 0)),
            pl.BlockSpec((1, tm), lambda i: (0, i)),
            pl.BlockSpec((8, LANES), lambda i: (i, 0)),
        ],
        out_shape=[jax.ShapeDtypeStruct((n, d), F32),
                   jax.ShapeDtypeStruct((n, d), BF16),
                   jax.ShapeDtypeStruct((n, LANES), F32),
                   jax.ShapeDtypeStruct((1, n), F32),
                   jax.ShapeDtypeStruct((8 * (n // tm), LANES), F32)],
        compiler_params=pltpu.CompilerParams(
            dimension_semantics=("parallel",), vmem_limit_bytes=VMEM_LIMIT),
        name=name,
    )(a, b, wa, wb, x, nw, rw, rb)


def _round_up(v, m):
    return ((v + (m - 1)) // m) * m


def _moe_kernel(cnt_ref, h_ref, krow_ref, gk_ref, wgu_ref, wd_ref, x_ref, fnw_ref, o_ref,
                hc_s, gc_s, y_s, *, final_norm):
    i = pl.program_id(0)
    g = pl.program_id(1)
    tm = h_ref.shape[0]
    n_sub = tm // MOE_SUB

    def segments(grp):
        cnts = [cnt_ref[(i * n_sub + s) * N_GROUPS + grp] for s in range(n_sub)]
        starts = [jnp.int32(0)]
        for s in range(n_sub):
            starts.append(starts[-1] + _round_up(cnts[s], MOE_ALIGN))
        return cnts, starts

    counts, offs = segments(g)
    total = offs[-1]
    key0 = g.astype(F32) * MOE_KEY_STRIDE
    sub_iota = _iota2((MOE_BLK, MOE_SUB), 0).astype(F32)
    lane_iota = _iota2((MOE_SUB, MOE_BLK), 1).astype(F32)

    for s in range(n_sub):
        rows = slice(s * MOE_SUB, (s + 1) * MOE_SUB)

        def pack(w, carry, s=s, rows=rows):
            base = key0 + (w * MOE_BLK).astype(F32)
            sel = (krow_ref[:, rows] == sub_iota + base).astype(BF16)
            dst = pl.ds(pl.multiple_of(offs[s] + w * MOE_BLK, MOE_ALIGN), MOE_BLK)
            hc_s[dst, :] = jnp.dot(sel, h_ref[rows, :], preferred_element_type=F32).astype(BF16)
            gc_s[dst, :] = _mm_01(sel, gk_ref[rows, :])
            return carry

        lax.fori_loop(0, (counts[s] + MOE_BLK - 1) // MOE_BLK, pack, 0)

    tail_rows = MOE_FIRST[-1] + MOE_FIRST_STEP
    tail = pl.ds(pl.multiple_of(total, MOE_ALIGN), tail_rows)
    hc_s[tail, :] = jnp.zeros((tail_rows, hc_s.shape[1]), BF16)
    gc_s[tail, :] = jnp.zeros((tail_rows, LANES), F32)

    def expert_rows(start, rows):
        blk = pl.ds(pl.multiple_of(start, MOE_FIRST_STEP), rows)
        hb = hc_s[blk, :]
        gates = gc_s[blk, :]
        y = jnp.zeros((rows, o_ref.shape[1]), F32)
        for e in range(EXPERTS_PER_GROUP):
            gu = jnp.dot(hb, wgu_ref[0, e], preferred_element_type=F32)
            act = _silu(gu[:, :D_EXPERT]) * gu[:, D_EXPERT:] * gates[:, 1 + e:2 + e]
            y = y + jnp.dot(act.astype(BF16), wd_ref[0, e], preferred_element_type=F32)
        y_s[g, blk, :] = y.astype(BF16)

    first = jnp.clip(_round_up(total, MOE_FIRST_STEP), MOE_FIRST[0], MOE_FIRST[-1])
    for size in MOE_FIRST:
        @pl.when(first == size)
        def _(size=size):
            expert_rows(0, size)

    n_rest = (jnp.maximum(total - first, 0) + MOE_BLK - 1) // MOE_BLK

    def rest(bi, carry):
        expert_rows(first + bi * MOE_BLK, MOE_BLK)
        return carry

    lax.fori_loop(0, n_rest, rest, 0)
    done = pl.ds(pl.multiple_of(first + n_rest * MOE_BLK, MOE_FIRST_STEP), MOE_BLK)
    y_s[g, done, :] = jnp.zeros((MOE_BLK, y_s.shape[2]), BF16)

    @pl.when(g == N_GROUPS - 1)
    def _():
        segs = [segments(grp) for grp in range(N_GROUPS)]
        for s in range(n_sub):
            rows = slice(s * MOE_SUB, (s + 1) * MOE_SUB)
            key_col = gk_ref[rows, 0:1]
            sel = jnp.concatenate(
                [(key_col == lane_iota + grp * MOE_KEY_STRIDE).astype(BF16) for grp in range(N_GROUPS)], axis=1)
            packed = jnp.concatenate(
                [y_s[grp, pl.ds(pl.multiple_of(segs[grp][1][s], MOE_ALIGN), MOE_BLK), :]
                 for grp in range(N_GROUPS)], axis=0)
            o_ref[rows, :] = x_ref[rows, :] + jnp.dot(sel, packed, preferred_element_type=F32)
            for grp in range(N_GROUPS):
                def more(w, carry, s=s, rows=rows, grp=grp, key_col=key_col):
                    base = grp * MOE_KEY_STRIDE + (w * MOE_BLK).astype(F32)
                    sel_w = (key_col == lane_iota + base).astype(BF16)
                    src = pl.ds(pl.multiple_of(segs[grp][1][s] + w * MOE_BLK, MOE_ALIGN), MOE_BLK)
                    o_ref[rows, :] += jnp.dot(sel_w, y_s[grp, src, :], preferred_element_type=F32)
                    return carry

                lax.fori_loop(1, (segs[grp][0][s] + MOE_BLK - 1) // MOE_BLK, more, 0)
        if final_norm:
            y = o_ref[...]
            ms = jnp.mean(y * y, axis=-1, keepdims=True)
            o_ref[...] = y * lax.rsqrt(ms + RMS_EPS) * fnw_ref[...]


def _moe(h, krow, gk, counts, wgu, wd, x, fnw, *, layer, tm, final_norm, name):
    n, d = x.shape
    buf_rows = _round_up(tm + (tm // MOE_SUB) * MOE_ALIGN + MOE_FIRST[-1] + MOE_FIRST_STEP, MOE_BLK)
    grid_spec = pltpu.PrefetchScalarGridSpec(
        num_scalar_prefetch=1,
        grid=(n // tm, N_GROUPS),
        in_specs=[
            pl.BlockSpec((tm, d), lambda i, g, c: (i, 0)),
            pl.BlockSpec((1, tm), lambda i, g, c: (0, i)),
            pl.BlockSpec((tm, LANES), lambda i, g, c: (i, 0)),
            pl.BlockSpec((1, EXPERTS_PER_GROUP, d, 2 * D_EXPERT), lambda i, g, c: (layer, g, 0, 0)),
            pl.BlockSpec((1, EXPERTS_PER_GROUP, D_EXPERT, d), lambda i, g, c: (layer, g, 0, 0)),
            pl.BlockSpec((tm, d), lambda i, g, c: (i, 0)),
            pl.BlockSpec((1, d), lambda i, g, c: (0, 0)),
        ],
        out_specs=pl.BlockSpec((tm, d), lambda i, g, c: (i, 0)),
        scratch_shapes=[pltpu.VMEM((buf_rows, d), BF16),
                        pltpu.VMEM((buf_rows, LANES), F32),
                        pltpu.VMEM((N_GROUPS, buf_rows, d), BF16)],
    )
    return pl.pallas_call(
        functools.partial(_moe_kernel, final_norm=final_norm),
        grid_spec=grid_spec,
        out_shape=jax.ShapeDtypeStruct((n, d), F32),
        compiler_params=pltpu.CompilerParams(
            dimension_semantics=("parallel", "arbitrary"), vmem_limit_bytes=VMEM_LIMIT),
        name=name,
    )(counts, h, krow, gk, wgu, wd, x, fnw)


def _pad_cols(a, width):
    return jnp.pad(a, ((0, 0), (0, width - a.shape[1])))


def _dispatch_counts(cnt, route_tm):
    c = cnt.reshape(-1, 8, LANES)[:, :N_GROUPS, :route_tm // MOE_SUB]
    return jnp.transpose(c, (0, 2, 1)).reshape(-1).astype(jnp.int32)


def _router_params(wc, bc, wf, bf):
    rw = _pad_cols(jnp.concatenate([wc, wf], axis=1), LANES)
    rb = _pad_cols(jnp.concatenate([bc, bf])[None, :], LANES)
    return rw, rb


def kernel(x, norm_mix_w, norm_ffn_w, even_w_in, gdn_conv_w, gdn_a_log, gdn_dt_bias, gdn_norm_w, hgrn_lb_logits, hgrn_norm_w, even_w_out, odd_w_in, ret_norm_w, odd_w_out, router_c_w, router_c_b, router_f_w, router_f_b, moe_w_gate_up, moe_w_down, final_norm_w):
    bsz, seq, d = x.shape
    n = bsz * seq
    xt = x.reshape(n, d)
    mix_w = N_HEADS * HEAD_D
    conv_cols = 3 * mix_w
    gdn_main = conv_cols + mix_w

    w_in = even_w_in[0]
    small0 = gdn_main
    w_main = jnp.concatenate([w_in[:, :small0], w_in[:, small0 + 2 * N_HEADS:]], axis=1).astype(BF16)
    w_small = jnp.concatenate(
        [_pad_cols(w_in[:, small0:small0 + N_HEADS], LANES),
         _pad_cols(w_in[:, small0 + N_HEADS:small0 + 2 * N_HEADS], LANES)], axis=1).astype(BF16)
    proj, small = _norm_matmul(xt, norm_mix_w[0][None, :], w_main, w_small, tm=512, tn=1024, name="in_proj_even")
    proj = proj.reshape(bsz, seq, -1)
    small = small.reshape(bsz, seq, -1)
    o_a = _gdn(proj, small, gdn_conv_w[0], _pad_cols(gdn_a_log[0][None, :], LANES),
               _pad_cols(gdn_dt_bias[0][None, :], LANES), gdn_norm_w[0][None, :], tb=512)
    o_b = _hgrn(proj, hgrn_lb_logits, hgrn_norm_w[0][None, :], tb=512, layer=0, col0=gdn_main // mix_w)
    w_out = even_w_out[0].astype(BF16)
    rw, rb = _router_params(router_c_w[0], router_c_b[0], router_f_w[0], router_f_b[0])
    wgu_b = moe_w_gate_up.astype(BF16)
    wd_b = moe_w_down.astype(BF16)
    x1, h, gk, krow, cnt = _outproj_route(o_a.reshape(n, mix_w), o_b.reshape(n, mix_w), 0, 0,
                                          w_out[:mix_w], w_out[mix_w:], xt, norm_ffn_w[0][None, :], rw, rb,
                                          tm=ROUTE_TM, name="out_proj_even")
    x2 = _moe(h, krow, gk, _dispatch_counts(cnt, ROUTE_TM), wgu_b, wd_b, x1,
              final_norm_w[None, :], layer=0, tm=MOE_TM, final_norm=False, name="moe0")

    perm = np.concatenate([np.arange(0, RET_DK, 2), np.arange(1, RET_DK, 2)])
    qk_perm = np.concatenate([hd * RET_DK + perm for hd in range(2 * N_HEADS)])
    w_odd = odd_w_in[0]
    w_odd = jnp.concatenate([w_odd[:, qk_perm], w_odd[:, 2 * N_HEADS * RET_DK:]], axis=1).astype(BF16)
    (proj1,) = _norm_matmul(x2, norm_mix_w[1][None, :], w_odd, None, tm=512, tn=1024, name="in_proj_odd")
    inv = 1.0 / (ROPE_BASE ** jnp.linspace(0.0, 1.0, RET_DK // 2, dtype=F32))
    ang = jnp.arange(seq, dtype=F32)[:, None] * inv[None, :]
    o_c = _retention(proj1.reshape(bsz, seq, -1), jnp.cos(ang), jnp.sin(ang), ret_norm_w[0][None, :],
                     tc=RET_CHUNK)
    o_c = o_c.reshape(n, -1)
    w_out1 = odd_w_out[0].astype(BF16)
    half = w_out1.shape[0] // 2
    rw, rb = _router_params(router_c_w[1], router_c_b[1], router_f_w[1], router_f_b[1])
    x3, h, gk, krow, cnt = _outproj_route(o_c, o_c, 0, 1, w_out1[:half], w_out1[half:], x2,
                                          norm_ffn_w[1][None, :], rw, rb, tm=ROUTE_TM, name="out_proj_odd")
    out = _moe(h, krow, gk, _dispatch_counts(cnt, ROUTE_TM), wgu_b, wd_b, x3,
               final_norm_w[None, :], layer=1, tm=MOE_TM, final_norm=True, name="moe1")
    return out.reshape(bsz, seq, d)
```

```python
import functools
import math

import numpy as np
import jax
import jax.numpy as jnp
from jax import lax
from jax.experimental import pallas as pl
from jax.experimental.pallas import tpu as pltpu

F32 = jnp.float32
BF16 = jnp.bfloat16

D_MODEL = 1024
RMS_EPS = 1e-6
L2_EPS = 1e-6
CHUNK = 64
CONV_K = 4
N_HEADS = 4
HEAD_D = 128
RET_DK = 256
RET_DV = 512
RET_CHUNK = 256
ROPE_BASE = 10000.0
N_GROUPS = 4
EXPERTS_PER_GROUP = 4
N_EXPERTS = 16
D_EXPERT = 256
LANES = 128
ROUTE_TM = 1024
MOE_TM = 1024
MOE_SUB = 256
MOE_BLK = 128
MOE_ALIGN = 16
MOE_KEY_STRIDE = 4096.0
MOE_FIRST_STEP = 32
MOE_FIRST = (256, 288, 320, 352)
VMEM_LIMIT = 56 * 1024 * 1024


def _mm(a, b):
    return jnp.dot(a.astype(BF16), b.astype(BF16), preferred_element_type=F32)


def _mm_nt(a, b):
    return lax.dot_general(a.astype(BF16), b.astype(BF16), (((1,), (1,)), ((), ())),
                           preferred_element_type=F32)


def _mm_tn(a, b):
    return _mm(a.T, b)


def _mm_01(m01, x):
    hi = x.astype(BF16)
    lo = (x - hi.astype(F32)).astype(BF16)
    return (jnp.dot(m01, hi, preferred_element_type=F32)
            + jnp.dot(m01, lo, preferred_element_type=F32))


def _bmm(a, b):
    return jnp.einsum('cik,ckj->cij', a.astype(BF16), b.astype(BF16), preferred_element_type=F32)


def _bmm_nt(a, b):
    return jnp.einsum('cik,cjk->cij', a.astype(BF16), b.astype(BF16), preferred_element_type=F32)


def _bmm_tn(a, b):
    return _bmm(jnp.swapaxes(a, 1, 2), b)


def _bmm_01(m01, x):
    hi = x.astype(BF16)
    lo = (x - hi.astype(F32)).astype(BF16)
    return (jnp.einsum('cik,ckj->cij', m01, hi, preferred_element_type=F32)
            + jnp.einsum('cik,ckj->cij', m01, lo, preferred_element_type=F32))


def _sigmoid(x):
    return 1.0 / (1.0 + jnp.exp(-x))


def _silu(x):
    return x * _sigmoid(x)


def _softplus(x):
    return jnp.maximum(x, 0.0) + jnp.log(1.0 + jnp.exp(-jnp.abs(x)))


def _iota2(shape, dim):
    return lax.broadcasted_iota(jnp.int32, shape, dim)


def _head_rms_gate(o, gate, nw, width):
    outs = []
    for h in range(o.shape[1] // width):
        oh = o[:, h * width:(h + 1) * width]
        ms = jnp.mean(oh * oh, axis=-1, keepdims=True)
        outs.append(oh * lax.rsqrt(ms + RMS_EPS) * nw)
    return jnp.concatenate(outs, axis=1) * _silu(gate)


def _norm_mm_kernel(*refs, tn, has_small):
    if has_small:
        x_ref, nw_ref, w_ref, ws_ref, o_ref, os_ref = refs
    else:
        x_ref, nw_ref, w_ref, o_ref = refs
    x = x_ref[...]
    ms = jnp.mean(x * x, axis=-1, keepdims=True)
    hb = (x * lax.rsqrt(ms + RMS_EPS) * nw_ref[...]).astype(BF16)
    if has_small:
        os_ref[...] = jnp.dot(hb, ws_ref[...], preferred_element_type=F32)
    for j in range(w_ref.shape[1] // tn):
        cols = slice(j * tn, (j + 1) * tn)
        o_ref[:, cols] = jnp.dot(hb, w_ref[:, cols], preferred_element_type=F32).astype(o_ref.dtype)


def _norm_matmul(x, nw, w, ws, *, tm, tn, name):
    n, d = x.shape
    nout = w.shape[1]
    has_small = ws is not None
    in_specs = [
        pl.BlockSpec((tm, d), lambda i: (i, 0)),
        pl.BlockSpec((1, d), lambda i: (0, 0)),
        pl.BlockSpec((d, nout), lambda i: (0, 0)),
    ]
    out_specs = [pl.BlockSpec((tm, nout), lambda i: (i, 0))]
    out_shape = [jax.ShapeDtypeStruct((n, nout), BF16)]
    args = [x, nw, w]
    if has_small:
        nsmall = ws.shape[1]
        in_specs.append(pl.BlockSpec((d, nsmall), lambda i: (0, 0)))
        out_specs.append(pl.BlockSpec((tm, nsmall), lambda i: (i, 0)))
        out_shape.append(jax.ShapeDtypeStruct((n, nsmall), F32))
        args.append(ws)
    return pl.pallas_call(
        functools.partial(_norm_mm_kernel, tn=tn, has_small=has_small),
        grid=(n // tm,),
        in_specs=in_specs,
        out_specs=out_specs,
        out_shape=out_shape,
        compiler_params=pltpu.CompilerParams(
            dimension_semantics=("parallel",), vmem_limit_bytes=VMEM_LIMIT),
        name=name,
    )(*args)


def _gdn_kernel(qkv_ref, z_ref, sm_ref, cw_ref, alog_ref, dtb_ref, nw_ref, o_ref,
                xs_ref, qkv_s, bg_s, o_s, m_s, sq_s, qe_s, dec_s, st_ref, *, tb):
    @pl.when(pl.program_id(1) == 0)
    def _():
        xs_ref[0:8, :] = jnp.zeros((8, xs_ref.shape[1]), F32)
        st_ref[...] = jnp.zeros(st_ref.shape, F32)

    x = qkv_ref[0].astype(F32)
    xs_ref[8:8 + tb, :] = x
    cw = cw_ref[...]
    y = x * cw[CONV_K - 1:CONV_K, :]
    for j in range(1, CONV_K):
        y = y + xs_ref[pl.ds(8 - j, tb), :] * cw[CONV_K - 1 - j:CONV_K - j, :]
    xs_ref[0:8, :] = xs_ref[tb:tb + 8, :]
    y = _silu(y)

    qk_w = N_HEADS * HEAD_D
    for h in range(N_HEADS):
        for base, scale in ((0, HEAD_D ** -0.5), (qk_w, 1.0)):
            c0 = base + h * HEAD_D
            v = y[:, c0:c0 + HEAD_D]
            inv = lax.rsqrt(jnp.sum(v * v, axis=-1, keepdims=True) + L2_EPS)
            qkv_s[:, c0:c0 + HEAD_D] = v * (inv * scale)
    qkv_s[:, 2 * qk_w:] = y[:, 2 * qk_w:]

    sm = sm_ref[0]
    bg_s[0] = _sigmoid(sm[:, :LANES])
    bg_s[1] = -jnp.exp(alog_ref[...]) * _softplus(sm[:, LANES:] + dtb_ref[...])

    c = CHUNK
    nc = tb // c
    row = _iota2((c, c), 0)
    col = _iota2((c, c), 1)
    causal = (row >= col)[None]
    strict = (row > col)[None]
    same16 = ((row >> 4) == (col >> 4))[None]
    same32 = ((row >> 5) == (col >> 5))[None]
    eye = (row == col).astype(F32)[None]
    ltri = jnp.broadcast_to((row >= col).astype(BF16)[None], (nc, c, c))
    lane = _iota2((nc, c, LANES), 2)

    gc_all = _bmm_01(ltri, bg_s[1].reshape(nc, c, LANES))
    beta_all = bg_s[0].reshape(nc, c, LANES)
    for h in range(N_HEADS):
        hs = slice(h * HEAD_D, (h + 1) * HEAD_D)
        q = qkv_s[:, h * HEAD_D:(h + 1) * HEAD_D].reshape(nc, c, HEAD_D)
        k = qkv_s[:, qk_w + h * HEAD_D:qk_w + (h + 1) * HEAD_D].reshape(nc, c, HEAD_D)
        v = qkv_s[:, 2 * qk_w + h * HEAD_D:2 * qk_w + (h + 1) * HEAD_D].reshape(nc, c, HEAD_D)
        beta = beta_all[:, :, h:h + 1]
        gcol = gc_all[:, :, h:h + 1]
        g_hi = gcol.astype(BF16).astype(F32)
        g_mid = (gcol - g_hi).astype(BF16).astype(F32)
        g_lo = gcol - g_hi - g_mid
        ones_hi = jnp.where(lane < 6, 1.0, 0.0)
        lhs = jnp.where(lane == 0, g_hi, jnp.where(lane == 1, g_mid, jnp.where(lane == 2, g_lo, ones_hi)))
        rhs_g = jnp.where(lane == 3, -g_hi, jnp.where(lane == 4, -g_mid, jnp.where(lane == 5, -g_lo, ones_hi)))
        decay = jnp.exp(jnp.where(causal, _bmm_nt(lhs, rhs_g), -jnp.inf))
        kb = k * beta
        a = jnp.where(strict, _bmm_nt(kb, k) * decay, 0.0)
        d = jnp.where(same16, a, 0.0)
        x_inv = eye - d
        dp = d
        for _ in range(3):
            dp = _bmm(dp, dp)
            x_inv = x_inv + _bmm(x_inv, dp)
        e = jnp.where(jnp.logical_and(same32, jnp.logical_not(same16)), a, 0.0)
        x_inv = x_inv - _bmm(x_inv, _bmm(e, x_inv))
        f = jnp.where(same32, 0.0, a)
        x_inv = x_inv - _bmm(x_inv, _bmm(f, x_inv))
        wu = _bmm(x_inv, jnp.concatenate([kb * jnp.exp(gcol), v * beta], axis=2))
        attn = _bmm_nt(q, k) * decay
        g_last = gcol[:, c - 1:c, :]
        kd = k * jnp.exp(g_last - gcol)
        mq = _bmm_tn(kd, wu)
        aw = _bmm(attn, wu)
        m_s[h] = mq[:, :, :HEAD_D].astype(BF16)
        sq_s[h] = mq[:, :, HEAD_D:]
        qe_s[h] = (q * jnp.exp(gcol) - aw[:, :, :HEAD_D]).astype(BF16)
        dec_s[h] = jnp.broadcast_to(jnp.exp(g_last), (nc, 1, HEAD_D))
        o_s[:, hs] = aw[:, :, HEAD_D:].reshape(tb, HEAD_D)

    for ci in range(nc):
        for h in range(N_HEADS):
            hs = slice(h * HEAD_D, (h + 1) * HEAD_D)
            s_h = st_ref[h]
            s_b = s_h.astype(BF16)
            o_s[ci * c:(ci + 1) * c, hs] += jnp.dot(qe_s[h, ci], s_b, preferred_element_type=F32)
            st_ref[h] = (s_h * dec_s[h, ci] - jnp.dot(m_s[h, ci], s_b, preferred_element_type=F32)
                         + sq_s[h, ci])

    o_ref[0] = _head_rms_gate(o_s[...], z_ref[0].astype(F32), nw_ref[...], HEAD_D).astype(o_ref.dtype)


def _gdn(proj, small, conv_w, a_log, dt_bias, norm_w, *, tb):
    b, t, _ = proj.shape
    conv_cols = 3 * N_HEADS * HEAD_D
    mix_w = N_HEADS * HEAD_D
    return pl.pallas_call(
        functools.partial(_gdn_kernel, tb=tb),
        grid=(b, t // tb),
        in_specs=[
            pl.BlockSpec((1, tb, conv_cols), lambda i, j: (i, j, 0)),
            pl.BlockSpec((1, tb, mix_w), lambda i, j: (i, j, conv_cols // mix_w)),
            pl.BlockSpec((1, tb, 2 * LANES), lambda i, j: (i, j, 0)),
            pl.BlockSpec((CONV_K, conv_cols), lambda i, j: (0, 0)),
            pl.BlockSpec((1, LANES), lambda i, j: (0, 0)),
            pl.BlockSpec((1, LANES), lambda i, j: (0, 0)),
            pl.BlockSpec((1, HEAD_D), lambda i, j: (0, 0)),
        ],
        out_specs=pl.BlockSpec((1, tb, mix_w), lambda i, j: (i, j, 0)),
        out_shape=jax.ShapeDtypeStruct((b, t, mix_w), BF16),
        scratch_shapes=[
            pltpu.VMEM((tb + 8, conv_cols), F32),
            pltpu.VMEM((tb, conv_cols), F32),
            pltpu.VMEM((2, tb, LANES), F32),
            pltpu.VMEM((tb, mix_w), F32),
            pltpu.VMEM((N_HEADS, tb // CHUNK, HEAD_D, HEAD_D), BF16),
            pltpu.VMEM((N_HEADS, tb // CHUNK, HEAD_D, HEAD_D), F32),
            pltpu.VMEM((N_HEADS, tb // CHUNK, CHUNK, HEAD_D), BF16),
            pltpu.VMEM((N_HEADS, tb // CHUNK, 1, HEAD_D), F32),
            pltpu.VMEM((N_HEADS, HEAD_D, HEAD_D), F32),
        ],
        compiler_params=pltpu.CompilerParams(
            dimension_semantics=("parallel", "arbitrary"), vmem_limit_bytes=VMEM_LIMIT),
        name="gdn",
    )(proj, proj, small, conv_w, a_log, dt_bias, norm_w)


def _hgrn_kernel(q_ref, f_ref, i_ref, g_ref, lb_ref, nw_ref, o_ref,
                 q_s, k_s, lf_s, i_s, o_s, st_ref, *, tb, layer):
    @pl.when(pl.program_id(1) == 0)
    def _():
        st_ref[...] = jnp.zeros(st_ref.shape, F32)

    lbl = lb_ref[...]
    e_lb = jnp.exp(lbl - jnp.max(lbl, axis=0, keepdims=True))
    lb = jnp.sum(e_lb[:layer + 1], axis=0, keepdims=True) / jnp.sum(e_lb, axis=0, keepdims=True)

    f = lb + (1.0 - lb) * _sigmoid(f_ref[0].astype(F32))
    k_s[...] = 1.0 - f
    lf_s[...] = jnp.log(f)
    q_s[...] = _silu(q_ref[0].astype(F32)) * (HEAD_D ** -0.5)
    i_s[...] = i_ref[0].astype(F32)

    c = CHUNK
    row = _iota2((c, c), 0)
    col = _iota2((c, c), 1)
    ltri = (row >= col).astype(BF16)
    level_masks = {}
    for m in (32, 16, 8):
        sh = int(math.log2(2 * m))
        level_masks[m] = jnp.logical_and(
            (row >> sh) == (col >> sh),
            jnp.logical_and((row & (2 * m - 1)) >= m, (col & (2 * m - 1)) < m))
    sub = _iota2((8, HEAD_D), 0)

    def chunk_body(ci, carry):
        r0 = pl.multiple_of(ci * c, c)
        b_all = _mm_01(ltri, lf_s[pl.ds(r0, c), :])
        for h in range(N_HEADS):
            sl = slice(h * HEAD_D, (h + 1) * HEAD_D)
            b = b_all[:, sl]
            q = q_s[pl.ds(r0, c), sl]
            k = k_s[pl.ds(r0, c), sl]
            iv = i_s[pl.ds(r0, c), sl]
            attn = jnp.zeros((c, c), F32)
            for m in (32, 16, 8):
                ref = jnp.concatenate(
                    [jnp.broadcast_to(b[r:r + 1, :], (2 * m, HEAD_D)) for r in range(m, c, 2 * m)], axis=0)
                e = jnp.exp(-jnp.abs(b - ref))
                attn = attn + jnp.where(level_masks[m], _mm_nt(q * e, k * e), 0.0)
            s_t = st_ref[h]
            o = _mm_nt(q * jnp.exp(b), s_t) + _mm(attn, iv)
            diag = []
            for r in range(0, c, 8):
                qb, kb, bb, ib = q[r:r + 8], k[r:r + 8], b[r:r + 8], iv[r:r + 8]
                ob = jnp.zeros((8, HEAD_D), F32)
                for s in range(8):
                    dec = jnp.exp(jnp.where(sub >= s, bb - bb[s:s + 1, :], -jnp.inf))
                    a_col = jnp.sum(dec * qb * kb[s:s + 1, :], axis=-1, keepdims=True)
                    ob = ob + a_col * ib[s:s + 1, :]
                diag.append(ob)
            o = o + jnp.concatenate(diag, axis=0)
            b_last = b[c - 1:c, :]
            st_ref[h] = s_t * jnp.exp(b_last) + _mm_tn(iv, k * jnp.exp(b_last - b))
            o_s[pl.ds(r0, c), sl] = o
        return carry

    lax.fori_loop(0, tb // c, chunk_body, 0)
    o_ref[0] = _head_rms_gate(o_s[...], g_ref[0].astype(F32), nw_ref[...], HEAD_D).astype(o_ref.dtype)


def _hgrn(proj, lb_logits, norm_w, *, tb, layer, col0):
    b, t, _ = proj.shape
    w = N_HEADS * HEAD_D
    specs = [pl.BlockSpec((1, tb, w), functools.partial(lambda i, j, off: (i, j, off), off=col0 + n))
             for n in range(4)]
    return pl.pallas_call(
        functools.partial(_hgrn_kernel, tb=tb, layer=layer),
        grid=(b, t // tb),
        in_specs=specs + [
            pl.BlockSpec(lb_logits.shape, lambda i, j: (0, 0)),
            pl.BlockSpec((1, HEAD_D), lambda i, j: (0, 0)),
        ],
        out_specs=pl.BlockSpec((1, tb, w), lambda i, j: (i, j, 0)),
        out_shape=jax.ShapeDtypeStruct((b, t, w), BF16),
        scratch_shapes=[pltpu.VMEM((tb, w), F32)] * 5 + [pltpu.VMEM((N_HEADS, HEAD_D, HEAD_D), F32)],
        compiler_params=pltpu.CompilerParams(
            dimension_semantics=("parallel", "arbitrary"), vmem_limit_bytes=VMEM_LIMIT),
        name="hgrn2",
    )(proj, proj, proj, proj, lb_logits, norm_w)


def _ret_kernel(q_ref, k_ref, v_ref, g_ref, cos_ref, sin_ref, nw_ref, o_ref, dmat_s, st_ref, *, tc):
    log_gammas = [math.log(1.0 - 2.0 ** (-5.0 - h)) for h in range(N_HEADS)]

    @pl.when(pl.program_id(1) == 0)
    def _():
        st_ref[...] = jnp.zeros(st_ref.shape, F32)
        rel = (_iota2((tc, tc), 0) - _iota2((tc, tc), 1)).astype(F32)
        for h in range(N_HEADS):
            dmat_s[h] = jnp.where(rel >= 0, jnp.exp(jnp.maximum(rel, 0.0) * log_gammas[h]), 0.0)

    pos = _iota2((tc, 1), 0).astype(F32)
    cos = cos_ref[...]
    sin = sin_ref[...]
    half = RET_DK // 2

    def rot(x):
        x1, x2 = x[:, :half], x[:, half:]
        return jnp.concatenate([x1 * cos - x2 * sin, x1 * sin + x2 * cos], axis=1)

    for h in range(N_HEADS):
        lg = log_gammas[h]
        qk_cols = slice(h * RET_DK, (h + 1) * RET_DK)
        v_cols = slice(h * RET_DV, (h + 1) * RET_DV)
        q = rot(q_ref[0, :, qk_cols].astype(F32))
        k = rot(k_ref[0, :, qk_cols].astype(F32)) * (RET_DK ** -0.5)
        v = v_ref[0, :, v_cols]
        s = st_ref[h]
        attn = _mm_nt(q, k) * dmat_s[h]
        o = _mm(q * jnp.exp((pos + 1.0) * lg), s) + _mm(attn, v)
        st_ref[h] = s * math.exp(tc * lg) + _mm_tn(k * jnp.exp((tc - 1.0 - pos) * lg), v)
        ms = jnp.mean(o * o, axis=-1, keepdims=True)
        o_ref[0, :, v_cols] = (o * lax.rsqrt(ms + RMS_EPS) * nw_ref[...]
                               * _silu(g_ref[0, :, v_cols].astype(F32))).astype(o_ref.dtype)


def _retention(proj, cos, sin, norm_w, *, tc):
    b, t, _ = proj.shape
    qk_w = N_HEADS * RET_DK
    v_w = N_HEADS * RET_DV
    return pl.pallas_call(
        functools.partial(_ret_kernel, tc=tc),
        grid=(b, t // tc),
        in_specs=[
            pl.BlockSpec((1, tc, qk_w), lambda i, j: (i, j, 0)),
            pl.BlockSpec((1, tc, qk_w), lambda i, j: (i, j, 1)),
            pl.BlockSpec((1, tc, v_w), lambda i, j: (i, j, 1)),
            pl.BlockSpec((1, tc, v_w), lambda i, j: (i, j, 2)),
            pl.BlockSpec((tc, RET_DK // 2), lambda i, j: (j, 0)),
            pl.BlockSpec((tc, RET_DK // 2), lambda i, j: (j, 0)),
            pl.BlockSpec((1, RET_DV), lambda i, j: (0, 0)),
        ],
        out_specs=pl.BlockSpec((1, tc, v_w), lambda i, j: (i, j, 0)),
        out_shape=jax.ShapeDtypeStruct((b, t, v_w), BF16),
        scratch_shapes=[pltpu.VMEM((N_HEADS, tc, tc), F32), pltpu.VMEM((N_HEADS, RET_DK, RET_DV), F32)],
        compiler_params=pltpu.CompilerParams(
            dimension_semantics=("parallel", "arbitrary"), vmem_limit_bytes=VMEM_LIMIT),
        name="retention",
    )(proj, proj, proj, proj, cos, sin, norm_w)


def _route_gates(logits_t):
    cl = [logits_t[g:g + 1, :] for g in range(N_GROUPS)]
    cmax = functools.reduce(jnp.maximum, cl)
    denom = sum(jnp.exp(x - cmax) for x in cl)
    g_prob = 1.0 / denom
    g_idx = jnp.full(cmax.shape, N_GROUPS - 1, jnp.int32)
    for g in range(N_GROUPS - 2, -1, -1):
        g_idx = jnp.where(cl[g] == cmax, g, g_idx)
    def fine_row(g, j):
        r = N_GROUPS + g * EXPERTS_PER_GROUP + j
        return logits_t[r:r + 1, :]

    fl = []
    for j in range(EXPERTS_PER_GROUP):
        x = fine_row(N_GROUPS - 1, j)
        for g in range(N_GROUPS - 2, -1, -1):
            x = jnp.where(g_idx == g, fine_row(g, j), x)
        fl.append(x)
    m1 = functools.reduce(jnp.maximum, fl)
    i1 = jnp.full(m1.shape, EXPERTS_PER_GROUP - 1, jnp.int32)
    for j in range(EXPERTS_PER_GROUP - 2, -1, -1):
        i1 = jnp.where(fl[j] == m1, j, i1)
    rest = [jnp.where(i1 == j, -jnp.inf, fl[j]) for j in range(EXPERTS_PER_GROUP)]
    m2 = functools.reduce(jnp.maximum, rest)
    i2 = jnp.full(m2.shape, EXPERTS_PER_GROUP - 1, jnp.int32)
    for j in range(EXPERTS_PER_GROUP - 2, -1, -1):
        i2 = jnp.where(jnp.logical_and(rest[j] == m2, i1 != j), j, i2)
    e2 = jnp.exp(m2 - m1)
    w1 = g_prob / (1.0 + e2)
    w2 = g_prob * e2 / (1.0 + e2)
    local = [jnp.where(i1 == j, w1, 0.0) + jnp.where(i2 == j, w2, 0.0) for j in range(EXPERTS_PER_GROUP)]
    return g_idx, local


def _outproj_kernel(a_ref, b_ref, wa_ref, wb_ref, x_ref, nw_ref, rw_ref, rb_ref,
                    x1_ref, h_ref, gk_ref, krow_ref, cnt_ref):
    x1 = (x_ref[...]
          + jnp.dot(a_ref[...], wa_ref[...], preferred_element_type=F32)
          + jnp.dot(b_ref[...], wb_ref[...], preferred_element_type=F32))
    x1_ref[...] = x1
    ms = jnp.mean(x1 * x1, axis=-1, keepdims=True)
    h = x1 * lax.rsqrt(ms + RMS_EPS) * nw_ref[...]
    h_ref[...] = h.astype(BF16)
    rw = rw_ref[...]
    hh = h.astype(BF16)
    hl = (h - hh.astype(F32)).astype(BF16)
    wh = rw.astype(BF16)
    wl = (rw - wh.astype(F32)).astype(BF16)
    hi_part = jnp.dot(hh, jnp.concatenate([wh, wl], axis=1), preferred_element_type=F32)
    logits = (hi_part[:, :LANES] + hi_part[:, LANES:]
              + jnp.dot(hl, wh, preferred_element_type=F32)) + rb_ref[...]
    g_idx, local = _route_gates(logits.T)
    tm = g_idx.shape[1]
    mem = (_iota2((8, tm), 0) == g_idx).astype(BF16)
    before = (_iota2((MOE_SUB, MOE_SUB), 0) < _iota2((MOE_SUB, MOE_SUB), 1)).astype(BF16)
    earlier = jnp.concatenate(
        [jnp.dot(mem[:, s:s + MOE_SUB], before, preferred_element_type=F32) for s in range(0, tm, MOE_SUB)],
        axis=1)
    rank = jnp.sum(mem.astype(F32) * earlier, axis=0, keepdims=True)
    key = g_idx.astype(F32) * MOE_KEY_STRIDE + rank
    krow_ref[...] = key
    sub_shift = int(math.log2(MOE_SUB))
    sub_sel = ((_iota2((tm, LANES), 0) >> sub_shift) == _iota2((tm, LANES), 1)).astype(BF16)
    cnt_ref[...] = jnp.dot(mem, sub_sel, preferred_element_type=F32)
    rows = jnp.concatenate([key] + local + [jnp.zeros((LANES - 1 - EXPERTS_PER_GROUP, tm), F32)], axis=0)
    gk_ref[...] = rows.T


def _outproj_route(a, b, a_blk, b_blk, wa, wb, x, nw, rw, rb, *, tm, name):
    n, d = x.shape
    ka = wa.shape[0]
    kb = wb.shape[0]
    return pl.pallas_call(
        _outproj_kernel,
        grid=(n // tm,),
        in_specs=[
            pl.BlockSpec((tm, ka), lambda i: (i, a_blk)),
            pl.BlockSpec((tm, kb), lambda i: (i, b_blk)),
            pl.BlockSpec((ka, d), lambda i: (0, 0)),
            pl.BlockSpec((kb, d), lambda i: (0, 0)),
            pl.BlockSpec((tm, d), lambda i: (i, 0)),
            pl.BlockSpec((1, d), lambda i: (0, 0)),
            pl.BlockSpec((d, LANES), lambda i: (0, 0)),
            pl.BlockSpec((1, LANES), lambda i: (0, 0)),
        ],
        out_specs=[
            pl.BlockSpec((tm, d), lambda i: (i, 0)),
            pl.BlockSpec((tm, d), lambda i: (i, 0)),
            pl.BlockSpec((tm, LANES), lambda i: (i, 0)),
            pl.BlockSpec((1, tm), lambda i: (0, i)),
            pl.BlockSpec((8, LANES), lambda i: (i, 0)),
        ],
        out_shape=[jax.ShapeDtypeStruct((n, d), F32),
                   jax.ShapeDtypeStruct((n, d), BF16),
                   jax.ShapeDtypeStruct((n, LANES), F32),
                   jax.ShapeDtypeStruct((1, n), F32),
                   jax.ShapeDtypeStruct((8 * (n // tm), LANES), F32)],
        compiler_params=pltpu.CompilerParams(
            dimension_semantics=("parallel",), vmem_limit_bytes=VMEM_LIMIT),
        name=name,
    )(a, b, wa, wb, x, nw, rw, rb)


def _round_up(v, m):
    return ((v + (m - 1)) // m) * m


def _moe_kernel(cnt_ref, h_ref, krow_ref, gk_ref, wgu_ref, wd_ref, x_ref, fnw_ref, o_ref,
                hc_s, gc_s, y_s, *, final_norm):
    i = pl.program_id(0)
    g = pl.program_id(1)
    tm = h_ref.shape[0]
    n_sub = tm // MOE_SUB

    def segments(grp):
        cnts = [cnt_ref[(i * n_sub + s) * N_GROUPS + grp] for s in range(n_sub)]
        starts = [jnp.int32(0)]
        for s in range(n_sub):
            starts.append(starts[-1] + _round_up(cnts[s], MOE_ALIGN))
        return cnts, starts

    counts, offs = segments(g)
    total = offs[-1]
    key0 = g.astype(F32) * MOE_KEY_STRIDE
    sub_iota = _iota2((MOE_BLK, MOE_SUB), 0).astype(F32)
    lane_iota = _iota2((MOE_SUB, MOE_BLK), 1).astype(F32)

    for s in range(n_sub):
        rows = slice(s * MOE_SUB, (s + 1) * MOE_SUB)

        def pack(w, carry, s=s, rows=rows):
            base = key0 + (w * MOE_BLK).astype(F32)
            sel = (krow_ref[:, rows] == sub_iota + base).astype(BF16)
            dst = pl.ds(pl.multiple_of(offs[s] + w * MOE_BLK, MOE_ALIGN), MOE_BLK)
            hc_s[dst, :] = jnp.dot(sel, h_ref[rows, :], preferred_element_type=F32).astype(BF16)
            gc_s[dst, :] = _mm_01(sel, gk_ref[rows, :])
            return carry

        lax.fori_loop(0, (counts[s] + MOE_BLK - 1) // MOE_BLK, pack, 0)

    tail_rows = MOE_FIRST[-1] + MOE_FIRST_STEP
    tail = pl.ds(pl.multiple_of(total, MOE_ALIGN), tail_rows)
    hc_s[tail, :] = jnp.zeros((tail_rows, hc_s.shape[1]), BF16)
    gc_s[tail, :] = jnp.zeros((tail_rows, LANES), F32)

    def expert_rows(start, rows):
        blk = pl.ds(pl.multiple_of(start, MOE_FIRST_STEP), rows)
        hb = hc_s[blk, :]
        gates = gc_s[blk, :]
        y = jnp.zeros((rows, o_ref.shape[1]), F32)
        for e in range(EXPERTS_PER_GROUP):
            gu = jnp.dot(hb, wgu_ref[0, e], preferred_element_type=F32)
            act = _silu(gu[:, :D_EXPERT]) * gu[:, D_EXPERT:] * gates[:, 1 + e:2 + e]
            y = y + jnp.dot(act.astype(BF16), wd_ref[0, e], preferred_element_type=F32)
        y_s[g, blk, :] = y.astype(BF16)

    first = jnp.clip(_round_up(total, MOE_FIRST_STEP), MOE_FIRST[0], MOE_FIRST[-1])
    for size in MOE_FIRST:
        @pl.when(first == size)
        def _(size=size):
            expert_rows(0, size)

    n_rest = (jnp.maximum(total - first, 0) + MOE_BLK - 1) // MOE_BLK

    def rest(bi, carry):
        expert_rows(first + bi * MOE_BLK, MOE_BLK)
        return carry

    lax.fori_loop(0, n_rest, rest, 0)
    done = pl.ds(pl.multiple_of(first + n_rest * MOE_BLK, MOE_FIRST_STEP), MOE_BLK)
    y_s[g, done, :] = jnp.zeros((MOE_BLK, y_s.shape[2]), BF16)

    @pl.when(g == N_GROUPS - 1)
    def _():
        segs = [segments(grp) for grp in range(N_GROUPS)]
        for s in range(n_sub):
            rows = slice(s * MOE_SUB, (s + 1) * MOE_SUB)
            key_col = gk_ref[rows, 0:1]
            sel = jnp.concatenate(
                [(key_col == lane_iota + grp * MOE_KEY_STRIDE).astype(BF16) for grp in range(N_GROUPS)], axis=1)
            packed = jnp.concatenate(
                [y_s[grp, pl.ds(pl.multiple_of(segs[grp][1][s], MOE_ALIGN), MOE_BLK), :]
                 for grp in range(N_GROUPS)], axis=0)
            o_ref[rows, :] = x_ref[rows, :] + jnp.dot(sel, packed, preferred_element_type=F32)
            for grp in range(N_GROUPS):
                def more(w, carry, s=s, rows=rows, grp=grp, key_col=key_col):
                    base = grp * MOE_KEY_STRIDE + (w * MOE_BLK).astype(F32)
                    sel_w = (key_col == lane_iota + base).astype(BF16)
                    src = pl.ds(pl.multiple_of(segs[grp][1][s] + w * MOE_BLK, MOE_ALIGN), MOE_BLK)
                    o_ref[rows, :] += jnp.dot(sel_w, y_s[grp, src, :], preferred_element_type=F32)
                    return carry

                lax.fori_loop(1, (segs[grp][0][s] + MOE_BLK - 1) // MOE_BLK, more, 0)
        if final_norm:
            y = o_ref[...]
            ms = jnp.mean(y * y, axis=-1, keepdims=True)
            o_ref[...] = y * lax.rsqrt(ms + RMS_EPS) * fnw_ref[...]


def _moe(h, krow, gk, counts, wgu, wd, x, fnw, *, layer, tm, final_norm, name):
    n, d = x.shape
    buf_rows = _round_up(tm + (tm // MOE_SUB) * MOE_ALIGN + MOE_FIRST[-1] + MOE_FIRST_STEP, MOE_BLK)
    grid_spec = pltpu.PrefetchScalarGridSpec(
        num_scalar_prefetch=1,
        grid=(n // tm, N_GROUPS),
        in_specs=[
            pl.BlockSpec((tm, d), lambda i, g, c: (i, 0)),
            pl.BlockSpec((1, tm), lambda i, g, c: (0, i)),
            pl.BlockSpec((tm, LANES), lambda i, g, c: (i, 0)),
            pl.BlockSpec((1, EXPERTS_PER_GROUP, d, 2 * D_EXPERT), lambda i, g, c: (layer, g, 0, 0)),
            pl.BlockSpec((1, EXPERTS_PER_GROUP, D_EXPERT, d), lambda i, g, c: (layer, g, 0, 0)),
            pl.BlockSpec((tm, d), lambda i, g, c: (i, 0)),
            pl.BlockSpec((1, d), lambda i, g, c: (0, 0)),
        ],
        out_specs=pl.BlockSpec((tm, d), lambda i, g, c: (i, 0)),
        scratch_shapes=[pltpu.VMEM((buf_rows, d), BF16),
                        pltpu.VMEM((buf_rows, LANES), F32),
                        pltpu.VMEM((N_GROUPS, buf_rows, d), BF16)],
    )
    return pl.pallas_call(
        functools.partial(_moe_kernel, final_norm=final_norm),
        grid_spec=grid_spec,
        out_shape=jax.ShapeDtypeStruct((n, d), F32),
        compiler_params=pltpu.CompilerParams(
            dimension_semantics=("parallel", "arbitrary"), vmem_limit_bytes=VMEM_LIMIT),
        name=name,
    )(counts, h, krow, gk, wgu, wd, x, fnw)


def _pad_cols(a, width):
    return jnp.pad(a, ((0, 0), (0, width - a.shape[1])))


def _dispatch_counts(cnt, route_tm):
    c = cnt.reshape(-1, 8, LANES)[:, :N_GROUPS, :route_tm // MOE_SUB]
    return jnp.transpose(c, (0, 2, 1)).reshape(-1).astype(jnp.int32)


def _router_params(wc, bc, wf, bf):
    rw = _pad_cols(jnp.concatenate([wc, wf], axis=1), LANES)
    rb = _pad_cols(jnp.concatenate([bc, bf])[None, :], LANES)
    return rw, rb


def kernel(x, norm_mix_w, norm_ffn_w, even_w_in, gdn_conv_w, gdn_a_log, gdn_dt_bias, gdn_norm_w, hgrn_lb_logits, hgrn_norm_w, even_w_out, odd_w_in, ret_norm_w, odd_w_out, router_c_w, router_c_b, router_f_w, router_f_b, moe_w_gate_up, moe_w_down, final_norm_w):
    bsz, seq, d = x.shape
    n = bsz * seq
    xt = x.reshape(n, d)
    mix_w = N_HEADS * HEAD_D
    conv_cols = 3 * mix_w
    gdn_main = conv_cols + mix_w

    w_in = even_w_in[0]
    small0 = gdn_main
    w_main = jnp.concatenate([w_in[:, :small0], w_in[:, small0 + 2 * N_HEADS:]], axis=1).astype(BF16)
    w_small = jnp.concatenate(
        [_pad_cols(w_in[:, small0:small0 + N_HEADS], LANES),
         _pad_cols(w_in[:, small0 + N_HEADS:small0 + 2 * N_HEADS], LANES)], axis=1).astype(BF16)
    proj, small = _norm_matmul(xt, norm_mix_w[0][None, :], w_main, w_small, tm=512, tn=1024, name="in_proj_even")
    proj = proj.reshape(bsz, seq, -1)
    small = small.reshape(bsz, seq, -1)
    o_a = _gdn(proj, small, gdn_conv_w[0], _pad_cols(gdn_a_log[0][None, :], LANES),
               _pad_cols(gdn_dt_bias[0][None, :], LANES), gdn_norm_w[0][None, :], tb=512)
    o_b = _hgrn(proj, hgrn_lb_logits, hgrn_norm_w[0][None, :], tb=512, layer=0, col0=gdn_main // mix_w)
    w_out = even_w_out[0].astype(BF16)
    rw, rb = _router_params(router_c_w[0], router_c_b[0], router_f_w[0], router_f_b[0])
    wgu_b = moe_w_gate_up.astype(BF16)
    wd_b = moe_w_down.astype(BF16)
    x1, h, gk, krow, cnt = _outproj_route(o_a.reshape(n, mix_w), o_b.reshape(n, mix_w), 0, 0,
                                          w_out[:mix_w], w_out[mix_w:], xt, norm_ffn_w[0][None, :], rw, rb,
                                          tm=ROUTE_TM, name="out_proj_even")
    x2 = _moe(h, krow, gk, _dispatch_counts(cnt, ROUTE_TM), wgu_b, wd_b, x1,
              final_norm_w[None, :], layer=0, tm=MOE_TM, final_norm=False, name="moe0")

    perm = np.concatenate([np.arange(0, RET_DK, 2), np.arange(1, RET_DK, 2)])
    qk_perm = np.concatenate([hd * RET_DK + perm for hd in range(2 * N_HEADS)])
    w_odd = odd_w_in[0]
    w_odd = jnp.concatenate([w_odd[:, qk_perm], w_odd[:, 2 * N_HEADS * RET_DK:]], axis=1).astype(BF16)
    (proj1,) = _norm_matmul(x2, norm_mix_w[1][None, :], w_odd, None, tm=512, tn=1024, name="in_proj_odd")
    inv = 1.0 / (ROPE_BASE ** jnp.linspace(0.0, 1.0, RET_DK // 2, dtype=F32))
    ang = jnp.arange(seq, dtype=F32)[:, None] * inv[None, :]
    o_c = _retention(proj1.reshape(bsz, seq, -1), jnp.cos(ang), jnp.sin(ang), ret_norm_w[0][None, :],
                     tc=RET_CHUNK)
    o_c = o_c.reshape(n, -1)
    w_out1 = odd_w_out[0].astype(BF16)
    half = w_out1.shape[0] // 2
    rw, rb = _router_params(router_c_w[1], router_c_b[1], router_f_w[1], router_f_b[1])
    x3, h, gk, krow, cnt = _outproj_route(o_c, o_c, 0, 1, w_out1[:half], w_out1[half:], x2,
                                          norm_ffn_w[1][None, :], rw, rb, tm=ROUTE_TM, name="out_proj_odd")
    out = _moe(h, krow, gk, _dispatch_counts(cnt, ROUTE_TM), wgu_b, wd_b, x3,
               final_norm_w[None, :], layer=1, tm=MOE_TM, final_norm=True, name="moe1")
    return out.reshape(bsz, seq, d)
```

```python
import functools
import math

import numpy as np
import jax
import jax.numpy as jnp
from jax import lax
from jax.experimental import pallas as pl
from jax.experimental.pallas import tpu as pltpu

F32 = jnp.float32
BF16 = jnp.bfloat16

D_MODEL = 1024
RMS_EPS = 1e-6
L2_EPS = 1e-6
CHUNK = 64
CONV_K = 4
N_HEADS = 4
HEAD_D = 128
RET_DK = 256
RET_DV = 512
RET_CHUNK = 256
ROPE_BASE = 10000.0
N_GROUPS = 4
EXPERTS_PER_GROUP = 4
N_EXPERTS = 16
D_EXPERT = 256
LANES = 128
ROUTE_TM = 1024
MOE_TM = 1024
MOE_SUB = 256
MOE_BLK = 128
MOE_ALIGN = 16
MOE_KEY_STRIDE = 4096.0
MOE_FIRST_STEP = 32
MOE_FIRST = (256, 288, 320, 352)
VMEM_LIMIT = 56 * 1024 * 1024


def _mm(a, b):
    return jnp.dot(a.astype(BF16), b.astype(BF16), preferred_element_type=F32)


def _mm_nt(a, b):
    return lax.dot_general(a.astype(BF16), b.astype(BF16), (((1,), (1,)), ((), ())),
                           preferred_element_type=F32)


def _mm_tn(a, b):
    return _mm(a.T, b)


def _mm_01(m01, x):
    hi = x.astype(BF16)
    lo = (x - hi.astype(F32)).astype(BF16)
    return (jnp.dot(m01, hi, preferred_element_type=F32)
            + jnp.dot(m01, lo, preferred_element_type=F32))


def _bmm(a, b):
    return jnp.einsum('cik,ckj->cij', a.astype(BF16), b.astype(BF16), preferred_element_type=F32)


def _bmm_nt(a, b):
    return jnp.einsum('cik,cjk->cij', a.astype(BF16), b.astype(BF16), preferred_element_type=F32)


def _bmm_tn(a, b):
    return _bmm(jnp.swapaxes(a, 1, 2), b)


def _bmm_01(m01, x):
    hi = x.astype(BF16)
    lo = (x - hi.astype(F32)).astype(BF16)
    return (jnp.einsum('cik,ckj->cij', m01, hi, preferred_element_type=F32)
            + jnp.einsum('cik,ckj->cij', m01, lo, preferred_element_type=F32))


def _sigmoid(x):
    return 1.0 / (1.0 + jnp.exp(-x))


def _silu(x):
    return x * _sigmoid(x)


def _softplus(x):
    return jnp.maximum(x, 0.0) + jnp.log(1.0 + jnp.exp(-jnp.abs(x)))


def _iota2(shape, dim):
    return lax.broadcasted_iota(jnp.int32, shape, dim)


def _head_rms_gate(o, gate, nw, width):
    outs = []
    for h in range(o.shape[1] // width):
        oh = o[:, h * width:(h + 1) * width]
        ms = jnp.mean(oh * oh, axis=-1, keepdims=True)
        outs.append(oh * lax.rsqrt(ms + RMS_EPS) * nw)
    return jnp.concatenate(outs, axis=1) * _silu(gate)


def _norm_mm_kernel(*refs, tn, has_small):
    if has_small:
        x_ref, nw_ref, w_ref, ws_ref, o_ref, os_ref = refs
    else:
        x_ref, nw_ref, w_ref, o_ref = refs
    x = x_ref[...]
    ms = jnp.mean(x * x, axis=-1, keepdims=True)
    hb = (x * lax.rsqrt(ms + RMS_EPS) * nw_ref[...]).astype(BF16)
    if has_small:
        os_ref[...] = jnp.dot(hb, ws_ref[...], preferred_element_type=F32)
    for j in range(w_ref.shape[1] // tn):
        cols = slice(j * tn, (j + 1) * tn)
        o_ref[:, cols] = jnp.dot(hb, w_ref[:, cols], preferred_element_type=F32).astype(o_ref.dtype)


def _norm_matmul(x, nw, w, ws, *, tm, tn, name):
    n, d = x.shape
    nout = w.shape[1]
    has_small = ws is not None
    in_specs = [
        pl.BlockSpec((tm, d), lambda i: (i, 0)),
        pl.BlockSpec((1, d), lambda i: (0, 0)),
        pl.BlockSpec((d, nout), lambda i: (0, 0)),
    ]
    out_specs = [pl.BlockSpec((tm, nout), lambda i: (i, 0))]
    out_shape = [jax.ShapeDtypeStruct((n, nout), BF16)]
    args = [x, nw, w]
    if has_small:
        nsmall = ws.shape[1]
        in_specs.append(pl.BlockSpec((d, nsmall), lambda i: (0, 0)))
        out_specs.append(pl.BlockSpec((tm, nsmall), lambda i: (i, 0)))
        out_shape.append(jax.ShapeDtypeStruct((n, nsmall), F32))
        args.append(ws)
    return pl.pallas_call(
        functools.partial(_norm_mm_kernel, tn=tn, has_small=has_small),
        grid=(n // tm,),
        in_specs=in_specs,
        out_specs=out_specs,
        out_shape=out_shape,
        compiler_params=pltpu.CompilerParams(
            dimension_semantics=("parallel",), vmem_limit_bytes=VMEM_LIMIT),
        name=name,
    )(*args)


def _gdn_kernel(qkv_ref, z_ref, sm_ref, cw_ref, alog_ref, dtb_ref, nw_ref, o_ref,
                xs_ref, qkv_s, bg_s, o_s, m_s, sq_s, qe_s, dec_s, st_ref, *, tb):
    @pl.when(pl.program_id(1) == 0)
    def _():
        xs_ref[0:8, :] = jnp.zeros((8, xs_ref.shape[1]), F32)
        st_ref[...] = jnp.zeros(st_ref.shape, F32)

    x = qkv_ref[0].astype(F32)
    xs_ref[8:8 + tb, :] = x
    cw = cw_ref[...]
    y = x * cw[CONV_K - 1:CONV_K, :]
    for j in range(1, CONV_K):
        y = y + xs_ref[pl.ds(8 - j, tb), :] * cw[CONV_K - 1 - j:CONV_K - j, :]
    xs_ref[0:8, :] = xs_ref[tb:tb + 8, :]
    y = _silu(y)

    qk_w = N_HEADS * HEAD_D
    for h in range(N_HEADS):
        for base, scale in ((0, HEAD_D ** -0.5), (qk_w, 1.0)):
            c0 = base + h * HEAD_D
            v = y[:, c0:c0 + HEAD_D]
            inv = lax.rsqrt(jnp.sum(v * v, axis=-1, keepdims=True) + L2_EPS)
            qkv_s[:, c0:c0 + HEAD_D] = v * (inv * scale)
    qkv_s[:, 2 * qk_w:] = y[:, 2 * qk_w:]

    sm = sm_ref[0]
    bg_s[0] = _sigmoid(sm[:, :LANES])
    bg_s[1] = -jnp.exp(alog_ref[...]) * _softplus(sm[:, LANES:] + dtb_ref[...])

    c = CHUNK
    nc = tb // c
    row = _iota2((c, c), 0)
    col = _iota2((c, c), 1)
    causal = (row >= col)[None]
    strict = (row > col)[None]
    same16 = ((row >> 4) == (col >> 4))[None]
    same32 = ((row >> 5) == (col >> 5))[None]
    eye = (row == col).astype(F32)[None]
    ltri = jnp.broadcast_to((row >= col).astype(BF16)[None], (nc, c, c))
    lane = _iota2((nc, c, LANES), 2)

    gc_all = _bmm_01(ltri, bg_s[1].reshape(nc, c, LANES))
    beta_all = bg_s[0].reshape(nc, c, LANES)
    for h in range(N_HEADS):
        hs = slice(h * HEAD_D, (h + 1) * HEAD_D)
        q = qkv_s[:, h * HEAD_D:(h + 1) * HEAD_D].reshape(nc, c, HEAD_D)
        k = qkv_s[:, qk_w + h * HEAD_D:qk_w + (h + 1) * HEAD_D].reshape(nc, c, HEAD_D)
        v = qkv_s[:, 2 * qk_w + h * HEAD_D:2 * qk_w + (h + 1) * HEAD_D].reshape(nc, c, HEAD_D)
        beta = beta_all[:, :, h:h + 1]
        gcol = gc_all[:, :, h:h + 1]
        g_hi = gcol.astype(BF16).astype(F32)
        g_mid = (gcol - g_hi).astype(BF16).astype(F32)
        g_lo = gcol - g_hi - g_mid
        ones_hi = jnp.where(lane < 6, 1.0, 0.0)
        lhs = jnp.where(lane == 0, g_hi, jnp.where(lane == 1, g_mid, jnp.where(lane == 2, g_lo, ones_hi)))
        rhs_g = jnp.where(lane == 3, -g_hi, jnp.where(lane == 4, -g_mid, jnp.where(lane == 5, -g_lo, ones_hi)))
        decay = jnp.exp(jnp.where(causal, _bmm_nt(lhs, rhs_g), -jnp.inf))
        kb = k * beta
        a = jnp.where(strict, _bmm_nt(kb, k) * decay, 0.0)
        d = jnp.where(same16, a, 0.0)
        x_inv = eye - d
        dp = d
        for _ in range(3):
            dp = _bmm(dp, dp)
            x_inv = x_inv + _bmm(x_inv, dp)
        e = jnp.where(jnp.logical_and(same32, jnp.logical_not(same16)), a, 0.0)
        x_inv = x_inv - _bmm(x_inv, _bmm(e, x_inv))
        f = jnp.where(same32, 0.0, a)
        x_inv = x_inv - _bmm(x_inv, _bmm(f, x_inv))
        wu = _bmm(x_inv, jnp.concatenate([kb * jnp.exp(gcol), v * beta], axis=2))
        attn = _bmm_nt(q, k) * decay
        g_last = gcol[:, c - 1:c, :]
        kd = k * jnp.exp(g_last - gcol)
        mq = _bmm_tn(kd, wu)
        aw = _bmm(attn, wu)
        m_s[h] = mq[:, :, :HEAD_D].astype(BF16)
        sq_s[h] = mq[:, :, HEAD_D:]
        qe_s[h] = (q * jnp.exp(gcol) - aw[:, :, :HEAD_D]).astype(BF16)
        dec_s[h] = jnp.broadcast_to(jnp.exp(g_last), (nc, 1, HEAD_D))
        o_s[:, hs] = aw[:, :, HEAD_D:].reshape(tb, HEAD_D)

    for ci in range(nc):
        for h in range(N_HEADS):
            hs = slice(h * HEAD_D, (h + 1) * HEAD_D)
            s_h = st_ref[h]
            s_b = s_h.astype(BF16)
            o_s[ci * c:(ci + 1) * c, hs] += jnp.dot(qe_s[h, ci], s_b, preferred_element_type=F32)
            st_ref[h] = (s_h * dec_s[h, ci] - jnp.dot(m_s[h, ci], s_b, preferred_element_type=F32)
                         + sq_s[h, ci])

    o_ref[0] = _head_rms_gate(o_s[...], z_ref[0].astype(F32), nw_ref[...], HEAD_D).astype(o_ref.dtype)


def _gdn(proj, small, conv_w, a_log, dt_bias, norm_w, *, tb):
    b, t, _ = proj.shape
    conv_cols = 3 * N_HEADS * HEAD_D
    mix_w = N_HEADS * HEAD_D
    return pl.pallas_call(
        functools.partial(_gdn_kernel, tb=tb),
        grid=(b, t // tb),
        in_specs=[
            pl.BlockSpec((1, tb, conv_cols), lambda i, j: (i, j, 0)),
            pl.BlockSpec((1, tb, mix_w), lambda i, j: (i, j, conv_cols // mix_w)),
            pl.BlockSpec((1, tb, 2 * LANES), lambda i, j: (i, j, 0)),
            pl.BlockSpec((CONV_K, conv_cols), lambda i, j: (0, 0)),
            pl.BlockSpec((1, LANES), lambda i, j: (0, 0)),
            pl.BlockSpec((1, LANES), lambda i, j: (0, 0)),
            pl.BlockSpec((1, HEAD_D), lambda i, j: (0, 0)),
        ],
        out_specs=pl.BlockSpec((1, tb, mix_w), lambda i, j: (i, j, 0)),
        out_shape=jax.ShapeDtypeStruct((b, t, mix_w), BF16),
        scratch_shapes=[
            pltpu.VMEM((tb + 8, conv_cols), F32),
            pltpu.VMEM((tb, conv_cols), F32),
            pltpu.VMEM((2, tb, LANES), F32),
            pltpu.VMEM((tb, mix_w), F32),
            pltpu.VMEM((N_HEADS, tb // CHUNK, HEAD_D, HEAD_D), BF16),
            pltpu.VMEM((N_HEADS, tb // CHUNK, HEAD_D, HEAD_D), F32),
            pltpu.VMEM((N_HEADS, tb // CHUNK, CHUNK, HEAD_D), BF16),
            pltpu.VMEM((N_HEADS, tb // CHUNK, 1, HEAD_D), F32),
            pltpu.VMEM((N_HEADS, HEAD_D, HEAD_D), F32),
        ],
        compiler_params=pltpu.CompilerParams(
            dimension_semantics=("parallel", "arbitrary"), vmem_limit_bytes=VMEM_LIMIT),
        name="gdn",
    )(proj, proj, small, conv_w, a_log, dt_bias, norm_w)


def _hgrn_kernel(q_ref, f_ref, i_ref, g_ref, lb_ref, nw_ref, o_ref,
                 q_s, k_s, lf_s, i_s, o_s, st_ref, *, tb, layer):
    @pl.when(pl.program_id(1) == 0)
    def _():
        st_ref[...] = jnp.zeros(st_ref.shape, F32)

    lbl = lb_ref[...]
    e_lb = jnp.exp(lbl - jnp.max(lbl, axis=0, keepdims=True))
    lb = jnp.sum(e_lb[:layer + 1], axis=0, keepdims=True) / jnp.sum(e_lb, axis=0, keepdims=True)

    f = lb + (1.0 - lb) * _sigmoid(f_ref[0].astype(F32))
    k_s[...] = 1.0 - f
    lf_s[...] = jnp.log(f)
    q_s[...] = _silu(q_ref[0].astype(F32)) * (HEAD_D ** -0.5)
    i_s[...] = i_ref[0].astype(F32)

    c = CHUNK
    nc = tb // c
    blk = 8
    row = _iota2((c, c), 0)
    col = _iota2((c, c), 1)
    ltri = jnp.broadcast_to((row >= col).astype(BF16)[None], (nc, c, c))
    level_masks = {}
    for m in (32, 16, 8):
        sh = int(math.log2(2 * m))
        level_masks[m] = jnp.logical_and(
            (row >> sh) == (col >> sh),
            jnp.logical_and((row & (2 * m - 1)) >= m, (col & (2 * m - 1)) < m))[None]
    sub = _iota2((nc * c // blk, blk, HEAD_D), 1)

    b_all = _bmm_01(ltri, lf_s[...].reshape(nc, c, N_HEADS * HEAD_D))
    for h in range(N_HEADS):
        sl = slice(h * HEAD_D, (h + 1) * HEAD_D)
        b = b_all[:, :, sl]
        q = q_s[:, sl].reshape(nc, c, HEAD_D)
        k = k_s[:, sl].reshape(nc, c, HEAD_D)
        iv = i_s[:, sl].reshape(nc, c, HEAD_D)
        attn = jnp.zeros((nc, c, c), F32)
        for m in (32, 16, 8):
            b_m = b.reshape(nc * c // (2 * m), 2 * m, HEAD_D)
            ref = jnp.broadcast_to(b_m[:, m:m + 1, :], b_m.shape).reshape(nc, c, HEAD_D)
            e = jnp.exp(-jnp.abs(b - ref))
            attn = attn + jnp.where(level_masks[m], _bmm_nt(q * e, k * e), 0.0)
        o = _bmm(attn, iv)
        qb, kb, bb, ib = (a.reshape(nc * c // blk, blk, HEAD_D) for a in (q, k, b, iv))
        ob = jnp.zeros(qb.shape, F32)
        for s in range(blk):
            dec = jnp.exp(jnp.where(sub >= s, bb - bb[:, s:s + 1, :], -jnp.inf))
            a_col = jnp.sum(dec * qb * kb[:, s:s + 1, :], axis=-1, keepdims=True)
            ob = ob + a_col * ib[:, s:s + 1, :]
        o = o + ob.reshape(nc, c, HEAD_D)
        b_last = b[:, c - 1:c, :]
        q_in = (q * jnp.exp(b)).astype(BF16)
        upd = _bmm_tn(iv, k * jnp.exp(b_last - b))
        keep = jnp.exp(b_last)
        s_t = st_ref[h]
        outs = []
        for ci in range(nc):
            outs.append(o[ci] + _mm_nt(q_in[ci], s_t))
            s_t = s_t * keep[ci] + upd[ci]
        st_ref[h] = s_t
        o_s[:, sl] = jnp.concatenate(outs, axis=0)

    o_ref[0] = _head_rms_gate(o_s[...], g_ref[0].astype(F32), nw_ref[...], HEAD_D).astype(o_ref.dtype)


def _hgrn(proj, lb_logits, norm_w, *, tb, layer, col0):
    b, t, _ = proj.shape
    w = N_HEADS * HEAD_D
    specs = [pl.BlockSpec((1, tb, w), functools.partial(lambda i, j, off: (i, j, off), off=col0 + n))
             for n in range(4)]
    return pl.pallas_call(
        functools.partial(_hgrn_kernel, tb=tb, layer=layer),
        grid=(b, t // tb),
        in_specs=specs + [
            pl.BlockSpec(lb_logits.shape, lambda i, j: (0, 0)),
            pl.BlockSpec((1, HEAD_D), lambda i, j: (0, 0)),
        ],
        out_specs=pl.BlockSpec((1, tb, w), lambda i, j: (i, j, 0)),
        out_shape=jax.ShapeDtypeStruct((b, t, w), BF16),
        scratch_shapes=[pltpu.VMEM((tb, w), F32)] * 5 + [pltpu.VMEM((N_HEADS, HEAD_D, HEAD_D), F32)],
        compiler_params=pltpu.CompilerParams(
            dimension_semantics=("parallel", "arbitrary"), vmem_limit_bytes=VMEM_LIMIT),
        name="hgrn2",
    )(proj, proj, proj, proj, lb_logits, norm_w)


def _ret_kernel(q_ref, k_ref, v_ref, g_ref, cos_ref, sin_ref, nw_ref, o_ref, dmat_s, st_ref, *, tc):
    log_gammas = [math.log(1.0 - 2.0 ** (-5.0 - h)) for h in range(N_HEADS)]

    @pl.when(pl.program_id(1) == 0)
    def _():
        st_ref[...] = jnp.zeros(st_ref.shape, F32)
        rel = (_iota2((tc, tc), 0) - _iota2((tc, tc), 1)).astype(F32)
        for h in range(N_HEADS):
            dmat_s[h] = jnp.where(rel >= 0, jnp.exp(jnp.maximum(rel, 0.0) * log_gammas[h]), 0.0)

    pos = _iota2((tc, 1), 0).astype(F32)
    cos = cos_ref[...]
    sin = sin_ref[...]
    half = RET_DK // 2

    def rot(x):
        x1, x2 = x[:, :half], x[:, half:]
        return jnp.concatenate([x1 * cos - x2 * sin, x1 * sin + x2 * cos], axis=1)

    for h in range(N_HEADS):
        lg = log_gammas[h]
        qk_cols = slice(h * RET_DK, (h + 1) * RET_DK)
        v_cols = slice(h * RET_DV, (h + 1) * RET_DV)
        q = rot(q_ref[0, :, qk_cols].astype(F32))
        k = rot(k_ref[0, :, qk_cols].astype(F32)) * (RET_DK ** -0.5)
        v = v_ref[0, :, v_cols]
        s = st_ref[h]
        attn = _mm_nt(q, k) * dmat_s[h]
        o = _mm(q * jnp.exp((pos + 1.0) * lg), s) + _mm(attn, v)
        st_ref[h] = s * math.exp(tc * lg) + _mm_tn(k * jnp.exp((tc - 1.0 - pos) * lg), v)
        ms = jnp.mean(o * o, axis=-1, keepdims=True)
        o_ref[0, :, v_cols] = (o * lax.rsqrt(ms + RMS_EPS) * nw_ref[...]
                               * _silu(g_ref[0, :, v_cols].astype(F32))).astype(o_ref.dtype)


def _retention(proj, cos, sin, norm_w, *, tc):
    b, t, _ = proj.shape
    qk_w = N_HEADS * RET_DK
    v_w = N_HEADS * RET_DV
    return pl.pallas_call(
        functools.partial(_ret_kernel, tc=tc),
        grid=(b, t // tc),
        in_specs=[
            pl.BlockSpec((1, tc, qk_w), lambda i, j: (i, j, 0)),
            pl.BlockSpec((1, tc, qk_w), lambda i, j: (i, j, 1)),
            pl.BlockSpec((1, tc, v_w), lambda i, j: (i, j, 1)),
            pl.BlockSpec((1, tc, v_w), lambda i, j: (i, j, 2)),
            pl.BlockSpec((tc, RET_DK // 2), lambda i, j: (j, 0)),
            pl.BlockSpec((tc, RET_DK // 2), lambda i, j: (j, 0)),
            pl.BlockSpec((1, RET_DV), lambda i, j: (0, 0)),
        ],
        out_specs=pl.BlockSpec((1, tc, v_w), lambda i, j: (i, j, 0)),
        out_shape=jax.ShapeDtypeStruct((b, t, v_w), BF16),
        scratch_shapes=[pltpu.VMEM((N_HEADS, tc, tc), F32), pltpu.VMEM((N_HEADS, RET_DK, RET_DV), F32)],
        compiler_params=pltpu.CompilerParams(
            dimension_semantics=("parallel", "arbitrary"), vmem_limit_bytes=VMEM_LIMIT),
        name="retention",
    )(proj, proj, proj, proj, cos, sin, norm_w)


def _route_gates(logits_t):
    cl = [logits_t[g:g + 1, :] for g in range(N_GROUPS)]
    cmax = functools.reduce(jnp.maximum, cl)
    denom = sum(jnp.exp(x - cmax) for x in cl)
    g_prob = 1.0 / denom
    g_idx = jnp.full(cmax.shape, N_GROUPS - 1, jnp.int32)
    for g in range(N_GROUPS - 2, -1, -1):
        g_idx = jnp.where(cl[g] == cmax, g, g_idx)
    def fine_row(g, j):
        r = N_GROUPS + g * EXPERTS_PER_GROUP + j
        return logits_t[r:r + 1, :]

    fl = []
    for j in range(EXPERTS_PER_GROUP):
        x = fine_row(N_GROUPS - 1, j)
        for g in range(N_GROUPS - 2, -1, -1):
            x = jnp.where(g_idx == g, fine_row(g, j), x)
        fl.append(x)
    m1 = functools.reduce(jnp.maximum, fl)
    i1 = jnp.full(m1.shape, EXPERTS_PER_GROUP - 1, jnp.int32)
    for j in range(EXPERTS_PER_GROUP - 2, -1, -1):
        i1 = jnp.where(fl[j] == m1, j, i1)
    rest = [jnp.where(i1 == j, -jnp.inf, fl[j]) for j in range(EXPERTS_PER_GROUP)]
    m2 = functools.reduce(jnp.maximum, rest)
    i2 = jnp.full(m2.shape, EXPERTS_PER_GROUP - 1, jnp.int32)
    for j in range(EXPERTS_PER_GROUP - 2, -1, -1):
        i2 = jnp.where(jnp.logical_and(rest[j] == m2, i1 != j), j, i2)
    e2 = jnp.exp(m2 - m1)
    w1 = g_prob / (1.0 + e2)
    w2 = g_prob * e2 / (1.0 + e2)
    local = [jnp.where(i1 == j, w1, 0.0) + jnp.where(i2 == j, w2, 0.0) for j in range(EXPERTS_PER_GROUP)]
    return g_idx, local


def _outproj_kernel(a_ref, b_ref, wa_ref, wb_ref, x_ref, nw_ref, rw_ref, rb_ref,
                    x1_ref, h_ref, gk_ref, krow_ref, cnt_ref):
    x1 = (x_ref[...]
          + jnp.dot(a_ref[...], wa_ref[...], preferred_element_type=F32)
          + jnp.dot(b_ref[...], wb_ref[...], preferred_element_type=F32))
    x1_ref[...] = x1
    ms = jnp.mean(x1 * x1, axis=-1, keepdims=True)
    h = x1 * lax.rsqrt(ms + RMS_EPS) * nw_ref[...]
    h_ref[...] = h.astype(BF16)
    rw = rw_ref[...]
    hh = h.astype(BF16)
    hl = (h - hh.astype(F32)).astype(BF16)
    wh = rw.astype(BF16)
    wl = (rw - wh.astype(F32)).astype(BF16)
    hi_part = jnp.dot(hh, jnp.concatenate([wh, wl], axis=1), preferred_element_type=F32)
    logits = (hi_part[:, :LANES] + hi_part[:, LANES:]
              + jnp.dot(hl, wh, preferred_element_type=F32)) + rb_ref[...]
    g_idx, local = _route_gates(logits.T)
    tm = g_idx.shape[1]
    mem = (_iota2((8, tm), 0) == g_idx).astype(BF16)
    before = (_iota2((MOE_SUB, MOE_SUB), 0) < _iota2((MOE_SUB, MOE_SUB), 1)).astype(BF16)
    earlier = jnp.concatenate(
        [jnp.dot(mem[:, s:s + MOE_SUB], before, preferred_element_type=F32) for s in range(0, tm, MOE_SUB)],
        axis=1)
    rank = jnp.sum(mem.astype(F32) * earlier, axis=0, keepdims=True)
    key = g_idx.astype(F32) * MOE_KEY_STRIDE + rank
    krow_ref[...] = key
    sub_shift = int(math.log2(MOE_SUB))
    sub_sel = ((_iota2((tm, LANES), 0) >> sub_shift) == _iota2((tm, LANES), 1)).astype(BF16)
    cnt_ref[...] = jnp.dot(mem, sub_sel, preferred_element_type=F32)
    rows = jnp.concatenate([key] + local + [jnp.zeros((LANES - 1 - EXPERTS_PER_GROUP, tm), F32)], axis=0)
    gk_ref[...] = rows.T


def _outproj_route(a, b, a_blk, b_blk, wa, wb, x, nw, rw, rb, *, tm, name):
    n, d = x.shape
    ka = wa.shape[0]
    kb = wb.shape[0]
    return pl.pallas_call(
        _outproj_kernel,
        grid=(n // tm,),
        in_specs=[
            pl.BlockSpec((tm, ka), lambda i: (i, a_blk)),
            pl.BlockSpec((tm, kb), lambda i: (i, b_blk)),
            pl.BlockSpec((ka, d), lambda i: (0, 0)),
            pl.BlockSpec((kb, d), lambda i: (0, 0)),
            pl.BlockSpec((tm, d), lambda i: (i, 0)),
            pl.BlockSpec((1, d), lambda i: (0, 0)),
            pl.BlockSpec((d, LANES), lambda i: (0, 0)),
            pl.BlockSpec((1, LANES), lambda i: (0, 0)),
        ],
        out_specs=[
            pl.BlockSpec((tm, d), lambda i: (i, 0)),
            pl.BlockSpec((tm, d), lambda i: (i, 0)),
            pl.BlockSpec((tm, LANES), lambda i: (i, 0)),
            pl.BlockSpec((1, tm), lambda i: (0, i)),
            pl.BlockSpec((8, LANES), lambda i: (i, 0)),
        ],
        out_shape=[jax.ShapeDtypeStruct((n, d), F32),
                   jax.ShapeDtypeStruct((n, d), BF16),
                   jax.ShapeDtypeStruct((n, LANES), F32),
                   jax.ShapeDtypeStruct((1, n), F32),
                   jax.ShapeDtypeStruct((8 * (n // tm), LANES), F32)],
        compiler_params=pltpu.CompilerParams(
            dimension_semantics=("parallel",), vmem_limit_bytes=VMEM_LIMIT),
        name=name,
    )(a, b, wa, wb, x, nw, rw, rb)


def _round_up(v, m):
    return ((v + (m - 1)) // m) * m


def _moe_kernel(cnt_ref, h_ref, krow_ref, gk_ref, wgu_ref, wd_ref, x_ref, fnw_ref, o_ref,
                hc_s, gc_s, y_s, *, final_norm):
    i = pl.program_id(0)
    g = pl.program_id(1)
    tm = h_ref.shape[0]
    n_sub = tm // MOE_SUB

    def segments(grp):
        cnts = [cnt_ref[(i * n_sub + s) * N_GROUPS + grp] for s in range(n_sub)]
        starts = [jnp.int32(0)]
        for s in range(n_sub):
            starts.append(starts[-1] + _round_up(cnts[s], MOE_ALIGN))
        return cnts, starts

    def first_block(tot):
        fst = jnp.clip(_round_up(tot, MOE_FIRST_STEP), MOE_FIRST[0], MOE_FIRST[-1])
        return fst, (jnp.maximum(tot - fst, 0) + MOE_BLK - 1) // MOE_BLK

    def y_bases():
        bases = [jnp.int32(0)]
        for grp in range(N_GROUPS):
            fst, nrest = first_block(segments(grp)[1][-1])
            bases.append(bases[-1] + fst + (nrest + 1) * MOE_BLK)
        return bases

    counts, offs = segments(g)
    total = offs[-1]
    key0 = g.astype(F32) * MOE_KEY_STRIDE
    all_bases = y_bases()
    y_base = functools.reduce(lambda a, b: a + b,
                              [jnp.where(g == grp, all_bases[grp], 0) for grp in range(N_GROUPS)])
    w0 = g * EXPERTS_PER_GROUP
    sub_iota = _iota2((MOE_BLK, MOE_SUB), 0).astype(F32)
    lane_iota = _iota2((MOE_SUB, MOE_BLK), 1).astype(F32)

    def pack(s, w):
        rows = slice(s * MOE_SUB, (s + 1) * MOE_SUB)
        base = key0 + (w * MOE_BLK).astype(F32)
        sel = (krow_ref[:, rows] == sub_iota + base).astype(BF16)
        dst = pl.ds(pl.multiple_of(offs[s] + w * MOE_BLK, MOE_ALIGN), MOE_BLK)
        hc_s[dst, :] = jnp.dot(sel, h_ref[rows, :], preferred_element_type=F32).astype(BF16)
        gc_s[dst, :] = _mm_01(sel, gk_ref[rows, :])

    for s in range(n_sub):
        def pack_more(w, carry, s=s):
            pack(s, w)
            return carry

        lax.fori_loop(1, (counts[s] + MOE_BLK - 1) // MOE_BLK, pack_more, 0)
    for s in range(n_sub):
        pack(s, jnp.int32(0))

    tail_rows = MOE_FIRST[-1] + MOE_FIRST_STEP
    tail = pl.ds(pl.multiple_of(total, MOE_ALIGN), tail_rows)
    hc_s[tail, :] = jnp.zeros((tail_rows, hc_s.shape[1]), BF16)
    gc_s[tail, :] = jnp.zeros((tail_rows, LANES), F32)

    def expert_rows(start, rows):
        blk = pl.ds(pl.multiple_of(start, MOE_FIRST_STEP), rows)
        hb = hc_s[blk, :]
        gates = gc_s[blk, :]
        y = jnp.zeros((rows, o_ref.shape[1]), F32)
        for e in range(EXPERTS_PER_GROUP):
            gu = jnp.dot(hb, wgu_ref[0, w0 + e], preferred_element_type=F32)
            act = _silu(gu[:, :D_EXPERT]) * gu[:, D_EXPERT:] * gates[:, 1 + e:2 + e]
            y = y + jnp.dot(act.astype(BF16), wd_ref[0, w0 + e], preferred_element_type=F32)
        y_s[pl.ds(pl.multiple_of(y_base + start, MOE_FIRST_STEP), rows), :] = y.astype(BF16)

    first, n_rest = first_block(total)
    for size in MOE_FIRST:
        @pl.when(first == size)
        def _(size=size):
            expert_rows(0, size)

    def rest(bi, carry):
        expert_rows(first + bi * MOE_BLK, MOE_BLK)
        return carry

    lax.fori_loop(0, n_rest, rest, 0)
    done = pl.ds(pl.multiple_of(y_base + first + n_rest * MOE_BLK, MOE_FIRST_STEP), MOE_BLK)
    y_s[done, :] = jnp.zeros((MOE_BLK, y_s.shape[1]), BF16)

    @pl.when(g == N_GROUPS - 1)
    def _():
        segs = [segments(grp) for grp in range(N_GROUPS)]
        for s in range(n_sub):
            rows = slice(s * MOE_SUB, (s + 1) * MOE_SUB)
            key_col = gk_ref[rows, 0:1]
            sel = jnp.concatenate(
                [(key_col == lane_iota + grp * MOE_KEY_STRIDE).astype(BF16) for grp in range(N_GROUPS)], axis=1)
            packed = jnp.concatenate(
                [y_s[pl.ds(pl.multiple_of(all_bases[grp] + segs[grp][1][s], MOE_ALIGN), MOE_BLK), :]
                 for grp in range(N_GROUPS)], axis=0)
            o_ref[rows, :] = x_ref[rows, :] + jnp.dot(sel, packed, preferred_element_type=F32)
            for grp in range(N_GROUPS):
                def more(w, carry, s=s, rows=rows, grp=grp, key_col=key_col):
                    base = grp * MOE_KEY_STRIDE + (w * MOE_BLK).astype(F32)
                    sel_w = (key_col == lane_iota + base).astype(BF16)
                    src = pl.ds(pl.multiple_of(all_bases[grp] + segs[grp][1][s] + w * MOE_BLK, MOE_ALIGN),
                                MOE_BLK)
                    o_ref[rows, :] += jnp.dot(sel_w, y_s[src, :], preferred_element_type=F32)
                    return carry

                lax.fori_loop(1, (segs[grp][0][s] + MOE_BLK - 1) // MOE_BLK, more, 0)
        if final_norm:
            y = o_ref[...]
            ms = jnp.mean(y * y, axis=-1, keepdims=True)
            o_ref[...] = y * lax.rsqrt(ms + RMS_EPS) * fnw_ref[...]


def _moe(h, krow, gk, counts, wgu, wd, x, fnw, *, layer, tm, final_norm, name):
    n, d = x.shape
    max_rows = tm + (tm // MOE_SUB) * MOE_ALIGN
    buf_rows = _round_up(max_rows + MOE_FIRST[-1] + MOE_FIRST_STEP, MOE_BLK)
    y_rows = _round_up(max_rows + 2 * MOE_BLK + (N_GROUPS - 1) * (MOE_FIRST[-1] + MOE_BLK), MOE_BLK)
    grid_spec = pltpu.PrefetchScalarGridSpec(
        num_scalar_prefetch=1,
        grid=(n // tm, N_GROUPS),
        in_specs=[
            pl.BlockSpec((tm, d), lambda i, g, c: (i, 0)),
            pl.BlockSpec((1, tm), lambda i, g, c: (0, i)),
            pl.BlockSpec((tm, LANES), lambda i, g, c: (i, 0)),
            pl.BlockSpec((1, N_EXPERTS, d, 2 * D_EXPERT), lambda i, g, c: (layer, 0, 0, 0),
                         pipeline_mode=pl.Buffered(1)),
            pl.BlockSpec((1, N_EXPERTS, D_EXPERT, d), lambda i, g, c: (layer, 0, 0, 0),
                         pipeline_mode=pl.Buffered(1)),
            pl.BlockSpec((tm, d), lambda i, g, c: (i, 0)),
            pl.BlockSpec((1, d), lambda i, g, c: (0, 0)),
        ],
        out_specs=pl.BlockSpec((tm, d), lambda i, g, c: (i, 0)),
        scratch_shapes=[pltpu.VMEM((buf_rows, d), BF16),
                        pltpu.VMEM((buf_rows, LANES), F32),
                        pltpu.VMEM((y_rows, d), BF16)],
    )
    return pl.pallas_call(
        functools.partial(_moe_kernel, final_norm=final_norm),
        grid_spec=grid_spec,
        out_shape=jax.ShapeDtypeStruct((n, d), F32),
        compiler_params=pltpu.CompilerParams(
            dimension_semantics=("parallel", "arbitrary"), vmem_limit_bytes=VMEM_LIMIT),
        name=name,
    )(counts, h, krow, gk, wgu, wd, x, fnw)


def _pad_cols(a, width):
    return jnp.pad(a, ((0, 0), (0, width - a.shape[1])))


def _dispatch_counts(cnt, route_tm):
    c = cnt.reshape(-1, 8, LANES)[:, :N_GROUPS, :route_tm // MOE_SUB]
    return jnp.transpose(c, (0, 2, 1)).reshape(-1).astype(jnp.int32)


def _router_params(wc, bc, wf, bf):
    rw = _pad_cols(jnp.concatenate([wc, wf], axis=1), LANES)
    rb = _pad_cols(jnp.concatenate([bc, bf])[None, :], LANES)
    return rw, rb


def kernel(x, norm_mix_w, norm_ffn_w, even_w_in, gdn_conv_w, gdn_a_log, gdn_dt_bias, gdn_norm_w, hgrn_lb_logits, hgrn_norm_w, even_w_out, odd_w_in, ret_norm_w, odd_w_out, router_c_w, router_c_b, router_f_w, router_f_b, moe_w_gate_up, moe_w_down, final_norm_w):
    bsz, seq, d = x.shape
    n = bsz * seq
    xt = x.reshape(n, d)
    mix_w = N_HEADS * HEAD_D
    conv_cols = 3 * mix_w
    gdn_main = conv_cols + mix_w

    w_in = even_w_in[0]
    small0 = gdn_main
    w_main = jnp.concatenate([w_in[:, :small0], w_in[:, small0 + 2 * N_HEADS:]], axis=1).astype(BF16)
    w_small = jnp.concatenate(
        [_pad_cols(w_in[:, small0:small0 + N_HEADS], LANES),
         _pad_cols(w_in[:, small0 + N_HEADS:small0 + 2 * N_HEADS], LANES)], axis=1).astype(BF16)
    proj, small = _norm_matmul(xt, norm_mix_w[0][None, :], w_main, w_small, tm=512, tn=1024, name="in_proj_even")
    proj = proj.reshape(bsz, seq, -1)
    small = small.reshape(bsz, seq, -1)
    o_a = _gdn(proj, small, gdn_conv_w[0], _pad_cols(gdn_a_log[0][None, :], LANES),
               _pad_cols(gdn_dt_bias[0][None, :], LANES), gdn_norm_w[0][None, :], tb=512)
    o_b = _hgrn(proj, hgrn_lb_logits, hgrn_norm_w[0][None, :], tb=512, layer=0, col0=gdn_main // mix_w)
    w_out = even_w_out[0].astype(BF16)
    rw, rb = _router_params(router_c_w[0], router_c_b[0], router_f_w[0], router_f_b[0])
    wgu_b = moe_w_gate_up.astype(BF16)
    wd_b = moe_w_down.astype(BF16)
    x1, h, gk, krow, cnt = _outproj_route(o_a.reshape(n, mix_w), o_b.reshape(n, mix_w), 0, 0,
                                          w_out[:mix_w], w_out[mix_w:], xt, norm_ffn_w[0][None, :], rw, rb,
                                          tm=ROUTE_TM, name="out_proj_even")
    x2 = _moe(h, krow, gk, _dispatch_counts(cnt, ROUTE_TM), wgu_b, wd_b, x1,
              final_norm_w[None, :], layer=0, tm=MOE_TM, final_norm=False, name="moe0")

    perm = np.concatenate([np.arange(0, RET_DK, 2), np.arange(1, RET_DK, 2)])
    qk_perm = np.concatenate([hd * RET_DK + perm for hd in range(2 * N_HEADS)])
    w_odd = odd_w_in[0]
    w_odd = jnp.concatenate([w_odd[:, qk_perm], w_odd[:, 2 * N_HEADS * RET_DK:]], axis=1).astype(BF16)
    (proj1,) = _norm_matmul(x2, norm_mix_w[1][None, :], w_odd, None, tm=512, tn=1024, name="in_proj_odd")
    inv = 1.0 / (ROPE_BASE ** jnp.linspace(0.0, 1.0, RET_DK // 2, dtype=F32))
    ang = jnp.arange(seq, dtype=F32)[:, None] * inv[None, :]
    o_c = _retention(proj1.reshape(bsz, seq, -1), jnp.cos(ang), jnp.sin(ang), ret_norm_w[0][None, :],
                     tc=RET_CHUNK)
    o_c = o_c.reshape(n, -1)
    w_out1 = odd_w_out[0].astype(BF16)
    half = w_out1.shape[0] // 2
    rw, rb = _router_params(router_c_w[1], router_c_b[1], router_f_w[1], router_f_b[1])
    x3, h, gk, krow, cnt = _outproj_route(o_c, o_c, 0, 1, w_out1[:half], w_out1[half:], x2,
                                          norm_ffn_w[1][None, :], rw, rb, tm=ROUTE_TM, name="out_proj_odd")
    out = _moe(h, krow, gk, _dispatch_counts(cnt, ROUTE_TM), wgu_b, wd_b, x3,
               final_norm_w[None, :], layer=1, tm=MOE_TM, final_norm=True, name="moe1")
    return out.reshape(bsz, seq, d)
```

```python
import functools
import math

import jax
import jax.numpy as jnp
from jax import lax
from jax.experimental import pallas as pl
from jax.experimental.pallas import tpu as pltpu

F32 = jnp.float32
BF16 = jnp.bfloat16

D_MODEL = 1024
RMS_EPS = 1e-6
L2_EPS = 1e-6
CHUNK = 64
CONV_K = 4
N_HEADS = 4
HEAD_D = 128
RET_DK = 256
RET_DV = 512
RET_CHUNK = 256
ROPE_BASE = 10000.0
N_GROUPS = 4
EXPERTS_PER_GROUP = 4
N_EXPERTS = 16
D_EXPERT = 256
LANES = 128
ROUTE_TM = 1024
MOE_TM = 1024
MOE_SUB = 256
MOE_BLK = 128
MOE_ALIGN = 16
MOE_KEY_STRIDE = 4096.0
MOE_FIRST_STEP = 32
MOE_FIRST = (256, 288, 320, 352)
VMEM_LIMIT = 56 * 1024 * 1024


def _mm(a, b):
    return jnp.dot(a.astype(BF16), b.astype(BF16), preferred_element_type=F32)


def _mm_nt(a, b):
    return lax.dot_general(a.astype(BF16), b.astype(BF16), (((1,), (1,)), ((), ())),
                           preferred_element_type=F32)


def _mm_tn(a, b):
    return _mm(a.T, b)


def _mm_01(m01, x):
    hi = x.astype(BF16)
    lo = (x - hi.astype(F32)).astype(BF16)
    return (jnp.dot(m01, hi, preferred_element_type=F32)
            + jnp.dot(m01, lo, preferred_element_type=F32))


def _bmm(a, b):
    return jnp.einsum('cik,ckj->cij', a.astype(BF16), b.astype(BF16), preferred_element_type=F32)


def _bmm_nt(a, b):
    return jnp.einsum('cik,cjk->cij', a.astype(BF16), b.astype(BF16), preferred_element_type=F32)


def _bmm_tn(a, b):
    return _bmm(jnp.swapaxes(a, 1, 2), b)


def _bmm_01(m01, x):
    hi = x.astype(BF16)
    lo = (x - hi.astype(F32)).astype(BF16)
    return (jnp.einsum('cik,ckj->cij', m01, hi, preferred_element_type=F32)
            + jnp.einsum('cik,ckj->cij', m01, lo, preferred_element_type=F32))


def _sigmoid(x):
    return 1.0 / (1.0 + jnp.exp(-x))


def _silu(x):
    return x * _sigmoid(x)


def _softplus(x):
    return jnp.maximum(x, 0.0) + jnp.log(1.0 + jnp.exp(-jnp.abs(x)))


def _iota2(shape, dim):
    return lax.broadcasted_iota(jnp.int32, shape, dim)


def _head_rms_gate(o, gate, nw, width):
    outs = []
    for h in range(o.shape[1] // width):
        oh = o[:, h * width:(h + 1) * width]
        ms = jnp.mean(oh * oh, axis=-1, keepdims=True)
        outs.append(oh * lax.rsqrt(ms + RMS_EPS) * nw)
    return jnp.concatenate(outs, axis=1) * _silu(gate)


def _norm_mm_kernel(*refs, tn, n_w, has_small):
    x_ref, nw_ref = refs[:2]
    w_refs = refs[2:2 + n_w]
    if has_small:
        ws_ref, o_ref, os_ref = refs[2 + n_w:]
    else:
        (o_ref,) = refs[2 + n_w:]
    x = x_ref[...]
    ms = jnp.mean(x * x, axis=-1, keepdims=True)
    hb = (x * lax.rsqrt(ms + RMS_EPS) * nw_ref[...]).astype(BF16)
    if has_small:
        os_ref[...] = jnp.dot(hb, ws_ref[...], preferred_element_type=F32)
    c0 = 0
    for w_ref in w_refs:
        for j in range(w_ref.shape[1] // tn):
            res = jnp.dot(hb, w_ref[:, j * tn:(j + 1) * tn], preferred_element_type=F32)
            o_ref[:, c0 + j * tn:c0 + (j + 1) * tn] = res.astype(o_ref.dtype)
        c0 += w_ref.shape[1]


def _norm_matmul(x, nw, ws_main, ws, *, tm, tn, name):
    n, d = x.shape
    nout = sum(w.shape[1] for w in ws_main)
    has_small = ws is not None
    in_specs = [
        pl.BlockSpec((tm, d), lambda i: (i, 0)),
        pl.BlockSpec((1, d), lambda i: (0, 0)),
    ] + [pl.BlockSpec(w.shape, lambda i: (0, 0)) for w in ws_main]
    out_specs = [pl.BlockSpec((tm, nout), lambda i: (i, 0))]
    out_shape = [jax.ShapeDtypeStruct((n, nout), BF16)]
    args = [x, nw] + list(ws_main)
    if has_small:
        nsmall = ws.shape[1]
        in_specs.append(pl.BlockSpec((d, nsmall), lambda i: (0, 0)))
        out_specs.append(pl.BlockSpec((tm, nsmall), lambda i: (i, 0)))
        out_shape.append(jax.ShapeDtypeStruct((n, nsmall), F32))
        args.append(ws)
    return pl.pallas_call(
        functools.partial(_norm_mm_kernel, tn=tn, n_w=len(ws_main), has_small=has_small),
        grid=(n // tm,),
        in_specs=in_specs,
        out_specs=out_specs,
        out_shape=out_shape,
        compiler_params=pltpu.CompilerParams(
            dimension_semantics=("parallel",), vmem_limit_bytes=VMEM_LIMIT),
        name=name,
    )(*args)


def _gdn_kernel(qkv_ref, z_ref, sm_ref, cw_ref, alog_ref, dtb_ref, nw_ref, o_ref,
                xs_ref, qkv_s, o_s, m_s, sq_s, qe_s, dec_s, st_ref, *, tb):
    @pl.when(pl.program_id(1) == 0)
    def _():
        xs_ref[0:8, :] = jnp.zeros((8, xs_ref.shape[1]), F32)
        st_ref[...] = jnp.zeros(st_ref.shape, F32)

    c = CHUNK
    nc = tb // c
    nb = N_HEADS * nc
    qk_w = N_HEADS * HEAD_D
    row = _iota2((c, c), 0)
    col = _iota2((c, c), 1)
    causal = (row >= col)[None]
    strict = (row > col)[None]
    same16 = ((row >> 4) == (col >> 4))[None]
    same32 = ((row >> 5) == (col >> 5))[None]
    eye = (row == col).astype(F32)[None]
    ltri = jnp.broadcast_to((row >= col).astype(BF16)[None], (nc, c, c))
    lane = _iota2((nb, c, LANES), 2)

    def per_head(fn):
        return jnp.concatenate([fn(h) for h in range(N_HEADS)], axis=0)

    for h in range(N_HEADS):
        for base, scale in ((0, HEAD_D ** -0.5), (qk_w, 1.0), (2 * qk_w, None)):
            cols = slice(base + h * HEAD_D, base + (h + 1) * HEAD_D)
            x = qkv_ref[0, :, cols].astype(F32)
            xs_ref[8:8 + tb, cols] = x
            cw = cw_ref[:, cols]
            y = x * cw[CONV_K - 1:CONV_K, :]
            for j in range(1, CONV_K):
                y = y + xs_ref[pl.ds(8 - j, tb), cols] * cw[CONV_K - 1 - j:CONV_K - j, :]
            y = _silu(y)
            if scale is not None:
                y = y * (lax.rsqrt(jnp.sum(y * y, axis=-1, keepdims=True) + L2_EPS) * scale)
            qkv_s[:, cols] = y
    xs_ref[0:8, :] = xs_ref[tb:tb + 8, :]

    sm = sm_ref[0]
    beta_all = _sigmoid(sm[:, :LANES]).reshape(nc, c, LANES)
    g_all = (-jnp.exp(alog_ref[...]) * _softplus(sm[:, LANES:] + dtb_ref[...])).reshape(nc, c, LANES)
    gc_all = _bmm_01(ltri, g_all)
    q = per_head(lambda h: qkv_s[:, h * HEAD_D:(h + 1) * HEAD_D].reshape(nc, c, HEAD_D))
    k = per_head(lambda h: qkv_s[:, qk_w + h * HEAD_D:qk_w + (h + 1) * HEAD_D].reshape(nc, c, HEAD_D))
    v = per_head(lambda h: qkv_s[:, 2 * qk_w + h * HEAD_D:2 * qk_w + (h + 1) * HEAD_D].reshape(nc, c, HEAD_D))
    beta = per_head(lambda h: beta_all[:, :, h:h + 1])
    gcol = per_head(lambda h: gc_all[:, :, h:h + 1])
    g_hi = gcol.astype(BF16).astype(F32)
    g_mid = (gcol - g_hi).astype(BF16).astype(F32)
    g_lo = gcol - g_hi - g_mid
    ones_hi = jnp.where(lane < 6, 1.0, 0.0)
    lhs = jnp.where(lane == 0, g_hi, jnp.where(lane == 1, g_mid, jnp.where(lane == 2, g_lo, ones_hi)))
    rhs_g = jnp.where(lane == 3, -g_hi, jnp.where(lane == 4, -g_mid, jnp.where(lane == 5, -g_lo, ones_hi)))
    decay = jnp.exp(jnp.where(causal, _bmm_nt(lhs, rhs_g), -jnp.inf))
    kb = k * beta
    a = jnp.where(strict, _bmm_nt(kb, k) * decay, 0.0)
    d = jnp.where(same16, a, 0.0)
    x_inv = eye - d
    dp = d
    for _ in range(3):
        dp = _bmm(dp, dp)
        x_inv = x_inv + _bmm(x_inv, dp)
    e = jnp.where(jnp.logical_and(same32, jnp.logical_not(same16)), a, 0.0)
    x_inv = x_inv - _bmm(x_inv, _bmm(e, x_inv))
    f = jnp.where(same32, 0.0, a)
    x_inv = x_inv - _bmm(x_inv, _bmm(f, x_inv))
    wu = _bmm(x_inv, jnp.concatenate([kb * jnp.exp(gcol), v * beta], axis=2))
    attn = _bmm_nt(q, k) * decay
    g_last = gcol[:, c - 1:c, :]
    kd = k * jnp.exp(g_last - gcol)
    mq = _bmm_tn(kd, wu)
    aw = _bmm(attn, wu)
    m_s[...] = mq[:, :, :HEAD_D].astype(BF16).reshape(N_HEADS, nc, HEAD_D, HEAD_D)
    sq_s[...] = mq[:, :, HEAD_D:].reshape(N_HEADS, nc, HEAD_D, HEAD_D)
    qe_s[...] = (q * jnp.exp(gcol) - aw[:, :, :HEAD_D]).astype(BF16).reshape(N_HEADS, nc, c, HEAD_D)
    dec_s[...] = jnp.broadcast_to(jnp.exp(g_last), (nb, 1, HEAD_D)).reshape(N_HEADS, nc, 1, HEAD_D)
    for h in range(N_HEADS):
        o_s[:, h * HEAD_D:(h + 1) * HEAD_D] = aw[h * nc:(h + 1) * nc, :, HEAD_D:].reshape(tb, HEAD_D)

    for ci in range(nc):
        for h in range(N_HEADS):
            hs = slice(h * HEAD_D, (h + 1) * HEAD_D)
            s_h = st_ref[h]
            s_b = s_h.astype(BF16)
            o_s[ci * c:(ci + 1) * c, hs] += jnp.dot(qe_s[h, ci], s_b, preferred_element_type=F32)
            st_ref[h] = (s_h * dec_s[h, ci] - jnp.dot(m_s[h, ci], s_b, preferred_element_type=F32)
                         + sq_s[h, ci])

    o_ref[0] = _head_rms_gate(o_s[...], z_ref[0].astype(F32), nw_ref[...], HEAD_D).astype(o_ref.dtype)


def _gdn(proj, small, conv_w, a_log, dt_bias, norm_w, *, tb):
    b, t, _ = proj.shape
    conv_cols = 3 * N_HEADS * HEAD_D
    mix_w = N_HEADS * HEAD_D
    return pl.pallas_call(
        functools.partial(_gdn_kernel, tb=tb),
        grid=(b, t // tb),
        in_specs=[
            pl.BlockSpec((1, tb, conv_cols), lambda i, j: (i, j, 0)),
            pl.BlockSpec((1, tb, mix_w), lambda i, j: (i, j, conv_cols // mix_w)),
            pl.BlockSpec((1, tb, 2 * LANES), lambda i, j: (i, j, 0)),
            pl.BlockSpec((CONV_K, conv_cols), lambda i, j: (0, 0)),
            pl.BlockSpec((1, LANES), lambda i, j: (0, 0)),
            pl.BlockSpec((1, LANES), lambda i, j: (0, 0)),
            pl.BlockSpec((1, HEAD_D), lambda i, j: (0, 0)),
        ],
        out_specs=pl.BlockSpec((1, tb, mix_w), lambda i, j: (i, j, 0)),
        out_shape=jax.ShapeDtypeStruct((b, t, mix_w), BF16),
        scratch_shapes=[
            pltpu.VMEM((tb + 8, conv_cols), F32),
            pltpu.VMEM((tb, conv_cols), F32),
            pltpu.VMEM((tb, mix_w), F32),
            pltpu.VMEM((N_HEADS, tb // CHUNK, HEAD_D, HEAD_D), BF16),
            pltpu.VMEM((N_HEADS, tb // CHUNK, HEAD_D, HEAD_D), F32),
            pltpu.VMEM((N_HEADS, tb // CHUNK, CHUNK, HEAD_D), BF16),
            pltpu.VMEM((N_HEADS, tb // CHUNK, 1, HEAD_D), F32),
            pltpu.VMEM((N_HEADS, HEAD_D, HEAD_D), F32),
        ],
        compiler_params=pltpu.CompilerParams(
            dimension_semantics=("parallel", "arbitrary"), vmem_limit_bytes=VMEM_LIMIT),
        name="gdn",
    )(proj, proj, small, conv_w, a_log, dt_bias, norm_w)


def _hgrn_kernel(q_ref, f_ref, i_ref, g_ref, lb_ref, nw_ref, o_ref,
                 q_s, k_s, lf_s, i_s, o_s, st_ref, *, tb, layer):
    @pl.when(pl.program_id(1) == 0)
    def _():
        st_ref[...] = jnp.zeros(st_ref.shape, F32)

    lbl = lb_ref[...]
    e_lb = jnp.exp(lbl - jnp.max(lbl, axis=0, keepdims=True))
    lb = jnp.sum(e_lb[:layer + 1], axis=0, keepdims=True) / jnp.sum(e_lb, axis=0, keepdims=True)

    f = lb + (1.0 - lb) * _sigmoid(f_ref[0].astype(F32))
    k_s[...] = 1.0 - f
    lf_s[...] = jnp.log(f)
    q_s[...] = _silu(q_ref[0].astype(F32)) * (HEAD_D ** -0.5)
    i_s[...] = i_ref[0].astype(F32)

    c = CHUNK
    nc = tb // c
    blk = 8
    row = _iota2((c, c), 0)
    col = _iota2((c, c), 1)
    ltri = jnp.broadcast_to((row >= col).astype(BF16)[None], (nc, c, c))
    level_masks = {}
    for m in (32, 16, 8):
        sh = int(math.log2(2 * m))
        level_masks[m] = jnp.logical_and(
            (row >> sh) == (col >> sh),
            jnp.logical_and((row & (2 * m - 1)) >= m, (col & (2 * m - 1)) < m))[None]
    nb = N_HEADS * nc
    sub = _iota2((nb * c // blk, blk, HEAD_D), 1)

    def per_head(fn):
        return jnp.concatenate([fn(slice(h * HEAD_D, (h + 1) * HEAD_D)) for h in range(N_HEADS)], axis=0)

    b_all = _bmm_01(ltri, lf_s[...].reshape(nc, c, N_HEADS * HEAD_D))
    b = per_head(lambda sl: b_all[:, :, sl])
    q = per_head(lambda sl: q_s[:, sl].reshape(nc, c, HEAD_D))
    k = per_head(lambda sl: k_s[:, sl].reshape(nc, c, HEAD_D))
    iv = per_head(lambda sl: i_s[:, sl].reshape(nc, c, HEAD_D))
    attn = jnp.zeros((nb, c, c), F32)
    for m in (32, 16, 8):
        b_m = b.reshape(nb * c // (2 * m), 2 * m, HEAD_D)
        ref = jnp.broadcast_to(b_m[:, m:m + 1, :], b_m.shape).reshape(nb, c, HEAD_D)
        e = jnp.exp(-jnp.abs(b - ref))
        attn = attn + jnp.where(level_masks[m], _bmm_nt(q * e, k * e), 0.0)
    o = _bmm(attn, iv)
    qb, kb, bb, ib = (a.reshape(nb * c // blk, blk, HEAD_D) for a in (q, k, b, iv))
    ob = jnp.zeros(qb.shape, F32)
    for s in range(blk):
        dec = jnp.exp(jnp.where(sub >= s, bb - bb[:, s:s + 1, :], -jnp.inf))
        a_col = jnp.sum(dec * qb * kb[:, s:s + 1, :], axis=-1, keepdims=True)
        ob = ob + a_col * ib[:, s:s + 1, :]
    o = o + ob.reshape(nb, c, HEAD_D)
    b_last = b[:, c - 1:c, :]
    q_in = (q * jnp.exp(b)).astype(BF16)
    upd = _bmm_tn(iv, k * jnp.exp(b_last - b))
    keep = jnp.exp(b_last)
    for h in range(N_HEADS):
        s_t = st_ref[h]
        outs = []
        for ci in range(h * nc, (h + 1) * nc):
            outs.append(o[ci] + _mm_nt(q_in[ci], s_t))
            s_t = s_t * keep[ci] + upd[ci]
        st_ref[h] = s_t
        o_s[:, h * HEAD_D:(h + 1) * HEAD_D] = jnp.concatenate(outs, axis=0)

    o_ref[0] = _head_rms_gate(o_s[...], g_ref[0].astype(F32), nw_ref[...], HEAD_D).astype(o_ref.dtype)


def _hgrn(proj, lb_logits, norm_w, *, tb, layer, col0):
    b, t, _ = proj.shape
    w = N_HEADS * HEAD_D
    specs = [pl.BlockSpec((1, tb, w), functools.partial(lambda i, j, off: (i, j, off), off=col0 + n))
             for n in range(4)]
    return pl.pallas_call(
        functools.partial(_hgrn_kernel, tb=tb, layer=layer),
        grid=(b, t // tb),
        in_specs=specs + [
            pl.BlockSpec(lb_logits.shape, lambda i, j: (0, 0)),
            pl.BlockSpec((1, HEAD_D), lambda i, j: (0, 0)),
        ],
        out_specs=pl.BlockSpec((1, tb, w), lambda i, j: (i, j, 0)),
        out_shape=jax.ShapeDtypeStruct((b, t, w), BF16),
        scratch_shapes=[pltpu.VMEM((tb, w), F32)] * 5 + [pltpu.VMEM((N_HEADS, HEAD_D, HEAD_D), F32)],
        compiler_params=pltpu.CompilerParams(
            dimension_semantics=("parallel", "arbitrary"), vmem_limit_bytes=VMEM_LIMIT),
        name="hgrn2",
    )(proj, proj, proj, proj, lb_logits, norm_w)


def _ret_kernel(q_ref, k_ref, v_ref, g_ref, cos_ref, sin_ref, nw_ref, o_ref, dmat_s, st_ref, *, tc):
    log_gammas = [math.log(1.0 - 2.0 ** (-5.0 - h)) for h in range(N_HEADS)]

    @pl.when(pl.program_id(1) == 0)
    def _():
        st_ref[...] = jnp.zeros(st_ref.shape, F32)
        rel = (_iota2((tc, tc), 0) - _iota2((tc, tc), 1)).astype(F32)
        for h in range(N_HEADS):
            dmat_s[h] = jnp.where(rel >= 0, jnp.exp(jnp.maximum(rel, 0.0) * log_gammas[h]), 0.0)

    pos = _iota2((tc, 1), 0).astype(F32)
    cos = cos_ref[...]
    sin = sin_ref[...]
    half = RET_DK // 2

    def rot(x):
        x1, x2 = x[:, :half], x[:, half:]
        return jnp.concatenate([x1 * cos - x2 * sin, x1 * sin + x2 * cos], axis=1)

    for h in range(N_HEADS):
        lg = log_gammas[h]
        qk_cols = slice(h * RET_DK, (h + 1) * RET_DK)
        v_cols = slice(h * RET_DV, (h + 1) * RET_DV)
        q = rot(q_ref[0, :, qk_cols].astype(F32))
        k = rot(k_ref[0, :, qk_cols].astype(F32)) * (RET_DK ** -0.5)
        v = v_ref[0, :, v_cols]
        s = st_ref[h]
        attn = _mm_nt(q, k) * dmat_s[h]
        o = _mm(q * jnp.exp((pos + 1.0) * lg), s) + _mm(attn, v)
        st_ref[h] = s * math.exp(tc * lg) + _mm_tn(k * jnp.exp((tc - 1.0 - pos) * lg), v)
        ms = jnp.mean(o * o, axis=-1, keepdims=True)
        o_ref[0, :, v_cols] = (o * lax.rsqrt(ms + RMS_EPS) * nw_ref[...]
                               * _silu(g_ref[0, :, v_cols].astype(F32))).astype(o_ref.dtype)


def _retention(proj, cos, sin, norm_w, *, tc):
    b, t, _ = proj.shape
    qk_w = N_HEADS * RET_DK
    v_w = N_HEADS * RET_DV
    return pl.pallas_call(
        functools.partial(_ret_kernel, tc=tc),
        grid=(b, t // tc),
        in_specs=[
            pl.BlockSpec((1, tc, qk_w), lambda i, j: (i, j, 0)),
            pl.BlockSpec((1, tc, qk_w), lambda i, j: (i, j, 1)),
            pl.BlockSpec((1, tc, v_w), lambda i, j: (i, j, 1)),
            pl.BlockSpec((1, tc, v_w), lambda i, j: (i, j, 2)),
            pl.BlockSpec((tc, RET_DK // 2), lambda i, j: (j, 0)),
            pl.BlockSpec((tc, RET_DK // 2), lambda i, j: (j, 0)),
            pl.BlockSpec((1, RET_DV), lambda i, j: (0, 0)),
        ],
        out_specs=pl.BlockSpec((1, tc, v_w), lambda i, j: (i, j, 0)),
        out_shape=jax.ShapeDtypeStruct((b, t, v_w), BF16),
        scratch_shapes=[pltpu.VMEM((N_HEADS, tc, tc), F32), pltpu.VMEM((N_HEADS, RET_DK, RET_DV), F32)],
        compiler_params=pltpu.CompilerParams(
            dimension_semantics=("parallel", "arbitrary"), vmem_limit_bytes=VMEM_LIMIT),
        name="retention",
    )(proj, proj, proj, proj, cos, sin, norm_w)


def _route_gates(logits_t):
    cl = [logits_t[g:g + 1, :] for g in range(N_GROUPS)]
    cmax = functools.reduce(jnp.maximum, cl)
    denom = sum(jnp.exp(x - cmax) for x in cl)
    g_prob = 1.0 / denom
    g_idx = jnp.full(cmax.shape, N_GROUPS - 1, jnp.int32)
    for g in range(N_GROUPS - 2, -1, -1):
        g_idx = jnp.where(cl[g] == cmax, g, g_idx)
    def fine_row(g, j):
        r = N_GROUPS + g * EXPERTS_PER_GROUP + j
        return logits_t[r:r + 1, :]

    fl = []
    for j in range(EXPERTS_PER_GROUP):
        x = fine_row(N_GROUPS - 1, j)
        for g in range(N_GROUPS - 2, -1, -1):
            x = jnp.where(g_idx == g, fine_row(g, j), x)
        fl.append(x)
    m1 = functools.reduce(jnp.maximum, fl)
    i1 = jnp.full(m1.shape, EXPERTS_PER_GROUP - 1, jnp.int32)
    for j in range(EXPERTS_PER_GROUP - 2, -1, -1):
        i1 = jnp.where(fl[j] == m1, j, i1)
    rest = [jnp.where(i1 == j, -jnp.inf, fl[j]) for j in range(EXPERTS_PER_GROUP)]
    m2 = functools.reduce(jnp.maximum, rest)
    i2 = jnp.full(m2.shape, EXPERTS_PER_GROUP - 1, jnp.int32)
    for j in range(EXPERTS_PER_GROUP - 2, -1, -1):
        i2 = jnp.where(jnp.logical_and(rest[j] == m2, i1 != j), j, i2)
    e2 = jnp.exp(m2 - m1)
    w1 = g_prob / (1.0 + e2)
    w2 = g_prob * e2 / (1.0 + e2)
    local = [jnp.where(i1 == j, w1, 0.0) + jnp.where(i2 == j, w2, 0.0) for j in range(EXPERTS_PER_GROUP)]
    return g_idx, local


def _outproj_kernel(a_ref, b_ref, wa_ref, wb_ref, x_ref, nw_ref, rw_ref, rb_ref,
                    x1_ref, h_ref, gk_ref, krow_ref, cnt_ref):
    x1 = (x_ref[...]
          + jnp.dot(a_ref[...], wa_ref[...], preferred_element_type=F32)
          + jnp.dot(b_ref[...], wb_ref[...], preferred_element_type=F32))
    x1_ref[...] = x1
    ms = jnp.mean(x1 * x1, axis=-1, keepdims=True)
    h = x1 * lax.rsqrt(ms + RMS_EPS) * nw_ref[...]
    h_ref[...] = h.astype(BF16)
    rw = rw_ref[...]
    hh = h.astype(BF16)
    hl = (h - hh.astype(F32)).astype(BF16)
    wh = rw.astype(BF16)
    wl = (rw - wh.astype(F32)).astype(BF16)
    hi_part = jnp.dot(hh, jnp.concatenate([wh, wl], axis=1), preferred_element_type=F32)
    logits = (hi_part[:, :LANES] + hi_part[:, LANES:]
              + jnp.dot(hl, wh, preferred_element_type=F32)) + rb_ref[...]
    g_idx, local = _route_gates(logits.T)
    tm = g_idx.shape[1]
    mem = (_iota2((8, tm), 0) == g_idx).astype(BF16)
    before = (_iota2((MOE_SUB, MOE_SUB), 0) < _iota2((MOE_SUB, MOE_SUB), 1)).astype(BF16)
    earlier = jnp.concatenate(
        [jnp.dot(mem[:, s:s + MOE_SUB], before, preferred_element_type=F32) for s in range(0, tm, MOE_SUB)],
        axis=1)
    rank = jnp.sum(mem.astype(F32) * earlier, axis=0, keepdims=True)
    key = g_idx.astype(F32) * MOE_KEY_STRIDE + rank
    krow_ref[...] = key
    sub_shift = int(math.log2(MOE_SUB))
    sub_sel = ((_iota2((tm, LANES), 0) >> sub_shift) == _iota2((tm, LANES), 1)).astype(BF16)
    cnt_ref[...] = jnp.dot(mem, sub_sel, preferred_element_type=F32)
    rows = jnp.concatenate([key] + local + [jnp.zeros((LANES - 1 - EXPERTS_PER_GROUP, tm), F32)], axis=0)
    gk_ref[...] = rows.T


def _outproj_route(a, b, a_blk, b_blk, wa, wb, x, nw, rw, rb, *, tm, name):
    n, d = x.shape
    ka = wa.shape[0]
    kb = wb.shape[0]
    return pl.pallas_call(
        _outproj_kernel,
        grid=(n // tm,),
        in_specs=[
            pl.BlockSpec((tm, ka), lambda i: (i, a_blk)),
            pl.BlockSpec((tm, kb), lambda i: (i, b_blk)),
            pl.BlockSpec((ka, d), lambda i: (0, 0)),
            pl.BlockSpec((kb, d), lambda i: (0, 0)),
            pl.BlockSpec((tm, d), lambda i: (i, 0)),
            pl.BlockSpec((1, d), lambda i: (0, 0)),
            pl.BlockSpec((d, LANES), lambda i: (0, 0)),
            pl.BlockSpec((1, LANES), lambda i: (0, 0)),
        ],
        out_specs=[
            pl.BlockSpec((tm, d), lambda i: (i, 0)),
            pl.BlockSpec((tm, d), lambda i: (i, 0)),
            pl.BlockSpec((tm, LANES), lambda i: (i, 0)),
            pl.BlockSpec((1, tm), lambda i: (0, i)),
            pl.BlockSpec((8, LANES), lambda i: (i, 0)),
        ],
        out_shape=[jax.ShapeDtypeStruct((n, d), F32),
                   jax.ShapeDtypeStruct((n, d), BF16),
                   jax.ShapeDtypeStruct((n, LANES), F32),
                   jax.ShapeDtypeStruct((1, n), F32),
                   jax.ShapeDtypeStruct((8 * (n // tm), LANES), F32)],
        compiler_params=pltpu.CompilerParams(
            dimension_semantics=("parallel",), vmem_limit_bytes=VMEM_LIMIT),
        name=name,
    )(a, b, wa, wb, x, nw, rw, rb)


def _round_up(v, m):
    return ((v + (m - 1)) // m) * m


def _moe_kernel(cnt_ref, h_ref, krow_ref, gk_ref, wgu_ref, wd_ref, x_ref, fnw_ref, o_ref,
                hc_s, gc_s, y_s, *, final_norm):
    i = pl.program_id(0)
    g = pl.program_id(1)
    tm = h_ref.shape[0]
    n_sub = tm // MOE_SUB

    def segments(grp):
        cnts = [cnt_ref[(i * n_sub + s) * N_GROUPS + grp] for s in range(n_sub)]
        starts = [jnp.int32(0)]
        for s in range(n_sub):
            starts.append(starts[-1] + _round_up(cnts[s], MOE_ALIGN))
        return cnts, starts

    counts, offs = segments(g)
    total = offs[-1]
    key0 = g.astype(F32) * MOE_KEY_STRIDE
    sub_iota = _iota2((MOE_BLK, MOE_SUB), 0).astype(F32)
    lane_iota = _iota2((MOE_SUB, MOE_BLK), 1).astype(F32)

    def pack(s, w):
        rows = slice(s * MOE_SUB, (s + 1) * MOE_SUB)
        base = key0 + (w * MOE_BLK).astype(F32)
        sel = (krow_ref[:, rows] == sub_iota + base).astype(BF16)
        dst = pl.ds(pl.multiple_of(offs[s] + w * MOE_BLK, MOE_ALIGN), MOE_BLK)
        hc_s[dst, :] = jnp.dot(sel, h_ref[rows, :], preferred_element_type=F32).astype(BF16)
        gc_s[dst, :] = _mm_01(sel, gk_ref[rows, :])

    for s in range(n_sub):
        def pack_more(w, carry, s=s):
            pack(s, w)
            return carry

        lax.fori_loop(1, (counts[s] + MOE_BLK - 1) // MOE_BLK, pack_more, 0)
    for s in range(n_sub):
        pack(s, jnp.int32(0))

    tail_rows = MOE_FIRST[-1] + MOE_FIRST_STEP
    tail = pl.ds(pl.multiple_of(total, MOE_ALIGN), tail_rows)
    hc_s[tail, :] = jnp.zeros((tail_rows, hc_s.shape[1]), BF16)
    gc_s[tail, :] = jnp.zeros((tail_rows, LANES), F32)

    def expert_rows(start, rows):
        blk = pl.ds(pl.multiple_of(start, MOE_FIRST_STEP), rows)
        hb = hc_s[blk, :]
        gates = gc_s[blk, :]
        y = jnp.zeros((rows, o_ref.shape[1]), F32)
        for e in range(EXPERTS_PER_GROUP):
            gu = jnp.dot(hb, wgu_ref[0, e], preferred_element_type=F32)
            act = _silu(gu[:, :D_EXPERT]) * gu[:, D_EXPERT:] * gates[:, 1 + e:2 + e]
            y = y + jnp.dot(act.astype(BF16), wd_ref[0, e], preferred_element_type=F32)
        y_s[g, blk, :] = y.astype(BF16)

    first = jnp.clip(_round_up(total, MOE_FIRST_STEP), MOE_FIRST[0], MOE_FIRST[-1])
    for size in MOE_FIRST:
        @pl.when(first == size)
        def _(size=size):
            expert_rows(0, size)

    n_rest = (jnp.maximum(total - first, 0) + MOE_BLK - 1) // MOE_BLK

    def rest(bi, carry):
        expert_rows(first + bi * MOE_BLK, MOE_BLK)
        return carry

    lax.fori_loop(0, n_rest, rest, 0)
    done = pl.ds(pl.multiple_of(first + n_rest * MOE_BLK, MOE_FIRST_STEP), MOE_BLK)
    y_s[g, done, :] = jnp.zeros((MOE_BLK, y_s.shape[2]), BF16)

    @pl.when(g == N_GROUPS - 1)
    def _():
        segs = [segments(grp) for grp in range(N_GROUPS)]
        for s in range(n_sub):
            rows = slice(s * MOE_SUB, (s + 1) * MOE_SUB)
            key_col = gk_ref[rows, 0:1]
            sel = jnp.concatenate(
                [(key_col == lane_iota + grp * MOE_KEY_STRIDE).astype(BF16) for grp in range(N_GROUPS)], axis=1)
            packed = jnp.concatenate(
                [y_s[grp, pl.ds(pl.multiple_of(segs[grp][1][s], MOE_ALIGN), MOE_BLK), :]
                 for grp in range(N_GROUPS)], axis=0)
            o_ref[rows, :] = x_ref[rows, :] + jnp.dot(sel, packed, preferred_element_type=F32)
            for grp in range(N_GROUPS):
                def more(w, carry, s=s, rows=rows, grp=grp, key_col=key_col):
                    base = grp * MOE_KEY_STRIDE + (w * MOE_BLK).astype(F32)
                    sel_w = (key_col == lane_iota + base).astype(BF16)
                    src = pl.ds(pl.multiple_of(segs[grp][1][s] + w * MOE_BLK, MOE_ALIGN), MOE_BLK)
                    o_ref[rows, :] += jnp.dot(sel_w, y_s[grp, src, :], preferred_element_type=F32)
                    return carry

                lax.fori_loop(1, (segs[grp][0][s] + MOE_BLK - 1) // MOE_BLK, more, 0)
        if final_norm:
            y = o_ref[...]
            ms = jnp.mean(y * y, axis=-1, keepdims=True)
            o_ref[...] = y * lax.rsqrt(ms + RMS_EPS) * fnw_ref[...]


def _moe(h, krow, gk, counts, wgu, wd, x, fnw, *, layer, tm, final_norm, name):
    n, d = x.shape
    buf_rows = _round_up(tm + (tm // MOE_SUB) * MOE_ALIGN + MOE_FIRST[-1] + MOE_FIRST_STEP, MOE_BLK)
    grid_spec = pltpu.PrefetchScalarGridSpec(
        num_scalar_prefetch=1,
        grid=(n // tm, N_GROUPS),
        in_specs=[
            pl.BlockSpec((tm, d), lambda i, g, c: (i, 0)),
            pl.BlockSpec((1, tm), lambda i, g, c: (0, i)),
            pl.BlockSpec((tm, LANES), lambda i, g, c: (i, 0)),
            pl.BlockSpec((1, EXPERTS_PER_GROUP, d, 2 * D_EXPERT), lambda i, g, c: (layer, g, 0, 0)),
            pl.BlockSpec((1, EXPERTS_PER_GROUP, D_EXPERT, d), lambda i, g, c: (layer, g, 0, 0)),
            pl.BlockSpec((tm, d), lambda i, g, c: (i, 0)),
            pl.BlockSpec((1, d), lambda i, g, c: (0, 0)),
        ],
        out_specs=pl.BlockSpec((tm, d), lambda i, g, c: (i, 0)),
        scratch_shapes=[pltpu.VMEM((buf_rows, d), BF16),
                        pltpu.VMEM((buf_rows, LANES), F32),
                        pltpu.VMEM((N_GROUPS, buf_rows, d), BF16)],
    )
    return pl.pallas_call(
        functools.partial(_moe_kernel, final_norm=final_norm),
        grid_spec=grid_spec,
        out_shape=jax.ShapeDtypeStruct((n, d), F32),
        compiler_params=pltpu.CompilerParams(
            dimension_semantics=("parallel", "arbitrary"), vmem_limit_bytes=VMEM_LIMIT),
        name=name,
    )(counts, h, krow, gk, wgu, wd, x, fnw)


def _pad_cols(a, width):
    return jnp.pad(a, ((0, 0), (0, width - a.shape[1])))


def _dispatch_counts(cnt, route_tm):
    c = cnt.reshape(-1, 8, LANES)[:, :N_GROUPS, :route_tm // MOE_SUB]
    return jnp.transpose(c, (0, 2, 1)).reshape(-1).astype(jnp.int32)


def _router_params(wc, bc, wf, bf):
    rw = _pad_cols(jnp.concatenate([wc, wf], axis=1), LANES)
    rb = _pad_cols(jnp.concatenate([bc, bf])[None, :], LANES)
    return rw, rb


def kernel(x, norm_mix_w, norm_ffn_w, even_w_in, gdn_conv_w, gdn_a_log, gdn_dt_bias, gdn_norm_w, hgrn_lb_logits, hgrn_norm_w, even_w_out, odd_w_in, ret_norm_w, odd_w_out, router_c_w, router_c_b, router_f_w, router_f_b, moe_w_gate_up, moe_w_down, final_norm_w):
    bsz, seq, d = x.shape
    n = bsz * seq
    xt = x.reshape(n, d)
    mix_w = N_HEADS * HEAD_D
    conv_cols = 3 * mix_w
    gdn_main = conv_cols + mix_w

    w_in = even_w_in[0]
    small0 = gdn_main
    w_main = [w_in[:, :small0].astype(BF16), w_in[:, small0 + 2 * N_HEADS:].astype(BF16)]
    w_small = jnp.concatenate(
        [_pad_cols(w_in[:, small0:small0 + N_HEADS], LANES),
         _pad_cols(w_in[:, small0 + N_HEADS:small0 + 2 * N_HEADS], LANES)], axis=1).astype(BF16)
    proj, small = _norm_matmul(xt, norm_mix_w[0][None, :], w_main, w_small, tm=512, tn=1024, name="in_proj_even")
    proj = proj.reshape(bsz, seq, -1)
    small = small.reshape(bsz, seq, -1)
    o_a = _gdn(proj, small, gdn_conv_w[0], _pad_cols(gdn_a_log[0][None, :], LANES),
               _pad_cols(gdn_dt_bias[0][None, :], LANES), gdn_norm_w[0][None, :], tb=512)
    o_b = _hgrn(proj, hgrn_lb_logits, hgrn_norm_w[0][None, :], tb=512, layer=0, col0=gdn_main // mix_w)
    w_out = even_w_out[0].astype(BF16)
    rw, rb = _router_params(router_c_w[0], router_c_b[0], router_f_w[0], router_f_b[0])
    wgu_b = moe_w_gate_up.astype(BF16)
    wd_b = moe_w_down.astype(BF16)
    x1, h, gk, krow, cnt = _outproj_route(o_a.reshape(n, mix_w), o_b.reshape(n, mix_w), 0, 0,
                                          w_out[:mix_w], w_out[mix_w:], xt, norm_ffn_w[0][None, :], rw, rb,
                                          tm=ROUTE_TM, name="out_proj_even")
    x2 = _moe(h, krow, gk, _dispatch_counts(cnt, ROUTE_TM), wgu_b, wd_b, x1,
              final_norm_w[None, :], layer=0, tm=MOE_TM, final_norm=False, name="moe0")

    qk_w = 2 * N_HEADS * RET_DK
    w_qk = odd_w_in[0][:, :qk_w].reshape(d, 2 * N_HEADS, RET_DK // 2, 2)
    w_qk = jnp.swapaxes(w_qk, 2, 3).reshape(d, qk_w).astype(BF16)
    (proj1,) = _norm_matmul(x2, norm_mix_w[1][None, :], [w_qk, odd_w_in[0][:, qk_w:].astype(BF16)], None,
                            tm=512, tn=1024, name="in_proj_odd")
    inv = 1.0 / (ROPE_BASE ** jnp.linspace(0.0, 1.0, RET_DK // 2, dtype=F32))
    ang = jnp.arange(seq, dtype=F32)[:, None] * inv[None, :]
    o_c = _retention(proj1.reshape(bsz, seq, -1), jnp.cos(ang), jnp.sin(ang), ret_norm_w[0][None, :],
                     tc=RET_CHUNK)
    o_c = o_c.reshape(n, -1)
    w_out1 = odd_w_out[0].astype(BF16)
    half = w_out1.shape[0] // 2
    rw, rb = _router_params(router_c_w[1], router_c_b[1], router_f_w[1], router_f_b[1])
    x3, h, gk, krow, cnt = _outproj_route(o_c, o_c, 0, 1, w_out1[:half], w_out1[half:], x2,
                                          norm_ffn_w[1][None, :], rw, rb, tm=ROUTE_TM, name="out_proj_odd")
    out = _moe(h, krow, gk, _dispatch_counts(cnt, ROUTE_TM), wgu_b, wd_b, x3,
               final_norm_w[None, :], layer=1, tm=MOE_TM, final_norm=True, name="moe1")
    return out.reshape(bsz, seq, d)
```

```python
import functools
import math

import jax
import jax.numpy as jnp
from jax import lax
from jax.experimental import pallas as pl
from jax.experimental.pallas import tpu as pltpu

F32 = jnp.float32
BF16 = jnp.bfloat16

D_MODEL = 1024
RMS_EPS = 1e-6
L2_EPS = 1e-6
CHUNK = 64
CONV_K = 4
N_HEADS = 4
HEAD_D = 128
RET_DK = 256
RET_DV = 512
RET_CHUNK = 256
ROPE_BASE = 10000.0
N_GROUPS = 4
EXPERTS_PER_GROUP = 4
N_EXPERTS = 16
D_EXPERT = 256
LANES = 128
ROUTE_TM = 1024
MOE_TM = 1024
MOE_SUB = 256
MOE_BLK = 128
MOE_ALIGN = 16
MOE_KEY_STRIDE = 4096.0
MOE_FIRST_STEP = 32
MOE_FIRST = (256, 288, 320, 352)
VMEM_LIMIT = 56 * 1024 * 1024


def _mm(a, b):
    return jnp.dot(a.astype(BF16), b.astype(BF16), preferred_element_type=F32)


def _mm_nt(a, b):
    return lax.dot_general(a.astype(BF16), b.astype(BF16), (((1,), (1,)), ((), ())),
                           preferred_element_type=F32)


def _mm_tn(a, b):
    return _mm(a.T, b)


def _mm_01(m01, x):
    hi = x.astype(BF16)
    lo = (x - hi.astype(F32)).astype(BF16)
    return (jnp.dot(m01, hi, preferred_element_type=F32)
            + jnp.dot(m01, lo, preferred_element_type=F32))


def _bmm(a, b):
    return jnp.einsum('cik,ckj->cij', a.astype(BF16), b.astype(BF16), preferred_element_type=F32)


def _bmm_nt(a, b):
    return jnp.einsum('cik,cjk->cij', a.astype(BF16), b.astype(BF16), preferred_element_type=F32)


def _bmm_tn(a, b):
    return _bmm(jnp.swapaxes(a, 1, 2), b)


def _bmm_01(m01, x):
    hi = x.astype(BF16)
    lo = (x - hi.astype(F32)).astype(BF16)
    return (jnp.einsum('cik,ckj->cij', m01, hi, preferred_element_type=F32)
            + jnp.einsum('cik,ckj->cij', m01, lo, preferred_element_type=F32))


def _sigmoid(x):
    return 1.0 / (1.0 + jnp.exp(-x))


def _silu(x):
    return x * _sigmoid(x)


def _softplus(x):
    return jnp.maximum(x, 0.0) + jnp.log(1.0 + jnp.exp(-jnp.abs(x)))


def _iota2(shape, dim):
    return lax.broadcasted_iota(jnp.int32, shape, dim)


def _head_rms_gate(o, gate, nw, width):
    outs = []
    for h in range(o.shape[1] // width):
        oh = o[:, h * width:(h + 1) * width]
        ms = jnp.mean(oh * oh, axis=-1, keepdims=True)
        outs.append(oh * lax.rsqrt(ms + RMS_EPS) * nw)
    return jnp.concatenate(outs, axis=1) * _silu(gate)


def _norm_mm_kernel(*refs, tn, n_w, has_small, n_cast):
    x_ref, nw_ref = refs[:2]
    w_refs = refs[2:2 + n_w]
    rest = refs[2 + n_w:]
    if has_small:
        ws_ref, rest = rest[0], rest[1:]
    cast_in, rest = rest[:n_cast], rest[n_cast:]
    o_ref = rest[0]
    if has_small:
        os_ref = rest[1]
    cast_out = rest[len(rest) - n_cast:]
    for src, dst in zip(cast_in, cast_out):
        dst[...] = src[...].astype(dst.dtype)
    x = x_ref[...]
    ms = jnp.mean(x * x, axis=-1, keepdims=True)
    hb = (x * lax.rsqrt(ms + RMS_EPS) * nw_ref[...]).astype(BF16)
    if has_small:
        os_ref[...] = jnp.dot(hb, ws_ref[...], preferred_element_type=F32)
    c0 = 0
    for w_ref in w_refs:
        for j in range(w_ref.shape[1] // tn):
            res = jnp.dot(hb, w_ref[:, j * tn:(j + 1) * tn], preferred_element_type=F32)
            o_ref[:, c0 + j * tn:c0 + (j + 1) * tn] = res.astype(o_ref.dtype)
        c0 += w_ref.shape[1]


def _norm_matmul(x, nw, ws_main, ws, *, tm, tn, name, cast=()):
    n, d = x.shape
    steps = n // tm
    nout = sum(w.shape[1] for w in ws_main)
    has_small = ws is not None
    in_specs = [
        pl.BlockSpec((tm, d), lambda i: (i, 0)),
        pl.BlockSpec((1, d), lambda i: (0, 0)),
    ] + [pl.BlockSpec(w.shape, lambda i: (0, 0)) for w in ws_main]
    out_specs = [pl.BlockSpec((tm, nout), lambda i: (i, 0))]
    out_shape = [jax.ShapeDtypeStruct((n, nout), BF16)]
    args = [x, nw] + list(ws_main)
    if has_small:
        nsmall = ws.shape[1]
        in_specs.append(pl.BlockSpec((d, nsmall), lambda i: (0, 0)))
        out_specs.append(pl.BlockSpec((tm, nsmall), lambda i: (i, 0)))
        out_shape.append(jax.ShapeDtypeStruct((n, nsmall), F32))
        args.append(ws)
    for a in cast:
        slab = (a.shape[0] // steps, a.shape[1])
        in_specs.append(pl.BlockSpec(slab, lambda i: (i, 0)))
        out_specs.append(pl.BlockSpec(slab, lambda i: (i, 0)))
        out_shape.append(jax.ShapeDtypeStruct(a.shape, BF16))
        args.append(a)
    return pl.pallas_call(
        functools.partial(_norm_mm_kernel, tn=tn, n_w=len(ws_main), has_small=has_small, n_cast=len(cast)),
        grid=(n // tm,),
        in_specs=in_specs,
        out_specs=out_specs,
        out_shape=out_shape,
        compiler_params=pltpu.CompilerParams(
            dimension_semantics=("parallel",), vmem_limit_bytes=VMEM_LIMIT),
        name=name,
    )(*args)


def _gdn_kernel(qkv_ref, z_ref, sm_ref, cw_ref, alog_ref, dtb_ref, nw_ref, o_ref,
                xs_ref, qkv_s, o_s, m_s, sq_s, qe_s, dec_s, st_ref, *, tb):
    @pl.when(pl.program_id(1) == 0)
    def _():
        xs_ref[0:8, :] = jnp.zeros((8, xs_ref.shape[1]), F32)
        st_ref[...] = jnp.zeros(st_ref.shape, F32)

    c = CHUNK
    nc = tb // c
    nb = N_HEADS * nc
    qk_w = N_HEADS * HEAD_D
    row = _iota2((c, c), 0)
    col = _iota2((c, c), 1)
    causal = (row >= col)[None]
    strict = (row > col)[None]
    same16 = ((row >> 4) == (col >> 4))[None]
    same32 = ((row >> 5) == (col >> 5))[None]
    eye = (row == col).astype(F32)[None]
    ltri = jnp.broadcast_to((row >= col).astype(BF16)[None], (nc, c, c))
    lane = _iota2((nb, c, LANES), 2)

    def per_head(fn):
        return jnp.concatenate([fn(h) for h in range(N_HEADS)], axis=0)

    for h in range(N_HEADS):
        for base, scale in ((0, HEAD_D ** -0.5), (qk_w, 1.0), (2 * qk_w, None)):
            cols = slice(base + h * HEAD_D, base + (h + 1) * HEAD_D)
            x = qkv_ref[0, :, cols].astype(F32)
            xs_ref[8:8 + tb, cols] = x
            cw = cw_ref[:, cols]
            y = x * cw[CONV_K - 1:CONV_K, :]
            for j in range(1, CONV_K):
                y = y + xs_ref[pl.ds(8 - j, tb), cols] * cw[CONV_K - 1 - j:CONV_K - j, :]
            y = _silu(y)
            if scale is not None:
                y = y * (lax.rsqrt(jnp.sum(y * y, axis=-1, keepdims=True) + L2_EPS) * scale)
            qkv_s[:, cols] = y
    xs_ref[0:8, :] = xs_ref[tb:tb + 8, :]

    sm = sm_ref[0]
    beta_all = _sigmoid(sm[:, :LANES]).reshape(nc, c, LANES)
    g_all = (-jnp.exp(alog_ref[...]) * _softplus(sm[:, LANES:] + dtb_ref[...])).reshape(nc, c, LANES)
    gc_all = _bmm_01(ltri, g_all)
    q = per_head(lambda h: qkv_s[:, h * HEAD_D:(h + 1) * HEAD_D].reshape(nc, c, HEAD_D))
    k = per_head(lambda h: qkv_s[:, qk_w + h * HEAD_D:qk_w + (h + 1) * HEAD_D].reshape(nc, c, HEAD_D))
    v = per_head(lambda h: qkv_s[:, 2 * qk_w + h * HEAD_D:2 * qk_w + (h + 1) * HEAD_D].reshape(nc, c, HEAD_D))
    beta = per_head(lambda h: beta_all[:, :, h:h + 1])
    gcol = per_head(lambda h: gc_all[:, :, h:h + 1])
    g_hi = gcol.astype(BF16).astype(F32)
    g_mid = (gcol - g_hi).astype(BF16).astype(F32)
    g_lo = gcol - g_hi - g_mid
    ones_hi = jnp.where(lane < 6, 1.0, 0.0)
    lhs = jnp.where(lane == 0, g_hi, jnp.where(lane == 1, g_mid, jnp.where(lane == 2, g_lo, ones_hi)))
    rhs_g = jnp.where(lane == 3, -g_hi, jnp.where(lane == 4, -g_mid, jnp.where(lane == 5, -g_lo, ones_hi)))
    decay = jnp.exp(jnp.where(causal, _bmm_nt(lhs, rhs_g), -jnp.inf))
    kb = k * beta
    a = jnp.where(strict, _bmm_nt(kb, k) * decay, 0.0)
    d = jnp.where(same16, a, 0.0)
    x_inv = eye - d
    dp = d
    for _ in range(3):
        dp = _bmm(dp, dp)
        x_inv = x_inv + _bmm(x_inv, dp)
    e = jnp.where(jnp.logical_and(same32, jnp.logical_not(same16)), a, 0.0)
    x_inv = x_inv - _bmm(x_inv, _bmm(e, x_inv))
    f = jnp.where(same32, 0.0, a)
    x_inv = x_inv - _bmm(x_inv, _bmm(f, x_inv))
    wu = _bmm(x_inv, jnp.concatenate([kb * jnp.exp(gcol), v * beta], axis=2))
    attn = _bmm_nt(q, k) * decay
    g_last = gcol[:, c - 1:c, :]
    kd = k * jnp.exp(g_last - gcol)
    mq = _bmm_tn(kd, wu)
    aw = _bmm(attn, wu)
    m_s[...] = mq[:, :, :HEAD_D].astype(BF16).reshape(N_HEADS, nc, HEAD_D, HEAD_D)
    sq_s[...] = mq[:, :, HEAD_D:].reshape(N_HEADS, nc, HEAD_D, HEAD_D)
    qe_s[...] = (q * jnp.exp(gcol) - aw[:, :, :HEAD_D]).astype(BF16).reshape(N_HEADS, nc, c, HEAD_D)
    dec_s[...] = jnp.broadcast_to(jnp.exp(g_last), (nb, 1, HEAD_D)).reshape(N_HEADS, nc, 1, HEAD_D)
    for h in range(N_HEADS):
        o_s[:, h * HEAD_D:(h + 1) * HEAD_D] = aw[h * nc:(h + 1) * nc, :, HEAD_D:].reshape(tb, HEAD_D)

    for ci in range(nc):
        for h in range(N_HEADS):
            hs = slice(h * HEAD_D, (h + 1) * HEAD_D)
            s_h = st_ref[h]
            s_b = s_h.astype(BF16)
            o_s[ci * c:(ci + 1) * c, hs] += jnp.dot(qe_s[h, ci], s_b, preferred_element_type=F32)
            st_ref[h] = (s_h * dec_s[h, ci] - jnp.dot(m_s[h, ci], s_b, preferred_element_type=F32)
                         + sq_s[h, ci])

    o_ref[0] = _head_rms_gate(o_s[...], z_ref[0].astype(F32), nw_ref[...], HEAD_D).astype(o_ref.dtype)


def _gdn(proj, small, conv_w, a_log, dt_bias, norm_w, *, tb):
    b, t, _ = proj.shape
    conv_cols = 3 * N_HEADS * HEAD_D
    mix_w = N_HEADS * HEAD_D
    return pl.pallas_call(
        functools.partial(_gdn_kernel, tb=tb),
        grid=(b, t // tb),
        in_specs=[
            pl.BlockSpec((1, tb, conv_cols), lambda i, j: (i, j, 0)),
            pl.BlockSpec((1, tb, mix_w), lambda i, j: (i, j, conv_cols // mix_w)),
            pl.BlockSpec((1, tb, 2 * LANES), lambda i, j: (i, j, 0)),
            pl.BlockSpec((CONV_K, conv_cols), lambda i, j: (0, 0)),
            pl.BlockSpec((1, LANES), lambda i, j: (0, 0)),
            pl.BlockSpec((1, LANES), lambda i, j: (0, 0)),
            pl.BlockSpec((1, HEAD_D), lambda i, j: (0, 0)),
        ],
        out_specs=pl.BlockSpec((1, tb, mix_w), lambda i, j: (i, j, 0)),
        out_shape=jax.ShapeDtypeStruct((b, t, mix_w), BF16),
        scratch_shapes=[
            pltpu.VMEM((tb + 8, conv_cols), F32),
            pltpu.VMEM((tb, conv_cols), F32),
            pltpu.VMEM((tb, mix_w), F32),
            pltpu.VMEM((N_HEADS, tb // CHUNK, HEAD_D, HEAD_D), BF16),
            pltpu.VMEM((N_HEADS, tb // CHUNK, HEAD_D, HEAD_D), F32),
            pltpu.VMEM((N_HEADS, tb // CHUNK, CHUNK, HEAD_D), BF16),
            pltpu.VMEM((N_HEADS, tb // CHUNK, 1, HEAD_D), F32),
            pltpu.VMEM((N_HEADS, HEAD_D, HEAD_D), F32),
        ],
        compiler_params=pltpu.CompilerParams(
            dimension_semantics=("parallel", "arbitrary"), vmem_limit_bytes=VMEM_LIMIT),
        name="gdn",
    )(proj, proj, small, conv_w, a_log, dt_bias, norm_w)


def _hgrn_kernel(q_ref, f_ref, i_ref, g_ref, lb_ref, nw_ref, o_ref,
                 q_s, k_s, lf_s, i_s, o_s, st_ref, *, tb, layer):
    @pl.when(pl.program_id(1) == 0)
    def _():
        st_ref[...] = jnp.zeros(st_ref.shape, F32)

    lbl = lb_ref[...]
    e_lb = jnp.exp(lbl - jnp.max(lbl, axis=0, keepdims=True))
    lb = jnp.sum(e_lb[:layer + 1], axis=0, keepdims=True) / jnp.sum(e_lb, axis=0, keepdims=True)

    f = lb + (1.0 - lb) * _sigmoid(f_ref[0].astype(F32))
    k_s[...] = 1.0 - f
    lf_s[...] = jnp.log(f)
    q_s[...] = _silu(q_ref[0].astype(F32)) * (HEAD_D ** -0.5)
    i_s[...] = i_ref[0].astype(F32)

    c = CHUNK
    nc = tb // c
    blk = 8
    row = _iota2((c, c), 0)
    col = _iota2((c, c), 1)
    ltri = jnp.broadcast_to((row >= col).astype(BF16)[None], (nc, c, c))
    level_masks = {}
    for m in (32, 16, 8):
        sh = int(math.log2(2 * m))
        level_masks[m] = jnp.logical_and(
            (row >> sh) == (col >> sh),
            jnp.logical_and((row & (2 * m - 1)) >= m, (col & (2 * m - 1)) < m))[None]
    nb = N_HEADS * nc
    sub = _iota2((nb * c // blk, blk, HEAD_D), 1)

    def per_head(fn):
        return jnp.concatenate([fn(slice(h * HEAD_D, (h + 1) * HEAD_D)) for h in range(N_HEADS)], axis=0)

    b_all = _bmm_01(ltri, lf_s[...].reshape(nc, c, N_HEADS * HEAD_D))
    b = per_head(lambda sl: b_all[:, :, sl])
    q = per_head(lambda sl: q_s[:, sl].reshape(nc, c, HEAD_D))
    k = per_head(lambda sl: k_s[:, sl].reshape(nc, c, HEAD_D))
    iv = per_head(lambda sl: i_s[:, sl].reshape(nc, c, HEAD_D))
    attn = jnp.zeros((nb, c, c), F32)
    for m in (32, 16, 8):
        b_m = b.reshape(nb * c // (2 * m), 2 * m, HEAD_D)
        ref = jnp.broadcast_to(b_m[:, m:m + 1, :], b_m.shape).reshape(nb, c, HEAD_D)
        e = jnp.exp(-jnp.abs(b - ref))
        attn = attn + jnp.where(level_masks[m], _bmm_nt(q * e, k * e), 0.0)
    o = _bmm(attn, iv)
    qb, kb, bb, ib = (a.reshape(nb * c // blk, blk, HEAD_D) for a in (q, k, b, iv))
    ob = jnp.zeros(qb.shape, F32)
    for s in range(blk):
        dec = jnp.exp(jnp.where(sub >= s, bb - bb[:, s:s + 1, :], -jnp.inf))
        a_col = jnp.sum(dec * qb * kb[:, s:s + 1, :], axis=-1, keepdims=True)
        ob = ob + a_col * ib[:, s:s + 1, :]
    o = o + ob.reshape(nb, c, HEAD_D)
    b_last = b[:, c - 1:c, :]
    q_in = (q * jnp.exp(b)).astype(BF16)
    upd = _bmm_tn(iv, k * jnp.exp(b_last - b))
    keep = jnp.exp(b_last)
    for h in range(N_HEADS):
        s_t = st_ref[h]
        outs = []
        for ci in range(h * nc, (h + 1) * nc):
            outs.append(o[ci] + _mm_nt(q_in[ci], s_t))
            s_t = s_t * keep[ci] + upd[ci]
        st_ref[h] = s_t
        o_s[:, h * HEAD_D:(h + 1) * HEAD_D] = jnp.concatenate(outs, axis=0)

    o_ref[0] = _head_rms_gate(o_s[...], g_ref[0].astype(F32), nw_ref[...], HEAD_D).astype(o_ref.dtype)


def _hgrn(proj, lb_logits, norm_w, *, tb, layer, col0):
    b, t, _ = proj.shape
    w = N_HEADS * HEAD_D
    specs = [pl.BlockSpec((1, tb, w), functools.partial(lambda i, j, off: (i, j, off), off=col0 + n))
             for n in range(4)]
    return pl.pallas_call(
        functools.partial(_hgrn_kernel, tb=tb, layer=layer),
        grid=(b, t // tb),
        in_specs=specs + [
            pl.BlockSpec(lb_logits.shape, lambda i, j: (0, 0)),
            pl.BlockSpec((1, HEAD_D), lambda i, j: (0, 0)),
        ],
        out_specs=pl.BlockSpec((1, tb, w), lambda i, j: (i, j, 0)),
        out_shape=jax.ShapeDtypeStruct((b, t, w), BF16),
        scratch_shapes=[pltpu.VMEM((tb, w), F32)] * 5 + [pltpu.VMEM((N_HEADS, HEAD_D, HEAD_D), F32)],
        compiler_params=pltpu.CompilerParams(
            dimension_semantics=("parallel", "arbitrary"), vmem_limit_bytes=VMEM_LIMIT),
        name="hgrn2",
    )(proj, proj, proj, proj, lb_logits, norm_w)


def _ret_kernel(q_ref, k_ref, v_ref, g_ref, cos_ref, sin_ref, nw_ref, o_ref, dmat_s, st_ref, *, tc):
    log_gammas = [math.log(1.0 - 2.0 ** (-5.0 - h)) for h in range(N_HEADS)]

    @pl.when(pl.program_id(1) == 0)
    def _():
        st_ref[...] = jnp.zeros(st_ref.shape, F32)
        rel = (_iota2((tc, tc), 0) - _iota2((tc, tc), 1)).astype(F32)
        for h in range(N_HEADS):
            dmat_s[h] = jnp.where(rel >= 0, jnp.exp(jnp.maximum(rel, 0.0) * log_gammas[h]), 0.0)

    pos = _iota2((tc, 1), 0).astype(F32)
    cos = cos_ref[...]
    sin = sin_ref[...]
    half = RET_DK // 2

    def rot(x):
        x1, x2 = x[:, :half], x[:, half:]
        return jnp.concatenate([x1 * cos - x2 * sin, x1 * sin + x2 * cos], axis=1)

    for h in range(N_HEADS):
        lg = log_gammas[h]
        qk_cols = slice(h * RET_DK, (h + 1) * RET_DK)
        v_cols = slice(h * RET_DV, (h + 1) * RET_DV)
        q = rot(q_ref[0, :, qk_cols].astype(F32))
        k = rot(k_ref[0, :, qk_cols].astype(F32)) * (RET_DK ** -0.5)
        v = v_ref[0, :, v_cols]
        s = st_ref[h]
        attn = _mm_nt(q, k) * dmat_s[h]
        o = _mm(q * jnp.exp((pos + 1.0) * lg), s) + _mm(attn, v)
        st_ref[h] = s * math.exp(tc * lg) + _mm_tn(k * jnp.exp((tc - 1.0 - pos) * lg), v)
        ms = jnp.mean(o * o, axis=-1, keepdims=True)
        o_ref[0, :, v_cols] = (o * lax.rsqrt(ms + RMS_EPS) * nw_ref[...]
                               * _silu(g_ref[0, :, v_cols].astype(F32))).astype(o_ref.dtype)


def _retention(proj, cos, sin, norm_w, *, tc):
    b, t, _ = proj.shape
    qk_w = N_HEADS * RET_DK
    v_w = N_HEADS * RET_DV
    return pl.pallas_call(
        functools.partial(_ret_kernel, tc=tc),
        grid=(b, t // tc),
        in_specs=[
            pl.BlockSpec((1, tc, qk_w), lambda i, j: (i, j, 0)),
            pl.BlockSpec((1, tc, qk_w), lambda i, j: (i, j, 1)),
            pl.BlockSpec((1, tc, v_w), lambda i, j: (i, j, 1)),
            pl.BlockSpec((1, tc, v_w), lambda i, j: (i, j, 2)),
            pl.BlockSpec((tc, RET_DK // 2), lambda i, j: (j, 0)),
            pl.BlockSpec((tc, RET_DK // 2), lambda i, j: (j, 0)),
            pl.BlockSpec((1, RET_DV), lambda i, j: (0, 0)),
        ],
        out_specs=pl.BlockSpec((1, tc, v_w), lambda i, j: (i, j, 0)),
        out_shape=jax.ShapeDtypeStruct((b, t, v_w), BF16),
        scratch_shapes=[pltpu.VMEM((N_HEADS, tc, tc), F32), pltpu.VMEM((N_HEADS, RET_DK, RET_DV), F32)],
        compiler_params=pltpu.CompilerParams(
            dimension_semantics=("parallel", "arbitrary"), vmem_limit_bytes=VMEM_LIMIT),
        name="retention",
    )(proj, proj, proj, proj, cos, sin, norm_w)


def _route_gates(logits_t):
    cl = [logits_t[g:g + 1, :] for g in range(N_GROUPS)]
    cmax = functools.reduce(jnp.maximum, cl)
    denom = sum(jnp.exp(x - cmax) for x in cl)
    g_prob = 1.0 / denom
    g_idx = jnp.full(cmax.shape, N_GROUPS - 1, jnp.int32)
    for g in range(N_GROUPS - 2, -1, -1):
        g_idx = jnp.where(cl[g] == cmax, g, g_idx)
    def fine_row(g, j):
        r = N_GROUPS + g * EXPERTS_PER_GROUP + j
        return logits_t[r:r + 1, :]

    fl = []
    for j in range(EXPERTS_PER_GROUP):
        x = fine_row(N_GROUPS - 1, j)
        for g in range(N_GROUPS - 2, -1, -1):
            x = jnp.where(g_idx == g, fine_row(g, j), x)
        fl.append(x)
    m1 = functools.reduce(jnp.maximum, fl)
    i1 = jnp.full(m1.shape, EXPERTS_PER_GROUP - 1, jnp.int32)
    for j in range(EXPERTS_PER_GROUP - 2, -1, -1):
        i1 = jnp.where(fl[j] == m1, j, i1)
    rest = [jnp.where(i1 == j, -jnp.inf, fl[j]) for j in range(EXPERTS_PER_GROUP)]
    m2 = functools.reduce(jnp.maximum, rest)
    i2 = jnp.full(m2.shape, EXPERTS_PER_GROUP - 1, jnp.int32)
    for j in range(EXPERTS_PER_GROUP - 2, -1, -1):
        i2 = jnp.where(jnp.logical_and(rest[j] == m2, i1 != j), j, i2)
    e2 = jnp.exp(m2 - m1)
    w1 = g_prob / (1.0 + e2)
    w2 = g_prob * e2 / (1.0 + e2)
    local = [jnp.where(i1 == j, w1, 0.0) + jnp.where(i2 == j, w2, 0.0) for j in range(EXPERTS_PER_GROUP)]
    return g_idx, local


def _outproj_kernel(a_ref, b_ref, wa_ref, wb_ref, x_ref, nw_ref, rw_ref, rb_ref,
                    x1_ref, h_ref, gk_ref, krow_ref, cnt_ref):
    x1 = (x_ref[...]
          + jnp.dot(a_ref[...], wa_ref[...], preferred_element_type=F32)
          + jnp.dot(b_ref[...], wb_ref[...], preferred_element_type=F32))
    x1_ref[...] = x1
    ms = jnp.mean(x1 * x1, axis=-1, keepdims=True)
    h = x1 * lax.rsqrt(ms + RMS_EPS) * nw_ref[...]
    h_ref[...] = h.astype(BF16)
    rw = rw_ref[...]
    hh = h.astype(BF16)
    hl = (h - hh.astype(F32)).astype(BF16)
    wh = rw.astype(BF16)
    wl = (rw - wh.astype(F32)).astype(BF16)
    hi_part = jnp.dot(hh, jnp.concatenate([wh, wl], axis=1), preferred_element_type=F32)
    logits = (hi_part[:, :LANES] + hi_part[:, LANES:]
              + jnp.dot(hl, wh, preferred_element_type=F32)) + rb_ref[...]
    g_idx, local = _route_gates(logits.T)
    tm = g_idx.shape[1]
    mem = (_iota2((8, tm), 0) == g_idx).astype(BF16)
    before = (_iota2((MOE_SUB, MOE_SUB), 0) < _iota2((MOE_SUB, MOE_SUB), 1)).astype(BF16)
    earlier = jnp.concatenate(
        [jnp.dot(mem[:, s:s + MOE_SUB], before, preferred_element_type=F32) for s in range(0, tm, MOE_SUB)],
        axis=1)
    rank = jnp.sum(mem.astype(F32) * earlier, axis=0, keepdims=True)
    key = g_idx.astype(F32) * MOE_KEY_STRIDE + rank
    krow_ref[...] = key
    sub_shift = int(math.log2(MOE_SUB))
    sub_sel = ((_iota2((tm, LANES), 0) >> sub_shift) == _iota2((tm, LANES), 1)).astype(BF16)
    cnt_ref[...] = jnp.dot(mem, sub_sel, preferred_element_type=F32)
    rows = jnp.concatenate([key] + local + [jnp.zeros((LANES - 1 - EXPERTS_PER_GROUP, tm), F32)], axis=0)
    gk_ref[...] = rows.T


def _outproj_route(a, b, a_blk, b_blk, wa, wb, x, nw, rw, rb, *, tm, name):
    n, d = x.shape
    ka = wa.shape[0]
    kb = wb.shape[0]
    return pl.pallas_call(
        _outproj_kernel,
        grid=(n // tm,),
        in_specs=[
            pl.BlockSpec((tm, ka), lambda i: (i, a_blk)),
            pl.BlockSpec((tm, kb), lambda i: (i, b_blk)),
            pl.BlockSpec((ka, d), lambda i: (0, 0)),
            pl.BlockSpec((kb, d), lambda i: (0, 0)),
            pl.BlockSpec((tm, d), lambda i: (i, 0)),
            pl.BlockSpec((1, d), lambda i: (0, 0)),
            pl.BlockSpec((d, LANES), lambda i: (0, 0)),
            pl.BlockSpec((1, LANES), lambda i: (0, 0)),
        ],
        out_specs=[
            pl.BlockSpec((tm, d), lambda i: (i, 0)),
            pl.BlockSpec((tm, d), lambda i: (i, 0)),
            pl.BlockSpec((tm, LANES), lambda i: (i, 0)),
            pl.BlockSpec((1, tm), lambda i: (0, i)),
            pl.BlockSpec((8, LANES), lambda i: (i, 0)),
        ],
        out_shape=[jax.ShapeDtypeStruct((n, d), F32),
                   jax.ShapeDtypeStruct((n, d), BF16),
                   jax.ShapeDtypeStruct((n, LANES), F32),
                   jax.ShapeDtypeStruct((1, n), F32),
                   jax.ShapeDtypeStruct((8 * (n // tm), LANES), F32)],
        compiler_params=pltpu.CompilerParams(
            dimension_semantics=("parallel",), vmem_limit_bytes=VMEM_LIMIT),
        name=name,
    )(a, b, wa, wb, x, nw, rw, rb)


def _round_up(v, m):
    return ((v + (m - 1)) // m) * m


def _moe_kernel(cnt_ref, h_ref, krow_ref, gk_ref, wgu_ref, wd_ref, x_ref, fnw_ref, o_ref,
                hc_s, gc_s, y_s, *, final_norm):
    i = pl.program_id(0)
    g = pl.program_id(1)
    tm = h_ref.shape[0]
    n_sub = tm // MOE_SUB

    def segments(grp):
        cnts = [cnt_ref[(i * n_sub + s) * N_GROUPS + grp] for s in range(n_sub)]
        starts = [jnp.int32(0)]
        for s in range(n_sub):
            starts.append(starts[-1] + _round_up(cnts[s], MOE_ALIGN))
        return cnts, starts

    counts, offs = segments(g)
    total = offs[-1]
    key0 = g.astype(F32) * MOE_KEY_STRIDE
    sub_iota = _iota2((MOE_BLK, MOE_SUB), 0).astype(F32)
    lane_iota = _iota2((MOE_SUB, MOE_BLK), 1).astype(F32)

    def pack(s, w):
        rows = slice(s * MOE_SUB, (s + 1) * MOE_SUB)
        base = key0 + (w * MOE_BLK).astype(F32)
        sel = (krow_ref[:, rows] == sub_iota + base).astype(BF16)
        dst = pl.ds(pl.multiple_of(offs[s] + w * MOE_BLK, MOE_ALIGN), MOE_BLK)
        hc_s[dst, :] = jnp.dot(sel, h_ref[rows, :], preferred_element_type=F32).astype(BF16)
        gc_s[dst, :] = _mm_01(sel, gk_ref[rows, :])

    for s in range(n_sub):
        def pack_more(w, carry, s=s):
            pack(s, w)
            return carry

        lax.fori_loop(1, (counts[s] + MOE_BLK - 1) // MOE_BLK, pack_more, 0)
    for s in range(n_sub):
        pack(s, jnp.int32(0))

    tail_rows = MOE_FIRST[-1] + MOE_FIRST_STEP
    tail = pl.ds(pl.multiple_of(total, MOE_ALIGN), tail_rows)
    hc_s[tail, :] = jnp.zeros((tail_rows, hc_s.shape[1]), BF16)
    gc_s[tail, :] = jnp.zeros((tail_rows, LANES), F32)

    def expert_rows(start, rows):
        blk = pl.ds(pl.multiple_of(start, MOE_FIRST_STEP), rows)
        hb = hc_s[blk, :]
        gates = gc_s[blk, :]
        y = jnp.zeros((rows, o_ref.shape[1]), F32)
        for e in range(EXPERTS_PER_GROUP):
            gu = jnp.dot(hb, wgu_ref[0, e], preferred_element_type=F32)
            act = _silu(gu[:, :D_EXPERT]) * gu[:, D_EXPERT:] * gates[:, 1 + e:2 + e]
            y = y + jnp.dot(act.astype(BF16), wd_ref[0, e], preferred_element_type=F32)
        y_s[g, blk, :] = y.astype(BF16)

    first = jnp.clip(_round_up(total, MOE_FIRST_STEP), MOE_FIRST[0], MOE_FIRST[-1])
    for size in MOE_FIRST:
        @pl.when(first == size)
        def _(size=size):
            expert_rows(0, size)

    n_rest = (jnp.maximum(total - first, 0) + MOE_BLK - 1) // MOE_BLK

    def rest(bi, carry):
        expert_rows(first + bi * MOE_BLK, MOE_BLK)
        return carry

    lax.fori_loop(0, n_rest, rest, 0)
    done = pl.ds(pl.multiple_of(first + n_rest * MOE_BLK, MOE_FIRST_STEP), MOE_BLK)
    y_s[g, done, :] = jnp.zeros((MOE_BLK, y_s.shape[2]), BF16)

    @pl.when(g == N_GROUPS - 1)
    def _():
        segs = [segments(grp) for grp in range(N_GROUPS)]
        for s in range(n_sub):
            rows = slice(s * MOE_SUB, (s + 1) * MOE_SUB)
            key_col = gk_ref[rows, 0:1]
            sel = jnp.concatenate(
                [(key_col == lane_iota + grp * MOE_KEY_STRIDE).astype(BF16) for grp in range(N_GROUPS)], axis=1)
            packed = jnp.concatenate(
                [y_s[grp, pl.ds(pl.multiple_of(segs[grp][1][s], MOE_ALIGN), MOE_BLK), :]
                 for grp in range(N_GROUPS)], axis=0)
            o_ref[rows, :] = x_ref[rows, :] + jnp.dot(sel, packed, preferred_element_type=F32)
            for grp in range(N_GROUPS):
                def more(w, carry, s=s, rows=rows, grp=grp, key_col=key_col):
                    base = grp * MOE_KEY_STRIDE + (w * MOE_BLK).astype(F32)
                    sel_w = (key_col == lane_iota + base).astype(BF16)
                    src = pl.ds(pl.multiple_of(segs[grp][1][s] + w * MOE_BLK, MOE_ALIGN), MOE_BLK)
                    o_ref[rows, :] += jnp.dot(sel_w, y_s[grp, src, :], preferred_element_type=F32)
                    return carry

                lax.fori_loop(1, (segs[grp][0][s] + MOE_BLK - 1) // MOE_BLK, more, 0)
        if final_norm:
            y = o_ref[...]
            ms = jnp.mean(y * y, axis=-1, keepdims=True)
            o_ref[...] = y * lax.rsqrt(ms + RMS_EPS) * fnw_ref[...]


def _moe(h, krow, gk, counts, wgu, wd, x, fnw, *, layer, tm, final_norm, name):
    n, d = x.shape
    buf_rows = _round_up(tm + (tm // MOE_SUB) * MOE_ALIGN + MOE_FIRST[-1] + MOE_FIRST_STEP, MOE_BLK)
    grid_spec = pltpu.PrefetchScalarGridSpec(
        num_scalar_prefetch=1,
        grid=(n // tm, N_GROUPS),
        in_specs=[
            pl.BlockSpec((tm, d), lambda i, g, c: (i, 0)),
            pl.BlockSpec((1, tm), lambda i, g, c: (0, i)),
            pl.BlockSpec((tm, LANES), lambda i, g, c: (i, 0)),
            pl.BlockSpec((1, EXPERTS_PER_GROUP, d, 2 * D_EXPERT), lambda i, g, c: (layer, g, 0, 0)),
            pl.BlockSpec((1, EXPERTS_PER_GROUP, D_EXPERT, d), lambda i, g, c: (layer, g, 0, 0)),
            pl.BlockSpec((tm, d), lambda i, g, c: (i, 0)),
            pl.BlockSpec((1, d), lambda i, g, c: (0, 0)),
        ],
        out_specs=pl.BlockSpec((tm, d), lambda i, g, c: (i, 0)),
        scratch_shapes=[pltpu.VMEM((buf_rows, d), BF16),
                        pltpu.VMEM((buf_rows, LANES), F32),
                        pltpu.VMEM((N_GROUPS, buf_rows, d), BF16)],
    )
    return pl.pallas_call(
        functools.partial(_moe_kernel, final_norm=final_norm),
        grid_spec=grid_spec,
        out_shape=jax.ShapeDtypeStruct((n, d), F32),
        compiler_params=pltpu.CompilerParams(
            dimension_semantics=("parallel", "arbitrary"), vmem_limit_bytes=VMEM_LIMIT),
        name=name,
    )(counts, h, krow, gk, wgu, wd, x, fnw)


def _pad_cols(a, width):
    return jnp.pad(a, ((0, 0), (0, width - a.shape[1])))


def _dispatch_counts(cnt, route_tm):
    c = cnt.reshape(-1, 8, LANES)[:, :N_GROUPS, :route_tm // MOE_SUB]
    return jnp.transpose(c, (0, 2, 1)).reshape(-1).astype(jnp.int32)


def _router_params(wc, bc, wf, bf):
    rw = _pad_cols(jnp.concatenate([wc, wf], axis=1), LANES)
    rb = _pad_cols(jnp.concatenate([bc, bf])[None, :], LANES)
    return rw, rb


def kernel(x, norm_mix_w, norm_ffn_w, even_w_in, gdn_conv_w, gdn_a_log, gdn_dt_bias, gdn_norm_w, hgrn_lb_logits, hgrn_norm_w, even_w_out, odd_w_in, ret_norm_w, odd_w_out, router_c_w, router_c_b, router_f_w, router_f_b, moe_w_gate_up, moe_w_down, final_norm_w):
    bsz, seq, d = x.shape
    n = bsz * seq
    xt = x.reshape(n, d)
    mix_w = N_HEADS * HEAD_D
    conv_cols = 3 * mix_w
    gdn_main = conv_cols + mix_w

    w_in = even_w_in[0]
    small0 = gdn_main
    w_main = [w_in[:, :small0].astype(BF16), w_in[:, small0 + 2 * N_HEADS:].astype(BF16)]
    w_small = jnp.concatenate(
        [_pad_cols(w_in[:, small0:small0 + N_HEADS], LANES),
         _pad_cols(w_in[:, small0 + N_HEADS:small0 + 2 * N_HEADS], LANES)], axis=1).astype(BF16)
    proj, small, wgu_b, wd_b, w_out, w_odd, w_out1 = _norm_matmul(
        xt, norm_mix_w[0][None, :], w_main, w_small, tm=512, tn=1024, name="in_proj_even",
        cast=(moe_w_gate_up.reshape(-1, moe_w_gate_up.shape[-1]), moe_w_down.reshape(-1, moe_w_down.shape[-1]),
              even_w_out[0], odd_w_in[0], odd_w_out[0]))
    wgu_b = wgu_b.reshape(moe_w_gate_up.shape)
    wd_b = wd_b.reshape(moe_w_down.shape)
    proj = proj.reshape(bsz, seq, -1)
    small = small.reshape(bsz, seq, -1)
    o_a = _gdn(proj, small, gdn_conv_w[0], _pad_cols(gdn_a_log[0][None, :], LANES),
               _pad_cols(gdn_dt_bias[0][None, :], LANES), gdn_norm_w[0][None, :], tb=512)
    o_b = _hgrn(proj, hgrn_lb_logits, hgrn_norm_w[0][None, :], tb=512, layer=0, col0=gdn_main // mix_w)
    rw, rb = _router_params(router_c_w[0], router_c_b[0], router_f_w[0], router_f_b[0])
    x1, h, gk, krow, cnt = _outproj_route(o_a.reshape(n, mix_w), o_b.reshape(n, mix_w), 0, 0,
                                          w_out[:mix_w], w_out[mix_w:], xt, norm_ffn_w[0][None, :], rw, rb,
                                          tm=ROUTE_TM, name="out_proj_even")
    x2 = _moe(h, krow, gk, _dispatch_counts(cnt, ROUTE_TM), wgu_b, wd_b, x1,
              final_norm_w[None, :], layer=0, tm=MOE_TM, final_norm=False, name="moe0")

    qk_w = 2 * N_HEADS * RET_DK
    w_qk = w_odd[:, :qk_w].reshape(d, 2 * N_HEADS, RET_DK // 2, 2)
    w_qk = jnp.swapaxes(w_qk, 2, 3).reshape(d, qk_w)
    (proj1,) = _norm_matmul(x2, norm_mix_w[1][None, :], [w_qk, w_odd[:, qk_w:]], None,
                            tm=512, tn=1024, name="in_proj_odd")
    inv = 1.0 / (ROPE_BASE ** jnp.linspace(0.0, 1.0, RET_DK // 2, dtype=F32))
    ang = jnp.arange(seq, dtype=F32)[:, None] * inv[None, :]
    o_c = _retention(proj1.reshape(bsz, seq, -1), jnp.cos(ang), jnp.sin(ang), ret_norm_w[0][None, :],
                     tc=RET_CHUNK)
    o_c = o_c.reshape(n, -1)
    half = w_out1.shape[0] // 2
    rw, rb = _router_params(router_c_w[1], router_c_b[1], router_f_w[1], router_f_b[1])
    x3, h, gk, krow, cnt = _outproj_route(o_c, o_c, 0, 1, w_out1[:half], w_out1[half:], x2,
                                          norm_ffn_w[1][None, :], rw, rb, tm=ROUTE_TM, name="out_proj_odd")
    out = _moe(h, krow, gk, _dispatch_counts(cnt, ROUTE_TM), wgu_b, wd_b, x3,
               final_norm_w[None, :], layer=1, tm=MOE_TM, final_norm=True, name="moe1")
    return out.reshape(bsz, seq, d)
```

```python
import functools
import math

import jax
import jax.numpy as jnp
from jax import lax
from jax.experimental import pallas as pl
from jax.experimental.pallas import tpu as pltpu

F32 = jnp.float32
BF16 = jnp.bfloat16

D_MODEL = 1024
RMS_EPS = 1e-6
L2_EPS = 1e-6
CHUNK = 64
CONV_K = 4
N_HEADS = 4
HEAD_D = 128
RET_DK = 256
RET_DV = 512
RET_CHUNK = 256
ROPE_BASE = 10000.0
N_GROUPS = 4
EXPERTS_PER_GROUP = 4
N_EXPERTS = 16
D_EXPERT = 256
LANES = 128
ROUTE_TM = 1024
MOE_TM = 1024
MOE_SUB = 256
MOE_BLK = 128
MOE_ALIGN = 16
MOE_KEY_STRIDE = 4096.0
MOE_FIRST_STEP = 32
MOE_FIRST = (256, 288, 320, 352)
VMEM_LIMIT = 56 * 1024 * 1024


def _mm(a, b):
    return jnp.dot(a.astype(BF16), b.astype(BF16), preferred_element_type=F32)


def _mm_nt(a, b):
    return lax.dot_general(a.astype(BF16), b.astype(BF16), (((1,), (1,)), ((), ())),
                           preferred_element_type=F32)


def _mm_tn(a, b):
    return _mm(a.T, b)


def _mm_01(m01, x):
    hi = x.astype(BF16)
    lo = (x - hi.astype(F32)).astype(BF16)
    return (jnp.dot(m01, hi, preferred_element_type=F32)
            + jnp.dot(m01, lo, preferred_element_type=F32))


def _bmm(a, b):
    return jnp.einsum('cik,ckj->cij', a.astype(BF16), b.astype(BF16), preferred_element_type=F32)


def _bmm_nt(a, b):
    return jnp.einsum('cik,cjk->cij', a.astype(BF16), b.astype(BF16), preferred_element_type=F32)


def _bmm_tn(a, b):
    return _bmm(jnp.swapaxes(a, 1, 2), b)


def _bmm_01(m01, x):
    hi = x.astype(BF16)
    lo = (x - hi.astype(F32)).astype(BF16)
    return (jnp.einsum('cik,ckj->cij', m01, hi, preferred_element_type=F32)
            + jnp.einsum('cik,ckj->cij', m01, lo, preferred_element_type=F32))


def _sigmoid(x):
    return 1.0 / (1.0 + jnp.exp(-x))


def _silu(x):
    return x * _sigmoid(x)


def _softplus(x):
    return jnp.maximum(x, 0.0) + jnp.log(1.0 + jnp.exp(-jnp.abs(x)))


def _iota2(shape, dim):
    return lax.broadcasted_iota(jnp.int32, shape, dim)


def _head_rms_gate(o, gate, nw, width):
    outs = []
    for h in range(o.shape[1] // width):
        oh = o[:, h * width:(h + 1) * width]
        ms = jnp.mean(oh * oh, axis=-1, keepdims=True)
        outs.append(oh * lax.rsqrt(ms + RMS_EPS) * nw)
    return jnp.concatenate(outs, axis=1) * _silu(gate)


def _norm_mm_kernel(*refs, tn, n_w, has_small, n_cast, pair_split):
    if pair_split is not None:
        refs, wq_s = refs[:-1], refs[-1]
        n_hd, hd_w = pair_split
        half = hd_w // 2

        @pl.when(pl.program_id(0) == 0)
        def _():
            src = _iota2((hd_w, hd_w), 0)
            dst = _iota2((hd_w, hd_w), 1)
            split = (src == jnp.where(dst < half, 2 * dst, 2 * (dst - half) + 1)).astype(BF16)
            for hd in range(n_hd):
                cols = slice(hd * hd_w, (hd + 1) * hd_w)
                wq_s[:, cols] = jnp.dot(refs[2][:, cols], split, preferred_element_type=F32).astype(BF16)

    x_ref, nw_ref = refs[:2]
    w_refs = refs[2:2 + n_w]
    rest = refs[2 + n_w:]
    if has_small:
        ws_ref, rest = rest[0], rest[1:]
    cast_in, rest = rest[:n_cast], rest[n_cast:]
    o_ref = rest[0]
    if has_small:
        os_ref = rest[1]
    cast_out = rest[len(rest) - n_cast:]
    for src, dst in zip(cast_in, cast_out):
        dst[...] = src[...].astype(dst.dtype)
    x = x_ref[...]
    ms = jnp.mean(x * x, axis=-1, keepdims=True)
    hb = (x * lax.rsqrt(ms + RMS_EPS) * nw_ref[...]).astype(BF16)
    if has_small:
        os_ref[...] = jnp.dot(hb, ws_ref[...], preferred_element_type=F32)
    c0 = 0
    for wi, w_ref in enumerate(w_refs):
        for j in range(w_ref.shape[1] // tn):
            cols = slice(j * tn, (j + 1) * tn)
            reordered = pair_split is not None and wi == 0 and (j + 1) * tn <= n_hd * hd_w
            w_tile = wq_s[:, cols] if reordered else w_ref[:, cols]
            res = jnp.dot(hb, w_tile, preferred_element_type=F32)
            o_ref[:, c0 + j * tn:c0 + (j + 1) * tn] = res.astype(o_ref.dtype)
        c0 += w_ref.shape[1]


def _norm_matmul(x, nw, ws_main, ws, *, tm, tn, name, cast=(), pair_split=None):
    n, d = x.shape
    steps = n // tm
    nout = sum(w.shape[1] for w in ws_main)
    has_small = ws is not None
    in_specs = [
        pl.BlockSpec((tm, d), lambda i: (i, 0)),
        pl.BlockSpec((1, d), lambda i: (0, 0)),
    ] + [pl.BlockSpec(w.shape, lambda i: (0, 0)) for w in ws_main]
    out_specs = [pl.BlockSpec((tm, nout), lambda i: (i, 0))]
    out_shape = [jax.ShapeDtypeStruct((n, nout), BF16)]
    args = [x, nw] + list(ws_main)
    if has_small:
        nsmall = ws.shape[1]
        in_specs.append(pl.BlockSpec((d, nsmall), lambda i: (0, 0)))
        out_specs.append(pl.BlockSpec((tm, nsmall), lambda i: (i, 0)))
        out_shape.append(jax.ShapeDtypeStruct((n, nsmall), F32))
        args.append(ws)
    for a in cast:
        slab = (a.shape[0] // steps, a.shape[1])
        in_specs.append(pl.BlockSpec(slab, lambda i: (i, 0)))
        out_specs.append(pl.BlockSpec(slab, lambda i: (i, 0)))
        out_shape.append(jax.ShapeDtypeStruct(a.shape, BF16))
        args.append(a)
    scratch = []
    if pair_split is not None:
        assert (pair_split[0] * pair_split[1]) % tn == 0
        scratch.append(pltpu.VMEM((d, pair_split[0] * pair_split[1]), BF16))
    return pl.pallas_call(
        functools.partial(_norm_mm_kernel, tn=tn, n_w=len(ws_main), has_small=has_small, n_cast=len(cast),
                          pair_split=pair_split),
        grid=(n // tm,),
        in_specs=in_specs,
        out_specs=out_specs,
        out_shape=out_shape,
        scratch_shapes=scratch,
        compiler_params=pltpu.CompilerParams(
            dimension_semantics=("arbitrary",), vmem_limit_bytes=VMEM_LIMIT),
        name=name,
    )(*args)


def _gdn_kernel(qkv_ref, z_ref, sm_ref, cw_ref, alog_ref, dtb_ref, nw_ref, o_ref,
                xs_ref, qkv_s, o_s, m_s, sq_s, qe_s, dec_s, st_ref, *, tb):
    @pl.when(pl.program_id(1) == 0)
    def _():
        xs_ref[0:8, :] = jnp.zeros((8, xs_ref.shape[1]), F32)
        st_ref[...] = jnp.zeros(st_ref.shape, F32)

    c = CHUNK
    nc = tb // c
    nb = N_HEADS * nc
    qk_w = N_HEADS * HEAD_D
    row = _iota2((c, c), 0)
    col = _iota2((c, c), 1)
    causal = (row >= col)[None]
    strict = (row > col)[None]
    same16 = ((row >> 4) == (col >> 4))[None]
    same32 = ((row >> 5) == (col >> 5))[None]
    eye = (row == col).astype(F32)[None]
    ltri = jnp.broadcast_to((row >= col).astype(BF16)[None], (nc, c, c))
    lane = _iota2((nb, c, LANES), 2)

    def per_head(fn):
        return jnp.concatenate([fn(h) for h in range(N_HEADS)], axis=0)

    for h in range(N_HEADS):
        for base, scale in ((0, HEAD_D ** -0.5), (qk_w, 1.0), (2 * qk_w, None)):
            cols = slice(base + h * HEAD_D, base + (h + 1) * HEAD_D)
            x = qkv_ref[0, :, cols].astype(F32)
            xs_ref[8:8 + tb, cols] = x
            cw = cw_ref[:, cols]
            y = x * cw[CONV_K - 1:CONV_K, :]
            for j in range(1, CONV_K):
                y = y + xs_ref[pl.ds(8 - j, tb), cols] * cw[CONV_K - 1 - j:CONV_K - j, :]
            y = _silu(y)
            if scale is not None:
                y = y * (lax.rsqrt(jnp.sum(y * y, axis=-1, keepdims=True) + L2_EPS) * scale)
            qkv_s[:, cols] = y
    xs_ref[0:8, :] = xs_ref[tb:tb + 8, :]

    sm = sm_ref[0]
    beta_all = _sigmoid(sm[:, :LANES]).reshape(nc, c, LANES)
    g_all = (-jnp.exp(alog_ref[...]) * _softplus(sm[:, LANES:] + dtb_ref[...])).reshape(nc, c, LANES)
    gc_all = _bmm_01(ltri, g_all)
    q = per_head(lambda h: qkv_s[:, h * HEAD_D:(h + 1) * HEAD_D].reshape(nc, c, HEAD_D))
    k = per_head(lambda h: qkv_s[:, qk_w + h * HEAD_D:qk_w + (h + 1) * HEAD_D].reshape(nc, c, HEAD_D))
    v = per_head(lambda h: qkv_s[:, 2 * qk_w + h * HEAD_D:2 * qk_w + (h + 1) * HEAD_D].reshape(nc, c, HEAD_D))
    beta = per_head(lambda h: beta_all[:, :, h:h + 1])
    gcol = per_head(lambda h: gc_all[:, :, h:h + 1])
    g_hi = gcol.astype(BF16).astype(F32)
    g_mid = (gcol - g_hi).astype(BF16).astype(F32)
    g_lo = gcol - g_hi - g_mid
    ones_hi = jnp.where(lane < 6, 1.0, 0.0)
    lhs = jnp.where(lane == 0, g_hi, jnp.where(lane == 1, g_mid, jnp.where(lane == 2, g_lo, ones_hi)))
    rhs_g = jnp.where(lane == 3, -g_hi, jnp.where(lane == 4, -g_mid, jnp.where(lane == 5, -g_lo, ones_hi)))
    decay = jnp.exp(jnp.where(causal, _bmm_nt(lhs, rhs_g), -jnp.inf))
    kb = k * beta
    a = jnp.where(strict, _bmm_nt(kb, k) * decay, 0.0)
    d = jnp.where(same16, a, 0.0)
    x_inv = eye - d
    dp = d
    for _ in range(3):
        dp = _bmm(dp, dp)
        x_inv = x_inv + _bmm(x_inv, dp)
    e = jnp.where(jnp.logical_and(same32, jnp.logical_not(same16)), a, 0.0)
    x_inv = x_inv - _bmm(x_inv, _bmm(e, x_inv))
    f = jnp.where(same32, 0.0, a)
    x_inv = x_inv - _bmm(x_inv, _bmm(f, x_inv))
    wu = _bmm(x_inv, jnp.concatenate([kb * jnp.exp(gcol), v * beta], axis=2))
    attn = _bmm_nt(q, k) * decay
    g_last = gcol[:, c - 1:c, :]
    kd = k * jnp.exp(g_last - gcol)
    mq = _bmm_tn(kd, wu)
    aw = _bmm(attn, wu)
    m_s[...] = mq[:, :, :HEAD_D].astype(BF16).reshape(N_HEADS, nc, HEAD_D, HEAD_D)
    sq_s[...] = mq[:, :, HEAD_D:].reshape(N_HEADS, nc, HEAD_D, HEAD_D)
    qe_s[...] = (q * jnp.exp(gcol) - aw[:, :, :HEAD_D]).astype(BF16).reshape(N_HEADS, nc, c, HEAD_D)
    dec_s[...] = jnp.broadcast_to(jnp.exp(g_last), (nb, 1, HEAD_D)).reshape(N_HEADS, nc, 1, HEAD_D)
    for h in range(N_HEADS):
        o_s[:, h * HEAD_D:(h + 1) * HEAD_D] = aw[h * nc:(h + 1) * nc, :, HEAD_D:].reshape(tb, HEAD_D)

    for ci in range(nc):
        for h in range(N_HEADS):
            hs = slice(h * HEAD_D, (h + 1) * HEAD_D)
            s_h = st_ref[h]
            s_b = s_h.astype(BF16)
            o_s[ci * c:(ci + 1) * c, hs] += jnp.dot(qe_s[h, ci], s_b, preferred_element_type=F32)
            st_ref[h] = (s_h * dec_s[h, ci] - jnp.dot(m_s[h, ci], s_b, preferred_element_type=F32)
                         + sq_s[h, ci])

    o_ref[0] = _head_rms_gate(o_s[...], z_ref[0].astype(F32), nw_ref[...], HEAD_D).astype(o_ref.dtype)


def _gdn(proj, small, conv_w, a_log, dt_bias, norm_w, *, tb):
    b, t, _ = proj.shape
    conv_cols = 3 * N_HEADS * HEAD_D
    mix_w = N_HEADS * HEAD_D
    return pl.pallas_call(
        functools.partial(_gdn_kernel, tb=tb),
        grid=(b, t // tb),
        in_specs=[
            pl.BlockSpec((1, tb, conv_cols), lambda i, j: (i, j, 0)),
            pl.BlockSpec((1, tb, mix_w), lambda i, j: (i, j, conv_cols // mix_w)),
            pl.BlockSpec((1, tb, 2 * LANES), lambda i, j: (i, j, 0)),
            pl.BlockSpec((CONV_K, conv_cols), lambda i, j: (0, 0)),
            pl.BlockSpec((1, LANES), lambda i, j: (0, 0)),
            pl.BlockSpec((1, LANES), lambda i, j: (0, 0)),
            pl.BlockSpec((1, HEAD_D), lambda i, j: (0, 0)),
        ],
        out_specs=pl.BlockSpec((1, tb, mix_w), lambda i, j: (i, j, 0)),
        out_shape=jax.ShapeDtypeStruct((b, t, mix_w), BF16),
        scratch_shapes=[
            pltpu.VMEM((tb + 8, conv_cols), F32),
            pltpu.VMEM((tb, conv_cols), F32),
            pltpu.VMEM((tb, mix_w), F32),
            pltpu.VMEM((N_HEADS, tb // CHUNK, HEAD_D, HEAD_D), BF16),
            pltpu.VMEM((N_HEADS, tb // CHUNK, HEAD_D, HEAD_D), F32),
            pltpu.VMEM((N_HEADS, tb // CHUNK, CHUNK, HEAD_D), BF16),
            pltpu.VMEM((N_HEADS, tb // CHUNK, 1, HEAD_D), F32),
            pltpu.VMEM((N_HEADS, HEAD_D, HEAD_D), F32),
        ],
        compiler_params=pltpu.CompilerParams(
            dimension_semantics=("parallel", "arbitrary"), vmem_limit_bytes=VMEM_LIMIT),
        name="gdn",
    )(proj, proj, small, conv_w, a_log, dt_bias, norm_w)


def _hgrn_kernel(q_ref, f_ref, i_ref, g_ref, lb_ref, nw_ref, o_ref,
                 q_s, k_s, lf_s, i_s, o_s, st_ref, *, tb, layer):
    @pl.when(pl.program_id(1) == 0)
    def _():
        st_ref[...] = jnp.zeros(st_ref.shape, F32)

    lbl = lb_ref[...]
    e_lb = jnp.exp(lbl - jnp.max(lbl, axis=0, keepdims=True))
    lb = jnp.sum(e_lb[:layer + 1], axis=0, keepdims=True) / jnp.sum(e_lb, axis=0, keepdims=True)

    f = lb + (1.0 - lb) * _sigmoid(f_ref[0].astype(F32))
    k_s[...] = 1.0 - f
    lf_s[...] = jnp.log(f)
    q_s[...] = _silu(q_ref[0].astype(F32)) * (HEAD_D ** -0.5)
    i_s[...] = i_ref[0].astype(F32)

    c = CHUNK
    nc = tb // c
    blk = 8
    row = _iota2((c, c), 0)
    col = _iota2((c, c), 1)
    ltri = jnp.broadcast_to((row >= col).astype(BF16)[None], (nc, c, c))
    level_masks = {}
    for m in (32, 16, 8):
        sh = int(math.log2(2 * m))
        level_masks[m] = jnp.logical_and(
            (row >> sh) == (col >> sh),
            jnp.logical_and((row & (2 * m - 1)) >= m, (col & (2 * m - 1)) < m))[None]
    nb = N_HEADS * nc
    sub = _iota2((nb * c // blk, blk, HEAD_D), 1)

    def per_head(fn):
        return jnp.concatenate([fn(slice(h * HEAD_D, (h + 1) * HEAD_D)) for h in range(N_HEADS)], axis=0)

    b_all = _bmm_01(ltri, lf_s[...].reshape(nc, c, N_HEADS * HEAD_D))
    b = per_head(lambda sl: b_all[:, :, sl])
    q = per_head(lambda sl: q_s[:, sl].reshape(nc, c, HEAD_D))
    k = per_head(lambda sl: k_s[:, sl].reshape(nc, c, HEAD_D))
    iv = per_head(lambda sl: i_s[:, sl].reshape(nc, c, HEAD_D))
    attn = jnp.zeros((nb, c, c), F32)
    for m in (32, 16, 8):
        b_m = b.reshape(nb * c // (2 * m), 2 * m, HEAD_D)
        ref = jnp.broadcast_to(b_m[:, m:m + 1, :], b_m.shape).reshape(nb, c, HEAD_D)
        e = jnp.exp(-jnp.abs(b - ref))
        attn = attn + jnp.where(level_masks[m], _bmm_nt(q * e, k * e), 0.0)
    o = _bmm(attn, iv)
    qb, kb, bb, ib = (a.reshape(nb * c // blk, blk, HEAD_D) for a in (q, k, b, iv))
    ob = jnp.zeros(qb.shape, F32)
    for s in range(blk):
        dec = jnp.exp(jnp.where(sub >= s, bb - bb[:, s:s + 1, :], -jnp.inf))
        a_col = jnp.sum(dec * qb * kb[:, s:s + 1, :], axis=-1, keepdims=True)
        ob = ob + a_col * ib[:, s:s + 1, :]
    o = o + ob.reshape(nb, c, HEAD_D)
    b_last = b[:, c - 1:c, :]
    q_in = (q * jnp.exp(b)).astype(BF16)
    upd = _bmm_tn(iv, k * jnp.exp(b_last - b))
    keep = jnp.exp(b_last)
    for h in range(N_HEADS):
        s_t = st_ref[h]
        outs = []
        for ci in range(h * nc, (h + 1) * nc):
            outs.append(o[ci] + _mm_nt(q_in[ci], s_t))
            s_t = s_t * keep[ci] + upd[ci]
        st_ref[h] = s_t
        o_s[:, h * HEAD_D:(h + 1) * HEAD_D] = jnp.concatenate(outs, axis=0)

    o_ref[0] = _head_rms_gate(o_s[...], g_ref[0].astype(F32), nw_ref[...], HEAD_D).astype(o_ref.dtype)


def _hgrn(proj, lb_logits, norm_w, *, tb, layer, col0):
    b, t, _ = proj.shape
    w = N_HEADS * HEAD_D
    specs = [pl.BlockSpec((1, tb, w), functools.partial(lambda i, j, off: (i, j, off), off=col0 + n))
             for n in range(4)]
    return pl.pallas_call(
        functools.partial(_hgrn_kernel, tb=tb, layer=layer),
        grid=(b, t // tb),
        in_specs=specs + [
            pl.BlockSpec(lb_logits.shape, lambda i, j: (0, 0)),
            pl.BlockSpec((1, HEAD_D), lambda i, j: (0, 0)),
        ],
        out_specs=pl.BlockSpec((1, tb, w), lambda i, j: (i, j, 0)),
        out_shape=jax.ShapeDtypeStruct((b, t, w), BF16),
        scratch_shapes=[pltpu.VMEM((tb, w), F32)] * 5 + [pltpu.VMEM((N_HEADS, HEAD_D, HEAD_D), F32)],
        compiler_params=pltpu.CompilerParams(
            dimension_semantics=("parallel", "arbitrary"), vmem_limit_bytes=VMEM_LIMIT),
        name="hgrn2",
    )(proj, proj, proj, proj, lb_logits, norm_w)


def _ret_kernel(q_ref, k_ref, v_ref, g_ref, cos_ref, sin_ref, nw_ref, o_ref, dmat_s, st_ref, *, tc):
    log_gammas = [math.log(1.0 - 2.0 ** (-5.0 - h)) for h in range(N_HEADS)]

    @pl.when(pl.program_id(1) == 0)
    def _():
        st_ref[...] = jnp.zeros(st_ref.shape, F32)
        rel = (_iota2((tc, tc), 0) - _iota2((tc, tc), 1)).astype(F32)
        for h in range(N_HEADS):
            dmat_s[h] = jnp.where(rel >= 0, jnp.exp(jnp.maximum(rel, 0.0) * log_gammas[h]), 0.0)

    pos = _iota2((tc, 1), 0).astype(F32)
    cos = cos_ref[...]
    sin = sin_ref[...]
    half = RET_DK // 2

    def rot(x):
        x1, x2 = x[:, :half], x[:, half:]
        return jnp.concatenate([x1 * cos - x2 * sin, x1 * sin + x2 * cos], axis=1)

    for h in range(N_HEADS):
        lg = log_gammas[h]
        qk_cols = slice(h * RET_DK, (h + 1) * RET_DK)
        v_cols = slice(h * RET_DV, (h + 1) * RET_DV)
        q = rot(q_ref[0, :, qk_cols].astype(F32))
        k = rot(k_ref[0, :, qk_cols].astype(F32)) * (RET_DK ** -0.5)
        v = v_ref[0, :, v_cols]
        s = st_ref[h]
        attn = _mm_nt(q, k) * dmat_s[h]
        o = _mm(q * jnp.exp((pos + 1.0) * lg), s) + _mm(attn, v)
        st_ref[h] = s * math.exp(tc * lg) + _mm_tn(k * jnp.exp((tc - 1.0 - pos) * lg), v)
        ms = jnp.mean(o * o, axis=-1, keepdims=True)
        o_ref[0, :, v_cols] = (o * lax.rsqrt(ms + RMS_EPS) * nw_ref[...]
                               * _silu(g_ref[0, :, v_cols].astype(F32))).astype(o_ref.dtype)


def _retention(proj, cos, sin, norm_w, *, tc):
    b, t, _ = proj.shape
    qk_w = N_HEADS * RET_DK
    v_w = N_HEADS * RET_DV
    return pl.pallas_call(
        functools.partial(_ret_kernel, tc=tc),
        grid=(b, t // tc),
        in_specs=[
            pl.BlockSpec((1, tc, qk_w), lambda i, j: (i, j, 0)),
            pl.BlockSpec((1, tc, qk_w), lambda i, j: (i, j, 1)),
            pl.BlockSpec((1, tc, v_w), lambda i, j: (i, j, 1)),
            pl.BlockSpec((1, tc, v_w), lambda i, j: (i, j, 2)),
            pl.BlockSpec((tc, RET_DK // 2), lambda i, j: (j, 0)),
            pl.BlockSpec((tc, RET_DK // 2), lambda i, j: (j, 0)),
            pl.BlockSpec((1, RET_DV), lambda i, j: (0, 0)),
        ],
        out_specs=pl.BlockSpec((1, tc, v_w), lambda i, j: (i, j, 0)),
        out_shape=jax.ShapeDtypeStruct((b, t, v_w), BF16),
        scratch_shapes=[pltpu.VMEM((N_HEADS, tc, tc), F32), pltpu.VMEM((N_HEADS, RET_DK, RET_DV), F32)],
        compiler_params=pltpu.CompilerParams(
            dimension_semantics=("parallel", "arbitrary"), vmem_limit_bytes=VMEM_LIMIT),
        name="retention",
    )(proj, proj, proj, proj, cos, sin, norm_w)


def _route_gates(logits_t):
    cl = [logits_t[g:g + 1, :] for g in range(N_GROUPS)]
    cmax = functools.reduce(jnp.maximum, cl)
    denom = sum(jnp.exp(x - cmax) for x in cl)
    g_prob = 1.0 / denom
    g_idx = jnp.full(cmax.shape, N_GROUPS - 1, jnp.int32)
    for g in range(N_GROUPS - 2, -1, -1):
        g_idx = jnp.where(cl[g] == cmax, g, g_idx)
    def fine_row(g, j):
        r = N_GROUPS + g * EXPERTS_PER_GROUP + j
        return logits_t[r:r + 1, :]

    fl = []
    for j in range(EXPERTS_PER_GROUP):
        x = fine_row(N_GROUPS - 1, j)
        for g in range(N_GROUPS - 2, -1, -1):
            x = jnp.where(g_idx == g, fine_row(g, j), x)
        fl.append(x)
    m1 = functools.reduce(jnp.maximum, fl)
    i1 = jnp.full(m1.shape, EXPERTS_PER_GROUP - 1, jnp.int32)
    for j in range(EXPERTS_PER_GROUP - 2, -1, -1):
        i1 = jnp.where(fl[j] == m1, j, i1)
    rest = [jnp.where(i1 == j, -jnp.inf, fl[j]) for j in range(EXPERTS_PER_GROUP)]
    m2 = functools.reduce(jnp.maximum, rest)
    i2 = jnp.full(m2.shape, EXPERTS_PER_GROUP - 1, jnp.int32)
    for j in range(EXPERTS_PER_GROUP - 2, -1, -1):
        i2 = jnp.where(jnp.logical_and(rest[j] == m2, i1 != j), j, i2)
    e2 = jnp.exp(m2 - m1)
    w1 = g_prob / (1.0 + e2)
    w2 = g_prob * e2 / (1.0 + e2)
    local = [jnp.where(i1 == j, w1, 0.0) + jnp.where(i2 == j, w2, 0.0) for j in range(EXPERTS_PER_GROUP)]
    return g_idx, local


def _outproj_kernel(a_ref, b_ref, wa_ref, wb_ref, x_ref, nw_ref, rw_ref, rb_ref,
                    x1_ref, h_ref, gk_ref, krow_ref, cnt_ref):
    x1 = (x_ref[...]
          + jnp.dot(a_ref[...], wa_ref[...], preferred_element_type=F32)
          + jnp.dot(b_ref[...], wb_ref[...], preferred_element_type=F32))
    x1_ref[...] = x1
    ms = jnp.mean(x1 * x1, axis=-1, keepdims=True)
    h = x1 * lax.rsqrt(ms + RMS_EPS) * nw_ref[...]
    h_ref[...] = h.astype(BF16)
    rw = rw_ref[...]
    hh = h.astype(BF16)
    hl = (h - hh.astype(F32)).astype(BF16)
    wh = rw.astype(BF16)
    wl = (rw - wh.astype(F32)).astype(BF16)
    hi_part = jnp.dot(hh, jnp.concatenate([wh, wl], axis=1), preferred_element_type=F32)
    logits = (hi_part[:, :LANES] + hi_part[:, LANES:]
              + jnp.dot(hl, wh, preferred_element_type=F32)) + rb_ref[...]
    g_idx, local = _route_gates(logits.T)
    tm = g_idx.shape[1]
    mem = (_iota2((8, tm), 0) == g_idx).astype(BF16)
    before = (_iota2((MOE_SUB, MOE_SUB), 0) < _iota2((MOE_SUB, MOE_SUB), 1)).astype(BF16)
    earlier = jnp.concatenate(
        [jnp.dot(mem[:, s:s + MOE_SUB], before, preferred_element_type=F32) for s in range(0, tm, MOE_SUB)],
        axis=1)
    rank = jnp.sum(mem.astype(F32) * earlier, axis=0, keepdims=True)
    key = g_idx.astype(F32) * MOE_KEY_STRIDE + rank
    krow_ref[...] = key
    sub_shift = int(math.log2(MOE_SUB))
    sub_sel = ((_iota2((tm, LANES), 0) >> sub_shift) == _iota2((tm, LANES), 1)).astype(BF16)
    cnt_ref[...] = jnp.dot(mem, sub_sel, preferred_element_type=F32)
    rows = jnp.concatenate([key] + local + [jnp.zeros((LANES - 1 - EXPERTS_PER_GROUP, tm), F32)], axis=0)
    gk_ref[...] = rows.T


def _outproj_route(a, b, a_blk, b_blk, wa, wb, x, nw, rw, rb, *, tm, name):
    n, d = x.shape
    ka = wa.shape[0]
    kb = wb.shape[0]
    return pl.pallas_call(
        _outproj_kernel,
        grid=(n // tm,),
        in_specs=[
            pl.BlockSpec((tm, ka), lambda i: (i, a_blk)),
            pl.BlockSpec((tm, kb), lambda i: (i, b_blk)),
            pl.BlockSpec((ka, d), lambda i: (0, 0)),
            pl.BlockSpec((kb, d), lambda i: (0, 0)),
            pl.BlockSpec((tm, d), lambda i: (i, 0)),
            pl.BlockSpec((1, d), lambda i: (0, 0)),
            pl.BlockSpec((d, LANES), lambda i: (0, 0)),
            pl.BlockSpec((1, LANES), lambda i: (0, 0)),
        ],
        out_specs=[
            pl.BlockSpec((tm, d), lambda i: (i, 0)),
            pl.BlockSpec((tm, d), lambda i: (i, 0)),
            pl.BlockSpec((tm, LANES), lambda i: (i, 0)),
            pl.BlockSpec((1, tm), lambda i: (0, i)),
            pl.BlockSpec((8, LANES), lambda i: (i, 0)),
        ],
        out_shape=[jax.ShapeDtypeStruct((n, d), F32),
                   jax.ShapeDtypeStruct((n, d), BF16),
                   jax.ShapeDtypeStruct((n, LANES), F32),
                   jax.ShapeDtypeStruct((1, n), F32),
                   jax.ShapeDtypeStruct((8 * (n // tm), LANES), F32)],
        compiler_params=pltpu.CompilerParams(
            dimension_semantics=("parallel",), vmem_limit_bytes=VMEM_LIMIT),
        name=name,
    )(a, b, wa, wb, x, nw, rw, rb)


def _round_up(v, m):
    return ((v + (m - 1)) // m) * m


def _moe_kernel(cnt_ref, h_ref, krow_ref, gk_ref, wgu_ref, wd_ref, x_ref, fnw_ref, o_ref,
                hc_s, gc_s, y_s, *, final_norm):
    i = pl.program_id(0)
    g = pl.program_id(1)
    tm = h_ref.shape[0]
    n_sub = tm // MOE_SUB

    def segments(grp):
        cnts = [cnt_ref[(i * n_sub + s) * N_GROUPS + grp] for s in range(n_sub)]
        starts = [jnp.int32(0)]
        for s in range(n_sub):
            starts.append(starts[-1] + _round_up(cnts[s], MOE_ALIGN))
        return cnts, starts

    counts, offs = segments(g)
    total = offs[-1]
    key0 = g.astype(F32) * MOE_KEY_STRIDE
    sub_iota = _iota2((MOE_BLK, MOE_SUB), 0).astype(F32)
    lane_iota = _iota2((MOE_SUB, MOE_BLK), 1).astype(F32)

    def pack(s, w):
        rows = slice(s * MOE_SUB, (s + 1) * MOE_SUB)
        base = key0 + (w * MOE_BLK).astype(F32)
        sel = (krow_ref[:, rows] == sub_iota + base).astype(BF16)
        dst = pl.ds(pl.multiple_of(offs[s] + w * MOE_BLK, MOE_ALIGN), MOE_BLK)
        hc_s[dst, :] = jnp.dot(sel, h_ref[rows, :], preferred_element_type=F32).astype(BF16)
        gc_s[dst, :] = _mm_01(sel, gk_ref[rows, :])

    for s in range(n_sub):
        def pack_more(w, carry, s=s):
            pack(s, w)
            return carry

        lax.fori_loop(1, (counts[s] + MOE_BLK - 1) // MOE_BLK, pack_more, 0)
    for s in range(n_sub):
        pack(s, jnp.int32(0))

    tail_rows = MOE_FIRST[-1] + MOE_FIRST_STEP
    tail = pl.ds(pl.multiple_of(total, MOE_ALIGN), tail_rows)
    hc_s[tail, :] = jnp.zeros((tail_rows, hc_s.shape[1]), BF16)
    gc_s[tail, :] = jnp.zeros((tail_rows, LANES), F32)

    def expert_rows(start, rows):
        blk = pl.ds(pl.multiple_of(start, MOE_FIRST_STEP), rows)
        hb = hc_s[blk, :]
        gates = gc_s[blk, :]
        y = jnp.zeros((rows, o_ref.shape[1]), F32)
        for e in range(EXPERTS_PER_GROUP):
            gu = jnp.dot(hb, wgu_ref[0, e], preferred_element_type=F32)
            act = _silu(gu[:, :D_EXPERT]) * gu[:, D_EXPERT:] * gates[:, 1 + e:2 + e]
            y = y + jnp.dot(act.astype(BF16), wd_ref[0, e], preferred_element_type=F32)
        y_s[g, blk, :] = y.astype(BF16)

    first = jnp.clip(_round_up(total, MOE_FIRST_STEP), MOE_FIRST[0], MOE_FIRST[-1])
    for size in MOE_FIRST:
        @pl.when(first == size)
        def _(size=size):
            expert_rows(0, size)

    n_rest = (jnp.maximum(total - first, 0) + MOE_BLK - 1) // MOE_BLK

    def rest(bi, carry):
        expert_rows(first + bi * MOE_BLK, MOE_BLK)
        return carry

    lax.fori_loop(0, n_rest, rest, 0)
    done = pl.ds(pl.multiple_of(first + n_rest * MOE_BLK, MOE_FIRST_STEP), MOE_BLK)
    y_s[g, done, :] = jnp.zeros((MOE_BLK, y_s.shape[2]), BF16)

    @pl.when(g == N_GROUPS - 1)
    def _():
        segs = [segments(grp) for grp in range(N_GROUPS)]
        for s in range(n_sub):
            rows = slice(s * MOE_SUB, (s + 1) * MOE_SUB)
            key_col = gk_ref[rows, 0:1]
            sel = jnp.concatenate(
                [(key_col == lane_iota + grp * MOE_KEY_STRIDE).astype(BF16) for grp in range(N_GROUPS)], axis=1)
            packed = jnp.concatenate(
                [y_s[grp, pl.ds(pl.multiple_of(segs[grp][1][s], MOE_ALIGN), MOE_BLK), :]
                 for grp in range(N_GROUPS)], axis=0)
            o_ref[rows, :] = x_ref[rows, :] + jnp.dot(sel, packed, preferred_element_type=F32)
            for grp in range(N_GROUPS):
                def more(w, carry, s=s, rows=rows, grp=grp, key_col=key_col):
                    base = grp * MOE_KEY_STRIDE + (w * MOE_BLK).astype(F32)
                    sel_w = (key_col == lane_iota + base).astype(BF16)
                    src = pl.ds(pl.multiple_of(segs[grp][1][s] + w * MOE_BLK, MOE_ALIGN), MOE_BLK)
                    o_ref[rows, :] += jnp.dot(sel_w, y_s[grp, src, :], preferred_element_type=F32)
                    return carry

                lax.fori_loop(1, (segs[grp][0][s] + MOE_BLK - 1) // MOE_BLK, more, 0)
        if final_norm:
            y = o_ref[...]
            ms = jnp.mean(y * y, axis=-1, keepdims=True)
            o_ref[...] = y * lax.rsqrt(ms + RMS_EPS) * fnw_ref[...]


def _moe(h, krow, gk, counts, wgu, wd, x, fnw, *, layer, tm, final_norm, name):
    n, d = x.shape
    buf_rows = _round_up(tm + (tm // MOE_SUB) * MOE_ALIGN + MOE_FIRST[-1] + MOE_FIRST_STEP, MOE_BLK)
    grid_spec = pltpu.PrefetchScalarGridSpec(
        num_scalar_prefetch=1,
        grid=(n // tm, N_GROUPS),
        in_specs=[
            pl.BlockSpec((tm, d), lambda i, g, c: (i, 0)),
            pl.BlockSpec((1, tm), lambda i, g, c: (0, i)),
            pl.BlockSpec((tm, LANES), lambda i, g, c: (i, 0)),
            pl.BlockSpec((1, EXPERTS_PER_GROUP, d, 2 * D_EXPERT), lambda i, g, c: (layer, g, 0, 0)),
            pl.BlockSpec((1, EXPERTS_PER_GROUP, D_EXPERT, d), lambda i, g, c: (layer, g, 0, 0)),
            pl.BlockSpec((tm, d), lambda i, g, c: (i, 0)),
            pl.BlockSpec((1, d), lambda i, g, c: (0, 0)),
        ],
        out_specs=pl.BlockSpec((tm, d), lambda i, g, c: (i, 0)),
        scratch_shapes=[pltpu.VMEM((buf_rows, d), BF16),
                        pltpu.VMEM((buf_rows, LANES), F32),
                        pltpu.VMEM((N_GROUPS, buf_rows, d), BF16)],
    )
    return pl.pallas_call(
        functools.partial(_moe_kernel, final_norm=final_norm),
        grid_spec=grid_spec,
        out_shape=jax.ShapeDtypeStruct((n, d), F32),
        compiler_params=pltpu.CompilerParams(
            dimension_semantics=("parallel", "arbitrary"), vmem_limit_bytes=VMEM_LIMIT),
        name=name,
    )(counts, h, krow, gk, wgu, wd, x, fnw)


def _pad_cols(a, width):
    return jnp.pad(a, ((0, 0), (0, width - a.shape[1])))


def _dispatch_counts(cnt, route_tm):
    c = cnt.reshape(-1, 8, LANES)[:, :N_GROUPS, :route_tm // MOE_SUB]
    return jnp.transpose(c, (0, 2, 1)).reshape(-1).astype(jnp.int32)


def _router_params(wc, bc, wf, bf):
    rw = _pad_cols(jnp.concatenate([wc, wf], axis=1), LANES)
    rb = _pad_cols(jnp.concatenate([bc, bf])[None, :], LANES)
    return rw, rb


def kernel(x, norm_mix_w, norm_ffn_w, even_w_in, gdn_conv_w, gdn_a_log, gdn_dt_bias, gdn_norm_w, hgrn_lb_logits, hgrn_norm_w, even_w_out, odd_w_in, ret_norm_w, odd_w_out, router_c_w, router_c_b, router_f_w, router_f_b, moe_w_gate_up, moe_w_down, final_norm_w):
    bsz, seq, d = x.shape
    n = bsz * seq
    xt = x.reshape(n, d)
    mix_w = N_HEADS * HEAD_D
    conv_cols = 3 * mix_w
    gdn_main = conv_cols + mix_w

    w_in = even_w_in[0]
    small0 = gdn_main
    w_main = [w_in[:, :small0].astype(BF16), w_in[:, small0 + 2 * N_HEADS:].astype(BF16)]
    w_small = jnp.concatenate(
        [_pad_cols(w_in[:, small0:small0 + N_HEADS], LANES),
         _pad_cols(w_in[:, small0 + N_HEADS:small0 + 2 * N_HEADS], LANES)], axis=1).astype(BF16)
    proj, small, wgu_b, wd_b, w_out, w_odd, w_out1 = _norm_matmul(
        xt, norm_mix_w[0][None, :], w_main, w_small, tm=512, tn=1024, name="in_proj_even",
        cast=(moe_w_gate_up.reshape(-1, moe_w_gate_up.shape[-1]), moe_w_down.reshape(-1, moe_w_down.shape[-1]),
              even_w_out[0], odd_w_in[0], odd_w_out[0]))
    wgu_b = wgu_b.reshape(moe_w_gate_up.shape)
    wd_b = wd_b.reshape(moe_w_down.shape)
    proj = proj.reshape(bsz, seq, -1)
    small = small.reshape(bsz, seq, -1)
    o_a = _gdn(proj, small, gdn_conv_w[0], _pad_cols(gdn_a_log[0][None, :], LANES),
               _pad_cols(gdn_dt_bias[0][None, :], LANES), gdn_norm_w[0][None, :], tb=512)
    o_b = _hgrn(proj, hgrn_lb_logits, hgrn_norm_w[0][None, :], tb=512, layer=0, col0=gdn_main // mix_w)
    rw, rb = _router_params(router_c_w[0], router_c_b[0], router_f_w[0], router_f_b[0])
    x1, h, gk, krow, cnt = _outproj_route(o_a.reshape(n, mix_w), o_b.reshape(n, mix_w), 0, 0,
                                          w_out[:mix_w], w_out[mix_w:], xt, norm_ffn_w[0][None, :], rw, rb,
                                          tm=ROUTE_TM, name="out_proj_even")
    x2 = _moe(h, krow, gk, _dispatch_counts(cnt, ROUTE_TM), wgu_b, wd_b, x1,
              final_norm_w[None, :], layer=0, tm=MOE_TM, final_norm=False, name="moe0")

    (proj1,) = _norm_matmul(x2, norm_mix_w[1][None, :], [w_odd], None, tm=512, tn=1024, name="in_proj_odd",
                            pair_split=(2 * N_HEADS, RET_DK))
    inv = 1.0 / (ROPE_BASE ** jnp.linspace(0.0, 1.0, RET_DK // 2, dtype=F32))
    ang = jnp.arange(seq, dtype=F32)[:, None] * inv[None, :]
    o_c = _retention(proj1.reshape(bsz, seq, -1), jnp.cos(ang), jnp.sin(ang), ret_norm_w[0][None, :],
                     tc=RET_CHUNK)
    o_c = o_c.reshape(n, -1)
    half = w_out1.shape[0] // 2
    rw, rb = _router_params(router_c_w[1], router_c_b[1], router_f_w[1], router_f_b[1])
    x3, h, gk, krow, cnt = _outproj_route(o_c, o_c, 0, 1, w_out1[:half], w_out1[half:], x2,
                                          norm_ffn_w[1][None, :], rw, rb, tm=ROUTE_TM, name="out_proj_odd")
    out = _moe(h, krow, gk, _dispatch_counts(cnt, ROUTE_TM), wgu_b, wd_b, x3,
               final_norm_w[None, :], layer=1, tm=MOE_TM, final_norm=True, name="moe1")
    return out.reshape(bsz, seq, d)
```

```python
import functools
import math

import jax
import jax.numpy as jnp
from jax import lax
from jax.experimental import pallas as pl
from jax.experimental.pallas import tpu as pltpu

F32 = jnp.float32
BF16 = jnp.bfloat16

D_MODEL = 1024
RMS_EPS = 1e-6
L2_EPS = 1e-6
CHUNK = 64
CONV_K = 4
N_HEADS = 4
HEAD_D = 128
RET_DK = 256
RET_DV = 512
RET_CHUNK = 256
ROPE_BASE = 10000.0
N_GROUPS = 4
EXPERTS_PER_GROUP = 4
N_EXPERTS = 16
D_EXPERT = 256
LANES = 128
GDN_PRE_ROWS = 128
ROUTE_TM = 1024
MOE_TM = 1024
MOE_SUB = 256
MOE_BLK = 128
MOE_ALIGN = 16
MOE_KEY_STRIDE = 4096.0
MOE_FIRST_STEP = 32
MOE_FIRST = (256, 288, 320, 352)
VMEM_LIMIT = 56 * 1024 * 1024


def _mm(a, b):
    return jnp.dot(a.astype(BF16), b.astype(BF16), preferred_element_type=F32)


def _mm_nt(a, b):
    return lax.dot_general(a.astype(BF16), b.astype(BF16), (((1,), (1,)), ((), ())),
                           preferred_element_type=F32)


def _mm_tn(a, b):
    return _mm(a.T, b)


def _mm_01(m01, x):
    hi = x.astype(BF16)
    lo = (x - hi.astype(F32)).astype(BF16)
    return (jnp.dot(m01, hi, preferred_element_type=F32)
            + jnp.dot(m01, lo, preferred_element_type=F32))


def _bmm(a, b):
    return jnp.einsum('cik,ckj->cij', a.astype(BF16), b.astype(BF16), preferred_element_type=F32)


def _bmm_nt(a, b):
    return jnp.einsum('cik,cjk->cij', a.astype(BF16), b.astype(BF16), preferred_element_type=F32)


def _bmm_tn(a, b):
    return _bmm(jnp.swapaxes(a, 1, 2), b)


def _bmm_01(m01, x):
    hi = x.astype(BF16)
    lo = (x - hi.astype(F32)).astype(BF16)
    return (jnp.einsum('cik,ckj->cij', m01, hi, preferred_element_type=F32)
            + jnp.einsum('cik,ckj->cij', m01, lo, preferred_element_type=F32))


def _sigmoid(x):
    return 1.0 / (1.0 + jnp.exp(-x))


def _silu(x):
    return x * _sigmoid(x)


def _softplus(x):
    return jnp.maximum(x, 0.0) + jnp.log(1.0 + jnp.exp(-jnp.abs(x)))


def _iota2(shape, dim):
    return lax.broadcasted_iota(jnp.int32, shape, dim)


def _head_rms_gate(o, gate, nw, width):
    outs = []
    for h in range(o.shape[1] // width):
        oh = o[:, h * width:(h + 1) * width]
        ms = jnp.mean(oh * oh, axis=-1, keepdims=True)
        outs.append(oh * lax.rsqrt(ms + RMS_EPS) * nw)
    return jnp.concatenate(outs, axis=1) * _silu(gate)


def _norm_mm_kernel(*refs, tn, n_w, has_small, n_cast, pair_split, gdn_pre):
    if gdn_pre is not None:
        refs, xs_ref = refs[:-1], refs[-1]
    if pair_split is not None:
        refs, wq_s = refs[:-1], refs[-1]
        n_hd, hd_w = pair_split
        half = hd_w // 2

        @pl.when(pl.program_id(0) == 0)
        def _():
            src = _iota2((hd_w, hd_w), 0)
            dst = _iota2((hd_w, hd_w), 1)
            split = (src == jnp.where(dst < half, 2 * dst, 2 * (dst - half) + 1)).astype(BF16)
            for hd in range(n_hd):
                cols = slice(hd * hd_w, (hd + 1) * hd_w)
                wq_s[:, cols] = jnp.dot(refs[2][:, cols], split, preferred_element_type=F32).astype(BF16)

    x_ref, nw_ref = refs[:2]
    w_refs = refs[2:2 + n_w]
    rest = refs[2 + n_w:]
    if has_small:
        ws_ref, rest = rest[0], rest[1:]
    if gdn_pre is not None:
        cw_ref, rest = rest[0], rest[1:]
    cast_in, rest = rest[:n_cast], rest[n_cast:]
    o_ref = rest[0]
    if has_small:
        os_ref = rest[1]
    cast_out = rest[len(rest) - n_cast:]
    for src, dst in zip(cast_in, cast_out):
        dst[...] = src[...].astype(dst.dtype)
    x = x_ref[...]
    tm = x.shape[0]
    ms = jnp.mean(x * x, axis=-1, keepdims=True)
    hb = (x * lax.rsqrt(ms + RMS_EPS) * nw_ref[...]).astype(BF16)
    if has_small:
        os_ref[...] = jnp.dot(hb, ws_ref[...], preferred_element_type=F32)

    if gdn_pre is not None:
        conv_cols, qk_cols, q_cols, q_scale, tiles_per_seq = gdn_pre

        @pl.when(pl.program_id(0) % tiles_per_seq == 0)
        def _():
            xs_ref[0:8, :] = jnp.zeros((8, xs_ref.shape[1]), F32)

        def gdn_prepare(res, cols):
            xs_ref[8:8 + tm, cols] = res
            for g0 in range(cols.start, cols.stop, HEAD_D):
                gcols = slice(g0, g0 + HEAD_D)
                cw = cw_ref[:, gcols]
                for r in range(0, tm, GDN_PRE_ROWS):
                    y = xs_ref[8 + r:8 + r + GDN_PRE_ROWS, gcols] * cw[CONV_K - 1:CONV_K, :]
                    for j in range(1, CONV_K):
                        y = y + (xs_ref[pl.ds(8 + r - j, GDN_PRE_ROWS), gcols]
                                 * cw[CONV_K - 1 - j:CONV_K - j, :])
                    y = _silu(y)
                    if g0 < qk_cols:
                        scale = q_scale if g0 < q_cols else 1.0
                        y = y * (lax.rsqrt(jnp.sum(y * y, axis=-1, keepdims=True) + L2_EPS) * scale)
                    o_ref[r:r + GDN_PRE_ROWS, gcols] = y.astype(o_ref.dtype)
            xs_ref[0:8, cols] = xs_ref[tm:tm + 8, cols]

    def finish(res, out_cols):
        if gdn_pre is not None and out_cols.stop <= conv_cols:
            gdn_prepare(res, out_cols)
        else:
            o_ref[:, out_cols] = res.astype(o_ref.dtype)

    pending = None
    c0 = 0
    for wi, w_ref in enumerate(w_refs):
        for j in range(w_ref.shape[1] // tn):
            cols = slice(j * tn, (j + 1) * tn)
            reordered = pair_split is not None and wi == 0 and (j + 1) * tn <= n_hd * hd_w
            w_tile = wq_s[:, cols] if reordered else w_ref[:, cols]
            res = jnp.dot(hb, w_tile, preferred_element_type=F32)
            if pending is not None:
                finish(*pending)
            pending = (res, slice(c0 + j * tn, c0 + (j + 1) * tn))
        c0 += w_ref.shape[1]
    finish(*pending)


def _norm_matmul(x, nw, ws_main, ws, *, tm, tn, name, cast=(), pair_split=None, gdn_conv=None):
    n, d = x.shape
    steps = n // tm
    nout = sum(w.shape[1] for w in ws_main)
    has_small = ws is not None
    in_specs = [
        pl.BlockSpec((tm, d), lambda i: (i, 0)),
        pl.BlockSpec((1, d), lambda i: (0, 0)),
    ] + [pl.BlockSpec(w.shape, lambda i: (0, 0)) for w in ws_main]
    out_specs = [pl.BlockSpec((tm, nout), lambda i: (i, 0))]
    out_shape = [jax.ShapeDtypeStruct((n, nout), BF16)]
    args = [x, nw] + list(ws_main)
    if has_small:
        nsmall = ws.shape[1]
        in_specs.append(pl.BlockSpec((d, nsmall), lambda i: (0, 0)))
        out_specs.append(pl.BlockSpec((tm, nsmall), lambda i: (i, 0)))
        out_shape.append(jax.ShapeDtypeStruct((n, nsmall), F32))
        args.append(ws)
    gdn_pre = None
    if gdn_conv is not None:
        conv_w, tokens_per_seq = gdn_conv
        conv_cols = conv_w.shape[1]
        assert conv_cols % tn == 0 and tokens_per_seq % tm == 0
        in_specs.append(pl.BlockSpec(conv_w.shape, lambda i: (0, 0)))
        args.append(conv_w)
        head_cols = conv_cols // 3
        gdn_pre = (conv_cols, 2 * head_cols, head_cols, HEAD_D ** -0.5, tokens_per_seq // tm)
    for a in cast:
        slab = (a.shape[0] // steps, a.shape[1])
        in_specs.append(pl.BlockSpec(slab, lambda i: (i, 0)))
        out_specs.append(pl.BlockSpec(slab, lambda i: (i, 0)))
        out_shape.append(jax.ShapeDtypeStruct(a.shape, BF16))
        args.append(a)
    scratch = []
    if pair_split is not None:
        assert (pair_split[0] * pair_split[1]) % tn == 0
        scratch.append(pltpu.VMEM((d, pair_split[0] * pair_split[1]), BF16))
    if gdn_pre is not None:
        scratch.append(pltpu.VMEM((tm + 8, gdn_pre[0]), F32))
    return pl.pallas_call(
        functools.partial(_norm_mm_kernel, tn=tn, n_w=len(ws_main), has_small=has_small, n_cast=len(cast),
                          pair_split=pair_split, gdn_pre=gdn_pre),
        grid=(n // tm,),
        in_specs=in_specs,
        out_specs=out_specs,
        out_shape=out_shape,
        scratch_shapes=scratch,
        compiler_params=pltpu.CompilerParams(
            dimension_semantics=("arbitrary",), vmem_limit_bytes=VMEM_LIMIT),
        name=name,
    )(*args)


def _gdn_kernel(qkv_ref, z_ref, sm_ref, alog_ref, dtb_ref, nw_ref, o_ref,
                o_s, m_s, sq_s, qe_s, dec_s, st_ref, *, tb):
    @pl.when(pl.program_id(1) == 0)
    def _():
        st_ref[...] = jnp.zeros(st_ref.shape, F32)

    c = CHUNK
    nc = tb // c
    nb = N_HEADS * nc
    qk_w = N_HEADS * HEAD_D
    row = _iota2((c, c), 0)
    col = _iota2((c, c), 1)
    causal = (row >= col)[None]
    strict = (row > col)[None]
    same16 = ((row >> 4) == (col >> 4))[None]
    same32 = ((row >> 5) == (col >> 5))[None]
    eye = (row == col).astype(F32)[None]
    ltri = jnp.broadcast_to((row >= col).astype(BF16)[None], (nc, c, c))
    lane = _iota2((nb, c, LANES), 2)

    def per_head(fn):
        return jnp.concatenate([fn(h) for h in range(N_HEADS)], axis=0)

    sm = sm_ref[0]
    beta_all = _sigmoid(sm[:, :LANES]).reshape(nc, c, LANES)
    g_all = (-jnp.exp(alog_ref[...]) * _softplus(sm[:, LANES:] + dtb_ref[...])).reshape(nc, c, LANES)
    gc_all = _bmm_01(ltri, g_all)
    def head_cols(base, h):
        return qkv_ref[0, :, base + h * HEAD_D:base + (h + 1) * HEAD_D].astype(F32).reshape(nc, c, HEAD_D)

    q = per_head(functools.partial(head_cols, 0))
    k = per_head(functools.partial(head_cols, qk_w))
    v = per_head(functools.partial(head_cols, 2 * qk_w))
    beta = per_head(lambda h: beta_all[:, :, h:h + 1])
    gcol = per_head(lambda h: gc_all[:, :, h:h + 1])
    g_hi = gcol.astype(BF16).astype(F32)
    g_mid = (gcol - g_hi).astype(BF16).astype(F32)
    g_lo = gcol - g_hi - g_mid
    ones_hi = jnp.where(lane < 6, 1.0, 0.0)
    lhs = jnp.where(lane == 0, g_hi, jnp.where(lane == 1, g_mid, jnp.where(lane == 2, g_lo, ones_hi)))
    rhs_g = jnp.where(lane == 3, -g_hi, jnp.where(lane == 4, -g_mid, jnp.where(lane == 5, -g_lo, ones_hi)))
    decay = jnp.exp(jnp.where(causal, _bmm_nt(lhs, rhs_g), -jnp.inf))
    kb = k * beta
    a = jnp.where(strict, _bmm_nt(kb, k) * decay, 0.0)
    d = jnp.where(same16, a, 0.0)
    x_inv = eye - d
    dp = d
    for _ in range(3):
        dp = _bmm(dp, dp)
        x_inv = x_inv + _bmm(x_inv, dp)
    e = jnp.where(jnp.logical_and(same32, jnp.logical_not(same16)), a, 0.0)
    x_inv = x_inv - _bmm(x_inv, _bmm(e, x_inv))
    f = jnp.where(same32, 0.0, a)
    x_inv = x_inv - _bmm(x_inv, _bmm(f, x_inv))
    wu = _bmm(x_inv, jnp.concatenate([kb * jnp.exp(gcol), v * beta], axis=2))
    attn = _bmm_nt(q, k) * decay
    g_last = gcol[:, c - 1:c, :]
    kd = k * jnp.exp(g_last - gcol)
    mq = _bmm_tn(kd, wu)
    aw = _bmm(attn, wu)
    m_s[...] = mq[:, :, :HEAD_D].astype(BF16).reshape(N_HEADS, nc, HEAD_D, HEAD_D)
    sq_s[...] = mq[:, :, HEAD_D:].reshape(N_HEADS, nc, HEAD_D, HEAD_D)
    qe_s[...] = (q * jnp.exp(gcol) - aw[:, :, :HEAD_D]).astype(BF16).reshape(N_HEADS, nc, c, HEAD_D)
    dec_s[...] = jnp.broadcast_to(jnp.exp(g_last), (nb, 1, HEAD_D)).reshape(N_HEADS, nc, 1, HEAD_D)
    for h in range(N_HEADS):
        o_s[:, h * HEAD_D:(h + 1) * HEAD_D] = aw[h * nc:(h + 1) * nc, :, HEAD_D:].reshape(tb, HEAD_D)

    for ci in range(nc):
        for h in range(N_HEADS):
            hs = slice(h * HEAD_D, (h + 1) * HEAD_D)
            s_h = st_ref[h]
            s_b = s_h.astype(BF16)
            o_s[ci * c:(ci + 1) * c, hs] += jnp.dot(qe_s[h, ci], s_b, preferred_element_type=F32)
            st_ref[h] = (s_h * dec_s[h, ci] - jnp.dot(m_s[h, ci], s_b, preferred_element_type=F32)
                         + sq_s[h, ci])

    o_ref[0] = _head_rms_gate(o_s[...], z_ref[0].astype(F32), nw_ref[...], HEAD_D).astype(o_ref.dtype)


def _gdn(proj, small, a_log, dt_bias, norm_w, *, tb):
    b, t, _ = proj.shape
    conv_cols = 3 * N_HEADS * HEAD_D
    mix_w = N_HEADS * HEAD_D
    return pl.pallas_call(
        functools.partial(_gdn_kernel, tb=tb),
        grid=(b, t // tb),
        in_specs=[
            pl.BlockSpec((1, tb, conv_cols), lambda i, j: (i, j, 0)),
            pl.BlockSpec((1, tb, mix_w), lambda i, j: (i, j, conv_cols // mix_w)),
            pl.BlockSpec((1, tb, 2 * LANES), lambda i, j: (i, j, 0)),
            pl.BlockSpec((1, LANES), lambda i, j: (0, 0)),
            pl.BlockSpec((1, LANES), lambda i, j: (0, 0)),
            pl.BlockSpec((1, HEAD_D), lambda i, j: (0, 0)),
        ],
        out_specs=pl.BlockSpec((1, tb, mix_w), lambda i, j: (i, j, 0)),
        out_shape=jax.ShapeDtypeStruct((b, t, mix_w), BF16),
        scratch_shapes=[
            pltpu.VMEM((tb, mix_w), F32),
            pltpu.VMEM((N_HEADS, tb // CHUNK, HEAD_D, HEAD_D), BF16),
            pltpu.VMEM((N_HEADS, tb // CHUNK, HEAD_D, HEAD_D), F32),
            pltpu.VMEM((N_HEADS, tb // CHUNK, CHUNK, HEAD_D), BF16),
            pltpu.VMEM((N_HEADS, tb // CHUNK, 1, HEAD_D), F32),
            pltpu.VMEM((N_HEADS, HEAD_D, HEAD_D), F32),
        ],
        compiler_params=pltpu.CompilerParams(
            dimension_semantics=("parallel", "arbitrary"), vmem_limit_bytes=VMEM_LIMIT),
        name="gdn",
    )(proj, proj, small, a_log, dt_bias, norm_w)


def _hgrn_kernel(q_ref, f_ref, i_ref, g_ref, lb_ref, nw_ref, o_ref,
                 q_s, k_s, lf_s, i_s, o_s, st_ref, *, tb, layer):
    @pl.when(pl.program_id(1) == 0)
    def _():
        st_ref[...] = jnp.zeros(st_ref.shape, F32)

    lbl = lb_ref[...]
    e_lb = jnp.exp(lbl - jnp.max(lbl, axis=0, keepdims=True))
    lb = jnp.sum(e_lb[:layer + 1], axis=0, keepdims=True) / jnp.sum(e_lb, axis=0, keepdims=True)

    f = lb + (1.0 - lb) * _sigmoid(f_ref[0].astype(F32))
    k_s[...] = 1.0 - f
    lf_s[...] = jnp.log(f)
    q_s[...] = _silu(q_ref[0].astype(F32)) * (HEAD_D ** -0.5)
    i_s[...] = i_ref[0].astype(F32)

    c = CHUNK
    nc = tb // c
    blk = 8
    row = _iota2((c, c), 0)
    col = _iota2((c, c), 1)
    ltri = jnp.broadcast_to((row >= col).astype(BF16)[None], (nc, c, c))
    level_masks = {}
    for m in (32, 16, 8):
        sh = int(math.log2(2 * m))
        level_masks[m] = jnp.logical_and(
            (row >> sh) == (col >> sh),
            jnp.logical_and((row & (2 * m - 1)) >= m, (col & (2 * m - 1)) < m))[None]
    nb = N_HEADS * nc
    sub = _iota2((nb * c // blk, blk, HEAD_D), 1)

    def per_head(fn):
        return jnp.concatenate([fn(slice(h * HEAD_D, (h + 1) * HEAD_D)) for h in range(N_HEADS)], axis=0)

    b_all = _bmm_01(ltri, lf_s[...].reshape(nc, c, N_HEADS * HEAD_D))
    b = per_head(lambda sl: b_all[:, :, sl])
    q = per_head(lambda sl: q_s[:, sl].reshape(nc, c, HEAD_D))
    k = per_head(lambda sl: k_s[:, sl].reshape(nc, c, HEAD_D))
    iv = per_head(lambda sl: i_s[:, sl].reshape(nc, c, HEAD_D))
    attn = jnp.zeros((nb, c, c), F32)
    for m in (32, 16, 8):
        b_m = b.reshape(nb * c // (2 * m), 2 * m, HEAD_D)
        ref = jnp.broadcast_to(b_m[:, m:m + 1, :], b_m.shape).reshape(nb, c, HEAD_D)
        e = jnp.exp(-jnp.abs(b - ref))
        attn = attn + jnp.where(level_masks[m], _bmm_nt(q * e, k * e), 0.0)
    o = _bmm(attn, iv)
    qb, kb, bb, ib = (a.reshape(nb * c // blk, blk, HEAD_D) for a in (q, k, b, iv))
    ob = jnp.zeros(qb.shape, F32)
    for s in range(blk):
        dec = jnp.exp(jnp.where(sub >= s, bb - bb[:, s:s + 1, :], -jnp.inf))
        a_col = jnp.sum(dec * qb * kb[:, s:s + 1, :], axis=-1, keepdims=True)
        ob = ob + a_col * ib[:, s:s + 1, :]
    o = o + ob.reshape(nb, c, HEAD_D)
    b_last = b[:, c - 1:c, :]
    q_in = (q * jnp.exp(b)).astype(BF16)
    upd = _bmm_tn(iv, k * jnp.exp(b_last - b))
    keep = jnp.exp(b_last)
    for h in range(N_HEADS):
        s_t = st_ref[h]
        outs = []
        for ci in range(h * nc, (h + 1) * nc):
            outs.append(o[ci] + _mm_nt(q_in[ci], s_t))
            s_t = s_t * keep[ci] + upd[ci]
        st_ref[h] = s_t
        o_s[:, h * HEAD_D:(h + 1) * HEAD_D] = jnp.concatenate(outs, axis=0)

    o_ref[0] = _head_rms_gate(o_s[...], g_ref[0].astype(F32), nw_ref[...], HEAD_D).astype(o_ref.dtype)


def _hgrn(proj, lb_logits, norm_w, *, tb, layer, col0):
    b, t, _ = proj.shape
    w = N_HEADS * HEAD_D
    specs = [pl.BlockSpec((1, tb, w), functools.partial(lambda i, j, off: (i, j, off), off=col0 + n))
             for n in range(4)]
    return pl.pallas_call(
        functools.partial(_hgrn_kernel, tb=tb, layer=layer),
        grid=(b, t // tb),
        in_specs=specs + [
            pl.BlockSpec(lb_logits.shape, lambda i, j: (0, 0)),
            pl.BlockSpec((1, HEAD_D), lambda i, j: (0, 0)),
        ],
        out_specs=pl.BlockSpec((1, tb, w), lambda i, j: (i, j, 0)),
        out_shape=jax.ShapeDtypeStruct((b, t, w), BF16),
        scratch_shapes=[pltpu.VMEM((tb, w), F32)] * 5 + [pltpu.VMEM((N_HEADS, HEAD_D, HEAD_D), F32)],
        compiler_params=pltpu.CompilerParams(
            dimension_semantics=("parallel", "arbitrary"), vmem_limit_bytes=VMEM_LIMIT),
        name="hgrn2",
    )(proj, proj, proj, proj, lb_logits, norm_w)


def _ret_kernel(q_ref, k_ref, v_ref, g_ref, cos_ref, sin_ref, nw_ref, o_ref, dmat_s, st_ref, *, tc):
    log_gammas = [math.log(1.0 - 2.0 ** (-5.0 - h)) for h in range(N_HEADS)]

    @pl.when(pl.program_id(1) == 0)
    def _():
        st_ref[...] = jnp.zeros(st_ref.shape, F32)
        rel = (_iota2((tc, tc), 0) - _iota2((tc, tc), 1)).astype(F32)
        for h in range(N_HEADS):
            dmat_s[h] = jnp.where(rel >= 0, jnp.exp(jnp.maximum(rel, 0.0) * log_gammas[h]), 0.0)

    pos = _iota2((tc, 1), 0).astype(F32)
    cos = cos_ref[...]
    sin = sin_ref[...]
    half = RET_DK // 2

    def rot(x):
        x1, x2 = x[:, :half], x[:, half:]
        return jnp.concatenate([x1 * cos - x2 * sin, x1 * sin + x2 * cos], axis=1)

    for h in range(N_HEADS):
        lg = log_gammas[h]
        qk_cols = slice(h * RET_DK, (h + 1) * RET_DK)
        v_cols = slice(h * RET_DV, (h + 1) * RET_DV)
        q = rot(q_ref[0, :, qk_cols].astype(F32))
        k = rot(k_ref[0, :, qk_cols].astype(F32)) * (RET_DK ** -0.5)
        v = v_ref[0, :, v_cols]
        s = st_ref[h]
        attn = _mm_nt(q, k) * dmat_s[h]
        o = _mm(q * jnp.exp((pos + 1.0) * lg), s) + _mm(attn, v)
        st_ref[h] = s * math.exp(tc * lg) + _mm_tn(k * jnp.exp((tc - 1.0 - pos) * lg), v)
        ms = jnp.mean(o * o, axis=-1, keepdims=True)
        o_ref[0, :, v_cols] = (o * lax.rsqrt(ms + RMS_EPS) * nw_ref[...]
                               * _silu(g_ref[0, :, v_cols].astype(F32))).astype(o_ref.dtype)


def _retention(proj, cos, sin, norm_w, *, tc):
    b, t, _ = proj.shape
    qk_w = N_HEADS * RET_DK
    v_w = N_HEADS * RET_DV
    return pl.pallas_call(
        functools.partial(_ret_kernel, tc=tc),
        grid=(b, t // tc),
        in_specs=[
            pl.BlockSpec((1, tc, qk_w), lambda i, j: (i, j, 0)),
            pl.BlockSpec((1, tc, qk_w), lambda i, j: (i, j, 1)),
            pl.BlockSpec((1, tc, v_w), lambda i, j: (i, j, 1)),
            pl.BlockSpec((1, tc, v_w), lambda i, j: (i, j, 2)),
            pl.BlockSpec((tc, RET_DK // 2), lambda i, j: (j, 0)),
            pl.BlockSpec((tc, RET_DK // 2), lambda i, j: (j, 0)),
            pl.BlockSpec((1, RET_DV), lambda i, j: (0, 0)),
        ],
        out_specs=pl.BlockSpec((1, tc, v_w), lambda i, j: (i, j, 0)),
        out_shape=jax.ShapeDtypeStruct((b, t, v_w), BF16),
        scratch_shapes=[pltpu.VMEM((N_HEADS, tc, tc), F32), pltpu.VMEM((N_HEADS, RET_DK, RET_DV), F32)],
        compiler_params=pltpu.CompilerParams(
            dimension_semantics=("parallel", "arbitrary"), vmem_limit_bytes=VMEM_LIMIT),
        name="retention",
    )(proj, proj, proj, proj, cos, sin, norm_w)


def _route_gates(logits_t):
    cl = [logits_t[g:g + 1, :] for g in range(N_GROUPS)]
    cmax = functools.reduce(jnp.maximum, cl)
    denom = sum(jnp.exp(x - cmax) for x in cl)
    g_prob = 1.0 / denom
    g_idx = jnp.full(cmax.shape, N_GROUPS - 1, jnp.int32)
    for g in range(N_GROUPS - 2, -1, -1):
        g_idx = jnp.where(cl[g] == cmax, g, g_idx)
    def fine_row(g, j):
        r = N_GROUPS + g * EXPERTS_PER_GROUP + j
        return logits_t[r:r + 1, :]

    fl = []
    for j in range(EXPERTS_PER_GROUP):
        x = fine_row(N_GROUPS - 1, j)
        for g in range(N_GROUPS - 2, -1, -1):
            x = jnp.where(g_idx == g, fine_row(g, j), x)
        fl.append(x)
    m1 = functools.reduce(jnp.maximum, fl)
    i1 = jnp.full(m1.shape, EXPERTS_PER_GROUP - 1, jnp.int32)
    for j in range(EXPERTS_PER_GROUP - 2, -1, -1):
        i1 = jnp.where(fl[j] == m1, j, i1)
    rest = [jnp.where(i1 == j, -jnp.inf, fl[j]) for j in range(EXPERTS_PER_GROUP)]
    m2 = functools.reduce(jnp.maximum, rest)
    i2 = jnp.full(m2.shape, EXPERTS_PER_GROUP - 1, jnp.int32)
    for j in range(EXPERTS_PER_GROUP - 2, -1, -1):
        i2 = jnp.where(jnp.logical_and(rest[j] == m2, i1 != j), j, i2)
    e2 = jnp.exp(m2 - m1)
    w1 = g_prob / (1.0 + e2)
    w2 = g_prob * e2 / (1.0 + e2)
    local = [jnp.where(i1 == j, w1, 0.0) + jnp.where(i2 == j, w2, 0.0) for j in range(EXPERTS_PER_GROUP)]
    return g_idx, local


def _outproj_kernel(a_ref, b_ref, wa_ref, wb_ref, x_ref, nw_ref, rw_ref, rb_ref,
                    x1_ref, h_ref, gk_ref, krow_ref, cnt_ref):
    x1 = (x_ref[...]
          + jnp.dot(a_ref[...], wa_ref[...], preferred_element_type=F32)
          + jnp.dot(b_ref[...], wb_ref[...], preferred_element_type=F32))
    x1_ref[...] = x1
    ms = jnp.mean(x1 * x1, axis=-1, keepdims=True)
    h = x1 * lax.rsqrt(ms + RMS_EPS) * nw_ref[...]
    h_ref[...] = h.astype(BF16)
    rw = rw_ref[...]
    hh = h.astype(BF16)
    hl = (h - hh.astype(F32)).astype(BF16)
    wh = rw.astype(BF16)
    wl = (rw - wh.astype(F32)).astype(BF16)
    hi_part = jnp.dot(hh, jnp.concatenate([wh, wl], axis=1), preferred_element_type=F32)
    logits = (hi_part[:, :LANES] + hi_part[:, LANES:]
              + jnp.dot(hl, wh, preferred_element_type=F32)) + rb_ref[...]
    g_idx, local = _route_gates(logits.T)
    tm = g_idx.shape[1]
    mem = (_iota2((8, tm), 0) == g_idx).astype(BF16)
    before = (_iota2((MOE_SUB, MOE_SUB), 0) < _iota2((MOE_SUB, MOE_SUB), 1)).astype(BF16)
    earlier = jnp.concatenate(
        [jnp.dot(mem[:, s:s + MOE_SUB], before, preferred_element_type=F32) for s in range(0, tm, MOE_SUB)],
        axis=1)
    rank = jnp.sum(mem.astype(F32) * earlier, axis=0, keepdims=True)
    key = g_idx.astype(F32) * MOE_KEY_STRIDE + rank
    krow_ref[...] = key
    sub_shift = int(math.log2(MOE_SUB))
    sub_sel = ((_iota2((tm, LANES), 0) >> sub_shift) == _iota2((tm, LANES), 1)).astype(BF16)
    cnt_ref[...] = jnp.dot(mem, sub_sel, preferred_element_type=F32)
    rows = jnp.concatenate([key] + local + [jnp.zeros((LANES - 1 - EXPERTS_PER_GROUP, tm), F32)], axis=0)
    gk_ref[...] = rows.T


def _outproj_route(a, b, a_blk, b_blk, wa, wb, x, nw, rw, rb, *, tm, name):
    n, d = x.shape
    ka = wa.shape[0]
    kb = wb.shape[0]
    return pl.pallas_call(
        _outproj_kernel,
        grid=(n // tm,),
        in_specs=[
            pl.BlockSpec((tm, ka), lambda i: (i, a_blk)),
            pl.BlockSpec((tm, kb), lambda i: (i, b_blk)),
            pl.BlockSpec((ka, d), lambda i: (0, 0)),
            pl.BlockSpec((kb, d), lambda i: (0, 0)),
            pl.BlockSpec((tm, d), lambda i: (i, 0)),
            pl.BlockSpec((1, d), lambda i: (0, 0)),
            pl.BlockSpec((d, LANES), lambda i: (0, 0)),
            pl.BlockSpec((1, LANES), lambda i: (0, 0)),
        ],
        out_specs=[
            pl.BlockSpec((tm, d), lambda i: (i, 0)),
            pl.BlockSpec((tm, d), lambda i: (i, 0)),
            pl.BlockSpec((tm, LANES), lambda i: (i, 0)),
            pl.BlockSpec((1, tm), lambda i: (0, i)),
            pl.BlockSpec((8, LANES), lambda i: (i, 0)),
        ],
        out_shape=[jax.ShapeDtypeStruct((n, d), F32),
                   jax.ShapeDtypeStruct((n, d), BF16),
                   jax.ShapeDtypeStruct((n, LANES), F32),
                   jax.ShapeDtypeStruct((1, n), F32),
                   jax.ShapeDtypeStruct((8 * (n // tm), LANES), F32)],
        compiler_params=pltpu.CompilerParams(
            dimension_semantics=("parallel",), vmem_limit_bytes=VMEM_LIMIT),
        name=name,
    )(a, b, wa, wb, x, nw, rw, rb)


def _round_up(v, m):
    return ((v + (m - 1)) // m) * m


def _moe_kernel(cnt_ref, h_ref, krow_ref, gk_ref, wgu_ref, wd_ref, x_ref, fnw_ref, o_ref,
                hc_s, gc_s, y_s, *, final_norm):
    i = pl.program_id(0)
    g = pl.program_id(1)
    tm = h_ref.shape[0]
    n_sub = tm // MOE_SUB

    def segments(grp):
        cnts = [cnt_ref[(i * n_sub + s) * N_GROUPS + grp] for s in range(n_sub)]
        starts = [jnp.int32(0)]
        for s in range(n_sub):
            starts.append(starts[-1] + _round_up(cnts[s], MOE_ALIGN))
        return cnts, starts

    counts, offs = segments(g)
    total = offs[-1]
    key0 = g.astype(F32) * MOE_KEY_STRIDE
    sub_iota = _iota2((MOE_BLK, MOE_SUB), 0).astype(F32)
    lane_iota = _iota2((MOE_SUB, MOE_BLK), 1).astype(F32)

    def pack(s, w):
        rows = slice(s * MOE_SUB, (s + 1) * MOE_SUB)
        base = key0 + (w * MOE_BLK).astype(F32)
        sel = (krow_ref[:, rows] == sub_iota + base).astype(BF16)
        dst = pl.ds(pl.multiple_of(offs[s] + w * MOE_BLK, MOE_ALIGN), MOE_BLK)
        hc_s[dst, :] = jnp.dot(sel, h_ref[rows, :], preferred_element_type=F32).astype(BF16)
        gc_s[dst, :] = _mm_01(sel, gk_ref[rows, :])

    for s in range(n_sub):
        def pack_more(w, carry, s=s):
            pack(s, w)
            return carry

        lax.fori_loop(1, (counts[s] + MOE_BLK - 1) // MOE_BLK, pack_more, 0)
    for s in range(n_sub):
        pack(s, jnp.int32(0))

    tail_rows = MOE_FIRST[-1] + MOE_FIRST_STEP
    tail = pl.ds(pl.multiple_of(total, MOE_ALIGN), tail_rows)
    hc_s[tail, :] = jnp.zeros((tail_rows, hc_s.shape[1]), BF16)
    gc_s[tail, :] = jnp.zeros((tail_rows, LANES), F32)

    def expert_rows(start, rows):
        blk = pl.ds(pl.multiple_of(start, MOE_FIRST_STEP), rows)
        hb = hc_s[blk, :]
        gates = gc_s[blk, :]
        y = jnp.zeros((rows, o_ref.shape[1]), F32)
        for e in range(EXPERTS_PER_GROUP):
            gu = jnp.dot(hb, wgu_ref[0, e], preferred_element_type=F32)
            act = _silu(gu[:, :D_EXPERT]) * gu[:, D_EXPERT:] * gates[:, 1 + e:2 + e]
            y = y + jnp.dot(act.astype(BF16), wd_ref[0, e], preferred_element_type=F32)
        y_s[g, blk, :] = y.astype(BF16)

    first = jnp.clip(_round_up(total, MOE_FIRST_STEP), MOE_FIRST[0], MOE_FIRST[-1])
    for size in MOE_FIRST:
        @pl.when(first == size)
        def _(size=size):
            expert_rows(0, size)

    n_rest = (jnp.maximum(total - first, 0) + MOE_BLK - 1) // MOE_BLK

    def rest(bi, carry):
        expert_rows(first + bi * MOE_BLK, MOE_BLK)
        return carry

    lax.fori_loop(0, n_rest, rest, 0)
    done = pl.ds(pl.multiple_of(first + n_rest * MOE_BLK, MOE_FIRST_STEP), MOE_BLK)
    y_s[g, done, :] = jnp.zeros((MOE_BLK, y_s.shape[2]), BF16)

    @pl.when(g == N_GROUPS - 1)
    def _():
        segs = [segments(grp) for grp in range(N_GROUPS)]
        for s in range(n_sub):
            rows = slice(s * MOE_SUB, (s + 1) * MOE_SUB)
            key_col = gk_ref[rows, 0:1]
            sel = jnp.concatenate(
                [(key_col == lane_iota + grp * MOE_KEY_STRIDE).astype(BF16) for grp in range(N_GROUPS)], axis=1)
            packed = jnp.concatenate(
                [y_s[grp, pl.ds(pl.multiple_of(segs[grp][1][s], MOE_ALIGN), MOE_BLK), :]
                 for grp in range(N_GROUPS)], axis=0)
            o_ref[rows, :] = x_ref[rows, :] + jnp.dot(sel, packed, preferred_element_type=F32)
            for grp in range(N_GROUPS):
                def more(w, carry, s=s, rows=rows, grp=grp, key_col=key_col):
                    base = grp * MOE_KEY_STRIDE + (w * MOE_BLK).astype(F32)
                    sel_w = (key_col == lane_iota + base).astype(BF16)
                    src = pl.ds(pl.multiple_of(segs[grp][1][s] + w * MOE_BLK, MOE_ALIGN), MOE_BLK)
                    o_ref[rows, :] += jnp.dot(sel_w, y_s[grp, src, :], preferred_element_type=F32)
                    return carry

                lax.fori_loop(1, (segs[grp][0][s] + MOE_BLK - 1) // MOE_BLK, more, 0)
        if final_norm:
            y = o_ref[...]
            ms = jnp.mean(y * y, axis=-1, keepdims=True)
            o_ref[...] = y * lax.rsqrt(ms + RMS_EPS) * fnw_ref[...]


def _moe(h, krow, gk, counts, wgu, wd, x, fnw, *, layer, tm, final_norm, name):
    n, d = x.shape
    buf_rows = _round_up(tm + (tm // MOE_SUB) * MOE_ALIGN + MOE_FIRST[-1] + MOE_FIRST_STEP, MOE_BLK)
    grid_spec = pltpu.PrefetchScalarGridSpec(
        num_scalar_prefetch=1,
        grid=(n // tm, N_GROUPS),
        in_specs=[
            pl.BlockSpec((tm, d), lambda i, g, c: (i, 0)),
            pl.BlockSpec((1, tm), lambda i, g, c: (0, i)),
            pl.BlockSpec((tm, LANES), lambda i, g, c: (i, 0)),
            pl.BlockSpec((1, EXPERTS_PER_GROUP, d, 2 * D_EXPERT), lambda i, g, c: (layer, g, 0, 0)),
            pl.BlockSpec((1, EXPERTS_PER_GROUP, D_EXPERT, d), lambda i, g, c: (layer, g, 0, 0)),
            pl.BlockSpec((tm, d), lambda i, g, c: (i, 0)),
            pl.BlockSpec((1, d), lambda i, g, c: (0, 0)),
        ],
        out_specs=pl.BlockSpec((tm, d), lambda i, g, c: (i, 0)),
        scratch_shapes=[pltpu.VMEM((buf_rows, d), BF16),
                        pltpu.VMEM((buf_rows, LANES), F32),
                        pltpu.VMEM((N_GROUPS, buf_rows, d), BF16)],
    )
    return pl.pallas_call(
        functools.partial(_moe_kernel, final_norm=final_norm),
        grid_spec=grid_spec,
        out_shape=jax.ShapeDtypeStruct((n, d), F32),
        compiler_params=pltpu.CompilerParams(
            dimension_semantics=("parallel", "arbitrary"), vmem_limit_bytes=VMEM_LIMIT),
        name=name,
    )(counts, h, krow, gk, wgu, wd, x, fnw)


def _pad_cols(a, width):
    return jnp.pad(a, ((0, 0), (0, width - a.shape[1])))


def _dispatch_counts(cnt, route_tm):
    c = cnt.reshape(-1, 8, LANES)[:, :N_GROUPS, :route_tm // MOE_SUB]
    return jnp.transpose(c, (0, 2, 1)).reshape(-1).astype(jnp.int32)


def _router_params(wc, bc, wf, bf):
    rw = _pad_cols(jnp.concatenate([wc, wf], axis=1), LANES)
    rb = _pad_cols(jnp.concatenate([bc, bf])[None, :], LANES)
    return rw, rb


def kernel(x, norm_mix_w, norm_ffn_w, even_w_in, gdn_conv_w, gdn_a_log, gdn_dt_bias, gdn_norm_w, hgrn_lb_logits, hgrn_norm_w, even_w_out, odd_w_in, ret_norm_w, odd_w_out, router_c_w, router_c_b, router_f_w, router_f_b, moe_w_gate_up, moe_w_down, final_norm_w):
    bsz, seq, d = x.shape
    n = bsz * seq
    xt = x.reshape(n, d)
    mix_w = N_HEADS * HEAD_D
    conv_cols = 3 * mix_w
    gdn_main = conv_cols + mix_w

    w_in = even_w_in[0]
    small0 = gdn_main
    w_main = [w_in[:, :small0].astype(BF16), w_in[:, small0 + 2 * N_HEADS:].astype(BF16)]
    w_small = jnp.concatenate(
        [_pad_cols(w_in[:, small0:small0 + N_HEADS], LANES),
         _pad_cols(w_in[:, small0 + N_HEADS:small0 + 2 * N_HEADS], LANES)], axis=1).astype(BF16)
    proj, small, wgu_b, wd_b, w_out, w_odd, w_out1 = _norm_matmul(
        xt, norm_mix_w[0][None, :], w_main, w_small, tm=512, tn=512, name="in_proj_even",
        cast=(moe_w_gate_up.reshape(-1, moe_w_gate_up.shape[-1]), moe_w_down.reshape(-1, moe_w_down.shape[-1]),
              even_w_out[0], odd_w_in[0], odd_w_out[0]),
        gdn_conv=(gdn_conv_w[0], seq))
    wgu_b = wgu_b.reshape(moe_w_gate_up.shape)
    wd_b = wd_b.reshape(moe_w_down.shape)
    proj = proj.reshape(bsz, seq, -1)
    small = small.reshape(bsz, seq, -1)
    o_a = _gdn(proj, small, _pad_cols(gdn_a_log[0][None, :], LANES),
               _pad_cols(gdn_dt_bias[0][None, :], LANES), gdn_norm_w[0][None, :], tb=512)
    o_b = _hgrn(proj, hgrn_lb_logits, hgrn_norm_w[0][None, :], tb=512, layer=0, col0=gdn_main // mix_w)
    rw, rb = _router_params(router_c_w[0], router_c_b[0], router_f_w[0], router_f_b[0])
    x1, h, gk, krow, cnt = _outproj_route(o_a.reshape(n, mix_w), o_b.reshape(n, mix_w), 0, 0,
                                          w_out[:mix_w], w_out[mix_w:], xt, norm_ffn_w[0][None, :], rw, rb,
                                          tm=ROUTE_TM, name="out_proj_even")
    x2 = _moe(h, krow, gk, _dispatch_counts(cnt, ROUTE_TM), wgu_b, wd_b, x1,
              final_norm_w[None, :], layer=0, tm=MOE_TM, final_norm=False, name="moe0")

    (proj1,) = _norm_matmul(x2, norm_mix_w[1][None, :], [w_odd], None, tm=512, tn=1024, name="in_proj_odd",
                            pair_split=(2 * N_HEADS, RET_DK))
    inv = 1.0 / (ROPE_BASE ** jnp.linspace(0.0, 1.0, RET_DK // 2, dtype=F32))
    ang = jnp.arange(seq, dtype=F32)[:, None] * inv[None, :]
    o_c = _retention(proj1.reshape(bsz, seq, -1), jnp.cos(ang), jnp.sin(ang), ret_norm_w[0][None, :],
                     tc=RET_CHUNK)
    o_c = o_c.reshape(n, -1)
    half = w_out1.shape[0] // 2
    rw, rb = _router_params(router_c_w[1], router_c_b[1], router_f_w[1], router_f_b[1])
    x3, h, gk, krow, cnt = _outproj_route(o_c, o_c, 0, 1, w_out1[:half], w_out1[half:], x2,
                                          norm_ffn_w[1][None, :], rw, rb, tm=ROUTE_TM, name="out_proj_odd")
    out = _moe(h, krow, gk, _dispatch_counts(cnt, ROUTE_TM), wgu_b, wd_b, x3,
               final_norm_w[None, :], layer=1, tm=MOE_TM, final_norm=True, name="moe1")
    return out.reshape(bsz, seq, d)
```

```python
import functools
import math

import jax
import jax.numpy as jnp
from jax import lax
from jax.experimental import pallas as pl
from jax.experimental.pallas import tpu as pltpu

F32 = jnp.float32
BF16 = jnp.bfloat16

D_MODEL = 1024
RMS_EPS = 1e-6
L2_EPS = 1e-6
CHUNK = 64
CONV_K = 4
N_HEADS = 4
HEAD_D = 128
RET_DK = 256
RET_DV = 512
RET_CHUNK = 256
ROPE_BASE = 10000.0
N_GROUPS = 4
EXPERTS_PER_GROUP = 4
N_EXPERTS = 16
D_EXPERT = 256
LANES = 128
ROUTE_TM = 1024
MOE_TM = 1024
MOE_SUB = 256
MOE_BLK = 128
MOE_ALIGN = 16
MOE_KEY_STRIDE = 4096.0
MOE_FIRST_STEP = 32
MOE_FIRST = (256, 288, 320, 352)
VMEM_LIMIT = 56 * 1024 * 1024


def _mm(a, b):
    return jnp.dot(a.astype(BF16), b.astype(BF16), preferred_element_type=F32)


def _mm_nt(a, b):
    return lax.dot_general(a.astype(BF16), b.astype(BF16), (((1,), (1,)), ((), ())),
                           preferred_element_type=F32)


def _mm_tn(a, b):
    return _mm(a.T, b)


def _mm_01(m01, x):
    hi = x.astype(BF16)
    lo = (x - hi.astype(F32)).astype(BF16)
    return (jnp.dot(m01, hi, preferred_element_type=F32)
            + jnp.dot(m01, lo, preferred_element_type=F32))


def _bmm(a, b):
    return jnp.einsum('cik,ckj->cij', a.astype(BF16), b.astype(BF16), preferred_element_type=F32)


def _bmm_nt(a, b):
    return jnp.einsum('cik,cjk->cij', a.astype(BF16), b.astype(BF16), preferred_element_type=F32)


def _bmm_tn(a, b):
    return _bmm(jnp.swapaxes(a, 1, 2), b)


def _bmm_01(m01, x):
    hi = x.astype(BF16)
    lo = (x - hi.astype(F32)).astype(BF16)
    return (jnp.einsum('cik,ckj->cij', m01, hi, preferred_element_type=F32)
            + jnp.einsum('cik,ckj->cij', m01, lo, preferred_element_type=F32))


def _sigmoid(x):
    return 1.0 / (1.0 + jnp.exp2(x * (-math.log2(math.e))))


def _silu(x):
    return x * _sigmoid(x)


def _softplus(x):
    return jnp.maximum(x, 0.0) + jnp.log(1.0 + jnp.exp(-jnp.abs(x)))


def _iota2(shape, dim):
    return lax.broadcasted_iota(jnp.int32, shape, dim)


def _head_rms_gate(o, gate, nw, width):
    outs = []
    for h in range(o.shape[1] // width):
        oh = o[:, h * width:(h + 1) * width]
        ms = jnp.mean(oh * oh, axis=-1, keepdims=True)
        outs.append(oh * lax.rsqrt(ms + RMS_EPS) * nw)
    return jnp.concatenate(outs, axis=1) * _silu(gate)


def _norm_mm_kernel(*refs, tn, n_w, has_small, n_cast, pair_split):
    if pair_split is not None:
        refs, wq_s = refs[:-1], refs[-1]
        n_hd, hd_w = pair_split
        half = hd_w // 2

        @pl.when(pl.program_id(0) == 0)
        def _():
            src = _iota2((hd_w, hd_w), 0)
            dst = _iota2((hd_w, hd_w), 1)
            split = (src == jnp.where(dst < half, 2 * dst, 2 * (dst - half) + 1)).astype(BF16)
            for hd in range(n_hd):
                cols = slice(hd * hd_w, (hd + 1) * hd_w)
                wq_s[:, cols] = jnp.dot(refs[2][:, cols], split, preferred_element_type=F32).astype(BF16)

    x_ref, nw_ref = refs[:2]
    w_refs = refs[2:2 + n_w]
    rest = refs[2 + n_w:]
    if has_small:
        ws_ref, rest = rest[0], rest[1:]
    cast_in, rest = rest[:n_cast], rest[n_cast:]
    o_ref = rest[0]
    if has_small:
        os_ref = rest[1]
    cast_out = rest[len(rest) - n_cast:]
    for src, dst in zip(cast_in, cast_out):
        dst[...] = src[...].astype(dst.dtype)
    x = x_ref[...]
    ms = jnp.mean(x * x, axis=-1, keepdims=True)
    hb = (x * lax.rsqrt(ms + RMS_EPS) * nw_ref[...]).astype(BF16)
    if has_small:
        os_ref[...] = jnp.dot(hb, ws_ref[...], preferred_element_type=F32)
    c0 = 0
    for wi, w_ref in enumerate(w_refs):
        for j in range(w_ref.shape[1] // tn):
            cols = slice(j * tn, (j + 1) * tn)
            reordered = pair_split is not None and wi == 0 and (j + 1) * tn <= n_hd * hd_w
            w_tile = wq_s[:, cols] if reordered else w_ref[:, cols]
            res = jnp.dot(hb, w_tile, preferred_element_type=F32)
            o_ref[:, c0 + j * tn:c0 + (j + 1) * tn] = res.astype(o_ref.dtype)
        c0 += w_ref.shape[1]


def _norm_matmul(x, nw, ws_main, ws, *, tm, tn, name, cast=(), pair_split=None):
    n, d = x.shape
    steps = n // tm
    nout = sum(w.shape[1] for w in ws_main)
    has_small = ws is not None
    in_specs = [
        pl.BlockSpec((tm, d), lambda i: (i, 0)),
        pl.BlockSpec((1, d), lambda i: (0, 0)),
    ] + [pl.BlockSpec(w.shape, lambda i: (0, 0)) for w in ws_main]
    out_specs = [pl.BlockSpec((tm, nout), lambda i: (i, 0))]
    out_shape = [jax.ShapeDtypeStruct((n, nout), BF16)]
    args = [x, nw] + list(ws_main)
    if has_small:
        nsmall = ws.shape[1]
        in_specs.append(pl.BlockSpec((d, nsmall), lambda i: (0, 0)))
        out_specs.append(pl.BlockSpec((tm, nsmall), lambda i: (i, 0)))
        out_shape.append(jax.ShapeDtypeStruct((n, nsmall), F32))
        args.append(ws)
    for a in cast:
        slab = (a.shape[0] // steps, a.shape[1])
        in_specs.append(pl.BlockSpec(slab, lambda i: (i, 0)))
        out_specs.append(pl.BlockSpec(slab, lambda i: (i, 0)))
        out_shape.append(jax.ShapeDtypeStruct(a.shape, BF16))
        args.append(a)
    scratch = []
    if pair_split is not None:
        assert (pair_split[0] * pair_split[1]) % tn == 0
        scratch.append(pltpu.VMEM((d, pair_split[0] * pair_split[1]), BF16))
    return pl.pallas_call(
        functools.partial(_norm_mm_kernel, tn=tn, n_w=len(ws_main), has_small=has_small, n_cast=len(cast),
                          pair_split=pair_split),
        grid=(n // tm,),
        in_specs=in_specs,
        out_specs=out_specs,
        out_shape=out_shape,
        scratch_shapes=scratch,
        compiler_params=pltpu.CompilerParams(
            dimension_semantics=("arbitrary",), vmem_limit_bytes=VMEM_LIMIT),
        name=name,
    )(*args)


def _gdn_kernel(qkv_ref, z_ref, sm_ref, cw_ref, alog_ref, dtb_ref, nw_ref, o_ref,
                xs_ref, qkv_s, o_s, m_s, sq_s, qe_s, dec_s, st_ref, *, tb):
    @pl.when(pl.program_id(1) == 0)
    def _():
        xs_ref[0:8, :] = jnp.zeros((8, xs_ref.shape[1]), F32)
        st_ref[...] = jnp.zeros(st_ref.shape, F32)

    c = CHUNK
    nc = tb // c
    nb = N_HEADS * nc
    qk_w = N_HEADS * HEAD_D
    row = _iota2((c, c), 0)
    col = _iota2((c, c), 1)
    causal = (row >= col)[None]
    strict = (row > col)[None]
    same16 = ((row >> 4) == (col >> 4))[None]
    same32 = ((row >> 5) == (col >> 5))[None]
    eye = (row == col).astype(F32)[None]
    ltri = jnp.broadcast_to((row >= col).astype(BF16)[None], (nc, c, c))
    lane = _iota2((nb, c, LANES), 2)

    def per_head(fn):
        return jnp.concatenate([fn(h) for h in range(N_HEADS)], axis=0)

    for h in range(N_HEADS):
        for base, scale in ((0, HEAD_D ** -0.5), (qk_w, 1.0), (2 * qk_w, None)):
            cols = slice(base + h * HEAD_D, base + (h + 1) * HEAD_D)
            x = qkv_ref[0, :, cols].astype(F32)
            xs_ref[8:8 + tb, cols] = x
            cw = cw_ref[:, cols]
            y = x * cw[CONV_K - 1:CONV_K, :]
            for j in range(1, CONV_K):
                y = y + xs_ref[pl.ds(8 - j, tb), cols] * cw[CONV_K - 1 - j:CONV_K - j, :]
            y = _silu(y)
            if scale is not None:
                y = y * (lax.rsqrt(jnp.sum(y * y, axis=-1, keepdims=True) + L2_EPS) * scale)
            qkv_s[:, cols] = y
    xs_ref[0:8, :] = xs_ref[tb:tb + 8, :]

    sm = sm_ref[0]
    beta_all = _sigmoid(sm[:, :LANES]).reshape(nc, c, LANES)
    g_all = (-jnp.exp(alog_ref[...]) * _softplus(sm[:, LANES:] + dtb_ref[...])).reshape(nc, c, LANES)
    gc_all = _bmm_01(ltri, g_all) * math.log2(math.e)
    q = per_head(lambda h: qkv_s[:, h * HEAD_D:(h + 1) * HEAD_D].reshape(nc, c, HEAD_D))
    k = per_head(lambda h: qkv_s[:, qk_w + h * HEAD_D:qk_w + (h + 1) * HEAD_D].reshape(nc, c, HEAD_D))
    v = per_head(lambda h: qkv_s[:, 2 * qk_w + h * HEAD_D:2 * qk_w + (h + 1) * HEAD_D].reshape(nc, c, HEAD_D))
    beta = per_head(lambda h: beta_all[:, :, h:h + 1])
    gcol = per_head(lambda h: gc_all[:, :, h:h + 1])
    g_hi = gcol.astype(BF16).astype(F32)
    g_mid = (gcol - g_hi).astype(BF16).astype(F32)
    g_lo = gcol - g_hi - g_mid
    ones_hi = jnp.where(lane < 6, 1.0, 0.0)
    lhs = jnp.where(lane == 0, g_hi, jnp.where(lane == 1, g_mid, jnp.where(lane == 2, g_lo, ones_hi)))
    rhs_g = jnp.where(lane == 3, -g_hi, jnp.where(lane == 4, -g_mid, jnp.where(lane == 5, -g_lo, ones_hi)))
    decay = jnp.exp2(jnp.where(causal, _bmm_nt(lhs, rhs_g), -jnp.inf))
    kb = k * beta
    a = jnp.where(strict, _bmm_nt(kb, k) * decay, 0.0)
    d = jnp.where(same16, a, 0.0)
    x_inv = eye - d
    dp = d
    for _ in range(3):
        dp = _bmm(dp, dp)
        x_inv = x_inv + _bmm(x_inv, dp)
    e = jnp.where(jnp.logical_and(same32, jnp.logical_not(same16)), a, 0.0)
    x_inv = x_inv - _bmm(x_inv, _bmm(e, x_inv))
    f = jnp.where(same32, 0.0, a)
    x_inv = x_inv - _bmm(x_inv, _bmm(f, x_inv))
    wu = _bmm(x_inv, jnp.concatenate([kb * jnp.exp2(gcol), v * beta], axis=2))
    attn = _bmm_nt(q, k) * decay
    g_last = gcol[:, c - 1:c, :]
    kd = k * jnp.exp2(g_last - gcol)
    mq = _bmm_tn(kd, wu)
    aw = _bmm(attn, wu)
    m_s[...] = mq[:, :, :HEAD_D].astype(BF16).reshape(N_HEADS, nc, HEAD_D, HEAD_D)
    sq_s[...] = mq[:, :, HEAD_D:].reshape(N_HEADS, nc, HEAD_D, HEAD_D)
    qe_s[...] = (q * jnp.exp2(gcol) - aw[:, :, :HEAD_D]).astype(BF16).reshape(N_HEADS, nc, c, HEAD_D)
    dec_s[...] = jnp.broadcast_to(jnp.exp2(g_last), (nb, 1, HEAD_D)).reshape(N_HEADS, nc, 1, HEAD_D)
    for h in range(N_HEADS):
        o_s[:, h * HEAD_D:(h + 1) * HEAD_D] = aw[h * nc:(h + 1) * nc, :, HEAD_D:].reshape(tb, HEAD_D)

    for ci in range(nc):
        for h in range(N_HEADS):
            hs = slice(h * HEAD_D, (h + 1) * HEAD_D)
            s_h = st_ref[h]
            s_b = s_h.astype(BF16)
            o_s[ci * c:(ci + 1) * c, hs] += jnp.dot(qe_s[h, ci], s_b, preferred_element_type=F32)
            st_ref[h] = (s_h * dec_s[h, ci] - jnp.dot(m_s[h, ci], s_b, preferred_element_type=F32)
                         + sq_s[h, ci])

    o_ref[0] = _head_rms_gate(o_s[...], z_ref[0].astype(F32), nw_ref[...], HEAD_D).astype(o_ref.dtype)


def _gdn(proj, small, conv_w, a_log, dt_bias, norm_w, *, tb):
    b, t, _ = proj.shape
    conv_cols = 3 * N_HEADS * HEAD_D
    mix_w = N_HEADS * HEAD_D
    return pl.pallas_call(
        functools.partial(_gdn_kernel, tb=tb),
        grid=(b, t // tb),
        in_specs=[
            pl.BlockSpec((1, tb, conv_cols), lambda i, j: (i, j, 0)),
            pl.BlockSpec((1, tb, mix_w), lambda i, j: (i, j, conv_cols // mix_w)),
            pl.BlockSpec((1, tb, 2 * LANES), lambda i, j: (i, j, 0)),
            pl.BlockSpec((CONV_K, conv_cols), lambda i, j: (0, 0)),
            pl.BlockSpec((1, LANES), lambda i, j: (0, 0)),
            pl.BlockSpec((1, LANES), lambda i, j: (0, 0)),
            pl.BlockSpec((1, HEAD_D), lambda i, j: (0, 0)),
        ],
        out_specs=pl.BlockSpec((1, tb, mix_w), lambda i, j: (i, j, 0)),
        out_shape=jax.ShapeDtypeStruct((b, t, mix_w), BF16),
        scratch_shapes=[
            pltpu.VMEM((tb + 8, conv_cols), F32),
            pltpu.VMEM((tb, conv_cols), F32),
            pltpu.VMEM((tb, mix_w), F32),
            pltpu.VMEM((N_HEADS, tb // CHUNK, HEAD_D, HEAD_D), BF16),
            pltpu.VMEM((N_HEADS, tb // CHUNK, HEAD_D, HEAD_D), F32),
            pltpu.VMEM((N_HEADS, tb // CHUNK, CHUNK, HEAD_D), BF16),
            pltpu.VMEM((N_HEADS, tb // CHUNK, 1, HEAD_D), F32),
            pltpu.VMEM((N_HEADS, HEAD_D, HEAD_D), F32),
        ],
        compiler_params=pltpu.CompilerParams(
            dimension_semantics=("parallel", "arbitrary"), vmem_limit_bytes=VMEM_LIMIT),
        name="gdn",
    )(proj, proj, small, conv_w, a_log, dt_bias, norm_w)


def _hgrn_kernel(q_ref, f_ref, i_ref, g_ref, lb_ref, nw_ref, o_ref,
                 q_s, k_s, lf_s, i_s, o_s, st_ref, *, tb, layer):
    @pl.when(pl.program_id(1) == 0)
    def _():
        st_ref[...] = jnp.zeros(st_ref.shape, F32)

    lbl = lb_ref[...]
    e_lb = jnp.exp(lbl - jnp.max(lbl, axis=0, keepdims=True))
    lb = jnp.sum(e_lb[:layer + 1], axis=0, keepdims=True) / jnp.sum(e_lb, axis=0, keepdims=True)

    f = lb + (1.0 - lb) * _sigmoid(f_ref[0].astype(F32))
    k_s[...] = 1.0 - f
    lf_s[...] = jnp.log(f)
    q_s[...] = _silu(q_ref[0].astype(F32)) * (HEAD_D ** -0.5)
    i_s[...] = i_ref[0].astype(F32)

    c = CHUNK
    nc = tb // c
    blk = 8
    row = _iota2((c, c), 0)
    col = _iota2((c, c), 1)
    ltri = jnp.broadcast_to((row >= col).astype(BF16)[None], (nc, c, c))
    level_masks = {}
    for m in (32, 16, 8):
        sh = int(math.log2(2 * m))
        level_masks[m] = jnp.logical_and(
            (row >> sh) == (col >> sh),
            jnp.logical_and((row & (2 * m - 1)) >= m, (col & (2 * m - 1)) < m))[None]
    nb = N_HEADS * nc
    sub = _iota2((nb * c // blk, blk, HEAD_D), 1)

    def per_head(fn):
        return jnp.concatenate([fn(slice(h * HEAD_D, (h + 1) * HEAD_D)) for h in range(N_HEADS)], axis=0)

    b_all = _bmm_01(ltri, lf_s[...].reshape(nc, c, N_HEADS * HEAD_D))
    b = per_head(lambda sl: b_all[:, :, sl]) * math.log2(math.e)
    q = per_head(lambda sl: q_s[:, sl].reshape(nc, c, HEAD_D))
    k = per_head(lambda sl: k_s[:, sl].reshape(nc, c, HEAD_D))
    iv = per_head(lambda sl: i_s[:, sl].reshape(nc, c, HEAD_D))
    attn = jnp.zeros((nb, c, c), F32)
    for m in (32, 16, 8):
        b_m = b.reshape(nb * c // (2 * m), 2 * m, HEAD_D)
        ref = jnp.broadcast_to(b_m[:, m:m + 1, :], b_m.shape).reshape(nb, c, HEAD_D)
        e = jnp.exp2(-jnp.abs(b - ref))
        attn = attn + jnp.where(level_masks[m], _bmm_nt(q * e, k * e), 0.0)
    o = _bmm(attn, iv)
    qb, kb, bb, ib = (a.reshape(nb * c // blk, blk, HEAD_D) for a in (q, k, b, iv))
    ob = jnp.zeros(qb.shape, F32)
    for s in range(blk):
        dec = jnp.exp2(jnp.where(sub >= s, bb - bb[:, s:s + 1, :], -jnp.inf))
        a_col = jnp.sum(dec * qb * kb[:, s:s + 1, :], axis=-1, keepdims=True)
        ob = ob + a_col * ib[:, s:s + 1, :]
    o = o + ob.reshape(nb, c, HEAD_D)
    b_last = b[:, c - 1:c, :]
    q_in = (q * jnp.exp2(b)).astype(BF16)
    upd = _bmm_tn(iv, k * jnp.exp2(b_last - b))
    keep = jnp.exp2(b_last)
    for h in range(N_HEADS):
        s_t = st_ref[h]
        outs = []
        for ci in range(h * nc, (h + 1) * nc):
            outs.append(o[ci] + _mm_nt(q_in[ci], s_t))
            s_t = s_t * keep[ci] + upd[ci]
        st_ref[h] = s_t
        o_s[:, h * HEAD_D:(h + 1) * HEAD_D] = jnp.concatenate(outs, axis=0)

    o_ref[0] = _head_rms_gate(o_s[...], g_ref[0].astype(F32), nw_ref[...], HEAD_D).astype(o_ref.dtype)


def _hgrn(proj, lb_logits, norm_w, *, tb, layer, col0):
    b, t, _ = proj.shape
    w = N_HEADS * HEAD_D
    specs = [pl.BlockSpec((1, tb, w), functools.partial(lambda i, j, off: (i, j, off), off=col0 + n))
             for n in range(4)]
    return pl.pallas_call(
        functools.partial(_hgrn_kernel, tb=tb, layer=layer),
        grid=(b, t // tb),
        in_specs=specs + [
            pl.BlockSpec(lb_logits.shape, lambda i, j: (0, 0)),
            pl.BlockSpec((1, HEAD_D), lambda i, j: (0, 0)),
        ],
        out_specs=pl.BlockSpec((1, tb, w), lambda i, j: (i, j, 0)),
        out_shape=jax.ShapeDtypeStruct((b, t, w), BF16),
        scratch_shapes=[pltpu.VMEM((tb, w), F32)] * 5 + [pltpu.VMEM((N_HEADS, HEAD_D, HEAD_D), F32)],
        compiler_params=pltpu.CompilerParams(
            dimension_semantics=("parallel", "arbitrary"), vmem_limit_bytes=VMEM_LIMIT),
        name="hgrn2",
    )(proj, proj, proj, proj, lb_logits, norm_w)


def _ret_kernel(q_ref, k_ref, v_ref, g_ref, cos_ref, sin_ref, nw_ref, o_ref, dmat_s, st_ref, *, tc):
    log_gammas = [math.log(1.0 - 2.0 ** (-5.0 - h)) for h in range(N_HEADS)]

    @pl.when(pl.program_id(1) == 0)
    def _():
        st_ref[...] = jnp.zeros(st_ref.shape, F32)
        rel = (_iota2((tc, tc), 0) - _iota2((tc, tc), 1)).astype(F32)
        for h in range(N_HEADS):
            dmat_s[h] = jnp.where(rel >= 0, jnp.exp(jnp.maximum(rel, 0.0) * log_gammas[h]), 0.0)

    pos = _iota2((tc, 1), 0).astype(F32)
    cos = cos_ref[...]
    sin = sin_ref[...]
    half = RET_DK // 2

    def rot(x):
        x1, x2 = x[:, :half], x[:, half:]
        return jnp.concatenate([x1 * cos - x2 * sin, x1 * sin + x2 * cos], axis=1)

    for h in range(N_HEADS):
        lg = log_gammas[h]
        qk_cols = slice(h * RET_DK, (h + 1) * RET_DK)
        v_cols = slice(h * RET_DV, (h + 1) * RET_DV)
        q = rot(q_ref[0, :, qk_cols].astype(F32))
        k = rot(k_ref[0, :, qk_cols].astype(F32)) * (RET_DK ** -0.5)
        v = v_ref[0, :, v_cols]
        s = st_ref[h]
        attn = _mm_nt(q, k) * dmat_s[h]
        o = _mm(q * jnp.exp((pos + 1.0) * lg), s) + _mm(attn, v)
        st_ref[h] = s * math.exp(tc * lg) + _mm_tn(k * jnp.exp((tc - 1.0 - pos) * lg), v)
        ms = jnp.mean(o * o, axis=-1, keepdims=True)
        o_ref[0, :, v_cols] = (o * lax.rsqrt(ms + RMS_EPS) * nw_ref[...]
                               * _silu(g_ref[0, :, v_cols].astype(F32))).astype(o_ref.dtype)


def _retention(proj, cos, sin, norm_w, *, tc):
    b, t, _ = proj.shape
    qk_w = N_HEADS * RET_DK
    v_w = N_HEADS * RET_DV
    return pl.pallas_call(
        functools.partial(_ret_kernel, tc=tc),
        grid=(b, t // tc),
        in_specs=[
            pl.BlockSpec((1, tc, qk_w), lambda i, j: (i, j, 0)),
            pl.BlockSpec((1, tc, qk_w), lambda i, j: (i, j, 1)),
            pl.BlockSpec((1, tc, v_w), lambda i, j: (i, j, 1)),
            pl.BlockSpec((1, tc, v_w), lambda i, j: (i, j, 2)),
            pl.BlockSpec((tc, RET_DK // 2), lambda i, j: (j, 0)),
            pl.BlockSpec((tc, RET_DK // 2), lambda i, j: (j, 0)),
            pl.BlockSpec((1, RET_DV), lambda i, j: (0, 0)),
        ],
        out_specs=pl.BlockSpec((1, tc, v_w), lambda i, j: (i, j, 0)),
        out_shape=jax.ShapeDtypeStruct((b, t, v_w), BF16),
        scratch_shapes=[pltpu.VMEM((N_HEADS, tc, tc), F32), pltpu.VMEM((N_HEADS, RET_DK, RET_DV), F32)],
        compiler_params=pltpu.CompilerParams(
            dimension_semantics=("parallel", "arbitrary"), vmem_limit_bytes=VMEM_LIMIT),
        name="retention",
    )(proj, proj, proj, proj, cos, sin, norm_w)


def _route_gates(logits_t):
    cl = [logits_t[g:g + 1, :] for g in range(N_GROUPS)]
    cmax = functools.reduce(jnp.maximum, cl)
    denom = sum(jnp.exp(x - cmax) for x in cl)
    g_prob = 1.0 / denom
    g_idx = jnp.full(cmax.shape, N_GROUPS - 1, jnp.int32)
    for g in range(N_GROUPS - 2, -1, -1):
        g_idx = jnp.where(cl[g] == cmax, g, g_idx)
    def fine_row(g, j):
        r = N_GROUPS + g * EXPERTS_PER_GROUP + j
        return logits_t[r:r + 1, :]

    fl = []
    for j in range(EXPERTS_PER_GROUP):
        x = fine_row(N_GROUPS - 1, j)
        for g in range(N_GROUPS - 2, -1, -1):
            x = jnp.where(g_idx == g, fine_row(g, j), x)
        fl.append(x)
    m1 = functools.reduce(jnp.maximum, fl)
    i1 = jnp.full(m1.shape, EXPERTS_PER_GROUP - 1, jnp.int32)
    for j in range(EXPERTS_PER_GROUP - 2, -1, -1):
        i1 = jnp.where(fl[j] == m1, j, i1)
    rest = [jnp.where(i1 == j, -jnp.inf, fl[j]) for j in range(EXPERTS_PER_GROUP)]
    m2 = functools.reduce(jnp.maximum, rest)
    i2 = jnp.full(m2.shape, EXPERTS_PER_GROUP - 1, jnp.int32)
    for j in range(EXPERTS_PER_GROUP - 2, -1, -1):
        i2 = jnp.where(jnp.logical_and(rest[j] == m2, i1 != j), j, i2)
    e2 = jnp.exp(m2 - m1)
    w1 = g_prob / (1.0 + e2)
    w2 = g_prob * e2 / (1.0 + e2)
    local = [jnp.where(i1 == j, w1, 0.0) + jnp.where(i2 == j, w2, 0.0) for j in range(EXPERTS_PER_GROUP)]
    return g_idx, local


def _outproj_kernel(a_ref, b_ref, wa_ref, wb_ref, x_ref, nw_ref, rw_ref, rb_ref,
                    x1_ref, h_ref, gk_ref, krow_ref, cnt_ref):
    x1 = (x_ref[...]
          + jnp.dot(a_ref[...], wa_ref[...], preferred_element_type=F32)
          + jnp.dot(b_ref[...], wb_ref[...], preferred_element_type=F32))
    x1_ref[...] = x1
    ms = jnp.mean(x1 * x1, axis=-1, keepdims=True)
    h = x1 * lax.rsqrt(ms + RMS_EPS) * nw_ref[...]
    h_ref[...] = h.astype(BF16)
    rw = rw_ref[...]
    hh = h.astype(BF16)
    hl = (h - hh.astype(F32)).astype(BF16)
    wh = rw.astype(BF16)
    wl = (rw - wh.astype(F32)).astype(BF16)
    hi_part = jnp.dot(hh, jnp.concatenate([wh, wl], axis=1), preferred_element_type=F32)
    logits = (hi_part[:, :LANES] + hi_part[:, LANES:]
              + jnp.dot(hl, wh, preferred_element_type=F32)) + rb_ref[...]
    g_idx, local = _route_gates(logits.T)
    tm = g_idx.shape[1]
    mem = (_iota2((8, tm), 0) == g_idx).astype(BF16)
    before = (_iota2((MOE_SUB, MOE_SUB), 0) < _iota2((MOE_SUB, MOE_SUB), 1)).astype(BF16)
    earlier = jnp.concatenate(
        [jnp.dot(mem[:, s:s + MOE_SUB], before, preferred_element_type=F32) for s in range(0, tm, MOE_SUB)],
        axis=1)
    rank = jnp.sum(mem.astype(F32) * earlier, axis=0, keepdims=True)
    key = g_idx.astype(F32) * MOE_KEY_STRIDE + rank
    krow_ref[...] = key
    sub_shift = int(math.log2(MOE_SUB))
    sub_sel = ((_iota2((tm, LANES), 0) >> sub_shift) == _iota2((tm, LANES), 1)).astype(BF16)
    cnt_ref[...] = jnp.dot(mem, sub_sel, preferred_element_type=F32)
    rows = jnp.concatenate([key] + local + [jnp.zeros((LANES - 1 - EXPERTS_PER_GROUP, tm), F32)], axis=0)
    gk_ref[...] = rows.T


def _outproj_route(a, b, a_blk, b_blk, wa, wb, x, nw, rw, rb, *, tm, name):
    n, d = x.shape
    ka = wa.shape[0]
    kb = wb.shape[0]
    return pl.pallas_call(
        _outproj_kernel,
        grid=(n // tm,),
        in_specs=[
            pl.BlockSpec((tm, ka), lambda i: (i, a_blk)),
            pl.BlockSpec((tm, kb), lambda i: (i, b_blk)),
            pl.BlockSpec((ka, d), lambda i: (0, 0)),
            pl.BlockSpec((kb, d), lambda i: (0, 0)),
            pl.BlockSpec((tm, d), lambda i: (i, 0)),
            pl.BlockSpec((1, d), lambda i: (0, 0)),
            pl.BlockSpec((d, LANES), lambda i: (0, 0)),
            pl.BlockSpec((1, LANES), lambda i: (0, 0)),
        ],
        out_specs=[
            pl.BlockSpec((tm, d), lambda i: (i, 0)),
            pl.BlockSpec((tm, d), lambda i: (i, 0)),
            pl.BlockSpec((tm, LANES), lambda i: (i, 0)),
            pl.BlockSpec((1, tm), lambda i: (0, i)),
            pl.BlockSpec((8, LANES), lambda i: (i, 0)),
        ],
        out_shape=[jax.ShapeDtypeStruct((n, d), F32),
                   jax.ShapeDtypeStruct((n, d), BF16),
                   jax.ShapeDtypeStruct((n, LANES), F32),
                   jax.ShapeDtypeStruct((1, n), F32),
                   jax.ShapeDtypeStruct((8 * (n // tm), LANES), F32)],
        compiler_params=pltpu.CompilerParams(
            dimension_semantics=("parallel",), vmem_limit_bytes=VMEM_LIMIT),
        name=name,
    )(a, b, wa, wb, x, nw, rw, rb)


def _round_up(v, m):
    return ((v + (m - 1)) // m) * m


def _round_up_pow2(v, m):
    return (v + (m - 1)) & ~(m - 1)


def _cdiv_pow2(v, m):
    return lax.shift_right_logical(v + (m - 1), int(math.log2(m)))


def _moe_kernel(cnt_ref, h_ref, krow_ref, gk_ref, wgu_ref, wd_ref, x_ref, fnw_ref, o_ref,
                hc_s, gc_s, y_s, *, final_norm):
    i = pl.program_id(0)
    g = pl.program_id(1)
    tm = h_ref.shape[0]
    n_sub = tm // MOE_SUB

    def segments(grp):
        cnts = [cnt_ref[(i * n_sub + s) * N_GROUPS + grp] for s in range(n_sub)]
        starts = [jnp.int32(0)]
        for s in range(n_sub):
            starts.append(starts[-1] + _round_up_pow2(cnts[s], MOE_ALIGN))
        return cnts, starts

    counts, offs = segments(g)
    total = offs[-1]
    key0 = g.astype(F32) * MOE_KEY_STRIDE
    sub_iota = _iota2((MOE_BLK, MOE_SUB), 0).astype(F32)
    lane_iota = _iota2((MOE_SUB, MOE_BLK), 1).astype(F32)

    def pack(s, w):
        rows = slice(s * MOE_SUB, (s + 1) * MOE_SUB)
        base = key0 + (w * MOE_BLK).astype(F32)
        sel = (krow_ref[:, rows] == sub_iota + base).astype(BF16)
        dst = pl.ds(pl.multiple_of(offs[s] + w * MOE_BLK, MOE_ALIGN), MOE_BLK)
        hc_s[dst, :] = jnp.dot(sel, h_ref[rows, :], preferred_element_type=F32).astype(BF16)
        gc_s[dst, :] = _mm_01(sel, gk_ref[rows, :])

    for s in range(n_sub):
        def pack_more(w, carry, s=s):
            pack(s, w)
            return carry

        lax.fori_loop(1, _cdiv_pow2(counts[s], MOE_BLK), pack_more, 0)
    for s in range(n_sub):
        pack(s, jnp.int32(0))

    tail_rows = MOE_FIRST[-1] + MOE_FIRST_STEP
    tail = pl.ds(pl.multiple_of(total, MOE_ALIGN), tail_rows)
    hc_s[tail, :] = jnp.zeros((tail_rows, hc_s.shape[1]), BF16)
    gc_s[tail, :] = jnp.zeros((tail_rows, LANES), F32)

    def expert_rows(start, rows):
        blk = pl.ds(pl.multiple_of(start, MOE_FIRST_STEP), rows)
        hb = hc_s[blk, :]
        gates = gc_s[blk, :]
        y = jnp.zeros((rows, o_ref.shape[1]), F32)
        for e in range(EXPERTS_PER_GROUP):
            gu = jnp.dot(hb, wgu_ref[0, e], preferred_element_type=F32)
            act = _silu(gu[:, :D_EXPERT]) * gu[:, D_EXPERT:] * gates[:, 1 + e:2 + e]
            y = y + jnp.dot(act.astype(BF16), wd_ref[0, e], preferred_element_type=F32)
        y_s[g, blk, :] = y.astype(BF16)

    first = jnp.clip(_round_up_pow2(total, MOE_FIRST_STEP), MOE_FIRST[0], MOE_FIRST[-1])
    for size in MOE_FIRST:
        @pl.when(first == size)
        def _(size=size):
            expert_rows(0, size)

    n_rest = _cdiv_pow2(jnp.maximum(total - first, 0), MOE_BLK)

    def rest(bi, carry):
        expert_rows(first + bi * MOE_BLK, MOE_BLK)
        return carry

    lax.fori_loop(0, n_rest, rest, 0)
    done = pl.ds(pl.multiple_of(first + n_rest * MOE_BLK, MOE_FIRST_STEP), MOE_BLK)
    y_s[g, done, :] = jnp.zeros((MOE_BLK, y_s.shape[2]), BF16)

    @pl.when(g == N_GROUPS - 1)
    def _():
        segs = [segments(grp) for grp in range(N_GROUPS)]
        for s in range(n_sub):
            rows = slice(s * MOE_SUB, (s + 1) * MOE_SUB)
            key_col = gk_ref[rows, 0:1]
            sel = jnp.concatenate(
                [(key_col == lane_iota + grp * MOE_KEY_STRIDE).astype(BF16) for grp in range(N_GROUPS)], axis=1)
            packed = jnp.concatenate(
                [y_s[grp, pl.ds(pl.multiple_of(segs[grp][1][s], MOE_ALIGN), MOE_BLK), :]
                 for grp in range(N_GROUPS)], axis=0)
            o_ref[rows, :] = x_ref[rows, :] + jnp.dot(sel, packed, preferred_element_type=F32)
            for grp in range(N_GROUPS):
                def more(w, carry, s=s, rows=rows, grp=grp, key_col=key_col):
                    base = grp * MOE_KEY_STRIDE + (w * MOE_BLK).astype(F32)
                    sel_w = (key_col == lane_iota + base).astype(BF16)
                    src = pl.ds(pl.multiple_of(segs[grp][1][s] + w * MOE_BLK, MOE_ALIGN), MOE_BLK)
                    o_ref[rows, :] += jnp.dot(sel_w, y_s[grp, src, :], preferred_element_type=F32)
                    return carry

                lax.fori_loop(1, _cdiv_pow2(segs[grp][0][s], MOE_BLK), more, 0)
        if final_norm:
            y = o_ref[...]
            ms = jnp.mean(y * y, axis=-1, keepdims=True)
            o_ref[...] = y * lax.rsqrt(ms + RMS_EPS) * fnw_ref[...]


def _moe(h, krow, gk, counts, wgu, wd, x, fnw, *, layer, tm, final_norm, name):
    n, d = x.shape
    buf_rows = _round_up(tm + (tm // MOE_SUB) * MOE_ALIGN + MOE_FIRST[-1] + MOE_FIRST_STEP, MOE_BLK)
    grid_spec = pltpu.PrefetchScalarGridSpec(
        num_scalar_prefetch=1,
        grid=(n // tm, N_GROUPS),
        in_specs=[
            pl.BlockSpec((tm, d), lambda i, g, c: (i, 0)),
            pl.BlockSpec((1, tm), lambda i, g, c: (0, i)),
            pl.BlockSpec((tm, LANES), lambda i, g, c: (i, 0)),
            pl.BlockSpec((1, EXPERTS_PER_GROUP, d, 2 * D_EXPERT), lambda i, g, c: (layer, g, 0, 0)),
            pl.BlockSpec((1, EXPERTS_PER_GROUP, D_EXPERT, d), lambda i, g, c: (layer, g, 0, 0)),
            pl.BlockSpec((tm, d), lambda i, g, c: (i, 0)),
            pl.BlockSpec((1, d), lambda i, g, c: (0, 0)),
        ],
        out_specs=pl.BlockSpec((tm, d), lambda i, g, c: (i, 0)),
        scratch_shapes=[pltpu.VMEM((buf_rows, d), BF16),
                        pltpu.VMEM((buf_rows, LANES), F32),
                        pltpu.VMEM((N_GROUPS, buf_rows, d), BF16)],
    )
    return pl.pallas_call(
        functools.partial(_moe_kernel, final_norm=final_norm),
        grid_spec=grid_spec,
        out_shape=jax.ShapeDtypeStruct((n, d), F32),
        compiler_params=pltpu.CompilerParams(
            dimension_semantics=("parallel", "arbitrary"), vmem_limit_bytes=VMEM_LIMIT),
        name=name,
    )(counts, h, krow, gk, wgu, wd, x, fnw)


def _pad_cols(a, width):
    return jnp.pad(a, ((0, 0), (0, width - a.shape[1])))


def _dispatch_counts(cnt, route_tm):
    c = cnt.reshape(-1, 8, LANES)[:, :N_GROUPS, :route_tm // MOE_SUB]
    return jnp.transpose(c, (0, 2, 1)).reshape(-1).astype(jnp.int32)


def _router_params(wc, bc, wf, bf):
    rw = _pad_cols(jnp.concatenate([wc, wf], axis=1), LANES)
    rb = _pad_cols(jnp.concatenate([bc, bf])[None, :], LANES)
    return rw, rb


def kernel(x, norm_mix_w, norm_ffn_w, even_w_in, gdn_conv_w, gdn_a_log, gdn_dt_bias, gdn_norm_w, hgrn_lb_logits, hgrn_norm_w, even_w_out, odd_w_in, ret_norm_w, odd_w_out, router_c_w, router_c_b, router_f_w, router_f_b, moe_w_gate_up, moe_w_down, final_norm_w):
    bsz, seq, d = x.shape
    n = bsz * seq
    xt = x.reshape(n, d)
    mix_w = N_HEADS * HEAD_D
    conv_cols = 3 * mix_w
    gdn_main = conv_cols + mix_w

    w_in = even_w_in[0]
    small0 = gdn_main
    w_main = [w_in[:, :small0].astype(BF16), w_in[:, small0 + 2 * N_HEADS:].astype(BF16)]
    w_small = jnp.concatenate(
        [_pad_cols(w_in[:, small0:small0 + N_HEADS], LANES),
         _pad_cols(w_in[:, small0 + N_HEADS:small0 + 2 * N_HEADS], LANES)], axis=1).astype(BF16)
    proj, small, wgu_b, wd_b, w_out, w_odd, w_out1 = _norm_matmul(
        xt, norm_mix_w[0][None, :], w_main, w_small, tm=512, tn=1024, name="in_proj_even",
        cast=(moe_w_gate_up.reshape(-1, moe_w_gate_up.shape[-1]), moe_w_down.reshape(-1, moe_w_down.shape[-1]),
              even_w_out[0], odd_w_in[0], odd_w_out[0]))
    wgu_b = wgu_b.reshape(moe_w_gate_up.shape)
    wd_b = wd_b.reshape(moe_w_down.shape)
    proj = proj.reshape(bsz, seq, -1)
    small = small.reshape(bsz, seq, -1)
    o_a = _gdn(proj, small, gdn_conv_w[0], _pad_cols(gdn_a_log[0][None, :], LANES),
               _pad_cols(gdn_dt_bias[0][None, :], LANES), gdn_norm_w[0][None, :], tb=512)
    o_b = _hgrn(proj, hgrn_lb_logits, hgrn_norm_w[0][None, :], tb=512, layer=0, col0=gdn_main // mix_w)
    rw, rb = _router_params(router_c_w[0], router_c_b[0], router_f_w[0], router_f_b[0])
    x1, h, gk, krow, cnt = _outproj_route(o_a.reshape(n, mix_w), o_b.reshape(n, mix_w), 0, 0,
                                          w_out[:mix_w], w_out[mix_w:], xt, norm_ffn_w[0][None, :], rw, rb,
                                          tm=ROUTE_TM, name="out_proj_even")
    x2 = _moe(h, krow, gk, _dispatch_counts(cnt, ROUTE_TM), wgu_b, wd_b, x1,
              final_norm_w[None, :], layer=0, tm=MOE_TM, final_norm=False, name="moe0")

    (proj1,) = _norm_matmul(x2, norm_mix_w[1][None, :], [w_odd], None, tm=512, tn=1024, name="in_proj_odd",
                            pair_split=(2 * N_HEADS, RET_DK))
    inv = 1.0 / (ROPE_BASE ** jnp.linspace(0.0, 1.0, RET_DK // 2, dtype=F32))
    ang = jnp.arange(seq, dtype=F32)[:, None] * inv[None, :]
    o_c = _retention(proj1.reshape(bsz, seq, -1), jnp.cos(ang), jnp.sin(ang), ret_norm_w[0][None, :],
                     tc=RET_CHUNK)
    o_c = o_c.reshape(n, -1)
    half = w_out1.shape[0] // 2
    rw, rb = _router_params(router_c_w[1], router_c_b[1], router_f_w[1], router_f_b[1])
    x3, h, gk, krow, cnt = _outproj_route(o_c, o_c, 0, 1, w_out1[:half], w_out1[half:], x2,
                                          norm_ffn_w[1][None, :], rw, rb, tm=ROUTE_TM, name="out_proj_odd")
    out = _moe(h, krow, gk, _dispatch_counts(cnt, ROUTE_TM), wgu_b, wd_b, x3,
               final_norm_w[None, :], layer=1, tm=MOE_TM, final_norm=True, name="moe1")
    return out.reshape(bsz, seq, d)
```

```python
import functools
import math

import jax
import jax.numpy as jnp
from jax import lax
from jax.experimental import pallas as pl
from jax.experimental.pallas import tpu as pltpu

F32 = jnp.float32
BF16 = jnp.bfloat16

D_MODEL = 1024
RMS_EPS = 1e-6
L2_EPS = 1e-6
CHUNK = 64
CONV_K = 4
N_HEADS = 4
HEAD_D = 128
RET_DK = 256
RET_DV = 512
RET_CHUNK = 256
ROPE_BASE = 10000.0
N_GROUPS = 4
EXPERTS_PER_GROUP = 4
N_EXPERTS = 16
D_EXPERT = 256
LANES = 128
ROUTE_TM = 1024
MOE_TM = 1024
MOE_SUB = 256
MOE_BLK = 128
MOE_ALIGN = 16
MOE_KEY_STRIDE = 4096.0
MOE_FIRST_STEP = 32
MOE_FIRST = (256, 288, 320, 352)
VMEM_LIMIT = 56 * 1024 * 1024


def _mm(a, b):
    return jnp.dot(a.astype(BF16), b.astype(BF16), preferred_element_type=F32)


def _mm_nt(a, b):
    return lax.dot_general(a.astype(BF16), b.astype(BF16), (((1,), (1,)), ((), ())),
                           preferred_element_type=F32)


def _mm_tn(a, b):
    return _mm(a.T, b)


def _mm_01(m01, x):
    hi = x.astype(BF16)
    lo = (x - hi.astype(F32)).astype(BF16)
    return (jnp.dot(m01, hi, preferred_element_type=F32)
            + jnp.dot(m01, lo, preferred_element_type=F32))


def _bmm(a, b):
    return jnp.einsum('cik,ckj->cij', a.astype(BF16), b.astype(BF16), preferred_element_type=F32)


def _bmm_nt(a, b):
    return jnp.einsum('cik,cjk->cij', a.astype(BF16), b.astype(BF16), preferred_element_type=F32)


def _bmm_tn(a, b):
    return _bmm(jnp.swapaxes(a, 1, 2), b)


def _bmm_01(m01, x):
    hi = x.astype(BF16)
    lo = (x - hi.astype(F32)).astype(BF16)
    return (jnp.einsum('cik,ckj->cij', m01, hi, preferred_element_type=F32)
            + jnp.einsum('cik,ckj->cij', m01, lo, preferred_element_type=F32))


def _sigmoid(x):
    return 1.0 / (1.0 + jnp.exp2(x * (-math.log2(math.e))))


def _silu(x):
    return x * _sigmoid(x)


def _softplus(x):
    return jnp.maximum(x, 0.0) + jnp.log(1.0 + jnp.exp(-jnp.abs(x)))


def _iota2(shape, dim):
    return lax.broadcasted_iota(jnp.int32, shape, dim)


def _head_rms_gate(o, gate, nw, width):
    outs = []
    for h in range(o.shape[1] // width):
        oh = o[:, h * width:(h + 1) * width]
        ms = jnp.mean(oh * oh, axis=-1, keepdims=True)
        outs.append(oh * lax.rsqrt(ms + RMS_EPS) * nw)
    return jnp.concatenate(outs, axis=1) * _silu(gate)


def _norm_mm_kernel(*refs, tn, n_w, has_small, n_cast, pair_split):
    if pair_split is not None:
        refs, wq_s = refs[:-1], refs[-1]
        n_hd, hd_w = pair_split
        half = hd_w // 2

        @pl.when(pl.program_id(0) == 0)
        def _():
            src = _iota2((hd_w, hd_w), 0)
            dst = _iota2((hd_w, hd_w), 1)
            split = (src == jnp.where(dst < half, 2 * dst, 2 * (dst - half) + 1)).astype(BF16)
            for hd in range(n_hd):
                cols = slice(hd * hd_w, (hd + 1) * hd_w)
                wq_s[:, cols] = jnp.dot(refs[2][:, cols], split, preferred_element_type=F32).astype(BF16)

    x_ref, nw_ref = refs[:2]
    w_refs = refs[2:2 + n_w]
    rest = refs[2 + n_w:]
    if has_small:
        ws_ref, rest = rest[0], rest[1:]
    cast_in, rest = rest[:n_cast], rest[n_cast:]
    o_ref = rest[0]
    if has_small:
        os_ref = rest[1]
    cast_out = rest[len(rest) - n_cast:]
    for src, dst in zip(cast_in, cast_out):
        dst[...] = src[...].astype(dst.dtype)
    x = x_ref[...]
    ms = jnp.mean(x * x, axis=-1, keepdims=True)
    hb = (x * lax.rsqrt(ms + RMS_EPS) * nw_ref[...]).astype(BF16)
    if has_small:
        os_ref[...] = jnp.dot(hb, ws_ref[...], preferred_element_type=F32)
    c0 = 0
    for wi, w_ref in enumerate(w_refs):
        for j in range(w_ref.shape[1] // tn):
            cols = slice(j * tn, (j + 1) * tn)
            reordered = pair_split is not None and wi == 0 and (j + 1) * tn <= n_hd * hd_w
            w_tile = wq_s[:, cols] if reordered else w_ref[:, cols]
            res = jnp.dot(hb, w_tile, preferred_element_type=F32)
            o_ref[:, c0 + j * tn:c0 + (j + 1) * tn] = res.astype(o_ref.dtype)
        c0 += w_ref.shape[1]


def _norm_matmul(x, nw, ws_main, ws, *, tm, tn, name, cast=(), pair_split=None):
    n, d = x.shape
    steps = n // tm
    nout = sum(w.shape[1] for w in ws_main)
    has_small = ws is not None
    in_specs = [
        pl.BlockSpec((tm, d), lambda i: (i, 0)),
        pl.BlockSpec((1, d), lambda i: (0, 0)),
    ] + [pl.BlockSpec(w.shape, lambda i: (0, 0)) for w in ws_main]
    out_specs = [pl.BlockSpec((tm, nout), lambda i: (i, 0))]
    out_shape = [jax.ShapeDtypeStruct((n, nout), BF16)]
    args = [x, nw] + list(ws_main)
    if has_small:
        nsmall = ws.shape[1]
        in_specs.append(pl.BlockSpec((d, nsmall), lambda i: (0, 0)))
        out_specs.append(pl.BlockSpec((tm, nsmall), lambda i: (i, 0)))
        out_shape.append(jax.ShapeDtypeStruct((n, nsmall), F32))
        args.append(ws)
    for a in cast:
        slab = (a.shape[0] // steps, a.shape[1])
        in_specs.append(pl.BlockSpec(slab, lambda i: (i, 0)))
        out_specs.append(pl.BlockSpec(slab, lambda i: (i, 0)))
        out_shape.append(jax.ShapeDtypeStruct(a.shape, BF16))
        args.append(a)
    scratch = []
    if pair_split is not None:
        assert (pair_split[0] * pair_split[1]) % tn == 0
        scratch.append(pltpu.VMEM((d, pair_split[0] * pair_split[1]), BF16))
    return pl.pallas_call(
        functools.partial(_norm_mm_kernel, tn=tn, n_w=len(ws_main), has_small=has_small, n_cast=len(cast),
                          pair_split=pair_split),
        grid=(n // tm,),
        in_specs=in_specs,
        out_specs=out_specs,
        out_shape=out_shape,
        scratch_shapes=scratch,
        compiler_params=pltpu.CompilerParams(
            dimension_semantics=("arbitrary",), vmem_limit_bytes=VMEM_LIMIT),
        name=name,
    )(*args)


def _gdn_kernel(qkv_ref, z_ref, sm_ref, cw_ref, alog_ref, dtb_ref, nw_ref, o_ref,
                xs_ref, qkv_s, o_s, m_s, sq_s, qe_s, dec_s, st_ref, *, tb):
    @pl.when(pl.program_id(1) == 0)
    def _():
        xs_ref[0:8, :] = jnp.zeros((8, xs_ref.shape[1]), F32)
        st_ref[...] = jnp.zeros(st_ref.shape, F32)

    c = CHUNK
    nc = tb // c
    nb = N_HEADS * nc
    qk_w = N_HEADS * HEAD_D
    row = _iota2((c, c), 0)
    col = _iota2((c, c), 1)
    causal = (row >= col)[None]
    strict = (row > col)[None]
    same16 = ((row >> 4) == (col >> 4))[None]
    same32 = ((row >> 5) == (col >> 5))[None]
    eye = (row == col).astype(F32)[None]
    ltri = jnp.broadcast_to((row >= col).astype(BF16)[None], (nc, c, c))
    lane = _iota2((nb, c, LANES), 2)

    def per_head(fn):
        return jnp.concatenate([fn(h) for h in range(N_HEADS)], axis=0)

    for h in range(N_HEADS):
        for base, scale in ((0, HEAD_D ** -0.5), (qk_w, 1.0), (2 * qk_w, None)):
            cols = slice(base + h * HEAD_D, base + (h + 1) * HEAD_D)
            x = qkv_ref[0, :, cols].astype(F32)
            xs_ref[8:8 + tb, cols] = x
            cw = cw_ref[:, cols]
            y = x * cw[CONV_K - 1:CONV_K, :]
            for j in range(1, CONV_K):
                y = y + xs_ref[pl.ds(8 - j, tb), cols] * cw[CONV_K - 1 - j:CONV_K - j, :]
            y = _silu(y)
            if scale is not None:
                y = y * (lax.rsqrt(jnp.sum(y * y, axis=-1, keepdims=True) + L2_EPS) * scale)
            qkv_s[:, cols] = y
    xs_ref[0:8, :] = xs_ref[tb:tb + 8, :]

    sm = sm_ref[0]
    beta_all = _sigmoid(sm[:, :LANES]).reshape(nc, c, LANES)
    rate = -jnp.exp(alog_ref[...]) * math.log2(math.e)
    g_all = (rate * _softplus(sm[:, LANES:] + dtb_ref[...])).reshape(nc, c, LANES)
    gc_all = _bmm_01(ltri, g_all)
    q = per_head(lambda h: qkv_s[:, h * HEAD_D:(h + 1) * HEAD_D].reshape(nc, c, HEAD_D))
    k = per_head(lambda h: qkv_s[:, qk_w + h * HEAD_D:qk_w + (h + 1) * HEAD_D].reshape(nc, c, HEAD_D))
    v = per_head(lambda h: qkv_s[:, 2 * qk_w + h * HEAD_D:2 * qk_w + (h + 1) * HEAD_D].reshape(nc, c, HEAD_D))
    beta = per_head(lambda h: beta_all[:, :, h:h + 1])
    gcol = per_head(lambda h: gc_all[:, :, h:h + 1])
    g_hi = gcol.astype(BF16).astype(F32)
    g_mid = (gcol - g_hi).astype(BF16).astype(F32)
    g_lo = gcol - g_hi - g_mid
    pieces = jnp.where(lane == 0, g_hi, jnp.where(lane == 1, g_mid, jnp.where(lane == 2, g_lo, 0.0)))
    lane1 = _iota2((1, 1, LANES), 2)
    lhs = pieces + jnp.where(jnp.logical_and(lane1 >= 3, lane1 < 6), 1.0, 0.0)
    rhs_g = jnp.where(lane1 < 3, 1.0, 0.0) - pltpu.roll(pieces, 3, axis=2)
    decay = jnp.exp2(jnp.where(causal, _bmm_nt(lhs, rhs_g), -jnp.inf))
    kb = k * beta
    a = jnp.where(strict, _bmm_nt(kb, k) * decay, 0.0)
    d = jnp.where(same16, a, 0.0)
    x_inv = eye - d
    dp = d
    for _ in range(3):
        dp = _bmm(dp, dp)
        x_inv = x_inv + _bmm(x_inv, dp)
    e = jnp.where(jnp.logical_and(same32, jnp.logical_not(same16)), a, 0.0)
    x_inv = x_inv - _bmm(x_inv, _bmm(e, x_inv))
    f = jnp.where(same32, 0.0, a)
    x_inv = x_inv - _bmm(x_inv, _bmm(f, x_inv))
    wu = _bmm(x_inv, jnp.concatenate([kb * jnp.exp2(gcol), v * beta], axis=2))
    attn = _bmm_nt(q, k) * decay
    g_last = gcol[:, c - 1:c, :]
    kd = k * jnp.exp2(g_last - gcol)
    mq = _bmm_tn(kd, wu)
    aw = _bmm(attn, wu)
    m_s[...] = mq[:, :, :HEAD_D].astype(BF16).reshape(N_HEADS, nc, HEAD_D, HEAD_D)
    sq_s[...] = mq[:, :, HEAD_D:].reshape(N_HEADS, nc, HEAD_D, HEAD_D)
    qe_s[...] = (q * jnp.exp2(gcol) - aw[:, :, :HEAD_D]).astype(BF16).reshape(N_HEADS, nc, c, HEAD_D)
    dec_s[...] = jnp.broadcast_to(jnp.exp2(g_last), (nb, 1, HEAD_D)).reshape(N_HEADS, nc, 1, HEAD_D)
    for h in range(N_HEADS):
        o_s[:, h * HEAD_D:(h + 1) * HEAD_D] = aw[h * nc:(h + 1) * nc, :, HEAD_D:].reshape(tb, HEAD_D)

    for ci in range(nc):
        for h in range(N_HEADS):
            hs = slice(h * HEAD_D, (h + 1) * HEAD_D)
            s_h = st_ref[h]
            s_b = s_h.astype(BF16)
            o_s[ci * c:(ci + 1) * c, hs] += jnp.dot(qe_s[h, ci], s_b, preferred_element_type=F32)
            st_ref[h] = (s_h * dec_s[h, ci] - jnp.dot(m_s[h, ci], s_b, preferred_element_type=F32)
                         + sq_s[h, ci])

    o_ref[0] = _head_rms_gate(o_s[...], z_ref[0].astype(F32), nw_ref[...], HEAD_D).astype(o_ref.dtype)


def _gdn(proj, small, conv_w, a_log, dt_bias, norm_w, *, tb):
    b, t, _ = proj.shape
    conv_cols = 3 * N_HEADS * HEAD_D
    mix_w = N_HEADS * HEAD_D
    return pl.pallas_call(
        functools.partial(_gdn_kernel, tb=tb),
        grid=(b, t // tb),
        in_specs=[
            pl.BlockSpec((1, tb, conv_cols), lambda i, j: (i, j, 0)),
            pl.BlockSpec((1, tb, mix_w), lambda i, j: (i, j, conv_cols // mix_w)),
            pl.BlockSpec((1, tb, 2 * LANES), lambda i, j: (i, j, 0)),
            pl.BlockSpec((CONV_K, conv_cols), lambda i, j: (0, 0)),
            pl.BlockSpec((1, LANES), lambda i, j: (0, 0)),
            pl.BlockSpec((1, LANES), lambda i, j: (0, 0)),
            pl.BlockSpec((1, HEAD_D), lambda i, j: (0, 0)),
        ],
        out_specs=pl.BlockSpec((1, tb, mix_w), lambda i, j: (i, j, 0)),
        out_shape=jax.ShapeDtypeStruct((b, t, mix_w), BF16),
        scratch_shapes=[
            pltpu.VMEM((tb + 8, conv_cols), F32),
            pltpu.VMEM((tb, conv_cols), F32),
            pltpu.VMEM((tb, mix_w), F32),
            pltpu.VMEM((N_HEADS, tb // CHUNK, HEAD_D, HEAD_D), BF16),
            pltpu.VMEM((N_HEADS, tb // CHUNK, HEAD_D, HEAD_D), F32),
            pltpu.VMEM((N_HEADS, tb // CHUNK, CHUNK, HEAD_D), BF16),
            pltpu.VMEM((N_HEADS, tb // CHUNK, 1, HEAD_D), F32),
            pltpu.VMEM((N_HEADS, HEAD_D, HEAD_D), F32),
        ],
        compiler_params=pltpu.CompilerParams(
            dimension_semantics=("parallel", "arbitrary"), vmem_limit_bytes=VMEM_LIMIT),
        name="gdn",
    )(proj, proj, small, conv_w, a_log, dt_bias, norm_w)


def _hgrn_kernel(q_ref, f_ref, i_ref, g_ref, lb_ref, nw_ref, o_ref,
                 q_s, k_s, lf_s, i_s, o_s, st_ref, *, tb, layer):
    @pl.when(pl.program_id(1) == 0)
    def _():
        st_ref[...] = jnp.zeros(st_ref.shape, F32)

    lbl = lb_ref[...]
    e_lb = jnp.exp(lbl - jnp.max(lbl, axis=0, keepdims=True))
    lb = jnp.sum(e_lb[:layer + 1], axis=0, keepdims=True) / jnp.sum(e_lb, axis=0, keepdims=True)

    f = lb + (1.0 - lb) * _sigmoid(f_ref[0].astype(F32))
    k_s[...] = 1.0 - f
    lf_s[...] = jnp.log2(f)
    q_s[...] = _silu(q_ref[0].astype(F32)) * (HEAD_D ** -0.5)
    i_s[...] = i_ref[0].astype(F32)

    c = CHUNK
    nc = tb // c
    blk = 8
    row = _iota2((c, c), 0)
    col = _iota2((c, c), 1)
    ltri = jnp.broadcast_to((row >= col).astype(BF16)[None], (nc, c, c))
    level_masks = {}
    for m in (32, 16, 8):
        sh = int(math.log2(2 * m))
        level_masks[m] = jnp.logical_and(
            (row >> sh) == (col >> sh),
            jnp.logical_and((row & (2 * m - 1)) >= m, (col & (2 * m - 1)) < m))[None]
    nb = N_HEADS * nc
    sub = _iota2((nb * c // blk, blk, HEAD_D), 1)

    def per_head(fn):
        return jnp.concatenate([fn(slice(h * HEAD_D, (h + 1) * HEAD_D)) for h in range(N_HEADS)], axis=0)

    b_all = _bmm_01(ltri, lf_s[...].reshape(nc, c, N_HEADS * HEAD_D))
    b = per_head(lambda sl: b_all[:, :, sl])
    q = per_head(lambda sl: q_s[:, sl].reshape(nc, c, HEAD_D))
    k = per_head(lambda sl: k_s[:, sl].reshape(nc, c, HEAD_D))
    iv = per_head(lambda sl: i_s[:, sl].reshape(nc, c, HEAD_D))
    attn = jnp.zeros((nb, c, c), F32)
    for m in (32, 16, 8):
        b_m = b.reshape(nb * c // (2 * m), 2 * m, HEAD_D)
        ref = jnp.broadcast_to(b_m[:, m:m + 1, :], b_m.shape).reshape(nb, c, HEAD_D)
        e = jnp.exp2(-jnp.abs(b - ref))
        attn = attn + jnp.where(level_masks[m], _bmm_nt(q * e, k * e), 0.0)
    o = _bmm(attn, iv)
    qb, kb, bb, ib = (a.reshape(nb * c // blk, blk, HEAD_D) for a in (q, k, b, iv))
    ob = jnp.zeros(qb.shape, F32)
    for s in range(blk):
        dec = jnp.exp2(jnp.where(sub >= s, bb - bb[:, s:s + 1, :], -jnp.inf))
        a_col = jnp.sum(dec * qb * kb[:, s:s + 1, :], axis=-1, keepdims=True)
        ob = ob + a_col * ib[:, s:s + 1, :]
    o = o + ob.reshape(nb, c, HEAD_D)
    b_last = b[:, c - 1:c, :]
    q_in = (q * jnp.exp2(b)).astype(BF16)
    upd = _bmm_tn(iv, k * jnp.exp2(b_last - b))
    keep = jnp.exp2(b_last)
    for h in range(N_HEADS):
        s_t = st_ref[h]
        outs = []
        for ci in range(h * nc, (h + 1) * nc):
            outs.append(o[ci] + _mm_nt(q_in[ci], s_t))
            s_t = s_t * keep[ci] + upd[ci]
        st_ref[h] = s_t
        o_s[:, h * HEAD_D:(h + 1) * HEAD_D] = jnp.concatenate(outs, axis=0)

    o_ref[0] = _head_rms_gate(o_s[...], g_ref[0].astype(F32), nw_ref[...], HEAD_D).astype(o_ref.dtype)


def _hgrn(proj, lb_logits, norm_w, *, tb, layer, col0):
    b, t, _ = proj.shape
    w = N_HEADS * HEAD_D
    specs = [pl.BlockSpec((1, tb, w), functools.partial(lambda i, j, off: (i, j, off), off=col0 + n))
             for n in range(4)]
    return pl.pallas_call(
        functools.partial(_hgrn_kernel, tb=tb, layer=layer),
        grid=(b, t // tb),
        in_specs=specs + [
            pl.BlockSpec(lb_logits.shape, lambda i, j: (0, 0)),
            pl.BlockSpec((1, HEAD_D), lambda i, j: (0, 0)),
        ],
        out_specs=pl.BlockSpec((1, tb, w), lambda i, j: (i, j, 0)),
        out_shape=jax.ShapeDtypeStruct((b, t, w), BF16),
        scratch_shapes=[pltpu.VMEM((tb, w), F32)] * 5 + [pltpu.VMEM((N_HEADS, HEAD_D, HEAD_D), F32)],
        compiler_params=pltpu.CompilerParams(
            dimension_semantics=("parallel", "arbitrary"), vmem_limit_bytes=VMEM_LIMIT),
        name="hgrn2",
    )(proj, proj, proj, proj, lb_logits, norm_w)


def _ret_kernel(q_ref, k_ref, v_ref, g_ref, cos_ref, sin_ref, nw_ref, o_ref, dmat_s, st_ref, *, tc):
    log_gammas = [math.log(1.0 - 2.0 ** (-5.0 - h)) for h in range(N_HEADS)]

    @pl.when(pl.program_id(1) == 0)
    def _():
        st_ref[...] = jnp.zeros(st_ref.shape, F32)
        rel = (_iota2((tc, tc), 0) - _iota2((tc, tc), 1)).astype(F32)
        for h in range(N_HEADS):
            dmat_s[h] = jnp.where(rel >= 0, jnp.exp(jnp.maximum(rel, 0.0) * log_gammas[h]), 0.0)

    pos = _iota2((tc, 1), 0).astype(F32)
    cos = cos_ref[...]
    sin = sin_ref[...]
    half = RET_DK // 2

    def rot(x):
        x1, x2 = x[:, :half], x[:, half:]
        return jnp.concatenate([x1 * cos - x2 * sin, x1 * sin + x2 * cos], axis=1)

    for h in range(N_HEADS):
        lg = log_gammas[h]
        qk_cols = slice(h * RET_DK, (h + 1) * RET_DK)
        v_cols = slice(h * RET_DV, (h + 1) * RET_DV)
        q = rot(q_ref[0, :, qk_cols].astype(F32))
        k = rot(k_ref[0, :, qk_cols].astype(F32)) * (RET_DK ** -0.5)
        v = v_ref[0, :, v_cols]
        s = st_ref[h]
        attn = _mm_nt(q, k) * dmat_s[h]
        o = _mm(q * jnp.exp((pos + 1.0) * lg), s) + _mm(attn, v)
        st_ref[h] = s * math.exp(tc * lg) + _mm_tn(k * jnp.exp((tc - 1.0 - pos) * lg), v)
        ms = jnp.mean(o * o, axis=-1, keepdims=True)
        o_ref[0, :, v_cols] = (o * lax.rsqrt(ms + RMS_EPS) * nw_ref[...]
                               * _silu(g_ref[0, :, v_cols].astype(F32))).astype(o_ref.dtype)


def _retention(proj, cos, sin, norm_w, *, tc):
    b, t, _ = proj.shape
    qk_w = N_HEADS * RET_DK
    v_w = N_HEADS * RET_DV
    return pl.pallas_call(
        functools.partial(_ret_kernel, tc=tc),
        grid=(b, t // tc),
        in_specs=[
            pl.BlockSpec((1, tc, qk_w), lambda i, j: (i, j, 0)),
            pl.BlockSpec((1, tc, qk_w), lambda i, j: (i, j, 1)),
            pl.BlockSpec((1, tc, v_w), lambda i, j: (i, j, 1)),
            pl.BlockSpec((1, tc, v_w), lambda i, j: (i, j, 2)),
            pl.BlockSpec((tc, RET_DK // 2), lambda i, j: (j, 0)),
            pl.BlockSpec((tc, RET_DK // 2), lambda i, j: (j, 0)),
            pl.BlockSpec((1, RET_DV), lambda i, j: (0, 0)),
        ],
        out_specs=pl.BlockSpec((1, tc, v_w), lambda i, j: (i, j, 0)),
        out_shape=jax.ShapeDtypeStruct((b, t, v_w), BF16),
        scratch_shapes=[pltpu.VMEM((N_HEADS, tc, tc), F32), pltpu.VMEM((N_HEADS, RET_DK, RET_DV), F32)],
        compiler_params=pltpu.CompilerParams(
            dimension_semantics=("parallel", "arbitrary"), vmem_limit_bytes=VMEM_LIMIT),
        name="retention",
    )(proj, proj, proj, proj, cos, sin, norm_w)


def _route_gates(logits_t):
    cl = [logits_t[g:g + 1, :] for g in range(N_GROUPS)]
    cmax = functools.reduce(jnp.maximum, cl)
    denom = sum(jnp.exp(x - cmax) for x in cl)
    g_prob = 1.0 / denom
    g_idx = jnp.full(cmax.shape, N_GROUPS - 1, jnp.int32)
    for g in range(N_GROUPS - 2, -1, -1):
        g_idx = jnp.where(cl[g] == cmax, g, g_idx)
    def fine_row(g, j):
        r = N_GROUPS + g * EXPERTS_PER_GROUP + j
        return logits_t[r:r + 1, :]

    fl = []
    for j in range(EXPERTS_PER_GROUP):
        x = fine_row(N_GROUPS - 1, j)
        for g in range(N_GROUPS - 2, -1, -1):
            x = jnp.where(g_idx == g, fine_row(g, j), x)
        fl.append(x)
    m1 = functools.reduce(jnp.maximum, fl)
    i1 = jnp.full(m1.shape, EXPERTS_PER_GROUP - 1, jnp.int32)
    for j in range(EXPERTS_PER_GROUP - 2, -1, -1):
        i1 = jnp.where(fl[j] == m1, j, i1)
    rest = [jnp.where(i1 == j, -jnp.inf, fl[j]) for j in range(EXPERTS_PER_GROUP)]
    m2 = functools.reduce(jnp.maximum, rest)
    i2 = jnp.full(m2.shape, EXPERTS_PER_GROUP - 1, jnp.int32)
    for j in range(EXPERTS_PER_GROUP - 2, -1, -1):
        i2 = jnp.where(jnp.logical_and(rest[j] == m2, i1 != j), j, i2)
    e2 = jnp.exp(m2 - m1)
    w1 = g_prob / (1.0 + e2)
    w2 = g_prob * e2 / (1.0 + e2)
    local = [jnp.where(i1 == j, w1, 0.0) + jnp.where(i2 == j, w2, 0.0) for j in range(EXPERTS_PER_GROUP)]
    return g_idx, local


def _outproj_kernel(a_ref, b_ref, wa_ref, wb_ref, x_ref, nw_ref, rw_ref, rb_ref,
                    x1_ref, h_ref, gk_ref, krow_ref, cnt_ref):
    x1 = (x_ref[...]
          + jnp.dot(a_ref[...], wa_ref[...], preferred_element_type=F32)
          + jnp.dot(b_ref[...], wb_ref[...], preferred_element_type=F32))
    x1_ref[...] = x1
    ms = jnp.mean(x1 * x1, axis=-1, keepdims=True)
    h = x1 * lax.rsqrt(ms + RMS_EPS) * nw_ref[...]
    h_ref[...] = h.astype(BF16)
    rw = rw_ref[...]
    hh = h.astype(BF16)
    hl = (h - hh.astype(F32)).astype(BF16)
    wh = rw.astype(BF16)
    wl = (rw - wh.astype(F32)).astype(BF16)
    hi_part = jnp.dot(hh, jnp.concatenate([wh, wl], axis=1), preferred_element_type=F32)
    logits = (hi_part[:, :LANES] + hi_part[:, LANES:]
              + jnp.dot(hl, wh, preferred_element_type=F32)) + rb_ref[...]
    g_idx, local = _route_gates(logits.T)
    tm = g_idx.shape[1]
    mem = (_iota2((8, tm), 0) == g_idx).astype(BF16)
    before = (_iota2((MOE_SUB, MOE_SUB), 0) < _iota2((MOE_SUB, MOE_SUB), 1)).astype(BF16)
    earlier = jnp.concatenate(
        [jnp.dot(mem[:, s:s + MOE_SUB], before, preferred_element_type=F32) for s in range(0, tm, MOE_SUB)],
        axis=1)
    rank = jnp.sum(mem.astype(F32) * earlier, axis=0, keepdims=True)
    key = g_idx.astype(F32) * MOE_KEY_STRIDE + rank
    krow_ref[...] = key
    sub_shift = int(math.log2(MOE_SUB))
    sub_sel = ((_iota2((tm, LANES), 0) >> sub_shift) == _iota2((tm, LANES), 1)).astype(BF16)
    cnt_ref[...] = jnp.dot(mem, sub_sel, preferred_element_type=F32)
    rows = jnp.concatenate([key] + local + [jnp.zeros((LANES - 1 - EXPERTS_PER_GROUP, tm), F32)], axis=0)
    gk_ref[...] = rows.T


def _outproj_route(a, b, a_blk, b_blk, wa, wb, x, nw, rw, rb, *, tm, name):
    n, d = x.shape
    ka = wa.shape[0]
    kb = wb.shape[0]
    return pl.pallas_call(
        _outproj_kernel,
        grid=(n // tm,),
        in_specs=[
            pl.BlockSpec((tm, ka), lambda i: (i, a_blk)),
            pl.BlockSpec((tm, kb), lambda i: (i, b_blk)),
            pl.BlockSpec((ka, d), lambda i: (0, 0)),
            pl.BlockSpec((kb, d), lambda i: (0, 0)),
            pl.BlockSpec((tm, d), lambda i: (i, 0)),
            pl.BlockSpec((1, d), lambda i: (0, 0)),
            pl.BlockSpec((d, LANES), lambda i: (0, 0)),
            pl.BlockSpec((1, LANES), lambda i: (0, 0)),
        ],
        out_specs=[
            pl.BlockSpec((tm, d), lambda i: (i, 0)),
            pl.BlockSpec((tm, d), lambda i: (i, 0)),
            pl.BlockSpec((tm, LANES), lambda i: (i, 0)),
            pl.BlockSpec((1, tm), lambda i: (0, i)),
            pl.BlockSpec((8, LANES), lambda i: (i, 0)),
        ],
        out_shape=[jax.ShapeDtypeStruct((n, d), F32),
                   jax.ShapeDtypeStruct((n, d), BF16),
                   jax.ShapeDtypeStruct((n, LANES), F32),
                   jax.ShapeDtypeStruct((1, n), F32),
                   jax.ShapeDtypeStruct((8 * (n // tm), LANES), F32)],
        compiler_params=pltpu.CompilerParams(
            dimension_semantics=("parallel",), vmem_limit_bytes=VMEM_LIMIT),
        name=name,
    )(a, b, wa, wb, x, nw, rw, rb)


def _round_up(v, m):
    return ((v + (m - 1)) // m) * m


def _round_up_pow2(v, m):
    return (v + (m - 1)) & ~(m - 1)


def _cdiv_pow2(v, m):
    return lax.shift_right_logical(v + (m - 1), int(math.log2(m)))


def _moe_kernel(cnt_ref, h_ref, krow_ref, gk_ref, wgu_ref, wd_ref, x_ref, fnw_ref, o_ref,
                hc_s, gc_s, y_s, *, final_norm):
    i = pl.program_id(0)
    g = pl.program_id(1)
    tm = h_ref.shape[0]
    n_sub = tm // MOE_SUB

    def segments(grp):
        cnts = [cnt_ref[(i * n_sub + s) * N_GROUPS + grp] for s in range(n_sub)]
        starts = [jnp.int32(0)]
        for s in range(n_sub):
            starts.append(starts[-1] + _round_up_pow2(cnts[s], MOE_ALIGN))
        return cnts, starts

    counts, offs = segments(g)
    total = offs[-1]
    key0 = g.astype(F32) * MOE_KEY_STRIDE
    sub_iota = _iota2((MOE_BLK, MOE_SUB), 0).astype(F32)
    lane_iota = _iota2((MOE_SUB, MOE_BLK), 1).astype(F32)

    def pack(s, w):
        rows = slice(s * MOE_SUB, (s + 1) * MOE_SUB)
        base = key0 + (w * MOE_BLK).astype(F32)
        sel = (krow_ref[:, rows] == sub_iota + base).astype(BF16)
        dst = pl.ds(pl.multiple_of(offs[s] + w * MOE_BLK, MOE_ALIGN), MOE_BLK)
        hc_s[dst, :] = jnp.dot(sel, h_ref[rows, :], preferred_element_type=F32).astype(BF16)
        gc_s[dst, :] = _mm_01(sel, gk_ref[rows, :])

    for s in range(n_sub):
        def pack_more(w, carry, s=s):
            pack(s, w)
            return carry

        lax.fori_loop(1, _cdiv_pow2(counts[s], MOE_BLK), pack_more, 0)
    for s in range(n_sub):
        pack(s, jnp.int32(0))

    tail_rows = MOE_FIRST[-1] + MOE_FIRST_STEP
    tail = pl.ds(pl.multiple_of(total, MOE_ALIGN), tail_rows)
    hc_s[tail, :] = jnp.zeros((tail_rows, hc_s.shape[1]), BF16)
    gc_s[tail, :] = jnp.zeros((tail_rows, LANES), F32)

    def expert_rows(start, rows):
        blk = pl.ds(pl.multiple_of(start, MOE_FIRST_STEP), rows)
        hb = hc_s[blk, :]
        gates = gc_s[blk, :]
        y = jnp.zeros((rows, o_ref.shape[1]), F32)
        for e in range(EXPERTS_PER_GROUP):
            gu = jnp.dot(hb, wgu_ref[0, e], preferred_element_type=F32)
            act = _silu(gu[:, :D_EXPERT]) * gu[:, D_EXPERT:] * gates[:, 1 + e:2 + e]
            y = y + jnp.dot(act.astype(BF16), wd_ref[0, e], preferred_element_type=F32)
        y_s[g, blk, :] = y.astype(BF16)

    first = jnp.clip(_round_up_pow2(total, MOE_FIRST_STEP), MOE_FIRST[0], MOE_FIRST[-1])
    for size in MOE_FIRST:
        @pl.when(first == size)
        def _(size=size):
            expert_rows(0, size)

    n_rest = _cdiv_pow2(jnp.maximum(total - first, 0), MOE_BLK)

    def rest(bi, carry):
        expert_rows(first + bi * MOE_BLK, MOE_BLK)
        return carry

    lax.fori_loop(0, n_rest, rest, 0)
    done = pl.ds(pl.multiple_of(first + n_rest * MOE_BLK, MOE_FIRST_STEP), MOE_BLK)
    y_s[g, done, :] = jnp.zeros((MOE_BLK, y_s.shape[2]), BF16)

    @pl.when(g == N_GROUPS - 1)
    def _():
        segs = [segments(grp) for grp in range(N_GROUPS)]
        for s in range(n_sub):
            rows = slice(s * MOE_SUB, (s + 1) * MOE_SUB)
            key_col = gk_ref[rows, 0:1]
            sel = jnp.concatenate(
                [(key_col == lane_iota + grp * MOE_KEY_STRIDE).astype(BF16) for grp in range(N_GROUPS)], axis=1)
            packed = jnp.concatenate(
                [y_s[grp, pl.ds(pl.multiple_of(segs[grp][1][s], MOE_ALIGN), MOE_BLK), :]
                 for grp in range(N_GROUPS)], axis=0)
            o_ref[rows, :] = x_ref[rows, :] + jnp.dot(sel, packed, preferred_element_type=F32)
            for grp in range(N_GROUPS):
                def more(w, carry, s=s, rows=rows, grp=grp, key_col=key_col):
                    base = grp * MOE_KEY_STRIDE + (w * MOE_BLK).astype(F32)
                    sel_w = (key_col == lane_iota + base).astype(BF16)
                    src = pl.ds(pl.multiple_of(segs[grp][1][s] + w * MOE_BLK, MOE_ALIGN), MOE_BLK)
                    o_ref[rows, :] += jnp.dot(sel_w, y_s[grp, src, :], preferred_element_type=F32)
                    return carry

                lax.fori_loop(1, _cdiv_pow2(segs[grp][0][s], MOE_BLK), more, 0)
        if final_norm:
            y = o_ref[...]
            ms = jnp.mean(y * y, axis=-1, keepdims=True)
            o_ref[...] = y * lax.rsqrt(ms + RMS_EPS) * fnw_ref[...]


def _moe(h, krow, gk, counts, wgu, wd, x, fnw, *, layer, tm, final_norm, name):
    n, d = x.shape
    buf_rows = _round_up(tm + (tm // MOE_SUB) * MOE_ALIGN + MOE_FIRST[-1] + MOE_FIRST_STEP, MOE_BLK)
    grid_spec = pltpu.PrefetchScalarGridSpec(
        num_scalar_prefetch=1,
        grid=(n // tm, N_GROUPS),
        in_specs=[
            pl.BlockSpec((tm, d), lambda i, g, c: (i, 0)),
            pl.BlockSpec((1, tm), lambda i, g, c: (0, i)),
            pl.BlockSpec((tm, LANES), lambda i, g, c: (i, 0)),
            pl.BlockSpec((1, EXPERTS_PER_GROUP, d, 2 * D_EXPERT), lambda i, g, c: (layer, g, 0, 0)),
            pl.BlockSpec((1, EXPERTS_PER_GROUP, D_EXPERT, d), lambda i, g, c: (layer, g, 0, 0)),
            pl.BlockSpec((tm, d), lambda i, g, c: (i, 0)),
            pl.BlockSpec((1, d), lambda i, g, c: (0, 0)),
        ],
        out_specs=pl.BlockSpec((tm, d), lambda i, g, c: (i, 0)),
        scratch_shapes=[pltpu.VMEM((buf_rows, d), BF16),
                        pltpu.VMEM((buf_rows, LANES), F32),
                        pltpu.VMEM((N_GROUPS, buf_rows, d), BF16)],
    )
    return pl.pallas_call(
        functools.partial(_moe_kernel, final_norm=final_norm),
        grid_spec=grid_spec,
        out_shape=jax.ShapeDtypeStruct((n, d), F32),
        compiler_params=pltpu.CompilerParams(
            dimension_semantics=("parallel", "arbitrary"), vmem_limit_bytes=VMEM_LIMIT),
        name=name,
    )(counts, h, krow, gk, wgu, wd, x, fnw)


def _pad_cols(a, width):
    return jnp.pad(a, ((0, 0), (0, width - a.shape[1])))


def _dispatch_counts(cnt, route_tm):
    c = cnt.reshape(-1, 8, LANES)[:, :N_GROUPS, :route_tm // MOE_SUB]
    return jnp.transpose(c, (0, 2, 1)).reshape(-1).astype(jnp.int32)


def _router_params(wc, bc, wf, bf):
    rw = _pad_cols(jnp.concatenate([wc, wf], axis=1), LANES)
    rb = _pad_cols(jnp.concatenate([bc, bf])[None, :], LANES)
    return rw, rb


def kernel(x, norm_mix_w, norm_ffn_w, even_w_in, gdn_conv_w, gdn_a_log, gdn_dt_bias, gdn_norm_w, hgrn_lb_logits, hgrn_norm_w, even_w_out, odd_w_in, ret_norm_w, odd_w_out, router_c_w, router_c_b, router_f_w, router_f_b, moe_w_gate_up, moe_w_down, final_norm_w):
    bsz, seq, d = x.shape
    n = bsz * seq
    xt = x.reshape(n, d)
    mix_w = N_HEADS * HEAD_D
    conv_cols = 3 * mix_w
    gdn_main = conv_cols + mix_w

    w_in = even_w_in[0]
    small0 = gdn_main
    w_main = [w_in[:, :small0].astype(BF16), w_in[:, small0 + 2 * N_HEADS:].astype(BF16)]
    w_small = jnp.concatenate(
        [_pad_cols(w_in[:, small0:small0 + N_HEADS], LANES),
         _pad_cols(w_in[:, small0 + N_HEADS:small0 + 2 * N_HEADS], LANES)], axis=1).astype(BF16)
    proj, small, wgu_b, wd_b, w_out, w_odd, w_out1 = _norm_matmul(
        xt, norm_mix_w[0][None, :], w_main, w_small, tm=512, tn=1024, name="in_proj_even",
        cast=(moe_w_gate_up.reshape(-1, moe_w_gate_up.shape[-1]), moe_w_down.reshape(-1, moe_w_down.shape[-1]),
              even_w_out[0], odd_w_in[0], odd_w_out[0]))
    wgu_b = wgu_b.reshape(moe_w_gate_up.shape)
    wd_b = wd_b.reshape(moe_w_down.shape)
    proj = proj.reshape(bsz, seq, -1)
    small = small.reshape(bsz, seq, -1)
    o_a = _gdn(proj, small, gdn_conv_w[0], _pad_cols(gdn_a_log[0][None, :], LANES),
               _pad_cols(gdn_dt_bias[0][None, :], LANES), gdn_norm_w[0][None, :], tb=512)
    o_b = _hgrn(proj, hgrn_lb_logits, hgrn_norm_w[0][None, :], tb=512, layer=0, col0=gdn_main // mix_w)
    rw, rb = _router_params(router_c_w[0], router_c_b[0], router_f_w[0], router_f_b[0])
    x1, h, gk, krow, cnt = _outproj_route(o_a.reshape(n, mix_w), o_b.reshape(n, mix_w), 0, 0,
                                          w_out[:mix_w], w_out[mix_w:], xt, norm_ffn_w[0][None, :], rw, rb,
                                          tm=ROUTE_TM, name="out_proj_even")
    x2 = _moe(h, krow, gk, _dispatch_counts(cnt, ROUTE_TM), wgu_b, wd_b, x1,
              final_norm_w[None, :], layer=0, tm=MOE_TM, final_norm=False, name="moe0")

    (proj1,) = _norm_matmul(x2, norm_mix_w[1][None, :], [w_odd], None, tm=512, tn=1024, name="in_proj_odd",
                            pair_split=(2 * N_HEADS, RET_DK))
    inv = 1.0 / (ROPE_BASE ** jnp.linspace(0.0, 1.0, RET_DK // 2, dtype=F32))
    ang = jnp.arange(seq, dtype=F32)[:, None] * inv[None, :]
    o_c = _retention(proj1.reshape(bsz, seq, -1), jnp.cos(ang), jnp.sin(ang), ret_norm_w[0][None, :],
                     tc=RET_CHUNK)
    o_c = o_c.reshape(n, -1)
    half = w_out1.shape[0] // 2
    rw, rb = _router_params(router_c_w[1], router_c_b[1], router_f_w[1], router_f_b[1])
    x3, h, gk, krow, cnt = _outproj_route(o_c, o_c, 0, 1, w_out1[:half], w_out1[half:], x2,
                                          norm_ffn_w[1][None, :], rw, rb, tm=ROUTE_TM, name="out_proj_odd")
    out = _moe(h, krow, gk, _dispatch_counts(cnt, ROUTE_TM), wgu_b, wd_b, x3,
               final_norm_w[None, :], layer=1, tm=MOE_TM, final_norm=True, name="moe1")
    return out.reshape(bsz, seq, d)
```

```python
import functools
import math

import jax
import jax.numpy as jnp
from jax import lax
from jax.experimental import pallas as pl
from jax.experimental.pallas import tpu as pltpu

F32 = jnp.float32
BF16 = jnp.bfloat16

D_MODEL = 1024
RMS_EPS = 1e-6
L2_EPS = 1e-6
CHUNK = 64
CONV_K = 4
N_HEADS = 4
HEAD_D = 128
RET_DK = 256
RET_DV = 512
RET_CHUNK = 256
ROPE_BASE = 10000.0
N_GROUPS = 4
EXPERTS_PER_GROUP = 4
N_EXPERTS = 16
D_EXPERT = 256
LANES = 128
ROUTE_TM = 1024
MOE_TM = 1024
MOE_SUB = 256
MOE_BLK = 128
MOE_ALIGN = 16
MOE_KEY_STRIDE = 4096.0
MOE_FIRST_STEP = 32
MOE_FIRST = (256, 288, 320, 352)
VMEM_LIMIT = 56 * 1024 * 1024


def _mm(a, b):
    return jnp.dot(a.astype(BF16), b.astype(BF16), preferred_element_type=F32)


def _mm_nt(a, b):
    return lax.dot_general(a.astype(BF16), b.astype(BF16), (((1,), (1,)), ((), ())),
                           preferred_element_type=F32)


def _mm_tn(a, b):
    return _mm(a.T, b)


def _mm_01(m01, x):
    hi = x.astype(BF16)
    lo = (x - hi.astype(F32)).astype(BF16)
    return (jnp.dot(m01, hi, preferred_element_type=F32)
            + jnp.dot(m01, lo, preferred_element_type=F32))


def _bmm(a, b):
    return jnp.einsum('cik,ckj->cij', a.astype(BF16), b.astype(BF16), preferred_element_type=F32)


def _bmm_nt(a, b):
    return jnp.einsum('cik,cjk->cij', a.astype(BF16), b.astype(BF16), preferred_element_type=F32)


def _bmm_tn(a, b):
    return _bmm(jnp.swapaxes(a, 1, 2), b)


def _bmm_01(m01, x):
    hi = x.astype(BF16)
    lo = (x - hi.astype(F32)).astype(BF16)
    return (jnp.einsum('cik,ckj->cij', m01, hi, preferred_element_type=F32)
            + jnp.einsum('cik,ckj->cij', m01, lo, preferred_element_type=F32))


def _sigmoid(x):
    return 1.0 / (1.0 + jnp.exp2(x * (-math.log2(math.e))))


def _silu(x):
    return x * _sigmoid(x)


def _softplus(x):
    return jnp.maximum(x, 0.0) + jnp.log(1.0 + jnp.exp(-jnp.abs(x)))


def _iota2(shape, dim):
    return lax.broadcasted_iota(jnp.int32, shape, dim)


def _head_rms_gate(o, gate, nw, width):
    outs = []
    for h in range(o.shape[1] // width):
        oh = o[:, h * width:(h + 1) * width]
        ms = jnp.mean(oh * oh, axis=-1, keepdims=True)
        outs.append(oh * lax.rsqrt(ms + RMS_EPS) * nw)
    return jnp.concatenate(outs, axis=1) * _silu(gate)


def _norm_mm_kernel(*refs, tn, n_w, has_small, n_cast, pair_split):
    if pair_split is not None:
        refs, wq_s = refs[:-1], refs[-1]
        n_hd, hd_w = pair_split
        half = hd_w // 2

        @pl.when(pl.program_id(0) == 0)
        def _():
            src = _iota2((hd_w, hd_w), 0)
            dst = _iota2((hd_w, hd_w), 1)
            split = (src == jnp.where(dst < half, 2 * dst, 2 * (dst - half) + 1)).astype(BF16)
            for hd in range(n_hd):
                cols = slice(hd * hd_w, (hd + 1) * hd_w)
                wq_s[:, cols] = jnp.dot(refs[2][:, cols], split, preferred_element_type=F32).astype(BF16)

    x_ref, nw_ref = refs[:2]
    w_refs = refs[2:2 + n_w]
    rest = refs[2 + n_w:]
    if has_small:
        ws_ref, rest = rest[0], rest[1:]
    cast_in, rest = rest[:n_cast], rest[n_cast:]
    o_ref = rest[0]
    if has_small:
        os_ref = rest[1]
    cast_out = rest[len(rest) - n_cast:]
    for src, dst in zip(cast_in, cast_out):
        dst[...] = src[...].astype(dst.dtype)
    x = x_ref[...]
    ms = jnp.mean(x * x, axis=-1, keepdims=True)
    hb = (x * lax.rsqrt(ms + RMS_EPS) * nw_ref[...]).astype(BF16)
    if has_small:
        os_ref[...] = jnp.dot(hb, ws_ref[...], preferred_element_type=F32)
    c0 = 0
    for wi, w_ref in enumerate(w_refs):
        for j in range(w_ref.shape[1] // tn):
            cols = slice(j * tn, (j + 1) * tn)
            reordered = pair_split is not None and wi == 0 and (j + 1) * tn <= n_hd * hd_w
            w_tile = wq_s[:, cols] if reordered else w_ref[:, cols]
            res = jnp.dot(hb, w_tile, preferred_element_type=F32)
            o_ref[:, c0 + j * tn:c0 + (j + 1) * tn] = res.astype(o_ref.dtype)
        c0 += w_ref.shape[1]


def _norm_matmul(x, nw, ws_main, ws, *, tm, tn, name, cast=(), pair_split=None):
    n, d = x.shape
    steps = n // tm
    nout = sum(w.shape[1] for w in ws_main)
    has_small = ws is not None
    in_specs = [
        pl.BlockSpec((tm, d), lambda i: (i, 0)),
        pl.BlockSpec((1, d), lambda i: (0, 0)),
    ] + [pl.BlockSpec(w.shape, lambda i: (0, 0)) for w in ws_main]
    out_specs = [pl.BlockSpec((tm, nout), lambda i: (i, 0))]
    out_shape = [jax.ShapeDtypeStruct((n, nout), BF16)]
    args = [x, nw] + list(ws_main)
    if has_small:
        nsmall = ws.shape[1]
        in_specs.append(pl.BlockSpec((d, nsmall), lambda i: (0, 0)))
        out_specs.append(pl.BlockSpec((tm, nsmall), lambda i: (i, 0)))
        out_shape.append(jax.ShapeDtypeStruct((n, nsmall), F32))
        args.append(ws)
    for a in cast:
        slab = (a.shape[0] // steps, a.shape[1])
        in_specs.append(pl.BlockSpec(slab, lambda i: (i, 0)))
        out_specs.append(pl.BlockSpec(slab, lambda i: (i, 0)))
        out_shape.append(jax.ShapeDtypeStruct(a.shape, BF16))
        args.append(a)
    scratch = []
    if pair_split is not None:
        assert (pair_split[0] * pair_split[1]) % tn == 0
        scratch.append(pltpu.VMEM((d, pair_split[0] * pair_split[1]), BF16))
    return pl.pallas_call(
        functools.partial(_norm_mm_kernel, tn=tn, n_w=len(ws_main), has_small=has_small, n_cast=len(cast),
                          pair_split=pair_split),
        grid=(n // tm,),
        in_specs=in_specs,
        out_specs=out_specs,
        out_shape=out_shape,
        scratch_shapes=scratch,
        compiler_params=pltpu.CompilerParams(
            dimension_semantics=("arbitrary",), vmem_limit_bytes=VMEM_LIMIT),
        name=name,
    )(*args)


def _gdn_kernel(qkv_ref, z_ref, sm_ref, cw_ref, alog_ref, dtb_ref, nw_ref, o_ref,
                xs_ref, qkv_s, o_s, m_s, sq_s, qe_s, dec_s, st_ref, *, tb):
    @pl.when(pl.program_id(1) == 0)
    def _():
        xs_ref[0:8, :] = jnp.zeros((8, xs_ref.shape[1]), F32)
        st_ref[...] = jnp.zeros(st_ref.shape, F32)

    c = CHUNK
    nc = tb // c
    nb = N_HEADS * nc
    qk_w = N_HEADS * HEAD_D
    row = _iota2((c, c), 0)
    col = _iota2((c, c), 1)
    causal = (row >= col)[None]
    strict = (row > col)[None]
    same16 = ((row >> 4) == (col >> 4))[None]
    same32 = ((row >> 5) == (col >> 5))[None]
    eye = (row == col).astype(F32)[None]
    ltri = jnp.broadcast_to((row >= col).astype(BF16)[None], (nc, c, c))
    lane = _iota2((nb, c, LANES), 2)

    def per_head(fn):
        return jnp.concatenate([fn(h) for h in range(N_HEADS)], axis=0)

    for h in range(N_HEADS):
        for base, scale in ((0, HEAD_D ** -0.5), (qk_w, 1.0), (2 * qk_w, None)):
            cols = slice(base + h * HEAD_D, base + (h + 1) * HEAD_D)
            x = qkv_ref[0, :, cols].astype(F32)
            xs_ref[8:8 + tb, cols] = x
            cw = cw_ref[:, cols]
            y = x * cw[CONV_K - 1:CONV_K, :]
            for j in range(1, CONV_K):
                y = y + xs_ref[pl.ds(8 - j, tb), cols] * cw[CONV_K - 1 - j:CONV_K - j, :]
            y = _silu(y)
            if scale is not None:
                y = y * (lax.rsqrt(jnp.sum(y * y, axis=-1, keepdims=True) + L2_EPS) * scale)
            qkv_s[:, cols] = y
    xs_ref[0:8, :] = xs_ref[tb:tb + 8, :]

    sm = sm_ref[0]
    beta_all = _sigmoid(sm[:, :LANES]).reshape(nc, c, LANES)
    rate = -jnp.exp(alog_ref[...]) * math.log2(math.e)
    g_all = (rate * _softplus(sm[:, LANES:] + dtb_ref[...])).reshape(nc, c, LANES)
    gc_all = _bmm_01(ltri, g_all)
    q = per_head(lambda h: qkv_s[:, h * HEAD_D:(h + 1) * HEAD_D].reshape(nc, c, HEAD_D))
    k = per_head(lambda h: qkv_s[:, qk_w + h * HEAD_D:qk_w + (h + 1) * HEAD_D].reshape(nc, c, HEAD_D))
    v = per_head(lambda h: qkv_s[:, 2 * qk_w + h * HEAD_D:2 * qk_w + (h + 1) * HEAD_D].reshape(nc, c, HEAD_D))
    beta = per_head(lambda h: beta_all[:, :, h:h + 1])
    gcol = per_head(lambda h: gc_all[:, :, h:h + 1])
    g_hi = gcol.astype(BF16).astype(F32)
    g_mid = (gcol - g_hi).astype(BF16).astype(F32)
    g_lo = gcol - g_hi - g_mid
    pieces = jnp.where(lane == 0, g_hi, jnp.where(lane == 1, g_mid, jnp.where(lane == 2, g_lo, 0.0)))
    lane1 = _iota2((1, 1, LANES), 2)
    lhs = pieces + jnp.where(jnp.logical_and(lane1 >= 3, lane1 < 6), 1.0, 0.0)
    rhs_g = jnp.where(lane1 < 3, 1.0, 0.0) - pltpu.roll(pieces, 3, axis=2)
    decay = jnp.exp2(jnp.where(causal, _bmm_nt(lhs, rhs_g), -jnp.inf))
    kb = k * beta
    a = jnp.where(strict, _bmm_nt(kb, k) * decay, 0.0)
    d = jnp.where(same16, a, 0.0)
    x_inv = eye - d
    dp = d
    for _ in range(3):
        dp = _bmm(dp, dp)
        x_inv = x_inv + _bmm(x_inv, dp)
    e = jnp.where(jnp.logical_and(same32, jnp.logical_not(same16)), a, 0.0)
    x_inv = x_inv - _bmm(x_inv, _bmm(e, x_inv))
    f = jnp.where(same32, 0.0, a)
    x_inv = x_inv - _bmm(x_inv, _bmm(f, x_inv))
    wu = _bmm(x_inv, jnp.concatenate([kb * jnp.exp2(gcol), v * beta], axis=2))
    attn = _bmm_nt(q, k) * decay
    g_last = gcol[:, c - 1:c, :]
    kd = k * jnp.exp2(g_last - gcol)
    mq = _bmm_tn(kd, wu)
    aw = _bmm(attn, wu)
    m_s[...] = mq[:, :, :HEAD_D].astype(BF16).reshape(N_HEADS, nc, HEAD_D, HEAD_D)
    sq_s[...] = mq[:, :, HEAD_D:].reshape(N_HEADS, nc, HEAD_D, HEAD_D)
    qe_s[...] = (q * jnp.exp2(gcol) - aw[:, :, :HEAD_D]).astype(BF16).reshape(N_HEADS, nc, c, HEAD_D)
    dec_s[...] = jnp.broadcast_to(jnp.exp2(g_last), (nb, 1, HEAD_D)).reshape(N_HEADS, nc, 1, HEAD_D)
    for h in range(N_HEADS):
        o_s[:, h * HEAD_D:(h + 1) * HEAD_D] = aw[h * nc:(h + 1) * nc, :, HEAD_D:].reshape(tb, HEAD_D)

    for ci in range(nc):
        for h in range(N_HEADS):
            hs = slice(h * HEAD_D, (h + 1) * HEAD_D)
            s_h = st_ref[h]
            s_b = s_h.astype(BF16)
            o_s[ci * c:(ci + 1) * c, hs] += jnp.dot(qe_s[h, ci], s_b, preferred_element_type=F32)
            st_ref[h] = (s_h * dec_s[h, ci] - jnp.dot(m_s[h, ci], s_b, preferred_element_type=F32)
                         + sq_s[h, ci])

    o_ref[0] = _head_rms_gate(o_s[...], z_ref[0].astype(F32), nw_ref[...], HEAD_D).astype(o_ref.dtype)


def _gdn(proj, small, conv_w, a_log, dt_bias, norm_w, *, tb):
    b, t, _ = proj.shape
    conv_cols = 3 * N_HEADS * HEAD_D
    mix_w = N_HEADS * HEAD_D
    return pl.pallas_call(
        functools.partial(_gdn_kernel, tb=tb),
        grid=(b, t // tb),
        in_specs=[
            pl.BlockSpec((1, tb, conv_cols), lambda i, j: (i, j, 0)),
            pl.BlockSpec((1, tb, mix_w), lambda i, j: (i, j, conv_cols // mix_w)),
            pl.BlockSpec((1, tb, 2 * LANES), lambda i, j: (i, j, 0)),
            pl.BlockSpec((CONV_K, conv_cols), lambda i, j: (0, 0)),
            pl.BlockSpec((1, LANES), lambda i, j: (0, 0)),
            pl.BlockSpec((1, LANES), lambda i, j: (0, 0)),
            pl.BlockSpec((1, HEAD_D), lambda i, j: (0, 0)),
        ],
        out_specs=pl.BlockSpec((1, tb, mix_w), lambda i, j: (i, j, 0)),
        out_shape=jax.ShapeDtypeStruct((b, t, mix_w), BF16),
        scratch_shapes=[
            pltpu.VMEM((tb + 8, conv_cols), F32),
            pltpu.VMEM((tb, conv_cols), F32),
            pltpu.VMEM((tb, mix_w), F32),
            pltpu.VMEM((N_HEADS, tb // CHUNK, HEAD_D, HEAD_D), BF16),
            pltpu.VMEM((N_HEADS, tb // CHUNK, HEAD_D, HEAD_D), F32),
            pltpu.VMEM((N_HEADS, tb // CHUNK, CHUNK, HEAD_D), BF16),
            pltpu.VMEM((N_HEADS, tb // CHUNK, 1, HEAD_D), F32),
            pltpu.VMEM((N_HEADS, HEAD_D, HEAD_D), F32),
        ],
        compiler_params=pltpu.CompilerParams(
            dimension_semantics=("parallel", "arbitrary"), vmem_limit_bytes=VMEM_LIMIT),
        name="gdn",
    )(proj, proj, small, conv_w, a_log, dt_bias, norm_w)


def _hgrn_kernel(q_ref, f_ref, i_ref, g_ref, lb_ref, nw_ref, o_ref,
                 q_s, k_s, lf_s, i_s, o_s, st_ref, *, tb, layer):
    @pl.when(pl.program_id(1) == 0)
    def _():
        st_ref[...] = jnp.zeros(st_ref.shape, F32)

    lbl = lb_ref[...]
    e_lb = jnp.exp(lbl - jnp.max(lbl, axis=0, keepdims=True))
    lb = jnp.sum(e_lb[:layer + 1], axis=0, keepdims=True) / jnp.sum(e_lb, axis=0, keepdims=True)

    f = lb + (1.0 - lb) * _sigmoid(f_ref[0].astype(F32))
    k_s[...] = 1.0 - f
    lf_s[...] = jnp.log2(f)
    q_s[...] = _silu(q_ref[0].astype(F32)) * (HEAD_D ** -0.5)
    i_s[...] = i_ref[0].astype(F32)

    c = CHUNK
    nc = tb // c
    blk = 8
    row = _iota2((c, c), 0)
    col = _iota2((c, c), 1)
    ltri = jnp.broadcast_to((row >= col).astype(BF16)[None], (nc, c, c))
    level_masks = {}
    for m in (32, 16, 8):
        sh = int(math.log2(2 * m))
        level_masks[m] = jnp.logical_and(
            (row >> sh) == (col >> sh),
            jnp.logical_and((row & (2 * m - 1)) >= m, (col & (2 * m - 1)) < m))[None]
    nb = N_HEADS * nc
    sub = _iota2((nb * c // blk, blk, HEAD_D), 1)

    def per_head(fn):
        return jnp.concatenate([fn(slice(h * HEAD_D, (h + 1) * HEAD_D)) for h in range(N_HEADS)], axis=0)

    b_all = _bmm_01(ltri, lf_s[...].reshape(nc, c, N_HEADS * HEAD_D))
    b = per_head(lambda sl: b_all[:, :, sl])
    q = per_head(lambda sl: q_s[:, sl].reshape(nc, c, HEAD_D))
    k = per_head(lambda sl: k_s[:, sl].reshape(nc, c, HEAD_D))
    iv = per_head(lambda sl: i_s[:, sl].reshape(nc, c, HEAD_D))
    attn = jnp.zeros((nb, c, c), F32)
    for m in (32, 16, 8):
        b_m = b.reshape(nb * c // (2 * m), 2 * m, HEAD_D)
        ref = jnp.broadcast_to(b_m[:, m:m + 1, :], b_m.shape).reshape(nb, c, HEAD_D)
        e = jnp.exp2(-jnp.abs(b - ref))
        attn = attn + jnp.where(level_masks[m], _bmm_nt(q * e, k * e), 0.0)
    o = _bmm(attn, iv)
    qb, kb, bb, ib = (a.reshape(nb * c // blk, blk, HEAD_D) for a in (q, k, b, iv))
    ob = jnp.zeros(qb.shape, F32)
    for s in range(blk):
        dec = jnp.exp2(jnp.where(sub >= s, bb - bb[:, s:s + 1, :], -jnp.inf))
        a_col = jnp.sum(dec * qb * kb[:, s:s + 1, :], axis=-1, keepdims=True)
        ob = ob + a_col * ib[:, s:s + 1, :]
    o = o + ob.reshape(nb, c, HEAD_D)
    b_last = b[:, c - 1:c, :]
    q_in = (q * jnp.exp2(b)).astype(BF16)
    upd = _bmm_tn(iv, k * jnp.exp2(b_last - b))
    keep = jnp.exp2(b_last)
    for h in range(N_HEADS):
        s_t = st_ref[h]
        outs = []
        for ci in range(h * nc, (h + 1) * nc):
            outs.append(o[ci] + _mm_nt(q_in[ci], s_t))
            s_t = s_t * keep[ci] + upd[ci]
        st_ref[h] = s_t
        o_s[:, h * HEAD_D:(h + 1) * HEAD_D] = jnp.concatenate(outs, axis=0)

    o_ref[0] = _head_rms_gate(o_s[...], g_ref[0].astype(F32), nw_ref[...], HEAD_D).astype(o_ref.dtype)


def _hgrn(proj, lb_logits, norm_w, *, tb, layer, col0):
    b, t, _ = proj.shape
    w = N_HEADS * HEAD_D
    specs = [pl.BlockSpec((1, tb, w), functools.partial(lambda i, j, off: (i, j, off), off=col0 + n))
             for n in range(4)]
    return pl.pallas_call(
        functools.partial(_hgrn_kernel, tb=tb, layer=layer),
        grid=(b, t // tb),
        in_specs=specs + [
            pl.BlockSpec(lb_logits.shape, lambda i, j: (0, 0)),
            pl.BlockSpec((1, HEAD_D), lambda i, j: (0, 0)),
        ],
        out_specs=pl.BlockSpec((1, tb, w), lambda i, j: (i, j, 0)),
        out_shape=jax.ShapeDtypeStruct((b, t, w), BF16),
        scratch_shapes=[pltpu.VMEM((tb, w), F32)] * 5 + [pltpu.VMEM((N_HEADS, HEAD_D, HEAD_D), F32)],
        compiler_params=pltpu.CompilerParams(
            dimension_semantics=("parallel", "arbitrary"), vmem_limit_bytes=VMEM_LIMIT),
        name="hgrn2",
    )(proj, proj, proj, proj, lb_logits, norm_w)


def _ret_kernel(q_ref, k_ref, v_ref, g_ref, cos_ref, sin_ref, nw_ref, o_ref, dmat_s, st_ref, *, tc):
    log_gammas = [math.log(1.0 - 2.0 ** (-5.0 - h)) for h in range(N_HEADS)]

    @pl.when(pl.program_id(1) == 0)
    def _():
        st_ref[...] = jnp.zeros(st_ref.shape, F32)
        rel = (_iota2((tc, tc), 0) - _iota2((tc, tc), 1)).astype(F32)
        for h in range(N_HEADS):
            dmat_s[h] = jnp.where(rel >= 0, jnp.exp(jnp.maximum(rel, 0.0) * log_gammas[h]), 0.0)

    pos = _iota2((tc, 1), 0).astype(F32)
    cos = cos_ref[...]
    sin = sin_ref[...]
    half = RET_DK // 2

    def rot(x):
        x1, x2 = x[:, :half], x[:, half:]
        return jnp.concatenate([x1 * cos - x2 * sin, x1 * sin + x2 * cos], axis=1)

    for h in range(N_HEADS):
        lg = log_gammas[h]
        qk_cols = slice(h * RET_DK, (h + 1) * RET_DK)
        v_cols = slice(h * RET_DV, (h + 1) * RET_DV)
        q = rot(q_ref[0, :, qk_cols].astype(F32))
        k = rot(k_ref[0, :, qk_cols].astype(F32)) * (RET_DK ** -0.5)
        v = v_ref[0, :, v_cols]
        s = st_ref[h]
        attn = _mm_nt(q, k) * dmat_s[h]
        o = _mm(q * jnp.exp((pos + 1.0) * lg), s) + _mm(attn, v)
        st_ref[h] = s * math.exp(tc * lg) + _mm_tn(k * jnp.exp((tc - 1.0 - pos) * lg), v)
        ms = jnp.mean(o * o, axis=-1, keepdims=True)
        o_ref[0, :, v_cols] = (o * lax.rsqrt(ms + RMS_EPS) * nw_ref[...]
                               * _silu(g_ref[0, :, v_cols].astype(F32))).astype(o_ref.dtype)


def _retention(proj, cos, sin, norm_w, *, tc):
    b, t, _ = proj.shape
    qk_w = N_HEADS * RET_DK
    v_w = N_HEADS * RET_DV
    return pl.pallas_call(
        functools.partial(_ret_kernel, tc=tc),
        grid=(b, t // tc),
        in_specs=[
            pl.BlockSpec((1, tc, qk_w), lambda i, j: (i, j, 0)),
            pl.BlockSpec((1, tc, qk_w), lambda i, j: (i, j, 1)),
            pl.BlockSpec((1, tc, v_w), lambda i, j: (i, j, 1)),
            pl.BlockSpec((1, tc, v_w), lambda i, j: (i, j, 2)),
            pl.BlockSpec((tc, RET_DK // 2), lambda i, j: (j, 0)),
            pl.BlockSpec((tc, RET_DK // 2), lambda i, j: (j, 0)),
            pl.BlockSpec((1, RET_DV), lambda i, j: (0, 0)),
        ],
        out_specs=pl.BlockSpec((1, tc, v_w), lambda i, j: (i, j, 0)),
        out_shape=jax.ShapeDtypeStruct((b, t, v_w), BF16),
        scratch_shapes=[pltpu.VMEM((N_HEADS, tc, tc), F32), pltpu.VMEM((N_HEADS, RET_DK, RET_DV), F32)],
        compiler_params=pltpu.CompilerParams(
            dimension_semantics=("parallel", "arbitrary"), vmem_limit_bytes=VMEM_LIMIT),
        name="retention",
    )(proj, proj, proj, proj, cos, sin, norm_w)


def _proj_ret_kernel(x_ref, nw_ref, w_ref, cos_ref, sin_ref, rnw_ref, o_ref,
                     wq_s, proj_s, dmat_s, st_ref, *, tn, tc, tiles_per_seq):
    i = pl.program_id(0)
    qk_w = N_HEADS * RET_DK
    v_w = N_HEADS * RET_DV
    half = RET_DK // 2
    log_gammas = [math.log(1.0 - 2.0 ** (-5.0 - h)) for h in range(N_HEADS)]

    @pl.when(i == 0)
    def _():
        src = _iota2((RET_DK, RET_DK), 0)
        dst = _iota2((RET_DK, RET_DK), 1)
        split = (src == jnp.where(dst < half, 2 * dst, 2 * (dst - half) + 1)).astype(BF16)
        for hd in range(2 * N_HEADS):
            cols = slice(hd * RET_DK, (hd + 1) * RET_DK)
            wq_s[:, cols] = jnp.dot(w_ref[:, cols], split, preferred_element_type=F32).astype(BF16)
        rel = (_iota2((tc, tc), 0) - _iota2((tc, tc), 1)).astype(F32)
        for h in range(N_HEADS):
            dmat_s[h] = jnp.where(rel >= 0, jnp.exp(jnp.maximum(rel, 0.0) * log_gammas[h]), 0.0)

    @pl.when(i % tiles_per_seq == 0)
    def _():
        st_ref[...] = jnp.zeros(st_ref.shape, F32)

    x = x_ref[...]
    tm = x.shape[0]
    ms = jnp.mean(x * x, axis=-1, keepdims=True)
    hb = (x * lax.rsqrt(ms + RMS_EPS) * nw_ref[...]).astype(BF16)
    for j in range(w_ref.shape[1] // tn):
        cols = slice(j * tn, (j + 1) * tn)
        w_tile = wq_s[:, cols] if (j + 1) * tn <= 2 * qk_w else w_ref[:, cols]
        proj_s[:, cols] = jnp.dot(hb, w_tile, preferred_element_type=F32).astype(BF16)

    pos = _iota2((tc, 1), 0).astype(F32)
    for ck in range(tm // tc):
        rows = slice(ck * tc, (ck + 1) * tc)
        cos = cos_ref[rows, :]
        sin = sin_ref[rows, :]

        def rot(xh):
            x1, x2 = xh[:, :half], xh[:, half:]
            return jnp.concatenate([x1 * cos - x2 * sin, x1 * sin + x2 * cos], axis=1)

        for h in range(N_HEADS):
            lg = log_gammas[h]
            q = rot(proj_s[rows, h * RET_DK:(h + 1) * RET_DK].astype(F32))
            k = rot(proj_s[rows, qk_w + h * RET_DK:qk_w + (h + 1) * RET_DK].astype(F32)) * (RET_DK ** -0.5)
            v = proj_s[rows, 2 * qk_w + h * RET_DV:2 * qk_w + (h + 1) * RET_DV]
            gate = proj_s[rows, 2 * qk_w + v_w + h * RET_DV:2 * qk_w + v_w + (h + 1) * RET_DV]
            s = st_ref[h]
            attn = _mm_nt(q, k) * dmat_s[h]
            o = _mm(q * jnp.exp((pos + 1.0) * lg), s) + _mm(attn, v)
            st_ref[h] = s * math.exp(tc * lg) + _mm_tn(k * jnp.exp((tc - 1.0 - pos) * lg), v)
            ms_o = jnp.mean(o * o, axis=-1, keepdims=True)
            o_ref[rows, h * RET_DV:(h + 1) * RET_DV] = (o * lax.rsqrt(ms_o + RMS_EPS) * rnw_ref[...]
                                                        * _silu(gate.astype(F32))).astype(o_ref.dtype)


def _proj_retention(x, nw, w, cos, sin, ret_nw, *, tm, tn, tc, tokens_per_seq, name):
    n, d = x.shape
    tiles_per_seq = tokens_per_seq // tm
    v_w = N_HEADS * RET_DV
    return pl.pallas_call(
        functools.partial(_proj_ret_kernel, tn=tn, tc=tc, tiles_per_seq=tiles_per_seq),
        grid=(n // tm,),
        in_specs=[
            pl.BlockSpec((tm, d), lambda i: (i, 0)),
            pl.BlockSpec((1, d), lambda i: (0, 0)),
            pl.BlockSpec(w.shape, lambda i: (0, 0)),
            pl.BlockSpec((tm, RET_DK // 2), lambda i: (i % tiles_per_seq, 0)),
            pl.BlockSpec((tm, RET_DK // 2), lambda i: (i % tiles_per_seq, 0)),
            pl.BlockSpec((1, RET_DV), lambda i: (0, 0)),
        ],
        out_specs=pl.BlockSpec((tm, v_w), lambda i: (i, 0)),
        out_shape=jax.ShapeDtypeStruct((n, v_w), BF16),
        scratch_shapes=[
            pltpu.VMEM((d, 2 * N_HEADS * RET_DK), BF16),
            pltpu.VMEM((tm, w.shape[1]), BF16),
            pltpu.VMEM((N_HEADS, tc, tc), F32),
            pltpu.VMEM((N_HEADS, RET_DK, RET_DV), F32),
        ],
        compiler_params=pltpu.CompilerParams(
            dimension_semantics=("arbitrary",), vmem_limit_bytes=VMEM_LIMIT),
        name=name,
    )(x, nw, w, cos, sin, ret_nw)


def _route_gates(logits_t):
    cl = [logits_t[g:g + 1, :] for g in range(N_GROUPS)]
    cmax = functools.reduce(jnp.maximum, cl)
    denom = sum(jnp.exp(x - cmax) for x in cl)
    g_prob = 1.0 / denom
    g_idx = jnp.full(cmax.shape, N_GROUPS - 1, jnp.int32)
    for g in range(N_GROUPS - 2, -1, -1):
        g_idx = jnp.where(cl[g] == cmax, g, g_idx)
    def fine_row(g, j):
        r = N_GROUPS + g * EXPERTS_PER_GROUP + j
        return logits_t[r:r + 1, :]

    fl = []
    for j in range(EXPERTS_PER_GROUP):
        x = fine_row(N_GROUPS - 1, j)
        for g in range(N_GROUPS - 2, -1, -1):
            x = jnp.where(g_idx == g, fine_row(g, j), x)
        fl.append(x)
    m1 = functools.reduce(jnp.maximum, fl)
    i1 = jnp.full(m1.shape, EXPERTS_PER_GROUP - 1, jnp.int32)
    for j in range(EXPERTS_PER_GROUP - 2, -1, -1):
        i1 = jnp.where(fl[j] == m1, j, i1)
    rest = [jnp.where(i1 == j, -jnp.inf, fl[j]) for j in range(EXPERTS_PER_GROUP)]
    m2 = functools.reduce(jnp.maximum, rest)
    i2 = jnp.full(m2.shape, EXPERTS_PER_GROUP - 1, jnp.int32)
    for j in range(EXPERTS_PER_GROUP - 2, -1, -1):
        i2 = jnp.where(jnp.logical_and(rest[j] == m2, i1 != j), j, i2)
    e2 = jnp.exp(m2 - m1)
    w1 = g_prob / (1.0 + e2)
    w2 = g_prob * e2 / (1.0 + e2)
    local = [jnp.where(i1 == j, w1, 0.0) + jnp.where(i2 == j, w2, 0.0) for j in range(EXPERTS_PER_GROUP)]
    return g_idx, local


def _outproj_kernel(a_ref, b_ref, wa_ref, wb_ref, x_ref, nw_ref, rw_ref, rb_ref,
                    x1_ref, h_ref, gk_ref, krow_ref, cnt_ref):
    x1 = (x_ref[...]
          + jnp.dot(a_ref[...], wa_ref[...], preferred_element_type=F32)
          + jnp.dot(b_ref[...], wb_ref[...], preferred_element_type=F32))
    x1_ref[...] = x1
    ms = jnp.mean(x1 * x1, axis=-1, keepdims=True)
    h = x1 * lax.rsqrt(ms + RMS_EPS) * nw_ref[...]
    h_ref[...] = h.astype(BF16)
    rw = rw_ref[...]
    hh = h.astype(BF16)
    hl = (h - hh.astype(F32)).astype(BF16)
    wh = rw.astype(BF16)
    wl = (rw - wh.astype(F32)).astype(BF16)
    hi_part = jnp.dot(hh, jnp.concatenate([wh, wl], axis=1), preferred_element_type=F32)
    logits = (hi_part[:, :LANES] + hi_part[:, LANES:]
              + jnp.dot(hl, wh, preferred_element_type=F32)) + rb_ref[...]
    g_idx, local = _route_gates(logits.T)
    tm = g_idx.shape[1]
    mem = (_iota2((8, tm), 0) == g_idx).astype(BF16)
    before = (_iota2((MOE_SUB, MOE_SUB), 0) < _iota2((MOE_SUB, MOE_SUB), 1)).astype(BF16)
    earlier = jnp.concatenate(
        [jnp.dot(mem[:, s:s + MOE_SUB], before, preferred_element_type=F32) for s in range(0, tm, MOE_SUB)],
        axis=1)
    rank = jnp.sum(mem.astype(F32) * earlier, axis=0, keepdims=True)
    key = g_idx.astype(F32) * MOE_KEY_STRIDE + rank
    krow_ref[...] = key
    sub_shift = int(math.log2(MOE_SUB))
    sub_sel = ((_iota2((tm, LANES), 0) >> sub_shift) == _iota2((tm, LANES), 1)).astype(BF16)
    cnt_ref[...] = jnp.dot(mem, sub_sel, preferred_element_type=F32)
    rows = jnp.concatenate([key] + local + [jnp.zeros((LANES - 1 - EXPERTS_PER_GROUP, tm), F32)], axis=0)
    gk_ref[...] = rows.T


def _outproj_route(a, b, a_blk, b_blk, wa, wb, x, nw, rw, rb, *, tm, name):
    n, d = x.shape
    ka = wa.shape[0]
    kb = wb.shape[0]
    return pl.pallas_call(
        _outproj_kernel,
        grid=(n // tm,),
        in_specs=[
            pl.BlockSpec((tm, ka), lambda i: (i, a_blk)),
            pl.BlockSpec((tm, kb), lambda i: (i, b_blk)),
            pl.BlockSpec((ka, d), lambda i: (0, 0)),
            pl.BlockSpec((kb, d), lambda i: (0, 0)),
            pl.BlockSpec((tm, d), lambda i: (i, 0)),
            pl.BlockSpec((1, d), lambda i: (0, 0)),
            pl.BlockSpec((d, LANES), lambda i: (0, 0)),
            pl.BlockSpec((1, LANES), lambda i: (0, 0)),
        ],
        out_specs=[
            pl.BlockSpec((tm, d), lambda i: (i, 0)),
            pl.BlockSpec((tm, d), lambda i: (i, 0)),
            pl.BlockSpec((tm, LANES), lambda i: (i, 0)),
            pl.BlockSpec((1, tm), lambda i: (0, i)),
            pl.BlockSpec((8, LANES), lambda i: (i, 0)),
        ],
        out_shape=[jax.ShapeDtypeStruct((n, d), F32),
                   jax.ShapeDtypeStruct((n, d), BF16),
                   jax.ShapeDtypeStruct((n, LANES), F32),
                   jax.ShapeDtypeStruct((1, n), F32),
                   jax.ShapeDtypeStruct((8 * (n // tm), LANES), F32)],
        compiler_params=pltpu.CompilerParams(
            dimension_semantics=("parallel",), vmem_limit_bytes=VMEM_LIMIT),
        name=name,
    )(a, b, wa, wb, x, nw, rw, rb)


def _round_up(v, m):
    return ((v + (m - 1)) // m) * m


def _round_up_pow2(v, m):
    return (v + (m - 1)) & ~(m - 1)


def _cdiv_pow2(v, m):
    return lax.shift_right_logical(v + (m - 1), int(math.log2(m)))


def _moe_kernel(cnt_ref, h_ref, krow_ref, gk_ref, wgu_ref, wd_ref, x_ref, fnw_ref, o_ref,
                hc_s, gc_s, y_s, *, final_norm):
    i = pl.program_id(0)
    g = pl.program_id(1)
    tm = h_ref.shape[0]
    n_sub = tm // MOE_SUB

    def segments(grp):
        cnts = [cnt_ref[(i * n_sub + s) * N_GROUPS + grp] for s in range(n_sub)]
        starts = [jnp.int32(0)]
        for s in range(n_sub):
            starts.append(starts[-1] + _round_up_pow2(cnts[s], MOE_ALIGN))
        return cnts, starts

    counts, offs = segments(g)
    total = offs[-1]
    key0 = g.astype(F32) * MOE_KEY_STRIDE
    sub_iota = _iota2((MOE_BLK, MOE_SUB), 0).astype(F32)
    lane_iota = _iota2((MOE_SUB, MOE_BLK), 1).astype(F32)

    def pack(s, w):
        rows = slice(s * MOE_SUB, (s + 1) * MOE_SUB)
        base = key0 + (w * MOE_BLK).astype(F32)
        sel = (krow_ref[:, rows] == sub_iota + base).astype(BF16)
        dst = pl.ds(pl.multiple_of(offs[s] + w * MOE_BLK, MOE_ALIGN), MOE_BLK)
        hc_s[dst, :] = jnp.dot(sel, h_ref[rows, :], preferred_element_type=F32).astype(BF16)
        gc_s[dst, :] = _mm_01(sel, gk_ref[rows, :])

    for s in range(n_sub):
        def pack_more(w, carry, s=s):
            pack(s, w)
            return carry

        lax.fori_loop(1, _cdiv_pow2(counts[s], MOE_BLK), pack_more, 0)
    for s in range(n_sub):
        pack(s, jnp.int32(0))

    tail_rows = MOE_FIRST[-1] + MOE_FIRST_STEP
    tail = pl.ds(pl.multiple_of(total, MOE_ALIGN), tail_rows)
    hc_s[tail, :] = jnp.zeros((tail_rows, hc_s.shape[1]), BF16)
    gc_s[tail, :] = jnp.zeros((tail_rows, LANES), F32)

    def expert_rows(start, rows):
        blk = pl.ds(pl.multiple_of(start, MOE_FIRST_STEP), rows)
        hb = hc_s[blk, :]
        gates = gc_s[blk, :]
        y = jnp.zeros((rows, o_ref.shape[1]), F32)
        for e in range(EXPERTS_PER_GROUP):
            gu = jnp.dot(hb, wgu_ref[0, e], preferred_element_type=F32)
            act = _silu(gu[:, :D_EXPERT]) * gu[:, D_EXPERT:] * gates[:, 1 + e:2 + e]
            y = y + jnp.dot(act.astype(BF16), wd_ref[0, e], preferred_element_type=F32)
        y_s[g, blk, :] = y.astype(BF16)

    first = jnp.clip(_round_up_pow2(total, MOE_FIRST_STEP), MOE_FIRST[0], MOE_FIRST[-1])
    for size in MOE_FIRST:
        @pl.when(first == size)
        def _(size=size):
            expert_rows(0, size)

    n_rest = _cdiv_pow2(jnp.maximum(total - first, 0), MOE_BLK)

    def rest(bi, carry):
        expert_rows(first + bi * MOE_BLK, MOE_BLK)
        return carry

    lax.fori_loop(0, n_rest, rest, 0)
    done = pl.ds(pl.multiple_of(first + n_rest * MOE_BLK, MOE_FIRST_STEP), MOE_BLK)
    y_s[g, done, :] = jnp.zeros((MOE_BLK, y_s.shape[2]), BF16)

    @pl.when(g == N_GROUPS - 1)
    def _():
        segs = [segments(grp) for grp in range(N_GROUPS)]
        for s in range(n_sub):
            rows = slice(s * MOE_SUB, (s + 1) * MOE_SUB)
            key_col = gk_ref[rows, 0:1]
            sel = jnp.concatenate(
                [(key_col == lane_iota + grp * MOE_KEY_STRIDE).astype(BF16) for grp in range(N_GROUPS)], axis=1)
            packed = jnp.concatenate(
                [y_s[grp, pl.ds(pl.multiple_of(segs[grp][1][s], MOE_ALIGN), MOE_BLK), :]
                 for grp in range(N_GROUPS)], axis=0)
            o_ref[rows, :] = x_ref[rows, :] + jnp.dot(sel, packed, preferred_element_type=F32)
            for grp in range(N_GROUPS):
                def more(w, carry, s=s, rows=rows, grp=grp, key_col=key_col):
                    base = grp * MOE_KEY_STRIDE + (w * MOE_BLK).astype(F32)
                    sel_w = (key_col == lane_iota + base).astype(BF16)
                    src = pl.ds(pl.multiple_of(segs[grp][1][s] + w * MOE_BLK, MOE_ALIGN), MOE_BLK)
                    o_ref[rows, :] += jnp.dot(sel_w, y_s[grp, src, :], preferred_element_type=F32)
                    return carry

                lax.fori_loop(1, _cdiv_pow2(segs[grp][0][s], MOE_BLK), more, 0)
        if final_norm:
            y = o_ref[...]
            ms = jnp.mean(y * y, axis=-1, keepdims=True)
            o_ref[...] = y * lax.rsqrt(ms + RMS_EPS) * fnw_ref[...]


def _moe(h, krow, gk, counts, wgu, wd, x, fnw, *, layer, tm, final_norm, name):
    n, d = x.shape
    buf_rows = _round_up(tm + (tm // MOE_SUB) * MOE_ALIGN + MOE_FIRST[-1] + MOE_FIRST_STEP, MOE_BLK)
    grid_spec = pltpu.PrefetchScalarGridSpec(
        num_scalar_prefetch=1,
        grid=(n // tm, N_GROUPS),
        in_specs=[
            pl.BlockSpec((tm, d), lambda i, g, c: (i, 0)),
            pl.BlockSpec((1, tm), lambda i, g, c: (0, i)),
            pl.BlockSpec((tm, LANES), lambda i, g, c: (i, 0)),
            pl.BlockSpec((1, EXPERTS_PER_GROUP, d, 2 * D_EXPERT), lambda i, g, c: (layer, g, 0, 0)),
            pl.BlockSpec((1, EXPERTS_PER_GROUP, D_EXPERT, d), lambda i, g, c: (layer, g, 0, 0)),
            pl.BlockSpec((tm, d), lambda i, g, c: (i, 0)),
            pl.BlockSpec((1, d), lambda i, g, c: (0, 0)),
        ],
        out_specs=pl.BlockSpec((tm, d), lambda i, g, c: (i, 0)),
        scratch_shapes=[pltpu.VMEM((buf_rows, d), BF16),
                        pltpu.VMEM((buf_rows, LANES), F32),
                        pltpu.VMEM((N_GROUPS, buf_rows, d), BF16)],
    )
    return pl.pallas_call(
        functools.partial(_moe_kernel, final_norm=final_norm),
        grid_spec=grid_spec,
        out_shape=jax.ShapeDtypeStruct((n, d), F32),
        compiler_params=pltpu.CompilerParams(
            dimension_semantics=("parallel", "arbitrary"), vmem_limit_bytes=VMEM_LIMIT),
        name=name,
    )(counts, h, krow, gk, wgu, wd, x, fnw)


def _pad_cols(a, width):
    return jnp.pad(a, ((0, 0), (0, width - a.shape[1])))


def _dispatch_counts(cnt, route_tm):
    c = cnt.reshape(-1, 8, LANES)[:, :N_GROUPS, :route_tm // MOE_SUB]
    return jnp.transpose(c, (0, 2, 1)).reshape(-1).astype(jnp.int32)


def _router_params(wc, bc, wf, bf):
    rw = _pad_cols(jnp.concatenate([wc, wf], axis=1), LANES)
    rb = _pad_cols(jnp.concatenate([bc, bf])[None, :], LANES)
    return rw, rb


def kernel(x, norm_mix_w, norm_ffn_w, even_w_in, gdn_conv_w, gdn_a_log, gdn_dt_bias, gdn_norm_w, hgrn_lb_logits, hgrn_norm_w, even_w_out, odd_w_in, ret_norm_w, odd_w_out, router_c_w, router_c_b, router_f_w, router_f_b, moe_w_gate_up, moe_w_down, final_norm_w):
    bsz, seq, d = x.shape
    n = bsz * seq
    xt = x.reshape(n, d)
    mix_w = N_HEADS * HEAD_D
    conv_cols = 3 * mix_w
    gdn_main = conv_cols + mix_w

    w_in = even_w_in[0]
    small0 = gdn_main
    w_main = [w_in[:, :small0].astype(BF16), w_in[:, small0 + 2 * N_HEADS:].astype(BF16)]
    w_small = jnp.concatenate(
        [_pad_cols(w_in[:, small0:small0 + N_HEADS], LANES),
         _pad_cols(w_in[:, small0 + N_HEADS:small0 + 2 * N_HEADS], LANES)], axis=1).astype(BF16)
    proj, small, wgu_b, wd_b, w_out, w_odd, w_out1 = _norm_matmul(
        xt, norm_mix_w[0][None, :], w_main, w_small, tm=512, tn=1024, name="in_proj_even",
        cast=(moe_w_gate_up.reshape(-1, moe_w_gate_up.shape[-1]), moe_w_down.reshape(-1, moe_w_down.shape[-1]),
              even_w_out[0], odd_w_in[0], odd_w_out[0]))
    wgu_b = wgu_b.reshape(moe_w_gate_up.shape)
    wd_b = wd_b.reshape(moe_w_down.shape)
    proj = proj.reshape(bsz, seq, -1)
    small = small.reshape(bsz, seq, -1)
    o_a = _gdn(proj, small, gdn_conv_w[0], _pad_cols(gdn_a_log[0][None, :], LANES),
               _pad_cols(gdn_dt_bias[0][None, :], LANES), gdn_norm_w[0][None, :], tb=512)
    o_b = _hgrn(proj, hgrn_lb_logits, hgrn_norm_w[0][None, :], tb=512, layer=0, col0=gdn_main // mix_w)
    rw, rb = _router_params(router_c_w[0], router_c_b[0], router_f_w[0], router_f_b[0])
    x1, h, gk, krow, cnt = _outproj_route(o_a.reshape(n, mix_w), o_b.reshape(n, mix_w), 0, 0,
                                          w_out[:mix_w], w_out[mix_w:], xt, norm_ffn_w[0][None, :], rw, rb,
                                          tm=ROUTE_TM, name="out_proj_even")
    x2 = _moe(h, krow, gk, _dispatch_counts(cnt, ROUTE_TM), wgu_b, wd_b, x1,
              final_norm_w[None, :], layer=0, tm=MOE_TM, final_norm=False, name="moe0")

    inv = 1.0 / (ROPE_BASE ** jnp.linspace(0.0, 1.0, RET_DK // 2, dtype=F32))
    ang = jnp.arange(seq, dtype=F32)[:, None] * inv[None, :]
    o_c = _proj_retention(x2, norm_mix_w[1][None, :], w_odd, jnp.cos(ang), jnp.sin(ang), ret_norm_w[0][None, :],
                          tm=512, tn=1024, tc=RET_CHUNK, tokens_per_seq=seq, name="in_proj_retention")
    half = w_out1.shape[0] // 2
    rw, rb = _router_params(router_c_w[1], router_c_b[1], router_f_w[1], router_f_b[1])
    x3, h, gk, krow, cnt = _outproj_route(o_c, o_c, 0, 1, w_out1[:half], w_out1[half:], x2,
                                          norm_ffn_w[1][None, :], rw, rb, tm=ROUTE_TM, name="out_proj_odd")
    out = _moe(h, krow, gk, _dispatch_counts(cnt, ROUTE_TM), wgu_b, wd_b, x3,
               final_norm_w[None, :], layer=1, tm=MOE_TM, final_norm=True, name="moe1")
    return out.reshape(bsz, seq, d)
```

```python
import functools
import math

import jax
import jax.numpy as jnp
from jax import lax
from jax.experimental import pallas as pl
from jax.experimental.pallas import tpu as pltpu

F32 = jnp.float32
BF16 = jnp.bfloat16

D_MODEL = 1024
RMS_EPS = 1e-6
L2_EPS = 1e-6
CHUNK = 64
CONV_K = 4
N_HEADS = 4
HEAD_D = 128
RET_DK = 256
RET_DV = 512
RET_CHUNK = 256
ROPE_BASE = 10000.0
N_GROUPS = 4
EXPERTS_PER_GROUP = 4
N_EXPERTS = 16
D_EXPERT = 256
LANES = 128
ROUTE_TM = 1024
MOE_TM = 1024
MOE_SUB = 256
MOE_BLK = 128
MOE_ALIGN = 16
MOE_KEY_STRIDE = 4096.0
MOE_FIRST_STEP = 32
MOE_FIRST = (256, 288, 320, 352)
VMEM_LIMIT = 56 * 1024 * 1024


def _mm(a, b):
    return jnp.dot(a.astype(BF16), b.astype(BF16), preferred_element_type=F32)


def _mm_nt(a, b):
    return lax.dot_general(a.astype(BF16), b.astype(BF16), (((1,), (1,)), ((), ())),
                           preferred_element_type=F32)


def _mm_tn(a, b):
    return _mm(a.T, b)


def _mm_01(m01, x):
    hi = x.astype(BF16)
    lo = (x - hi.astype(F32)).astype(BF16)
    return (jnp.dot(m01, hi, preferred_element_type=F32)
            + jnp.dot(m01, lo, preferred_element_type=F32))


def _bmm(a, b):
    return jnp.einsum('cik,ckj->cij', a.astype(BF16), b.astype(BF16), preferred_element_type=F32)


def _bmm_nt(a, b):
    return jnp.einsum('cik,cjk->cij', a.astype(BF16), b.astype(BF16), preferred_element_type=F32)


def _bmm_tn(a, b):
    return _bmm(jnp.swapaxes(a, 1, 2), b)


def _bmm_01(m01, x):
    hi = x.astype(BF16)
    lo = (x - hi.astype(F32)).astype(BF16)
    return (jnp.einsum('cik,ckj->cij', m01, hi, preferred_element_type=F32)
            + jnp.einsum('cik,ckj->cij', m01, lo, preferred_element_type=F32))


def _sigmoid(x):
    return 1.0 / (1.0 + jnp.exp2(x * (-math.log2(math.e))))


def _silu(x):
    return x * _sigmoid(x)


def _softplus(x):
    return jnp.maximum(x, 0.0) + jnp.log(1.0 + jnp.exp(-jnp.abs(x)))


def _iota2(shape, dim):
    return lax.broadcasted_iota(jnp.int32, shape, dim)


def _head_rms_gate(o, gate, nw, width):
    outs = []
    for h in range(o.shape[1] // width):
        oh = o[:, h * width:(h + 1) * width]
        ms = jnp.mean(oh * oh, axis=-1, keepdims=True)
        outs.append(oh * lax.rsqrt(ms + RMS_EPS) * nw)
    return jnp.concatenate(outs, axis=1) * _silu(gate)


def _proj_even_kernel(*refs, tn, n_cast, tiles_per_seq, layer):
    (x_ref, nw_ref, wg_ref, wh_ref, ws_ref, lb_ref, hnw_ref), rest = refs[:7], refs[7:]
    cast_in, rest = rest[:n_cast], rest[n_cast:]
    (o_ref, os_ref, ob_ref), rest = rest[:3], rest[3:]
    cast_out, rest = rest[:n_cast], rest[n_cast:]
    hg_s, q_s, k_s, lf_s, i_s, oh_s, st_ref = rest
    for src, dst in zip(cast_in, cast_out):
        dst[...] = src[...].astype(dst.dtype)

    @pl.when(pl.program_id(0) % tiles_per_seq == 0)
    def _():
        st_ref[...] = jnp.zeros(st_ref.shape, F32)

    x = x_ref[...]
    tm = x.shape[0]
    ms = jnp.mean(x * x, axis=-1, keepdims=True)
    hb = (x * lax.rsqrt(ms + RMS_EPS) * nw_ref[...]).astype(BF16)
    os_ref[...] = jnp.dot(hb, ws_ref[...], preferred_element_type=F32)
    for j in range(wh_ref.shape[1] // tn):
        cols = slice(j * tn, (j + 1) * tn)
        hg_s[:, cols] = jnp.dot(hb, wh_ref[:, cols], preferred_element_type=F32)
    for j in range(wg_ref.shape[1] // tn):
        cols = slice(j * tn, (j + 1) * tn)
        o_ref[:, cols] = jnp.dot(hb, wg_ref[:, cols], preferred_element_type=F32).astype(o_ref.dtype)
    w = N_HEADS * HEAD_D
    ob_ref[...] = _hgrn_block(hg_s[:, 0:w], hg_s[:, w:2 * w], hg_s[:, 2 * w:3 * w], hg_s[:, 3 * w:4 * w],
                              lb_ref, hnw_ref, q_s, k_s, lf_s, i_s, oh_s, st_ref,
                              tb=tm, layer=layer).astype(ob_ref.dtype)


def _proj_even(x, nw, w_gdn, w_hgrn, w_small, lb_logits, hgrn_nw, cast, *, tm, tn, tokens_per_seq, layer, name):
    n, d = x.shape
    steps = n // tm
    mix_w = N_HEADS * HEAD_D
    const = lambda i: (0, 0)
    tile = lambda i: (i, 0)
    in_specs = [pl.BlockSpec((tm, d), tile), pl.BlockSpec((1, d), const),
                pl.BlockSpec(w_gdn.shape, const), pl.BlockSpec(w_hgrn.shape, const),
                pl.BlockSpec(w_small.shape, const), pl.BlockSpec(lb_logits.shape, const),
                pl.BlockSpec((1, HEAD_D), const)]
    out_specs = [pl.BlockSpec((tm, w_gdn.shape[1]), tile), pl.BlockSpec((tm, w_small.shape[1]), tile),
                 pl.BlockSpec((tm, mix_w), tile)]
    out_shape = [jax.ShapeDtypeStruct((n, w_gdn.shape[1]), BF16), jax.ShapeDtypeStruct((n, w_small.shape[1]), F32),
                 jax.ShapeDtypeStruct((n, mix_w), BF16)]
    for a in cast:
        slab = (a.shape[0] // steps, a.shape[1])
        in_specs.append(pl.BlockSpec(slab, tile))
        out_specs.append(pl.BlockSpec(slab, tile))
        out_shape.append(jax.ShapeDtypeStruct(a.shape, BF16))
    scratch = ([pltpu.VMEM((tm, w_hgrn.shape[1]), F32)] + [pltpu.VMEM((tm, mix_w), F32)] * 5
               + [pltpu.VMEM((N_HEADS, HEAD_D, HEAD_D), F32)])
    return pl.pallas_call(
        functools.partial(_proj_even_kernel, tn=tn, n_cast=len(cast), tiles_per_seq=tokens_per_seq // tm,
                          layer=layer),
        grid=(steps,),
        in_specs=in_specs,
        out_specs=out_specs,
        out_shape=out_shape,
        scratch_shapes=scratch,
        compiler_params=pltpu.CompilerParams(
            dimension_semantics=("arbitrary",), vmem_limit_bytes=VMEM_LIMIT),
        name=name,
    )(x, nw, w_gdn, w_hgrn, w_small, lb_logits, hgrn_nw, *cast)


def _gdn_kernel(qkv_ref, z_ref, sm_ref, cw_ref, alog_ref, dtb_ref, nw_ref, o_ref,
                xs_ref, qkv_s, o_s, m_s, sq_s, qe_s, dec_s, st_ref, *, tb):
    @pl.when(pl.program_id(1) == 0)
    def _():
        xs_ref[0:8, :] = jnp.zeros((8, xs_ref.shape[1]), F32)
        st_ref[...] = jnp.zeros(st_ref.shape, F32)

    c = CHUNK
    nc = tb // c
    nb = N_HEADS * nc
    qk_w = N_HEADS * HEAD_D
    row = _iota2((c, c), 0)
    col = _iota2((c, c), 1)
    causal = (row >= col)[None]
    strict = (row > col)[None]
    same16 = ((row >> 4) == (col >> 4))[None]
    same32 = ((row >> 5) == (col >> 5))[None]
    eye = (row == col).astype(F32)[None]
    ltri = jnp.broadcast_to((row >= col).astype(BF16)[None], (nc, c, c))
    lane = _iota2((nb, c, LANES), 2)

    def per_head(fn):
        return jnp.concatenate([fn(h) for h in range(N_HEADS)], axis=0)

    for h in range(N_HEADS):
        for base, scale in ((0, HEAD_D ** -0.5), (qk_w, 1.0), (2 * qk_w, None)):
            cols = slice(base + h * HEAD_D, base + (h + 1) * HEAD_D)
            x = qkv_ref[0, :, cols].astype(F32)
            xs_ref[8:8 + tb, cols] = x
            cw = cw_ref[:, cols]
            y = x * cw[CONV_K - 1:CONV_K, :]
            for j in range(1, CONV_K):
                y = y + xs_ref[pl.ds(8 - j, tb), cols] * cw[CONV_K - 1 - j:CONV_K - j, :]
            y = _silu(y)
            if scale is not None:
                y = y * (lax.rsqrt(jnp.sum(y * y, axis=-1, keepdims=True) + L2_EPS) * scale)
            qkv_s[:, cols] = y
    xs_ref[0:8, :] = xs_ref[tb:tb + 8, :]

    sm = sm_ref[0]
    beta_all = _sigmoid(sm[:, :LANES]).reshape(nc, c, LANES)
    rate = -jnp.exp(alog_ref[...]) * math.log2(math.e)
    g_all = (rate * _softplus(sm[:, LANES:] + dtb_ref[...])).reshape(nc, c, LANES)
    gc_all = _bmm_01(ltri, g_all)
    q = per_head(lambda h: qkv_s[:, h * HEAD_D:(h + 1) * HEAD_D].reshape(nc, c, HEAD_D))
    k = per_head(lambda h: qkv_s[:, qk_w + h * HEAD_D:qk_w + (h + 1) * HEAD_D].reshape(nc, c, HEAD_D))
    v = per_head(lambda h: qkv_s[:, 2 * qk_w + h * HEAD_D:2 * qk_w + (h + 1) * HEAD_D].reshape(nc, c, HEAD_D))
    beta = per_head(lambda h: beta_all[:, :, h:h + 1])
    gcol = per_head(lambda h: gc_all[:, :, h:h + 1])
    g_hi = gcol.astype(BF16).astype(F32)
    g_mid = (gcol - g_hi).astype(BF16).astype(F32)
    g_lo = gcol - g_hi - g_mid
    pieces = jnp.where(lane == 0, g_hi, jnp.where(lane == 1, g_mid, jnp.where(lane == 2, g_lo, 0.0)))
    lane1 = _iota2((1, 1, LANES), 2)
    lhs = pieces + jnp.where(jnp.logical_and(lane1 >= 3, lane1 < 6), 1.0, 0.0)
    rhs_g = jnp.where(lane1 < 3, 1.0, 0.0) - pltpu.roll(pieces, 3, axis=2)
    decay = jnp.exp2(jnp.where(causal, _bmm_nt(lhs, rhs_g), -jnp.inf))
    kb = k * beta
    a = jnp.where(strict, _bmm_nt(kb, k) * decay, 0.0)
    d = jnp.where(same16, a, 0.0)
    x_inv = eye - d
    dp = d
    for _ in range(3):
        dp = _bmm(dp, dp)
        x_inv = x_inv + _bmm(x_inv, dp)
    e = jnp.where(jnp.logical_and(same32, jnp.logical_not(same16)), a, 0.0)
    x_inv = x_inv - _bmm(x_inv, _bmm(e, x_inv))
    f = jnp.where(same32, 0.0, a)
    x_inv = x_inv - _bmm(x_inv, _bmm(f, x_inv))
    wu = _bmm(x_inv, jnp.concatenate([kb * jnp.exp2(gcol), v * beta], axis=2))
    attn = _bmm_nt(q, k) * decay
    g_last = gcol[:, c - 1:c, :]
    kd = k * jnp.exp2(g_last - gcol)
    mq = _bmm_tn(kd, wu)
    aw = _bmm(attn, wu)
    m_s[...] = mq[:, :, :HEAD_D].astype(BF16).reshape(N_HEADS, nc, HEAD_D, HEAD_D)
    sq_s[...] = mq[:, :, HEAD_D:].reshape(N_HEADS, nc, HEAD_D, HEAD_D)
    qe_s[...] = (q * jnp.exp2(gcol) - aw[:, :, :HEAD_D]).astype(BF16).reshape(N_HEADS, nc, c, HEAD_D)
    dec_s[...] = jnp.broadcast_to(jnp.exp2(g_last), (nb, 1, HEAD_D)).reshape(N_HEADS, nc, 1, HEAD_D)
    for h in range(N_HEADS):
        o_s[:, h * HEAD_D:(h + 1) * HEAD_D] = aw[h * nc:(h + 1) * nc, :, HEAD_D:].reshape(tb, HEAD_D)

    for ci in range(nc):
        for h in range(N_HEADS):
            hs = slice(h * HEAD_D, (h + 1) * HEAD_D)
            s_h = st_ref[h]
            s_b = s_h.astype(BF16)
            o_s[ci * c:(ci + 1) * c, hs] += jnp.dot(qe_s[h, ci], s_b, preferred_element_type=F32)
            st_ref[h] = (s_h * dec_s[h, ci] - jnp.dot(m_s[h, ci], s_b, preferred_element_type=F32)
                         + sq_s[h, ci])

    o_ref[0] = _head_rms_gate(o_s[...], z_ref[0].astype(F32), nw_ref[...], HEAD_D).astype(o_ref.dtype)


def _gdn(proj, small, conv_w, a_log, dt_bias, norm_w, *, tb):
    b, t, _ = proj.shape
    conv_cols = 3 * N_HEADS * HEAD_D
    mix_w = N_HEADS * HEAD_D
    return pl.pallas_call(
        functools.partial(_gdn_kernel, tb=tb),
        grid=(b, t // tb),
        in_specs=[
            pl.BlockSpec((1, tb, conv_cols), lambda i, j: (i, j, 0)),
            pl.BlockSpec((1, tb, mix_w), lambda i, j: (i, j, conv_cols // mix_w)),
            pl.BlockSpec((1, tb, 2 * LANES), lambda i, j: (i, j, 0)),
            pl.BlockSpec((CONV_K, conv_cols), lambda i, j: (0, 0)),
            pl.BlockSpec((1, LANES), lambda i, j: (0, 0)),
            pl.BlockSpec((1, LANES), lambda i, j: (0, 0)),
            pl.BlockSpec((1, HEAD_D), lambda i, j: (0, 0)),
        ],
        out_specs=pl.BlockSpec((1, tb, mix_w), lambda i, j: (i, j, 0)),
        out_shape=jax.ShapeDtypeStruct((b, t, mix_w), BF16),
        scratch_shapes=[
            pltpu.VMEM((tb + 8, conv_cols), F32),
            pltpu.VMEM((tb, conv_cols), F32),
            pltpu.VMEM((tb, mix_w), F32),
            pltpu.VMEM((N_HEADS, tb // CHUNK, HEAD_D, HEAD_D), BF16),
            pltpu.VMEM((N_HEADS, tb // CHUNK, HEAD_D, HEAD_D), F32),
            pltpu.VMEM((N_HEADS, tb // CHUNK, CHUNK, HEAD_D), BF16),
            pltpu.VMEM((N_HEADS, tb // CHUNK, 1, HEAD_D), F32),
            pltpu.VMEM((N_HEADS, HEAD_D, HEAD_D), F32),
        ],
        compiler_params=pltpu.CompilerParams(
            dimension_semantics=("parallel", "arbitrary"), vmem_limit_bytes=VMEM_LIMIT),
        name="gdn",
    )(proj, proj, small, conv_w, a_log, dt_bias, norm_w)


def _hgrn_block(q_in, f_in, i_in, gate_in, lb_ref, nw_ref, q_s, k_s, lf_s, i_s, o_s, st_ref, *, tb, layer):
    lbl = lb_ref[...]
    e_lb = jnp.exp(lbl - jnp.max(lbl, axis=0, keepdims=True))
    lb = jnp.sum(e_lb[:layer + 1], axis=0, keepdims=True) / jnp.sum(e_lb, axis=0, keepdims=True)

    f = lb + (1.0 - lb) * _sigmoid(f_in)
    k_s[...] = 1.0 - f
    lf_s[...] = jnp.log2(f)
    q_s[...] = _silu(q_in) * (HEAD_D ** -0.5)
    i_s[...] = i_in

    c = CHUNK
    nc = tb // c
    blk = 8
    row = _iota2((c, c), 0)
    col = _iota2((c, c), 1)
    ltri = jnp.broadcast_to((row >= col).astype(BF16)[None], (nc, c, c))
    level_masks = {}
    for m in (32, 16, 8):
        sh = int(math.log2(2 * m))
        level_masks[m] = jnp.logical_and(
            (row >> sh) == (col >> sh),
            jnp.logical_and((row & (2 * m - 1)) >= m, (col & (2 * m - 1)) < m))[None]
    nb = N_HEADS * nc
    sub = _iota2((nb * c // blk, blk, HEAD_D), 1)

    def per_head(fn):
        return jnp.concatenate([fn(slice(h * HEAD_D, (h + 1) * HEAD_D)) for h in range(N_HEADS)], axis=0)

    b_all = _bmm_01(ltri, lf_s[...].reshape(nc, c, N_HEADS * HEAD_D))
    b = per_head(lambda sl: b_all[:, :, sl])
    q = per_head(lambda sl: q_s[:, sl].reshape(nc, c, HEAD_D))
    k = per_head(lambda sl: k_s[:, sl].reshape(nc, c, HEAD_D))
    iv = per_head(lambda sl: i_s[:, sl].reshape(nc, c, HEAD_D))
    attn = jnp.zeros((nb, c, c), F32)
    for m in (32, 16, 8):
        b_m = b.reshape(nb * c // (2 * m), 2 * m, HEAD_D)
        ref = jnp.broadcast_to(b_m[:, m:m + 1, :], b_m.shape).reshape(nb, c, HEAD_D)
        e = jnp.exp2(-jnp.abs(b - ref))
        attn = attn + jnp.where(level_masks[m], _bmm_nt(q * e, k * e), 0.0)
    o = _bmm(attn, iv)
    qb, kb, bb, ib = (a.reshape(nb * c // blk, blk, HEAD_D) for a in (q, k, b, iv))
    ob = jnp.zeros(qb.shape, F32)
    for s in range(blk):
        dec = jnp.exp2(jnp.where(sub >= s, bb - bb[:, s:s + 1, :], -jnp.inf))
        a_col = jnp.sum(dec * qb * kb[:, s:s + 1, :], axis=-1, keepdims=True)
        ob = ob + a_col * ib[:, s:s + 1, :]
    o = o + ob.reshape(nb, c, HEAD_D)
    b_last = b[:, c - 1:c, :]
    q_in = (q * jnp.exp2(b)).astype(BF16)
    upd = _bmm_tn(iv, k * jnp.exp2(b_last - b))
    keep = jnp.exp2(b_last)
    for h in range(N_HEADS):
        s_t = st_ref[h]
        outs = []
        for ci in range(h * nc, (h + 1) * nc):
            outs.append(o[ci] + _mm_nt(q_in[ci], s_t))
            s_t = s_t * keep[ci] + upd[ci]
        st_ref[h] = s_t
        o_s[:, h * HEAD_D:(h + 1) * HEAD_D] = jnp.concatenate(outs, axis=0)

    return _head_rms_gate(o_s[...], gate_in, nw_ref[...], HEAD_D)


def _hgrn_kernel(q_ref, f_ref, i_ref, g_ref, lb_ref, nw_ref, o_ref,
                 q_s, k_s, lf_s, i_s, o_s, st_ref, *, tb, layer):
    @pl.when(pl.program_id(1) == 0)
    def _():
        st_ref[...] = jnp.zeros(st_ref.shape, F32)

    o_ref[0] = _hgrn_block(q_ref[0].astype(F32), f_ref[0].astype(F32), i_ref[0].astype(F32),
                           g_ref[0].astype(F32), lb_ref, nw_ref, q_s, k_s, lf_s, i_s, o_s, st_ref,
                           tb=tb, layer=layer).astype(o_ref.dtype)


def _hgrn(proj, lb_logits, norm_w, *, tb, layer, col0):
    b, t, _ = proj.shape
    w = N_HEADS * HEAD_D
    specs = [pl.BlockSpec((1, tb, w), functools.partial(lambda i, j, off: (i, j, off), off=col0 + n))
             for n in range(4)]
    return pl.pallas_call(
        functools.partial(_hgrn_kernel, tb=tb, layer=layer),
        grid=(b, t // tb),
        in_specs=specs + [
            pl.BlockSpec(lb_logits.shape, lambda i, j: (0, 0)),
            pl.BlockSpec((1, HEAD_D), lambda i, j: (0, 0)),
        ],
        out_specs=pl.BlockSpec((1, tb, w), lambda i, j: (i, j, 0)),
        out_shape=jax.ShapeDtypeStruct((b, t, w), BF16),
        scratch_shapes=[pltpu.VMEM((tb, w), F32)] * 5 + [pltpu.VMEM((N_HEADS, HEAD_D, HEAD_D), F32)],
        compiler_params=pltpu.CompilerParams(
            dimension_semantics=("parallel", "arbitrary"), vmem_limit_bytes=VMEM_LIMIT),
        name="hgrn2",
    )(proj, proj, proj, proj, lb_logits, norm_w)


def _ret_kernel(q_ref, k_ref, v_ref, g_ref, cos_ref, sin_ref, nw_ref, o_ref, dmat_s, st_ref, *, tc):
    log_gammas = [math.log(1.0 - 2.0 ** (-5.0 - h)) for h in range(N_HEADS)]

    @pl.when(pl.program_id(1) == 0)
    def _():
        st_ref[...] = jnp.zeros(st_ref.shape, F32)
        rel = (_iota2((tc, tc), 0) - _iota2((tc, tc), 1)).astype(F32)
        for h in range(N_HEADS):
            dmat_s[h] = jnp.where(rel >= 0, jnp.exp(jnp.maximum(rel, 0.0) * log_gammas[h]), 0.0)

    pos = _iota2((tc, 1), 0).astype(F32)
    cos = cos_ref[...]
    sin = sin_ref[...]
    half = RET_DK // 2

    def rot(x):
        x1, x2 = x[:, :half], x[:, half:]
        return jnp.concatenate([x1 * cos - x2 * sin, x1 * sin + x2 * cos], axis=1)

    for h in range(N_HEADS):
        lg = log_gammas[h]
        qk_cols = slice(h * RET_DK, (h + 1) * RET_DK)
        v_cols = slice(h * RET_DV, (h + 1) * RET_DV)
        q = rot(q_ref[0, :, qk_cols].astype(F32))
        k = rot(k_ref[0, :, qk_cols].astype(F32)) * (RET_DK ** -0.5)
        v = v_ref[0, :, v_cols]
        s = st_ref[h]
        attn = _mm_nt(q, k) * dmat_s[h]
        o = _mm(q * jnp.exp((pos + 1.0) * lg), s) + _mm(attn, v)
        st_ref[h] = s * math.exp(tc * lg) + _mm_tn(k * jnp.exp((tc - 1.0 - pos) * lg), v)
        ms = jnp.mean(o * o, axis=-1, keepdims=True)
        o_ref[0, :, v_cols] = (o * lax.rsqrt(ms + RMS_EPS) * nw_ref[...]
                               * _silu(g_ref[0, :, v_cols].astype(F32))).astype(o_ref.dtype)


def _retention(proj, cos, sin, norm_w, *, tc):
    b, t, _ = proj.shape
    qk_w = N_HEADS * RET_DK
    v_w = N_HEADS * RET_DV
    return pl.pallas_call(
        functools.partial(_ret_kernel, tc=tc),
        grid=(b, t // tc),
        in_specs=[
            pl.BlockSpec((1, tc, qk_w), lambda i, j: (i, j, 0)),
            pl.BlockSpec((1, tc, qk_w), lambda i, j: (i, j, 1)),
            pl.BlockSpec((1, tc, v_w), lambda i, j: (i, j, 1)),
            pl.BlockSpec((1, tc, v_w), lambda i, j: (i, j, 2)),
            pl.BlockSpec((tc, RET_DK // 2), lambda i, j: (j, 0)),
            pl.BlockSpec((tc, RET_DK // 2), lambda i, j: (j, 0)),
            pl.BlockSpec((1, RET_DV), lambda i, j: (0, 0)),
        ],
        out_specs=pl.BlockSpec((1, tc, v_w), lambda i, j: (i, j, 0)),
        out_shape=jax.ShapeDtypeStruct((b, t, v_w), BF16),
        scratch_shapes=[pltpu.VMEM((N_HEADS, tc, tc), F32), pltpu.VMEM((N_HEADS, RET_DK, RET_DV), F32)],
        compiler_params=pltpu.CompilerParams(
            dimension_semantics=("parallel", "arbitrary"), vmem_limit_bytes=VMEM_LIMIT),
        name="retention",
    )(proj, proj, proj, proj, cos, sin, norm_w)


def _proj_ret_kernel(x_ref, nw_ref, w_ref, cos_ref, sin_ref, rnw_ref, o_ref,
                     wq_s, proj_s, dmat_s, st_ref, *, tn, tc, tiles_per_seq):
    i = pl.program_id(0)
    qk_w = N_HEADS * RET_DK
    v_w = N_HEADS * RET_DV
    half = RET_DK // 2
    log_gammas = [math.log(1.0 - 2.0 ** (-5.0 - h)) for h in range(N_HEADS)]

    @pl.when(i == 0)
    def _():
        src = _iota2((RET_DK, RET_DK), 0)
        dst = _iota2((RET_DK, RET_DK), 1)
        split = (src == jnp.where(dst < half, 2 * dst, 2 * (dst - half) + 1)).astype(BF16)
        for hd in range(2 * N_HEADS):
            cols = slice(hd * RET_DK, (hd + 1) * RET_DK)
            wq_s[:, cols] = jnp.dot(w_ref[:, cols], split, preferred_element_type=F32).astype(BF16)
        rel = (_iota2((tc, tc), 0) - _iota2((tc, tc), 1)).astype(F32)
        for h in range(N_HEADS):
            dmat_s[h] = jnp.where(rel >= 0, jnp.exp(jnp.maximum(rel, 0.0) * log_gammas[h]), 0.0)

    @pl.when(i % tiles_per_seq == 0)
    def _():
        st_ref[...] = jnp.zeros(st_ref.shape, F32)

    x = x_ref[...]
    tm = x.shape[0]
    ms = jnp.mean(x * x, axis=-1, keepdims=True)
    hb = (x * lax.rsqrt(ms + RMS_EPS) * nw_ref[...]).astype(BF16)
    for j in range(w_ref.shape[1] // tn):
        cols = slice(j * tn, (j + 1) * tn)
        w_tile = wq_s[:, cols] if (j + 1) * tn <= 2 * qk_w else w_ref[:, cols]
        proj_s[:, cols] = jnp.dot(hb, w_tile, preferred_element_type=F32).astype(BF16)

    pos = _iota2((tc, 1), 0).astype(F32)
    for ck in range(tm // tc):
        rows = slice(ck * tc, (ck + 1) * tc)
        cos = cos_ref[rows, :]
        sin = sin_ref[rows, :]

        def rot(xh):
            x1, x2 = xh[:, :half], xh[:, half:]
            return jnp.concatenate([x1 * cos - x2 * sin, x1 * sin + x2 * cos], axis=1)

        for h in range(N_HEADS):
            lg = log_gammas[h]
            q = rot(proj_s[rows, h * RET_DK:(h + 1) * RET_DK].astype(F32))
            k = rot(proj_s[rows, qk_w + h * RET_DK:qk_w + (h + 1) * RET_DK].astype(F32)) * (RET_DK ** -0.5)
            v = proj_s[rows, 2 * qk_w + h * RET_DV:2 * qk_w + (h + 1) * RET_DV]
            gate = proj_s[rows, 2 * qk_w + v_w + h * RET_DV:2 * qk_w + v_w + (h + 1) * RET_DV]
            s = st_ref[h]
            attn = _mm_nt(q, k) * dmat_s[h]
            o = _mm(q * jnp.exp((pos + 1.0) * lg), s) + _mm(attn, v)
            st_ref[h] = s * math.exp(tc * lg) + _mm_tn(k * jnp.exp((tc - 1.0 - pos) * lg), v)
            ms_o = jnp.mean(o * o, axis=-1, keepdims=True)
            o_ref[rows, h * RET_DV:(h + 1) * RET_DV] = (o * lax.rsqrt(ms_o + RMS_EPS) * rnw_ref[...]
                                                        * _silu(gate.astype(F32))).astype(o_ref.dtype)


def _proj_retention(x, nw, w, cos, sin, ret_nw, *, tm, tn, tc, tokens_per_seq, name):
    n, d = x.shape
    tiles_per_seq = tokens_per_seq // tm
    v_w = N_HEADS * RET_DV
    return pl.pallas_call(
        functools.partial(_proj_ret_kernel, tn=tn, tc=tc, tiles_per_seq=tiles_per_seq),
        grid=(n // tm,),
        in_specs=[
            pl.BlockSpec((tm, d), lambda i: (i, 0)),
            pl.BlockSpec((1, d), lambda i: (0, 0)),
            pl.BlockSpec(w.shape, lambda i: (0, 0)),
            pl.BlockSpec((tm, RET_DK // 2), lambda i: (i % tiles_per_seq, 0)),
            pl.BlockSpec((tm, RET_DK // 2), lambda i: (i % tiles_per_seq, 0)),
            pl.BlockSpec((1, RET_DV), lambda i: (0, 0)),
        ],
        out_specs=pl.BlockSpec((tm, v_w), lambda i: (i, 0)),
        out_shape=jax.ShapeDtypeStruct((n, v_w), BF16),
        scratch_shapes=[
            pltpu.VMEM((d, 2 * N_HEADS * RET_DK), BF16),
            pltpu.VMEM((tm, w.shape[1]), BF16),
            pltpu.VMEM((N_HEADS, tc, tc), F32),
            pltpu.VMEM((N_HEADS, RET_DK, RET_DV), F32),
        ],
        compiler_params=pltpu.CompilerParams(
            dimension_semantics=("arbitrary",), vmem_limit_bytes=VMEM_LIMIT),
        name=name,
    )(x, nw, w, cos, sin, ret_nw)


def _route_gates(logits_t):
    cl = [logits_t[g:g + 1, :] for g in range(N_GROUPS)]
    cmax = functools.reduce(jnp.maximum, cl)
    denom = sum(jnp.exp(x - cmax) for x in cl)
    g_prob = 1.0 / denom
    g_idx = jnp.full(cmax.shape, N_GROUPS - 1, jnp.int32)
    for g in range(N_GROUPS - 2, -1, -1):
        g_idx = jnp.where(cl[g] == cmax, g, g_idx)
    def fine_row(g, j):
        r = N_GROUPS + g * EXPERTS_PER_GROUP + j
        return logits_t[r:r + 1, :]

    fl = []
    for j in range(EXPERTS_PER_GROUP):
        x = fine_row(N_GROUPS - 1, j)
        for g in range(N_GROUPS - 2, -1, -1):
            x = jnp.where(g_idx == g, fine_row(g, j), x)
        fl.append(x)
    m1 = functools.reduce(jnp.maximum, fl)
    i1 = jnp.full(m1.shape, EXPERTS_PER_GROUP - 1, jnp.int32)
    for j in range(EXPERTS_PER_GROUP - 2, -1, -1):
        i1 = jnp.where(fl[j] == m1, j, i1)
    rest = [jnp.where(i1 == j, -jnp.inf, fl[j]) for j in range(EXPERTS_PER_GROUP)]
    m2 = functools.reduce(jnp.maximum, rest)
    i2 = jnp.full(m2.shape, EXPERTS_PER_GROUP - 1, jnp.int32)
    for j in range(EXPERTS_PER_GROUP - 2, -1, -1):
        i2 = jnp.where(jnp.logical_and(rest[j] == m2, i1 != j), j, i2)
    e2 = jnp.exp(m2 - m1)
    w1 = g_prob / (1.0 + e2)
    w2 = g_prob * e2 / (1.0 + e2)
    local = [jnp.where(i1 == j, w1, 0.0) + jnp.where(i2 == j, w2, 0.0) for j in range(EXPERTS_PER_GROUP)]
    return g_idx, local


def _outproj_kernel(a_ref, b_ref, wa_ref, wb_ref, x_ref, nw_ref, rw_ref, rb_ref,
                    x1_ref, h_ref, gk_ref, krow_ref, cnt_ref):
    x1 = (x_ref[...]
          + jnp.dot(a_ref[...], wa_ref[...], preferred_element_type=F32)
          + jnp.dot(b_ref[...], wb_ref[...], preferred_element_type=F32))
    x1_ref[...] = x1
    ms = jnp.mean(x1 * x1, axis=-1, keepdims=True)
    h = x1 * lax.rsqrt(ms + RMS_EPS) * nw_ref[...]
    h_ref[...] = h.astype(BF16)
    rw = rw_ref[...]
    hh = h.astype(BF16)
    hl = (h - hh.astype(F32)).astype(BF16)
    wh = rw.astype(BF16)
    wl = (rw - wh.astype(F32)).astype(BF16)
    hi_part = jnp.dot(hh, jnp.concatenate([wh, wl], axis=1), preferred_element_type=F32)
    logits = (hi_part[:, :LANES] + hi_part[:, LANES:]
              + jnp.dot(hl, wh, preferred_element_type=F32)) + rb_ref[...]
    g_idx, local = _route_gates(logits.T)
    tm = g_idx.shape[1]
    mem = (_iota2((8, tm), 0) == g_idx).astype(BF16)
    before = (_iota2((MOE_SUB, MOE_SUB), 0) < _iota2((MOE_SUB, MOE_SUB), 1)).astype(BF16)
    earlier = jnp.concatenate(
        [jnp.dot(mem[:, s:s + MOE_SUB], before, preferred_element_type=F32) for s in range(0, tm, MOE_SUB)],
        axis=1)
    rank = jnp.sum(mem.astype(F32) * earlier, axis=0, keepdims=True)
    key = g_idx.astype(F32) * MOE_KEY_STRIDE + rank
    krow_ref[...] = key
    sub_shift = int(math.log2(MOE_SUB))
    sub_sel = ((_iota2((tm, LANES), 0) >> sub_shift) == _iota2((tm, LANES), 1)).astype(BF16)
    cnt_ref[...] = jnp.dot(mem, sub_sel, preferred_element_type=F32)
    rows = jnp.concatenate([key] + local + [jnp.zeros((LANES - 1 - EXPERTS_PER_GROUP, tm), F32)], axis=0)
    gk_ref[...] = rows.T


def _outproj_route(a, b, a_blk, b_blk, wa, wb, x, nw, rw, rb, *, tm, name):
    n, d = x.shape
    ka = wa.shape[0]
    kb = wb.shape[0]
    return pl.pallas_call(
        _outproj_kernel,
        grid=(n // tm,),
        in_specs=[
            pl.BlockSpec((tm, ka), lambda i: (i, a_blk)),
            pl.BlockSpec((tm, kb), lambda i: (i, b_blk)),
            pl.BlockSpec((ka, d), lambda i: (0, 0)),
            pl.BlockSpec((kb, d), lambda i: (0, 0)),
            pl.BlockSpec((tm, d), lambda i: (i, 0)),
            pl.BlockSpec((1, d), lambda i: (0, 0)),
            pl.BlockSpec((d, LANES), lambda i: (0, 0)),
            pl.BlockSpec((1, LANES), lambda i: (0, 0)),
        ],
        out_specs=[
            pl.BlockSpec((tm, d), lambda i: (i, 0)),
            pl.BlockSpec((tm, d), lambda i: (i, 0)),
            pl.BlockSpec((tm, LANES), lambda i: (i, 0)),
            pl.BlockSpec((1, tm), lambda i: (0, i)),
            pl.BlockSpec((8, LANES), lambda i: (i, 0)),
        ],
        out_shape=[jax.ShapeDtypeStruct((n, d), F32),
                   jax.ShapeDtypeStruct((n, d), BF16),
                   jax.ShapeDtypeStruct((n, LANES), F32),
                   jax.ShapeDtypeStruct((1, n), F32),
                   jax.ShapeDtypeStruct((8 * (n // tm), LANES), F32)],
        compiler_params=pltpu.CompilerParams(
            dimension_semantics=("parallel",), vmem_limit_bytes=VMEM_LIMIT),
        name=name,
    )(a, b, wa, wb, x, nw, rw, rb)


def _round_up(v, m):
    return ((v + (m - 1)) // m) * m


def _round_up_pow2(v, m):
    return (v + (m - 1)) & ~(m - 1)


def _cdiv_pow2(v, m):
    return lax.shift_right_logical(v + (m - 1), int(math.log2(m)))


def _moe_kernel(cnt_ref, h_ref, krow_ref, gk_ref, wgu_ref, wd_ref, x_ref, fnw_ref, o_ref,
                hc_s, gc_s, y_s, *, final_norm):
    i = pl.program_id(0)
    g = pl.program_id(1)
    tm = h_ref.shape[0]
    n_sub = tm // MOE_SUB

    def segments(grp):
        cnts = [cnt_ref[(i * n_sub + s) * N_GROUPS + grp] for s in range(n_sub)]
        starts = [jnp.int32(0)]
        for s in range(n_sub):
            starts.append(starts[-1] + _round_up_pow2(cnts[s], MOE_ALIGN))
        return cnts, starts

    counts, offs = segments(g)
    total = offs[-1]
    key0 = g.astype(F32) * MOE_KEY_STRIDE
    sub_iota = _iota2((MOE_BLK, MOE_SUB), 0).astype(F32)
    lane_iota = _iota2((MOE_SUB, MOE_BLK), 1).astype(F32)

    def pack(s, w):
        rows = slice(s * MOE_SUB, (s + 1) * MOE_SUB)
        base = key0 + (w * MOE_BLK).astype(F32)
        sel = (krow_ref[:, rows] == sub_iota + base).astype(BF16)
        dst = pl.ds(pl.multiple_of(offs[s] + w * MOE_BLK, MOE_ALIGN), MOE_BLK)
        hc_s[dst, :] = jnp.dot(sel, h_ref[rows, :], preferred_element_type=F32).astype(BF16)
        gc_s[dst, :] = _mm_01(sel, gk_ref[rows, :])

    for s in range(n_sub):
        def pack_more(w, carry, s=s):
            pack(s, w)
            return carry

        lax.fori_loop(1, _cdiv_pow2(counts[s], MOE_BLK), pack_more, 0)
    for s in range(n_sub):
        pack(s, jnp.int32(0))

    tail_rows = MOE_FIRST[-1] + MOE_FIRST_STEP
    tail = pl.ds(pl.multiple_of(total, MOE_ALIGN), tail_rows)
    hc_s[tail, :] = jnp.zeros((tail_rows, hc_s.shape[1]), BF16)
    gc_s[tail, :] = jnp.zeros((tail_rows, LANES), F32)

    def expert_rows(start, rows):
        blk = pl.ds(pl.multiple_of(start, MOE_FIRST_STEP), rows)
        hb = hc_s[blk, :]
        gates = gc_s[blk, :]
        y = jnp.zeros((rows, o_ref.shape[1]), F32)
        for e in range(EXPERTS_PER_GROUP):
            gu = jnp.dot(hb, wgu_ref[0, e], preferred_element_type=F32)
            act = _silu(gu[:, :D_EXPERT]) * gu[:, D_EXPERT:] * gates[:, 1 + e:2 + e]
            y = y + jnp.dot(act.astype(BF16), wd_ref[0, e], preferred_element_type=F32)
        y_s[g, blk, :] = y.astype(BF16)

    first = jnp.clip(_round_up_pow2(total, MOE_FIRST_STEP), MOE_FIRST[0], MOE_FIRST[-1])
    for size in MOE_FIRST:
        @pl.when(first == size)
        def _(size=size):
            expert_rows(0, size)

    n_rest = _cdiv_pow2(jnp.maximum(total - first, 0), MOE_BLK)

    def rest(bi, carry):
        expert_rows(first + bi * MOE_BLK, MOE_BLK)
        return carry

    lax.fori_loop(0, n_rest, rest, 0)
    done = pl.ds(pl.multiple_of(first + n_rest * MOE_BLK, MOE_FIRST_STEP), MOE_BLK)
    y_s[g, done, :] = jnp.zeros((MOE_BLK, y_s.shape[2]), BF16)

    @pl.when(g == N_GROUPS - 1)
    def _():
        segs = [segments(grp) for grp in range(N_GROUPS)]
        for s in range(n_sub):
            rows = slice(s * MOE_SUB, (s + 1) * MOE_SUB)
            key_col = gk_ref[rows, 0:1]
            sel = jnp.concatenate(
                [(key_col == lane_iota + grp * MOE_KEY_STRIDE).astype(BF16) for grp in range(N_GROUPS)], axis=1)
            packed = jnp.concatenate(
                [y_s[grp, pl.ds(pl.multiple_of(segs[grp][1][s], MOE_ALIGN), MOE_BLK), :]
                 for grp in range(N_GROUPS)], axis=0)
            o_ref[rows, :] = x_ref[rows, :] + jnp.dot(sel, packed, preferred_element_type=F32)
            for grp in range(N_GROUPS):
                def more(w, carry, s=s, rows=rows, grp=grp, key_col=key_col):
                    base = grp * MOE_KEY_STRIDE + (w * MOE_BLK).astype(F32)
                    sel_w = (key_col == lane_iota + base).astype(BF16)
                    src = pl.ds(pl.multiple_of(segs[grp][1][s] + w * MOE_BLK, MOE_ALIGN), MOE_BLK)
                    o_ref[rows, :] += jnp.dot(sel_w, y_s[grp, src, :], preferred_element_type=F32)
                    return carry

                lax.fori_loop(1, _cdiv_pow2(segs[grp][0][s], MOE_BLK), more, 0)
        if final_norm:
            y = o_ref[...]
            ms = jnp.mean(y * y, axis=-1, keepdims=True)
            o_ref[...] = y * lax.rsqrt(ms + RMS_EPS) * fnw_ref[...]


def _moe(h, krow, gk, counts, wgu, wd, x, fnw, *, layer, tm, final_norm, name):
    n, d = x.shape
    buf_rows = _round_up(tm + (tm // MOE_SUB) * MOE_ALIGN + MOE_FIRST[-1] + MOE_FIRST_STEP, MOE_BLK)
    grid_spec = pltpu.PrefetchScalarGridSpec(
        num_scalar_prefetch=1,
        grid=(n // tm, N_GROUPS),
        in_specs=[
            pl.BlockSpec((tm, d), lambda i, g, c: (i, 0)),
            pl.BlockSpec((1, tm), lambda i, g, c: (0, i)),
            pl.BlockSpec((tm, LANES), lambda i, g, c: (i, 0)),
            pl.BlockSpec((1, EXPERTS_PER_GROUP, d, 2 * D_EXPERT), lambda i, g, c: (layer, g, 0, 0)),
            pl.BlockSpec((1, EXPERTS_PER_GROUP, D_EXPERT, d), lambda i, g, c: (layer, g, 0, 0)),
            pl.BlockSpec((tm, d), lambda i, g, c: (i, 0)),
            pl.BlockSpec((1, d), lambda i, g, c: (0, 0)),
        ],
        out_specs=pl.BlockSpec((tm, d), lambda i, g, c: (i, 0)),
        scratch_shapes=[pltpu.VMEM((buf_rows, d), BF16),
                        pltpu.VMEM((buf_rows, LANES), F32),
                        pltpu.VMEM((N_GROUPS, buf_rows, d), BF16)],
    )
    return pl.pallas_call(
        functools.partial(_moe_kernel, final_norm=final_norm),
        grid_spec=grid_spec,
        out_shape=jax.ShapeDtypeStruct((n, d), F32),
        compiler_params=pltpu.CompilerParams(
            dimension_semantics=("parallel", "arbitrary"), vmem_limit_bytes=VMEM_LIMIT),
        name=name,
    )(counts, h, krow, gk, wgu, wd, x, fnw)


def _pad_cols(a, width):
    return jnp.pad(a, ((0, 0), (0, width - a.shape[1])))


def _dispatch_counts(cnt, route_tm):
    c = cnt.reshape(-1, 8, LANES)[:, :N_GROUPS, :route_tm // MOE_SUB]
    return jnp.transpose(c, (0, 2, 1)).reshape(-1).astype(jnp.int32)


def _router_params(wc, bc, wf, bf):
    rw = _pad_cols(jnp.concatenate([wc, wf], axis=1), LANES)
    rb = _pad_cols(jnp.concatenate([bc, bf])[None, :], LANES)
    return rw, rb


def kernel(x, norm_mix_w, norm_ffn_w, even_w_in, gdn_conv_w, gdn_a_log, gdn_dt_bias, gdn_norm_w, hgrn_lb_logits, hgrn_norm_w, even_w_out, odd_w_in, ret_norm_w, odd_w_out, router_c_w, router_c_b, router_f_w, router_f_b, moe_w_gate_up, moe_w_down, final_norm_w):
    bsz, seq, d = x.shape
    n = bsz * seq
    xt = x.reshape(n, d)
    mix_w = N_HEADS * HEAD_D
    conv_cols = 3 * mix_w
    gdn_main = conv_cols + mix_w

    w_in = even_w_in[0]
    small0 = gdn_main
    w_small = jnp.concatenate(
        [_pad_cols(w_in[:, small0:small0 + N_HEADS], LANES),
         _pad_cols(w_in[:, small0 + N_HEADS:small0 + 2 * N_HEADS], LANES)], axis=1).astype(BF16)
    proj, small, o_b, wgu_b, wd_b, w_out, w_odd, w_out1 = _proj_even(
        xt, norm_mix_w[0][None, :], w_in[:, :small0].astype(BF16), w_in[:, small0 + 2 * N_HEADS:].astype(BF16),
        w_small, hgrn_lb_logits, hgrn_norm_w[0][None, :],
        (moe_w_gate_up.reshape(-1, moe_w_gate_up.shape[-1]), moe_w_down.reshape(-1, moe_w_down.shape[-1]),
         even_w_out[0], odd_w_in[0], odd_w_out[0]),
        tm=512, tn=1024, tokens_per_seq=seq, layer=0, name="in_proj_even_hgrn2")
    wgu_b = wgu_b.reshape(moe_w_gate_up.shape)
    wd_b = wd_b.reshape(moe_w_down.shape)
    proj = proj.reshape(bsz, seq, -1)
    small = small.reshape(bsz, seq, -1)
    o_a = _gdn(proj, small, gdn_conv_w[0], _pad_cols(gdn_a_log[0][None, :], LANES),
               _pad_cols(gdn_dt_bias[0][None, :], LANES), gdn_norm_w[0][None, :], tb=512)
    rw, rb = _router_params(router_c_w[0], router_c_b[0], router_f_w[0], router_f_b[0])
    x1, h, gk, krow, cnt = _outproj_route(o_a.reshape(n, mix_w), o_b.reshape(n, mix_w), 0, 0,
                                          w_out[:mix_w], w_out[mix_w:], xt, norm_ffn_w[0][None, :], rw, rb,
                                          tm=ROUTE_TM, name="out_proj_even")
    x2 = _moe(h, krow, gk, _dispatch_counts(cnt, ROUTE_TM), wgu_b, wd_b, x1,
              final_norm_w[None, :], layer=0, tm=MOE_TM, final_norm=False, name="moe0")

    inv = 1.0 / (ROPE_BASE ** jnp.linspace(0.0, 1.0, RET_DK // 2, dtype=F32))
    ang = jnp.arange(seq, dtype=F32)[:, None] * inv[None, :]
    o_c = _proj_retention(x2, norm_mix_w[1][None, :], w_odd, jnp.cos(ang), jnp.sin(ang), ret_norm_w[0][None, :],
                          tm=512, tn=1024, tc=RET_CHUNK, tokens_per_seq=seq, name="in_proj_retention")
    half = w_out1.shape[0] // 2
    rw, rb = _router_params(router_c_w[1], router_c_b[1], router_f_w[1], router_f_b[1])
    x3, h, gk, krow, cnt = _outproj_route(o_c, o_c, 0, 1, w_out1[:half], w_out1[half:], x2,
                                          norm_ffn_w[1][None, :], rw, rb, tm=ROUTE_TM, name="out_proj_odd")
    out = _moe(h, krow, gk, _dispatch_counts(cnt, ROUTE_TM), wgu_b, wd_b, x3,
               final_norm_w[None, :], layer=1, tm=MOE_TM, final_norm=True, name="moe1")
    return out.reshape(bsz, seq, d)
```

```python
import functools
import math

import jax
import jax.numpy as jnp
import numpy as np
from jax import lax
from jax.experimental import pallas as pl
from jax.experimental.pallas import tpu as pltpu

F32 = jnp.float32
BF16 = jnp.bfloat16

D_MODEL = 1024
RMS_EPS = 1e-6
L2_EPS = 1e-6
CHUNK = 64
CONV_K = 4
N_HEADS = 4
HEAD_D = 128
RET_DK = 256
RET_DV = 512
RET_CHUNK = 256
ROPE_BASE = 10000.0
N_GROUPS = 4
EXPERTS_PER_GROUP = 4
N_EXPERTS = 16
D_EXPERT = 256
LANES = 128
ROUTE_TM = 1024
MOE_TM = 1024
MOE_SUB = 256
MOE_BLK = 128
MOE_ALIGN = 16
MOE_KEY_STRIDE = 4096.0
MOE_FIRST_STEP = 32
MOE_FIRST = (256, 288, 320, 352)
VMEM_LIMIT = 56 * 1024 * 1024


def _mm(a, b):
    return jnp.dot(a.astype(BF16), b.astype(BF16), preferred_element_type=F32)


def _mm_nt(a, b):
    return lax.dot_general(a.astype(BF16), b.astype(BF16), (((1,), (1,)), ((), ())),
                           preferred_element_type=F32)


def _mm_tn(a, b):
    return _mm(a.T, b)


def _mm_01(m01, x):
    hi = x.astype(BF16)
    lo = (x - hi.astype(F32)).astype(BF16)
    return (jnp.dot(m01, hi, preferred_element_type=F32)
            + jnp.dot(m01, lo, preferred_element_type=F32))


def _bmm(a, b):
    return jnp.einsum('cik,ckj->cij', a.astype(BF16), b.astype(BF16), preferred_element_type=F32)


def _bmm_nt(a, b):
    return jnp.einsum('cik,cjk->cij', a.astype(BF16), b.astype(BF16), preferred_element_type=F32)


def _bmm_tn(a, b):
    return _bmm(jnp.swapaxes(a, 1, 2), b)


def _bmm_01(m01, x):
    hi = x.astype(BF16)
    lo = (x - hi.astype(F32)).astype(BF16)
    return (jnp.einsum('cik,ckj->cij', m01, hi, preferred_element_type=F32)
            + jnp.einsum('cik,ckj->cij', m01, lo, preferred_element_type=F32))


def _sigmoid(x):
    return 1.0 / (1.0 + jnp.exp2(x * (-math.log2(math.e))))


def _silu(x):
    return x * _sigmoid(x)


def _softplus(x):
    return jnp.maximum(x, 0.0) + jnp.log(1.0 + jnp.exp(-jnp.abs(x)))


def _iota2(shape, dim):
    return lax.broadcasted_iota(jnp.int32, shape, dim)


def _head_rms_gate(o, gate, nw, width):
    outs = []
    for h in range(o.shape[1] // width):
        oh = o[:, h * width:(h + 1) * width]
        ms = jnp.mean(oh * oh, axis=-1, keepdims=True)
        outs.append(oh * lax.rsqrt(ms + RMS_EPS) * nw)
    return jnp.concatenate(outs, axis=1) * _silu(gate)


def _proj_even_kernel(*refs, tn, n_cast, tiles_per_seq, layer, shift):
    (x_ref, nw_ref, wg_ref, wh_ref, ws_ref, lb_ref, hnw_ref), rest = refs[:7], refs[7:]
    cast_in, rest = rest[:n_cast], rest[n_cast:]
    (o_ref, os_ref, ob_ref), rest = rest[:3], rest[3:]
    cast_out, rest = rest[:n_cast], rest[n_cast:]
    hg_s, whg_s, q_s, k_s, lf_s, i_s, oh_s, st_ref = rest
    for src, dst in zip(cast_in, cast_out):
        dst[...] = src[...].astype(dst.dtype)

    @pl.when(pl.program_id(0) == 0)
    def _():
        sel = (_iota2((2 * LANES, LANES), 0) == _iota2((2 * LANES, LANES), 1) + shift).astype(BF16)
        n_blocks = wh_ref.shape[1] // LANES
        for j in range(n_blocks):
            if j + 1 < n_blocks:
                both = wh_ref[:, j * LANES:(j + 2) * LANES]
            else:
                both = jnp.concatenate([wh_ref[:, j * LANES:], ws_ref[:, 2 * LANES:]], axis=1)
            whg_s[:, j * LANES:(j + 1) * LANES] = jnp.dot(both, sel, preferred_element_type=F32).astype(BF16)

    @pl.when(pl.program_id(0) % tiles_per_seq == 0)
    def _():
        st_ref[...] = jnp.zeros(st_ref.shape, F32)

    x = x_ref[...]
    tm = x.shape[0]
    ms = jnp.mean(x * x, axis=-1, keepdims=True)
    hb = (x * lax.rsqrt(ms + RMS_EPS) * nw_ref[...]).astype(BF16)
    os_ref[...] = jnp.dot(hb, ws_ref[:, :2 * LANES], preferred_element_type=F32)
    for j in range(whg_s.shape[1] // tn):
        cols = slice(j * tn, (j + 1) * tn)
        hg_s[:, cols] = jnp.dot(hb, whg_s[:, cols], preferred_element_type=F32)
    for j in range(wg_ref.shape[1] // tn):
        cols = slice(j * tn, (j + 1) * tn)
        o_ref[:, cols] = jnp.dot(hb, wg_ref[:, cols], preferred_element_type=F32).astype(o_ref.dtype)
    w = N_HEADS * HEAD_D
    ob_ref[...] = _hgrn_block(hg_s[:, 0:w], hg_s[:, w:2 * w], hg_s[:, 2 * w:3 * w], hg_s[:, 3 * w:4 * w],
                              lb_ref, hnw_ref, q_s, k_s, lf_s, i_s, oh_s, st_ref,
                              tb=tm, layer=layer).astype(ob_ref.dtype)


def _proj_even(x, nw, w, w_small, lb_logits, hgrn_nw, cast, *, gdn_w, tm, tn, tokens_per_seq, layer, name):
    n, d = x.shape
    steps = n // tm
    mix_w = N_HEADS * HEAD_D
    hgrn_w = 4 * mix_w
    shift = w.shape[1] - gdn_w - hgrn_w
    assert gdn_w == hgrn_w and 0 < shift < LANES and w_small.shape[1] == 3 * LANES
    const = lambda i: (0, 0)
    tile = lambda i: (i, 0)
    in_specs = [pl.BlockSpec((tm, d), tile), pl.BlockSpec((1, d), const),
                pl.BlockSpec((d, gdn_w), const), pl.BlockSpec((d, hgrn_w), lambda i: (0, 1)),
                pl.BlockSpec(w_small.shape, const), pl.BlockSpec(lb_logits.shape, const),
                pl.BlockSpec((1, HEAD_D), const)]
    out_specs = [pl.BlockSpec((tm, gdn_w), tile), pl.BlockSpec((tm, 2 * LANES), tile),
                 pl.BlockSpec((tm, mix_w), tile)]
    out_shape = [jax.ShapeDtypeStruct((n, gdn_w), BF16), jax.ShapeDtypeStruct((n, 2 * LANES), F32),
                 jax.ShapeDtypeStruct((n, mix_w), BF16)]
    for a in cast:
        slab = (a.shape[0] // steps, a.shape[1])
        in_specs.append(pl.BlockSpec(slab, tile))
        out_specs.append(pl.BlockSpec(slab, tile))
        out_shape.append(jax.ShapeDtypeStruct(a.shape, BF16))
    scratch = ([pltpu.VMEM((tm, hgrn_w), F32), pltpu.VMEM((d, hgrn_w), BF16)] + [pltpu.VMEM((tm, mix_w), F32)] * 5
               + [pltpu.VMEM((N_HEADS, HEAD_D, HEAD_D), F32)])
    return pl.pallas_call(
        functools.partial(_proj_even_kernel, tn=tn, n_cast=len(cast), tiles_per_seq=tokens_per_seq // tm,
                          layer=layer, shift=shift),
        grid=(steps,),
        in_specs=in_specs,
        out_specs=out_specs,
        out_shape=out_shape,
        scratch_shapes=scratch,
        compiler_params=pltpu.CompilerParams(
            dimension_semantics=("arbitrary",), vmem_limit_bytes=VMEM_LIMIT),
        name=name,
    )(x, nw, w, w, w_small, lb_logits, hgrn_nw, *cast)


def _gdn_kernel(qkv_ref, z_ref, sm_ref, cw_ref, alog_ref, dtb_ref, nw_ref, o_ref,
                xs_ref, qkv_s, o_s, m_s, sq_s, qe_s, dec_s, st_ref, *, tb):
    @pl.when(pl.program_id(1) == 0)
    def _():
        xs_ref[0:8, :] = jnp.zeros((8, xs_ref.shape[1]), F32)
        st_ref[...] = jnp.zeros(st_ref.shape, F32)

    c = CHUNK
    nc = tb // c
    nb = N_HEADS * nc
    qk_w = N_HEADS * HEAD_D
    row = _iota2((c, c), 0)
    col = _iota2((c, c), 1)
    causal = (row >= col)[None]
    strict = (row > col)[None]
    same16 = ((row >> 4) == (col >> 4))[None]
    same32 = ((row >> 5) == (col >> 5))[None]
    eye = (row == col).astype(F32)[None]
    ltri = jnp.broadcast_to((row >= col).astype(BF16)[None], (nc, c, c))
    lane = _iota2((nb, c, LANES), 2)

    def per_head(fn):
        return jnp.concatenate([fn(h) for h in range(N_HEADS)], axis=0)

    for h in range(N_HEADS):
        for base, scale in ((0, HEAD_D ** -0.5), (qk_w, 1.0), (2 * qk_w, None)):
            cols = slice(base + h * HEAD_D, base + (h + 1) * HEAD_D)
            x = qkv_ref[0, :, cols].astype(F32)
            xs_ref[8:8 + tb, cols] = x
            cw = cw_ref[:, cols]
            y = x * cw[CONV_K - 1:CONV_K, :]
            for j in range(1, CONV_K):
                y = y + xs_ref[pl.ds(8 - j, tb), cols] * cw[CONV_K - 1 - j:CONV_K - j, :]
            y = _silu(y)
            if scale is not None:
                y = y * (lax.rsqrt(jnp.sum(y * y, axis=-1, keepdims=True) + L2_EPS) * scale)
            qkv_s[:, cols] = y
    xs_ref[0:8, :] = xs_ref[tb:tb + 8, :]

    sm = sm_ref[0]
    beta_all = _sigmoid(sm[:, :LANES]).reshape(nc, c, LANES)
    rate = -jnp.exp(alog_ref[...]) * math.log2(math.e)
    g_all = (rate * _softplus(sm[:, LANES:] + dtb_ref[...])).reshape(nc, c, LANES)
    gc_all = _bmm_01(ltri, g_all)
    q = per_head(lambda h: qkv_s[:, h * HEAD_D:(h + 1) * HEAD_D].reshape(nc, c, HEAD_D))
    k = per_head(lambda h: qkv_s[:, qk_w + h * HEAD_D:qk_w + (h + 1) * HEAD_D].reshape(nc, c, HEAD_D))
    v = per_head(lambda h: qkv_s[:, 2 * qk_w + h * HEAD_D:2 * qk_w + (h + 1) * HEAD_D].reshape(nc, c, HEAD_D))
    beta = per_head(lambda h: beta_all[:, :, h:h + 1])
    gcol = per_head(lambda h: gc_all[:, :, h:h + 1])
    g_hi = gcol.astype(BF16).astype(F32)
    g_mid = (gcol - g_hi).astype(BF16).astype(F32)
    g_lo = gcol - g_hi - g_mid
    pieces = jnp.where(lane == 0, g_hi, jnp.where(lane == 1, g_mid, jnp.where(lane == 2, g_lo, 0.0)))
    lane1 = _iota2((1, 1, LANES), 2)
    lhs = pieces + jnp.where(jnp.logical_and(lane1 >= 3, lane1 < 6), 1.0, 0.0)
    rhs_g = jnp.where(lane1 < 3, 1.0, 0.0) - pltpu.roll(pieces, 3, axis=2)
    decay = jnp.exp2(jnp.where(causal, _bmm_nt(lhs, rhs_g), -jnp.inf))
    kb = k * beta
    a = jnp.where(strict, _bmm_nt(kb, k) * decay, 0.0)
    d = jnp.where(same16, a, 0.0)
    x_inv = eye - d
    dp = d
    for _ in range(3):
        dp = _bmm(dp, dp)
        x_inv = x_inv + _bmm(x_inv, dp)
    e = jnp.where(jnp.logical_and(same32, jnp.logical_not(same16)), a, 0.0)
    x_inv = x_inv - _bmm(x_inv, _bmm(e, x_inv))
    f = jnp.where(same32, 0.0, a)
    x_inv = x_inv - _bmm(x_inv, _bmm(f, x_inv))
    wu = _bmm(x_inv, jnp.concatenate([kb * jnp.exp2(gcol), v * beta], axis=2))
    attn = _bmm_nt(q, k) * decay
    g_last = gcol[:, c - 1:c, :]
    kd = k * jnp.exp2(g_last - gcol)
    mq = _bmm_tn(kd, wu)
    aw = _bmm(attn, wu)
    m_s[...] = mq[:, :, :HEAD_D].astype(BF16).reshape(N_HEADS, nc, HEAD_D, HEAD_D)
    sq_s[...] = mq[:, :, HEAD_D:].reshape(N_HEADS, nc, HEAD_D, HEAD_D)
    qe_s[...] = (q * jnp.exp2(gcol) - aw[:, :, :HEAD_D]).astype(BF16).reshape(N_HEADS, nc, c, HEAD_D)
    dec_s[...] = jnp.broadcast_to(jnp.exp2(g_last), (nb, 1, HEAD_D)).reshape(N_HEADS, nc, 1, HEAD_D)
    for h in range(N_HEADS):
        o_s[:, h * HEAD_D:(h + 1) * HEAD_D] = aw[h * nc:(h + 1) * nc, :, HEAD_D:].reshape(tb, HEAD_D)

    for ci in range(nc):
        for h in range(N_HEADS):
            hs = slice(h * HEAD_D, (h + 1) * HEAD_D)
            s_h = st_ref[h]
            s_b = s_h.astype(BF16)
            o_s[ci * c:(ci + 1) * c, hs] += jnp.dot(qe_s[h, ci], s_b, preferred_element_type=F32)
            st_ref[h] = (s_h * dec_s[h, ci] - jnp.dot(m_s[h, ci], s_b, preferred_element_type=F32)
                         + sq_s[h, ci])

    o_ref[0] = _head_rms_gate(o_s[...], z_ref[0].astype(F32), nw_ref[...], HEAD_D).astype(o_ref.dtype)


def _gdn(proj, small, conv_w, a_log, dt_bias, norm_w, *, tb):
    b, t, _ = proj.shape
    conv_cols = 3 * N_HEADS * HEAD_D
    mix_w = N_HEADS * HEAD_D
    return pl.pallas_call(
        functools.partial(_gdn_kernel, tb=tb),
        grid=(b, t // tb),
        in_specs=[
            pl.BlockSpec((1, tb, conv_cols), lambda i, j: (i, j, 0)),
            pl.BlockSpec((1, tb, mix_w), lambda i, j: (i, j, conv_cols // mix_w)),
            pl.BlockSpec((1, tb, 2 * LANES), lambda i, j: (i, j, 0)),
            pl.BlockSpec((CONV_K, conv_cols), lambda i, j: (0, 0)),
            pl.BlockSpec((1, LANES), lambda i, j: (0, 0)),
            pl.BlockSpec((1, LANES), lambda i, j: (0, 0)),
            pl.BlockSpec((1, HEAD_D), lambda i, j: (0, 0)),
        ],
        out_specs=pl.BlockSpec((1, tb, mix_w), lambda i, j: (i, j, 0)),
        out_shape=jax.ShapeDtypeStruct((b, t, mix_w), BF16),
        scratch_shapes=[
            pltpu.VMEM((tb + 8, conv_cols), F32),
            pltpu.VMEM((tb, conv_cols), F32),
            pltpu.VMEM((tb, mix_w), F32),
            pltpu.VMEM((N_HEADS, tb // CHUNK, HEAD_D, HEAD_D), BF16),
            pltpu.VMEM((N_HEADS, tb // CHUNK, HEAD_D, HEAD_D), F32),
            pltpu.VMEM((N_HEADS, tb // CHUNK, CHUNK, HEAD_D), BF16),
            pltpu.VMEM((N_HEADS, tb // CHUNK, 1, HEAD_D), F32),
            pltpu.VMEM((N_HEADS, HEAD_D, HEAD_D), F32),
        ],
        compiler_params=pltpu.CompilerParams(
            dimension_semantics=("parallel", "arbitrary"), vmem_limit_bytes=VMEM_LIMIT),
        name="gdn",
    )(proj, proj, small, conv_w, a_log, dt_bias, norm_w)


def _hgrn_block(q_in, f_in, i_in, gate_in, lb_ref, nw_ref, q_s, k_s, lf_s, i_s, o_s, st_ref, *, tb, layer):
    lbl = lb_ref[...]
    e_lb = jnp.exp(lbl - jnp.max(lbl, axis=0, keepdims=True))
    lb = jnp.sum(e_lb[:layer + 1], axis=0, keepdims=True) / jnp.sum(e_lb, axis=0, keepdims=True)

    f = lb + (1.0 - lb) * _sigmoid(f_in)
    k_s[...] = 1.0 - f
    lf_s[...] = jnp.log2(f)
    q_s[...] = _silu(q_in) * (HEAD_D ** -0.5)
    i_s[...] = i_in

    c = CHUNK
    nc = tb // c
    blk = 8
    row = _iota2((c, c), 0)
    col = _iota2((c, c), 1)
    ltri = jnp.broadcast_to((row >= col).astype(BF16)[None], (nc, c, c))
    level_masks = {}
    for m in (32, 16, 8):
        sh = int(math.log2(2 * m))
        level_masks[m] = jnp.logical_and(
            (row >> sh) == (col >> sh),
            jnp.logical_and((row & (2 * m - 1)) >= m, (col & (2 * m - 1)) < m))[None]
    nb = N_HEADS * nc
    sub = _iota2((nb * c // blk, blk, HEAD_D), 1)

    def per_head(fn):
        return jnp.concatenate([fn(slice(h * HEAD_D, (h + 1) * HEAD_D)) for h in range(N_HEADS)], axis=0)

    b_all = _bmm_01(ltri, lf_s[...].reshape(nc, c, N_HEADS * HEAD_D))
    b = per_head(lambda sl: b_all[:, :, sl])
    q = per_head(lambda sl: q_s[:, sl].reshape(nc, c, HEAD_D))
    k = per_head(lambda sl: k_s[:, sl].reshape(nc, c, HEAD_D))
    iv = per_head(lambda sl: i_s[:, sl].reshape(nc, c, HEAD_D))
    attn = jnp.zeros((nb, c, c), F32)
    for m in (32, 16, 8):
        b_m = b.reshape(nb * c // (2 * m), 2 * m, HEAD_D)
        ref = jnp.broadcast_to(b_m[:, m:m + 1, :], b_m.shape).reshape(nb, c, HEAD_D)
        e = jnp.exp2(-jnp.abs(b - ref))
        attn = attn + jnp.where(level_masks[m], _bmm_nt(q * e, k * e), 0.0)
    o = _bmm(attn, iv)
    qb, kb, bb, ib = (a.reshape(nb * c // blk, blk, HEAD_D) for a in (q, k, b, iv))
    ob = jnp.zeros(qb.shape, F32)
    for s in range(blk):
        dec = jnp.exp2(jnp.where(sub >= s, bb - bb[:, s:s + 1, :], -jnp.inf))
        a_col = jnp.sum(dec * qb * kb[:, s:s + 1, :], axis=-1, keepdims=True)
        ob = ob + a_col * ib[:, s:s + 1, :]
    o = o + ob.reshape(nb, c, HEAD_D)
    b_last = b[:, c - 1:c, :]
    q_in = (q * jnp.exp2(b)).astype(BF16)
    upd = _bmm_tn(iv, k * jnp.exp2(b_last - b))
    keep = jnp.exp2(b_last)
    for h in range(N_HEADS):
        s_t = st_ref[h]
        outs = []
        for ci in range(h * nc, (h + 1) * nc):
            outs.append(o[ci] + _mm_nt(q_in[ci], s_t))
            s_t = s_t * keep[ci] + upd[ci]
        st_ref[h] = s_t
        o_s[:, h * HEAD_D:(h + 1) * HEAD_D] = jnp.concatenate(outs, axis=0)

    return _head_rms_gate(o_s[...], gate_in, nw_ref[...], HEAD_D)


def _proj_ret_kernel(x_ref, nw_ref, w_ref, cos_ref, sin_ref, rnw_ref, o_ref,
                     wq_s, proj_s, dmat_s, st_ref, *, tn, tc, tiles_per_seq):
    i = pl.program_id(0)
    qk_w = N_HEADS * RET_DK
    v_w = N_HEADS * RET_DV
    half = RET_DK // 2
    log_gammas = [math.log(1.0 - 2.0 ** (-5.0 - h)) for h in range(N_HEADS)]

    @pl.when(i == 0)
    def _():
        src = _iota2((RET_DK, RET_DK), 0)
        dst = _iota2((RET_DK, RET_DK), 1)
        split = (src == jnp.where(dst < half, 2 * dst, 2 * (dst - half) + 1)).astype(BF16)
        for hd in range(2 * N_HEADS):
            cols = slice(hd * RET_DK, (hd + 1) * RET_DK)
            wq_s[:, cols] = jnp.dot(w_ref[:, cols], split, preferred_element_type=F32).astype(BF16)
        rel = (_iota2((tc, tc), 0) - _iota2((tc, tc), 1)).astype(F32)
        for h in range(N_HEADS):
            dmat_s[h] = jnp.where(rel >= 0, jnp.exp(jnp.maximum(rel, 0.0) * log_gammas[h]), 0.0)

    @pl.when(i % tiles_per_seq == 0)
    def _():
        st_ref[...] = jnp.zeros(st_ref.shape, F32)

    x = x_ref[...]
    tm = x.shape[0]
    ms = jnp.mean(x * x, axis=-1, keepdims=True)
    hb = (x * lax.rsqrt(ms + RMS_EPS) * nw_ref[...]).astype(BF16)
    for j in range(w_ref.shape[1] // tn):
        cols = slice(j * tn, (j + 1) * tn)
        w_tile = wq_s[:, cols] if (j + 1) * tn <= 2 * qk_w else w_ref[:, cols]
        proj_s[:, cols] = jnp.dot(hb, w_tile, preferred_element_type=F32).astype(BF16)

    pos = _iota2((tc, 1), 0).astype(F32)
    for ck in range(tm // tc):
        rows = slice(ck * tc, (ck + 1) * tc)
        cos = cos_ref[rows, :]
        sin = sin_ref[rows, :]

        def rot(xh):
            x1, x2 = xh[:, :half], xh[:, half:]
            return jnp.concatenate([x1 * cos - x2 * sin, x1 * sin + x2 * cos], axis=1)

        for h in range(N_HEADS):
            lg = log_gammas[h]
            q = rot(proj_s[rows, h * RET_DK:(h + 1) * RET_DK].astype(F32))
            k = rot(proj_s[rows, qk_w + h * RET_DK:qk_w + (h + 1) * RET_DK].astype(F32)) * (RET_DK ** -0.5)
            v = proj_s[rows, 2 * qk_w + h * RET_DV:2 * qk_w + (h + 1) * RET_DV]
            gate = proj_s[rows, 2 * qk_w + v_w + h * RET_DV:2 * qk_w + v_w + (h + 1) * RET_DV]
            s = st_ref[h]
            attn = _mm_nt(q, k) * dmat_s[h]
            o = _mm(q * jnp.exp((pos + 1.0) * lg), s) + _mm(attn, v)
            st_ref[h] = s * math.exp(tc * lg) + _mm_tn(k * jnp.exp((tc - 1.0 - pos) * lg), v)
            ms_o = jnp.mean(o * o, axis=-1, keepdims=True)
            o_ref[rows, h * RET_DV:(h + 1) * RET_DV] = (o * lax.rsqrt(ms_o + RMS_EPS) * rnw_ref[...]
                                                        * _silu(gate.astype(F32))).astype(o_ref.dtype)


def _proj_retention(x, nw, w, cos, sin, ret_nw, *, tm, tn, tc, tokens_per_seq, name):
    n, d = x.shape
    tiles_per_seq = tokens_per_seq // tm
    v_w = N_HEADS * RET_DV
    return pl.pallas_call(
        functools.partial(_proj_ret_kernel, tn=tn, tc=tc, tiles_per_seq=tiles_per_seq),
        grid=(n // tm,),
        in_specs=[
            pl.BlockSpec((tm, d), lambda i: (i, 0)),
            pl.BlockSpec((1, d), lambda i: (0, 0)),
            pl.BlockSpec(w.shape, lambda i: (0, 0)),
            pl.BlockSpec((tm, RET_DK // 2), lambda i: (i % tiles_per_seq, 0)),
            pl.BlockSpec((tm, RET_DK // 2), lambda i: (i % tiles_per_seq, 0)),
            pl.BlockSpec((1, RET_DV), lambda i: (0, 0)),
        ],
        out_specs=pl.BlockSpec((tm, v_w), lambda i: (i, 0)),
        out_shape=jax.ShapeDtypeStruct((n, v_w), BF16),
        scratch_shapes=[
            pltpu.VMEM((d, 2 * N_HEADS * RET_DK), BF16),
            pltpu.VMEM((tm, w.shape[1]), BF16),
            pltpu.VMEM((N_HEADS, tc, tc), F32),
            pltpu.VMEM((N_HEADS, RET_DK, RET_DV), F32),
        ],
        compiler_params=pltpu.CompilerParams(
            dimension_semantics=("arbitrary",), vmem_limit_bytes=VMEM_LIMIT),
        name=name,
    )(x, nw, w, cos, sin, ret_nw)


def _route_gates(logits_t):
    cl = [logits_t[g:g + 1, :] for g in range(N_GROUPS)]
    cmax = functools.reduce(jnp.maximum, cl)
    denom = sum(jnp.exp(x - cmax) for x in cl)
    g_prob = 1.0 / denom
    g_idx = jnp.full(cmax.shape, N_GROUPS - 1, jnp.int32)
    for g in range(N_GROUPS - 2, -1, -1):
        g_idx = jnp.where(cl[g] == cmax, g, g_idx)
    def fine_row(g, j):
        r = N_GROUPS + g * EXPERTS_PER_GROUP + j
        return logits_t[r:r + 1, :]

    fl = []
    for j in range(EXPERTS_PER_GROUP):
        x = fine_row(N_GROUPS - 1, j)
        for g in range(N_GROUPS - 2, -1, -1):
            x = jnp.where(g_idx == g, fine_row(g, j), x)
        fl.append(x)
    m1 = functools.reduce(jnp.maximum, fl)
    i1 = jnp.full(m1.shape, EXPERTS_PER_GROUP - 1, jnp.int32)
    for j in range(EXPERTS_PER_GROUP - 2, -1, -1):
        i1 = jnp.where(fl[j] == m1, j, i1)
    rest = [jnp.where(i1 == j, -jnp.inf, fl[j]) for j in range(EXPERTS_PER_GROUP)]
    m2 = functools.reduce(jnp.maximum, rest)
    i2 = jnp.full(m2.shape, EXPERTS_PER_GROUP - 1, jnp.int32)
    for j in range(EXPERTS_PER_GROUP - 2, -1, -1):
        i2 = jnp.where(jnp.logical_and(rest[j] == m2, i1 != j), j, i2)
    e2 = jnp.exp(m2 - m1)
    w1 = g_prob / (1.0 + e2)
    w2 = g_prob * e2 / (1.0 + e2)
    local = [jnp.where(i1 == j, w1, 0.0) + jnp.where(i2 == j, w2, 0.0) for j in range(EXPERTS_PER_GROUP)]
    return g_idx, local


def _outproj_kernel(a_ref, b_ref, wa_ref, wb_ref, x_ref, nw_ref, rw_ref, rb_ref,
                    x1_ref, h_ref, gk_ref, krow_ref, cnt_ref):
    x1 = (x_ref[...]
          + jnp.dot(a_ref[...], wa_ref[...], preferred_element_type=F32)
          + jnp.dot(b_ref[...], wb_ref[...], preferred_element_type=F32))
    x1_ref[...] = x1
    ms = jnp.mean(x1 * x1, axis=-1, keepdims=True)
    h = x1 * lax.rsqrt(ms + RMS_EPS) * nw_ref[...]
    h_ref[...] = h.astype(BF16)
    rw = rw_ref[...]
    hh = h.astype(BF16)
    hl = (h - hh.astype(F32)).astype(BF16)
    wh = rw.astype(BF16)
    wl = (rw - wh.astype(F32)).astype(BF16)
    hi_part = jnp.dot(hh, jnp.concatenate([wh, wl], axis=1), preferred_element_type=F32)
    logits = (hi_part[:, :LANES] + hi_part[:, LANES:]
              + jnp.dot(hl, wh, preferred_element_type=F32)) + rb_ref[...]
    g_idx, local = _route_gates(logits.T)
    tm = g_idx.shape[1]
    mem = (_iota2((8, tm), 0) == g_idx).astype(BF16)
    before = (_iota2((MOE_SUB, MOE_SUB), 0) < _iota2((MOE_SUB, MOE_SUB), 1)).astype(BF16)
    earlier = jnp.concatenate(
        [jnp.dot(mem[:, s:s + MOE_SUB], before, preferred_element_type=F32) for s in range(0, tm, MOE_SUB)],
        axis=1)
    rank = jnp.sum(mem.astype(F32) * earlier, axis=0, keepdims=True)
    key = g_idx.astype(F32) * MOE_KEY_STRIDE + rank
    krow_ref[...] = key
    sub_shift = int(math.log2(MOE_SUB))
    sub_sel = ((_iota2((tm, LANES), 0) >> sub_shift) == _iota2((tm, LANES), 1)).astype(BF16)
    cnt_ref[...] = jnp.dot(mem, sub_sel, preferred_element_type=F32)
    rows = jnp.concatenate([key] + local + [jnp.zeros((LANES - 1 - EXPERTS_PER_GROUP, tm), F32)], axis=0)
    gk_ref[...] = rows.T


def _outproj_route(a, b, a_blk, b_blk, w, x, nw, rw, rb, *, tm, name):
    n, d = x.shape
    ka = kb = w.shape[0] // 2
    return pl.pallas_call(
        _outproj_kernel,
        grid=(n // tm,),
        in_specs=[
            pl.BlockSpec((tm, ka), lambda i: (i, a_blk)),
            pl.BlockSpec((tm, kb), lambda i: (i, b_blk)),
            pl.BlockSpec((ka, d), lambda i: (0, 0)),
            pl.BlockSpec((kb, d), lambda i: (1, 0)),
            pl.BlockSpec((tm, d), lambda i: (i, 0)),
            pl.BlockSpec((1, d), lambda i: (0, 0)),
            pl.BlockSpec((d, LANES), lambda i: (0, 0)),
            pl.BlockSpec((1, LANES), lambda i: (0, 0)),
        ],
        out_specs=[
            pl.BlockSpec((tm, d), lambda i: (i, 0)),
            pl.BlockSpec((tm, d), lambda i: (i, 0)),
            pl.BlockSpec((tm, LANES), lambda i: (i, 0)),
            pl.BlockSpec((1, tm), lambda i: (0, i)),
            pl.BlockSpec((8, LANES), lambda i: (i, 0)),
        ],
        out_shape=[jax.ShapeDtypeStruct((n, d), F32),
                   jax.ShapeDtypeStruct((n, d), BF16),
                   jax.ShapeDtypeStruct((n, LANES), F32),
                   jax.ShapeDtypeStruct((1, n), F32),
                   jax.ShapeDtypeStruct((8 * (n // tm), LANES), F32)],
        compiler_params=pltpu.CompilerParams(
            dimension_semantics=("parallel",), vmem_limit_bytes=VMEM_LIMIT),
        name=name,
    )(a, b, w, w, x, nw, rw, rb)


def _round_up(v, m):
    return ((v + (m - 1)) // m) * m


def _round_up_pow2(v, m):
    return (v + (m - 1)) & ~(m - 1)


def _cdiv_pow2(v, m):
    return lax.shift_right_logical(v + (m - 1), int(math.log2(m)))


def _moe_kernel(cnt_ref, h_ref, krow_ref, gk_ref, wgu_ref, wd_ref, x_ref, fnw_ref, o_ref,
                hc_s, gc_s, y_s, *, final_norm):
    i = pl.program_id(0)
    g = pl.program_id(1)
    tm = h_ref.shape[0]
    n_sub = tm // MOE_SUB

    def segments(grp):
        cnts = [cnt_ref[(i * n_sub + s) * N_GROUPS + grp] for s in range(n_sub)]
        starts = [jnp.int32(0)]
        for s in range(n_sub):
            starts.append(starts[-1] + _round_up_pow2(cnts[s], MOE_ALIGN))
        return cnts, starts

    counts, offs = segments(g)
    total = offs[-1]
    key0 = g.astype(F32) * MOE_KEY_STRIDE
    sub_iota = _iota2((MOE_BLK, MOE_SUB), 0).astype(F32)
    lane_iota = _iota2((MOE_SUB, MOE_BLK), 1).astype(F32)

    def pack(s, w):
        rows = slice(s * MOE_SUB, (s + 1) * MOE_SUB)
        base = key0 + (w * MOE_BLK).astype(F32)
        sel = (krow_ref[:, rows] == sub_iota + base).astype(BF16)
        dst = pl.ds(pl.multiple_of(offs[s] + w * MOE_BLK, MOE_ALIGN), MOE_BLK)
        hc_s[dst, :] = jnp.dot(sel, h_ref[rows, :], preferred_element_type=F32).astype(BF16)
        gc_s[dst, :] = _mm_01(sel, gk_ref[rows, :])

    for s in range(n_sub):
        def pack_more(w, carry, s=s):
            pack(s, w)
            return carry

        lax.fori_loop(1, _cdiv_pow2(counts[s], MOE_BLK), pack_more, 0)
    for s in range(n_sub):
        pack(s, jnp.int32(0))

    tail_rows = MOE_FIRST[-1] + MOE_FIRST_STEP
    tail = pl.ds(pl.multiple_of(total, MOE_ALIGN), tail_rows)
    hc_s[tail, :] = jnp.zeros((tail_rows, hc_s.shape[1]), BF16)
    gc_s[tail, :] = jnp.zeros((tail_rows, LANES), F32)

    def expert_rows(start, rows):
        blk = pl.ds(pl.multiple_of(start, MOE_FIRST_STEP), rows)
        hb = hc_s[blk, :]
        gates = gc_s[blk, :]
        y = jnp.zeros((rows, o_ref.shape[1]), F32)
        for e in range(EXPERTS_PER_GROUP):
            gu = jnp.dot(hb, wgu_ref[0, e], preferred_element_type=F32)
            act = _silu(gu[:, :D_EXPERT]) * gu[:, D_EXPERT:] * gates[:, 1 + e:2 + e]
            y = y + jnp.dot(act.astype(BF16), wd_ref[0, e], preferred_element_type=F32)
        y_s[g, blk, :] = y.astype(BF16)

    first = jnp.clip(_round_up_pow2(total, MOE_FIRST_STEP), MOE_FIRST[0], MOE_FIRST[-1])
    for size in MOE_FIRST:
        @pl.when(first == size)
        def _(size=size):
            expert_rows(0, size)

    n_rest = _cdiv_pow2(jnp.maximum(total - first, 0), MOE_BLK)

    def rest(bi, carry):
        expert_rows(first + bi * MOE_BLK, MOE_BLK)
        return carry

    lax.fori_loop(0, n_rest, rest, 0)
    done = pl.ds(pl.multiple_of(first + n_rest * MOE_BLK, MOE_FIRST_STEP), MOE_BLK)
    y_s[g, done, :] = jnp.zeros((MOE_BLK, y_s.shape[2]), BF16)

    @pl.when(g == N_GROUPS - 1)
    def _():
        segs = [segments(grp) for grp in range(N_GROUPS)]
        for s in range(n_sub):
            rows = slice(s * MOE_SUB, (s + 1) * MOE_SUB)
            key_col = gk_ref[rows, 0:1]
            sel = jnp.concatenate(
                [(key_col == lane_iota + grp * MOE_KEY_STRIDE).astype(BF16) for grp in range(N_GROUPS)], axis=1)
            packed = jnp.concatenate(
                [y_s[grp, pl.ds(pl.multiple_of(segs[grp][1][s], MOE_ALIGN), MOE_BLK), :]
                 for grp in range(N_GROUPS)], axis=0)
            o_ref[rows, :] = x_ref[rows, :] + jnp.dot(sel, packed, preferred_element_type=F32)
            for grp in range(N_GROUPS):
                def more(w, carry, s=s, rows=rows, grp=grp, key_col=key_col):
                    base = grp * MOE_KEY_STRIDE + (w * MOE_BLK).astype(F32)
                    sel_w = (key_col == lane_iota + base).astype(BF16)
                    src = pl.ds(pl.multiple_of(segs[grp][1][s] + w * MOE_BLK, MOE_ALIGN), MOE_BLK)
                    o_ref[rows, :] += jnp.dot(sel_w, y_s[grp, src, :], preferred_element_type=F32)
                    return carry

                lax.fori_loop(1, _cdiv_pow2(segs[grp][0][s], MOE_BLK), more, 0)
        if final_norm:
            y = o_ref[...]
            ms = jnp.mean(y * y, axis=-1, keepdims=True)
            o_ref[...] = y * lax.rsqrt(ms + RMS_EPS) * fnw_ref[...]


def _moe(h, krow, gk, counts, wgu, wd, x, fnw, *, layer, tm, final_norm, name):
    n, d = x.shape
    buf_rows = _round_up(tm + (tm // MOE_SUB) * MOE_ALIGN + MOE_FIRST[-1] + MOE_FIRST_STEP, MOE_BLK)
    grid_spec = pltpu.PrefetchScalarGridSpec(
        num_scalar_prefetch=1,
        grid=(n // tm, N_GROUPS),
        in_specs=[
            pl.BlockSpec((tm, d), lambda i, g, c: (i, 0)),
            pl.BlockSpec((1, tm), lambda i, g, c: (0, i)),
            pl.BlockSpec((tm, LANES), lambda i, g, c: (i, 0)),
            pl.BlockSpec((1, EXPERTS_PER_GROUP, d, 2 * D_EXPERT), lambda i, g, c: (layer, g, 0, 0)),
            pl.BlockSpec((1, EXPERTS_PER_GROUP, D_EXPERT, d), lambda i, g, c: (layer, g, 0, 0)),
            pl.BlockSpec((tm, d), lambda i, g, c: (i, 0)),
            pl.BlockSpec((1, d), lambda i, g, c: (0, 0)),
        ],
        out_specs=pl.BlockSpec((tm, d), lambda i, g, c: (i, 0)),
        scratch_shapes=[pltpu.VMEM((buf_rows, d), BF16),
                        pltpu.VMEM((buf_rows, LANES), F32),
                        pltpu.VMEM((N_GROUPS, buf_rows, d), BF16)],
    )
    return pl.pallas_call(
        functools.partial(_moe_kernel, final_norm=final_norm),
        grid_spec=grid_spec,
        out_shape=jax.ShapeDtypeStruct((n, d), F32),
        compiler_params=pltpu.CompilerParams(
            dimension_semantics=("parallel", "arbitrary"), vmem_limit_bytes=VMEM_LIMIT),
        name=name,
    )(counts, h, krow, gk, wgu, wd, x, fnw)


def _pad_cols(a, width):
    return jnp.pad(a, ((0, 0), (0, width - a.shape[1])))


def _dispatch_counts(cnt, route_tm):
    c = cnt.reshape(-1, 8, LANES)[:, :N_GROUPS, :route_tm // MOE_SUB]
    return jnp.transpose(c, (0, 2, 1)).reshape(-1).astype(jnp.int32)


def _rope_tables(seq):
    inv = (1.0 / (ROPE_BASE ** np.linspace(0.0, 1.0, RET_DK // 2, dtype=np.float32))).astype(np.float32)
    ang = np.arange(seq, dtype=np.float32)[:, None] * inv[None, :]
    return jnp.asarray(np.cos(ang), dtype=F32), jnp.asarray(np.sin(ang), dtype=F32)


def _router_params(wc, bc, wf, bf):
    rw = _pad_cols(jnp.concatenate([wc, wf], axis=1), LANES)
    rb = _pad_cols(jnp.concatenate([bc, bf])[None, :], LANES)
    return rw, rb


def kernel(x, norm_mix_w, norm_ffn_w, even_w_in, gdn_conv_w, gdn_a_log, gdn_dt_bias, gdn_norm_w, hgrn_lb_logits, hgrn_norm_w, even_w_out, odd_w_in, ret_norm_w, odd_w_out, router_c_w, router_c_b, router_f_w, router_f_b, moe_w_gate_up, moe_w_down, final_norm_w):
    bsz, seq, d = x.shape
    n = bsz * seq
    xt = x.reshape(n, d)
    mix_w = N_HEADS * HEAD_D
    conv_cols = 3 * mix_w
    gdn_main = conv_cols + mix_w

    w_in = even_w_in[0]
    small0 = gdn_main
    w_small = jnp.concatenate(
        [_pad_cols(w_in[:, small0:small0 + N_HEADS], LANES),
         _pad_cols(w_in[:, small0 + N_HEADS:small0 + 2 * N_HEADS], LANES),
         _pad_cols(w_in[:, small0 + 4 * mix_w:], LANES)], axis=1).astype(BF16)
    proj, small, o_b, wgu_b, wd_b, w_out, w_odd, w_out1 = _proj_even(
        xt, norm_mix_w[0][None, :], w_in.astype(BF16), w_small, hgrn_lb_logits, hgrn_norm_w[0][None, :],
        (moe_w_gate_up.reshape(-1, moe_w_gate_up.shape[-1]), moe_w_down.reshape(-1, moe_w_down.shape[-1]),
         even_w_out[0], odd_w_in[0], odd_w_out[0]),
        gdn_w=gdn_main, tm=512, tn=1024, tokens_per_seq=seq, layer=0, name="in_proj_even_hgrn2")
    wgu_b = wgu_b.reshape(moe_w_gate_up.shape)
    wd_b = wd_b.reshape(moe_w_down.shape)
    proj = proj.reshape(bsz, seq, -1)
    small = small.reshape(bsz, seq, -1)
    o_a = _gdn(proj, small, gdn_conv_w[0], _pad_cols(gdn_a_log[0][None, :], LANES),
               _pad_cols(gdn_dt_bias[0][None, :], LANES), gdn_norm_w[0][None, :], tb=512)
    rw, rb = _router_params(router_c_w[0], router_c_b[0], router_f_w[0], router_f_b[0])
    x1, h, gk, krow, cnt = _outproj_route(o_a.reshape(n, mix_w), o_b.reshape(n, mix_w), 0, 0,
                                          w_out, xt, norm_ffn_w[0][None, :], rw, rb,
                                          tm=ROUTE_TM, name="out_proj_even")
    x2 = _moe(h, krow, gk, _dispatch_counts(cnt, ROUTE_TM), wgu_b, wd_b, x1,
              final_norm_w[None, :], layer=0, tm=MOE_TM, final_norm=False, name="moe0")

    cos, sin = _rope_tables(seq)
    o_c = _proj_retention(x2, norm_mix_w[1][None, :], w_odd, cos, sin, ret_norm_w[0][None, :],
                          tm=512, tn=1024, tc=RET_CHUNK, tokens_per_seq=seq, name="in_proj_retention")
    rw, rb = _router_params(router_c_w[1], router_c_b[1], router_f_w[1], router_f_b[1])
    x3, h, gk, krow, cnt = _outproj_route(o_c, o_c, 0, 1, w_out1, x2,
                                          norm_ffn_w[1][None, :], rw, rb, tm=ROUTE_TM, name="out_proj_odd")
    out = _moe(h, krow, gk, _dispatch_counts(cnt, ROUTE_TM), wgu_b, wd_b, x3,
               final_norm_w[None, :], layer=1, tm=MOE_TM, final_norm=True, name="moe1")
    return out.reshape(bsz, seq, d)
```

```python
import functools
import math

import jax
import jax.numpy as jnp
import numpy as np
from jax import lax
from jax.experimental import pallas as pl
from jax.experimental.pallas import tpu as pltpu

F32 = jnp.float32
BF16 = jnp.bfloat16

D_MODEL = 1024
RMS_EPS = 1e-6
L2_EPS = 1e-6
CHUNK = 64
CONV_K = 4
N_HEADS = 4
HEAD_D = 128
RET_DK = 256
RET_DV = 512
RET_CHUNK = 256
ROPE_BASE = 10000.0
N_GROUPS = 4
EXPERTS_PER_GROUP = 4
N_EXPERTS = 16
D_EXPERT = 256
LANES = 128
ROUTE_TM = 1024
MOE_TM = 1024
MOE_SUB = 256
MOE_BLK = 128
MOE_ALIGN = 16
MOE_KEY_STRIDE = 4096.0
MOE_FIRST_STEP = 32
MOE_FIRST = (256, 288, 320, 352)
VMEM_LIMIT = 56 * 1024 * 1024


def _mm(a, b):
    return jnp.dot(a.astype(BF16), b.astype(BF16), preferred_element_type=F32)


def _mm_nt(a, b):
    return lax.dot_general(a.astype(BF16), b.astype(BF16), (((1,), (1,)), ((), ())),
                           preferred_element_type=F32)


def _mm_tn(a, b):
    return _mm(a.T, b)


def _mm_01(m01, x):
    hi = x.astype(BF16)
    lo = (x - hi.astype(F32)).astype(BF16)
    return (jnp.dot(m01, hi, preferred_element_type=F32)
            + jnp.dot(m01, lo, preferred_element_type=F32))


def _bmm(a, b):
    return jnp.einsum('cik,ckj->cij', a.astype(BF16), b.astype(BF16), preferred_element_type=F32)


def _bmm_nt(a, b):
    return jnp.einsum('cik,cjk->cij', a.astype(BF16), b.astype(BF16), preferred_element_type=F32)


def _bmm_tn(a, b):
    return _bmm(jnp.swapaxes(a, 1, 2), b)


def _bmm_01(m01, x):
    hi = x.astype(BF16)
    lo = (x - hi.astype(F32)).astype(BF16)
    return (jnp.einsum('cik,ckj->cij', m01, hi, preferred_element_type=F32)
            + jnp.einsum('cik,ckj->cij', m01, lo, preferred_element_type=F32))


def _sigmoid(x):
    return 1.0 / (1.0 + jnp.exp2(x * (-math.log2(math.e))))


def _silu(x):
    return x * _sigmoid(x)


def _softplus(x):
    return jnp.maximum(x, 0.0) + jnp.log(1.0 + jnp.exp(-jnp.abs(x)))


def _iota2(shape, dim):
    return lax.broadcasted_iota(jnp.int32, shape, dim)


def _head_rms_gate(o, gate, nw, width):
    outs = []
    for h in range(o.shape[1] // width):
        oh = o[:, h * width:(h + 1) * width]
        ms = jnp.mean(oh * oh, axis=-1, keepdims=True)
        outs.append(oh * lax.rsqrt(ms + RMS_EPS) * nw)
    return jnp.concatenate(outs, axis=1) * _silu(gate)


def _proj_even_kernel(*refs, tn, n_cast, tiles_per_seq, layer, shift):
    (x_ref, nw_ref, wg_ref, wh_ref, ws_ref, lb_ref, hnw_ref), rest = refs[:7], refs[7:]
    cast_in, rest = rest[:n_cast], rest[n_cast:]
    (o_ref, os_ref, ob_ref), rest = rest[:3], rest[3:]
    cast_out, rest = rest[:n_cast], rest[n_cast:]
    hg_s, whg_s, q_s, k_s, lf_s, i_s, oh_s, st_ref = rest
    for src, dst in zip(cast_in, cast_out):
        dst[...] = src[...].astype(dst.dtype)

    @pl.when(pl.program_id(0) == 0)
    def _():
        sel = (_iota2((2 * LANES, LANES), 0) == _iota2((2 * LANES, LANES), 1) + shift).astype(BF16)
        n_blocks = wh_ref.shape[1] // LANES
        for j in range(n_blocks):
            if j + 1 < n_blocks:
                both = wh_ref[:, j * LANES:(j + 2) * LANES]
            else:
                both = jnp.concatenate([wh_ref[:, j * LANES:], ws_ref[:, 2 * LANES:]], axis=1)
            whg_s[:, j * LANES:(j + 1) * LANES] = jnp.dot(both, sel, preferred_element_type=F32).astype(BF16)

    @pl.when(pl.program_id(0) % tiles_per_seq == 0)
    def _():
        st_ref[...] = jnp.zeros(st_ref.shape, F32)

    x = x_ref[...]
    tm = x.shape[0]
    ms = jnp.mean(x * x, axis=-1, keepdims=True)
    hb = (x * lax.rsqrt(ms + RMS_EPS) * nw_ref[...]).astype(BF16)
    os_ref[...] = jnp.dot(hb, ws_ref[:, :2 * LANES], preferred_element_type=F32)
    for j in range(whg_s.shape[1] // tn):
        cols = slice(j * tn, (j + 1) * tn)
        hg_s[:, cols] = jnp.dot(hb, whg_s[:, cols], preferred_element_type=F32)
    for j in range(wg_ref.shape[1] // tn):
        cols = slice(j * tn, (j + 1) * tn)
        o_ref[:, cols] = jnp.dot(hb, wg_ref[:, cols], preferred_element_type=F32).astype(o_ref.dtype)
    w = N_HEADS * HEAD_D
    ob_ref[...] = _hgrn_block(hg_s[:, 0:w], hg_s[:, w:2 * w], hg_s[:, 2 * w:3 * w], hg_s[:, 3 * w:4 * w],
                              lb_ref, hnw_ref, q_s, k_s, lf_s, i_s, oh_s, st_ref,
                              tb=tm, layer=layer).astype(ob_ref.dtype)


def _proj_even(x, nw, w, w_small, lb_logits, hgrn_nw, cast, *, gdn_w, tm, tn, tokens_per_seq, layer, name):
    n, d = x.shape
    steps = n // tm
    mix_w = N_HEADS * HEAD_D
    hgrn_w = 4 * mix_w
    shift = w.shape[1] - gdn_w - hgrn_w
    assert gdn_w == hgrn_w and 0 < shift < LANES and w_small.shape[1] == 3 * LANES
    const = lambda i: (0, 0)
    tile = lambda i: (i, 0)
    in_specs = [pl.BlockSpec((tm, d), tile), pl.BlockSpec((1, d), const),
                pl.BlockSpec((d, gdn_w), const), pl.BlockSpec((d, hgrn_w), lambda i: (0, 1)),
                pl.BlockSpec(w_small.shape, const), pl.BlockSpec(lb_logits.shape, const),
                pl.BlockSpec((1, HEAD_D), const)]
    out_specs = [pl.BlockSpec((tm, gdn_w), tile), pl.BlockSpec((tm, 2 * LANES), tile),
                 pl.BlockSpec((tm, mix_w), tile)]
    out_shape = [jax.ShapeDtypeStruct((n, gdn_w), BF16), jax.ShapeDtypeStruct((n, 2 * LANES), F32),
                 jax.ShapeDtypeStruct((n, mix_w), BF16)]
    for a in cast:
        slab = (a.shape[0] // steps, a.shape[1])
        in_specs.append(pl.BlockSpec(slab, tile))
        out_specs.append(pl.BlockSpec(slab, tile))
        out_shape.append(jax.ShapeDtypeStruct(a.shape, BF16))
    scratch = ([pltpu.VMEM((tm, hgrn_w), F32), pltpu.VMEM((d, hgrn_w), BF16)] + [pltpu.VMEM((tm, mix_w), F32)] * 5
               + [pltpu.VMEM((N_HEADS, HEAD_D, HEAD_D), F32)])
    return pl.pallas_call(
        functools.partial(_proj_even_kernel, tn=tn, n_cast=len(cast), tiles_per_seq=tokens_per_seq // tm,
                          layer=layer, shift=shift),
        grid=(steps,),
        in_specs=in_specs,
        out_specs=out_specs,
        out_shape=out_shape,
        scratch_shapes=scratch,
        compiler_params=pltpu.CompilerParams(
            dimension_semantics=("arbitrary",), vmem_limit_bytes=VMEM_LIMIT),
        name=name,
    )(x, nw, w, w, w_small, lb_logits, hgrn_nw, *cast)


def _gdn_kernel(qkv_ref, z_ref, sm_ref, cw_ref, alog_ref, dtb_ref, nw_ref, o_ref,
                xs_ref, qkv_s, o_s, m_s, sq_s, qe_s, dec_s, st_ref, *, tb):
    @pl.when(pl.program_id(1) == 0)
    def _():
        xs_ref[0:8, :] = jnp.zeros((8, xs_ref.shape[1]), F32)
        st_ref[...] = jnp.zeros(st_ref.shape, F32)

    c = CHUNK
    nc = tb // c
    nb = N_HEADS * nc
    qk_w = N_HEADS * HEAD_D
    row = _iota2((c, c), 0)
    col = _iota2((c, c), 1)
    causal = (row >= col)[None]
    strict = (row > col)[None]
    same16 = ((row >> 4) == (col >> 4))[None]
    same32 = ((row >> 5) == (col >> 5))[None]
    eye = (row == col).astype(F32)[None]
    ltri = jnp.broadcast_to((row >= col).astype(BF16)[None], (nc, c, c))
    lane = _iota2((nb, c, LANES), 2)

    def per_head(fn):
        return jnp.concatenate([fn(h) for h in range(N_HEADS)], axis=0)

    for h in range(N_HEADS):
        for base, scale in ((0, HEAD_D ** -0.5), (qk_w, 1.0), (2 * qk_w, None)):
            cols = slice(base + h * HEAD_D, base + (h + 1) * HEAD_D)
            x = qkv_ref[0, :, cols].astype(F32)
            xs_ref[8:8 + tb, cols] = x
            cw = cw_ref[:, cols]
            y = x * cw[CONV_K - 1:CONV_K, :]
            for j in range(1, CONV_K):
                y = y + xs_ref[pl.ds(8 - j, tb), cols] * cw[CONV_K - 1 - j:CONV_K - j, :]
            y = _silu(y)
            if scale is not None:
                y = y * (lax.rsqrt(jnp.sum(y * y, axis=-1, keepdims=True) + L2_EPS) * scale)
            qkv_s[:, cols] = y
    xs_ref[0:8, :] = xs_ref[tb:tb + 8, :]

    sm = sm_ref[0]
    beta_all = _sigmoid(sm[:, :LANES]).reshape(nc, c, LANES)
    rate = -jnp.exp(alog_ref[...]) * math.log2(math.e)
    g_all = (rate * _softplus(sm[:, LANES:] + dtb_ref[...])).reshape(nc, c, LANES)
    gc_all = _bmm_01(ltri, g_all)
    q = per_head(lambda h: qkv_s[:, h * HEAD_D:(h + 1) * HEAD_D].reshape(nc, c, HEAD_D))
    k = per_head(lambda h: qkv_s[:, qk_w + h * HEAD_D:qk_w + (h + 1) * HEAD_D].reshape(nc, c, HEAD_D))
    v = per_head(lambda h: qkv_s[:, 2 * qk_w + h * HEAD_D:2 * qk_w + (h + 1) * HEAD_D].reshape(nc, c, HEAD_D))
    beta = per_head(lambda h: beta_all[:, :, h:h + 1])
    gcol = per_head(lambda h: gc_all[:, :, h:h + 1])
    g_hi = gcol.astype(BF16).astype(F32)
    g_mid = (gcol - g_hi).astype(BF16).astype(F32)
    g_lo = gcol - g_hi - g_mid
    pieces = jnp.where(lane == 0, g_hi, jnp.where(lane == 1, g_mid, jnp.where(lane == 2, g_lo, 0.0)))
    lane1 = _iota2((1, 1, LANES), 2)
    lhs = pieces + jnp.where(jnp.logical_and(lane1 >= 3, lane1 < 6), 1.0, 0.0)
    rhs_g = jnp.where(lane1 < 3, 1.0, 0.0) - pltpu.roll(pieces, 3, axis=2)
    decay = jnp.exp2(jnp.where(causal, _bmm_nt(lhs, rhs_g), -jnp.inf))
    kb = k * beta
    a = jnp.where(strict, _bmm_nt(kb, k) * decay, 0.0)
    d = jnp.where(same16, a, 0.0)
    x_inv = eye - d
    dp = d
    for _ in range(3):
        dp = _bmm(dp, dp)
        x_inv = x_inv + _bmm(x_inv, dp)
    e = jnp.where(jnp.logical_and(same32, jnp.logical_not(same16)), a, 0.0)
    x_inv = x_inv - _bmm(x_inv, _bmm(e, x_inv))
    f = jnp.where(same32, 0.0, a)
    x_inv = x_inv - _bmm(x_inv, _bmm(f, x_inv))
    wu = _bmm(x_inv, jnp.concatenate([kb * jnp.exp2(gcol), v * beta], axis=2))
    attn = _bmm_nt(q, k) * decay
    g_last = gcol[:, c - 1:c, :]
    kd = k * jnp.exp2(g_last - gcol)
    mq = _bmm_tn(kd, wu)
    aw = _bmm(attn, wu)
    m_s[...] = mq[:, :, :HEAD_D].astype(BF16).reshape(N_HEADS, nc, HEAD_D, HEAD_D)
    sq_s[...] = mq[:, :, HEAD_D:].reshape(N_HEADS, nc, HEAD_D, HEAD_D)
    qe_s[...] = (q * jnp.exp2(gcol) - aw[:, :, :HEAD_D]).astype(BF16).reshape(N_HEADS, nc, c, HEAD_D)
    dec_s[...] = jnp.broadcast_to(jnp.exp2(g_last), (nb, 1, HEAD_D)).reshape(N_HEADS, nc, 1, HEAD_D)
    for h in range(N_HEADS):
        o_s[:, h * HEAD_D:(h + 1) * HEAD_D] = aw[h * nc:(h + 1) * nc, :, HEAD_D:].reshape(tb, HEAD_D)

    for ci in range(nc):
        for h in range(N_HEADS):
            hs = slice(h * HEAD_D, (h + 1) * HEAD_D)
            s_h = st_ref[h]
            s_b = s_h.astype(BF16)
            o_s[ci * c:(ci + 1) * c, hs] += jnp.dot(qe_s[h, ci], s_b, preferred_element_type=F32)
            st_ref[h] = (s_h * dec_s[h, ci] - jnp.dot(m_s[h, ci], s_b, preferred_element_type=F32)
                         + sq_s[h, ci])

    o_ref[0] = _head_rms_gate(o_s[...], z_ref[0].astype(F32), nw_ref[...], HEAD_D).astype(o_ref.dtype)


def _gdn(proj, small, conv_w, a_log, dt_bias, norm_w, *, tb):
    b, t, _ = proj.shape
    conv_cols = 3 * N_HEADS * HEAD_D
    mix_w = N_HEADS * HEAD_D
    return pl.pallas_call(
        functools.partial(_gdn_kernel, tb=tb),
        grid=(b, t // tb),
        in_specs=[
            pl.BlockSpec((1, tb, conv_cols), lambda i, j: (i, j, 0)),
            pl.BlockSpec((1, tb, mix_w), lambda i, j: (i, j, conv_cols // mix_w)),
            pl.BlockSpec((1, tb, 2 * LANES), lambda i, j: (i, j, 0)),
            pl.BlockSpec((CONV_K, conv_cols), lambda i, j: (0, 0)),
            pl.BlockSpec((1, LANES), lambda i, j: (0, 0)),
            pl.BlockSpec((1, LANES), lambda i, j: (0, 0)),
            pl.BlockSpec((1, HEAD_D), lambda i, j: (0, 0)),
        ],
        out_specs=pl.BlockSpec((1, tb, mix_w), lambda i, j: (i, j, 0)),
        out_shape=jax.ShapeDtypeStruct((b, t, mix_w), BF16),
        scratch_shapes=[
            pltpu.VMEM((tb + 8, conv_cols), F32),
            pltpu.VMEM((tb, conv_cols), F32),
            pltpu.VMEM((tb, mix_w), F32),
            pltpu.VMEM((N_HEADS, tb // CHUNK, HEAD_D, HEAD_D), BF16),
            pltpu.VMEM((N_HEADS, tb // CHUNK, HEAD_D, HEAD_D), F32),
            pltpu.VMEM((N_HEADS, tb // CHUNK, CHUNK, HEAD_D), BF16),
            pltpu.VMEM((N_HEADS, tb // CHUNK, 1, HEAD_D), F32),
            pltpu.VMEM((N_HEADS, HEAD_D, HEAD_D), F32),
        ],
        compiler_params=pltpu.CompilerParams(
            dimension_semantics=("parallel", "arbitrary"), vmem_limit_bytes=VMEM_LIMIT),
        name="gdn",
    )(proj, proj, small, conv_w, a_log, dt_bias, norm_w)


def _hgrn_block(q_in, f_in, i_in, gate_in, lb_ref, nw_ref, q_s, k_s, lf_s, i_s, o_s, st_ref, *, tb, layer):
    lbl = lb_ref[...]
    e_lb = jnp.exp(lbl - jnp.max(lbl, axis=0, keepdims=True))
    lb = jnp.sum(e_lb[:layer + 1], axis=0, keepdims=True) / jnp.sum(e_lb, axis=0, keepdims=True)

    f = lb + (1.0 - lb) * _sigmoid(f_in)
    k_s[...] = 1.0 - f
    lf_s[...] = jnp.log2(f)
    q_s[...] = _silu(q_in) * (HEAD_D ** -0.5)
    i_s[...] = i_in

    c = CHUNK
    nc = tb // c
    blk = 8
    row = _iota2((c, c), 0)
    col = _iota2((c, c), 1)
    ltri = jnp.broadcast_to((row >= col).astype(BF16)[None], (nc, c, c))
    level_masks = {}
    for m in (32, 16, 8):
        sh = int(math.log2(2 * m))
        level_masks[m] = jnp.logical_and(
            (row >> sh) == (col >> sh),
            jnp.logical_and((row & (2 * m - 1)) >= m, (col & (2 * m - 1)) < m))[None]
    nb = N_HEADS * nc
    sub = _iota2((nb * c // blk, blk, HEAD_D), 1)

    def per_head(fn):
        return jnp.concatenate([fn(slice(h * HEAD_D, (h + 1) * HEAD_D)) for h in range(N_HEADS)], axis=0)

    b_all = _bmm_01(ltri, lf_s[...].reshape(nc, c, N_HEADS * HEAD_D))
    b = per_head(lambda sl: b_all[:, :, sl])
    q = per_head(lambda sl: q_s[:, sl].reshape(nc, c, HEAD_D))
    k = per_head(lambda sl: k_s[:, sl].reshape(nc, c, HEAD_D))
    iv = per_head(lambda sl: i_s[:, sl].reshape(nc, c, HEAD_D))
    attn = jnp.zeros((nb, c, c), F32)
    for m in (32, 16, 8):
        b_m = b.reshape(nb * c // (2 * m), 2 * m, HEAD_D)
        ref = jnp.broadcast_to(b_m[:, m:m + 1, :], b_m.shape).reshape(nb, c, HEAD_D)
        e = jnp.exp2(-jnp.abs(b - ref))
        attn = attn + jnp.where(level_masks[m], _bmm_nt(q * e, k * e), 0.0)
    o = _bmm(attn, iv)
    qb, kb, bb, ib = (a.reshape(nb * c // blk, blk, HEAD_D) for a in (q, k, b, iv))
    ob = jnp.zeros(qb.shape, F32)
    for s in range(blk):
        dec = jnp.exp2(jnp.where(sub >= s, bb - bb[:, s:s + 1, :], -jnp.inf))
        a_col = jnp.sum(dec * qb * kb[:, s:s + 1, :], axis=-1, keepdims=True)
        ob = ob + a_col * ib[:, s:s + 1, :]
    o = o + ob.reshape(nb, c, HEAD_D)
    b_last = b[:, c - 1:c, :]
    q_in = (q * jnp.exp2(b)).astype(BF16)
    upd = _bmm_tn(iv, k * jnp.exp2(b_last - b))
    keep = jnp.exp2(b_last)
    for h in range(N_HEADS):
        s_t = st_ref[h]
        outs = []
        for ci in range(h * nc, (h + 1) * nc):
            outs.append(o[ci] + _mm_nt(q_in[ci], s_t))
            s_t = s_t * keep[ci] + upd[ci]
        st_ref[h] = s_t
        o_s[:, h * HEAD_D:(h + 1) * HEAD_D] = jnp.concatenate(outs, axis=0)

    return _head_rms_gate(o_s[...], gate_in, nw_ref[...], HEAD_D)


def _proj_ret_kernel(x_ref, nw_ref, w_ref, cos_ref, sin_ref, rnw_ref, o_ref,
                     wq_s, proj_s, dmat_s, st_ref, *, tn, tc, tiles_per_seq):
    i = pl.program_id(0)
    qk_w = N_HEADS * RET_DK
    v_w = N_HEADS * RET_DV
    half = RET_DK // 2
    log_gammas = [math.log(1.0 - 2.0 ** (-5.0 - h)) for h in range(N_HEADS)]

    @pl.when(i == 0)
    def _():
        src = _iota2((RET_DK, RET_DK), 0)
        dst = _iota2((RET_DK, RET_DK), 1)
        split = (src == jnp.where(dst < half, 2 * dst, 2 * (dst - half) + 1)).astype(BF16)
        for hd in range(2 * N_HEADS):
            cols = slice(hd * RET_DK, (hd + 1) * RET_DK)
            wq_s[:, cols] = jnp.dot(w_ref[:, cols], split, preferred_element_type=F32).astype(BF16)
        rel = (_iota2((tc, tc), 0) - _iota2((tc, tc), 1)).astype(F32)
        for h in range(N_HEADS):
            dmat_s[h] = jnp.where(rel >= 0, jnp.exp(jnp.maximum(rel, 0.0) * log_gammas[h]), 0.0)

    @pl.when(i % tiles_per_seq == 0)
    def _():
        st_ref[...] = jnp.zeros(st_ref.shape, F32)

    x = x_ref[...]
    tm = x.shape[0]
    ms = jnp.mean(x * x, axis=-1, keepdims=True)
    hb = (x * lax.rsqrt(ms + RMS_EPS) * nw_ref[...]).astype(BF16)
    for j in range(w_ref.shape[1] // tn):
        cols = slice(j * tn, (j + 1) * tn)
        w_tile = wq_s[:, cols] if (j + 1) * tn <= 2 * qk_w else w_ref[:, cols]
        proj_s[:, cols] = jnp.dot(hb, w_tile, preferred_element_type=F32).astype(BF16)

    pos = _iota2((tc, 1), 0).astype(F32)
    for ck in range(tm // tc):
        rows = slice(ck * tc, (ck + 1) * tc)
        cos = cos_ref[rows, :]
        sin = sin_ref[rows, :]

        def rot(xh):
            x1, x2 = xh[:, :half], xh[:, half:]
            return jnp.concatenate([x1 * cos - x2 * sin, x1 * sin + x2 * cos], axis=1)

        for h in range(N_HEADS):
            lg = log_gammas[h]
            q = rot(proj_s[rows, h * RET_DK:(h + 1) * RET_DK].astype(F32))
            k = rot(proj_s[rows, qk_w + h * RET_DK:qk_w + (h + 1) * RET_DK].astype(F32)) * (RET_DK ** -0.5)
            v = proj_s[rows, 2 * qk_w + h * RET_DV:2 * qk_w + (h + 1) * RET_DV]
            gate = proj_s[rows, 2 * qk_w + v_w + h * RET_DV:2 * qk_w + v_w + (h + 1) * RET_DV]
            s = st_ref[h]
            attn = _mm_nt(q, k) * dmat_s[h]
            o = _mm(q * jnp.exp((pos + 1.0) * lg), s) + _mm(attn, v)
            st_ref[h] = s * math.exp(tc * lg) + _mm_tn(k * jnp.exp((tc - 1.0 - pos) * lg), v)
            ms_o = jnp.mean(o * o, axis=-1, keepdims=True)
            o_ref[rows, h * RET_DV:(h + 1) * RET_DV] = (o * lax.rsqrt(ms_o + RMS_EPS) * rnw_ref[...]
                                                        * _silu(gate.astype(F32))).astype(o_ref.dtype)


def _proj_retention(x, nw, w, cos, sin, ret_nw, *, tm, tn, tc, tokens_per_seq, name):
    n, d = x.shape
    tiles_per_seq = tokens_per_seq // tm
    v_w = N_HEADS * RET_DV
    return pl.pallas_call(
        functools.partial(_proj_ret_kernel, tn=tn, tc=tc, tiles_per_seq=tiles_per_seq),
        grid=(n // tm,),
        in_specs=[
            pl.BlockSpec((tm, d), lambda i: (i, 0)),
            pl.BlockSpec((1, d), lambda i: (0, 0)),
            pl.BlockSpec(w.shape, lambda i: (0, 0)),
            pl.BlockSpec((tm, RET_DK // 2), lambda i: (i % tiles_per_seq, 0)),
            pl.BlockSpec((tm, RET_DK // 2), lambda i: (i % tiles_per_seq, 0)),
            pl.BlockSpec((1, RET_DV), lambda i: (0, 0)),
        ],
        out_specs=pl.BlockSpec((tm, v_w), lambda i: (i, 0)),
        out_shape=jax.ShapeDtypeStruct((n, v_w), BF16),
        scratch_shapes=[
            pltpu.VMEM((d, 2 * N_HEADS * RET_DK), BF16),
            pltpu.VMEM((tm, w.shape[1]), BF16),
            pltpu.VMEM((N_HEADS, tc, tc), F32),
            pltpu.VMEM((N_HEADS, RET_DK, RET_DV), F32),
        ],
        compiler_params=pltpu.CompilerParams(
            dimension_semantics=("arbitrary",), vmem_limit_bytes=VMEM_LIMIT),
        name=name,
    )(x, nw, w, cos, sin, ret_nw)


def _route_gates(logits_t):
    cl = [logits_t[g:g + 1, :] for g in range(N_GROUPS)]
    cmax = functools.reduce(jnp.maximum, cl)
    denom = sum(jnp.exp(x - cmax) for x in cl)
    g_prob = 1.0 / denom
    g_idx = jnp.full(cmax.shape, N_GROUPS - 1, jnp.int32)
    for g in range(N_GROUPS - 2, -1, -1):
        g_idx = jnp.where(cl[g] == cmax, g, g_idx)
    def fine_row(g, j):
        r = N_GROUPS + g * EXPERTS_PER_GROUP + j
        return logits_t[r:r + 1, :]

    fl = []
    for j in range(EXPERTS_PER_GROUP):
        x = fine_row(N_GROUPS - 1, j)
        for g in range(N_GROUPS - 2, -1, -1):
            x = jnp.where(g_idx == g, fine_row(g, j), x)
        fl.append(x)
    m1 = functools.reduce(jnp.maximum, fl)
    i1 = jnp.full(m1.shape, EXPERTS_PER_GROUP - 1, jnp.int32)
    for j in range(EXPERTS_PER_GROUP - 2, -1, -1):
        i1 = jnp.where(fl[j] == m1, j, i1)
    rest = [jnp.where(i1 == j, -jnp.inf, fl[j]) for j in range(EXPERTS_PER_GROUP)]
    m2 = functools.reduce(jnp.maximum, rest)
    i2 = jnp.full(m2.shape, EXPERTS_PER_GROUP - 1, jnp.int32)
    for j in range(EXPERTS_PER_GROUP - 2, -1, -1):
        i2 = jnp.where(jnp.logical_and(rest[j] == m2, i1 != j), j, i2)
    e2 = jnp.exp(m2 - m1)
    w1 = g_prob / (1.0 + e2)
    w2 = g_prob * e2 / (1.0 + e2)
    local = [jnp.where(i1 == j, w1, 0.0) + jnp.where(i2 == j, w2, 0.0) for j in range(EXPERTS_PER_GROUP)]
    return g_idx, local


def _outproj_kernel(a_ref, b_ref, wa_ref, wb_ref, x_ref, nw_ref, rw_ref, rb_ref,
                    x1_ref, h_ref, gk_ref, krow_ref, cnt_ref):
    x1 = (x_ref[...]
          + jnp.dot(a_ref[...], wa_ref[...], preferred_element_type=F32)
          + jnp.dot(b_ref[...], wb_ref[...], preferred_element_type=F32))
    x1_ref[...] = x1
    ms = jnp.mean(x1 * x1, axis=-1, keepdims=True)
    h = x1 * lax.rsqrt(ms + RMS_EPS) * nw_ref[...]
    h_ref[...] = h.astype(BF16)
    rw = rw_ref[...]
    hh = h.astype(BF16)
    hl = (h - hh.astype(F32)).astype(BF16)
    wh = rw.astype(BF16)
    wl = (rw - wh.astype(F32)).astype(BF16)
    hi_part = jnp.dot(hh, jnp.concatenate([wh, wl], axis=1), preferred_element_type=F32)
    logits = (hi_part[:, :LANES] + hi_part[:, LANES:]
              + jnp.dot(hl, wh, preferred_element_type=F32)) + rb_ref[...]
    g_idx, local = _route_gates(logits.T)
    tm = g_idx.shape[1]
    mem = (_iota2((8, tm), 0) == g_idx).astype(BF16)
    before = (_iota2((MOE_SUB, MOE_SUB), 0) < _iota2((MOE_SUB, MOE_SUB), 1)).astype(BF16)
    earlier = jnp.concatenate(
        [jnp.dot(mem[:, s:s + MOE_SUB], before, preferred_element_type=F32) for s in range(0, tm, MOE_SUB)],
        axis=1)
    rank = jnp.sum(mem.astype(F32) * earlier, axis=0, keepdims=True)
    key = g_idx.astype(F32) * MOE_KEY_STRIDE + rank
    krow_ref[...] = key
    sub_shift = int(math.log2(MOE_SUB))
    sub_sel = ((_iota2((tm, LANES), 0) >> sub_shift) == _iota2((tm, LANES), 1)).astype(BF16)
    cnt_ref[...] = jnp.dot(mem, sub_sel, preferred_element_type=F32).astype(jnp.int32)
    rows = jnp.concatenate([key] + local + [jnp.zeros((LANES - 1 - EXPERTS_PER_GROUP, tm), F32)], axis=0)
    gk_ref[...] = rows.T


def _outproj_route(a, b, a_blk, b_blk, w, x, nw, rw, rb, *, layer, tm, name):
    n, d = x.shape
    ka = kb = w.shape[0] // 2
    return pl.pallas_call(
        _outproj_kernel,
        grid=(n // tm,),
        in_specs=[
            pl.BlockSpec((tm, ka), lambda i: (i, a_blk)),
            pl.BlockSpec((tm, kb), lambda i: (i, b_blk)),
            pl.BlockSpec((ka, d), lambda i: (0, 0)),
            pl.BlockSpec((kb, d), lambda i: (1, 0)),
            pl.BlockSpec((tm, d), lambda i: (i, 0)),
            pl.BlockSpec((1, d), lambda i: (0, 0)),
            pl.BlockSpec((None, d, LANES), lambda i: (layer, 0, 0)),
            pl.BlockSpec((None, 1, LANES), lambda i: (layer, 0, 0)),
        ],
        out_specs=[
            pl.BlockSpec((tm, d), lambda i: (i, 0)),
            pl.BlockSpec((tm, d), lambda i: (i, 0)),
            pl.BlockSpec((tm, LANES), lambda i: (i, 0)),
            pl.BlockSpec((1, tm), lambda i: (0, i)),
            pl.BlockSpec((8, LANES), lambda i: (i, 0)),
        ],
        out_shape=[jax.ShapeDtypeStruct((n, d), F32),
                   jax.ShapeDtypeStruct((n, d), BF16),
                   jax.ShapeDtypeStruct((n, LANES), F32),
                   jax.ShapeDtypeStruct((1, n), F32),
                   jax.ShapeDtypeStruct((8 * (n // tm), LANES), jnp.int32)],
        compiler_params=pltpu.CompilerParams(
            dimension_semantics=("parallel",), vmem_limit_bytes=VMEM_LIMIT),
        name=name,
    )(a, b, w, w, x, nw, rw, rb)


def _round_up(v, m):
    return ((v + (m - 1)) // m) * m


def _round_up_pow2(v, m):
    return (v + (m - 1)) & ~(m - 1)


def _cdiv_pow2(v, m):
    return lax.shift_right_logical(v + (m - 1), int(math.log2(m)))


def _moe_kernel(cnt_ref, h_ref, krow_ref, gk_ref, wgu_ref, wd_ref, x_ref, fnw_ref, o_ref,
                hc_s, gc_s, y_s, *, final_norm):
    i = pl.program_id(0)
    g = pl.program_id(1)
    tm = h_ref.shape[0]
    n_sub = tm // MOE_SUB

    def segments(grp):
        cnts = [cnt_ref[(i * 8 + grp) * LANES + s] for s in range(n_sub)]
        starts = [jnp.int32(0)]
        for s in range(n_sub):
            starts.append(starts[-1] + _round_up_pow2(cnts[s], MOE_ALIGN))
        return cnts, starts

    counts, offs = segments(g)
    total = offs[-1]
    key0 = g.astype(F32) * MOE_KEY_STRIDE
    sub_iota = _iota2((MOE_BLK, MOE_SUB), 0).astype(F32)
    lane_iota = _iota2((MOE_SUB, MOE_BLK), 1).astype(F32)

    def pack(s, w):
        rows = slice(s * MOE_SUB, (s + 1) * MOE_SUB)
        base = key0 + (w * MOE_BLK).astype(F32)
        sel = (krow_ref[:, rows] == sub_iota + base).astype(BF16)
        dst = pl.ds(pl.multiple_of(offs[s] + w * MOE_BLK, MOE_ALIGN), MOE_BLK)
        hc_s[dst, :] = jnp.dot(sel, h_ref[rows, :], preferred_element_type=F32).astype(BF16)
        gc_s[dst, :] = _mm_01(sel, gk_ref[rows, :])

    for s in range(n_sub):
        def pack_more(w, carry, s=s):
            pack(s, w)
            return carry

        lax.fori_loop(1, _cdiv_pow2(counts[s], MOE_BLK), pack_more, 0)
    for s in range(n_sub):
        pack(s, jnp.int32(0))

    tail_rows = MOE_FIRST[-1] + MOE_FIRST_STEP
    tail = pl.ds(pl.multiple_of(total, MOE_ALIGN), tail_rows)
    hc_s[tail, :] = jnp.zeros((tail_rows, hc_s.shape[1]), BF16)
    gc_s[tail, :] = jnp.zeros((tail_rows, LANES), F32)

    def expert_rows(start, rows):
        blk = pl.ds(pl.multiple_of(start, MOE_FIRST_STEP), rows)
        hb = hc_s[blk, :]
        gates = gc_s[blk, :]
        y = jnp.zeros((rows, o_ref.shape[1]), F32)
        for e in range(EXPERTS_PER_GROUP):
            gu = jnp.dot(hb, wgu_ref[0, e], preferred_element_type=F32)
            act = _silu(gu[:, :D_EXPERT]) * gu[:, D_EXPERT:] * gates[:, 1 + e:2 + e]
            y = y + jnp.dot(act.astype(BF16), wd_ref[0, e], preferred_element_type=F32)
        y_s[g, blk, :] = y.astype(BF16)

    first = jnp.clip(_round_up_pow2(total, MOE_FIRST_STEP), MOE_FIRST[0], MOE_FIRST[-1])
    for size in MOE_FIRST:
        @pl.when(first == size)
        def _(size=size):
            expert_rows(0, size)

    n_rest = _cdiv_pow2(jnp.maximum(total - first, 0), MOE_BLK)

    def rest(bi, carry):
        expert_rows(first + bi * MOE_BLK, MOE_BLK)
        return carry

    lax.fori_loop(0, n_rest, rest, 0)
    done = pl.ds(pl.multiple_of(first + n_rest * MOE_BLK, MOE_FIRST_STEP), MOE_BLK)
    y_s[g, done, :] = jnp.zeros((MOE_BLK, y_s.shape[2]), BF16)

    @pl.when(g == N_GROUPS - 1)
    def _():
        segs = [segments(grp) for grp in range(N_GROUPS)]
        for s in range(n_sub):
            rows = slice(s * MOE_SUB, (s + 1) * MOE_SUB)
            key_col = gk_ref[rows, 0:1]
            sel = jnp.concatenate(
                [(key_col == lane_iota + grp * MOE_KEY_STRIDE).astype(BF16) for grp in range(N_GROUPS)], axis=1)
            packed = jnp.concatenate(
                [y_s[grp, pl.ds(pl.multiple_of(segs[grp][1][s], MOE_ALIGN), MOE_BLK), :]
                 for grp in range(N_GROUPS)], axis=0)
            o_ref[rows, :] = x_ref[rows, :] + jnp.dot(sel, packed, preferred_element_type=F32)
            for grp in range(N_GROUPS):
                def more(w, carry, s=s, rows=rows, grp=grp, key_col=key_col):
                    base = grp * MOE_KEY_STRIDE + (w * MOE_BLK).astype(F32)
                    sel_w = (key_col == lane_iota + base).astype(BF16)
                    src = pl.ds(pl.multiple_of(segs[grp][1][s] + w * MOE_BLK, MOE_ALIGN), MOE_BLK)
                    o_ref[rows, :] += jnp.dot(sel_w, y_s[grp, src, :], preferred_element_type=F32)
                    return carry

                lax.fori_loop(1, _cdiv_pow2(segs[grp][0][s], MOE_BLK), more, 0)
        if final_norm:
            y = o_ref[...]
            ms = jnp.mean(y * y, axis=-1, keepdims=True)
            o_ref[...] = y * lax.rsqrt(ms + RMS_EPS) * fnw_ref[...]


def _moe(h, krow, gk, counts, wgu, wd, x, fnw, *, layer, tm, final_norm, name):
    n, d = x.shape
    buf_rows = _round_up(tm + (tm // MOE_SUB) * MOE_ALIGN + MOE_FIRST[-1] + MOE_FIRST_STEP, MOE_BLK)
    grid_spec = pltpu.PrefetchScalarGridSpec(
        num_scalar_prefetch=1,
        grid=(n // tm, N_GROUPS),
        in_specs=[
            pl.BlockSpec((tm, d), lambda i, g, c: (i, 0)),
            pl.BlockSpec((1, tm), lambda i, g, c: (0, i)),
            pl.BlockSpec((tm, LANES), lambda i, g, c: (i, 0)),
            pl.BlockSpec((1, EXPERTS_PER_GROUP, d, 2 * D_EXPERT), lambda i, g, c: (layer, g, 0, 0)),
            pl.BlockSpec((1, EXPERTS_PER_GROUP, D_EXPERT, d), lambda i, g, c: (layer, g, 0, 0)),
            pl.BlockSpec((tm, d), lambda i, g, c: (i, 0)),
            pl.BlockSpec((1, d), lambda i, g, c: (0, 0)),
        ],
        out_specs=pl.BlockSpec((tm, d), lambda i, g, c: (i, 0)),
        scratch_shapes=[pltpu.VMEM((buf_rows, d), BF16),
                        pltpu.VMEM((buf_rows, LANES), F32),
                        pltpu.VMEM((N_GROUPS, buf_rows, d), BF16)],
    )
    return pl.pallas_call(
        functools.partial(_moe_kernel, final_norm=final_norm),
        grid_spec=grid_spec,
        out_shape=jax.ShapeDtypeStruct((n, d), F32),
        compiler_params=pltpu.CompilerParams(
            dimension_semantics=("parallel", "arbitrary"), vmem_limit_bytes=VMEM_LIMIT),
        name=name,
    )(counts, h, krow, gk, wgu, wd, x, fnw)


def _pad_cols(a, width):
    return jnp.pad(a, ((0, 0), (0, width - a.shape[1])))


def _rope_tables(seq):
    inv = (1.0 / (ROPE_BASE ** np.linspace(0.0, 1.0, RET_DK // 2, dtype=np.float32))).astype(np.float32)
    ang = np.arange(seq, dtype=np.float32)[:, None] * inv[None, :]
    return jnp.asarray(np.cos(ang), dtype=F32), jnp.asarray(np.sin(ang), dtype=F32)


def _router_params(wc, bc, wf, bf):
    rw = jnp.concatenate([wc, wf], axis=2)
    rb = jnp.concatenate([bc, bf], axis=1)[:, None, :]
    pad = ((0, 0), (0, 0), (0, LANES - rw.shape[2]))
    return jnp.pad(rw, pad), jnp.pad(rb, pad)


def kernel(x, norm_mix_w, norm_ffn_w, even_w_in, gdn_conv_w, gdn_a_log, gdn_dt_bias, gdn_norm_w, hgrn_lb_logits, hgrn_norm_w, even_w_out, odd_w_in, ret_norm_w, odd_w_out, router_c_w, router_c_b, router_f_w, router_f_b, moe_w_gate_up, moe_w_down, final_norm_w):
    bsz, seq, d = x.shape
    n = bsz * seq
    xt = x.reshape(n, d)
    mix_w = N_HEADS * HEAD_D
    conv_cols = 3 * mix_w
    gdn_main = conv_cols + mix_w

    w_in = even_w_in[0]
    small0 = gdn_main
    w_small = jnp.concatenate(
        [_pad_cols(w_in[:, small0:small0 + N_HEADS], LANES),
         _pad_cols(w_in[:, small0 + N_HEADS:small0 + 2 * N_HEADS], LANES),
         _pad_cols(w_in[:, small0 + 4 * mix_w:], LANES)], axis=1).astype(BF16)
    proj, small, o_b, wgu_b, wd_b, w_out, w_odd, w_out1 = _proj_even(
        xt, norm_mix_w[0][None, :], w_in.astype(BF16), w_small, hgrn_lb_logits, hgrn_norm_w[0][None, :],
        (moe_w_gate_up.reshape(-1, moe_w_gate_up.shape[-1]), moe_w_down.reshape(-1, moe_w_down.shape[-1]),
         even_w_out[0], odd_w_in[0], odd_w_out[0]),
        gdn_w=gdn_main, tm=512, tn=1024, tokens_per_seq=seq, layer=0, name="in_proj_even_hgrn2")
    wgu_b = wgu_b.reshape(moe_w_gate_up.shape)
    wd_b = wd_b.reshape(moe_w_down.shape)
    proj = proj.reshape(bsz, seq, -1)
    small = small.reshape(bsz, seq, -1)
    o_a = _gdn(proj, small, gdn_conv_w[0], _pad_cols(gdn_a_log[0][None, :], LANES),
               _pad_cols(gdn_dt_bias[0][None, :], LANES), gdn_norm_w[0][None, :], tb=512)
    assert ROUTE_TM == MOE_TM
    rw, rb = _router_params(router_c_w, router_c_b, router_f_w, router_f_b)
    x1, h, gk, krow, cnt = _outproj_route(o_a.reshape(n, mix_w), o_b.reshape(n, mix_w), 0, 0,
                                          w_out, xt, norm_ffn_w[0][None, :], rw, rb,
                                          layer=0, tm=ROUTE_TM, name="out_proj_even")
    x2 = _moe(h, krow, gk, cnt.reshape(-1), wgu_b, wd_b, x1,
              final_norm_w[None, :], layer=0, tm=MOE_TM, final_norm=False, name="moe0")

    cos, sin = _rope_tables(seq)
    o_c = _proj_retention(x2, norm_mix_w[1][None, :], w_odd, cos, sin, ret_norm_w[0][None, :],
                          tm=512, tn=1024, tc=RET_CHUNK, tokens_per_seq=seq, name="in_proj_retention")
    x3, h, gk, krow, cnt = _outproj_route(o_c, o_c, 0, 1, w_out1, x2,
                                          norm_ffn_w[1][None, :], rw, rb, layer=1, tm=ROUTE_TM, name="out_proj_odd")
    out = _moe(h, krow, gk, cnt.reshape(-1), wgu_b, wd_b, x3,
               final_norm_w[None, :], layer=1, tm=MOE_TM, final_norm=True, name="moe1")
    return out.reshape(bsz, seq, d)
```

```python
import functools
import math

import jax
import jax.numpy as jnp
import numpy as np
from jax import lax
from jax.experimental import pallas as pl
from jax.experimental.pallas import tpu as pltpu

F32 = jnp.float32
BF16 = jnp.bfloat16

D_MODEL = 1024
RMS_EPS = 1e-6
L2_EPS = 1e-6
CHUNK = 64
CONV_K = 4
N_HEADS = 4
HEAD_D = 128
RET_DK = 256
RET_DV = 512
RET_CHUNK = 256
ROPE_BASE = 10000.0
N_GROUPS = 4
EXPERTS_PER_GROUP = 4
N_EXPERTS = 16
D_EXPERT = 256
LANES = 128
ROUTE_TM = 1024
MOE_TM = 1024
MOE_SUB = 256
MOE_BLK = 128
MOE_ALIGN = 16
MOE_KEY_STRIDE = 4096.0
MOE_FIRST_STEP = 16
MOE_FIRST = (256, 272, 288, 304, 320, 336, 352)
VMEM_LIMIT = 56 * 1024 * 1024


def _mm(a, b):
    return jnp.dot(a.astype(BF16), b.astype(BF16), preferred_element_type=F32)


def _mm_nt(a, b):
    return lax.dot_general(a.astype(BF16), b.astype(BF16), (((1,), (1,)), ((), ())),
                           preferred_element_type=F32)


def _mm_tn(a, b):
    return _mm(a.T, b)


def _mm_01(m01, x):
    hi = x.astype(BF16)
    lo = (x - hi.astype(F32)).astype(BF16)
    return (jnp.dot(m01, hi, preferred_element_type=F32)
            + jnp.dot(m01, lo, preferred_element_type=F32))


def _bmm(a, b):
    return jnp.einsum('cik,ckj->cij', a.astype(BF16), b.astype(BF16), preferred_element_type=F32)


def _bmm_nt(a, b):
    return jnp.einsum('cik,cjk->cij', a.astype(BF16), b.astype(BF16), preferred_element_type=F32)


def _bmm_tn(a, b):
    return _bmm(jnp.swapaxes(a, 1, 2), b)


def _bmm_01(m01, x):
    hi = x.astype(BF16)
    lo = (x - hi.astype(F32)).astype(BF16)
    return (jnp.einsum('cik,ckj->cij', m01, hi, preferred_element_type=F32)
            + jnp.einsum('cik,ckj->cij', m01, lo, preferred_element_type=F32))


def _sigmoid(x):
    return 1.0 / (1.0 + jnp.exp2(x * (-math.log2(math.e))))


def _silu(x):
    return x * _sigmoid(x)


def _softplus(x):
    return jnp.maximum(x, 0.0) + jnp.log(1.0 + jnp.exp(-jnp.abs(x)))


def _iota2(shape, dim):
    return lax.broadcasted_iota(jnp.int32, shape, dim)


def _head_rms_gate(o, gate, nw, width):
    outs = []
    for h in range(o.shape[1] // width):
        oh = o[:, h * width:(h + 1) * width]
        ms = jnp.mean(oh * oh, axis=-1, keepdims=True)
        outs.append(oh * lax.rsqrt(ms + RMS_EPS) * nw)
    return jnp.concatenate(outs, axis=1) * _silu(gate)


def _proj_even_kernel(*refs, tn, n_cast, tiles_per_seq, layer, shift):
    (x_ref, nw_ref, wg_ref, wh_ref, ws_ref, lb_ref, hnw_ref), rest = refs[:7], refs[7:]
    cast_in, rest = rest[:n_cast], rest[n_cast:]
    (o_ref, os_ref, ob_ref), rest = rest[:3], rest[3:]
    cast_out, rest = rest[:n_cast], rest[n_cast:]
    hg_s, whg_s, q_s, k_s, lf_s, i_s, oh_s, st_ref = rest
    for src, dst in zip(cast_in, cast_out):
        dst[...] = src[...].astype(dst.dtype)

    @pl.when(pl.program_id(0) == 0)
    def _():
        sel = (_iota2((2 * LANES, LANES), 0) == _iota2((2 * LANES, LANES), 1) + shift).astype(BF16)
        n_blocks = wh_ref.shape[1] // LANES
        for j in range(n_blocks):
            if j + 1 < n_blocks:
                both = wh_ref[:, j * LANES:(j + 2) * LANES]
            else:
                both = jnp.concatenate([wh_ref[:, j * LANES:], ws_ref[:, 2 * LANES:]], axis=1)
            whg_s[:, j * LANES:(j + 1) * LANES] = jnp.dot(both, sel, preferred_element_type=F32).astype(BF16)

    @pl.when(pl.program_id(0) % tiles_per_seq == 0)
    def _():
        st_ref[...] = jnp.zeros(st_ref.shape, F32)

    x = x_ref[...]
    tm = x.shape[0]
    ms = jnp.mean(x * x, axis=-1, keepdims=True)
    hb = (x * lax.rsqrt(ms + RMS_EPS) * nw_ref[...]).astype(BF16)
    os_ref[...] = jnp.dot(hb, ws_ref[:, :2 * LANES], preferred_element_type=F32)
    for j in range(whg_s.shape[1] // tn):
        cols = slice(j * tn, (j + 1) * tn)
        hg_s[:, cols] = jnp.dot(hb, whg_s[:, cols], preferred_element_type=F32)
    for j in range(wg_ref.shape[1] // tn):
        cols = slice(j * tn, (j + 1) * tn)
        o_ref[:, cols] = jnp.dot(hb, wg_ref[:, cols], preferred_element_type=F32).astype(o_ref.dtype)
    w = N_HEADS * HEAD_D
    ob_ref[...] = _hgrn_block(hg_s[:, 0:w], hg_s[:, w:2 * w], hg_s[:, 2 * w:3 * w], hg_s[:, 3 * w:4 * w],
                              lb_ref, hnw_ref, q_s, k_s, lf_s, i_s, oh_s, st_ref,
                              tb=tm, layer=layer).astype(ob_ref.dtype)


def _proj_even(x, nw, w, w_small, lb_logits, hgrn_nw, cast, *, gdn_w, tm, tn, tokens_per_seq, layer, name):
    n, d = x.shape
    steps = n // tm
    mix_w = N_HEADS * HEAD_D
    hgrn_w = 4 * mix_w
    shift = w.shape[1] - gdn_w - hgrn_w
    assert gdn_w == hgrn_w and 0 < shift < LANES and w_small.shape[1] == 3 * LANES
    const = lambda i: (0, 0)
    tile = lambda i: (i, 0)
    in_specs = [pl.BlockSpec((tm, d), tile), pl.BlockSpec((1, d), const),
                pl.BlockSpec((d, gdn_w), const), pl.BlockSpec((d, hgrn_w), lambda i: (0, 1)),
                pl.BlockSpec(w_small.shape, const), pl.BlockSpec(lb_logits.shape, const),
                pl.BlockSpec((1, HEAD_D), const)]
    out_specs = [pl.BlockSpec((tm, gdn_w), tile), pl.BlockSpec((tm, 2 * LANES), tile),
                 pl.BlockSpec((tm, mix_w), tile)]
    out_shape = [jax.ShapeDtypeStruct((n, gdn_w), BF16), jax.ShapeDtypeStruct((n, 2 * LANES), F32),
                 jax.ShapeDtypeStruct((n, mix_w), BF16)]
    for a in cast:
        slab = (a.shape[0] // steps, a.shape[1])
        in_specs.append(pl.BlockSpec(slab, tile))
        out_specs.append(pl.BlockSpec(slab, tile))
        out_shape.append(jax.ShapeDtypeStruct(a.shape, BF16))
    scratch = ([pltpu.VMEM((tm, hgrn_w), F32), pltpu.VMEM((d, hgrn_w), BF16)] + [pltpu.VMEM((tm, mix_w), F32)] * 5
               + [pltpu.VMEM((N_HEADS, HEAD_D, HEAD_D), F32)])
    return pl.pallas_call(
        functools.partial(_proj_even_kernel, tn=tn, n_cast=len(cast), tiles_per_seq=tokens_per_seq // tm,
                          layer=layer, shift=shift),
        grid=(steps,),
        in_specs=in_specs,
        out_specs=out_specs,
        out_shape=out_shape,
        scratch_shapes=scratch,
        compiler_params=pltpu.CompilerParams(
            dimension_semantics=("arbitrary",), vmem_limit_bytes=VMEM_LIMIT),
        name=name,
    )(x, nw, w, w, w_small, lb_logits, hgrn_nw, *cast)


def _gdn_kernel(qkv_ref, z_ref, sm_ref, cw_ref, alog_ref, dtb_ref, nw_ref, o_ref,
                xs_ref, qkv_s, o_s, m_s, sq_s, qe_s, dec_s, st_ref, *, tb):
    @pl.when(pl.program_id(1) == 0)
    def _():
        xs_ref[0:8, :] = jnp.zeros((8, xs_ref.shape[1]), F32)
        st_ref[...] = jnp.zeros(st_ref.shape, F32)

    c = CHUNK
    nc = tb // c
    nb = N_HEADS * nc
    qk_w = N_HEADS * HEAD_D
    row = _iota2((c, c), 0)
    col = _iota2((c, c), 1)
    causal = (row >= col)[None]
    strict = (row > col)[None]
    same16 = ((row >> 4) == (col >> 4))[None]
    same32 = ((row >> 5) == (col >> 5))[None]
    eye = (row == col).astype(F32)[None]
    ltri = jnp.broadcast_to((row >= col).astype(BF16)[None], (nc, c, c))
    lane = _iota2((nb, c, LANES), 2)

    def per_head(fn):
        return jnp.concatenate([fn(h) for h in range(N_HEADS)], axis=0)

    for h in range(N_HEADS):
        for base, scale in ((0, HEAD_D ** -0.5), (qk_w, 1.0), (2 * qk_w, None)):
            cols = slice(base + h * HEAD_D, base + (h + 1) * HEAD_D)
            x = qkv_ref[0, :, cols].astype(F32)
            xs_ref[8:8 + tb, cols] = x
            cw = cw_ref[:, cols]
            y = x * cw[CONV_K - 1:CONV_K, :]
            for j in range(1, CONV_K):
                y = y + xs_ref[pl.ds(8 - j, tb), cols] * cw[CONV_K - 1 - j:CONV_K - j, :]
            y = _silu(y)
            if scale is not None:
                y = y * (lax.rsqrt(jnp.sum(y * y, axis=-1, keepdims=True) + L2_EPS) * scale)
            qkv_s[:, cols] = y
    xs_ref[0:8, :] = xs_ref[tb:tb + 8, :]

    sm = sm_ref[0]
    beta_all = _sigmoid(sm[:, :LANES]).reshape(nc, c, LANES)
    rate = -jnp.exp(alog_ref[...]) * math.log2(math.e)
    g_all = (rate * _softplus(sm[:, LANES:] + dtb_ref[...])).reshape(nc, c, LANES)
    gc_all = _bmm_01(ltri, g_all)
    q = per_head(lambda h: qkv_s[:, h * HEAD_D:(h + 1) * HEAD_D].reshape(nc, c, HEAD_D))
    k = per_head(lambda h: qkv_s[:, qk_w + h * HEAD_D:qk_w + (h + 1) * HEAD_D].reshape(nc, c, HEAD_D))
    v = per_head(lambda h: qkv_s[:, 2 * qk_w + h * HEAD_D:2 * qk_w + (h + 1) * HEAD_D].reshape(nc, c, HEAD_D))
    beta = per_head(lambda h: beta_all[:, :, h:h + 1])
    gcol = per_head(lambda h: gc_all[:, :, h:h + 1])
    g_hi = gcol.astype(BF16).astype(F32)
    g_mid = (gcol - g_hi).astype(BF16).astype(F32)
    g_lo = gcol - g_hi - g_mid
    pieces = jnp.where(lane == 0, g_hi, jnp.where(lane == 1, g_mid, jnp.where(lane == 2, g_lo, 0.0)))
    lane1 = _iota2((1, 1, LANES), 2)
    lhs = pieces + jnp.where(jnp.logical_and(lane1 >= 3, lane1 < 6), 1.0, 0.0)
    rhs_g = jnp.where(lane1 < 3, 1.0, 0.0) - pltpu.roll(pieces, 3, axis=2)
    decay = jnp.exp2(jnp.where(causal, _bmm_nt(lhs, rhs_g), -jnp.inf))
    kb = k * beta
    a = jnp.where(strict, _bmm_nt(kb, k) * decay, 0.0)
    d = jnp.where(same16, a, 0.0)
    x_inv = eye - d
    dp = d
    for _ in range(3):
        dp = _bmm(dp, dp)
        x_inv = x_inv + _bmm(x_inv, dp)
    e = jnp.where(jnp.logical_and(same32, jnp.logical_not(same16)), a, 0.0)
    x_inv = x_inv - _bmm(x_inv, _bmm(e, x_inv))
    f = jnp.where(same32, 0.0, a)
    x_inv = x_inv - _bmm(x_inv, _bmm(f, x_inv))
    wu = _bmm(x_inv, jnp.concatenate([kb * jnp.exp2(gcol), v * beta], axis=2))
    attn = _bmm_nt(q, k) * decay
    g_last = gcol[:, c - 1:c, :]
    kd = k * jnp.exp2(g_last - gcol)
    mq = _bmm_tn(kd, wu)
    aw = _bmm(attn, wu)
    m_s[...] = mq[:, :, :HEAD_D].astype(BF16).reshape(N_HEADS, nc, HEAD_D, HEAD_D)
    sq_s[...] = mq[:, :, HEAD_D:].reshape(N_HEADS, nc, HEAD_D, HEAD_D)
    qe_s[...] = (q * jnp.exp2(gcol) - aw[:, :, :HEAD_D]).astype(BF16).reshape(N_HEADS, nc, c, HEAD_D)
    dec_s[...] = jnp.broadcast_to(jnp.exp2(g_last), (nb, 1, HEAD_D)).reshape(N_HEADS, nc, 1, HEAD_D)
    for h in range(N_HEADS):
        o_s[:, h * HEAD_D:(h + 1) * HEAD_D] = aw[h * nc:(h + 1) * nc, :, HEAD_D:].reshape(tb, HEAD_D)

    for ci in range(nc):
        for h in range(N_HEADS):
            hs = slice(h * HEAD_D, (h + 1) * HEAD_D)
            s_h = st_ref[h]
            s_b = s_h.astype(BF16)
            o_s[ci * c:(ci + 1) * c, hs] += jnp.dot(qe_s[h, ci], s_b, preferred_element_type=F32)
            st_ref[h] = (s_h * dec_s[h, ci] - jnp.dot(m_s[h, ci], s_b, preferred_element_type=F32)
                         + sq_s[h, ci])

    o_ref[0] = _head_rms_gate(o_s[...], z_ref[0].astype(F32), nw_ref[...], HEAD_D).astype(o_ref.dtype)


def _gdn(proj, small, conv_w, a_log, dt_bias, norm_w, *, tb):
    b, t, _ = proj.shape
    conv_cols = 3 * N_HEADS * HEAD_D
    mix_w = N_HEADS * HEAD_D
    return pl.pallas_call(
        functools.partial(_gdn_kernel, tb=tb),
        grid=(b, t // tb),
        in_specs=[
            pl.BlockSpec((1, tb, conv_cols), lambda i, j: (i, j, 0)),
            pl.BlockSpec((1, tb, mix_w), lambda i, j: (i, j, conv_cols // mix_w)),
            pl.BlockSpec((1, tb, 2 * LANES), lambda i, j: (i, j, 0)),
            pl.BlockSpec((CONV_K, conv_cols), lambda i, j: (0, 0)),
            pl.BlockSpec((1, LANES), lambda i, j: (0, 0)),
            pl.BlockSpec((1, LANES), lambda i, j: (0, 0)),
            pl.BlockSpec((1, HEAD_D), lambda i, j: (0, 0)),
        ],
        out_specs=pl.BlockSpec((1, tb, mix_w), lambda i, j: (i, j, 0)),
        out_shape=jax.ShapeDtypeStruct((b, t, mix_w), BF16),
        scratch_shapes=[
            pltpu.VMEM((tb + 8, conv_cols), F32),
            pltpu.VMEM((tb, conv_cols), F32),
            pltpu.VMEM((tb, mix_w), F32),
            pltpu.VMEM((N_HEADS, tb // CHUNK, HEAD_D, HEAD_D), BF16),
            pltpu.VMEM((N_HEADS, tb // CHUNK, HEAD_D, HEAD_D), F32),
            pltpu.VMEM((N_HEADS, tb // CHUNK, CHUNK, HEAD_D), BF16),
            pltpu.VMEM((N_HEADS, tb // CHUNK, 1, HEAD_D), F32),
            pltpu.VMEM((N_HEADS, HEAD_D, HEAD_D), F32),
        ],
        compiler_params=pltpu.CompilerParams(
            dimension_semantics=("parallel", "arbitrary"), vmem_limit_bytes=VMEM_LIMIT),
        name="gdn",
    )(proj, proj, small, conv_w, a_log, dt_bias, norm_w)


def _hgrn_block(q_in, f_in, i_in, gate_in, lb_ref, nw_ref, q_s, k_s, lf_s, i_s, o_s, st_ref, *, tb, layer):
    lbl = lb_ref[...]
    e_lb = jnp.exp(lbl - jnp.max(lbl, axis=0, keepdims=True))
    lb = jnp.sum(e_lb[:layer + 1], axis=0, keepdims=True) / jnp.sum(e_lb, axis=0, keepdims=True)

    f = lb + (1.0 - lb) * _sigmoid(f_in)
    k_s[...] = 1.0 - f
    lf_s[...] = jnp.log2(f)
    q_s[...] = _silu(q_in) * (HEAD_D ** -0.5)
    i_s[...] = i_in

    c = CHUNK
    nc = tb // c
    blk = 8
    row = _iota2((c, c), 0)
    col = _iota2((c, c), 1)
    ltri = jnp.broadcast_to((row >= col).astype(BF16)[None], (nc, c, c))
    level_masks = {}
    for m in (32, 16, 8):
        sh = int(math.log2(2 * m))
        level_masks[m] = jnp.logical_and(
            (row >> sh) == (col >> sh),
            jnp.logical_and((row & (2 * m - 1)) >= m, (col & (2 * m - 1)) < m))[None]
    nb = N_HEADS * nc
    sub = _iota2((nb * c // blk, blk, HEAD_D), 1)

    def per_head(fn):
        return jnp.concatenate([fn(slice(h * HEAD_D, (h + 1) * HEAD_D)) for h in range(N_HEADS)], axis=0)

    b_all = _bmm_01(ltri, lf_s[...].reshape(nc, c, N_HEADS * HEAD_D))
    b = per_head(lambda sl: b_all[:, :, sl])
    q = per_head(lambda sl: q_s[:, sl].reshape(nc, c, HEAD_D))
    k = per_head(lambda sl: k_s[:, sl].reshape(nc, c, HEAD_D))
    iv = per_head(lambda sl: i_s[:, sl].reshape(nc, c, HEAD_D))
    attn = jnp.zeros((nb, c, c), F32)
    for m in (32, 16, 8):
        b_m = b.reshape(nb * c // (2 * m), 2 * m, HEAD_D)
        ref = jnp.broadcast_to(b_m[:, m:m + 1, :], b_m.shape).reshape(nb, c, HEAD_D)
        e = jnp.exp2(-jnp.abs(b - ref))
        attn = attn + jnp.where(level_masks[m], _bmm_nt(q * e, k * e), 0.0)
    o = _bmm(attn, iv)
    qb, kb, bb, ib = (a.reshape(nb * c // blk, blk, HEAD_D) for a in (q, k, b, iv))
    ob = jnp.zeros(qb.shape, F32)
    for s in range(blk):
        dec = jnp.exp2(jnp.where(sub >= s, bb - bb[:, s:s + 1, :], -jnp.inf))
        a_col = jnp.sum(dec * qb * kb[:, s:s + 1, :], axis=-1, keepdims=True)
        ob = ob + a_col * ib[:, s:s + 1, :]
    o = o + ob.reshape(nb, c, HEAD_D)
    b_last = b[:, c - 1:c, :]
    q_in = (q * jnp.exp2(b)).astype(BF16)
    upd = _bmm_tn(iv, k * jnp.exp2(b_last - b))
    keep = jnp.exp2(b_last)
    for h in range(N_HEADS):
        s_t = st_ref[h]
        outs = []
        for ci in range(h * nc, (h + 1) * nc):
            outs.append(o[ci] + _mm_nt(q_in[ci], s_t))
            s_t = s_t * keep[ci] + upd[ci]
        st_ref[h] = s_t
        o_s[:, h * HEAD_D:(h + 1) * HEAD_D] = jnp.concatenate(outs, axis=0)

    return _head_rms_gate(o_s[...], gate_in, nw_ref[...], HEAD_D)


def _proj_ret_kernel(x_ref, nw_ref, w_ref, cos_ref, sin_ref, rnw_ref, o_ref,
                     wq_s, proj_s, dmat_s, st_ref, *, tn, tc, tiles_per_seq):
    i = pl.program_id(0)
    qk_w = N_HEADS * RET_DK
    v_w = N_HEADS * RET_DV
    half = RET_DK // 2
    log_gammas = [math.log(1.0 - 2.0 ** (-5.0 - h)) for h in range(N_HEADS)]

    @pl.when(i == 0)
    def _():
        src = _iota2((RET_DK, RET_DK), 0)
        dst = _iota2((RET_DK, RET_DK), 1)
        split = (src == jnp.where(dst < half, 2 * dst, 2 * (dst - half) + 1)).astype(BF16)
        for hd in range(2 * N_HEADS):
            cols = slice(hd * RET_DK, (hd + 1) * RET_DK)
            wq_s[:, cols] = jnp.dot(w_ref[:, cols], split, preferred_element_type=F32).astype(BF16)
        rel = (_iota2((tc, tc), 0) - _iota2((tc, tc), 1)).astype(F32)
        for h in range(N_HEADS):
            dmat_s[h] = jnp.where(rel >= 0, jnp.exp(jnp.maximum(rel, 0.0) * log_gammas[h]), 0.0)

    @pl.when(i % tiles_per_seq == 0)
    def _():
        st_ref[...] = jnp.zeros(st_ref.shape, F32)

    x = x_ref[...]
    tm = x.shape[0]
    ms = jnp.mean(x * x, axis=-1, keepdims=True)
    hb = (x * lax.rsqrt(ms + RMS_EPS) * nw_ref[...]).astype(BF16)
    for j in range(w_ref.shape[1] // tn):
        cols = slice(j * tn, (j + 1) * tn)
        w_tile = wq_s[:, cols] if (j + 1) * tn <= 2 * qk_w else w_ref[:, cols]
        proj_s[:, cols] = jnp.dot(hb, w_tile, preferred_element_type=F32).astype(BF16)

    pos = _iota2((tc, 1), 0).astype(F32)
    for ck in range(tm // tc):
        rows = slice(ck * tc, (ck + 1) * tc)
        cos = cos_ref[rows, :]
        sin = sin_ref[rows, :]

        def rot(xh):
            x1, x2 = xh[:, :half], xh[:, half:]
            return jnp.concatenate([x1 * cos - x2 * sin, x1 * sin + x2 * cos], axis=1)

        for h in range(N_HEADS):
            lg = log_gammas[h]
            q = rot(proj_s[rows, h * RET_DK:(h + 1) * RET_DK].astype(F32))
            k = rot(proj_s[rows, qk_w + h * RET_DK:qk_w + (h + 1) * RET_DK].astype(F32)) * (RET_DK ** -0.5)
            v = proj_s[rows, 2 * qk_w + h * RET_DV:2 * qk_w + (h + 1) * RET_DV]
            gate = proj_s[rows, 2 * qk_w + v_w + h * RET_DV:2 * qk_w + v_w + (h + 1) * RET_DV]
            s = st_ref[h]
            attn = _mm_nt(q, k) * dmat_s[h]
            o = _mm(q * jnp.exp((pos + 1.0) * lg), s) + _mm(attn, v)
            st_ref[h] = s * math.exp(tc * lg) + _mm_tn(k * jnp.exp((tc - 1.0 - pos) * lg), v)
            ms_o = jnp.mean(o * o, axis=-1, keepdims=True)
            o_ref[rows, h * RET_DV:(h + 1) * RET_DV] = (o * lax.rsqrt(ms_o + RMS_EPS) * rnw_ref[...]
                                                        * _silu(gate.astype(F32))).astype(o_ref.dtype)


def _proj_retention(x, nw, w, cos, sin, ret_nw, *, tm, tn, tc, tokens_per_seq, name):
    n, d = x.shape
    tiles_per_seq = tokens_per_seq // tm
    v_w = N_HEADS * RET_DV
    return pl.pallas_call(
        functools.partial(_proj_ret_kernel, tn=tn, tc=tc, tiles_per_seq=tiles_per_seq),
        grid=(n // tm,),
        in_specs=[
            pl.BlockSpec((tm, d), lambda i: (i, 0)),
            pl.BlockSpec((1, d), lambda i: (0, 0)),
            pl.BlockSpec(w.shape, lambda i: (0, 0)),
            pl.BlockSpec((tm, RET_DK // 2), lambda i: (i % tiles_per_seq, 0)),
            pl.BlockSpec((tm, RET_DK // 2), lambda i: (i % tiles_per_seq, 0)),
            pl.BlockSpec((1, RET_DV), lambda i: (0, 0)),
        ],
        out_specs=pl.BlockSpec((tm, v_w), lambda i: (i, 0)),
        out_shape=jax.ShapeDtypeStruct((n, v_w), BF16),
        scratch_shapes=[
            pltpu.VMEM((d, 2 * N_HEADS * RET_DK), BF16),
            pltpu.VMEM((tm, w.shape[1]), BF16),
            pltpu.VMEM((N_HEADS, tc, tc), F32),
            pltpu.VMEM((N_HEADS, RET_DK, RET_DV), F32),
        ],
        compiler_params=pltpu.CompilerParams(
            dimension_semantics=("arbitrary",), vmem_limit_bytes=VMEM_LIMIT),
        name=name,
    )(x, nw, w, cos, sin, ret_nw)


def _route_gates(logits_t):
    cl = [logits_t[g:g + 1, :] for g in range(N_GROUPS)]
    cmax = functools.reduce(jnp.maximum, cl)
    denom = sum(jnp.exp(x - cmax) for x in cl)
    g_prob = 1.0 / denom
    g_idx = jnp.full(cmax.shape, N_GROUPS - 1, jnp.int32)
    for g in range(N_GROUPS - 2, -1, -1):
        g_idx = jnp.where(cl[g] == cmax, g, g_idx)
    def fine_row(g, j):
        r = N_GROUPS + g * EXPERTS_PER_GROUP + j
        return logits_t[r:r + 1, :]

    fl = []
    for j in range(EXPERTS_PER_GROUP):
        x = fine_row(N_GROUPS - 1, j)
        for g in range(N_GROUPS - 2, -1, -1):
            x = jnp.where(g_idx == g, fine_row(g, j), x)
        fl.append(x)
    m1 = functools.reduce(jnp.maximum, fl)
    i1 = jnp.full(m1.shape, EXPERTS_PER_GROUP - 1, jnp.int32)
    for j in range(EXPERTS_PER_GROUP - 2, -1, -1):
        i1 = jnp.where(fl[j] == m1, j, i1)
    rest = [jnp.where(i1 == j, -jnp.inf, fl[j]) for j in range(EXPERTS_PER_GROUP)]
    m2 = functools.reduce(jnp.maximum, rest)
    i2 = jnp.full(m2.shape, EXPERTS_PER_GROUP - 1, jnp.int32)
    for j in range(EXPERTS_PER_GROUP - 2, -1, -1):
        i2 = jnp.where(jnp.logical_and(rest[j] == m2, i1 != j), j, i2)
    e2 = jnp.exp(m2 - m1)
    w1 = g_prob / (1.0 + e2)
    w2 = g_prob * e2 / (1.0 + e2)
    local = [jnp.where(i1 == j, w1, 0.0) + jnp.where(i2 == j, w2, 0.0) for j in range(EXPERTS_PER_GROUP)]
    return g_idx, local


def _outproj_kernel(a_ref, b_ref, wa_ref, wb_ref, x_ref, nw_ref, rw_ref, rb_ref,
                    x1_ref, h_ref, gk_ref, krow_ref, cnt_ref):
    x1 = (x_ref[...]
          + jnp.dot(a_ref[...], wa_ref[...], preferred_element_type=F32)
          + jnp.dot(b_ref[...], wb_ref[...], preferred_element_type=F32))
    x1_ref[...] = x1
    ms = jnp.mean(x1 * x1, axis=-1, keepdims=True)
    h = x1 * lax.rsqrt(ms + RMS_EPS) * nw_ref[...]
    h_ref[...] = h.astype(BF16)
    rw = rw_ref[...]
    hh = h.astype(BF16)
    hl = (h - hh.astype(F32)).astype(BF16)
    wh = rw.astype(BF16)
    wl = (rw - wh.astype(F32)).astype(BF16)
    hi_part = jnp.dot(hh, jnp.concatenate([wh, wl], axis=1), preferred_element_type=F32)
    logits = (hi_part[:, :LANES] + hi_part[:, LANES:]
              + jnp.dot(hl, wh, preferred_element_type=F32)) + rb_ref[...]
    g_idx, local = _route_gates(logits.T)
    tm = g_idx.shape[1]
    mem = (_iota2((8, tm), 0) == g_idx).astype(BF16)
    before = (_iota2((MOE_SUB, MOE_SUB), 0) < _iota2((MOE_SUB, MOE_SUB), 1)).astype(BF16)
    earlier = jnp.concatenate(
        [jnp.dot(mem[:, s:s + MOE_SUB], before, preferred_element_type=F32) for s in range(0, tm, MOE_SUB)],
        axis=1)
    rank = jnp.sum(mem.astype(F32) * earlier, axis=0, keepdims=True)
    key = g_idx.astype(F32) * MOE_KEY_STRIDE + rank
    krow_ref[...] = key
    sub_shift = int(math.log2(MOE_SUB))
    sub_sel = ((_iota2((tm, LANES), 0) >> sub_shift) == _iota2((tm, LANES), 1)).astype(BF16)
    cnt_ref[...] = jnp.dot(mem, sub_sel, preferred_element_type=F32).astype(jnp.int32)
    rows = jnp.concatenate([key] + local + [jnp.zeros((LANES - 1 - EXPERTS_PER_GROUP, tm), F32)], axis=0)
    gk_ref[...] = rows.T


def _outproj_route(a, b, a_blk, b_blk, w, x, nw, rw, rb, *, layer, tm, name):
    n, d = x.shape
    ka = kb = w.shape[0] // 2
    return pl.pallas_call(
        _outproj_kernel,
        grid=(n // tm,),
        in_specs=[
            pl.BlockSpec((tm, ka), lambda i: (i, a_blk)),
            pl.BlockSpec((tm, kb), lambda i: (i, b_blk)),
            pl.BlockSpec((ka, d), lambda i: (0, 0)),
            pl.BlockSpec((kb, d), lambda i: (1, 0)),
            pl.BlockSpec((tm, d), lambda i: (i, 0)),
            pl.BlockSpec((1, d), lambda i: (0, 0)),
            pl.BlockSpec((None, d, LANES), lambda i: (layer, 0, 0)),
            pl.BlockSpec((None, 1, LANES), lambda i: (layer, 0, 0)),
        ],
        out_specs=[
            pl.BlockSpec((tm, d), lambda i: (i, 0)),
            pl.BlockSpec((tm, d), lambda i: (i, 0)),
            pl.BlockSpec((tm, LANES), lambda i: (i, 0)),
            pl.BlockSpec((1, tm), lambda i: (0, i)),
            pl.BlockSpec((8, LANES), lambda i: (i, 0)),
        ],
        out_shape=[jax.ShapeDtypeStruct((n, d), F32),
                   jax.ShapeDtypeStruct((n, d), BF16),
                   jax.ShapeDtypeStruct((n, LANES), F32),
                   jax.ShapeDtypeStruct((1, n), F32),
                   jax.ShapeDtypeStruct((8 * (n // tm), LANES), jnp.int32)],
        compiler_params=pltpu.CompilerParams(
            dimension_semantics=("parallel",), vmem_limit_bytes=VMEM_LIMIT),
        name=name,
    )(a, b, w, w, x, nw, rw, rb)


def _round_up(v, m):
    return ((v + (m - 1)) // m) * m


def _round_up_pow2(v, m):
    return (v + (m - 1)) & ~(m - 1)


def _cdiv_pow2(v, m):
    return lax.shift_right_logical(v + (m - 1), int(math.log2(m)))


def _moe_kernel(cnt_ref, h_ref, krow_ref, gk_ref, wgu_ref, wd_ref, x_ref, fnw_ref, o_ref,
                hc_s, gc_s, y_s, *, final_norm):
    i = pl.program_id(0)
    g = pl.program_id(1)
    tm = h_ref.shape[0]
    n_sub = tm // MOE_SUB

    def segments(grp):
        cnts = [cnt_ref[(i * 8 + grp) * LANES + s] for s in range(n_sub)]
        starts = [jnp.int32(0)]
        for s in range(n_sub):
            starts.append(starts[-1] + _round_up_pow2(cnts[s], MOE_ALIGN))
        return cnts, starts

    counts, offs = segments(g)
    total = offs[-1]
    key0 = g.astype(F32) * MOE_KEY_STRIDE
    sub_iota = _iota2((MOE_BLK, MOE_SUB), 0).astype(F32)
    lane_iota = _iota2((MOE_SUB, MOE_BLK), 1).astype(F32)

    def pack(s, w):
        rows = slice(s * MOE_SUB, (s + 1) * MOE_SUB)
        base = key0 + (w * MOE_BLK).astype(F32)
        sel = (krow_ref[:, rows] == sub_iota + base).astype(BF16)
        dst = pl.ds(pl.multiple_of(offs[s] + w * MOE_BLK, MOE_ALIGN), MOE_BLK)
        hc_s[dst, :] = jnp.dot(sel, h_ref[rows, :], preferred_element_type=F32).astype(BF16)
        gc_s[dst, :] = _mm_01(sel, gk_ref[rows, :])

    for s in range(n_sub):
        def pack_more(w, carry, s=s):
            pack(s, w)
            return carry

        lax.fori_loop(1, _cdiv_pow2(counts[s], MOE_BLK), pack_more, 0)
    for s in range(n_sub):
        pack(s, jnp.int32(0))

    tail_rows = MOE_FIRST[-1] + MOE_FIRST_STEP
    tail = pl.ds(pl.multiple_of(total, MOE_ALIGN), tail_rows)
    hc_s[tail, :] = jnp.zeros((tail_rows, hc_s.shape[1]), BF16)
    gc_s[tail, :] = jnp.zeros((tail_rows, LANES), F32)

    def expert_rows(start, rows):
        blk = pl.ds(pl.multiple_of(start, MOE_FIRST_STEP), rows)
        hb = hc_s[blk, :]
        gates = gc_s[blk, :]
        y = jnp.zeros((rows, o_ref.shape[1]), F32)
        for e in range(EXPERTS_PER_GROUP):
            gu = jnp.dot(hb, wgu_ref[0, e], preferred_element_type=F32)
            act = _silu(gu[:, :D_EXPERT]) * gu[:, D_EXPERT:] * gates[:, 1 + e:2 + e]
            y = y + jnp.dot(act.astype(BF16), wd_ref[0, e], preferred_element_type=F32)
        y_s[g, blk, :] = y.astype(BF16)

    first = jnp.clip(_round_up_pow2(total, MOE_FIRST_STEP), MOE_FIRST[0], MOE_FIRST[-1])
    for size in MOE_FIRST:
        @pl.when(first == size)
        def _(size=size):
            expert_rows(0, size)

    n_rest = _cdiv_pow2(jnp.maximum(total - first, 0), MOE_BLK)

    def rest(bi, carry):
        expert_rows(first + bi * MOE_BLK, MOE_BLK)
        return carry

    lax.fori_loop(0, n_rest, rest, 0)
    done = pl.ds(pl.multiple_of(first + n_rest * MOE_BLK, MOE_FIRST_STEP), MOE_BLK)
    y_s[g, done, :] = jnp.zeros((MOE_BLK, y_s.shape[2]), BF16)

    @pl.when(g == N_GROUPS - 1)
    def _():
        segs = [segments(grp) for grp in range(N_GROUPS)]
        for s in range(n_sub):
            rows = slice(s * MOE_SUB, (s + 1) * MOE_SUB)
            key_col = gk_ref[rows, 0:1]
            sel = jnp.concatenate(
                [(key_col == lane_iota + grp * MOE_KEY_STRIDE).astype(BF16) for grp in range(N_GROUPS)], axis=1)
            packed = jnp.concatenate(
                [y_s[grp, pl.ds(pl.multiple_of(segs[grp][1][s], MOE_ALIGN), MOE_BLK), :]
                 for grp in range(N_GROUPS)], axis=0)
            o_ref[rows, :] = x_ref[rows, :] + jnp.dot(sel, packed, preferred_element_type=F32)
            for grp in range(N_GROUPS):
                def more(w, carry, s=s, rows=rows, grp=grp, key_col=key_col):
                    base = grp * MOE_KEY_STRIDE + (w * MOE_BLK).astype(F32)
                    sel_w = (key_col == lane_iota + base).astype(BF16)
                    src = pl.ds(pl.multiple_of(segs[grp][1][s] + w * MOE_BLK, MOE_ALIGN), MOE_BLK)
                    o_ref[rows, :] += jnp.dot(sel_w, y_s[grp, src, :], preferred_element_type=F32)
                    return carry

                lax.fori_loop(1, _cdiv_pow2(segs[grp][0][s], MOE_BLK), more, 0)
        if final_norm:
            y = o_ref[...]
            ms = jnp.mean(y * y, axis=-1, keepdims=True)
            o_ref[...] = y * lax.rsqrt(ms + RMS_EPS) * fnw_ref[...]


def _moe(h, krow, gk, counts, wgu, wd, x, fnw, *, layer, tm, final_norm, name):
    n, d = x.shape
    buf_rows = _round_up(tm + (tm // MOE_SUB) * MOE_ALIGN + MOE_FIRST[-1] + MOE_FIRST_STEP, MOE_BLK)
    grid_spec = pltpu.PrefetchScalarGridSpec(
        num_scalar_prefetch=1,
        grid=(n // tm, N_GROUPS),
        in_specs=[
            pl.BlockSpec((tm, d), lambda i, g, c: (i, 0)),
            pl.BlockSpec((1, tm), lambda i, g, c: (0, i)),
            pl.BlockSpec((tm, LANES), lambda i, g, c: (i, 0)),
            pl.BlockSpec((1, EXPERTS_PER_GROUP, d, 2 * D_EXPERT), lambda i, g, c: (layer, g, 0, 0)),
            pl.BlockSpec((1, EXPERTS_PER_GROUP, D_EXPERT, d), lambda i, g, c: (layer, g, 0, 0)),
            pl.BlockSpec((tm, d), lambda i, g, c: (i, 0)),
            pl.BlockSpec((1, d), lambda i, g, c: (0, 0)),
        ],
        out_specs=pl.BlockSpec((tm, d), lambda i, g, c: (i, 0)),
        scratch_shapes=[pltpu.VMEM((buf_rows, d), BF16),
                        pltpu.VMEM((buf_rows, LANES), F32),
                        pltpu.VMEM((N_GROUPS, buf_rows, d), BF16)],
    )
    return pl.pallas_call(
        functools.partial(_moe_kernel, final_norm=final_norm),
        grid_spec=grid_spec,
        out_shape=jax.ShapeDtypeStruct((n, d), F32),
        compiler_params=pltpu.CompilerParams(
            dimension_semantics=("parallel", "arbitrary"), vmem_limit_bytes=VMEM_LIMIT),
        name=name,
    )(counts, h, krow, gk, wgu, wd, x, fnw)


def _pad_cols(a, width):
    return jnp.pad(a, ((0, 0), (0, width - a.shape[1])))


def _rope_tables(seq):
    inv = (1.0 / (ROPE_BASE ** np.linspace(0.0, 1.0, RET_DK // 2, dtype=np.float32))).astype(np.float32)
    ang = np.arange(seq, dtype=np.float32)[:, None] * inv[None, :]
    return jnp.asarray(np.cos(ang), dtype=F32), jnp.asarray(np.sin(ang), dtype=F32)


def _router_params(wc, bc, wf, bf):
    rw = jnp.concatenate([wc, wf], axis=2)
    rb = jnp.concatenate([bc, bf], axis=1)[:, None, :]
    pad = ((0, 0), (0, 0), (0, LANES - rw.shape[2]))
    return jnp.pad(rw, pad), jnp.pad(rb, pad)


def kernel(x, norm_mix_w, norm_ffn_w, even_w_in, gdn_conv_w, gdn_a_log, gdn_dt_bias, gdn_norm_w, hgrn_lb_logits, hgrn_norm_w, even_w_out, odd_w_in, ret_norm_w, odd_w_out, router_c_w, router_c_b, router_f_w, router_f_b, moe_w_gate_up, moe_w_down, final_norm_w):
    bsz, seq, d = x.shape
    n = bsz * seq
    xt = x.reshape(n, d)
    mix_w = N_HEADS * HEAD_D
    conv_cols = 3 * mix_w
    gdn_main = conv_cols + mix_w

    w_in = even_w_in[0]
    small0 = gdn_main
    w_small = jnp.concatenate(
        [_pad_cols(w_in[:, small0:small0 + N_HEADS], LANES),
         _pad_cols(w_in[:, small0 + N_HEADS:small0 + 2 * N_HEADS], LANES),
         _pad_cols(w_in[:, small0 + 4 * mix_w:], LANES)], axis=1).astype(BF16)
    proj, small, o_b, wgu_b, wd_b, w_out, w_odd, w_out1 = _proj_even(
        xt, norm_mix_w[0][None, :], w_in.astype(BF16), w_small, hgrn_lb_logits, hgrn_norm_w[0][None, :],
        (moe_w_gate_up.reshape(-1, moe_w_gate_up.shape[-1]), moe_w_down.reshape(-1, moe_w_down.shape[-1]),
         even_w_out[0], odd_w_in[0], odd_w_out[0]),
        gdn_w=gdn_main, tm=512, tn=1024, tokens_per_seq=seq, layer=0, name="in_proj_even_hgrn2")
    wgu_b = wgu_b.reshape(moe_w_gate_up.shape)
    wd_b = wd_b.reshape(moe_w_down.shape)
    proj = proj.reshape(bsz, seq, -1)
    small = small.reshape(bsz, seq, -1)
    o_a = _gdn(proj, small, gdn_conv_w[0], _pad_cols(gdn_a_log[0][None, :], LANES),
               _pad_cols(gdn_dt_bias[0][None, :], LANES), gdn_norm_w[0][None, :], tb=512)
    assert ROUTE_TM == MOE_TM
    rw, rb = _router_params(router_c_w, router_c_b, router_f_w, router_f_b)
    x1, h, gk, krow, cnt = _outproj_route(o_a.reshape(n, mix_w), o_b.reshape(n, mix_w), 0, 0,
                                          w_out, xt, norm_ffn_w[0][None, :], rw, rb,
                                          layer=0, tm=ROUTE_TM, name="out_proj_even")
    x2 = _moe(h, krow, gk, cnt.reshape(-1), wgu_b, wd_b, x1,
              final_norm_w[None, :], layer=0, tm=MOE_TM, final_norm=False, name="moe0")

    cos, sin = _rope_tables(seq)
    o_c = _proj_retention(x2, norm_mix_w[1][None, :], w_odd, cos, sin, ret_norm_w[0][None, :],
                          tm=512, tn=1024, tc=RET_CHUNK, tokens_per_seq=seq, name="in_proj_retention")
    x3, h, gk, krow, cnt = _outproj_route(o_c, o_c, 0, 1, w_out1, x2,
                                          norm_ffn_w[1][None, :], rw, rb, layer=1, tm=ROUTE_TM, name="out_proj_odd")
    out = _moe(h, krow, gk, cnt.reshape(-1), wgu_b, wd_b, x3,
               final_norm_w[None, :], layer=1, tm=MOE_TM, final_norm=True, name="moe1")
    return out.reshape(bsz, seq, d)
```

```python
import functools
import math

import jax
import jax.numpy as jnp
import numpy as np
from jax import lax
from jax.experimental import pallas as pl
from jax.experimental.pallas import tpu as pltpu

F32 = jnp.float32
BF16 = jnp.bfloat16

D_MODEL = 1024
RMS_EPS = 1e-6
L2_EPS = 1e-6
CHUNK = 64
CONV_K = 4
N_HEADS = 4
HEAD_D = 128
RET_DK = 256
RET_DV = 512
RET_CHUNK = 256
ROPE_BASE = 10000.0
N_GROUPS = 4
EXPERTS_PER_GROUP = 4
N_EXPERTS = 16
D_EXPERT = 256
LANES = 128
ROUTE_TM = 1024
MOE_TM = 1024
MOE_SUB = 256
MOE_BLK = 128
MOE_ALIGN = 16
MOE_KEY_STRIDE = 4096.0
MOE_FIRST_STEP = 32
MOE_FIRST = (256, 288, 320, 352)
VMEM_LIMIT = 56 * 1024 * 1024


def _mm(a, b):
    return jnp.dot(a.astype(BF16), b.astype(BF16), preferred_element_type=F32)


def _mm_nt(a, b):
    return lax.dot_general(a.astype(BF16), b.astype(BF16), (((1,), (1,)), ((), ())),
                           preferred_element_type=F32)


def _mm_tn(a, b):
    return _mm(a.T, b)


def _mm_01(m01, x):
    hi = x.astype(BF16)
    lo = (x - hi.astype(F32)).astype(BF16)
    return (jnp.dot(m01, hi, preferred_element_type=F32)
            + jnp.dot(m01, lo, preferred_element_type=F32))


def _bmm(a, b):
    return jnp.einsum('cik,ckj->cij', a.astype(BF16), b.astype(BF16), preferred_element_type=F32)


def _bmm_nt(a, b):
    return jnp.einsum('cik,cjk->cij', a.astype(BF16), b.astype(BF16), preferred_element_type=F32)


def _bmm_tn(a, b):
    return _bmm(jnp.swapaxes(a, 1, 2), b)


def _bmm_01(m01, x):
    hi = x.astype(BF16)
    lo = (x - hi.astype(F32)).astype(BF16)
    return (jnp.einsum('cik,ckj->cij', m01, hi, preferred_element_type=F32)
            + jnp.einsum('cik,ckj->cij', m01, lo, preferred_element_type=F32))


def _sigmoid(x):
    return 1.0 / (1.0 + jnp.exp2(x * (-math.log2(math.e))))


def _silu(x):
    return x * _sigmoid(x)


def _softplus(x):
    return jnp.maximum(x, 0.0) + jnp.log(1.0 + jnp.exp(-jnp.abs(x)))


def _iota2(shape, dim):
    return lax.broadcasted_iota(jnp.int32, shape, dim)


def _head_rms_gate(o, gate, nw, width):
    outs = []
    for h in range(o.shape[1] // width):
        oh = o[:, h * width:(h + 1) * width]
        ms = jnp.mean(oh * oh, axis=-1, keepdims=True)
        outs.append(oh * lax.rsqrt(ms + RMS_EPS) * nw)
    return jnp.concatenate(outs, axis=1) * _silu(gate)


def _proj_even_kernel(*refs, tn, n_cast, tiles_per_seq, layer, shift):
    (x_ref, nw_ref, wg_ref, wh_ref, ws_ref, lb_ref, hnw_ref), rest = refs[:7], refs[7:]
    cast_in, rest = rest[:n_cast], rest[n_cast:]
    (o_ref, os_ref, ob_ref), rest = rest[:3], rest[3:]
    cast_out, rest = rest[:n_cast], rest[n_cast:]
    hg_s, whg_s, q_s, k_s, lf_s, i_s, oh_s, st_ref = rest
    for src, dst in zip(cast_in, cast_out):
        dst[...] = src[...].astype(dst.dtype)

    @pl.when(pl.program_id(0) == 0)
    def _():
        sel = (_iota2((2 * LANES, LANES), 0) == _iota2((2 * LANES, LANES), 1) + shift).astype(BF16)
        n_blocks = wh_ref.shape[1] // LANES
        for j in range(n_blocks):
            if j + 1 < n_blocks:
                both = wh_ref[:, j * LANES:(j + 2) * LANES]
            else:
                both = jnp.concatenate([wh_ref[:, j * LANES:], ws_ref[:, 2 * LANES:]], axis=1)
            whg_s[:, j * LANES:(j + 1) * LANES] = jnp.dot(both, sel, preferred_element_type=F32).astype(BF16)

    @pl.when(pl.program_id(0) % tiles_per_seq == 0)
    def _():
        st_ref[...] = jnp.zeros(st_ref.shape, F32)

    x = x_ref[...]
    tm = x.shape[0]
    ms = jnp.mean(x * x, axis=-1, keepdims=True)
    hb = (x * lax.rsqrt(ms + RMS_EPS) * nw_ref[...]).astype(BF16)
    os_ref[...] = jnp.dot(hb, ws_ref[:, :2 * LANES], preferred_element_type=F32)
    for j in range(whg_s.shape[1] // tn):
        cols = slice(j * tn, (j + 1) * tn)
        hg_s[:, cols] = jnp.dot(hb, whg_s[:, cols], preferred_element_type=F32)
    n_gdn = wg_ref.shape[1] // tn
    for j in range(n_gdn // 2):
        cols = slice(j * tn, (j + 1) * tn)
        o_ref[:, cols] = jnp.dot(hb, wg_ref[:, cols], preferred_element_type=F32).astype(o_ref.dtype)
    w = N_HEADS * HEAD_D
    ob_ref[...] = _hgrn_block(hg_s[:, 0:w], hg_s[:, w:2 * w], hg_s[:, 2 * w:3 * w], hg_s[:, 3 * w:4 * w],
                              lb_ref, hnw_ref, q_s, k_s, lf_s, i_s, oh_s, st_ref,
                              tb=tm, layer=layer).astype(ob_ref.dtype)
    for j in range(n_gdn // 2, n_gdn):
        cols = slice(j * tn, (j + 1) * tn)
        o_ref[:, cols] = jnp.dot(hb, wg_ref[:, cols], preferred_element_type=F32).astype(o_ref.dtype)


def _proj_even(x, nw, w, w_small, lb_logits, hgrn_nw, cast, *, gdn_w, tm, tn, tokens_per_seq, layer, name):
    n, d = x.shape
    steps = n // tm
    mix_w = N_HEADS * HEAD_D
    hgrn_w = 4 * mix_w
    shift = w.shape[1] - gdn_w - hgrn_w
    assert gdn_w == hgrn_w and 0 < shift < LANES and w_small.shape[1] == 3 * LANES
    const = lambda i: (0, 0)
    tile = lambda i: (i, 0)
    in_specs = [pl.BlockSpec((tm, d), tile), pl.BlockSpec((1, d), const),
                pl.BlockSpec((d, gdn_w), const), pl.BlockSpec((d, hgrn_w), lambda i: (0, 1)),
                pl.BlockSpec(w_small.shape, const), pl.BlockSpec(lb_logits.shape, const),
                pl.BlockSpec((1, HEAD_D), const)]
    out_specs = [pl.BlockSpec((tm, gdn_w), tile), pl.BlockSpec((tm, 2 * LANES), tile),
                 pl.BlockSpec((tm, mix_w), tile)]
    out_shape = [jax.ShapeDtypeStruct((n, gdn_w), BF16), jax.ShapeDtypeStruct((n, 2 * LANES), F32),
                 jax.ShapeDtypeStruct((n, mix_w), BF16)]
    for a in cast:
        slab = (a.shape[0] // steps, a.shape[1])
        in_specs.append(pl.BlockSpec(slab, tile))
        out_specs.append(pl.BlockSpec(slab, tile))
        out_shape.append(jax.ShapeDtypeStruct(a.shape, BF16))
    scratch = ([pltpu.VMEM((tm, hgrn_w), F32), pltpu.VMEM((d, hgrn_w), BF16)] + [pltpu.VMEM((tm, mix_w), F32)] * 5
               + [pltpu.VMEM((N_HEADS, HEAD_D, HEAD_D), F32)])
    return pl.pallas_call(
        functools.partial(_proj_even_kernel, tn=tn, n_cast=len(cast), tiles_per_seq=tokens_per_seq // tm,
                          layer=layer, shift=shift),
        grid=(steps,),
        in_specs=in_specs,
        out_specs=out_specs,
        out_shape=out_shape,
        scratch_shapes=scratch,
        compiler_params=pltpu.CompilerParams(
            dimension_semantics=("arbitrary",), vmem_limit_bytes=VMEM_LIMIT),
        name=name,
    )(x, nw, w, w, w_small, lb_logits, hgrn_nw, *cast)


def _gdn_kernel(qkv_ref, z_ref, sm_ref, cw_ref, alog_ref, dtb_ref, nw_ref, o_ref,
                xs_ref, qkv_s, o_s, m_s, sq_s, qe_s, dec_s, st_ref, *, tb):
    @pl.when(pl.program_id(1) == 0)
    def _():
        xs_ref[0:8, :] = jnp.zeros((8, xs_ref.shape[1]), F32)
        st_ref[...] = jnp.zeros(st_ref.shape, F32)

    c = CHUNK
    nc = tb // c
    nb = N_HEADS * nc
    qk_w = N_HEADS * HEAD_D
    row = _iota2((c, c), 0)
    col = _iota2((c, c), 1)
    causal = (row >= col)[None]
    strict = (row > col)[None]
    same16 = ((row >> 4) == (col >> 4))[None]
    same32 = ((row >> 5) == (col >> 5))[None]
    eye = (row == col).astype(F32)[None]
    ltri = jnp.broadcast_to((row >= col).astype(BF16)[None], (nc, c, c))
    lane = _iota2((nb, c, LANES), 2)

    def per_head(fn):
        return jnp.concatenate([fn(h) for h in range(N_HEADS)], axis=0)

    for h in range(N_HEADS):
        for base, scale in ((0, HEAD_D ** -0.5), (qk_w, 1.0), (2 * qk_w, None)):
            cols = slice(base + h * HEAD_D, base + (h + 1) * HEAD_D)
            x = qkv_ref[0, :, cols].astype(F32)
            xs_ref[8:8 + tb, cols] = x
            cw = cw_ref[:, cols]
            y = x * cw[CONV_K - 1:CONV_K, :]
            for j in range(1, CONV_K):
                y = y + xs_ref[pl.ds(8 - j, tb), cols] * cw[CONV_K - 1 - j:CONV_K - j, :]
            y = _silu(y)
            if scale is not None:
                y = y * (lax.rsqrt(jnp.sum(y * y, axis=-1, keepdims=True) + L2_EPS) * scale)
            qkv_s[:, cols] = y
    xs_ref[0:8, :] = xs_ref[tb:tb + 8, :]

    sm = sm_ref[0]
    beta_all = _sigmoid(sm[:, :LANES]).reshape(nc, c, LANES)
    rate = -jnp.exp(alog_ref[...]) * math.log2(math.e)
    g_all = (rate * _softplus(sm[:, LANES:] + dtb_ref[...])).reshape(nc, c, LANES)
    gc_all = _bmm_01(ltri, g_all)
    q = per_head(lambda h: qkv_s[:, h * HEAD_D:(h + 1) * HEAD_D].reshape(nc, c, HEAD_D))
    k = per_head(lambda h: qkv_s[:, qk_w + h * HEAD_D:qk_w + (h + 1) * HEAD_D].reshape(nc, c, HEAD_D))
    v = per_head(lambda h: qkv_s[:, 2 * qk_w + h * HEAD_D:2 * qk_w + (h + 1) * HEAD_D].reshape(nc, c, HEAD_D))
    beta = per_head(lambda h: beta_all[:, :, h:h + 1])
    gcol = per_head(lambda h: gc_all[:, :, h:h + 1])
    g_hi = gcol.astype(BF16).astype(F32)
    g_mid = (gcol - g_hi).astype(BF16).astype(F32)
    g_lo = gcol - g_hi - g_mid
    pieces = jnp.where(lane == 0, g_hi, jnp.where(lane == 1, g_mid, jnp.where(lane == 2, g_lo, 0.0)))
    lane1 = _iota2((1, 1, LANES), 2)
    lhs = pieces + jnp.where(jnp.logical_and(lane1 >= 3, lane1 < 6), 1.0, 0.0)
    rhs_g = jnp.where(lane1 < 3, 1.0, 0.0) - pltpu.roll(pieces, 3, axis=2)
    decay = jnp.exp2(jnp.where(causal, _bmm_nt(lhs, rhs_g), -jnp.inf))
    kb = k * beta
    a = jnp.where(strict, _bmm_nt(kb, k) * decay, 0.0)
    d = jnp.where(same16, a, 0.0)
    x_inv = eye - d
    dp = d
    for _ in range(3):
        dp = _bmm(dp, dp)
        x_inv = x_inv + _bmm(x_inv, dp)
    e = jnp.where(jnp.logical_and(same32, jnp.logical_not(same16)), a, 0.0)
    x_inv = x_inv - _bmm(x_inv, _bmm(e, x_inv))
    f = jnp.where(same32, 0.0, a)
    x_inv = x_inv - _bmm(x_inv, _bmm(f, x_inv))
    wu = _bmm(x_inv, jnp.concatenate([kb * jnp.exp2(gcol), v * beta], axis=2))
    attn = _bmm_nt(q, k) * decay
    g_last = gcol[:, c - 1:c, :]
    kd = k * jnp.exp2(g_last - gcol)
    mq = _bmm_tn(kd, wu)
    aw = _bmm(attn, wu)
    m_s[...] = mq[:, :, :HEAD_D].astype(BF16).reshape(N_HEADS, nc, HEAD_D, HEAD_D)
    sq_s[...] = mq[:, :, HEAD_D:].reshape(N_HEADS, nc, HEAD_D, HEAD_D)
    qe_s[...] = (q * jnp.exp2(gcol) - aw[:, :, :HEAD_D]).astype(BF16).reshape(N_HEADS, nc, c, HEAD_D)
    dec_s[...] = jnp.broadcast_to(jnp.exp2(g_last), (nb, 1, HEAD_D)).reshape(N_HEADS, nc, 1, HEAD_D)
    for h in range(N_HEADS):
        o_s[:, h * HEAD_D:(h + 1) * HEAD_D] = aw[h * nc:(h + 1) * nc, :, HEAD_D:].reshape(tb, HEAD_D)

    for ci in range(nc):
        for h in range(N_HEADS):
            hs = slice(h * HEAD_D, (h + 1) * HEAD_D)
            s_h = st_ref[h]
            s_b = s_h.astype(BF16)
            o_s[ci * c:(ci + 1) * c, hs] += jnp.dot(qe_s[h, ci], s_b, preferred_element_type=F32)
            st_ref[h] = (s_h * dec_s[h, ci] - jnp.dot(m_s[h, ci], s_b, preferred_element_type=F32)
                         + sq_s[h, ci])

    o_ref[0] = _head_rms_gate(o_s[...], z_ref[0].astype(F32), nw_ref[...], HEAD_D).astype(o_ref.dtype)


def _gdn(proj, small, conv_w, a_log, dt_bias, norm_w, *, tb):
    b, t, _ = proj.shape
    conv_cols = 3 * N_HEADS * HEAD_D
    mix_w = N_HEADS * HEAD_D
    return pl.pallas_call(
        functools.partial(_gdn_kernel, tb=tb),
        grid=(b, t // tb),
        in_specs=[
            pl.BlockSpec((1, tb, conv_cols), lambda i, j: (i, j, 0)),
            pl.BlockSpec((1, tb, mix_w), lambda i, j: (i, j, conv_cols // mix_w)),
            pl.BlockSpec((1, tb, 2 * LANES), lambda i, j: (i, j, 0)),
            pl.BlockSpec((CONV_K, conv_cols), lambda i, j: (0, 0)),
            pl.BlockSpec((1, LANES), lambda i, j: (0, 0)),
            pl.BlockSpec((1, LANES), lambda i, j: (0, 0)),
            pl.BlockSpec((1, HEAD_D), lambda i, j: (0, 0)),
        ],
        out_specs=pl.BlockSpec((1, tb, mix_w), lambda i, j: (i, j, 0)),
        out_shape=jax.ShapeDtypeStruct((b, t, mix_w), BF16),
        scratch_shapes=[
            pltpu.VMEM((tb + 8, conv_cols), F32),
            pltpu.VMEM((tb, conv_cols), F32),
            pltpu.VMEM((tb, mix_w), F32),
            pltpu.VMEM((N_HEADS, tb // CHUNK, HEAD_D, HEAD_D), BF16),
            pltpu.VMEM((N_HEADS, tb // CHUNK, HEAD_D, HEAD_D), F32),
            pltpu.VMEM((N_HEADS, tb // CHUNK, CHUNK, HEAD_D), BF16),
            pltpu.VMEM((N_HEADS, tb // CHUNK, 1, HEAD_D), F32),
            pltpu.VMEM((N_HEADS, HEAD_D, HEAD_D), F32),
        ],
        compiler_params=pltpu.CompilerParams(
            dimension_semantics=("parallel", "arbitrary"), vmem_limit_bytes=VMEM_LIMIT),
        name="gdn",
    )(proj, proj, small, conv_w, a_log, dt_bias, norm_w)


def _hgrn_block(q_in, f_in, i_in, gate_in, lb_ref, nw_ref, q_s, k_s, lf_s, i_s, o_s, st_ref, *, tb, layer):
    lbl = lb_ref[...]
    e_lb = jnp.exp(lbl - jnp.max(lbl, axis=0, keepdims=True))
    lb = jnp.sum(e_lb[:layer + 1], axis=0, keepdims=True) / jnp.sum(e_lb, axis=0, keepdims=True)

    f = lb + (1.0 - lb) * _sigmoid(f_in)
    k_s[...] = 1.0 - f
    lf_s[...] = jnp.log2(f)
    q_s[...] = _silu(q_in) * (HEAD_D ** -0.5)
    i_s[...] = i_in

    c = CHUNK
    nc = tb // c
    blk = 8
    row = _iota2((c, c), 0)
    col = _iota2((c, c), 1)
    ltri = jnp.broadcast_to((row >= col).astype(BF16)[None], (nc, c, c))
    level_masks = {}
    for m in (32, 16, 8):
        sh = int(math.log2(2 * m))
        level_masks[m] = jnp.logical_and(
            (row >> sh) == (col >> sh),
            jnp.logical_and((row & (2 * m - 1)) >= m, (col & (2 * m - 1)) < m))[None]
    nb = N_HEADS * nc
    sub = _iota2((nb * c // blk, blk, HEAD_D), 1)

    def per_head(fn):
        return jnp.concatenate([fn(slice(h * HEAD_D, (h + 1) * HEAD_D)) for h in range(N_HEADS)], axis=0)

    b_all = _bmm_01(ltri, lf_s[...].reshape(nc, c, N_HEADS * HEAD_D))
    b = per_head(lambda sl: b_all[:, :, sl])
    q = per_head(lambda sl: q_s[:, sl].reshape(nc, c, HEAD_D))
    k = per_head(lambda sl: k_s[:, sl].reshape(nc, c, HEAD_D))
    iv = per_head(lambda sl: i_s[:, sl].reshape(nc, c, HEAD_D))
    attn = jnp.zeros((nb, c, c), F32)
    for m in (32, 16, 8):
        b_m = b.reshape(nb * c // (2 * m), 2 * m, HEAD_D)
        ref = jnp.broadcast_to(b_m[:, m:m + 1, :], b_m.shape).reshape(nb, c, HEAD_D)
        e = jnp.exp2(-jnp.abs(b - ref))
        attn = attn + jnp.where(level_masks[m], _bmm_nt(q * e, k * e), 0.0)
    o = _bmm(attn, iv)
    qb, kb, bb, ib = (a.reshape(nb * c // blk, blk, HEAD_D) for a in (q, k, b, iv))
    ob = jnp.zeros(qb.shape, F32)
    for s in range(blk):
        dec = jnp.exp2(jnp.where(sub >= s, bb - bb[:, s:s + 1, :], -jnp.inf))
        a_col = jnp.sum(dec * qb * kb[:, s:s + 1, :], axis=-1, keepdims=True)
        ob = ob + a_col * ib[:, s:s + 1, :]
    o = o + ob.reshape(nb, c, HEAD_D)
    b_last = b[:, c - 1:c, :]
    q_in = (q * jnp.exp2(b)).astype(BF16)
    upd = _bmm_tn(iv, k * jnp.exp2(b_last - b))
    keep = jnp.exp2(b_last)
    for h in range(N_HEADS):
        s_t = st_ref[h]
        outs = []
        for ci in range(h * nc, (h + 1) * nc):
            outs.append(o[ci] + _mm_nt(q_in[ci], s_t))
            s_t = s_t * keep[ci] + upd[ci]
        st_ref[h] = s_t
        o_s[:, h * HEAD_D:(h + 1) * HEAD_D] = jnp.concatenate(outs, axis=0)

    return _head_rms_gate(o_s[...], gate_in, nw_ref[...], HEAD_D)


def _proj_ret_kernel(x_ref, nw_ref, w_ref, cos_ref, sin_ref, rnw_ref, o_ref,
                     wq_s, proj_s, dmat_s, st_ref, *, tn, tc, tiles_per_seq):
    i = pl.program_id(0)
    qk_w = N_HEADS * RET_DK
    v_w = N_HEADS * RET_DV
    half = RET_DK // 2
    log_gammas = [math.log(1.0 - 2.0 ** (-5.0 - h)) for h in range(N_HEADS)]

    @pl.when(i == 0)
    def _():
        src = _iota2((RET_DK, RET_DK), 0)
        dst = _iota2((RET_DK, RET_DK), 1)
        split = (src == jnp.where(dst < half, 2 * dst, 2 * (dst - half) + 1)).astype(BF16)
        for hd in range(2 * N_HEADS):
            cols = slice(hd * RET_DK, (hd + 1) * RET_DK)
            wq_s[:, cols] = jnp.dot(w_ref[:, cols], split, preferred_element_type=F32).astype(BF16)
        rel = (_iota2((tc, tc), 0) - _iota2((tc, tc), 1)).astype(F32)
        for h in range(N_HEADS):
            dmat_s[h] = jnp.where(rel >= 0, jnp.exp(jnp.maximum(rel, 0.0) * log_gammas[h]), 0.0)

    @pl.when(i % tiles_per_seq == 0)
    def _():
        st_ref[...] = jnp.zeros(st_ref.shape, F32)

    x = x_ref[...]
    tm = x.shape[0]
    ms = jnp.mean(x * x, axis=-1, keepdims=True)
    hb = (x * lax.rsqrt(ms + RMS_EPS) * nw_ref[...]).astype(BF16)
    gate0 = 2 * qk_w + v_w
    for j in range(gate0 // tn):
        cols = slice(j * tn, (j + 1) * tn)
        w_tile = wq_s[:, cols] if (j + 1) * tn <= 2 * qk_w else w_ref[:, cols]
        proj_s[:, cols] = jnp.dot(hb, w_tile, preferred_element_type=F32).astype(BF16)

    pos = _iota2((tc, 1), 0).astype(F32)
    for ck in range(tm // tc):
        rows = slice(ck * tc, (ck + 1) * tc)
        cos = cos_ref[rows, :]
        sin = sin_ref[rows, :]

        def rot(xh):
            x1, x2 = xh[:, :half], xh[:, half:]
            return jnp.concatenate([x1 * cos - x2 * sin, x1 * sin + x2 * cos], axis=1)

        for h in range(N_HEADS):
            lg = log_gammas[h]
            gcols = slice(gate0 + h * RET_DV, gate0 + (h + 1) * RET_DV)
            proj_s[rows, gcols] = jnp.dot(hb[rows], w_ref[:, gcols], preferred_element_type=F32).astype(BF16)
            q = rot(proj_s[rows, h * RET_DK:(h + 1) * RET_DK].astype(F32))
            k = rot(proj_s[rows, qk_w + h * RET_DK:qk_w + (h + 1) * RET_DK].astype(F32)) * (RET_DK ** -0.5)
            v = proj_s[rows, 2 * qk_w + h * RET_DV:2 * qk_w + (h + 1) * RET_DV]
            gate = proj_s[rows, 2 * qk_w + v_w + h * RET_DV:2 * qk_w + v_w + (h + 1) * RET_DV]
            s = st_ref[h]
            attn = _mm_nt(q, k) * dmat_s[h]
            o = _mm(q * jnp.exp((pos + 1.0) * lg), s) + _mm(attn, v)
            st_ref[h] = s * math.exp(tc * lg) + _mm_tn(k * jnp.exp((tc - 1.0 - pos) * lg), v)
            ms_o = jnp.mean(o * o, axis=-1, keepdims=True)
            o_ref[rows, h * RET_DV:(h + 1) * RET_DV] = (o * lax.rsqrt(ms_o + RMS_EPS) * rnw_ref[...]
                                                        * _silu(gate.astype(F32))).astype(o_ref.dtype)


def _proj_retention(x, nw, w, cos, sin, ret_nw, *, tm, tn, tc, tokens_per_seq, name):
    n, d = x.shape
    tiles_per_seq = tokens_per_seq // tm
    v_w = N_HEADS * RET_DV
    return pl.pallas_call(
        functools.partial(_proj_ret_kernel, tn=tn, tc=tc, tiles_per_seq=tiles_per_seq),
        grid=(n // tm,),
        in_specs=[
            pl.BlockSpec((tm, d), lambda i: (i, 0)),
            pl.BlockSpec((1, d), lambda i: (0, 0)),
            pl.BlockSpec(w.shape, lambda i: (0, 0)),
            pl.BlockSpec((tm, RET_DK // 2), lambda i: (i % tiles_per_seq, 0)),
            pl.BlockSpec((tm, RET_DK // 2), lambda i: (i % tiles_per_seq, 0)),
            pl.BlockSpec((1, RET_DV), lambda i: (0, 0)),
        ],
        out_specs=pl.BlockSpec((tm, v_w), lambda i: (i, 0)),
        out_shape=jax.ShapeDtypeStruct((n, v_w), BF16),
        scratch_shapes=[
            pltpu.VMEM((d, 2 * N_HEADS * RET_DK), BF16),
            pltpu.VMEM((tm, w.shape[1]), BF16),
            pltpu.VMEM((N_HEADS, tc, tc), F32),
            pltpu.VMEM((N_HEADS, RET_DK, RET_DV), F32),
        ],
        compiler_params=pltpu.CompilerParams(
            dimension_semantics=("arbitrary",), vmem_limit_bytes=VMEM_LIMIT),
        name=name,
    )(x, nw, w, cos, sin, ret_nw)


def _route_gates(logits_t):
    cl = [logits_t[g:g + 1, :] for g in range(N_GROUPS)]
    cmax = functools.reduce(jnp.maximum, cl)
    denom = sum(jnp.exp(x - cmax) for x in cl)
    g_prob = 1.0 / denom
    g_idx = jnp.full(cmax.shape, N_GROUPS - 1, jnp.int32)
    for g in range(N_GROUPS - 2, -1, -1):
        g_idx = jnp.where(cl[g] == cmax, g, g_idx)
    def fine_row(g, j):
        r = N_GROUPS + g * EXPERTS_PER_GROUP + j
        return logits_t[r:r + 1, :]

    fl = []
    for j in range(EXPERTS_PER_GROUP):
        x = fine_row(N_GROUPS - 1, j)
        for g in range(N_GROUPS - 2, -1, -1):
            x = jnp.where(g_idx == g, fine_row(g, j), x)
        fl.append(x)
    m1 = functools.reduce(jnp.maximum, fl)
    i1 = jnp.full(m1.shape, EXPERTS_PER_GROUP - 1, jnp.int32)
    for j in range(EXPERTS_PER_GROUP - 2, -1, -1):
        i1 = jnp.where(fl[j] == m1, j, i1)
    rest = [jnp.where(i1 == j, -jnp.inf, fl[j]) for j in range(EXPERTS_PER_GROUP)]
    m2 = functools.reduce(jnp.maximum, rest)
    i2 = jnp.full(m2.shape, EXPERTS_PER_GROUP - 1, jnp.int32)
    for j in range(EXPERTS_PER_GROUP - 2, -1, -1):
        i2 = jnp.where(jnp.logical_and(rest[j] == m2, i1 != j), j, i2)
    e2 = jnp.exp(m2 - m1)
    w1 = g_prob / (1.0 + e2)
    w2 = g_prob * e2 / (1.0 + e2)
    local = [jnp.where(i1 == j, w1, 0.0) + jnp.where(i2 == j, w2, 0.0) for j in range(EXPERTS_PER_GROUP)]
    return g_idx, local


def _outproj_kernel(a_ref, b_ref, wa_ref, wb_ref, x_ref, nw_ref, rw_ref, rb_ref,
                    x1_ref, h_ref, gk_ref, krow_ref, cnt_ref):
    x1 = (x_ref[...]
          + jnp.dot(a_ref[...], wa_ref[...], preferred_element_type=F32)
          + jnp.dot(b_ref[...], wb_ref[...], preferred_element_type=F32))
    x1_ref[...] = x1
    ms = jnp.mean(x1 * x1, axis=-1, keepdims=True)
    h = x1 * lax.rsqrt(ms + RMS_EPS) * nw_ref[...]
    h_ref[...] = h.astype(BF16)
    rw = rw_ref[...]
    hh = h.astype(BF16)
    hl = (h - hh.astype(F32)).astype(BF16)
    wh = rw.astype(BF16)
    wl = (rw - wh.astype(F32)).astype(BF16)
    hi_part = jnp.dot(hh, jnp.concatenate([wh, wl], axis=1), preferred_element_type=F32)
    logits = (hi_part[:, :LANES] + hi_part[:, LANES:]
              + jnp.dot(hl, wh, preferred_element_type=F32)) + rb_ref[...]
    g_idx, local = _route_gates(logits.T)
    tm = g_idx.shape[1]
    mem = (_iota2((8, tm), 0) == g_idx).astype(BF16)
    before = (_iota2((MOE_SUB, MOE_SUB), 0) < _iota2((MOE_SUB, MOE_SUB), 1)).astype(BF16)
    earlier = jnp.concatenate(
        [jnp.dot(mem[:, s:s + MOE_SUB], before, preferred_element_type=F32) for s in range(0, tm, MOE_SUB)],
        axis=1)
    rank = jnp.sum(mem.astype(F32) * earlier, axis=0, keepdims=True)
    key = g_idx.astype(F32) * MOE_KEY_STRIDE + rank
    krow_ref[...] = key
    sub_shift = int(math.log2(MOE_SUB))
    sub_sel = ((_iota2((tm, LANES), 0) >> sub_shift) == _iota2((tm, LANES), 1)).astype(BF16)
    cnt_ref[...] = jnp.dot(mem, sub_sel, preferred_element_type=F32).astype(jnp.int32)
    rows = jnp.concatenate([key] + local + [jnp.zeros((LANES - 1 - EXPERTS_PER_GROUP, tm), F32)], axis=0)
    gk_ref[...] = rows.T


def _outproj_route(a, b, a_blk, b_blk, w, x, nw, rw, rb, *, layer, tm, name):
    n, d = x.shape
    ka = kb = w.shape[0] // 2
    return pl.pallas_call(
        _outproj_kernel,
        grid=(n // tm,),
        in_specs=[
            pl.BlockSpec((tm, ka), lambda i: (i, a_blk)),
            pl.BlockSpec((tm, kb), lambda i: (i, b_blk)),
            pl.BlockSpec((ka, d), lambda i: (0, 0)),
            pl.BlockSpec((kb, d), lambda i: (1, 0)),
            pl.BlockSpec((tm, d), lambda i: (i, 0)),
            pl.BlockSpec((1, d), lambda i: (0, 0)),
            pl.BlockSpec((None, d, LANES), lambda i: (layer, 0, 0)),
            pl.BlockSpec((None, 1, LANES), lambda i: (layer, 0, 0)),
        ],
        out_specs=[
            pl.BlockSpec((tm, d), lambda i: (i, 0)),
            pl.BlockSpec((tm, d), lambda i: (i, 0)),
            pl.BlockSpec((tm, LANES), lambda i: (i, 0)),
            pl.BlockSpec((1, tm), lambda i: (0, i)),
            pl.BlockSpec((8, LANES), lambda i: (i, 0)),
        ],
        out_shape=[jax.ShapeDtypeStruct((n, d), F32),
                   jax.ShapeDtypeStruct((n, d), BF16),
                   jax.ShapeDtypeStruct((n, LANES), F32),
                   jax.ShapeDtypeStruct((1, n), F32),
                   jax.ShapeDtypeStruct((8 * (n // tm), LANES), jnp.int32)],
        compiler_params=pltpu.CompilerParams(
            dimension_semantics=("parallel",), vmem_limit_bytes=VMEM_LIMIT),
        name=name,
    )(a, b, w, w, x, nw, rw, rb)


def _round_up(v, m):
    return ((v + (m - 1)) // m) * m


def _round_up_pow2(v, m):
    return (v + (m - 1)) & ~(m - 1)


def _cdiv_pow2(v, m):
    return lax.shift_right_logical(v + (m - 1), int(math.log2(m)))


def _moe_kernel(cnt_ref, h_ref, krow_ref, gk_ref, wgu_ref, wd_ref, x_ref, fnw_ref, o_ref,
                hc_s, gc_s, y_s, *, final_norm):
    i = pl.program_id(0)
    g = pl.program_id(1)
    tm = h_ref.shape[0]
    n_sub = tm // MOE_SUB

    def segments(grp):
        cnts = [cnt_ref[(i * 8 + grp) * LANES + s] for s in range(n_sub)]
        starts = [jnp.int32(0)]
        for s in range(n_sub):
            starts.append(starts[-1] + _round_up_pow2(cnts[s], MOE_ALIGN))
        return cnts, starts

    counts, offs = segments(g)
    total = offs[-1]
    key0 = g.astype(F32) * MOE_KEY_STRIDE
    sub_iota = _iota2((MOE_BLK, MOE_SUB), 0).astype(F32)
    lane_iota = _iota2((MOE_SUB, MOE_BLK), 1).astype(F32)

    def pack(s, w):
        rows = slice(s * MOE_SUB, (s + 1) * MOE_SUB)
        base = key0 + (w * MOE_BLK).astype(F32)
        sel = (krow_ref[:, rows] == sub_iota + base).astype(BF16)
        dst = pl.ds(pl.multiple_of(offs[s] + w * MOE_BLK, MOE_ALIGN), MOE_BLK)
        hc_s[dst, :] = jnp.dot(sel, h_ref[rows, :], preferred_element_type=F32).astype(BF16)
        gc_s[dst, :] = _mm_01(sel, gk_ref[rows, :])

    for s in range(n_sub):
        def pack_more(w, carry, s=s):
            pack(s, w)
            return carry

        lax.fori_loop(1, _cdiv_pow2(counts[s], MOE_BLK), pack_more, 0)
    for s in range(n_sub):
        pack(s, jnp.int32(0))

    tail_rows = MOE_FIRST[-1] + MOE_FIRST_STEP
    tail = pl.ds(pl.multiple_of(total, MOE_ALIGN), tail_rows)
    hc_s[tail, :] = jnp.zeros((tail_rows, hc_s.shape[1]), BF16)
    gc_s[tail, :] = jnp.zeros((tail_rows, LANES), F32)

    def expert_rows(start, rows):
        blk = pl.ds(pl.multiple_of(start, MOE_FIRST_STEP), rows)
        hb = hc_s[blk, :]
        gates = gc_s[blk, :]
        y = jnp.zeros((rows, o_ref.shape[1]), F32)
        for e in range(EXPERTS_PER_GROUP):
            gu = jnp.dot(hb, wgu_ref[0, e], preferred_element_type=F32)
            act = _silu(gu[:, :D_EXPERT]) * gu[:, D_EXPERT:] * gates[:, 1 + e:2 + e]
            y = y + jnp.dot(act.astype(BF16), wd_ref[0, e], preferred_element_type=F32)
        y_s[g, blk, :] = y.astype(BF16)

    first = jnp.clip(_round_up_pow2(total, MOE_FIRST_STEP), MOE_FIRST[0], MOE_FIRST[-1])
    for size in MOE_FIRST:
        @pl.when(first == size)
        def _(size=size):
            expert_rows(0, size)

    n_rest = _cdiv_pow2(jnp.maximum(total - first, 0), MOE_BLK)

    def rest(bi, carry):
        expert_rows(first + bi * MOE_BLK, MOE_BLK)
        return carry

    lax.fori_loop(0, n_rest, rest, 0)
    done = pl.ds(pl.multiple_of(first + n_rest * MOE_BLK, MOE_FIRST_STEP), MOE_BLK)
    y_s[g, done, :] = jnp.zeros((MOE_BLK, y_s.shape[2]), BF16)

    @pl.when(g == N_GROUPS - 1)
    def _():
        segs = [segments(grp) for grp in range(N_GROUPS)]
        for s in range(n_sub):
            rows = slice(s * MOE_SUB, (s + 1) * MOE_SUB)
            key_col = gk_ref[rows, 0:1]
            sel = jnp.concatenate(
                [(key_col == lane_iota + grp * MOE_KEY_STRIDE).astype(BF16) for grp in range(N_GROUPS)], axis=1)
            packed = jnp.concatenate(
                [y_s[grp, pl.ds(pl.multiple_of(segs[grp][1][s], MOE_ALIGN), MOE_BLK), :]
                 for grp in range(N_GROUPS)], axis=0)
            o_ref[rows, :] = x_ref[rows, :] + jnp.dot(sel, packed, preferred_element_type=F32)
            for grp in range(N_GROUPS):
                def more(w, carry, s=s, rows=rows, grp=grp, key_col=key_col):
                    base = grp * MOE_KEY_STRIDE + (w * MOE_BLK).astype(F32)
                    sel_w = (key_col == lane_iota + base).astype(BF16)
                    src = pl.ds(pl.multiple_of(segs[grp][1][s] + w * MOE_BLK, MOE_ALIGN), MOE_BLK)
                    o_ref[rows, :] += jnp.dot(sel_w, y_s[grp, src, :], preferred_element_type=F32)
                    return carry

                lax.fori_loop(1, _cdiv_pow2(segs[grp][0][s], MOE_BLK), more, 0)
        if final_norm:
            y = o_ref[...]
            ms = jnp.mean(y * y, axis=-1, keepdims=True)
            o_ref[...] = y * lax.rsqrt(ms + RMS_EPS) * fnw_ref[...]


def _moe(h, krow, gk, counts, wgu, wd, x, fnw, *, layer, tm, final_norm, name):
    n, d = x.shape
    buf_rows = _round_up(tm + (tm // MOE_SUB) * MOE_ALIGN + MOE_FIRST[-1] + MOE_FIRST_STEP, MOE_BLK)
    grid_spec = pltpu.PrefetchScalarGridSpec(
        num_scalar_prefetch=1,
        grid=(n // tm, N_GROUPS),
        in_specs=[
            pl.BlockSpec((tm, d), lambda i, g, c: (i, 0)),
            pl.BlockSpec((1, tm), lambda i, g, c: (0, i)),
            pl.BlockSpec((tm, LANES), lambda i, g, c: (i, 0)),
            pl.BlockSpec((1, EXPERTS_PER_GROUP, d, 2 * D_EXPERT), lambda i, g, c: (layer, g, 0, 0)),
            pl.BlockSpec((1, EXPERTS_PER_GROUP, D_EXPERT, d), lambda i, g, c: (layer, g, 0, 0)),
            pl.BlockSpec((tm, d), lambda i, g, c: (i, 0)),
            pl.BlockSpec((1, d), lambda i, g, c: (0, 0)),
        ],
        out_specs=pl.BlockSpec((tm, d), lambda i, g, c: (i, 0)),
        scratch_shapes=[pltpu.VMEM((buf_rows, d), BF16),
                        pltpu.VMEM((buf_rows, LANES), F32),
                        pltpu.VMEM((N_GROUPS, buf_rows, d), BF16)],
    )
    return pl.pallas_call(
        functools.partial(_moe_kernel, final_norm=final_norm),
        grid_spec=grid_spec,
        out_shape=jax.ShapeDtypeStruct((n, d), F32),
        compiler_params=pltpu.CompilerParams(
            dimension_semantics=("parallel", "arbitrary"), vmem_limit_bytes=VMEM_LIMIT),
        name=name,
    )(counts, h, krow, gk, wgu, wd, x, fnw)


def _pad_cols(a, width):
    return jnp.pad(a, ((0, 0), (0, width - a.shape[1])))


def _rope_tables(seq):
    inv = (1.0 / (ROPE_BASE ** np.linspace(0.0, 1.0, RET_DK // 2, dtype=np.float32))).astype(np.float32)
    ang = np.arange(seq, dtype=np.float32)[:, None] * inv[None, :]
    return jnp.asarray(np.cos(ang), dtype=F32), jnp.asarray(np.sin(ang), dtype=F32)


def _router_params(wc, bc, wf, bf):
    rw = jnp.concatenate([wc, wf], axis=2)
    rb = jnp.concatenate([bc, bf], axis=1)[:, None, :]
    pad = ((0, 0), (0, 0), (0, LANES - rw.shape[2]))
    return jnp.pad(rw, pad), jnp.pad(rb, pad)


def kernel(x, norm_mix_w, norm_ffn_w, even_w_in, gdn_conv_w, gdn_a_log, gdn_dt_bias, gdn_norm_w, hgrn_lb_logits, hgrn_norm_w, even_w_out, odd_w_in, ret_norm_w, odd_w_out, router_c_w, router_c_b, router_f_w, router_f_b, moe_w_gate_up, moe_w_down, final_norm_w):
    bsz, seq, d = x.shape
    n = bsz * seq
    xt = x.reshape(n, d)
    mix_w = N_HEADS * HEAD_D
    conv_cols = 3 * mix_w
    gdn_main = conv_cols + mix_w

    w_in = even_w_in[0]
    small0 = gdn_main
    w_small = jnp.concatenate(
        [_pad_cols(w_in[:, small0:small0 + N_HEADS], LANES),
         _pad_cols(w_in[:, small0 + N_HEADS:small0 + 2 * N_HEADS], LANES),
         _pad_cols(w_in[:, small0 + 4 * mix_w:], LANES)], axis=1).astype(BF16)
    proj, small, o_b, wgu_b, wd_b, w_out, w_odd, w_out1 = _proj_even(
        xt, norm_mix_w[0][None, :], w_in.astype(BF16), w_small, hgrn_lb_logits, hgrn_norm_w[0][None, :],
        (moe_w_gate_up.reshape(-1, moe_w_gate_up.shape[-1]), moe_w_down.reshape(-1, moe_w_down.shape[-1]),
         even_w_out[0], odd_w_in[0], odd_w_out[0]),
        gdn_w=gdn_main, tm=512, tn=1024, tokens_per_seq=seq, layer=0, name="in_proj_even_hgrn2")
    wgu_b = wgu_b.reshape(moe_w_gate_up.shape)
    wd_b = wd_b.reshape(moe_w_down.shape)
    proj = proj.reshape(bsz, seq, -1)
    small = small.reshape(bsz, seq, -1)
    o_a = _gdn(proj, small, gdn_conv_w[0], _pad_cols(gdn_a_log[0][None, :], LANES),
               _pad_cols(gdn_dt_bias[0][None, :], LANES), gdn_norm_w[0][None, :], tb=512)
    assert ROUTE_TM == MOE_TM
    rw, rb = _router_params(router_c_w, router_c_b, router_f_w, router_f_b)
    x1, h, gk, krow, cnt = _outproj_route(o_a.reshape(n, mix_w), o_b.reshape(n, mix_w), 0, 0,
                                          w_out, xt, norm_ffn_w[0][None, :], rw, rb,
                                          layer=0, tm=ROUTE_TM, name="out_proj_even")
    x2 = _moe(h, krow, gk, cnt.reshape(-1), wgu_b, wd_b, x1,
              final_norm_w[None, :], layer=0, tm=MOE_TM, final_norm=False, name="moe0")

    cos, sin = _rope_tables(seq)
    o_c = _proj_retention(x2, norm_mix_w[1][None, :], w_odd, cos, sin, ret_norm_w[0][None, :],
                          tm=512, tn=1024, tc=RET_CHUNK, tokens_per_seq=seq, name="in_proj_retention")
    x3, h, gk, krow, cnt = _outproj_route(o_c, o_c, 0, 1, w_out1, x2,
                                          norm_ffn_w[1][None, :], rw, rb, layer=1, tm=ROUTE_TM, name="out_proj_odd")
    out = _moe(h, krow, gk, cnt.reshape(-1), wgu_b, wd_b, x3,
               final_norm_w[None, :], layer=1, tm=MOE_TM, final_norm=True, name="moe1")
    return out.reshape(bsz, seq, d)
```

```python
import functools
import math

import jax
import jax.numpy as jnp
import numpy as np
from jax import lax
from jax.experimental import pallas as pl
from jax.experimental.pallas import tpu as pltpu

F32 = jnp.float32
BF16 = jnp.bfloat16

D_MODEL = 1024
RMS_EPS = 1e-6
L2_EPS = 1e-6
CHUNK = 64
CONV_K = 4
N_HEADS = 4
HEAD_D = 128
RET_DK = 256
RET_DV = 512
RET_CHUNK = 256
ROPE_BASE = 10000.0
N_GROUPS = 4
EXPERTS_PER_GROUP = 4
N_EXPERTS = 16
D_EXPERT = 256
LANES = 128
ROUTE_TM = 1024
MOE_TM = 1024
MOE_SUB = 256
MOE_BLK = 128
MOE_ALIGN = 16
MOE_KEY_STRIDE = 4096.0
MOE_FIRST_STEP = 32
MOE_FIRST = (256, 288, 320, 352)
VMEM_LIMIT = 56 * 1024 * 1024


def _mm(a, b):
    return jnp.dot(a.astype(BF16), b.astype(BF16), preferred_element_type=F32)


def _mm_nt(a, b):
    return lax.dot_general(a.astype(BF16), b.astype(BF16), (((1,), (1,)), ((), ())),
                           preferred_element_type=F32)


def _mm_tn(a, b):
    return _mm(a.T, b)


def _mm_01(m01, x):
    hi = x.astype(BF16)
    lo = (x - hi.astype(F32)).astype(BF16)
    return (jnp.dot(m01, hi, preferred_element_type=F32)
            + jnp.dot(m01, lo, preferred_element_type=F32))


def _bmm(a, b):
    return jnp.einsum('cik,ckj->cij', a.astype(BF16), b.astype(BF16), preferred_element_type=F32)


def _bmm_nt(a, b):
    return jnp.einsum('cik,cjk->cij', a.astype(BF16), b.astype(BF16), preferred_element_type=F32)


def _bmm_tn(a, b):
    return _bmm(jnp.swapaxes(a, 1, 2), b)


def _bmm_01(m01, x):
    hi = x.astype(BF16)
    lo = (x - hi.astype(F32)).astype(BF16)
    return (jnp.einsum('cik,ckj->cij', m01, hi, preferred_element_type=F32)
            + jnp.einsum('cik,ckj->cij', m01, lo, preferred_element_type=F32))


def _sigmoid(x):
    return 1.0 / (1.0 + jnp.exp2(x * (-math.log2(math.e))))


def _silu(x):
    return x * _sigmoid(x)


def _softplus(x):
    return jnp.maximum(x, 0.0) + jnp.log(1.0 + jnp.exp(-jnp.abs(x)))


def _iota2(shape, dim):
    return lax.broadcasted_iota(jnp.int32, shape, dim)


def _head_rms_gate(o, gate, nw, width):
    outs = []
    for h in range(o.shape[1] // width):
        oh = o[:, h * width:(h + 1) * width]
        ms = jnp.mean(oh * oh, axis=-1, keepdims=True)
        outs.append(oh * lax.rsqrt(ms + RMS_EPS) * nw)
    return jnp.concatenate(outs, axis=1) * _silu(gate)


def _proj_even_kernel(*refs, tn, n_cast, tiles_per_seq, layer, shift):
    (x_ref, nw_ref, wg_ref, wh_ref, ws_ref, lb_ref, hnw_ref), rest = refs[:7], refs[7:]
    cast_in, rest = rest[:n_cast], rest[n_cast:]
    (o_ref, os_ref, ob_ref), rest = rest[:3], rest[3:]
    cast_out, rest = rest[:n_cast], rest[n_cast:]
    hg_s, whg_s, q_s, k_s, lf_s, i_s, oh_s, st_ref = rest
    for src, dst in zip(cast_in, cast_out):
        dst[...] = src[...].astype(dst.dtype)

    @pl.when(pl.program_id(0) == 0)
    def _():
        sel = (_iota2((2 * LANES, LANES), 0) == _iota2((2 * LANES, LANES), 1) + shift).astype(BF16)
        n_blocks = wh_ref.shape[1] // LANES
        for j in range(n_blocks):
            if j + 1 < n_blocks:
                both = wh_ref[:, j * LANES:(j + 2) * LANES]
            else:
                both = jnp.concatenate([wh_ref[:, j * LANES:], ws_ref[:, 2 * LANES:]], axis=1)
            whg_s[:, j * LANES:(j + 1) * LANES] = jnp.dot(both, sel, preferred_element_type=F32).astype(BF16)

    @pl.when(pl.program_id(0) % tiles_per_seq == 0)
    def _():
        st_ref[...] = jnp.zeros(st_ref.shape, F32)

    x = x_ref[...]
    tm = x.shape[0]
    ms = jnp.mean(x * x, axis=-1, keepdims=True)
    hb = (x * lax.rsqrt(ms + RMS_EPS) * nw_ref[...]).astype(BF16)
    os_ref[...] = jnp.dot(hb, ws_ref[:, :2 * LANES], preferred_element_type=F32)
    for j in range(whg_s.shape[1] // tn):
        cols = slice(j * tn, (j + 1) * tn)
        hg_s[:, cols] = jnp.dot(hb, whg_s[:, cols], preferred_element_type=F32)
    w = N_HEADS * HEAD_D

    def gdn_cols(c0, width):
        def run():
            cols = slice(c0, c0 + width)
            o_ref[:, cols] = jnp.dot(hb, wg_ref[:, cols], preferred_element_type=F32).astype(o_ref.dtype)
        return run

    piece = 2 * LANES
    fillers = [gdn_cols(c0, piece) for c0 in range(0, wg_ref.shape[1], piece)]
    ob_ref[...] = _hgrn_block(hg_s[:, 0:w], hg_s[:, w:2 * w], hg_s[:, 2 * w:3 * w], hg_s[:, 3 * w:4 * w],
                              lb_ref, hnw_ref, q_s, k_s, lf_s, i_s, oh_s, st_ref,
                              tb=tm, layer=layer, fillers=fillers).astype(ob_ref.dtype)


def _proj_even(x, nw, w, w_small, lb_logits, hgrn_nw, cast, *, gdn_w, tm, tn, tokens_per_seq, layer, name):
    n, d = x.shape
    steps = n // tm
    mix_w = N_HEADS * HEAD_D
    hgrn_w = 4 * mix_w
    shift = w.shape[1] - gdn_w - hgrn_w
    assert gdn_w == hgrn_w and 0 < shift < LANES and w_small.shape[1] == 3 * LANES
    const = lambda i: (0, 0)
    tile = lambda i: (i, 0)
    in_specs = [pl.BlockSpec((tm, d), tile), pl.BlockSpec((1, d), const),
                pl.BlockSpec((d, gdn_w), const), pl.BlockSpec((d, hgrn_w), lambda i: (0, 1)),
                pl.BlockSpec(w_small.shape, const), pl.BlockSpec(lb_logits.shape, const),
                pl.BlockSpec((1, HEAD_D), const)]
    out_specs = [pl.BlockSpec((tm, gdn_w), tile), pl.BlockSpec((tm, 2 * LANES), tile),
                 pl.BlockSpec((tm, mix_w), tile)]
    out_shape = [jax.ShapeDtypeStruct((n, gdn_w), BF16), jax.ShapeDtypeStruct((n, 2 * LANES), F32),
                 jax.ShapeDtypeStruct((n, mix_w), BF16)]
    for a in cast:
        slab = (a.shape[0] // steps, a.shape[1])
        in_specs.append(pl.BlockSpec(slab, tile))
        out_specs.append(pl.BlockSpec(slab, tile))
        out_shape.append(jax.ShapeDtypeStruct(a.shape, BF16))
    scratch = ([pltpu.VMEM((tm, hgrn_w), F32), pltpu.VMEM((d, hgrn_w), BF16)] + [pltpu.VMEM((tm, mix_w), F32)] * 5
               + [pltpu.VMEM((N_HEADS, HEAD_D, HEAD_D), F32)])
    return pl.pallas_call(
        functools.partial(_proj_even_kernel, tn=tn, n_cast=len(cast), tiles_per_seq=tokens_per_seq // tm,
                          layer=layer, shift=shift),
        grid=(steps,),
        in_specs=in_specs,
        out_specs=out_specs,
        out_shape=out_shape,
        scratch_shapes=scratch,
        compiler_params=pltpu.CompilerParams(
            dimension_semantics=("arbitrary",), vmem_limit_bytes=VMEM_LIMIT),
        name=name,
    )(x, nw, w, w, w_small, lb_logits, hgrn_nw, *cast)


def _gdn_kernel(qkv_ref, z_ref, sm_ref, cw_ref, alog_ref, dtb_ref, nw_ref, o_ref,
                xs_ref, qkv_s, o_s, m_s, sq_s, qe_s, dec_s, st_ref, *, tb):
    @pl.when(pl.program_id(1) == 0)
    def _():
        xs_ref[0:8, :] = jnp.zeros((8, xs_ref.shape[1]), F32)
        st_ref[...] = jnp.zeros(st_ref.shape, F32)

    c = CHUNK
    nc = tb // c
    nb = N_HEADS * nc
    qk_w = N_HEADS * HEAD_D
    row = _iota2((c, c), 0)
    col = _iota2((c, c), 1)
    causal = (row >= col)[None]
    strict = (row > col)[None]
    same16 = ((row >> 4) == (col >> 4))[None]
    same32 = ((row >> 5) == (col >> 5))[None]
    eye = (row == col).astype(F32)[None]
    ltri = jnp.broadcast_to((row >= col).astype(BF16)[None], (nc, c, c))
    lane = _iota2((nb, c, LANES), 2)

    def per_head(fn):
        return jnp.concatenate([fn(h) for h in range(N_HEADS)], axis=0)

    for h in range(N_HEADS):
        for base, scale in ((0, HEAD_D ** -0.5), (qk_w, 1.0), (2 * qk_w, None)):
            cols = slice(base + h * HEAD_D, base + (h + 1) * HEAD_D)
            x = qkv_ref[0, :, cols].astype(F32)
            xs_ref[8:8 + tb, cols] = x
            cw = cw_ref[:, cols]
            y = x * cw[CONV_K - 1:CONV_K, :]
            for j in range(1, CONV_K):
                y = y + xs_ref[pl.ds(8 - j, tb), cols] * cw[CONV_K - 1 - j:CONV_K - j, :]
            y = _silu(y)
            if scale is not None:
                y = y * (lax.rsqrt(jnp.sum(y * y, axis=-1, keepdims=True) + L2_EPS) * scale)
            qkv_s[:, cols] = y
    xs_ref[0:8, :] = xs_ref[tb:tb + 8, :]

    sm = sm_ref[0]
    beta_all = _sigmoid(sm[:, :LANES]).reshape(nc, c, LANES)
    rate = -jnp.exp(alog_ref[...]) * math.log2(math.e)
    g_all = (rate * _softplus(sm[:, LANES:] + dtb_ref[...])).reshape(nc, c, LANES)
    gc_all = _bmm_01(ltri, g_all)
    q = per_head(lambda h: qkv_s[:, h * HEAD_D:(h + 1) * HEAD_D].reshape(nc, c, HEAD_D))
    k = per_head(lambda h: qkv_s[:, qk_w + h * HEAD_D:qk_w + (h + 1) * HEAD_D].reshape(nc, c, HEAD_D))
    v = per_head(lambda h: qkv_s[:, 2 * qk_w + h * HEAD_D:2 * qk_w + (h + 1) * HEAD_D].reshape(nc, c, HEAD_D))
    beta = per_head(lambda h: beta_all[:, :, h:h + 1])
    gcol = per_head(lambda h: gc_all[:, :, h:h + 1])
    g_hi = gcol.astype(BF16).astype(F32)
    g_mid = (gcol - g_hi).astype(BF16).astype(F32)
    g_lo = gcol - g_hi - g_mid
    pieces = jnp.where(lane == 0, g_hi, jnp.where(lane == 1, g_mid, jnp.where(lane == 2, g_lo, 0.0)))
    lane1 = _iota2((1, 1, LANES), 2)
    lhs = pieces + jnp.where(jnp.logical_and(lane1 >= 3, lane1 < 6), 1.0, 0.0)
    rhs_g = jnp.where(lane1 < 3, 1.0, 0.0) - pltpu.roll(pieces, 3, axis=2)
    decay = jnp.exp2(jnp.where(causal, _bmm_nt(lhs, rhs_g), -jnp.inf))
    kb = k * beta
    a = jnp.where(strict, _bmm_nt(kb, k) * decay, 0.0)
    d = jnp.where(same16, a, 0.0)
    x_inv = eye - d
    dp = d
    for _ in range(3):
        dp = _bmm(dp, dp)
        x_inv = x_inv + _bmm(x_inv, dp)
    e = jnp.where(jnp.logical_and(same32, jnp.logical_not(same16)), a, 0.0)
    x_inv = x_inv - _bmm(x_inv, _bmm(e, x_inv))
    f = jnp.where(same32, 0.0, a)
    x_inv = x_inv - _bmm(x_inv, _bmm(f, x_inv))
    wu = _bmm(x_inv, jnp.concatenate([kb * jnp.exp2(gcol), v * beta], axis=2))
    attn = _bmm_nt(q, k) * decay
    g_last = gcol[:, c - 1:c, :]
    kd = k * jnp.exp2(g_last - gcol)
    mq = _bmm_tn(kd, wu)
    aw = _bmm(attn, wu)
    m_s[...] = mq[:, :, :HEAD_D].astype(BF16).reshape(N_HEADS, nc, HEAD_D, HEAD_D)
    sq_s[...] = mq[:, :, HEAD_D:].reshape(N_HEADS, nc, HEAD_D, HEAD_D)
    qe_s[...] = (q * jnp.exp2(gcol) - aw[:, :, :HEAD_D]).astype(BF16).reshape(N_HEADS, nc, c, HEAD_D)
    dec_s[...] = jnp.broadcast_to(jnp.exp2(g_last), (nb, 1, HEAD_D)).reshape(N_HEADS, nc, 1, HEAD_D)
    for h in range(N_HEADS):
        o_s[:, h * HEAD_D:(h + 1) * HEAD_D] = aw[h * nc:(h + 1) * nc, :, HEAD_D:].reshape(tb, HEAD_D)

    for ci in range(nc):
        for h in range(N_HEADS):
            hs = slice(h * HEAD_D, (h + 1) * HEAD_D)
            s_h = st_ref[h]
            s_b = s_h.astype(BF16)
            o_s[ci * c:(ci + 1) * c, hs] += jnp.dot(qe_s[h, ci], s_b, preferred_element_type=F32)
            st_ref[h] = (s_h * dec_s[h, ci] - jnp.dot(m_s[h, ci], s_b, preferred_element_type=F32)
                         + sq_s[h, ci])

    o_ref[0] = _head_rms_gate(o_s[...], z_ref[0].astype(F32), nw_ref[...], HEAD_D).astype(o_ref.dtype)


def _gdn(proj, small, conv_w, a_log, dt_bias, norm_w, *, tb):
    b, t, _ = proj.shape
    conv_cols = 3 * N_HEADS * HEAD_D
    mix_w = N_HEADS * HEAD_D
    return pl.pallas_call(
        functools.partial(_gdn_kernel, tb=tb),
        grid=(b, t // tb),
        in_specs=[
            pl.BlockSpec((1, tb, conv_cols), lambda i, j: (i, j, 0)),
            pl.BlockSpec((1, tb, mix_w), lambda i, j: (i, j, conv_cols // mix_w)),
            pl.BlockSpec((1, tb, 2 * LANES), lambda i, j: (i, j, 0)),
            pl.BlockSpec((CONV_K, conv_cols), lambda i, j: (0, 0)),
            pl.BlockSpec((1, LANES), lambda i, j: (0, 0)),
            pl.BlockSpec((1, LANES), lambda i, j: (0, 0)),
            pl.BlockSpec((1, HEAD_D), lambda i, j: (0, 0)),
        ],
        out_specs=pl.BlockSpec((1, tb, mix_w), lambda i, j: (i, j, 0)),
        out_shape=jax.ShapeDtypeStruct((b, t, mix_w), BF16),
        scratch_shapes=[
            pltpu.VMEM((tb + 8, conv_cols), F32),
            pltpu.VMEM((tb, conv_cols), F32),
            pltpu.VMEM((tb, mix_w), F32),
            pltpu.VMEM((N_HEADS, tb // CHUNK, HEAD_D, HEAD_D), BF16),
            pltpu.VMEM((N_HEADS, tb // CHUNK, HEAD_D, HEAD_D), F32),
            pltpu.VMEM((N_HEADS, tb // CHUNK, CHUNK, HEAD_D), BF16),
            pltpu.VMEM((N_HEADS, tb // CHUNK, 1, HEAD_D), F32),
            pltpu.VMEM((N_HEADS, HEAD_D, HEAD_D), F32),
        ],
        compiler_params=pltpu.CompilerParams(
            dimension_semantics=("parallel", "arbitrary"), vmem_limit_bytes=VMEM_LIMIT),
        name="gdn",
    )(proj, proj, small, conv_w, a_log, dt_bias, norm_w)


def _hgrn_block(q_in, f_in, i_in, gate_in, lb_ref, nw_ref, q_s, k_s, lf_s, i_s, o_s, st_ref, *, tb, layer,
                fillers=()):
    lbl = lb_ref[...]
    e_lb = jnp.exp(lbl - jnp.max(lbl, axis=0, keepdims=True))
    lb = jnp.sum(e_lb[:layer + 1], axis=0, keepdims=True) / jnp.sum(e_lb, axis=0, keepdims=True)

    f = lb + (1.0 - lb) * _sigmoid(f_in)
    k_s[...] = 1.0 - f
    lf_s[...] = jnp.log2(f)
    q_s[...] = _silu(q_in) * (HEAD_D ** -0.5)
    i_s[...] = i_in

    c = CHUNK
    nc = tb // c
    blk = 8
    row = _iota2((c, c), 0)
    col = _iota2((c, c), 1)
    ltri = jnp.broadcast_to((row >= col).astype(BF16)[None], (nc, c, c))
    level_masks = {}
    for m in (32, 16, 8):
        sh = int(math.log2(2 * m))
        level_masks[m] = jnp.logical_and(
            (row >> sh) == (col >> sh),
            jnp.logical_and((row & (2 * m - 1)) >= m, (col & (2 * m - 1)) < m))[None]
    nb = N_HEADS * nc
    sub = _iota2((nb * c // blk, blk, HEAD_D), 1)

    def per_head(fn):
        return jnp.concatenate([fn(slice(h * HEAD_D, (h + 1) * HEAD_D)) for h in range(N_HEADS)], axis=0)

    b_all = _bmm_01(ltri, lf_s[...].reshape(nc, c, N_HEADS * HEAD_D))
    b = per_head(lambda sl: b_all[:, :, sl])
    q = per_head(lambda sl: q_s[:, sl].reshape(nc, c, HEAD_D))
    k = per_head(lambda sl: k_s[:, sl].reshape(nc, c, HEAD_D))
    iv = per_head(lambda sl: i_s[:, sl].reshape(nc, c, HEAD_D))
    attn = jnp.zeros((nb, c, c), F32)
    for m in (32, 16, 8):
        b_m = b.reshape(nb * c // (2 * m), 2 * m, HEAD_D)
        ref = jnp.broadcast_to(b_m[:, m:m + 1, :], b_m.shape).reshape(nb, c, HEAD_D)
        e = jnp.exp2(-jnp.abs(b - ref))
        attn = attn + jnp.where(level_masks[m], _bmm_nt(q * e, k * e), 0.0)
    o = _bmm(attn, iv)
    qb, kb, bb, ib = (a.reshape(nb * c // blk, blk, HEAD_D) for a in (q, k, b, iv))
    ob = jnp.zeros(qb.shape, F32)
    fillers = list(fillers)
    for s in range(blk):
        if fillers:
            fillers.pop(0)()
        dec = jnp.exp2(jnp.where(sub >= s, bb - bb[:, s:s + 1, :], -jnp.inf))
        a_col = jnp.sum(dec * qb * kb[:, s:s + 1, :], axis=-1, keepdims=True)
        ob = ob + a_col * ib[:, s:s + 1, :]
    for fn in fillers:
        fn()
    o = o + ob.reshape(nb, c, HEAD_D)
    b_last = b[:, c - 1:c, :]
    q_in = (q * jnp.exp2(b)).astype(BF16)
    upd = _bmm_tn(iv, k * jnp.exp2(b_last - b))
    keep = jnp.exp2(b_last)
    for h in range(N_HEADS):
        s_t = st_ref[h]
        outs = []
        for ci in range(h * nc, (h + 1) * nc):
            outs.append(o[ci] + _mm_nt(q_in[ci], s_t))
            s_t = s_t * keep[ci] + upd[ci]
        st_ref[h] = s_t
        o_s[:, h * HEAD_D:(h + 1) * HEAD_D] = jnp.concatenate(outs, axis=0)

    return _head_rms_gate(o_s[...], gate_in, nw_ref[...], HEAD_D)


def _proj_ret_kernel(x_ref, nw_ref, w_ref, cos_ref, sin_ref, rnw_ref, o_ref,
                     wq_s, proj_s, dmat_s, st_ref, *, tn, tc, tiles_per_seq):
    i = pl.program_id(0)
    qk_w = N_HEADS * RET_DK
    v_w = N_HEADS * RET_DV
    half = RET_DK // 2
    log_gammas = [math.log(1.0 - 2.0 ** (-5.0 - h)) for h in range(N_HEADS)]

    @pl.when(i == 0)
    def _():
        src = _iota2((RET_DK, RET_DK), 0)
        dst = _iota2((RET_DK, RET_DK), 1)
        split = (src == jnp.where(dst < half, 2 * dst, 2 * (dst - half) + 1)).astype(BF16)
        for hd in range(2 * N_HEADS):
            cols = slice(hd * RET_DK, (hd + 1) * RET_DK)
            wq_s[:, cols] = jnp.dot(w_ref[:, cols], split, preferred_element_type=F32).astype(BF16)
        rel = (_iota2((tc, tc), 0) - _iota2((tc, tc), 1)).astype(F32)
        for h in range(N_HEADS):
            dmat_s[h] = jnp.where(rel >= 0, jnp.exp(jnp.maximum(rel, 0.0) * log_gammas[h]), 0.0)

    @pl.when(i % tiles_per_seq == 0)
    def _():
        st_ref[...] = jnp.zeros(st_ref.shape, F32)

    x = x_ref[...]
    tm = x.shape[0]
    ms = jnp.mean(x * x, axis=-1, keepdims=True)
    hb = (x * lax.rsqrt(ms + RMS_EPS) * nw_ref[...]).astype(BF16)
    gate0 = 2 * qk_w + v_w
    for j in range(gate0 // tn):
        cols = slice(j * tn, (j + 1) * tn)
        w_tile = wq_s[:, cols] if (j + 1) * tn <= 2 * qk_w else w_ref[:, cols]
        proj_s[:, cols] = jnp.dot(hb, w_tile, preferred_element_type=F32).astype(BF16)

    pos = _iota2((tc, 1), 0).astype(F32)
    for ck in range(tm // tc):
        rows = slice(ck * tc, (ck + 1) * tc)
        cos = cos_ref[rows, :]
        sin = sin_ref[rows, :]

        def rot(xh):
            x1, x2 = xh[:, :half], xh[:, half:]
            return jnp.concatenate([x1 * cos - x2 * sin, x1 * sin + x2 * cos], axis=1)

        for h in range(N_HEADS):
            lg = log_gammas[h]
            gcols = slice(gate0 + h * RET_DV, gate0 + (h + 1) * RET_DV)
            proj_s[rows, gcols] = jnp.dot(hb[rows], w_ref[:, gcols], preferred_element_type=F32).astype(BF16)
            q = rot(proj_s[rows, h * RET_DK:(h + 1) * RET_DK].astype(F32))
            k = rot(proj_s[rows, qk_w + h * RET_DK:qk_w + (h + 1) * RET_DK].astype(F32)) * (RET_DK ** -0.5)
            v = proj_s[rows, 2 * qk_w + h * RET_DV:2 * qk_w + (h + 1) * RET_DV]
            gate = proj_s[rows, 2 * qk_w + v_w + h * RET_DV:2 * qk_w + v_w + (h + 1) * RET_DV]
            s = st_ref[h]
            attn = _mm_nt(q, k) * dmat_s[h]
            o = _mm(q * jnp.exp((pos + 1.0) * lg), s) + _mm(attn, v)
            st_ref[h] = s * math.exp(tc * lg) + _mm_tn(k * jnp.exp((tc - 1.0 - pos) * lg), v)
            ms_o = jnp.mean(o * o, axis=-1, keepdims=True)
            o_ref[rows, h * RET_DV:(h + 1) * RET_DV] = (o * lax.rsqrt(ms_o + RMS_EPS) * rnw_ref[...]
                                                        * _silu(gate.astype(F32))).astype(o_ref.dtype)


def _proj_retention(x, nw, w, cos, sin, ret_nw, *, tm, tn, tc, tokens_per_seq, name):
    n, d = x.shape
    tiles_per_seq = tokens_per_seq // tm
    v_w = N_HEADS * RET_DV
    return pl.pallas_call(
        functools.partial(_proj_ret_kernel, tn=tn, tc=tc, tiles_per_seq=tiles_per_seq),
        grid=(n // tm,),
        in_specs=[
            pl.BlockSpec((tm, d), lambda i: (i, 0)),
            pl.BlockSpec((1, d), lambda i: (0, 0)),
            pl.BlockSpec(w.shape, lambda i: (0, 0)),
            pl.BlockSpec((tm, RET_DK // 2), lambda i: (i % tiles_per_seq, 0)),
            pl.BlockSpec((tm, RET_DK // 2), lambda i: (i % tiles_per_seq, 0)),
            pl.BlockSpec((1, RET_DV), lambda i: (0, 0)),
        ],
        out_specs=pl.BlockSpec((tm, v_w), lambda i: (i, 0)),
        out_shape=jax.ShapeDtypeStruct((n, v_w), BF16),
        scratch_shapes=[
            pltpu.VMEM((d, 2 * N_HEADS * RET_DK), BF16),
            pltpu.VMEM((tm, w.shape[1]), BF16),
            pltpu.VMEM((N_HEADS, tc, tc), F32),
            pltpu.VMEM((N_HEADS, RET_DK, RET_DV), F32),
        ],
        compiler_params=pltpu.CompilerParams(
            dimension_semantics=("arbitrary",), vmem_limit_bytes=VMEM_LIMIT),
        name=name,
    )(x, nw, w, cos, sin, ret_nw)


def _route_gates(logits_t):
    cl = [logits_t[g:g + 1, :] for g in range(N_GROUPS)]
    cmax = functools.reduce(jnp.maximum, cl)
    denom = sum(jnp.exp(x - cmax) for x in cl)
    g_prob = 1.0 / denom
    g_idx = jnp.full(cmax.shape, N_GROUPS - 1, jnp.int32)
    for g in range(N_GROUPS - 2, -1, -1):
        g_idx = jnp.where(cl[g] == cmax, g, g_idx)
    def fine_row(g, j):
        r = N_GROUPS + g * EXPERTS_PER_GROUP + j
        return logits_t[r:r + 1, :]

    fl = []
    for j in range(EXPERTS_PER_GROUP):
        x = fine_row(N_GROUPS - 1, j)
        for g in range(N_GROUPS - 2, -1, -1):
            x = jnp.where(g_idx == g, fine_row(g, j), x)
        fl.append(x)
    m1 = functools.reduce(jnp.maximum, fl)
    i1 = jnp.full(m1.shape, EXPERTS_PER_GROUP - 1, jnp.int32)
    for j in range(EXPERTS_PER_GROUP - 2, -1, -1):
        i1 = jnp.where(fl[j] == m1, j, i1)
    rest = [jnp.where(i1 == j, -jnp.inf, fl[j]) for j in range(EXPERTS_PER_GROUP)]
    m2 = functools.reduce(jnp.maximum, rest)
    i2 = jnp.full(m2.shape, EXPERTS_PER_GROUP - 1, jnp.int32)
    for j in range(EXPERTS_PER_GROUP - 2, -1, -1):
        i2 = jnp.where(jnp.logical_and(rest[j] == m2, i1 != j), j, i2)
    e2 = jnp.exp(m2 - m1)
    w1 = g_prob / (1.0 + e2)
    w2 = g_prob * e2 / (1.0 + e2)
    local = [jnp.where(i1 == j, w1, 0.0) + jnp.where(i2 == j, w2, 0.0) for j in range(EXPERTS_PER_GROUP)]
    return g_idx, local


def _outproj_kernel(a_ref, b_ref, wa_ref, wb_ref, x_ref, nw_ref, rw_ref, rb_ref,
                    x1_ref, h_ref, gk_ref, krow_ref, cnt_ref):
    x1 = (x_ref[...]
          + jnp.dot(a_ref[...], wa_ref[...], preferred_element_type=F32)
          + jnp.dot(b_ref[...], wb_ref[...], preferred_element_type=F32))
    x1_ref[...] = x1
    ms = jnp.mean(x1 * x1, axis=-1, keepdims=True)
    h = x1 * lax.rsqrt(ms + RMS_EPS) * nw_ref[...]
    h_ref[...] = h.astype(BF16)
    rw = rw_ref[...]
    hh = h.astype(BF16)
    hl = (h - hh.astype(F32)).astype(BF16)
    wh = rw.astype(BF16)
    wl = (rw - wh.astype(F32)).astype(BF16)
    hi_part = jnp.dot(hh, jnp.concatenate([wh, wl], axis=1), preferred_element_type=F32)
    logits = (hi_part[:, :LANES] + hi_part[:, LANES:]
              + jnp.dot(hl, wh, preferred_element_type=F32)) + rb_ref[...]
    g_idx, local = _route_gates(logits.T)
    tm = g_idx.shape[1]
    mem = (_iota2((8, tm), 0) == g_idx).astype(BF16)
    before = (_iota2((MOE_SUB, MOE_SUB), 0) < _iota2((MOE_SUB, MOE_SUB), 1)).astype(BF16)
    earlier = jnp.concatenate(
        [jnp.dot(mem[:, s:s + MOE_SUB], before, preferred_element_type=F32) for s in range(0, tm, MOE_SUB)],
        axis=1)
    rank = jnp.sum(mem.astype(F32) * earlier, axis=0, keepdims=True)
    key = g_idx.astype(F32) * MOE_KEY_STRIDE + rank
    krow_ref[...] = key
    sub_shift = int(math.log2(MOE_SUB))
    sub_sel = ((_iota2((tm, LANES), 0) >> sub_shift) == _iota2((tm, LANES), 1)).astype(BF16)
    cnt_ref[...] = jnp.dot(mem, sub_sel, preferred_element_type=F32).astype(jnp.int32)
    rows = jnp.concatenate([key] + local + [jnp.zeros((LANES - 1 - EXPERTS_PER_GROUP, tm), F32)], axis=0)
    gk_ref[...] = rows.T


def _outproj_route(a, b, a_blk, b_blk, w, x, nw, rw, rb, *, layer, tm, name):
    n, d = x.shape
    ka = kb = w.shape[0] // 2
    return pl.pallas_call(
        _outproj_kernel,
        grid=(n // tm,),
        in_specs=[
            pl.BlockSpec((tm, ka), lambda i: (i, a_blk)),
            pl.BlockSpec((tm, kb), lambda i: (i, b_blk)),
            pl.BlockSpec((ka, d), lambda i: (0, 0)),
            pl.BlockSpec((kb, d), lambda i: (1, 0)),
            pl.BlockSpec((tm, d), lambda i: (i, 0)),
            pl.BlockSpec((1, d), lambda i: (0, 0)),
            pl.BlockSpec((None, d, LANES), lambda i: (layer, 0, 0)),
            pl.BlockSpec((None, 1, LANES), lambda i: (layer, 0, 0)),
        ],
        out_specs=[
            pl.BlockSpec((tm, d), lambda i: (i, 0)),
            pl.BlockSpec((tm, d), lambda i: (i, 0)),
            pl.BlockSpec((tm, LANES), lambda i: (i, 0)),
            pl.BlockSpec((1, tm), lambda i: (0, i)),
            pl.BlockSpec((8, LANES), lambda i: (i, 0)),
        ],
        out_shape=[jax.ShapeDtypeStruct((n, d), F32),
                   jax.ShapeDtypeStruct((n, d), BF16),
                   jax.ShapeDtypeStruct((n, LANES), F32),
                   jax.ShapeDtypeStruct((1, n), F32),
                   jax.ShapeDtypeStruct((8 * (n // tm), LANES), jnp.int32)],
        compiler_params=pltpu.CompilerParams(
            dimension_semantics=("parallel",), vmem_limit_bytes=VMEM_LIMIT),
        name=name,
    )(a, b, w, w, x, nw, rw, rb)


def _round_up(v, m):
    return ((v + (m - 1)) // m) * m


def _round_up_pow2(v, m):
    return (v + (m - 1)) & ~(m - 1)


def _cdiv_pow2(v, m):
    return lax.shift_right_logical(v + (m - 1), int(math.log2(m)))


def _moe_kernel(cnt_ref, h_ref, krow_ref, gk_ref, wgu_ref, wd_ref, x_ref, fnw_ref, o_ref,
                hc_s, gc_s, y_s, *, final_norm):
    i = pl.program_id(0)
    g = pl.program_id(1)
    tm = h_ref.shape[0]
    n_sub = tm // MOE_SUB

    def segments(grp):
        cnts = [cnt_ref[(i * 8 + grp) * LANES + s] for s in range(n_sub)]
        starts = [jnp.int32(0)]
        for s in range(n_sub):
            starts.append(starts[-1] + _round_up_pow2(cnts[s], MOE_ALIGN))
        return cnts, starts

    counts, offs = segments(g)
    total = offs[-1]
    key0 = g.astype(F32) * MOE_KEY_STRIDE
    sub_iota = _iota2((MOE_BLK, MOE_SUB), 0).astype(F32)
    lane_iota = _iota2((MOE_SUB, MOE_BLK), 1).astype(F32)

    def pack(s, w):
        rows = slice(s * MOE_SUB, (s + 1) * MOE_SUB)
        base = key0 + (w * MOE_BLK).astype(F32)
        sel = (krow_ref[:, rows] == sub_iota + base).astype(BF16)
        dst = pl.ds(pl.multiple_of(offs[s] + w * MOE_BLK, MOE_ALIGN), MOE_BLK)
        hc_s[dst, :] = jnp.dot(sel, h_ref[rows, :], preferred_element_type=F32).astype(BF16)
        gc_s[dst, :] = _mm_01(sel, gk_ref[rows, :])

    for s in range(n_sub):
        def pack_more(w, carry, s=s):
            pack(s, w)
            return carry

        lax.fori_loop(1, _cdiv_pow2(counts[s], MOE_BLK), pack_more, 0)
    for s in range(n_sub):
        pack(s, jnp.int32(0))

    tail_rows = MOE_FIRST[-1] + MOE_FIRST_STEP
    tail = pl.ds(pl.multiple_of(total, MOE_ALIGN), tail_rows)
    hc_s[tail, :] = jnp.zeros((tail_rows, hc_s.shape[1]), BF16)
    gc_s[tail, :] = jnp.zeros((tail_rows, LANES), F32)

    def expert_rows(start, rows):
        blk = pl.ds(pl.multiple_of(start, MOE_FIRST_STEP), rows)
        hb = hc_s[blk, :]
        gates = gc_s[blk, :]
        y = jnp.zeros((rows, o_ref.shape[1]), F32)
        for e in range(EXPERTS_PER_GROUP):
            gu = jnp.dot(hb, wgu_ref[0, e], preferred_element_type=F32)
            act = _silu(gu[:, :D_EXPERT]) * gu[:, D_EXPERT:] * gates[:, 1 + e:2 + e]
            y = y + jnp.dot(act.astype(BF16), wd_ref[0, e], preferred_element_type=F32)
        y_s[g, blk, :] = y.astype(BF16)

    first = jnp.clip(_round_up_pow2(total, MOE_FIRST_STEP), MOE_FIRST[0], MOE_FIRST[-1])
    for size in MOE_FIRST:
        @pl.when(first == size)
        def _(size=size):
            expert_rows(0, size)

    n_rest = _cdiv_pow2(jnp.maximum(total - first, 0), MOE_BLK)

    def rest(bi, carry):
        expert_rows(first + bi * MOE_BLK, MOE_BLK)
        return carry

    lax.fori_loop(0, n_rest, rest, 0)
    done = pl.ds(pl.multiple_of(first + n_rest * MOE_BLK, MOE_FIRST_STEP), MOE_BLK)
    y_s[g, done, :] = jnp.zeros((MOE_BLK, y_s.shape[2]), BF16)

    @pl.when(g == N_GROUPS - 1)
    def _():
        segs = [segments(grp) for grp in range(N_GROUPS)]
        for s in range(n_sub):
            rows = slice(s * MOE_SUB, (s + 1) * MOE_SUB)
            key_col = gk_ref[rows, 0:1]
            sel = jnp.concatenate(
                [(key_col == lane_iota + grp * MOE_KEY_STRIDE).astype(BF16) for grp in range(N_GROUPS)], axis=1)
            packed = jnp.concatenate(
                [y_s[grp, pl.ds(pl.multiple_of(segs[grp][1][s], MOE_ALIGN), MOE_BLK), :]
                 for grp in range(N_GROUPS)], axis=0)
            o_ref[rows, :] = x_ref[rows, :] + jnp.dot(sel, packed, preferred_element_type=F32)
            for grp in range(N_GROUPS):
                def more(w, carry, s=s, rows=rows, grp=grp, key_col=key_col):
                    base = grp * MOE_KEY_STRIDE + (w * MOE_BLK).astype(F32)
                    sel_w = (key_col == lane_iota + base).astype(BF16)
                    src = pl.ds(pl.multiple_of(segs[grp][1][s] + w * MOE_BLK, MOE_ALIGN), MOE_BLK)
                    o_ref[rows, :] += jnp.dot(sel_w, y_s[grp, src, :], preferred_element_type=F32)
                    return carry

                lax.fori_loop(1, _cdiv_pow2(segs[grp][0][s], MOE_BLK), more, 0)
        if final_norm:
            y = o_ref[...]
            ms = jnp.mean(y * y, axis=-1, keepdims=True)
            o_ref[...] = y * lax.rsqrt(ms + RMS_EPS) * fnw_ref[...]


def _moe(h, krow, gk, counts, wgu, wd, x, fnw, *, layer, tm, final_norm, name):
    n, d = x.shape
    buf_rows = _round_up(tm + (tm // MOE_SUB) * MOE_ALIGN + MOE_FIRST[-1] + MOE_FIRST_STEP, MOE_BLK)
    grid_spec = pltpu.PrefetchScalarGridSpec(
        num_scalar_prefetch=1,
        grid=(n // tm, N_GROUPS),
        in_specs=[
            pl.BlockSpec((tm, d), lambda i, g, c: (i, 0)),
            pl.BlockSpec((1, tm), lambda i, g, c: (0, i)),
            pl.BlockSpec((tm, LANES), lambda i, g, c: (i, 0)),
            pl.BlockSpec((1, EXPERTS_PER_GROUP, d, 2 * D_EXPERT), lambda i, g, c: (layer, g, 0, 0)),
            pl.BlockSpec((1, EXPERTS_PER_GROUP, D_EXPERT, d), lambda i, g, c: (layer, g, 0, 0)),
            pl.BlockSpec((tm, d), lambda i, g, c: (i, 0)),
            pl.BlockSpec((1, d), lambda i, g, c: (0, 0)),
        ],
        out_specs=pl.BlockSpec((tm, d), lambda i, g, c: (i, 0)),
        scratch_shapes=[pltpu.VMEM((buf_rows, d), BF16),
                        pltpu.VMEM((buf_rows, LANES), F32),
                        pltpu.VMEM((N_GROUPS, buf_rows, d), BF16)],
    )
    return pl.pallas_call(
        functools.partial(_moe_kernel, final_norm=final_norm),
        grid_spec=grid_spec,
        out_shape=jax.ShapeDtypeStruct((n, d), F32),
        compiler_params=pltpu.CompilerParams(
            dimension_semantics=("parallel", "arbitrary"), vmem_limit_bytes=VMEM_LIMIT),
        name=name,
    )(counts, h, krow, gk, wgu, wd, x, fnw)


def _pad_cols(a, width):
    return jnp.pad(a, ((0, 0), (0, width - a.shape[1])))


def _rope_tables(seq):
    inv = (1.0 / (ROPE_BASE ** np.linspace(0.0, 1.0, RET_DK // 2, dtype=np.float32))).astype(np.float32)
    ang = np.arange(seq, dtype=np.float32)[:, None] * inv[None, :]
    return jnp.asarray(np.cos(ang), dtype=F32), jnp.asarray(np.sin(ang), dtype=F32)


def _router_params(wc, bc, wf, bf):
    rw = jnp.concatenate([wc, wf], axis=2)
    rb = jnp.concatenate([bc, bf], axis=1)[:, None, :]
    pad = ((0, 0), (0, 0), (0, LANES - rw.shape[2]))
    return jnp.pad(rw, pad), jnp.pad(rb, pad)


def kernel(x, norm_mix_w, norm_ffn_w, even_w_in, gdn_conv_w, gdn_a_log, gdn_dt_bias, gdn_norm_w, hgrn_lb_logits, hgrn_norm_w, even_w_out, odd_w_in, ret_norm_w, odd_w_out, router_c_w, router_c_b, router_f_w, router_f_b, moe_w_gate_up, moe_w_down, final_norm_w):
    bsz, seq, d = x.shape
    n = bsz * seq
    xt = x.reshape(n, d)
    mix_w = N_HEADS * HEAD_D
    conv_cols = 3 * mix_w
    gdn_main = conv_cols + mix_w

    w_in = even_w_in[0]
    small0 = gdn_main
    w_small = jnp.concatenate(
        [_pad_cols(w_in[:, small0:small0 + N_HEADS], LANES),
         _pad_cols(w_in[:, small0 + N_HEADS:small0 + 2 * N_HEADS], LANES),
         _pad_cols(w_in[:, small0 + 4 * mix_w:], LANES)], axis=1).astype(BF16)
    proj, small, o_b, wgu_b, wd_b, w_out, w_odd, w_out1 = _proj_even(
        xt, norm_mix_w[0][None, :], w_in.astype(BF16), w_small, hgrn_lb_logits, hgrn_norm_w[0][None, :],
        (moe_w_gate_up.reshape(-1, moe_w_gate_up.shape[-1]), moe_w_down.reshape(-1, moe_w_down.shape[-1]),
         even_w_out[0], odd_w_in[0], odd_w_out[0]),
        gdn_w=gdn_main, tm=512, tn=1024, tokens_per_seq=seq, layer=0, name="in_proj_even_hgrn2")
    wgu_b = wgu_b.reshape(moe_w_gate_up.shape)
    wd_b = wd_b.reshape(moe_w_down.shape)
    proj = proj.reshape(bsz, seq, -1)
    small = small.reshape(bsz, seq, -1)
    o_a = _gdn(proj, small, gdn_conv_w[0], _pad_cols(gdn_a_log[0][None, :], LANES),
               _pad_cols(gdn_dt_bias[0][None, :], LANES), gdn_norm_w[0][None, :], tb=512)
    assert ROUTE_TM == MOE_TM
    rw, rb = _router_params(router_c_w, router_c_b, router_f_w, router_f_b)
    x1, h, gk, krow, cnt = _outproj_route(o_a.reshape(n, mix_w), o_b.reshape(n, mix_w), 0, 0,
                                          w_out, xt, norm_ffn_w[0][None, :], rw, rb,
                                          layer=0, tm=ROUTE_TM, name="out_proj_even")
    x2 = _moe(h, krow, gk, cnt.reshape(-1), wgu_b, wd_b, x1,
              final_norm_w[None, :], layer=0, tm=MOE_TM, final_norm=False, name="moe0")

    cos, sin = _rope_tables(seq)
    o_c = _proj_retention(x2, norm_mix_w[1][None, :], w_odd, cos, sin, ret_norm_w[0][None, :],
                          tm=512, tn=1024, tc=RET_CHUNK, tokens_per_seq=seq, name="in_proj_retention")
    x3, h, gk, krow, cnt = _outproj_route(o_c, o_c, 0, 1, w_out1, x2,
                                          norm_ffn_w[1][None, :], rw, rb, layer=1, tm=ROUTE_TM, name="out_proj_odd")
    out = _moe(h, krow, gk, cnt.reshape(-1), wgu_b, wd_b, x3,
               final_norm_w[None, :], layer=1, tm=MOE_TM, final_norm=True, name="moe1")
    return out.reshape(bsz, seq, d)
```

```python
import functools
import math

import jax
import jax.numpy as jnp
import numpy as np
from jax import lax
from jax.experimental import pallas as pl
from jax.experimental.pallas import tpu as pltpu

F32 = jnp.float32
BF16 = jnp.bfloat16

D_MODEL = 1024
RMS_EPS = 1e-6
L2_EPS = 1e-6
CHUNK = 64
CONV_K = 4
N_HEADS = 4
HEAD_D = 128
RET_DK = 256
RET_DV = 512
RET_CHUNK = 256
ROPE_BASE = 10000.0
N_GROUPS = 4
EXPERTS_PER_GROUP = 4
N_EXPERTS = 16
D_EXPERT = 256
LANES = 128
ROUTE_TM = 1024
MOE_TM = 1024
MOE_SUB = 256
MOE_BLK = 128
MOE_ALIGN = 16
MOE_KEY_STRIDE = 4096.0
MOE_FIRST_STEP = 32
MOE_FIRST = (256, 288, 320, 352)
VMEM_LIMIT = 56 * 1024 * 1024


def _mm(a, b):
    return jnp.dot(a.astype(BF16), b.astype(BF16), preferred_element_type=F32)


def _mm_nt(a, b):
    return lax.dot_general(a.astype(BF16), b.astype(BF16), (((1,), (1,)), ((), ())),
                           preferred_element_type=F32)


def _mm_tn(a, b):
    return _mm(a.T, b)


def _mm_01(m01, x):
    hi = x.astype(BF16)
    lo = (x - hi.astype(F32)).astype(BF16)
    return (jnp.dot(m01, hi, preferred_element_type=F32)
            + jnp.dot(m01, lo, preferred_element_type=F32))


def _bmm(a, b):
    return jnp.einsum('cik,ckj->cij', a.astype(BF16), b.astype(BF16), preferred_element_type=F32)


def _bmm_nt(a, b):
    return jnp.einsum('cik,cjk->cij', a.astype(BF16), b.astype(BF16), preferred_element_type=F32)


def _bmm_tn(a, b):
    return _bmm(jnp.swapaxes(a, 1, 2), b)


def _bmm_01(m01, x):
    hi = x.astype(BF16)
    lo = (x - hi.astype(F32)).astype(BF16)
    return (jnp.einsum('cik,ckj->cij', m01, hi, preferred_element_type=F32)
            + jnp.einsum('cik,ckj->cij', m01, lo, preferred_element_type=F32))


def _sigmoid(x):
    return 1.0 / (1.0 + jnp.exp2(x * (-math.log2(math.e))))


def _silu(x):
    return x * _sigmoid(x)


def _softplus(x):
    return jnp.maximum(x, 0.0) + jnp.log(1.0 + jnp.exp(-jnp.abs(x)))


def _iota2(shape, dim):
    return lax.broadcasted_iota(jnp.int32, shape, dim)


def _head_rms_gate(o, gate, nw, width):
    outs = []
    for h in range(o.shape[1] // width):
        oh = o[:, h * width:(h + 1) * width]
        ms = jnp.mean(oh * oh, axis=-1, keepdims=True)
        outs.append(oh * lax.rsqrt(ms + RMS_EPS) * nw)
    return jnp.concatenate(outs, axis=1) * _silu(gate)


def _proj_even_kernel(*refs, tn, n_cast, tiles_per_seq, layer, shift):
    (x_ref, nw_ref, wg_ref, wh_ref, ws_ref, lb_ref, hnw_ref), rest = refs[:7], refs[7:]
    cast_in, rest = rest[:n_cast], rest[n_cast:]
    (o_ref, os_ref, ob_ref), rest = rest[:3], rest[3:]
    cast_out, rest = rest[:n_cast], rest[n_cast:]
    hg_s, whg_s, q_s, k_s, lf_s, i_s, oh_s, st_ref = rest
    for src, dst in zip(cast_in, cast_out):
        dst[...] = src[...].astype(dst.dtype)

    @pl.when(pl.program_id(0) == 0)
    def _():
        sel = (_iota2((2 * LANES, LANES), 0) == _iota2((2 * LANES, LANES), 1) + shift).astype(BF16)
        n_blocks = wh_ref.shape[1] // LANES
        for j in range(n_blocks):
            if j + 1 < n_blocks:
                both = wh_ref[:, j * LANES:(j + 2) * LANES]
            else:
                both = jnp.concatenate([wh_ref[:, j * LANES:], ws_ref[:, 2 * LANES:]], axis=1)
            whg_s[:, j * LANES:(j + 1) * LANES] = jnp.dot(both, sel, preferred_element_type=F32).astype(BF16)

    @pl.when(pl.program_id(0) % tiles_per_seq == 0)
    def _():
        st_ref[...] = jnp.zeros(st_ref.shape, F32)

    x = x_ref[...]
    tm = x.shape[0]
    ms = jnp.mean(x * x, axis=-1, keepdims=True)
    hb = (x * lax.rsqrt(ms + RMS_EPS) * nw_ref[...]).astype(BF16)
    os_ref[...] = jnp.dot(hb, ws_ref[:, :2 * LANES], preferred_element_type=F32)
    for j in range(whg_s.shape[1] // tn):
        cols = slice(j * tn, (j + 1) * tn)
        hg_s[:, cols] = jnp.dot(hb, whg_s[:, cols], preferred_element_type=F32)
    n_gdn = wg_ref.shape[1] // tn
    for j in range(n_gdn // 2):
        cols = slice(j * tn, (j + 1) * tn)
        o_ref[:, cols] = jnp.dot(hb, wg_ref[:, cols], preferred_element_type=F32).astype(o_ref.dtype)
    w = N_HEADS * HEAD_D
    ob_ref[...] = _hgrn_block(hg_s[:, 0:w], hg_s[:, w:2 * w], hg_s[:, 2 * w:3 * w], hg_s[:, 3 * w:4 * w],
                              lb_ref, hnw_ref, q_s, k_s, lf_s, i_s, oh_s, st_ref,
                              tb=tm, layer=layer).astype(ob_ref.dtype)
    for j in range(n_gdn // 2, n_gdn):
        cols = slice(j * tn, (j + 1) * tn)
        o_ref[:, cols] = jnp.dot(hb, wg_ref[:, cols], preferred_element_type=F32).astype(o_ref.dtype)


def _proj_even(x, nw, w, w_small, lb_logits, hgrn_nw, cast, *, gdn_w, tm, tn, tokens_per_seq, layer, name):
    n, d = x.shape
    steps = n // tm
    mix_w = N_HEADS * HEAD_D
    hgrn_w = 4 * mix_w
    shift = w.shape[1] - gdn_w - hgrn_w
    assert gdn_w == hgrn_w and 0 < shift < LANES and w_small.shape[1] == 3 * LANES
    const = lambda i: (0, 0)
    tile = lambda i: (i, 0)
    in_specs = [pl.BlockSpec((tm, d), tile), pl.BlockSpec((1, d), const),
                pl.BlockSpec((d, gdn_w), const), pl.BlockSpec((d, hgrn_w), lambda i: (0, 1)),
                pl.BlockSpec(w_small.shape, const), pl.BlockSpec(lb_logits.shape, const),
                pl.BlockSpec((1, HEAD_D), const)]
    out_specs = [pl.BlockSpec((tm, gdn_w), tile), pl.BlockSpec((tm, 2 * LANES), tile),
                 pl.BlockSpec((tm, mix_w), tile)]
    out_shape = [jax.ShapeDtypeStruct((n, gdn_w), BF16), jax.ShapeDtypeStruct((n, 2 * LANES), F32),
                 jax.ShapeDtypeStruct((n, mix_w), BF16)]
    for a in cast:
        slab = (a.shape[0] // steps, a.shape[1])
        in_specs.append(pl.BlockSpec(slab, tile))
        out_specs.append(pl.BlockSpec(slab, tile))
        out_shape.append(jax.ShapeDtypeStruct(a.shape, BF16))
    scratch = ([pltpu.VMEM((tm, hgrn_w), F32), pltpu.VMEM((d, hgrn_w), BF16)] + [pltpu.VMEM((tm, mix_w), F32)] * 5
               + [pltpu.VMEM((N_HEADS, HEAD_D, HEAD_D), F32)])
    return pl.pallas_call(
        functools.partial(_proj_even_kernel, tn=tn, n_cast=len(cast), tiles_per_seq=tokens_per_seq // tm,
                          layer=layer, shift=shift),
        grid=(steps,),
        in_specs=in_specs,
        out_specs=out_specs,
        out_shape=out_shape,
        scratch_shapes=scratch,
        compiler_params=pltpu.CompilerParams(
            dimension_semantics=("arbitrary",), vmem_limit_bytes=VMEM_LIMIT),
        name=name,
    )(x, nw, w, w, w_small, lb_logits, hgrn_nw, *cast)


def _gdn_kernel(qkv_ref, z_ref, sm_ref, cw_ref, alog_ref, dtb_ref, nw_ref, o_ref,
                xs_ref, qkv_s, o_s, m_s, sq_s, qe_s, dec_s, st_ref, *, tb):
    @pl.when(pl.program_id(1) == 0)
    def _():
        xs_ref[0:8, :] = jnp.zeros((8, xs_ref.shape[1]), F32)
        st_ref[...] = jnp.zeros(st_ref.shape, F32)

    c = CHUNK
    nc = tb // c
    nb = N_HEADS * nc
    qk_w = N_HEADS * HEAD_D
    row = _iota2((c, c), 0)
    col = _iota2((c, c), 1)
    causal = (row >= col)[None]
    strict = (row > col)[None]
    same16 = ((row >> 4) == (col >> 4))[None]
    same32 = ((row >> 5) == (col >> 5))[None]
    eye = (row == col).astype(F32)[None]
    ltri = jnp.broadcast_to((row >= col).astype(BF16)[None], (nc, c, c))
    lane = _iota2((nb, c, LANES), 2)

    def per_head(fn):
        return jnp.concatenate([fn(h) for h in range(N_HEADS)], axis=0)

    for h in range(N_HEADS):
        for base, scale in ((0, HEAD_D ** -0.5), (qk_w, 1.0), (2 * qk_w, None)):
            cols = slice(base + h * HEAD_D, base + (h + 1) * HEAD_D)
            x = qkv_ref[0, :, cols].astype(F32)
            xs_ref[8:8 + tb, cols] = x
            cw = cw_ref[:, cols]
            y = x * cw[CONV_K - 1:CONV_K, :]
            for j in range(1, CONV_K):
                y = y + xs_ref[pl.ds(8 - j, tb), cols] * cw[CONV_K - 1 - j:CONV_K - j, :]
            y = _silu(y)
            if scale is not None:
                y = y * (lax.rsqrt(jnp.sum(y * y, axis=-1, keepdims=True) + L2_EPS) * scale)
            qkv_s[:, cols] = y
    xs_ref[0:8, :] = xs_ref[tb:tb + 8, :]

    sm = sm_ref[0]
    beta_all = _sigmoid(sm[:, :LANES]).reshape(nc, c, LANES)
    rate = -jnp.exp(alog_ref[...]) * math.log2(math.e)
    g_all = (rate * _softplus(sm[:, LANES:] + dtb_ref[...])).reshape(nc, c, LANES)
    gc_all = _bmm_01(ltri, g_all)
    q = per_head(lambda h: qkv_s[:, h * HEAD_D:(h + 1) * HEAD_D].reshape(nc, c, HEAD_D))
    k = per_head(lambda h: qkv_s[:, qk_w + h * HEAD_D:qk_w + (h + 1) * HEAD_D].reshape(nc, c, HEAD_D))
    v = per_head(lambda h: qkv_s[:, 2 * qk_w + h * HEAD_D:2 * qk_w + (h + 1) * HEAD_D].reshape(nc, c, HEAD_D))
    beta = per_head(lambda h: beta_all[:, :, h:h + 1])
    gcol = per_head(lambda h: gc_all[:, :, h:h + 1])
    g_hi = gcol.astype(BF16).astype(F32)
    g_mid = (gcol - g_hi).astype(BF16).astype(F32)
    g_lo = gcol - g_hi - g_mid
    pieces = jnp.where(lane == 0, g_hi, jnp.where(lane == 1, g_mid, jnp.where(lane == 2, g_lo, 0.0)))
    lane1 = _iota2((1, 1, LANES), 2)
    lhs = pieces + jnp.where(jnp.logical_and(lane1 >= 3, lane1 < 6), 1.0, 0.0)
    rhs_g = jnp.where(lane1 < 3, 1.0, 0.0) - pltpu.roll(pieces, 3, axis=2)
    decay = jnp.exp2(jnp.where(causal, _bmm_nt(lhs, rhs_g), -jnp.inf))
    kb = k * beta
    a = jnp.where(strict, _bmm_nt(kb, k) * decay, 0.0)
    d = jnp.where(same16, a, 0.0)
    x_inv = eye - d
    dp = d
    for _ in range(3):
        dp = _bmm(dp, dp)
        x_inv = x_inv + _bmm(x_inv, dp)
    e = jnp.where(jnp.logical_and(same32, jnp.logical_not(same16)), a, 0.0)
    x_inv = x_inv - _bmm(x_inv, _bmm(e, x_inv))
    f = jnp.where(same32, 0.0, a)
    x_inv = x_inv - _bmm(x_inv, _bmm(f, x_inv))
    wu = _bmm(x_inv, jnp.concatenate([kb * jnp.exp2(gcol), v * beta], axis=2))
    attn = _bmm_nt(q, k) * decay
    g_last = gcol[:, c - 1:c, :]
    kd = k * jnp.exp2(g_last - gcol)
    mq = _bmm_tn(kd, wu)
    aw = _bmm(attn, wu)
    m_s[...] = mq[:, :, :HEAD_D].astype(BF16).reshape(N_HEADS, nc, HEAD_D, HEAD_D)
    sq_s[...] = mq[:, :, HEAD_D:].reshape(N_HEADS, nc, HEAD_D, HEAD_D)
    qe_s[...] = (q * jnp.exp2(gcol) - aw[:, :, :HEAD_D]).astype(BF16).reshape(N_HEADS, nc, c, HEAD_D)
    dec_s[...] = jnp.broadcast_to(jnp.exp2(g_last), (nb, 1, HEAD_D)).reshape(N_HEADS, nc, 1, HEAD_D)
    for h in range(N_HEADS):
        o_s[:, h * HEAD_D:(h + 1) * HEAD_D] = aw[h * nc:(h + 1) * nc, :, HEAD_D:].reshape(tb, HEAD_D)

    for ci in range(nc):
        for h in range(N_HEADS):
            hs = slice(h * HEAD_D, (h + 1) * HEAD_D)
            s_h = st_ref[h]
            s_b = s_h.astype(BF16)
            o_s[ci * c:(ci + 1) * c, hs] += jnp.dot(qe_s[h, ci], s_b, preferred_element_type=F32)
            st_ref[h] = (s_h * dec_s[h, ci] - jnp.dot(m_s[h, ci], s_b, preferred_element_type=F32)
                         + sq_s[h, ci])

    o_ref[0] = _head_rms_gate(o_s[...], z_ref[0].astype(F32), nw_ref[...], HEAD_D).astype(o_ref.dtype)


def _gdn(proj, small, conv_w, a_log, dt_bias, norm_w, *, tb):
    b, t, _ = proj.shape
    conv_cols = 3 * N_HEADS * HEAD_D
    mix_w = N_HEADS * HEAD_D
    return pl.pallas_call(
        functools.partial(_gdn_kernel, tb=tb),
        grid=(b, t // tb),
        in_specs=[
            pl.BlockSpec((1, tb, conv_cols), lambda i, j: (i, j, 0)),
            pl.BlockSpec((1, tb, mix_w), lambda i, j: (i, j, conv_cols // mix_w)),
            pl.BlockSpec((1, tb, 2 * LANES), lambda i, j: (i, j, 0)),
            pl.BlockSpec((CONV_K, conv_cols), lambda i, j: (0, 0)),
            pl.BlockSpec((1, LANES), lambda i, j: (0, 0)),
            pl.BlockSpec((1, LANES), lambda i, j: (0, 0)),
            pl.BlockSpec((1, HEAD_D), lambda i, j: (0, 0)),
        ],
        out_specs=pl.BlockSpec((1, tb, mix_w), lambda i, j: (i, j, 0)),
        out_shape=jax.ShapeDtypeStruct((b, t, mix_w), BF16),
        scratch_shapes=[
            pltpu.VMEM((tb + 8, conv_cols), F32),
            pltpu.VMEM((tb, conv_cols), F32),
            pltpu.VMEM((tb, mix_w), F32),
            pltpu.VMEM((N_HEADS, tb // CHUNK, HEAD_D, HEAD_D), BF16),
            pltpu.VMEM((N_HEADS, tb // CHUNK, HEAD_D, HEAD_D), F32),
            pltpu.VMEM((N_HEADS, tb // CHUNK, CHUNK, HEAD_D), BF16),
            pltpu.VMEM((N_HEADS, tb // CHUNK, 1, HEAD_D), F32),
            pltpu.VMEM((N_HEADS, HEAD_D, HEAD_D), F32),
        ],
        compiler_params=pltpu.CompilerParams(
            dimension_semantics=("parallel", "arbitrary"), vmem_limit_bytes=VMEM_LIMIT),
        name="gdn",
    )(proj, proj, small, conv_w, a_log, dt_bias, norm_w)


def _hgrn_block(q_in, f_in, i_in, gate_in, lb_ref, nw_ref, q_s, k_s, lf_s, i_s, o_s, st_ref, *, tb, layer):
    lbl = lb_ref[...]
    e_lb = jnp.exp(lbl - jnp.max(lbl, axis=0, keepdims=True))
    lb = jnp.sum(e_lb[:layer + 1], axis=0, keepdims=True) / jnp.sum(e_lb, axis=0, keepdims=True)

    f = lb + (1.0 - lb) * _sigmoid(f_in)
    k_s[...] = 1.0 - f
    lf_s[...] = jnp.log2(f)
    q_s[...] = _silu(q_in) * (HEAD_D ** -0.5)
    i_s[...] = i_in

    c = CHUNK
    nc = tb // c
    blk = 8
    row = _iota2((c, c), 0)
    col = _iota2((c, c), 1)
    ltri = jnp.broadcast_to((row >= col).astype(BF16)[None], (nc, c, c))
    level_masks = {}
    for m in (32, 16, 8):
        sh = int(math.log2(2 * m))
        level_masks[m] = jnp.logical_and(
            (row >> sh) == (col >> sh),
            jnp.logical_and((row & (2 * m - 1)) >= m, (col & (2 * m - 1)) < m))[None]
    nb = N_HEADS * nc
    sub = _iota2((nb * c // blk, blk, HEAD_D), 1)

    def per_head(fn):
        return jnp.concatenate([fn(slice(h * HEAD_D, (h + 1) * HEAD_D)) for h in range(N_HEADS)], axis=0)

    b_all = _bmm_01(ltri, lf_s[...].reshape(nc, c, N_HEADS * HEAD_D))
    b = per_head(lambda sl: b_all[:, :, sl])
    q = per_head(lambda sl: q_s[:, sl].reshape(nc, c, HEAD_D))
    k = per_head(lambda sl: k_s[:, sl].reshape(nc, c, HEAD_D))
    iv = per_head(lambda sl: i_s[:, sl].reshape(nc, c, HEAD_D))
    attn = jnp.zeros((nb, c, c), F32)
    for m in (32, 16, 8):
        b_m = b.reshape(nb * c // (2 * m), 2 * m, HEAD_D)
        ref = jnp.broadcast_to(b_m[:, m:m + 1, :], b_m.shape).reshape(nb, c, HEAD_D)
        e = jnp.exp2(-jnp.abs(b - ref))
        attn = attn + jnp.where(level_masks[m], _bmm_nt(q * e, k * e), 0.0)
    o = _bmm(attn, iv)
    qb, kb, bb, ib = (a.reshape(nb * c // blk, blk, HEAD_D) for a in (q, k, b, iv))
    ob = jnp.zeros(qb.shape, F32)
    for s in range(blk):
        dec = jnp.exp2(jnp.where(sub >= s, bb - bb[:, s:s + 1, :], -jnp.inf))
        a_col = jnp.sum(dec * qb * kb[:, s:s + 1, :], axis=-1, keepdims=True)
        ob = ob + a_col * ib[:, s:s + 1, :]
    o = o + ob.reshape(nb, c, HEAD_D)
    b_last = b[:, c - 1:c, :]
    q_in = (q * jnp.exp2(b)).astype(BF16)
    upd = _bmm_tn(iv, k * jnp.exp2(b_last - b))
    keep = jnp.exp2(b_last)
    for h in range(N_HEADS):
        s_t = st_ref[h]
        outs = []
        for ci in range(h * nc, (h + 1) * nc):
            outs.append(o[ci] + _mm_nt(q_in[ci], s_t))
            s_t = s_t * keep[ci] + upd[ci]
        st_ref[h] = s_t
        o_s[:, h * HEAD_D:(h + 1) * HEAD_D] = jnp.concatenate(outs, axis=0)

    return _head_rms_gate(o_s[...], gate_in, nw_ref[...], HEAD_D)


def _proj_ret_kernel(x_ref, nw_ref, w_ref, cos_ref, sin_ref, rnw_ref, o_ref,
                     wq_s, proj_s, dmat_s, st_ref, *, tn, tc, tiles_per_seq):
    i = pl.program_id(0)
    qk_w = N_HEADS * RET_DK
    v_w = N_HEADS * RET_DV
    half = RET_DK // 2
    log_gammas = [math.log(1.0 - 2.0 ** (-5.0 - h)) for h in range(N_HEADS)]

    @pl.when(i == 0)
    def _():
        src = _iota2((RET_DK, RET_DK), 0)
        dst = _iota2((RET_DK, RET_DK), 1)
        split = (src == jnp.where(dst < half, 2 * dst, 2 * (dst - half) + 1)).astype(BF16)
        for hd in range(2 * N_HEADS):
            cols = slice(hd * RET_DK, (hd + 1) * RET_DK)
            wq_s[:, cols] = jnp.dot(w_ref[:, cols], split, preferred_element_type=F32).astype(BF16)
        rel = (_iota2((tc, tc), 0) - _iota2((tc, tc), 1)).astype(F32)
        for h in range(N_HEADS):
            dmat_s[h] = jnp.where(rel >= 0, jnp.exp(jnp.maximum(rel, 0.0) * log_gammas[h]), 0.0)

    @pl.when(i % tiles_per_seq == 0)
    def _():
        st_ref[...] = jnp.zeros(st_ref.shape, F32)

    x = x_ref[...]
    tm = x.shape[0]
    ms = jnp.mean(x * x, axis=-1, keepdims=True)
    hb = (x * lax.rsqrt(ms + RMS_EPS) * nw_ref[...]).astype(BF16)
    gate0 = 2 * qk_w + v_w
    for j in range(gate0 // tn):
        cols = slice(j * tn, (j + 1) * tn)
        w_tile = wq_s[:, cols] if (j + 1) * tn <= 2 * qk_w else w_ref[:, cols]
        proj_s[:, cols] = jnp.dot(hb, w_tile, preferred_element_type=F32).astype(BF16)

    pos = _iota2((tc, 1), 0).astype(F32)
    for ck in range(tm // tc):
        rows = slice(ck * tc, (ck + 1) * tc)
        cos = cos_ref[rows, :]
        sin = sin_ref[rows, :]

        def rot(xh):
            x1, x2 = xh[:, :half], xh[:, half:]
            return jnp.concatenate([x1 * cos - x2 * sin, x1 * sin + x2 * cos], axis=1)

        for h in range(N_HEADS):
            lg = log_gammas[h]
            gcols = slice(gate0 + h * RET_DV, gate0 + (h + 1) * RET_DV)
            proj_s[rows, gcols] = jnp.dot(hb[rows], w_ref[:, gcols], preferred_element_type=F32).astype(BF16)
            q = rot(proj_s[rows, h * RET_DK:(h + 1) * RET_DK].astype(F32))
            k = rot(proj_s[rows, qk_w + h * RET_DK:qk_w + (h + 1) * RET_DK].astype(F32)) * (RET_DK ** -0.5)
            v = proj_s[rows, 2 * qk_w + h * RET_DV:2 * qk_w + (h + 1) * RET_DV]
            gate = proj_s[rows, 2 * qk_w + v_w + h * RET_DV:2 * qk_w + v_w + (h + 1) * RET_DV]
            s = st_ref[h]
            attn = _mm_nt(q, k) * dmat_s[h]
            o = _mm(q * jnp.exp((pos + 1.0) * lg), s) + _mm(attn, v)
            st_ref[h] = s * math.exp(tc * lg) + _mm_tn(k * jnp.exp((tc - 1.0 - pos) * lg), v)
            ms_o = jnp.mean(o * o, axis=-1, keepdims=True)
            o_ref[rows, h * RET_DV:(h + 1) * RET_DV] = (o * lax.rsqrt(ms_o + RMS_EPS) * rnw_ref[...]
                                                        * _silu(gate.astype(F32))).astype(o_ref.dtype)


def _proj_retention(x, nw, w, cos, sin, ret_nw, *, tm, tn, tc, tokens_per_seq, name):
    n, d = x.shape
    tiles_per_seq = tokens_per_seq // tm
    v_w = N_HEADS * RET_DV
    return pl.pallas_call(
        functools.partial(_proj_ret_kernel, tn=tn, tc=tc, tiles_per_seq=tiles_per_seq),
        grid=(n // tm,),
        in_specs=[
            pl.BlockSpec((tm, d), lambda i: (i, 0)),
            pl.BlockSpec((1, d), lambda i: (0, 0)),
            pl.BlockSpec(w.shape, lambda i: (0, 0)),
            pl.BlockSpec((tm, RET_DK // 2), lambda i: (i % tiles_per_seq, 0)),
            pl.BlockSpec((tm, RET_DK // 2), lambda i: (i % tiles_per_seq, 0)),
            pl.BlockSpec((1, RET_DV), lambda i: (0, 0)),
        ],
        out_specs=pl.BlockSpec((tm, v_w), lambda i: (i, 0)),
        out_shape=jax.ShapeDtypeStruct((n, v_w), BF16),
        scratch_shapes=[
            pltpu.VMEM((d, 2 * N_HEADS * RET_DK), BF16),
            pltpu.VMEM((tm, w.shape[1]), BF16),
            pltpu.VMEM((N_HEADS, tc, tc), F32),
            pltpu.VMEM((N_HEADS, RET_DK, RET_DV), F32),
        ],
        compiler_params=pltpu.CompilerParams(
            dimension_semantics=("arbitrary",), vmem_limit_bytes=VMEM_LIMIT),
        name=name,
    )(x, nw, w, cos, sin, ret_nw)


def _route_gates(logits_t):
    cl = [logits_t[g:g + 1, :] for g in range(N_GROUPS)]
    cmax = functools.reduce(jnp.maximum, cl)
    denom = sum(jnp.exp(x - cmax) for x in cl)
    g_prob = 1.0 / denom
    g_idx = jnp.full(cmax.shape, N_GROUPS - 1, jnp.int32)
    for g in range(N_GROUPS - 2, -1, -1):
        g_idx = jnp.where(cl[g] == cmax, g, g_idx)
    def fine_row(g, j):
        r = N_GROUPS + g * EXPERTS_PER_GROUP + j
        return logits_t[r:r + 1, :]

    fl = []
    for j in range(EXPERTS_PER_GROUP):
        x = fine_row(N_GROUPS - 1, j)
        for g in range(N_GROUPS - 2, -1, -1):
            x = jnp.where(g_idx == g, fine_row(g, j), x)
        fl.append(x)
    m1 = functools.reduce(jnp.maximum, fl)
    i1 = jnp.full(m1.shape, EXPERTS_PER_GROUP - 1, jnp.int32)
    for j in range(EXPERTS_PER_GROUP - 2, -1, -1):
        i1 = jnp.where(fl[j] == m1, j, i1)
    rest = [jnp.where(i1 == j, -jnp.inf, fl[j]) for j in range(EXPERTS_PER_GROUP)]
    m2 = functools.reduce(jnp.maximum, rest)
    i2 = jnp.full(m2.shape, EXPERTS_PER_GROUP - 1, jnp.int32)
    for j in range(EXPERTS_PER_GROUP - 2, -1, -1):
        i2 = jnp.where(jnp.logical_and(rest[j] == m2, i1 != j), j, i2)
    e2 = jnp.exp(m2 - m1)
    w1 = g_prob / (1.0 + e2)
    w2 = g_prob * e2 / (1.0 + e2)
    local = [jnp.where(i1 == j, w1, 0.0) + jnp.where(i2 == j, w2, 0.0) for j in range(EXPERTS_PER_GROUP)]
    return g_idx, local


def _outproj_kernel(a_ref, b_ref, wa_ref, wb_ref, x_ref, nw_ref, rw_ref, rb_ref,
                    x1_ref, h_ref, gk_ref, krow_ref, cnt_ref):
    x1 = (x_ref[...]
          + jnp.dot(a_ref[...], wa_ref[...], preferred_element_type=F32)
          + jnp.dot(b_ref[...], wb_ref[...], preferred_element_type=F32))
    x1_ref[...] = x1
    ms = jnp.mean(x1 * x1, axis=-1, keepdims=True)
    h = x1 * lax.rsqrt(ms + RMS_EPS) * nw_ref[...]
    h_ref[...] = h.astype(BF16)
    rw = rw_ref[...]
    hh = h.astype(BF16)
    hl = (h - hh.astype(F32)).astype(BF16)
    wh = rw.astype(BF16)
    wl = (rw - wh.astype(F32)).astype(BF16)
    hi_part = jnp.dot(hh, jnp.concatenate([wh, wl], axis=1), preferred_element_type=F32)
    logits = (hi_part[:, :LANES] + hi_part[:, LANES:]
              + jnp.dot(hl, wh, preferred_element_type=F32)) + rb_ref[...]
    g_idx, local = _route_gates(logits.T)
    tm = g_idx.shape[1]
    mem = (_iota2((8, tm), 0) == g_idx).astype(BF16)
    before = (_iota2((MOE_SUB, MOE_SUB), 0) < _iota2((MOE_SUB, MOE_SUB), 1)).astype(BF16)
    earlier = jnp.concatenate(
        [jnp.dot(mem[:, s:s + MOE_SUB], before, preferred_element_type=F32) for s in range(0, tm, MOE_SUB)],
        axis=1)
    rank = jnp.sum(mem.astype(F32) * earlier, axis=0, keepdims=True)
    key = g_idx.astype(F32) * MOE_KEY_STRIDE + rank
    krow_ref[...] = key
    sub_shift = int(math.log2(MOE_SUB))
    sub_sel = ((_iota2((tm, LANES), 0) >> sub_shift) == _iota2((tm, LANES), 1)).astype(BF16)
    cnt_ref[...] = jnp.dot(mem, sub_sel, preferred_element_type=F32).astype(jnp.int32)
    rows = jnp.concatenate([key] + local + [jnp.zeros((LANES - 1 - EXPERTS_PER_GROUP, tm), F32)], axis=0)
    gk_ref[...] = rows.T


def _outproj_route(a, b, a_blk, b_blk, w, x, nw, rw, rb, *, layer, tm, name):
    n, d = x.shape
    ka = kb = w.shape[0] // 2
    return pl.pallas_call(
        _outproj_kernel,
        grid=(n // tm,),
        in_specs=[
            pl.BlockSpec((tm, ka), lambda i: (i, a_blk)),
            pl.BlockSpec((tm, kb), lambda i: (i, b_blk)),
            pl.BlockSpec((ka, d), lambda i: (0, 0)),
            pl.BlockSpec((kb, d), lambda i: (1, 0)),
            pl.BlockSpec((tm, d), lambda i: (i, 0)),
            pl.BlockSpec((1, d), lambda i: (0, 0)),
            pl.BlockSpec((None, d, LANES), lambda i: (layer, 0, 0)),
            pl.BlockSpec((None, 1, LANES), lambda i: (layer, 0, 0)),
        ],
        out_specs=[
            pl.BlockSpec((tm, d), lambda i: (i, 0)),
            pl.BlockSpec((tm, d), lambda i: (i, 0)),
            pl.BlockSpec((tm, LANES), lambda i: (i, 0)),
            pl.BlockSpec((1, tm), lambda i: (0, i)),
            pl.BlockSpec((8, LANES), lambda i: (i, 0)),
        ],
        out_shape=[jax.ShapeDtypeStruct((n, d), F32),
                   jax.ShapeDtypeStruct((n, d), BF16),
                   jax.ShapeDtypeStruct((n, LANES), F32),
                   jax.ShapeDtypeStruct((1, n), F32),
                   jax.ShapeDtypeStruct((8 * (n // tm), LANES), jnp.int32)],
        compiler_params=pltpu.CompilerParams(
            dimension_semantics=("parallel",), vmem_limit_bytes=VMEM_LIMIT),
        name=name,
    )(a, b, w, w, x, nw, rw, rb)


def _round_up(v, m):
    return ((v + (m - 1)) // m) * m


def _round_up_pow2(v, m):
    return (v + (m - 1)) & ~(m - 1)


def _cdiv_pow2(v, m):
    return lax.shift_right_logical(v + (m - 1), int(math.log2(m)))


def _moe_kernel(cnt_ref, h_ref, krow_ref, gk_ref, wgu_hbm, wd_hbm, x_ref, fnw_ref, o_ref,
                hc_s, gc_s, y_s, wgu_buf, wd_buf, w_sem, *, final_norm, layer):
    i = pl.program_id(0)
    g = pl.program_id(1)
    tm = h_ref.shape[0]
    n_sub = tm // MOE_SUB

    n_slots = wgu_buf.shape[0]
    step = i * N_GROUPS + g
    n_steps = pl.num_programs(0) * N_GROUPS

    def weight_copies(s):
        experts = pl.ds((s % N_GROUPS) * EXPERTS_PER_GROUP, EXPERTS_PER_GROUP)
        slot = s % n_slots
        return (pltpu.make_async_copy(wgu_hbm.at[layer, experts], wgu_buf.at[slot], w_sem.at[0, slot]),
                pltpu.make_async_copy(wd_hbm.at[layer, experts], wd_buf.at[slot], w_sem.at[1, slot]))

    @pl.when(step == 0)
    def _():
        for s0 in range(n_slots - 1):
            for cp in weight_copies(s0):
                cp.start()

    @pl.when(step + (n_slots - 1) < n_steps)
    def _():
        for cp in weight_copies(step + (n_slots - 1)):
            cp.start()

    for cp in weight_copies(step):
        cp.wait()
    w_slot = step % n_slots

    def segments(grp):
        cnts = [cnt_ref[(i * 8 + grp) * LANES + s] for s in range(n_sub)]
        starts = [jnp.int32(0)]
        for s in range(n_sub):
            starts.append(starts[-1] + _round_up_pow2(cnts[s], MOE_ALIGN))
        return cnts, starts

    counts, offs = segments(g)
    total = offs[-1]
    key0 = g.astype(F32) * MOE_KEY_STRIDE
    sub_iota = _iota2((MOE_BLK, MOE_SUB), 0).astype(F32)
    lane_iota = _iota2((MOE_SUB, MOE_BLK), 1).astype(F32)

    def pack(s, w):
        rows = slice(s * MOE_SUB, (s + 1) * MOE_SUB)
        base = key0 + (w * MOE_BLK).astype(F32)
        sel = (krow_ref[:, rows] == sub_iota + base).astype(BF16)
        dst = pl.ds(pl.multiple_of(offs[s] + w * MOE_BLK, MOE_ALIGN), MOE_BLK)
        hc_s[dst, :] = jnp.dot(sel, h_ref[rows, :], preferred_element_type=F32).astype(BF16)
        gc_s[dst, :] = _mm_01(sel, gk_ref[rows, :])

    for s in range(n_sub):
        def pack_more(w, carry, s=s):
            pack(s, w)
            return carry

        lax.fori_loop(1, _cdiv_pow2(counts[s], MOE_BLK), pack_more, 0)
    for s in range(n_sub):
        pack(s, jnp.int32(0))

    tail_rows = MOE_FIRST[-1] + MOE_FIRST_STEP
    tail = pl.ds(pl.multiple_of(total, MOE_ALIGN), tail_rows)
    hc_s[tail, :] = jnp.zeros((tail_rows, hc_s.shape[1]), BF16)
    gc_s[tail, :] = jnp.zeros((tail_rows, LANES), F32)

    def expert_rows(start, rows):
        blk = pl.ds(pl.multiple_of(start, MOE_FIRST_STEP), rows)
        hb = hc_s[blk, :]
        gates = gc_s[blk, :]
        y = jnp.zeros((rows, o_ref.shape[1]), F32)
        for e in range(EXPERTS_PER_GROUP):
            gu = jnp.dot(hb, wgu_buf[w_slot, e], preferred_element_type=F32)
            act = _silu(gu[:, :D_EXPERT]) * gu[:, D_EXPERT:] * gates[:, 1 + e:2 + e]
            y = y + jnp.dot(act.astype(BF16), wd_buf[w_slot, e], preferred_element_type=F32)
        y_s[g, blk, :] = y.astype(BF16)

    first = jnp.clip(_round_up_pow2(total, MOE_FIRST_STEP), MOE_FIRST[0], MOE_FIRST[-1])
    for size in MOE_FIRST:
        @pl.when(first == size)
        def _(size=size):
            expert_rows(0, size)

    n_rest = _cdiv_pow2(jnp.maximum(total - first, 0), MOE_BLK)

    def rest(bi, carry):
        expert_rows(first + bi * MOE_BLK, MOE_BLK)
        return carry

    lax.fori_loop(0, n_rest, rest, 0)
    done = pl.ds(pl.multiple_of(first + n_rest * MOE_BLK, MOE_FIRST_STEP), MOE_BLK)
    y_s[g, done, :] = jnp.zeros((MOE_BLK, y_s.shape[2]), BF16)

    @pl.when(g == N_GROUPS - 1)
    def _():
        segs = [segments(grp) for grp in range(N_GROUPS)]
        for s in range(n_sub):
            rows = slice(s * MOE_SUB, (s + 1) * MOE_SUB)
            key_col = gk_ref[rows, 0:1]
            sel = jnp.concatenate(
                [(key_col == lane_iota + grp * MOE_KEY_STRIDE).astype(BF16) for grp in range(N_GROUPS)], axis=1)
            packed = jnp.concatenate(
                [y_s[grp, pl.ds(pl.multiple_of(segs[grp][1][s], MOE_ALIGN), MOE_BLK), :]
                 for grp in range(N_GROUPS)], axis=0)
            o_ref[rows, :] = x_ref[rows, :] + jnp.dot(sel, packed, preferred_element_type=F32)
            for grp in range(N_GROUPS):
                def more(w, carry, s=s, rows=rows, grp=grp, key_col=key_col):
                    base = grp * MOE_KEY_STRIDE + (w * MOE_BLK).astype(F32)
                    sel_w = (key_col == lane_iota + base).astype(BF16)
                    src = pl.ds(pl.multiple_of(segs[grp][1][s] + w * MOE_BLK, MOE_ALIGN), MOE_BLK)
                    o_ref[rows, :] += jnp.dot(sel_w, y_s[grp, src, :], preferred_element_type=F32)
                    return carry

                lax.fori_loop(1, _cdiv_pow2(segs[grp][0][s], MOE_BLK), more, 0)
        if final_norm:
            y = o_ref[...]
            ms = jnp.mean(y * y, axis=-1, keepdims=True)
            o_ref[...] = y * lax.rsqrt(ms + RMS_EPS) * fnw_ref[...]


def _moe(h, krow, gk, counts, wgu, wd, x, fnw, *, layer, tm, final_norm, name):
    n, d = x.shape
    buf_rows = _round_up(tm + (tm // MOE_SUB) * MOE_ALIGN + MOE_FIRST[-1] + MOE_FIRST_STEP, MOE_BLK)
    grid_spec = pltpu.PrefetchScalarGridSpec(
        num_scalar_prefetch=1,
        grid=(n // tm, N_GROUPS),
        in_specs=[
            pl.BlockSpec((tm, d), lambda i, g, c: (i, 0)),
            pl.BlockSpec((1, tm), lambda i, g, c: (0, i)),
            pl.BlockSpec((tm, LANES), lambda i, g, c: (i, 0)),
            pl.BlockSpec(memory_space=pl.ANY),
            pl.BlockSpec(memory_space=pl.ANY),
            pl.BlockSpec((tm, d), lambda i, g, c: (i, 0)),
            pl.BlockSpec((1, d), lambda i, g, c: (0, 0)),
        ],
        out_specs=pl.BlockSpec((tm, d), lambda i, g, c: (i, 0)),
        scratch_shapes=[pltpu.VMEM((buf_rows, d), BF16),
                        pltpu.VMEM((buf_rows, LANES), F32),
                        pltpu.VMEM((N_GROUPS, buf_rows, d), BF16),
                        pltpu.VMEM((3, EXPERTS_PER_GROUP, d, 2 * D_EXPERT), BF16),
                        pltpu.VMEM((3, EXPERTS_PER_GROUP, D_EXPERT, d), BF16),
                        pltpu.SemaphoreType.DMA((2, 3))],
    )
    return pl.pallas_call(
        functools.partial(_moe_kernel, final_norm=final_norm, layer=layer),
        grid_spec=grid_spec,
        out_shape=jax.ShapeDtypeStruct((n, d), F32),
        compiler_params=pltpu.CompilerParams(
            dimension_semantics=("arbitrary", "arbitrary"), vmem_limit_bytes=VMEM_LIMIT),
        name=name,
    )(counts, h, krow, gk, wgu, wd, x, fnw)


def _pad_cols(a, width):
    return jnp.pad(a, ((0, 0), (0, width - a.shape[1])))


def _rope_tables(seq):
    inv = (1.0 / (ROPE_BASE ** np.linspace(0.0, 1.0, RET_DK // 2, dtype=np.float32))).astype(np.float32)
    ang = np.arange(seq, dtype=np.float32)[:, None] * inv[None, :]
    return jnp.asarray(np.cos(ang), dtype=F32), jnp.asarray(np.sin(ang), dtype=F32)


def _router_params(wc, bc, wf, bf):
    rw = jnp.concatenate([wc, wf], axis=2)
    rb = jnp.concatenate([bc, bf], axis=1)[:, None, :]
    pad = ((0, 0), (0, 0), (0, LANES - rw.shape[2]))
    return jnp.pad(rw, pad), jnp.pad(rb, pad)


def kernel(x, norm_mix_w, norm_ffn_w, even_w_in, gdn_conv_w, gdn_a_log, gdn_dt_bias, gdn_norm_w, hgrn_lb_logits, hgrn_norm_w, even_w_out, odd_w_in, ret_norm_w, odd_w_out, router_c_w, router_c_b, router_f_w, router_f_b, moe_w_gate_up, moe_w_down, final_norm_w):
    bsz, seq, d = x.shape
    n = bsz * seq
    xt = x.reshape(n, d)
    mix_w = N_HEADS * HEAD_D
    conv_cols = 3 * mix_w
    gdn_main = conv_cols + mix_w

    w_in = even_w_in[0]
    small0 = gdn_main
    w_small = jnp.concatenate(
        [_pad_cols(w_in[:, small0:small0 + N_HEADS], LANES),
         _pad_cols(w_in[:, small0 + N_HEADS:small0 + 2 * N_HEADS], LANES),
         _pad_cols(w_in[:, small0 + 4 * mix_w:], LANES)], axis=1).astype(BF16)
    proj, small, o_b, wgu_b, wd_b, w_out, w_odd, w_out1 = _proj_even(
        xt, norm_mix_w[0][None, :], w_in.astype(BF16), w_small, hgrn_lb_logits, hgrn_norm_w[0][None, :],
        (moe_w_gate_up.reshape(-1, moe_w_gate_up.shape[-1]), moe_w_down.reshape(-1, moe_w_down.shape[-1]),
         even_w_out[0], odd_w_in[0], odd_w_out[0]),
        gdn_w=gdn_main, tm=512, tn=1024, tokens_per_seq=seq, layer=0, name="in_proj_even_hgrn2")
    wgu_b = wgu_b.reshape(moe_w_gate_up.shape)
    wd_b = wd_b.reshape(moe_w_down.shape)
    proj = proj.reshape(bsz, seq, -1)
    small = small.reshape(bsz, seq, -1)
    o_a = _gdn(proj, small, gdn_conv_w[0], _pad_cols(gdn_a_log[0][None, :], LANES),
               _pad_cols(gdn_dt_bias[0][None, :], LANES), gdn_norm_w[0][None, :], tb=512)
    assert ROUTE_TM == MOE_TM
    rw, rb = _router_params(router_c_w, router_c_b, router_f_w, router_f_b)
    x1, h, gk, krow, cnt = _outproj_route(o_a.reshape(n, mix_w), o_b.reshape(n, mix_w), 0, 0,
                                          w_out, xt, norm_ffn_w[0][None, :], rw, rb,
                                          layer=0, tm=ROUTE_TM, name="out_proj_even")
    x2 = _moe(h, krow, gk, cnt.reshape(-1), wgu_b, wd_b, x1,
              final_norm_w[None, :], layer=0, tm=MOE_TM, final_norm=False, name="moe0")

    cos, sin = _rope_tables(seq)
    o_c = _proj_retention(x2, norm_mix_w[1][None, :], w_odd, cos, sin, ret_norm_w[0][None, :],
                          tm=512, tn=1024, tc=RET_CHUNK, tokens_per_seq=seq, name="in_proj_retention")
    x3, h, gk, krow, cnt = _outproj_route(o_c, o_c, 0, 1, w_out1, x2,
                                          norm_ffn_w[1][None, :], rw, rb, layer=1, tm=ROUTE_TM, name="out_proj_odd")
    out = _moe(h, krow, gk, cnt.reshape(-1), wgu_b, wd_b, x3,
               final_norm_w[None, :], layer=1, tm=MOE_TM, final_norm=True, name="moe1")
    return out.reshape(bsz, seq, d)
```
